```python
import math
import jax, jax.numpy as jnp
from jax import lax
import numpy as np

D_MODEL = 1024
BATCH = 32
SEQ = 2048
DEPTH = 1

SSD_HEADS = 16
SSD_HEAD_DIM = 64
SSD_WIDTH = SSD_HEADS * SSD_HEAD_DIM
SSD_GROUPS = 2
SSD_STATE = 128
CONV_WIDTH = 4
CHUNK = 128
CONV_CH = SSD_WIDTH + 2 * SSD_GROUPS * SSD_STATE
ATT_HEADS = 16
ATT_HEAD_DIM = 64
ATT_WIDTH = ATT_HEADS * ATT_HEAD_DIM
Q_BLOCK = 128
MIX_WIDTH = SSD_WIDTH + ATT_WIDTH
IN_SIZES = (SSD_WIDTH, CONV_CH, SSD_HEADS, ATT_WIDTH, ATT_WIDTH, ATT_WIDTH, ATT_HEADS)
IN_WIDTH = sum(IN_SIZES)
D_FF = 4 * D_MODEL
EPS = 1e-5

kernel_name = "hymba_ssd_fox_sqrelu_block"


def rmsnorm(x, w):
    xf = x.astype(jnp.float32)
    y = xf * lax.rsqrt(jnp.mean(xf * xf, axis=-1, keepdims=True) + EPS)
    return (y * w.astype(jnp.float32)).astype(x.dtype)


def causal_dwconv(u, w, b):
    out = lax.conv_general_dilated(
        u, w[:, None, :].astype(u.dtype), window_strides=(1,),
        padding=[(CONV_WIDTH - 1, 0)], dimension_numbers=("NWC", "WIO", "NWC"),
        feature_group_count=u.shape[-1])
    return out + b


def segsum(a):
    cum = jnp.cumsum(a, axis=-1)
    diff = cum[..., :, None] - cum[..., None, :]
    l = a.shape[-1]
    mask = jnp.tril(jnp.ones((l, l), dtype=bool))
    return jnp.where(mask, diff, -jnp.inf)


def ssd_chunked(xh, a, bmat, cmat):
    bsz, T = xh.shape[:2]
    nc = T // CHUNK
    r = SSD_HEADS // SSD_GROUPS
    f32 = jnp.float32
    x = xh.astype(f32).reshape(bsz, nc, CHUNK, SSD_GROUPS, r, SSD_HEAD_DIM)
    a = a.astype(f32).reshape(bsz, nc, CHUNK, SSD_GROUPS, r).transpose(0, 3, 4, 1, 2)
    B = bmat.astype(f32).reshape(bsz, nc, CHUNK, SSD_GROUPS, SSD_STATE)
    C = cmat.astype(f32).reshape(bsz, nc, CHUNK, SSD_GROUPS, SSD_STATE)
    a_cum = jnp.cumsum(a, axis=-1)
    ldec = jnp.exp(segsum(a))
    cb = jnp.einsum("bclgn,bcsgn->bcgls", C, B)
    y_diag = jnp.einsum("bcgls,bgrcls,bcsgrp->bclgrp", cb, ldec, x)
    decay_states = jnp.exp(a_cum[..., -1:] - a_cum)
    states = jnp.einsum("bclgn,bgrcl,bclgrp->bcgrpn", B, decay_states, x)
    chunk_decay = jnp.exp(a_cum[..., -1])

    def step(h, inp):
        s_c, d_c = inp
        return h * d_c[..., None, None] + s_c, h

    h0 = jnp.zeros((bsz, SSD_GROUPS, r, SSD_HEAD_DIM, SSD_STATE), f32)
    _, prev = lax.scan(step, h0, (jnp.moveaxis(states, 1, 0), jnp.moveaxis(chunk_decay, -1, 0)))
    y_off = jnp.einsum("bclgn,cbgrpn,bgrcl->bclgrp", C, prev, jnp.exp(a_cum))
    return (y_diag + y_off).reshape(bsz, T, SSD_HEADS, SSD_HEAD_DIM)


def forgetting_attention(q, k, v, log_f):
    T = q.shape[2]
    scale = 1.0 / math.sqrt(ATT_HEAD_DIM)
    c = jnp.cumsum(log_f, axis=-1)
    outs = []
    for i in range(T // Q_BLOCK):
        qs, qe = i * Q_BLOCK, (i + 1) * Q_BLOCK
        s = jnp.einsum("bhqd,bhkd->bhqk", q[:, :, qs:qe], k[:, :, :qe]).astype(jnp.float32) * scale
        s = s + (c[:, :, qs:qe, None] - c[:, :, None, :qe])
        mask = (qs + jnp.arange(Q_BLOCK))[:, None] >= jnp.arange(qe)[None, :]
        p = jax.nn.softmax(jnp.where(mask, s, -jnp.inf), axis=-1)
        outs.append(jnp.einsum("bhqk,bhkd->bhqd", p.astype(v.dtype), v[:, :, :qe]))
    return jnp.concatenate(outs, axis=2)


def hybrid_mixer(h, w_in, conv_w, conv_b, dt_bias, a_log, d_skip, ssd_norm_w, f_bias, w_out):
    bsz, T, _ = h.shape
    proj = jnp.einsum("btd,de->bte", h, w_in)
    idx = np.cumsum(IN_SIZES)[:-1].tolist()
    z, xbc, dt_raw, q, k, v, f_raw = jnp.split(proj, idx, axis=-1)
    xbc = jax.nn.silu(causal_dwconv(xbc, conv_w, conv_b))
    xs, bm, cm = jnp.split(xbc, [SSD_WIDTH, SSD_WIDTH + SSD_GROUPS * SSD_STATE], axis=-1)
    xs = xs.reshape(bsz, T, SSD_HEADS, SSD_HEAD_DIM)
    bm = bm.reshape(bsz, T, SSD_GROUPS, SSD_STATE)
    cm = cm.reshape(bsz, T, SSD_GROUPS, SSD_STATE)
    dt = jax.nn.softplus(dt_raw.astype(jnp.float32) + dt_bias.astype(jnp.float32))
    A = -jnp.exp(a_log.astype(jnp.float32))
    y = ssd_chunked(xs.astype(jnp.float32) * dt[..., None], A * dt, bm, cm)
    y = y + d_skip.astype(jnp.float32)[:, None] * xs.astype(jnp.float32)
    y = y.reshape(bsz, T, SSD_WIDTH) * jax.nn.silu(z.astype(jnp.float32))
    yg = y.reshape(bsz, T, SSD_GROUPS, SSD_WIDTH // SSD_GROUPS)
    yg = yg * lax.rsqrt(jnp.mean(yg * yg, axis=-1, keepdims=True) + EPS)
    y_ssd = (yg.reshape(bsz, T, SSD_WIDTH) * ssd_norm_w.astype(jnp.float32)).astype(h.dtype)
    heads = lambda t: t.reshape(bsz, T, ATT_HEADS, ATT_HEAD_DIM).transpose(0, 2, 1, 3)
    log_f = jax.nn.log_sigmoid(f_raw.astype(jnp.float32) + f_bias.astype(jnp.float32)).transpose(0, 2, 1)
    o = forgetting_attention(heads(q), heads(k), heads(v), log_f)
    y_att = o.transpose(0, 2, 1, 3).reshape(bsz, T, ATT_WIDTH).astype(h.dtype)
    return jnp.einsum("bte,ed->btd", jnp.concatenate([y_ssd, y_att], axis=-1), w_out)


def _fwd_setup_inputs(seed: int = 0) -> dict:
    key = jax.random.key(seed)
    ks = jax.random.split(key, 16)
    f32 = jnp.float32
    L = DEPTH
    x = jax.random.normal(ks[0], (BATCH, SEQ, D_MODEL), f32)
    norm_mix_w = 1.0 + 0.01 * jax.random.normal(ks[1], (L, D_MODEL), f32)
    w_in = jax.random.normal(ks[2], (L, D_MODEL, IN_WIDTH), f32) * D_MODEL ** -0.5
    conv_w = jax.random.uniform(ks[3], (L, CONV_WIDTH, CONV_CH), f32, -1.0, 1.0) * CONV_WIDTH ** -0.5
    conv_b = 0.01 * jax.random.normal(ks[4], (L, CONV_CH), f32)
    dt0 = jnp.exp(jax.random.uniform(ks[5], (L, SSD_HEADS), f32, math.log(1e-3), math.log(1e-1)))
    dt_bias = dt0 + jnp.log(-jnp.expm1(-dt0))
    a_log = jnp.log(jax.random.uniform(ks[6], (L, SSD_HEADS), f32, 1.0, 16.0))
    d_skip = 1.0 + 0.01 * jax.random.normal(ks[7], (L, SSD_HEADS), f32)
    ssd_norm_w = 1.0 + 0.01 * jax.random.normal(ks[8], (L, SSD_WIDTH), f32)
    f_bias = jax.random.uniform(ks[9], (L, ATT_HEADS), f32, 1.0, 4.0)
    w_out = jax.random.normal(ks[10], (L, MIX_WIDTH, D_MODEL), f32) * MIX_WIDTH ** -0.5
    norm_mlp_w = 1.0 + 0.01 * jax.random.normal(ks[11], (L, D_MODEL), f32)
    w_up = jax.random.normal(ks[12], (L, D_MODEL, D_FF), f32) * D_MODEL ** -0.5
    w_down = jax.random.normal(ks[13], (L, D_FF, D_MODEL), f32) * D_FF ** -0.5
    norm_final_w = 1.0 + 0.01 * jax.random.normal(ks[14], (D_MODEL,), f32)
    return {"x": x, "norm_mix_w": norm_mix_w, "w_in": w_in, "conv_w": conv_w, "conv_b": conv_b,
            "dt_bias": dt_bias, "a_log": a_log, "d_skip": d_skip, "ssd_norm_w": ssd_norm_w,
            "f_bias": f_bias, "w_out": w_out, "norm_mlp_w": norm_mlp_w, "w_up": w_up,
            "w_down": w_down, "norm_final_w": norm_final_w}


def _fwd_reference(x, norm_mix_w, w_in, conv_w, conv_b, dt_bias, a_log, d_skip, ssd_norm_w,
              f_bias, w_out, norm_mlp_w, w_up, w_down, norm_final_w):
    h = x
    for l in range(DEPTH):
        h = h + hybrid_mixer(rmsnorm(h, norm_mix_w[l]), w_in[l], conv_w[l], conv_b[l], dt_bias[l],
                             a_log[l], d_skip[l], ssd_norm_w[l], f_bias[l], w_out[l])
        u = jnp.square(jax.nn.relu(jnp.einsum("btd,df->btf", rmsnorm(h, norm_mlp_w[l]), w_up[l])))
        h = h + jnp.einsum("btf,fd->btd", u, w_down[l])
    return rmsnorm(h, norm_final_w)


import jax as _jax
import jax.numpy as _jnp

TWIN_FORMAT = 'train_step'
FWD_PARAMS = ['x', 'norm_mix_w', 'w_in', 'conv_w', 'conv_b', 'dt_bias', 'a_log', 'd_skip', 'ssd_norm_w', 'f_bias', 'w_out', 'norm_mlp_w', 'w_up', 'w_down', 'norm_final_w']
TWIN_WEIGHTS = ['norm_mix_w', 'w_in', 'conv_w', 'conv_b', 'dt_bias', 'a_log', 'd_skip', 'ssd_norm_w', 'f_bias', 'w_out', 'norm_mlp_w', 'w_up', 'w_down', 'norm_final_w']
TWIN_DIFF_INPUT = 'x'
TWIN_INPUTS = ['x', 'norm_mix_w', 'w_in', 'conv_w', 'conv_b', 'dt_bias', 'a_log', 'd_skip', 'ssd_norm_w', 'f_bias', 'w_out', 'norm_mlp_w', 'w_up', 'w_down', 'norm_final_w', 'loss_target', 'm_norm_mix_w', 'm_w_in', 'm_conv_w', 'm_conv_b', 'm_dt_bias', 'm_a_log', 'm_d_skip', 'm_ssd_norm_w', 'm_f_bias', 'm_w_out', 'm_norm_mlp_w', 'm_w_up', 'm_w_down', 'm_norm_final_w', 'v_norm_mix_w', 'v_w_in', 'v_conv_w', 'v_conv_b', 'v_dt_bias', 'v_a_log', 'v_d_skip', 'v_ssd_norm_w', 'v_f_bias', 'v_w_out', 'v_norm_mlp_w', 'v_w_up', 'v_w_down', 'v_norm_final_w']
TWIN_OUTPUTS = ['loss', 'grad_x', 'grad_norm_mix_w', 'grad_w_in', 'grad_conv_w', 'grad_conv_b', 'grad_dt_bias', 'grad_a_log', 'grad_d_skip', 'grad_ssd_norm_w', 'grad_f_bias', 'grad_w_out', 'grad_norm_mlp_w', 'grad_w_up', 'grad_w_down', 'grad_norm_final_w', 'delta_norm_mix_w', 'delta_w_in', 'delta_conv_w', 'delta_conv_b', 'delta_dt_bias', 'delta_a_log', 'delta_d_skip', 'delta_ssd_norm_w', 'delta_f_bias', 'delta_w_out', 'delta_norm_mlp_w', 'delta_w_up', 'delta_w_down', 'delta_norm_final_w', 'new_m_norm_mix_w', 'new_m_w_in', 'new_m_conv_w', 'new_m_conv_b', 'new_m_dt_bias', 'new_m_a_log', 'new_m_d_skip', 'new_m_ssd_norm_w', 'new_m_f_bias', 'new_m_w_out', 'new_m_norm_mlp_w', 'new_m_w_up', 'new_m_w_down', 'new_m_norm_final_w', 'new_v_norm_mix_w', 'new_v_w_in', 'new_v_conv_w', 'new_v_conv_b', 'new_v_dt_bias', 'new_v_a_log', 'new_v_d_skip', 'new_v_ssd_norm_w', 'new_v_f_bias', 'new_v_w_out', 'new_v_norm_mlp_w', 'new_v_w_up', 'new_v_w_down', 'new_v_norm_final_w']
TWIN_LEAF_KINDS = {'loss': 'loss', 'grad_x': 'grad_x', 'grad_norm_mix_w': 'grad_w', 'grad_w_in': 'grad_w', 'grad_conv_w': 'grad_w', 'grad_conv_b': 'grad_w', 'grad_dt_bias': 'grad_w', 'grad_a_log': 'grad_w', 'grad_d_skip': 'grad_w', 'grad_ssd_norm_w': 'grad_w', 'grad_f_bias': 'grad_w', 'grad_w_out': 'grad_w', 'grad_norm_mlp_w': 'grad_w', 'grad_w_up': 'grad_w', 'grad_w_down': 'grad_w', 'grad_norm_final_w': 'grad_w', 'delta_norm_mix_w': 'delta_w', 'delta_w_in': 'delta_w', 'delta_conv_w': 'delta_w', 'delta_conv_b': 'delta_w', 'delta_dt_bias': 'delta_w', 'delta_a_log': 'delta_w', 'delta_d_skip': 'delta_w', 'delta_ssd_norm_w': 'delta_w', 'delta_f_bias': 'delta_w', 'delta_w_out': 'delta_w', 'delta_norm_mlp_w': 'delta_w', 'delta_w_up': 'delta_w', 'delta_w_down': 'delta_w', 'delta_norm_final_w': 'delta_w', 'new_m_norm_mix_w': 'new_m', 'new_m_w_in': 'new_m', 'new_m_conv_w': 'new_m', 'new_m_conv_b': 'new_m', 'new_m_dt_bias': 'new_m', 'new_m_a_log': 'new_m', 'new_m_d_skip': 'new_m', 'new_m_ssd_norm_w': 'new_m', 'new_m_f_bias': 'new_m', 'new_m_w_out': 'new_m', 'new_m_norm_mlp_w': 'new_m', 'new_m_w_up': 'new_m', 'new_m_w_down': 'new_m', 'new_m_norm_final_w': 'new_m', 'new_v_norm_mix_w': 'new_v', 'new_v_w_in': 'new_v', 'new_v_conv_w': 'new_v', 'new_v_conv_b': 'new_v', 'new_v_dt_bias': 'new_v', 'new_v_a_log': 'new_v', 'new_v_d_skip': 'new_v', 'new_v_ssd_norm_w': 'new_v', 'new_v_f_bias': 'new_v', 'new_v_w_out': 'new_v', 'new_v_norm_mlp_w': 'new_v', 'new_v_w_up': 'new_v', 'new_v_w_down': 'new_v', 'new_v_norm_final_w': 'new_v'}


def _forward(args):
    return _fwd_reference(*[args[k] for k in FWD_PARAMS])


def _output_shape():
    out = _jax.eval_shape(lambda: _forward(_fwd_setup_inputs(0)))
    return out.shape, out.dtype

N_MICROBATCH = 1
ADAM_LR = 0.001
ADAM_B1 = 0.9
ADAM_B2 = 0.999
ADAM_EPS = 1e-08
ADAM_WD = 0.01
ADAM_STEP = 10
PER_EXAMPLE_BATCH_AXIS = {'x': 0, 'loss_target': 0}
SHARED_INPUTS = []
_WEIGHT_DTYPES = {'norm_mix_w': _jnp.float32, 'w_in': _jnp.float32, 'conv_w': _jnp.float32, 'conv_b': _jnp.float32, 'dt_bias': _jnp.float32, 'a_log': _jnp.float32, 'd_skip': _jnp.float32, 'ssd_norm_w': _jnp.float32, 'f_bias': _jnp.float32, 'w_out': _jnp.float32, 'norm_mlp_w': _jnp.float32, 'w_up': _jnp.float32, 'w_down': _jnp.float32, 'norm_final_w': _jnp.float32}
MOMENT_SCALE = {'norm_mix_w': 2.724085e-01, 'w_in': 1.025434e-01, 'conv_w': 2.294656e-01, 'conv_b': 2.374268e-01, 'dt_bias': 3.459330e-01, 'a_log': 1.380660e-01, 'd_skip': 9.956820e-01, 'ssd_norm_w': 1.572307e-01, 'f_bias': 2.195374e-01, 'w_out': 1.628823e-01, 'norm_mlp_w': 2.057585e-01, 'w_up': 1.017188e-01, 'w_down': 1.799646e-01, 'norm_final_w': 6.440163e+01}


def _to_microbatches(a, axis):
    t = _jnp.moveaxis(a, axis, 0)
    t = t.reshape((N_MICROBATCH, t.shape[0] // N_MICROBATCH) + t.shape[1:])
    return _jnp.moveaxis(t, 1, axis + 1)


def setup_inputs(seed: int = 0) -> dict:
    inp = _fwd_setup_inputs(seed)
    key = _jax.random.fold_in(_jax.random.key(seed), 7919)
    shape, _ = _output_shape()
    out = dict(inp)
    out["loss_target"] = _jax.random.normal(_jax.random.fold_in(key, 0), shape, _jnp.float32)
    for i, name in enumerate(TWIN_WEIGHTS):
        w = inp[name].astype(_jnp.float32)
        if MOMENT_SCALE is None:
            s = _jnp.sqrt(_jnp.mean(_jnp.square(w)) + 1e-30)
        else:
            s = MOMENT_SCALE[name]
        km, kv = _jax.random.split(_jax.random.fold_in(key, i + 1))
        out[name] = w
        out["m_" + name] = s * _jax.random.normal(km, w.shape, _jnp.float32)
        out["v_" + name] = (s * s) * _jax.random.uniform(kv, w.shape, _jnp.float32, 0.5, 1.5)
    if N_MICROBATCH > 1:
        for name, axis in PER_EXAMPLE_BATCH_AXIS.items():
            out[name] = _to_microbatches(out[name], axis)
    return {'x': out['x'], 'norm_mix_w': out['norm_mix_w'], 'w_in': out['w_in'], 'conv_w': out['conv_w'], 'conv_b': out['conv_b'], 'dt_bias': out['dt_bias'], 'a_log': out['a_log'], 'd_skip': out['d_skip'], 'ssd_norm_w': out['ssd_norm_w'], 'f_bias': out['f_bias'], 'w_out': out['w_out'], 'norm_mlp_w': out['norm_mlp_w'], 'w_up': out['w_up'], 'w_down': out['w_down'], 'norm_final_w': out['norm_final_w'], 'loss_target': out['loss_target'], 'm_norm_mix_w': out['m_norm_mix_w'], 'm_w_in': out['m_w_in'], 'm_conv_w': out['m_conv_w'], 'm_conv_b': out['m_conv_b'], 'm_dt_bias': out['m_dt_bias'], 'm_a_log': out['m_a_log'], 'm_d_skip': out['m_d_skip'], 'm_ssd_norm_w': out['m_ssd_norm_w'], 'm_f_bias': out['m_f_bias'], 'm_w_out': out['m_w_out'], 'm_norm_mlp_w': out['m_norm_mlp_w'], 'm_w_up': out['m_w_up'], 'm_w_down': out['m_w_down'], 'm_norm_final_w': out['m_norm_final_w'], 'v_norm_mix_w': out['v_norm_mix_w'], 'v_w_in': out['v_w_in'], 'v_conv_w': out['v_conv_w'], 'v_conv_b': out['v_conv_b'], 'v_dt_bias': out['v_dt_bias'], 'v_a_log': out['v_a_log'], 'v_d_skip': out['v_d_skip'], 'v_ssd_norm_w': out['v_ssd_norm_w'], 'v_f_bias': out['v_f_bias'], 'v_w_out': out['v_w_out'], 'v_norm_mlp_w': out['v_norm_mlp_w'], 'v_w_up': out['v_w_up'], 'v_w_down': out['v_w_down'], 'v_norm_final_w': out['v_norm_final_w']}


def _loss(weights, diff, rest, loss_target):
    with _jax.named_scope("forward"):
        args = {**rest, TWIN_DIFF_INPUT: diff, **{k: w.astype(_WEIGHT_DTYPES[k]) for k, w in weights.items()}}
        y = _forward(args)
    with _jax.named_scope("loss_head"):
        err = _jnp.square(y.astype(_jnp.float32) - loss_target)
        return 0.5 * _jnp.sum(_jnp.mean(err, axis=-1)) if err.ndim else 0.5 * err


def _adamw(w, g, m, v):
    m = ADAM_B1 * m + (1.0 - ADAM_B1) * g
    v = ADAM_B2 * v + (1.0 - ADAM_B2) * _jnp.square(g)
    m_hat = m / (1.0 - ADAM_B1 ** ADAM_STEP)
    v_hat = v / (1.0 - ADAM_B2 ** ADAM_STEP)
    delta = -ADAM_LR * (m_hat / (_jnp.sqrt(v_hat) + ADAM_EPS) + ADAM_WD * w)
    return delta, m, v


def reference(x, norm_mix_w, w_in, conv_w, conv_b, dt_bias, a_log, d_skip, ssd_norm_w, f_bias, w_out, norm_mlp_w, w_up, w_down, norm_final_w, loss_target, m_norm_mix_w, m_w_in, m_conv_w, m_conv_b, m_dt_bias, m_a_log, m_d_skip, m_ssd_norm_w, m_f_bias, m_w_out, m_norm_mlp_w, m_w_up, m_w_down, m_norm_final_w, v_norm_mix_w, v_w_in, v_conv_w, v_conv_b, v_dt_bias, v_a_log, v_d_skip, v_ssd_norm_w, v_f_bias, v_w_out, v_norm_mlp_w, v_w_up, v_w_down, v_norm_final_w):
    given = dict(x=x, norm_mix_w=norm_mix_w, w_in=w_in, conv_w=conv_w, conv_b=conv_b, dt_bias=dt_bias, a_log=a_log, d_skip=d_skip, ssd_norm_w=ssd_norm_w, f_bias=f_bias, w_out=w_out, norm_mlp_w=norm_mlp_w, w_up=w_up, w_down=w_down, norm_final_w=norm_final_w, loss_target=loss_target, m_norm_mix_w=m_norm_mix_w, m_w_in=m_w_in, m_conv_w=m_conv_w, m_conv_b=m_conv_b, m_dt_bias=m_dt_bias, m_a_log=m_a_log, m_d_skip=m_d_skip, m_ssd_norm_w=m_ssd_norm_w, m_f_bias=m_f_bias, m_w_out=m_w_out, m_norm_mlp_w=m_norm_mlp_w, m_w_up=m_w_up, m_w_down=m_w_down, m_norm_final_w=m_norm_final_w, v_norm_mix_w=v_norm_mix_w, v_w_in=v_w_in, v_conv_w=v_conv_w, v_conv_b=v_conv_b, v_dt_bias=v_dt_bias, v_a_log=v_a_log, v_d_skip=v_d_skip, v_ssd_norm_w=v_ssd_norm_w, v_f_bias=v_f_bias, v_w_out=v_w_out, v_norm_mlp_w=v_norm_mlp_w, v_w_up=v_w_up, v_w_down=v_w_down, v_norm_final_w=v_norm_final_w)
    weights = {n: given[n] for n in TWIN_WEIGHTS}
    shared = {n: given[n] for n in SHARED_INPUTS}
    per_example = {n: given[n] for n in ['x']}
    grad_fn = _jax.value_and_grad(_loss, argnums=(0, 1))

    def one_microbatch(ex, loss_target):
        ex = dict(ex)
        diff = ex.pop(TWIN_DIFF_INPUT)
        return grad_fn(weights, diff, {**shared, **ex}, loss_target)

    if N_MICROBATCH == 1:
        loss, (grad_w, grad_x) = one_microbatch(per_example, given["loss_target"])
    else:
        def body(carry, xs):
            loss_sum, grad_sum = carry
            l_k, (gw_k, gx_k) = one_microbatch(xs[0], xs[1])
            with _jax.named_scope("update"):
                return (loss_sum + l_k, _jax.tree.map(_jnp.add, grad_sum, gw_k)), gx_k

        init = (_jnp.zeros((), _jnp.float32), _jax.tree.map(_jnp.zeros_like, weights))
        (loss, grad_w), grad_x = _jax.lax.scan(body, init, (per_example, given["loss_target"]))
    with _jax.named_scope("update"):
        delta_w, new_m, new_v = {}, {}, {}
        for n in TWIN_WEIGHTS:
            delta_w[n], new_m[n], new_v[n] = _adamw(weights[n], grad_w[n], given["m_" + n], given["v_" + n])
    return (loss, grad_x, *[grad_w[n] for n in TWIN_WEIGHTS], *[delta_w[n] for n in TWIN_WEIGHTS],
            *[new_m[n] for n in TWIN_WEIGHTS], *[new_v[n] for n in TWIN_WEIGHTS])
```

```python
import functools

import jax
import jax.numpy as jnp
from jax import lax
from jax.experimental import pallas as pl
from jax.experimental.pallas import tpu as pltpu

F32, BF16 = jnp.float32, jnp.bfloat16
HI = lax.Precision.HIGHEST

D_MODEL = 1024
SSD_HEADS = 16
HEAD_DIM = 64
SSD_STATE = 128
CHUNK = 128
CONV_CH = 1536
CONV_K = 4
D_FF = 4096
MIX_WIDTH = 2048
IN_WIDTH = 5664
NORM_EPS = 1e-5
ATT_SCALE = 0.125

LANES = 128
SUBLANES = 8
BF16_ROWS = 16
HEAD_PAIRS = SSD_HEADS // 2
W1_COLS = 5760
DTF_COL = 5632
NEG = -1e30
VMEM_LIMIT = 52 * 1024 * 1024

N_DEV = 8
MESH_AXES = ("x", "y", "c")
MESH = pl.DeviceIdType.MESH

ADAM_LR, ADAM_B1, ADAM_B2, ADAM_EPS, ADAM_WD, ADAM_STEP = 0.001, 0.9, 0.999, 1e-08, 0.01, 10
ADAM_C1 = 1.0 - ADAM_B1 ** ADAM_STEP
ADAM_C2 = 1.0 - ADAM_B2 ** ADAM_STEP

ROWS_IN, ROWS_OUT, ROWS_UP, ROWS_DOWN = 5664, 2048, 4096, 4096
ROWS_BIG = ROWS_IN + ROWS_OUT + ROWS_UP + ROWS_DOWN
SMALL_COLS = CONV_CH


def _params(sem=None):
    return pltpu.CompilerParams(dimension_semantics=sem, vmem_limit_bytes=VMEM_LIMIT)


def _sigmoid(u):
    return 1.0 / (1.0 + jnp.exp(-u))


def _softplus(u):
    return jnp.maximum(u, 0.0) + jnp.log(1.0 + jnp.exp(-jnp.abs(u)))


def _log_sigmoid(u):
    return jnp.minimum(u, 0.0) - jnp.log(1.0 + jnp.exp(-jnp.abs(u)))


def _bdot(a, b):
    return jnp.dot(a.astype(BF16), b.astype(BF16), preferred_element_type=F32)


def _bdot_nt(a, b):
    return lax.dot_general(a.astype(BF16), b.astype(BF16), (((1,), (1,)), ((), ())), preferred_element_type=F32)


def _bdot_tn(a, b):
    return lax.dot_general(a.astype(BF16), b.astype(BF16), (((0,), (0,)), ((), ())), preferred_element_type=F32)


def _hdot(a, b):
    return jnp.dot(a, b, precision=HI, preferred_element_type=F32)


def _iota(shape, dim):
    return lax.broadcasted_iota(jnp.int32, shape, dim)


def _head_expand():
    return (jnp.right_shift(_iota((LANES, D_MODEL), 1), 6) == _iota((LANES, D_MODEL), 0)).astype(F32)


def _head_reduce():
    return (jnp.right_shift(_iota((D_MODEL, LANES), 0), 6) == _iota((D_MODEL, LANES), 1)).astype(F32)


def _rms_fwd(x, w, name):
    n, d = x.shape
    tm = min(512, n)

    def body(x_ref, w_ref, o_ref):
        xv = x_ref[...]
        r = lax.rsqrt(jnp.mean(xv * xv, axis=-1, keepdims=True) + NORM_EPS)
        o_ref[...] = (xv * r * w_ref[...]).astype(BF16)

    return pl.pallas_call(
        body, name=name, grid=(n // tm,),
        in_specs=[pl.BlockSpec((tm, d), lambda i: (i, 0)), pl.BlockSpec((1, d), lambda i: (0, 0))],
        out_specs=pl.BlockSpec((tm, d), lambda i: (i, 0)),
        out_shape=jax.ShapeDtypeStruct((n, d), BF16),
        compiler_params=_params(("parallel",)),
    )(x, w)


def _rms_bwd(dn, x, w, dres, name):
    n, d = x.shape
    tm = min(256, n)

    def body(dn_ref, x_ref, w_ref, dres_ref, dx_ref, gw_ref):
        xv = x_ref[...]
        r = lax.rsqrt(jnp.mean(xv * xv, axis=-1, keepdims=True) + NORM_EPS)
        nrm = xv * r
        dnv = dn_ref[...]
        g = dnv * w_ref[...]
        dx_ref[...] = dres_ref[...] + r * (g - nrm * jnp.mean(g * nrm, axis=-1, keepdims=True))

        @pl.when(pl.program_id(0) == 0)
        def _():
            gw_ref[...] = jnp.zeros_like(gw_ref)

        gw_ref[...] += jnp.sum(dnv * nrm, axis=0, keepdims=True)

    tok = pl.BlockSpec((tm, d), lambda i: (i, 0))
    row = pl.BlockSpec((1, d), lambda i: (0, 0))
    return pl.pallas_call(
        body, name=name, grid=(n // tm,),
        in_specs=[tok, tok, row, tok], out_specs=[tok, row],
        out_shape=[jax.ShapeDtypeStruct((n, d), F32), jax.ShapeDtypeStruct((1, d), F32)],
        compiler_params=_params(("arbitrary",)),
    )(dn, x, w, dres)


def _mm_nn(a, b, name, out_dtype, tm, tn, res=None):
    m = a.shape[0]
    k, n = b.shape
    tm, tn = min(tm, m), min(tn, n)

    def body(*refs):
        if res is None:
            a_ref, b_ref, o_ref = refs
            acc = _bdot(a_ref[...], b_ref[...])
        else:
            a_ref, b_ref, r_ref, o_ref = refs
            acc = r_ref[...] + _bdot(a_ref[...], b_ref[...])
        o_ref[...] = acc.astype(o_ref.dtype)

    in_specs = [pl.BlockSpec((tm, k), lambda i, j: (i, 0)), pl.BlockSpec((k, tn), lambda i, j: (0, j))]
    args = [a, b]
    if res is not None:
        in_specs.append(pl.BlockSpec((tm, tn), lambda i, j: (i, j)))
        args.append(res)
    return pl.pallas_call(
        body, name=name, grid=(m // tm, n // tn), in_specs=in_specs,
        out_specs=pl.BlockSpec((tm, tn), lambda i, j: (i, j)),
        out_shape=jax.ShapeDtypeStruct((m, n), out_dtype),
        compiler_params=_params(("parallel", "parallel")),
    )(*args)


def _mm_tn(a, b, name, tm, tn, tk):
    t, m = a.shape
    n = b.shape[1]
    tm, tn, tk = min(tm, m), min(tn, n), min(tk, t)
    nk = t // tk

    def body(a_ref, b_ref, o_ref, acc_ref):
        kk = pl.program_id(2)

        @pl.when(kk == 0)
        def _():
            acc_ref[...] = jnp.zeros_like(acc_ref)

        acc_ref[...] += _bdot_tn(a_ref[...], b_ref[...])

        @pl.when(kk == nk - 1)
        def _():
            o_ref[...] = acc_ref[...].astype(o_ref.dtype)

    return pl.pallas_call(
        body, name=name, grid=(m // tm, n // tn, nk),
        in_specs=[pl.BlockSpec((tk, tm), lambda i, j, kk: (kk, i)), pl.BlockSpec((tk, tn), lambda i, j, kk: (kk, j))],
        out_specs=pl.BlockSpec((tm, tn), lambda i, j, kk: (i, j)),
        out_shape=jax.ShapeDtypeStruct((m, n), BF16),
        scratch_shapes=[pltpu.VMEM((tm, tn), F32)],
        compiler_params=_params(("parallel", "parallel", "arbitrary")),
    )(a, b)


def _mlp_down_fwd(pre, w_down, h2):
    m, k = pre.shape
    n = w_down.shape[1]
    tm, tn = min(256, m), min(512, n)

    def body(p_ref, b_ref, r_ref, o_ref):
        u = jnp.square(jnp.maximum(p_ref[...], 0.0))
        o_ref[...] = r_ref[...] + _bdot(u, b_ref[...])

    return pl.pallas_call(
        body, name="mlp_down_fwd", grid=(m // tm, n // tn),
        in_specs=[pl.BlockSpec((tm, k), lambda i, j: (i, 0)), pl.BlockSpec((k, tn), lambda i, j: (0, j)),
                  pl.BlockSpec((tm, tn), lambda i, j: (i, j))],
        out_specs=pl.BlockSpec((tm, tn), lambda i, j: (i, j)),
        out_shape=jax.ShapeDtypeStruct((m, n), F32),
        compiler_params=_params(("parallel", "parallel")),
    )(pre, w_down, h2)


def _mlp_down_bwd(dh3, w_down_t, pre):
    m, k = dh3.shape
    n = w_down_t.shape[1]
    tm, tn = min(512, m), min(1024, n)

    def body(a_ref, b_ref, p_ref, dpre_ref, u_ref):
        r = jnp.maximum(p_ref[...], 0.0)
        du = _bdot(a_ref[...], b_ref[...])
        dpre_ref[...] = (du * (2.0 * r)).astype(BF16)
        u_ref[...] = (r * r).astype(BF16)

    blk = pl.BlockSpec((tm, tn), lambda i, j: (i, j))
    return pl.pallas_call(
        body, name="mlp_down_bwd", grid=(m // tm, n // tn),
        in_specs=[pl.BlockSpec((tm, k), lambda i, j: (i, 0)), pl.BlockSpec((k, tn), lambda i, j: (0, j)), blk],
        out_specs=[blk, blk],
        out_shape=[jax.ShapeDtypeStruct((m, n), BF16), jax.ShapeDtypeStruct((m, n), BF16)],
        compiler_params=_params(("parallel", "parallel")),
    )(dh3, w_down_t, pre)


def _loss_bwd(h3, target, w):
    n, d = h3.shape
    tm = min(256, n)

    def body(h_ref, t_ref, w_ref, dh_ref, loss_ref, gw_ref):
        xv = h_ref[...]
        r = lax.rsqrt(jnp.mean(xv * xv, axis=-1, keepdims=True) + NORM_EPS)
        nrm = xv * r
        wv = w_ref[...]
        err = nrm * wv - t_ref[...]
        part = 0.5 * jnp.sum(jnp.mean(err * err, axis=-1, keepdims=True), axis=0, keepdims=True)
        dy = err * (1.0 / d)
        g = dy * wv
        dh_ref[...] = r * (g - nrm * jnp.mean(g * nrm, axis=-1, keepdims=True))

        @pl.when(pl.program_id(0) == 0)
        def _():
            loss_ref[...] = jnp.zeros_like(loss_ref)
            gw_ref[...] = jnp.zeros_like(gw_ref)

        loss_ref[...] += jnp.broadcast_to(part, loss_ref.shape)
        gw_ref[...] += jnp.sum(dy * nrm, axis=0, keepdims=True)

    tok = pl.BlockSpec((tm, d), lambda i: (i, 0))
    row = pl.BlockSpec((1, d), lambda i: (0, 0))
    return pl.pallas_call(
        body, name="loss_bwd", grid=(n // tm,),
        in_specs=[tok, tok, row], out_specs=[tok, pl.BlockSpec((1, LANES), lambda i: (0, 0)), row],
        out_shape=[jax.ShapeDtypeStruct((n, d), F32), jax.ShapeDtypeStruct((1, LANES), F32),
                   jax.ShapeDtypeStruct((1, d), F32)],
        compiler_params=_params(("arbitrary",)),
    )(h3, target, w)


CONV_TC = 512


def _conv_fwd(xbc, cw, cb, bsz, seq):
    tt = min(256, seq)
    nt, hb = seq // tt, tt // SUBLANES
    x3 = xbc.reshape(bsz, seq, CONV_CH)

    def body(xm_ref, xh_ref, w_ref, b_ref, o_ref):
        i = pl.program_id(2)
        x = xm_ref[0]
        halo = jnp.where(i > 0, xh_ref[0], 0.0)
        xx = jnp.concatenate([halo, x], axis=0)
        acc = x * w_ref[3:4, :] + b_ref[...]
        for s in range(1, CONV_K):
            acc = acc + pltpu.roll(xx, s, 0)[SUBLANES:, :] * w_ref[3 - s:4 - s, :]
        o_ref[0] = acc * _sigmoid(acc)

    out = pl.pallas_call(
        body, name="conv_fwd", grid=(CONV_CH // CONV_TC, bsz, nt),
        in_specs=[pl.BlockSpec((1, tt, CONV_TC), lambda c, b, i: (b, i, c)),
                  pl.BlockSpec((1, SUBLANES, CONV_TC), lambda c, b, i: (b, jnp.maximum(i * hb - 1, 0), c)),
                  pl.BlockSpec((CONV_K, CONV_TC), lambda c, b, i: (0, c)),
                  pl.BlockSpec((1, CONV_TC), lambda c, b, i: (0, c))],
        out_specs=pl.BlockSpec((1, tt, CONV_TC), lambda c, b, i: (b, i, c)),
        out_shape=jax.ShapeDtypeStruct((bsz, seq, CONV_CH), F32),
        compiler_params=_params(("parallel", "parallel", "parallel")),
    )(x3, x3, cw, cb)
    return out.reshape(bsz * seq, CONV_CH)


def _conv_bwd(da, xbc, cw, cb, bsz, seq):
    tt = min(256, seq)
    nt, hb = seq // tt, tt // SUBLANES
    x3 = xbc.reshape(bsz, seq, CONV_CH)
    da3 = da.reshape(bsz, seq, CONV_CH)
    n2 = tt + SUBLANES

    def body(dam_ref, daa_ref, xm_ref, xb_ref, xa_ref, w_ref, b_ref, du_ref, gw_ref, gb_ref):
        bi, i = pl.program_id(1), pl.program_id(2)
        before = jnp.where(i > 0, xb_ref[0], 0.0)
        after = jnp.where(i < nt - 1, xa_ref[0], 0.0)
        xx = jnp.concatenate([before, xm_ref[0], after], axis=0)
        rolled = [xx] + [pltpu.roll(xx, s, 0) for s in range(1, CONV_K)]
        pre = b_ref[...]
        for s in range(CONV_K):
            pre = pre + rolled[s][SUBLANES:, :] * w_ref[3 - s:4 - s, :]
        da_ext = jnp.concatenate([dam_ref[0], jnp.where(i < nt - 1, daa_ref[0], 0.0)], axis=0)
        sg = _sigmoid(pre)
        dpre = da_ext * sg * (1.0 + pre * (1.0 - sg))
        du = dpre[:tt, :] * w_ref[3:4, :]
        for s in range(1, CONV_K):
            du = du + pltpu.roll(dpre, n2 - s, 0)[:tt, :] * w_ref[3 - s:4 - s, :]
        du_ref[0] = du.astype(BF16)
        dpm = dpre[:tt, :]
        gws = [jnp.sum(dpm * rolled[3 - kk][SUBLANES:SUBLANES + tt, :], axis=0, keepdims=True) for kk in range(CONV_K)]

        @pl.when((bi == 0) & (i == 0))
        def _():
            gw_ref[...] = jnp.zeros_like(gw_ref)
            gb_ref[...] = jnp.zeros_like(gb_ref)

        gw_ref[...] += jnp.concatenate(gws, axis=0)
        gb_ref[...] += jnp.sum(dpm, axis=0, keepdims=True)

    main = pl.BlockSpec((1, tt, CONV_TC), lambda c, b, i: (b, i, c))
    prev = pl.BlockSpec((1, SUBLANES, CONV_TC), lambda c, b, i: (b, jnp.maximum(i * hb - 1, 0), c))
    nxt = pl.BlockSpec((1, SUBLANES, CONV_TC), lambda c, b, i: (b, jnp.minimum((i + 1) * hb, seq // SUBLANES - 1), c))
    du, gw, gb = pl.pallas_call(
        body, name="conv_bwd", grid=(CONV_CH // CONV_TC, bsz, nt),
        in_specs=[main, nxt, main, prev, nxt,
                  pl.BlockSpec((CONV_K, CONV_TC), lambda c, b, i: (0, c)),
                  pl.BlockSpec((1, CONV_TC), lambda c, b, i: (0, c))],
        out_specs=[main, pl.BlockSpec((CONV_K, CONV_TC), lambda c, b, i: (0, c)),
                   pl.BlockSpec((1, CONV_TC), lambda c, b, i: (0, c))],
        out_shape=[jax.ShapeDtypeStruct((bsz, seq, CONV_CH), BF16), jax.ShapeDtypeStruct((CONV_K, CONV_CH), F32),
                   jax.ShapeDtypeStruct((1, CONV_CH), F32)],
        compiler_params=_params(("parallel", "arbitrary", "arbitrary")),
    )(da3, da3, x3, x3, x3, cw, cb)
    return du.reshape(bsz * seq, CONV_CH), gw, gb


def _ssd_prologue(dtf_ref, dtft_ref, bias_ref, biast_ref, arow_ref, acol_ref, dtfull_scr, acfull_scr, acr_scr):
    tri = (_iota((CHUNK, CHUNK), 1) <= _iota((CHUNK, CHUNK), 0)).astype(F32)
    triu = (_iota((CHUNK, CHUNK), 0) <= _iota((CHUNK, CHUNK), 1)).astype(F32)
    dtc = _softplus(dtf_ref[0] + bias_ref[...])
    a = dtc * arow_ref[...]
    acum = _hdot(tri, a)
    expand = _head_expand()
    dtfull_scr[...] = _hdot(dtc, expand)
    acfull_scr[...] = _hdot(acum, expand)
    dtr = _softplus(dtft_ref[0] + biast_ref[...])
    acr_scr[...] = _hdot(dtr * acol_ref[...], triu)
    return dtc, acum


def _decay_matrix(acum, acr_scr, h):
    lane = _iota((CHUNK, LANES), 1)
    ac_col = jnp.sum(jnp.where(lane == h, acum, 0.0), axis=1, keepdims=True)
    ac_row = acr_scr[h:h + 1, :]
    causal = _iota((CHUNK, CHUNK), 1) <= _iota((CHUNK, CHUNK), 0)
    return jnp.exp(jnp.where(causal, ac_col - ac_row, NEG))


def _ssd_fwd(xbc, dtf, dtft, bias, biast, arow, acol, dfull, bsz, seq):
    nc = seq // CHUNK
    x3 = xbc.reshape(bsz, seq, CONV_CH)

    def body(xbc_ref, dtf_ref, dtft_ref, bias_ref, biast_ref, arow_ref, acol_ref, dfull_ref,
             y_ref, hall_ref, h_scr, dtfull_scr, acfull_scr, acr_scr):
        @pl.when(pl.program_id(1) == 0)
        def _():
            h_scr[...] = jnp.zeros_like(h_scr)

        _, acum = _ssd_prologue(dtf_ref, dtft_ref, bias_ref, biast_ref, arow_ref, acol_ref,
                                dtfull_scr, acfull_scr, acr_scr)
        lane = _iota((CHUNK, LANES), 1)
        for g in range(2):
            bg = xbc_ref[0, :, D_MODEL + LANES * g:D_MODEL + LANES * (g + 1)]
            cg = xbc_ref[0, :, D_MODEL + 2 * LANES + LANES * g:D_MODEL + 2 * LANES + LANES * (g + 1)]
            cg_bf = cg.astype(BF16)
            gmat = _bdot_nt(cg_bf, bg)
            bgt_bf = bg.T.astype(BF16)
            for j in range(4):
                p = 4 * g + j
                sl = slice(LANES * p, LANES * (p + 1))
                xs = xbc_ref[0, :, sl]
                ac = acfull_scr[:, sl]
                acl = acfull_scr[CHUNK - 1:CHUNK, sl]
                xdt = xs * dtfull_scr[:, sl]
                xdt_bf = xdt.astype(BF16)
                hp = h_scr[p]
                hall_ref[0, 0, p] = hp
                yo = _bdot(cg_bf, hp) * jnp.exp(ac)
                yd = []
                for hh in range(2):
                    mmat = gmat * _decay_matrix(acum, acr_scr, 2 * p + hh)
                    yd.append(_bdot(mmat, xdt_bf))
                y_ref[0, :, sl] = jnp.where(lane < HEAD_DIM, yd[0], yd[1]) + yo + dfull_ref[:, sl] * xs
                h_scr[p] = hp * jnp.exp(acl) + _bdot(bgt_bf, xdt * jnp.exp(acl - ac))

    row = lambda w: pl.BlockSpec((1, w), lambda b, c: (0, 0))
    y, hall = pl.pallas_call(
        body, name="ssd_fwd", grid=(bsz, nc),
        in_specs=[pl.BlockSpec((1, CHUNK, CONV_CH), lambda b, c: (b, c, 0)),
                  pl.BlockSpec((1, CHUNK, LANES), lambda b, c: (b, c, 0)),
                  pl.BlockSpec((1, LANES, CHUNK), lambda b, c: (b, 0, c)),
                  row(LANES), pl.BlockSpec((LANES, 1), lambda b, c: (0, 0)),
                  row(LANES), pl.BlockSpec((LANES, 1), lambda b, c: (0, 0)), row(D_MODEL)],
        out_specs=[pl.BlockSpec((1, CHUNK, D_MODEL), lambda b, c: (b, c, 0)),
                   pl.BlockSpec((1, 1, HEAD_PAIRS, SSD_STATE, LANES), lambda b, c: (b, c, 0, 0, 0))],
        out_shape=[jax.ShapeDtypeStruct((bsz, seq, D_MODEL), F32),
                   jax.ShapeDtypeStruct((bsz, nc, HEAD_PAIRS, SSD_STATE, LANES), F32)],
        scratch_shapes=[pltpu.VMEM((HEAD_PAIRS, SSD_STATE, LANES), F32), pltpu.VMEM((CHUNK, D_MODEL), F32),
                        pltpu.VMEM((CHUNK, D_MODEL), F32), pltpu.VMEM((LANES, CHUNK), F32)],
        compiler_params=_params(("parallel", "arbitrary")),
    )(x3, dtf.reshape(bsz, seq, LANES), dtft, bias, biast, arow, acol, dfull)
    return y.reshape(bsz * seq, D_MODEL), hall


def _ssd_bwd(dy, xbc, dtf, dtft, bias, biast, arow, acol, dfull, hall, bsz, seq):
    nc = seq // CHUNK
    x3 = xbc.reshape(bsz, seq, CONV_CH)
    dy3 = dy.reshape(bsz, seq, D_MODEL)

    def body(dy_ref, xbc_ref, dtf_ref, dtft_ref, bias_ref, biast_ref, arow_ref, acol_ref, dfull_ref, hall_ref,
             dx_ref, ddt_ref, acc_ref, dh_scr, dtfull_scr, acfull_scr, acr_scr, csum_scr, dacf_scr, ddtf_scr, ddl_scr):
        first = (pl.program_id(0) == 0) & (pl.program_id(1) == 0)

        @pl.when(first)
        def _():
            acc_ref[...] = jnp.zeros_like(acc_ref)

        @pl.when(pl.program_id(1) == 0)
        def _():
            dh_scr[...] = jnp.zeros_like(dh_scr)

        dtc, acum = _ssd_prologue(dtf_ref, dtft_ref, bias_ref, biast_ref, arow_ref, acol_ref,
                                  dtfull_scr, acfull_scr, acr_scr)
        csum_scr[...] = jnp.zeros_like(csum_scr)
        lane = _iota((CHUNK, LANES), 1)
        last_row = _iota((CHUNK, LANES), 0) == CHUNK - 1
        rs16 = jnp.zeros((CHUNK, LANES), F32)
        for g in range(2):
            bsl = slice(D_MODEL + LANES * g, D_MODEL + LANES * (g + 1))
            csl = slice(D_MODEL + 2 * LANES + LANES * g, D_MODEL + 2 * LANES + LANES * (g + 1))
            bg_bf = xbc_ref[0, :, bsl].astype(BF16)
            cg = xbc_ref[0, :, csl]
            cg_bf = cg.astype(BF16)
            cgt_bf = cg.T.astype(BF16)
            gmat = _bdot_nt(cg_bf, bg_bf)
            dg = jnp.zeros((CHUNK, CHUNK), F32)
            dbg = jnp.zeros((CHUNK, SSD_STATE), F32)
            dcg = jnp.zeros((CHUNK, SSD_STATE), F32)
            for j in range(4):
                p = 4 * g + j
                sl = slice(LANES * p, LANES * (p + 1))
                xs = xbc_ref[0, :, sl]
                dt = dtfull_scr[:, sl]
                ac = acfull_scr[:, sl]
                acl = acfull_scr[CHUNK - 1:CHUNK, sl]
                dyp = dy_ref[0, :, sl]
                xdt = xs * dt
                xdt_bf = xdt.astype(BF16)
                e = jnp.exp(ac)
                dsd = jnp.exp(acl - ac)
                cd = jnp.exp(acl)
                xds_bf = (xdt * dsd).astype(BF16)
                hp = hall_ref[0, 0, p]
                hp_bf = hp.astype(BF16)
                dhn = dh_scr[p]
                dhn_bf = dhn.astype(BF16)
                yo = _bdot(cg_bf, hp_bf) * e
                dw_bf = (dyp * e).astype(BF16)
                dcg = dcg + _bdot_nt(dw_bf, hp_bf)
                dhin = _bdot(cgt_bf, dw_bf)
                dac = dyp * yo
                dacl = jnp.sum(dhn * hp, axis=0, keepdims=True) * cd
                dxds = _bdot(bg_bf, dhn_bf)
                dbg = dbg + _bdot_nt(xds_bf, dhn_bf)
                dxdt = dxds * dsd
                tdec = dxds * xdt * dsd
                dacl = dacl + jnp.sum(tdec, axis=0, keepdims=True)
                dac = dac - tdec
                dh_scr[p] = dhn * cd + dhin
                for hh in range(2):
                    h = 2 * p + hh
                    lm = (lane < HEAD_DIM) if hh == 0 else (lane >= HEAD_DIM)
                    dec = _decay_matrix(acum, acr_scr, h)
                    mmat = gmat * dec
                    dyh_bf = jnp.where(lm, dyp, 0.0).astype(BF16)
                    dm = _bdot_nt(dyh_bf, xdt_bf)
                    dxdt = dxdt + _bdot(mmat.T, dyh_bf)
                    dg = dg + dm * dec
                    q = dm * mmat
                    rs16 = rs16 + jnp.where(lane == h, jnp.sum(q, axis=1, keepdims=True), 0.0)
                    csum_scr[h:h + 1, :] = jnp.sum(q, axis=0, keepdims=True)
                dx_ref[0, :, sl] = dfull_ref[:, sl] * dyp + dxdt * dt
                dacf_scr[:, sl] = dac + jnp.where(last_row, dacl, 0.0)
                ddtf_scr[:, sl] = dxdt * xs
                ddl_scr[:, sl] = jnp.broadcast_to(jnp.sum(dyp * xs, axis=0, keepdims=True), (SUBLANES, LANES))
            dg_bf = dg.astype(BF16)
            dx_ref[0, :, bsl] = dbg + _bdot(dg.T, cg_bf)
            dx_ref[0, :, csl] = dcg + _bdot(dg_bf, bg_bf)
        reduce = _head_reduce()
        triu = (_iota((CHUNK, CHUNK), 0) <= _iota((CHUNK, CHUNK), 1)).astype(F32)
        dac16 = _hdot(dacf_scr[...], reduce) + rs16 - csum_scr[...].T
        da16 = _hdot(triu, dac16)
        ddt16 = da16 * arow_ref[...] + _hdot(ddtf_scr[...], reduce)
        ddtraw = ddt16 * _sigmoid(dtf_ref[0] + bias_ref[...])
        ddt_ref[0] = ddtraw
        acc_ref[0:8, :] += jnp.broadcast_to(jnp.sum(da16 * dtc * arow_ref[...], axis=0, keepdims=True), (SUBLANES, LANES))
        acc_ref[8:16, :] += _hdot(ddl_scr[...], reduce)
        acc_ref[16:24, :] += jnp.broadcast_to(jnp.sum(ddtraw, axis=0, keepdims=True), (SUBLANES, LANES))

    rev = lambda b, c: (b, nc - 1 - c, 0)
    row = lambda w: pl.BlockSpec((1, w), lambda b, c: (0, 0))
    dx, ddt, acc = pl.pallas_call(
        body, name="ssd_bwd", grid=(bsz, nc),
        in_specs=[pl.BlockSpec((1, CHUNK, D_MODEL), rev), pl.BlockSpec((1, CHUNK, CONV_CH), rev),
                  pl.BlockSpec((1, CHUNK, LANES), rev),
                  pl.BlockSpec((1, LANES, CHUNK), lambda b, c: (b, 0, nc - 1 - c)),
                  row(LANES), pl.BlockSpec((LANES, 1), lambda b, c: (0, 0)),
                  row(LANES), pl.BlockSpec((LANES, 1), lambda b, c: (0, 0)), row(D_MODEL),
                  pl.BlockSpec((1, 1, HEAD_PAIRS, SSD_STATE, LANES), lambda b, c: (b, nc - 1 - c, 0, 0, 0))],
        out_specs=[pl.BlockSpec((1, CHUNK, CONV_CH), rev), pl.BlockSpec((1, CHUNK, LANES), rev),
                   pl.BlockSpec((3 * SUBLANES, LANES), lambda b, c: (0, 0))],
        out_shape=[jax.ShapeDtypeStruct((bsz, seq, CONV_CH), F32), jax.ShapeDtypeStruct((bsz, seq, LANES), F32),
                   jax.ShapeDtypeStruct((3 * SUBLANES, LANES), F32)],
        scratch_shapes=[pltpu.VMEM((HEAD_PAIRS, SSD_STATE, LANES), F32), pltpu.VMEM((CHUNK, D_MODEL), F32),
                        pltpu.VMEM((CHUNK, D_MODEL), F32), pltpu.VMEM((LANES, CHUNK), F32),
                        pltpu.VMEM((LANES, CHUNK), F32), pltpu.VMEM((CHUNK, D_MODEL), F32),
                        pltpu.VMEM((CHUNK, D_MODEL), F32), pltpu.VMEM((SUBLANES, D_MODEL), F32)],
        compiler_params=_params(("arbitrary", "arbitrary")),
    )(dy3, x3, dtf.reshape(bsz, seq, LANES), dtft, bias, biast, arow, acol, dfull, hall)
    return dx.reshape(bsz * seq, CONV_CH), ddt.reshape(bsz * seq, LANES), acc


GROUP_W = D_MODEL // 2


def _gnorm_fwd(y, z, o_att, w):
    n = y.shape[0]
    tm = min(512, n)

    def body(y_ref, z_ref, o_ref, w_ref, out_ref):
        zv = z_ref[...]
        g = y_ref[...] * (zv * _sigmoid(zv))
        for gi in range(2):
            sl = slice(GROUP_W * gi, GROUP_W * (gi + 1))
            gs = g[:, sl]
            r = lax.rsqrt(jnp.mean(gs * gs, axis=-1, keepdims=True) + NORM_EPS)
            out_ref[:, sl] = (gs * r * w_ref[:, sl]).astype(BF16)
        out_ref[:, D_MODEL:] = o_ref[...].astype(BF16)

    tok = pl.BlockSpec((tm, D_MODEL), lambda i: (i, 0))
    return pl.pallas_call(
        body, name="gnorm_fwd", grid=(n // tm,),
        in_specs=[tok, tok, tok, pl.BlockSpec((1, D_MODEL), lambda i: (0, 0))],
        out_specs=pl.BlockSpec((tm, MIX_WIDTH), lambda i: (i, 0)),
        out_shape=jax.ShapeDtypeStruct((n, MIX_WIDTH), BF16),
        compiler_params=_params(("parallel",)),
    )(y, z, o_att, w)


def _gnorm_bwd(dycat, y, z, w):
    n = y.shape[0]
    tm = min(256, n)

    def body(dn_ref, y_ref, z_ref, w_ref, dy_ref, dz_ref, gw_ref):
        zv, yv = z_ref[...], y_ref[...]
        sz = _sigmoid(zv)
        sil = zv * sz
        g = yv * sil

        @pl.when(pl.program_id(0) == 0)
        def _():
            gw_ref[...] = jnp.zeros_like(gw_ref)

        for gi in range(2):
            sl = slice(GROUP_W * gi, GROUP_W * (gi + 1))
            gs = g[:, sl]
            r = lax.rsqrt(jnp.mean(gs * gs, axis=-1, keepdims=True) + NORM_EPS)
            nrm = gs * r
            dyn = dn_ref[:, sl]
            dn = dyn * w_ref[:, sl]
            dgs = r * (dn - nrm * jnp.mean(dn * nrm, axis=-1, keepdims=True))
            dy_ref[:, sl] = dgs * sil[:, sl]
            dz_ref[:, sl] = (dgs * yv[:, sl] * (sz[:, sl] * (1.0 + zv[:, sl] * (1.0 - sz[:, sl])))).astype(BF16)
            gw_ref[:, sl] += jnp.sum(dyn * nrm, axis=0, keepdims=True)

    tok = pl.BlockSpec((tm, D_MODEL), lambda i: (i, 0))
    row = pl.BlockSpec((1, D_MODEL), lambda i: (0, 0))
    return pl.pallas_call(
        body, name="gnorm_bwd", grid=(n // tm,),
        in_specs=[tok, tok, tok, row], out_specs=[tok, tok, row],
        out_shape=[jax.ShapeDtypeStruct((n, D_MODEL), F32), jax.ShapeDtypeStruct((n, D_MODEL), BF16),
                   jax.ShapeDtypeStruct((1, D_MODEL), F32)],
        compiler_params=_params(("arbitrary",)),
    )(dycat, y, z, w)


def _fox_prep(dtft, fbt):
    bsz, _, seq = dtft.shape

    def body(x_ref, b_ref, c_ref):
        triu = (_iota((LANES, LANES), 0) <= _iota((LANES, LANES), 1)).astype(F32)
        carry = jnp.zeros((LANES, 1), F32)
        for j in range(seq // LANES):
            sl = slice(LANES * j, LANES * (j + 1))
            lf = _log_sigmoid(x_ref[0, :, sl] + b_ref[...])
            c_ref[0, :, sl] = _hdot(lf, triu) + carry
            carry = carry + jnp.sum(lf, axis=1, keepdims=True)

    return pl.pallas_call(
        body, name="fox_prep", grid=(bsz,),
        in_specs=[pl.BlockSpec((1, LANES, seq), lambda b: (b, 0, 0)), pl.BlockSpec((LANES, 1), lambda b: (0, 0))],
        out_specs=pl.BlockSpec((1, LANES, seq), lambda b: (b, 0, 0)),
        out_shape=jax.ShapeDtypeStruct((bsz, LANES, seq), F32),
        compiler_params=_params(("parallel",)),
    )(dtft, fbt)


def _fox_prep_bwd(dck, dcq, dtft, fbt):
    bsz, _, seq = dtft.shape
    nj = seq // LANES

    def body(dck_ref, dcq_ref, x_ref, b_ref, df_ref, gb_ref):
        tril = (_iota((LANES, LANES), 0) >= _iota((LANES, LANES), 1)).astype(F32)
        carry = jnp.zeros((LANES, 1), F32)
        total = jnp.zeros((LANES, 1), F32)
        for j in range(nj - 1, -1, -1):
            sl = slice(LANES * j, LANES * (j + 1))
            dcv = dck_ref[0, :, sl] + dcq_ref[0, :, sl]
            dlf = _hdot(dcv, tril) + carry
            carry = carry + jnp.sum(dcv, axis=1, keepdims=True)
            df = dlf * _sigmoid(-(x_ref[0, :, sl] + b_ref[...]))
            df_ref[0, :, sl] = df
            total = total + jnp.sum(df, axis=1, keepdims=True)

        @pl.when(pl.program_id(0) == 0)
        def _():
            gb_ref[...] = jnp.zeros_like(gb_ref)

        gb_ref[...] += jnp.broadcast_to(total, (LANES, LANES))

    blk = pl.BlockSpec((1, LANES, seq), lambda b: (b, 0, 0))
    return pl.pallas_call(
        body, name="fox_prep_bwd", grid=(bsz,),
        in_specs=[blk, blk, blk, pl.BlockSpec((LANES, 1), lambda b: (0, 0))],
        out_specs=[blk, pl.BlockSpec((LANES, LANES), lambda b: (0, 0))],
        out_shape=[jax.ShapeDtypeStruct((bsz, LANES, seq), F32), jax.ShapeDtypeStruct((LANES, LANES), F32)],
        compiler_params=_params(("arbitrary",)),
    )(dck, dcq, dtft, fbt)


def _att_block(seq):
    return min(256, seq)


def _att_fwd(qkv, crow, bsz, seq):
    blk = _att_block(seq)
    nb = seq // blk

    def body(q_ref, k_ref, v_ref, c_ref, o_ref, lse_ref, km_scr, vm_scr):
        lane = _iota((seq, LANES), 1)
        rel = _iota((blk, blk), 0) - _iota((blk, blk), 1)
        zero_bf = jnp.zeros((seq, LANES), BF16)
        for hh in range(2):
            lm = (lane < HEAD_DIM) if hh == 0 else (lane >= HEAD_DIM)
            km_scr[...] = jnp.where(lm, k_ref[...], zero_bf)
            vm_scr[...] = jnp.where(lm, v_ref[...], zero_bf)

            def q_step(i, carry0):
                q0 = pl.multiple_of(i * blk, blk)
                qs = q_ref[pl.ds(q0, blk), :] * ATT_SCALE

                def kv_step(j, carry):
                    m, l, acc = carry
                    k0 = pl.multiple_of(j * blk, blk)
                    s = _bdot_nt(qs, km_scr[pl.ds(k0, blk), :]) - c_ref[0, 0, hh:hh + 1, pl.ds(k0, blk)]
                    s = jnp.where(rel + (q0 - k0) >= 0, s, NEG)
                    mn = jnp.maximum(m, jnp.max(s, axis=1, keepdims=True))
                    alpha = jnp.exp(m - mn)
                    pr = jnp.exp(s - mn)
                    l = alpha * l + jnp.sum(pr, axis=1, keepdims=True)
                    acc = alpha * acc + _bdot(pr, vm_scr[pl.ds(k0, blk), :])
                    return mn, l, acc

                init = (jnp.full((blk, 1), NEG, F32), jnp.zeros((blk, 1), F32), jnp.zeros((blk, LANES), F32))
                m, l, acc = lax.fori_loop(0, i + 1, kv_step, init)
                o = acc / l
                if hh == 0:
                    o_ref[pl.ds(q0, blk), :] = o
                else:
                    o_ref[pl.ds(q0, blk), :] += o
                lse_ref[0, hh, pl.ds(q0, blk), :] = m + jnp.log(l)
                return carry0

            lax.fori_loop(0, nb, q_step, 0)

    col = lambda off: pl.BlockSpec((seq, LANES), lambda b, p: (b, off + p))
    return pl.pallas_call(
        body, name="att_fwd", grid=(bsz, HEAD_PAIRS),
        in_specs=[col(0), col(HEAD_PAIRS), col(2 * HEAD_PAIRS),
                  pl.BlockSpec((1, 1, SUBLANES, seq), lambda b, p: (b, p, 0, 0))],
        out_specs=[pl.BlockSpec((seq, LANES), lambda b, p: (b, p)),
                   pl.BlockSpec((1, 2, seq, 1), lambda b, p: (b, p, 0, 0))],
        out_shape=[jax.ShapeDtypeStruct((bsz * seq, D_MODEL), F32), jax.ShapeDtypeStruct((bsz, SSD_HEADS, seq, 1), F32)],
        scratch_shapes=[pltpu.VMEM((seq, LANES), BF16), pltpu.VMEM((seq, LANES), BF16)],
        compiler_params=_params(("parallel", "parallel")),
    )(qkv, qkv, qkv, crow)


def _att_bwd(qkv, dycat, o, lse, crow, bsz, seq):
    blk = _att_block(seq)
    nb = seq // blk

    def body(q_ref, k_ref, v_ref, do_ref, o_ref, lse_ref, c_ref, dq_ref, dk_ref, dv_ref, dc_ref, dcq_ref,
             qm_scr, km_scr, vm_scr, dom_scr, d_scr, dq_scr):
        lane = _iota((seq, LANES), 1)
        rel = _iota((blk, blk), 0) - _iota((blk, blk), 1)
        zero_bf = jnp.zeros((seq, LANES), BF16)
        dc_ref[...] = jnp.zeros_like(dc_ref)
        dcq_ref[...] = jnp.zeros_like(dcq_ref)
        dq_scr[...] = jnp.zeros_like(dq_scr)
        for hh in range(2):
            lm = (lane < HEAD_DIM) if hh == 0 else (lane >= HEAD_DIM)
            dov = do_ref[...]
            qm_scr[...] = jnp.where(lm, q_ref[...], zero_bf) * ATT_SCALE
            km_scr[...] = jnp.where(lm, k_ref[...], zero_bf)
            vm_scr[...] = jnp.where(lm, v_ref[...], zero_bf)
            dom = jnp.where(lm, dov, 0.0).astype(BF16)
            dom_scr[...] = dom
            d_scr[...] = jnp.sum(dom.astype(F32) * o_ref[...], axis=1, keepdims=True)

            def kv_step(j, carry0):
                k0 = pl.multiple_of(j * blk, blk)
                kb = km_scr[pl.ds(k0, blk), :]
                vb = vm_scr[pl.ds(k0, blk), :]
                crow_j = c_ref[0, 0, hh:hh + 1, pl.ds(k0, blk)]

                def q_step(i, carry):
                    dk_acc, dv_acc, dc_acc = carry
                    q0 = pl.multiple_of(i * blk, blk)
                    rows = pl.ds(q0, blk)
                    qb = qm_scr[rows, :]
                    dob = dom_scr[rows, :]
                    s = _bdot_nt(qb, kb) - crow_j
                    s = jnp.where(rel + (q0 - k0) >= 0, s, NEG)
                    pr = jnp.exp(s - lse_ref[0, hh, rows, :])
                    dp = _bdot_nt(dob, vb)
                    ds = pr * (dp - d_scr[rows, :])
                    ds_bf = ds.astype(BF16)
                    dv_acc = dv_acc + _bdot_tn(pr, dob)
                    dk_acc = dk_acc + _bdot_tn(ds_bf, qb)
                    dq_scr[rows, :] += _bdot(ds_bf, kb)
                    dc_acc = dc_acc - jnp.sum(ds, axis=0, keepdims=True)
                    dcq_ref[0, hh, rows, :] += jnp.sum(ds, axis=1, keepdims=True)
                    return dk_acc, dv_acc, dc_acc

                init = (jnp.zeros((blk, LANES), F32), jnp.zeros((blk, LANES), F32), jnp.zeros((1, blk), F32))
                dk_acc, dv_acc, dc_acc = lax.fori_loop(j, nb, q_step, init)
                krows = pl.ds(k0, blk)
                if hh == 0:
                    dk_ref[krows, :] = dk_acc.astype(BF16)
                    dv_ref[krows, :] = dv_acc.astype(BF16)
                else:
                    dk_ref[krows, :] = (dk_ref[krows, :].astype(F32) + dk_acc).astype(BF16)
                    dv_ref[krows, :] = (dv_ref[krows, :].astype(F32) + dv_acc).astype(BF16)
                dc_ref[0, 0, hh:hh + 1, krows] = dc_acc
                return carry0

            lax.fori_loop(0, nb, kv_step, 0)
        dq_ref[...] = (dq_scr[...] * ATT_SCALE).astype(BF16)

    col = lambda off: pl.BlockSpec((seq, LANES), lambda b, p: (b, off + p))
    pair = pl.BlockSpec((1, 1, SUBLANES, seq), lambda b, p: (b, p, 0, 0))
    out = jax.ShapeDtypeStruct((bsz * seq, D_MODEL), BF16)
    return pl.pallas_call(
        body, name="att_bwd", grid=(bsz, HEAD_PAIRS),
        in_specs=[col(0), col(HEAD_PAIRS), col(2 * HEAD_PAIRS), col(HEAD_PAIRS), col(0),
                  pl.BlockSpec((1, 2, seq, 1), lambda b, p: (b, p, 0, 0)), pair],
        out_specs=[col(0), col(0), col(0), pair, pl.BlockSpec((1, 2, seq, 1), lambda b, p: (b, p, 0, 0))],
        out_shape=[out, out, out, jax.ShapeDtypeStruct((bsz, HEAD_PAIRS, SUBLANES, seq), F32),
                   jax.ShapeDtypeStruct((bsz, SSD_HEADS, seq, 1), F32)],
        scratch_shapes=[pltpu.VMEM((seq, LANES), BF16), pltpu.VMEM((seq, LANES), BF16), pltpu.VMEM((seq, LANES), BF16),
                        pltpu.VMEM((seq, LANES), BF16), pltpu.VMEM((seq, 1), F32), pltpu.VMEM((seq, LANES), F32)],
        compiler_params=_params(("parallel", "parallel")),
    )(qkv, qkv, qkv, dycat, o, lse, crow)


def _adamw_math(w, g, m, v):
    m = ADAM_B1 * m + (1.0 - ADAM_B1) * g
    v = ADAM_B2 * v + (1.0 - ADAM_B2) * jnp.square(g)
    m_hat = m / ADAM_C1
    v_hat = v / ADAM_C2
    delta = -ADAM_LR * (m_hat / (jnp.sqrt(v_hat) + ADAM_EPS) + ADAM_WD * w)
    return delta, m, v


def _adamw_big(recv, w, m, v):
    rows = w.shape[0]
    tr = 1136 if rows % 1136 == 0 else rows

    def body(r_ref, w_ref, m_ref, v_ref, g_ref, d_ref, mo_ref, vo_ref):
        g = r_ref[0].astype(F32)
        for s in range(1, N_DEV):
            g = g + r_ref[s].astype(F32)
        delta, mn, vn = _adamw_math(w_ref[...], g, m_ref[...], v_ref[...])
        g_ref[...] = g
        d_ref[...] = delta
        mo_ref[...] = mn
        vo_ref[...] = vn

    blk = pl.BlockSpec((tr, LANES), lambda i: (i, 0))
    shp = jax.ShapeDtypeStruct((rows, LANES), F32)
    return pl.pallas_call(
        body, name="adamw_big", grid=(rows // tr,),
        in_specs=[pl.BlockSpec((N_DEV, tr, LANES), lambda i: (0, i, 0)), blk, blk, blk],
        out_specs=[blk, blk, blk, blk], out_shape=[shp, shp, shp, shp],
        compiler_params=_params(("parallel",)),
    )(recv, w, m, v)


def _me():
    return lax.axis_index("x"), lax.axis_index("y"), lax.axis_index("c")


def _flip(pos, k):
    x, y, c = pos
    kx, ky, kc = (k >> 2) & 1, (k >> 1) & 1, k & 1
    return (x ^ kx if kx else x, y ^ ky if ky else y, c ^ kc if kc else c)


def _logical(pos):
    return 4 * pos[0] + 2 * pos[1] + pos[2]


def _all_gather_big(shard):
    rows = shard.shape[0]

    def body(x_ref, out_ref, send_sems, recv_sems, local_sem):
        x, y, c = _me()
        me, sibling = (x, y, c), (x, y, 1 - c)
        chips = [(1 - x, y), (x, 1 - y), (1 - x, 1 - y)]

        def slot(pos):
            return out_ref.at[_logical(pos)]

        def copy(k, block, to, src=None):
            return pltpu.make_async_remote_copy(
                src_ref=slot(block) if src is None else src, dst_ref=slot(block),
                send_sem=send_sems.at[k], recv_sem=recv_sems.at[k], device_id=to, device_id_type=MESH)

        mine = pltpu.make_async_copy(x_ref, slot(me), local_sem)
        mine.start()
        first = [copy(0, me, sibling, src=x_ref)]
        first += [copy(1 + j, me, (*chip, c), src=x_ref) for j, chip in enumerate(chips)]
        for cp in first:
            cp.start()
        passed = [copy(4 + j, (*chip, c), sibling) for j, chip in enumerate(chips)]
        for j, chip in enumerate(chips):
            copy(1 + j, (*chip, c), me).wait_recv()
            passed[j].start()
        copy(0, sibling, me).wait_recv()
        for j, chip in enumerate(chips):
            copy(4 + j, (*chip, 1 - c), me).wait_recv()
        for cp in first + passed:
            cp.wait_send()
        mine.wait()

    return pl.pallas_call(
        body, name="all_gather_big",
        out_shape=jax.ShapeDtypeStruct((N_DEV, rows, LANES), shard.dtype),
        in_specs=[pl.BlockSpec(memory_space=pl.ANY)], out_specs=pl.BlockSpec(memory_space=pl.ANY),
        scratch_shapes=[pltpu.SemaphoreType.DMA((7,)), pltpu.SemaphoreType.DMA((7,)), pltpu.SemaphoreType.DMA],
    )(shard)


def _all_to_all_big(parts):
    _, rows, _ = parts.shape

    def body(g_ref, out_ref, send_sems, recv_sems, local_sem):
        me = _me()
        mine = pltpu.make_async_copy(g_ref.at[_logical(me)], out_ref.at[_logical(me)], local_sem)
        mine.start()
        copies = []
        for k in range(1, N_DEV):
            peer = _flip(me, k)
            copies.append(pltpu.make_async_remote_copy(
                src_ref=g_ref.at[_logical(peer)], dst_ref=out_ref.at[_logical(me)],
                send_sem=send_sems.at[k - 1], recv_sem=recv_sems.at[k - 1], device_id=peer, device_id_type=MESH))
        for cp in copies:
            cp.start()
        for cp in copies:
            cp.wait_recv()
        for cp in copies:
            cp.wait_send()
        mine.wait()

    return pl.pallas_call(
        body, name="all_to_all_big",
        out_shape=jax.ShapeDtypeStruct((N_DEV, rows, LANES), parts.dtype),
        in_specs=[pl.BlockSpec(memory_space=pl.ANY)], out_specs=pl.BlockSpec(memory_space=pl.ANY),
        scratch_shapes=[pltpu.SemaphoreType.DMA((7,)), pltpu.SemaphoreType.DMA((7,)), pltpu.SemaphoreType.DMA],
    )(parts)


def _exchange_rows(x_ref, buf_ref, send_sems, recv_sems):
    me = _me()
    buf_ref[_logical(me)] = x_ref[...]
    copies = []
    for k in range(1, N_DEV):
        peer = _flip(me, k)
        copies.append(pltpu.make_async_remote_copy(
            src_ref=x_ref, dst_ref=buf_ref.at[_logical(me)],
            send_sem=send_sems.at[k - 1], recv_sem=recv_sems.at[k - 1], device_id=peer, device_id_type=MESH))
    for cp in copies:
        cp.start()
    for cp in copies:
        cp.wait_recv()
    for cp in copies:
        cp.wait_send()


def _sum_slots(buf_ref):
    total = buf_ref[0]
    for s in range(1, N_DEV):
        total = total + buf_ref[s]
    return total


def _small_gather(x):
    def body(x_ref, o_ref, buf_ref, send_sems, recv_sems):
        _exchange_rows(x_ref, buf_ref, send_sems, recv_sems)
        o_ref[...] = _sum_slots(buf_ref)

    return pl.pallas_call(
        body, name="small_gather", out_shape=jax.ShapeDtypeStruct(x.shape, F32),
        in_specs=[pl.BlockSpec(memory_space=pltpu.VMEM)], out_specs=pl.BlockSpec(memory_space=pltpu.VMEM),
        scratch_shapes=[pltpu.VMEM((N_DEV,) + x.shape, F32), pltpu.SemaphoreType.DMA((7,)), pltpu.SemaphoreType.DMA((7,))],
    )(x)


def _small_reduce_adamw(g, w, m, v):
    def body(g_ref, w_ref, m_ref, v_ref, go_ref, d_ref, mo_ref, vo_ref, buf_ref, send_sems, recv_sems):
        _exchange_rows(g_ref, buf_ref, send_sems, recv_sems)
        gs = _sum_slots(buf_ref)
        delta, mn, vn = _adamw_math(w_ref[...], gs, m_ref[...], v_ref[...])
        go_ref[...] = gs
        d_ref[...] = delta
        mo_ref[...] = mn
        vo_ref[...] = vn

    vm = pl.BlockSpec(memory_space=pltpu.VMEM)
    shp = jax.ShapeDtypeStruct(g.shape, F32)
    return pl.pallas_call(
        body, name="small_reduce_adamw", out_shape=[shp, shp, shp, shp],
        in_specs=[vm, vm, vm, vm], out_specs=[vm, vm, vm, vm],
        scratch_shapes=[pltpu.VMEM((N_DEV,) + g.shape, F32), pltpu.SemaphoreType.DMA((7,)), pltpu.SemaphoreType.DMA((7,))],
    )(g, w, m, v)


def _pad_cols(a, width):
    return jnp.pad(a, ((0, 0), (0, width - a.shape[1])))


def _local_step(x, target, norm_mix_w, w1, conv_w, conv_b, dt_bias, a_log, d_skip, ssd_norm_w, f_bias,
                w_out, norm_mlp_w, w_up, w_down, norm_final_w):
    bsz, seq, _ = x.shape
    n = bsz * seq
    x2 = x.reshape(n, D_MODEL)
    t2 = target.reshape(n, D_MODEL)
    nfw = norm_final_w.reshape(1, D_MODEL)

    bias = _pad_cols(dt_bias, LANES)
    a_neg = -jnp.exp(a_log)
    arow = _pad_cols(a_neg, LANES)
    biast, acol = bias.reshape(LANES, 1), arow.reshape(LANES, 1)
    dfull = jnp.repeat(d_skip, HEAD_DIM, axis=1)
    fbt = jnp.pad(f_bias, ((0, 0), (SSD_HEADS, LANES - 2 * SSD_HEADS))).reshape(LANES, 1)

    h1 = _rms_fwd(x2, norm_mix_w, "rms_fwd_mix")
    z = _mm_nn(h1, w1[:, :1024], "inproj_z", F32, 512, 512)
    xbc_raw = _mm_nn(h1, w1[:, 1024:2560], "inproj_xbc", F32, 512, 512)
    qkv = _mm_nn(h1, w1[:, 2560:DTF_COL], "inproj_qkv", BF16, 512, 512)
    dtf = _mm_nn(h1, w1[:, DTF_COL:], "inproj_dtf", F32, 512, LANES)
    dtft = dtf.reshape(bsz, seq, LANES).transpose(0, 2, 1)

    xbc = _conv_fwd(xbc_raw, conv_w, conv_b, bsz, seq)
    y_pre, hall = _ssd_fwd(xbc, dtf, dtft, bias, biast, arow, acol, dfull, bsz, seq)

    ccum = _fox_prep(dtft, fbt)
    crow = ccum[:, SSD_HEADS:2 * SSD_HEADS, :].reshape(bsz, HEAD_PAIRS, 2, seq)
    crow = jnp.pad(crow, ((0, 0), (0, 0), (0, SUBLANES - 2), (0, 0)))
    o_att, lse = _att_fwd(qkv, crow, bsz, seq)

    ycat = _gnorm_fwd(y_pre, z, o_att, ssd_norm_w)
    h2 = _mm_nn(ycat, w_out, "outproj", F32, 512, 512, res=x2)
    h2n = _rms_fwd(h2, norm_mlp_w, "rms_fwd_mlp")
    pre = _mm_nn(h2n, w_up, "mlp_up", F32, 512, 1024)
    h3 = _mlp_down_fwd(pre, w_down, h2)

    dh3, loss_row, g_nfw = _loss_bwd(h3, t2, nfw)
    dpre, u = _mlp_down_bwd(dh3, w_down.T, pre)
    g_w_down = _mm_tn(u, dh3, "grad_w_down", 1024, 1024, 512)
    g_w_up = _mm_tn(h2n, dpre, "grad_w_up", 1024, 1024, 512)
    dh2n = _mm_nn(dpre, w_up.T, "mlp_up_bwd", F32, 256, 512)
    dh2, g_nmlp = _rms_bwd(dh2n, h2, norm_mlp_w, dh3, "rms_bwd_mlp")

    g_w_out = _mm_tn(ycat, dh2, "grad_w_out", 1024, 1024, 512)
    dycat = _mm_nn(dh2, w_out.T, "outproj_bwd", F32, 512, 512)
    dy_pre, dz, g_ssdn = _gnorm_bwd(dycat, y_pre, z, ssd_norm_w)
    dq, dk, dv, dcrow, dcq = _att_bwd(qkv, dycat, o_att, lse, crow, bsz, seq)

    head_rows = ((0, 0), (SSD_HEADS, LANES - 2 * SSD_HEADS), (0, 0))
    dck = jnp.pad(dcrow[:, :, :2, :].reshape(bsz, SSD_HEADS, seq), head_rows)
    dft, g_fb = _fox_prep_bwd(dck, jnp.pad(dcq.reshape(bsz, SSD_HEADS, seq), head_rows), dtft, fbt)

    dxbc, ddt, ssd_acc = _ssd_bwd(dy_pre, xbc, dtf, dtft, bias, biast, arow, acol, dfull, hall, bsz, seq)
    dxbc_raw, g_cw, g_cb = _conv_bwd(dxbc, xbc_raw, conv_w, conv_b, bsz, seq)

    df = dft.transpose(0, 2, 1).reshape(n, LANES)
    ddtf = jnp.concatenate([ddt[:, :SSD_HEADS], df[:, SSD_HEADS:2 * SSD_HEADS],
                            jnp.zeros((n, LANES - 2 * SSD_HEADS), F32)], axis=1).astype(BF16)
    dproj = jnp.concatenate([dz, dxbc_raw, dq, dk, dv, ddtf], axis=1)
    g_w1 = _mm_tn(h1, dproj, "grad_w_in", 1024, 1152, 512)
    dh1 = _mm_nn(dproj, w1.T, "inproj_bwd", F32, 256, 512)
    dx, g_nmix = _rms_bwd(dh1, x2, norm_mix_w, dh2, "rms_bwd_mix")

    small = dict(
        norm_mix_w=g_nmix, norm_mlp_w=g_nmlp, norm_final_w=g_nfw, ssd_norm_w=g_ssdn, conv_b=g_cb, conv_w=g_cw,
        dt_bias=ssd_acc[16:17, :SSD_HEADS], a_log=ssd_acc[0:1, :SSD_HEADS], d_skip=ssd_acc[8:9, :SSD_HEADS],
        f_bias=g_fb[SSD_HEADS:2 * SSD_HEADS, 0].reshape(1, SSD_HEADS))
    big = dict(w1=g_w1, w_out=g_w_out, w_up=g_w_up, w_down=g_w_down)
    return loss_row[0, 0], dx.reshape(bsz, seq, D_MODEL), small, big


def _pack_big_shard(w_in, w_out, w_up, w_down, dtype):
    parts = [w_in.reshape(ROWS_IN, LANES), w_out.reshape(ROWS_OUT, LANES), w_up.reshape(ROWS_UP, LANES),
             w_down.reshape(ROWS_DOWN, LANES)]
    return jnp.concatenate(parts, axis=0).astype(dtype)


def _unpack_big_shard(flat):
    r1, r2, r3 = ROWS_IN, ROWS_IN + ROWS_OUT, ROWS_IN + ROWS_OUT + ROWS_UP
    return (flat[:r1].reshape(1, D_MODEL, IN_WIDTH // N_DEV), flat[r1:r2].reshape(1, MIX_WIDTH // N_DEV, D_MODEL),
            flat[r2:r3].reshape(1, D_MODEL, D_FF // N_DEV), flat[r3:].reshape(1, D_FF // N_DEV, D_MODEL))


def _w1_from_w_in(w_in_full):
    z, xbc, dt = w_in_full[:, :1024], w_in_full[:, 1024:2560], w_in_full[:, 2560:2576]
    qkv, f = w_in_full[:, 2576:5648], w_in_full[:, 5648:]
    pad = jnp.zeros((D_MODEL, W1_COLS - DTF_COL - 2 * SSD_HEADS), w_in_full.dtype)
    return jnp.concatenate([z, xbc, qkv, dt, f, pad], axis=1)


def _w_in_from_w1(g1):
    return jnp.concatenate([g1[:, :2560], g1[:, DTF_COL:DTF_COL + SSD_HEADS], g1[:, 2560:DTF_COL],
                            g1[:, DTF_COL + SSD_HEADS:DTF_COL + 2 * SSD_HEADS]], axis=1)


SMALL_LAYOUT = (("norm_mix_w", 1, D_MODEL), ("norm_mlp_w", 1, D_MODEL), ("norm_final_w", 1, D_MODEL),
                ("ssd_norm_w", 1, D_MODEL), ("conv_b", 1, CONV_CH), ("conv_w", CONV_K, CONV_CH),
                ("dt_bias", 1, SSD_HEADS), ("a_log", 1, SSD_HEADS), ("d_skip", 1, SSD_HEADS), ("f_bias", 1, SSD_HEADS))


def _pack_small(vals):
    tiles = []
    for name, nrows, width in SMALL_LAYOUT:
        tiles.append(jnp.pad(vals[name].reshape(nrows, width), ((0, SUBLANES - nrows), (0, SMALL_COLS - width))))
    return jnp.concatenate(tiles, axis=0)


def _unpack_small(packed, like):
    out = {}
    for i, (name, nrows, width) in enumerate(SMALL_LAYOUT):
        out[name] = packed[SUBLANES * i:SUBLANES * i + nrows, :width].reshape(like[name].shape)
    return out


def kernel(x, norm_mix_w, w_in, conv_w, conv_b, dt_bias, a_log, d_skip, ssd_norm_w, f_bias, w_out, norm_mlp_w, w_up, w_down, norm_final_w, loss_target, m_norm_mix_w, m_w_in, m_conv_w, m_conv_b, m_dt_bias, m_a_log, m_d_skip, m_ssd_norm_w, m_f_bias, m_w_out, m_norm_mlp_w, m_w_up, m_w_down, m_norm_final_w, v_norm_mix_w, v_w_in, v_conv_w, v_conv_b, v_dt_bias, v_a_log, v_d_skip, v_ssd_norm_w, v_f_bias, v_w_out, v_norm_mlp_w, v_w_up, v_w_down, v_norm_final_w):
    me = 4 * lax.axis_index("x") + 2 * lax.axis_index("y") + lax.axis_index("c")
    conv_shard_w = CONV_CH // N_DEV

    def place_conv(shard):
        return lax.dynamic_update_slice(jnp.zeros((CONV_K, CONV_CH), F32), shard[0], (jnp.zeros((), jnp.int32), me * conv_shard_w))

    gathered = _all_gather_big(_pack_big_shard(w_in[0], w_out[0], w_up[0], w_down[0], BF16))
    r1, r2, r3 = ROWS_IN, ROWS_IN + ROWS_OUT, ROWS_IN + ROWS_OUT + ROWS_UP
    w_in_full = gathered[:, :r1].reshape(N_DEV, D_MODEL, IN_WIDTH // N_DEV).transpose(1, 0, 2).reshape(D_MODEL, IN_WIDTH)
    w_out_full = gathered[:, r1:r2].reshape(MIX_WIDTH, D_MODEL)
    w_up_full = gathered[:, r2:r3].reshape(N_DEV, D_MODEL, D_FF // N_DEV).transpose(1, 0, 2).reshape(D_MODEL, D_FF)
    w_down_full = gathered[:, r3:].reshape(D_FF, D_MODEL)
    w1 = _w1_from_w_in(w_in_full)
    conv_full = _small_gather(jnp.pad(place_conv(conv_w), ((0, SUBLANES - CONV_K), (0, 0))))[:CONV_K]

    loss_local, grad_x, g_small, g_big = _local_step(
        x, loss_target, norm_mix_w, w1, conv_full, conv_b, dt_bias, a_log, d_skip, ssd_norm_w, f_bias,
        w_out_full, norm_mlp_w, w_up_full, w_down_full, norm_final_w)
    loss = lax.psum(loss_local, MESH_AXES)

    g_in = _w_in_from_w1(g_big["w1"]).reshape(D_MODEL, N_DEV, IN_WIDTH // N_DEV).transpose(1, 0, 2)
    g_up = g_big["w_up"].reshape(D_MODEL, N_DEV, D_FF // N_DEV).transpose(1, 0, 2)
    parts = jnp.concatenate([g_in.reshape(N_DEV, ROWS_IN, LANES), g_big["w_out"].reshape(N_DEV, ROWS_OUT, LANES),
                             g_up.reshape(N_DEV, ROWS_UP, LANES), g_big["w_down"].reshape(N_DEV, ROWS_DOWN, LANES)], axis=1)
    recv = _all_to_all_big(parts)
    big = _adamw_big(recv, _pack_big_shard(w_in[0], w_out[0], w_up[0], w_down[0], F32),
                     _pack_big_shard(m_w_in[0], m_w_out[0], m_w_up[0], m_w_down[0], F32),
                     _pack_big_shard(v_w_in[0], v_w_out[0], v_w_up[0], v_w_down[0], F32))
    big = [_unpack_big_shard(b) for b in big]

    like = dict(norm_mix_w=norm_mix_w, norm_mlp_w=norm_mlp_w, norm_final_w=norm_final_w, ssd_norm_w=ssd_norm_w,
                conv_b=conv_b, conv_w=jnp.zeros((1, CONV_K, CONV_CH), F32), dt_bias=dt_bias, a_log=a_log,
                d_skip=d_skip, f_bias=f_bias)
    w_small = dict(like, conv_w=place_conv(conv_w))
    m_small = dict(norm_mix_w=m_norm_mix_w, norm_mlp_w=m_norm_mlp_w, norm_final_w=m_norm_final_w,
                   ssd_norm_w=m_ssd_norm_w, conv_b=m_conv_b, conv_w=place_conv(m_conv_w), dt_bias=m_dt_bias,
                   a_log=m_a_log, d_skip=m_d_skip, f_bias=m_f_bias)
    v_small = dict(norm_mix_w=v_norm_mix_w, norm_mlp_w=v_norm_mlp_w, norm_final_w=v_norm_final_w,
                   ssd_norm_w=v_ssd_norm_w, conv_b=v_conv_b, conv_w=place_conv(v_conv_w), dt_bias=v_dt_bias,
                   a_log=v_a_log, d_skip=v_d_skip, f_bias=v_f_bias)
    small = _small_reduce_adamw(_pack_small(g_small), _pack_small(w_small), _pack_small(m_small), _pack_small(v_small))
    small = [_unpack_small(s, like) for s in small]
    for s in small:
        zero = jnp.zeros((), jnp.int32)
        s["conv_w"] = lax.dynamic_slice(s["conv_w"], (zero, zero, me * conv_shard_w), (1, CONV_K, conv_shard_w))

    order = ["norm_mix_w", "w_in", "conv_w", "conv_b", "dt_bias", "a_log", "d_skip", "ssd_norm_w", "f_bias", "w_out",
             "norm_mlp_w", "w_up", "w_down", "norm_final_w"]
    big_index = {"w_in": 0, "w_out": 1, "w_up": 2, "w_down": 3}
    outs = [loss, grad_x]
    for kind in range(4):
        for name in order:
            outs.append(big[kind][big_index[name]] if name in big_index else small[kind][name])
    return tuple(outs)
```

```python
import jax
import jax.numpy as jnp
from jax import lax
from jax.experimental import pallas as pl
from jax.experimental.pallas import tpu as pltpu

F32, BF16 = jnp.float32, jnp.bfloat16
HI = lax.Precision.HIGHEST

D_MODEL = 1024
SSD_HEADS = 16
HEAD_DIM = 64
SSD_STATE = 128
CHUNK = 128
CONV_CH = 1536
CONV_K = 4
D_FF = 4096
MIX_WIDTH = 2048
IN_WIDTH = 5664
NORM_EPS = 1e-5
ATT_SCALE = 0.125

LANES = 128
SUBLANES = 8
HEAD_PAIRS = SSD_HEADS // 2
NEG = -1e30
VMEM_LIMIT = 52 * 1024 * 1024

ROW_XBC, ROW_DT, ROW_Q, ROW_F = 1024, 2560, 2576, 5648
F_COL = SSD_HEADS

N_DEV = 8
MESH_AXES = ("x", "y", "c")
MESH = pl.DeviceIdType.MESH

ADAM_LR, ADAM_B1, ADAM_B2, ADAM_EPS, ADAM_WD, ADAM_STEP = 0.001, 0.9, 0.999, 1e-08, 0.01, 10
ADAM_C1 = 1.0 - ADAM_B1 ** ADAM_STEP
ADAM_C2 = 1.0 - ADAM_B2 ** ADAM_STEP

SMALL_COLS = CONV_CH


def _params(sem=None):
    return pltpu.CompilerParams(dimension_semantics=sem, vmem_limit_bytes=VMEM_LIMIT)


def _sigmoid(u):
    return 1.0 / (1.0 + jnp.exp(-u))


def _softplus(u):
    return jnp.maximum(u, 0.0) + jnp.log(1.0 + jnp.exp(-jnp.abs(u)))


def _log_sigmoid(u):
    return jnp.minimum(u, 0.0) - jnp.log(1.0 + jnp.exp(-jnp.abs(u)))


def _bdot(a, b):
    return jnp.dot(a.astype(BF16), b.astype(BF16), preferred_element_type=F32)


def _bdot_nt(a, b):
    return lax.dot_general(a.astype(BF16), b.astype(BF16), (((1,), (1,)), ((), ())), preferred_element_type=F32)


def _bdot_tn(a, b):
    return lax.dot_general(a.astype(BF16), b.astype(BF16), (((0,), (0,)), ((), ())), preferred_element_type=F32)


def _hdot(a, b):
    return jnp.dot(a, b, precision=HI, preferred_element_type=F32)


def _iota(shape, dim):
    return lax.broadcasted_iota(jnp.int32, shape, dim)


def _head_expand():
    return (jnp.right_shift(_iota((LANES, D_MODEL), 1), 6) == _iota((LANES, D_MODEL), 0)).astype(F32)


def _head_reduce():
    return (jnp.right_shift(_iota((D_MODEL, LANES), 0), 6) == _iota((D_MODEL, LANES), 1)).astype(F32)


def _rms_fwd(x, w, name):
    n, d = x.shape
    tm = min(512, n)

    def body(x_ref, w_ref, o_ref):
        xv = x_ref[...]
        r = lax.rsqrt(jnp.mean(xv * xv, axis=-1, keepdims=True) + NORM_EPS)
        o_ref[...] = (xv * r * w_ref[...]).astype(BF16)

    return pl.pallas_call(
        body, name=name, grid=(n // tm,),
        in_specs=[pl.BlockSpec((tm, d), lambda i: (i, 0)), pl.BlockSpec((1, d), lambda i: (0, 0))],
        out_specs=pl.BlockSpec((tm, d), lambda i: (i, 0)),
        out_shape=jax.ShapeDtypeStruct((n, d), BF16),
        compiler_params=_params(("parallel",)),
    )(x, w)


def _rms_bwd(dn, x, w, dres, name):
    n, d = x.shape
    tm = min(256, n)

    def body(dn_ref, x_ref, w_ref, dres_ref, dx_ref, gw_ref):
        xv = x_ref[...]
        r = lax.rsqrt(jnp.mean(xv * xv, axis=-1, keepdims=True) + NORM_EPS)
        nrm = xv * r
        dnv = dn_ref[...]
        g = dnv * w_ref[...]
        dx_ref[...] = dres_ref[...] + r * (g - nrm * jnp.mean(g * nrm, axis=-1, keepdims=True))

        @pl.when(pl.program_id(0) == 0)
        def _():
            gw_ref[...] = jnp.zeros_like(gw_ref)

        gw_ref[...] += jnp.sum(dnv * nrm, axis=0, keepdims=True)

    tok = pl.BlockSpec((tm, d), lambda i: (i, 0))
    row = pl.BlockSpec((1, d), lambda i: (0, 0))
    return pl.pallas_call(
        body, name=name, grid=(n // tm,),
        in_specs=[tok, tok, row, tok], out_specs=[tok, row],
        out_shape=[jax.ShapeDtypeStruct((n, d), F32), jax.ShapeDtypeStruct((1, d), F32)],
        compiler_params=_params(("arbitrary",)),
    )(dn, x, w, dres)


def _mm(pairs, name, out_dtype, tm, tn, nt=False, res=None):
    m = pairs[0][0].shape[0]
    n = pairs[0][1].shape[0 if nt else 1]
    tm, tn = min(tm, m), min(tn, n)
    npairs = len(pairs)

    def body(*refs):
        o_ref = refs[-1]
        acc = None if res is None else refs[2 * npairs][...]
        for p in range(npairs):
            a_v, b_v = refs[2 * p][...], refs[2 * p + 1][...]
            d = _bdot_nt(a_v, b_v) if nt else _bdot(a_v, b_v)
            acc = d if acc is None else acc + d
        o_ref[...] = acc.astype(o_ref.dtype)

    in_specs, args = [], []
    for a, b in pairs:
        k = a.shape[1]
        in_specs.append(pl.BlockSpec((tm, k), lambda i, j: (i, 0)))
        in_specs.append(pl.BlockSpec((tn, k), lambda i, j: (j, 0)) if nt else pl.BlockSpec((k, tn), lambda i, j: (0, j)))
        args += [a, b]
    if res is not None:
        in_specs.append(pl.BlockSpec((tm, tn), lambda i, j: (i, j)))
        args.append(res)
    return pl.pallas_call(
        body, name=name, grid=(m // tm, n // tn), in_specs=in_specs,
        out_specs=pl.BlockSpec((tm, tn), lambda i, j: (i, j)),
        out_shape=jax.ShapeDtypeStruct((m, n), out_dtype),
        compiler_params=_params(("parallel", "parallel")),
    )(*args)


def _mm_tn(a, b, name, tm, tn, tk):
    t, m = a.shape
    n = b.shape[1]
    tm, tn, tk = min(tm, m), min(tn, n), min(tk, t)
    nk = t // tk

    def body(a_ref, b_ref, o_ref, acc_ref):
        kk = pl.program_id(2)

        @pl.when(kk == 0)
        def _():
            acc_ref[...] = jnp.zeros_like(acc_ref)

        acc_ref[...] += _bdot_tn(a_ref[...], b_ref[...])

        @pl.when(kk == nk - 1)
        def _():
            o_ref[...] = acc_ref[...].astype(o_ref.dtype)

    return pl.pallas_call(
        body, name=name, grid=(m // tm, n // tn, nk),
        in_specs=[pl.BlockSpec((tk, tm), lambda i, j, kk: (kk, i)), pl.BlockSpec((tk, tn), lambda i, j, kk: (kk, j))],
        out_specs=pl.BlockSpec((tm, tn), lambda i, j, kk: (i, j)),
        out_shape=jax.ShapeDtypeStruct((m, n), BF16),
        scratch_shapes=[pltpu.VMEM((tm, tn), F32)],
        compiler_params=_params(("parallel", "parallel", "arbitrary")),
    )(a, b)


def _mlp_down_fwd(pre, w_down, h2):
    m, k = pre.shape
    n = w_down.shape[1]
    tm, tn = min(256, m), min(512, n)

    def body(p_ref, b_ref, r_ref, o_ref):
        u = jnp.square(jnp.maximum(p_ref[...], 0.0))
        o_ref[...] = r_ref[...] + _bdot(u, b_ref[...])

    return pl.pallas_call(
        body, name="mlp_down_fwd", grid=(m // tm, n // tn),
        in_specs=[pl.BlockSpec((tm, k), lambda i, j: (i, 0)), pl.BlockSpec((k, tn), lambda i, j: (0, j)),
                  pl.BlockSpec((tm, tn), lambda i, j: (i, j))],
        out_specs=pl.BlockSpec((tm, tn), lambda i, j: (i, j)),
        out_shape=jax.ShapeDtypeStruct((m, n), F32),
        compiler_params=_params(("parallel", "parallel")),
    )(pre, w_down, h2)


def _mlp_down_bwd(dh3, w_down, pre):
    m, k = dh3.shape
    n = w_down.shape[0]
    tm, tn = min(512, m), min(1024, n)

    def body(a_ref, b_ref, p_ref, dpre_ref, u_ref):
        r = jnp.maximum(p_ref[...], 0.0)
        du = _bdot_nt(a_ref[...], b_ref[...])
        dpre_ref[...] = (du * (2.0 * r)).astype(BF16)
        u_ref[...] = (r * r).astype(BF16)

    blk = pl.BlockSpec((tm, tn), lambda i, j: (i, j))
    return pl.pallas_call(
        body, name="mlp_down_bwd", grid=(m // tm, n // tn),
        in_specs=[pl.BlockSpec((tm, k), lambda i, j: (i, 0)), pl.BlockSpec((tn, k), lambda i, j: (j, 0)), blk],
        out_specs=[blk, blk],
        out_shape=[jax.ShapeDtypeStruct((m, n), BF16), jax.ShapeDtypeStruct((m, n), BF16)],
        compiler_params=_params(("parallel", "parallel")),
    )(dh3, w_down, pre)


def _loss_bwd(h3, target, w):
    n, d = h3.shape
    tm = min(256, n)

    def body(h_ref, t_ref, w_ref, dh_ref, loss_ref, gw_ref):
        xv = h_ref[...]
        r = lax.rsqrt(jnp.mean(xv * xv, axis=-1, keepdims=True) + NORM_EPS)
        nrm = xv * r
        wv = w_ref[...]
        err = nrm * wv - t_ref[...]
        part = 0.5 * jnp.sum(jnp.mean(err * err, axis=-1, keepdims=True), axis=0, keepdims=True)
        dy = err * (1.0 / d)
        g = dy * wv
        dh_ref[...] = r * (g - nrm * jnp.mean(g * nrm, axis=-1, keepdims=True))

        @pl.when(pl.program_id(0) == 0)
        def _():
            loss_ref[...] = jnp.zeros_like(loss_ref)
            gw_ref[...] = jnp.zeros_like(gw_ref)

        loss_ref[...] += jnp.broadcast_to(part, loss_ref.shape)
        gw_ref[...] += jnp.sum(dy * nrm, axis=0, keepdims=True)

    tok = pl.BlockSpec((tm, d), lambda i: (i, 0))
    row = pl.BlockSpec((1, d), lambda i: (0, 0))
    return pl.pallas_call(
        body, name="loss_bwd", grid=(n // tm,),
        in_specs=[tok, tok, row], out_specs=[tok, pl.BlockSpec((1, LANES), lambda i: (0, 0)), row],
        out_shape=[jax.ShapeDtypeStruct((n, d), F32), jax.ShapeDtypeStruct((1, LANES), F32),
                   jax.ShapeDtypeStruct((1, d), F32)],
        compiler_params=_params(("arbitrary",)),
    )(h3, target, w)


CONV_TC = 512


def _conv_fwd(xbc, cw, cb, bsz, seq):
    tt = min(256, seq)
    nt, hb = seq // tt, tt // SUBLANES
    x3 = xbc.reshape(bsz, seq, CONV_CH)

    def body(xm_ref, xh_ref, w_ref, b_ref, o_ref):
        i = pl.program_id(2)
        x = xm_ref[0]
        halo = jnp.where(i > 0, xh_ref[0], 0.0)
        xx = jnp.concatenate([halo, x], axis=0)
        acc = x * w_ref[3:4, :] + b_ref[...]
        for s in range(1, CONV_K):
            acc = acc + pltpu.roll(xx, s, 0)[SUBLANES:, :] * w_ref[3 - s:4 - s, :]
        o_ref[0] = acc * _sigmoid(acc)

    out = pl.pallas_call(
        body, name="conv_fwd", grid=(CONV_CH // CONV_TC, bsz, nt),
        in_specs=[pl.BlockSpec((1, tt, CONV_TC), lambda c, b, i: (b, i, c)),
                  pl.BlockSpec((1, SUBLANES, CONV_TC), lambda c, b, i: (b, jnp.maximum(i * hb - 1, 0), c)),
                  pl.BlockSpec((CONV_K, CONV_TC), lambda c, b, i: (0, c)),
                  pl.BlockSpec((1, CONV_TC), lambda c, b, i: (0, c))],
        out_specs=pl.BlockSpec((1, tt, CONV_TC), lambda c, b, i: (b, i, c)),
        out_shape=jax.ShapeDtypeStruct((bsz, seq, CONV_CH), F32),
        compiler_params=_params(("parallel", "parallel", "parallel")),
    )(x3, x3, cw, cb)
    return out.reshape(bsz * seq, CONV_CH)


def _conv_bwd(da, xbc, cw, cb, bsz, seq):
    tt = min(256, seq)
    nt, hb = seq // tt, tt // SUBLANES
    x3 = xbc.reshape(bsz, seq, CONV_CH)
    da3 = da.reshape(bsz, seq, CONV_CH)
    n2 = tt + SUBLANES

    def body(dam_ref, daa_ref, xm_ref, xb_ref, xa_ref, w_ref, b_ref, du_ref, gw_ref, gb_ref):
        bi, i = pl.program_id(1), pl.program_id(2)
        before = jnp.where(i > 0, xb_ref[0], 0.0)
        after = jnp.where(i < nt - 1, xa_ref[0], 0.0)
        xx = jnp.concatenate([before, xm_ref[0], after], axis=0)
        rolled = [xx] + [pltpu.roll(xx, s, 0) for s in range(1, CONV_K)]
        pre = b_ref[...]
        for s in range(CONV_K):
            pre = pre + rolled[s][SUBLANES:, :] * w_ref[3 - s:4 - s, :]
        da_ext = jnp.concatenate([dam_ref[0], jnp.where(i < nt - 1, daa_ref[0], 0.0)], axis=0)
        sg = _sigmoid(pre)
        dpre = da_ext * sg * (1.0 + pre * (1.0 - sg))
        du = dpre[:tt, :] * w_ref[3:4, :]
        for s in range(1, CONV_K):
            du = du + pltpu.roll(dpre, n2 - s, 0)[:tt, :] * w_ref[3 - s:4 - s, :]
        du_ref[0] = du.astype(BF16)
        dpm = dpre[:tt, :]
        gws = [jnp.sum(dpm * rolled[3 - kk][SUBLANES:SUBLANES + tt, :], axis=0, keepdims=True) for kk in range(CONV_K)]

        @pl.when((bi == 0) & (i == 0))
        def _():
            gw_ref[...] = jnp.zeros_like(gw_ref)
            gb_ref[...] = jnp.zeros_like(gb_ref)

        gw_ref[...] += jnp.concatenate(gws, axis=0)
        gb_ref[...] += jnp.sum(dpm, axis=0, keepdims=True)

    main = pl.BlockSpec((1, tt, CONV_TC), lambda c, b, i: (b, i, c))
    prev = pl.BlockSpec((1, SUBLANES, CONV_TC), lambda c, b, i: (b, jnp.maximum(i * hb - 1, 0), c))
    nxt = pl.BlockSpec((1, SUBLANES, CONV_TC), lambda c, b, i: (b, jnp.minimum((i + 1) * hb, seq // SUBLANES - 1), c))
    du, gw, gb = pl.pallas_call(
        body, name="conv_bwd", grid=(CONV_CH // CONV_TC, bsz, nt),
        in_specs=[main, nxt, main, prev, nxt,
                  pl.BlockSpec((CONV_K, CONV_TC), lambda c, b, i: (0, c)),
                  pl.BlockSpec((1, CONV_TC), lambda c, b, i: (0, c))],
        out_specs=[main, pl.BlockSpec((CONV_K, CONV_TC), lambda c, b, i: (0, c)),
                   pl.BlockSpec((1, CONV_TC), lambda c, b, i: (0, c))],
        out_shape=[jax.ShapeDtypeStruct((bsz, seq, CONV_CH), BF16), jax.ShapeDtypeStruct((CONV_K, CONV_CH), F32),
                   jax.ShapeDtypeStruct((1, CONV_CH), F32)],
        compiler_params=_params(("parallel", "arbitrary", "arbitrary")),
    )(da3, da3, x3, x3, x3, cw, cb)
    return du.reshape(bsz * seq, CONV_CH), gw, gb


def _ssd_prologue(dtf_ref, dtft_ref, bias_ref, biast_ref, arow_ref, acol_ref, dtfull_scr, acfull_scr, acr_scr):
    tri = (_iota((CHUNK, CHUNK), 1) <= _iota((CHUNK, CHUNK), 0)).astype(F32)
    triu = (_iota((CHUNK, CHUNK), 0) <= _iota((CHUNK, CHUNK), 1)).astype(F32)
    dtc = _softplus(dtf_ref[0] + bias_ref[...])
    a = dtc * arow_ref[...]
    acum = _hdot(tri, a)
    expand = _head_expand()
    dtfull_scr[...] = _hdot(dtc, expand)
    acfull_scr[...] = _hdot(acum, expand)
    dtr = _softplus(dtft_ref[0] + biast_ref[...])
    acr_scr[...] = _hdot(dtr * acol_ref[...], triu)
    return dtc, acum


def _decay_matrix(acum, acr_scr, h):
    lane = _iota((CHUNK, LANES), 1)
    ac_col = jnp.sum(jnp.where(lane == h, acum, 0.0), axis=1, keepdims=True)
    ac_row = acr_scr[h:h + 1, :]
    causal = _iota((CHUNK, CHUNK), 1) <= _iota((CHUNK, CHUNK), 0)
    return jnp.exp(jnp.where(causal, ac_col - ac_row, NEG))


def _ssd_fwd(xbc, dtf, dtft, bias, biast, arow, acol, dfull, bsz, seq):
    nc = seq // CHUNK
    x3 = xbc.reshape(bsz, seq, CONV_CH)

    def body(xbc_ref, dtf_ref, dtft_ref, bias_ref, biast_ref, arow_ref, acol_ref, dfull_ref,
             y_ref, hall_ref, h_scr, dtfull_scr, acfull_scr, acr_scr):
        @pl.when(pl.program_id(1) == 0)
        def _():
            h_scr[...] = jnp.zeros_like(h_scr)

        _, acum = _ssd_prologue(dtf_ref, dtft_ref, bias_ref, biast_ref, arow_ref, acol_ref,
                                dtfull_scr, acfull_scr, acr_scr)
        lane = _iota((CHUNK, LANES), 1)
        for g in range(2):
            bg = xbc_ref[0, :, D_MODEL + LANES * g:D_MODEL + LANES * (g + 1)]
            cg = xbc_ref[0, :, D_MODEL + 2 * LANES + LANES * g:D_MODEL + 2 * LANES + LANES * (g + 1)]
            cg_bf = cg.astype(BF16)
            gmat = _bdot_nt(cg_bf, bg)
            bgt_bf = bg.T.astype(BF16)
            for j in range(4):
                p = 4 * g + j
                sl = slice(LANES * p, LANES * (p + 1))
                xs = xbc_ref[0, :, sl]
                ac = acfull_scr[:, sl]
                acl = acfull_scr[CHUNK - 1:CHUNK, sl]
                xdt = xs * dtfull_scr[:, sl]
                xdt_bf = xdt.astype(BF16)
                hp = h_scr[p]
                hall_ref[0, 0, p] = hp
                yo = _bdot(cg_bf, hp) * jnp.exp(ac)
                yd = []
                for hh in range(2):
                    mmat = gmat * _decay_matrix(acum, acr_scr, 2 * p + hh)
                    yd.append(_bdot(mmat, xdt_bf))
                y_ref[0, :, sl] = jnp.where(lane < HEAD_DIM, yd[0], yd[1]) + yo + dfull_ref[:, sl] * xs
                h_scr[p] = hp * jnp.exp(acl) + _bdot(bgt_bf, xdt * jnp.exp(acl - ac))

    row = lambda w: pl.BlockSpec((1, w), lambda b, c: (0, 0))
    y, hall = pl.pallas_call(
        body, name="ssd_fwd", grid=(bsz, nc),
        in_specs=[pl.BlockSpec((1, CHUNK, CONV_CH), lambda b, c: (b, c, 0)),
                  pl.BlockSpec((1, CHUNK, LANES), lambda b, c: (b, c, 0)),
                  pl.BlockSpec((1, LANES, CHUNK), lambda b, c: (b, 0, c)),
                  row(LANES), pl.BlockSpec((LANES, 1), lambda b, c: (0, 0)),
                  row(LANES), pl.BlockSpec((LANES, 1), lambda b, c: (0, 0)), row(D_MODEL)],
        out_specs=[pl.BlockSpec((1, CHUNK, D_MODEL), lambda b, c: (b, c, 0)),
                   pl.BlockSpec((1, 1, HEAD_PAIRS, SSD_STATE, LANES), lambda b, c: (b, c, 0, 0, 0))],
        out_shape=[jax.ShapeDtypeStruct((bsz, seq, D_MODEL), F32),
                   jax.ShapeDtypeStruct((bsz, nc, HEAD_PAIRS, SSD_STATE, LANES), F32)],
        scratch_shapes=[pltpu.VMEM((HEAD_PAIRS, SSD_STATE, LANES), F32), pltpu.VMEM((CHUNK, D_MODEL), F32),
                        pltpu.VMEM((CHUNK, D_MODEL), F32), pltpu.VMEM((LANES, CHUNK), F32)],
        compiler_params=_params(("parallel", "arbitrary")),
    )(x3, dtf.reshape(bsz, seq, LANES), dtft, bias, biast, arow, acol, dfull)
    return y.reshape(bsz * seq, D_MODEL), hall


def _ssd_bwd(dy, xbc, dtf, dtft, bias, biast, arow, acol, dfull, hall, bsz, seq):
    nc = seq // CHUNK
    x3 = xbc.reshape(bsz, seq, CONV_CH)
    dy3 = dy.reshape(bsz, seq, D_MODEL)

    def body(dy_ref, xbc_ref, dtf_ref, dtft_ref, bias_ref, biast_ref, arow_ref, acol_ref, dfull_ref, hall_ref,
             dx_ref, ddt_ref, acc_ref, dh_scr, dtfull_scr, acfull_scr, acr_scr, csum_scr, dacf_scr, ddtf_scr, ddl_scr):
        first = (pl.program_id(0) == 0) & (pl.program_id(1) == 0)

        @pl.when(first)
        def _():
            acc_ref[...] = jnp.zeros_like(acc_ref)

        @pl.when(pl.program_id(1) == 0)
        def _():
            dh_scr[...] = jnp.zeros_like(dh_scr)

        dtc, acum = _ssd_prologue(dtf_ref, dtft_ref, bias_ref, biast_ref, arow_ref, acol_ref,
                                  dtfull_scr, acfull_scr, acr_scr)
        csum_scr[...] = jnp.zeros_like(csum_scr)
        lane = _iota((CHUNK, LANES), 1)
        last_row = _iota((CHUNK, LANES), 0) == CHUNK - 1
        rs16 = jnp.zeros((CHUNK, LANES), F32)
        for g in range(2):
            bsl = slice(D_MODEL + LANES * g, D_MODEL + LANES * (g + 1))
            csl = slice(D_MODEL + 2 * LANES + LANES * g, D_MODEL + 2 * LANES + LANES * (g + 1))
            bg_bf = xbc_ref[0, :, bsl].astype(BF16)
            cg = xbc_ref[0, :, csl]
            cg_bf = cg.astype(BF16)
            cgt_bf = cg.T.astype(BF16)
            gmat = _bdot_nt(cg_bf, bg_bf)
            dg = jnp.zeros((CHUNK, CHUNK), F32)
            dbg = jnp.zeros((CHUNK, SSD_STATE), F32)
            dcg = jnp.zeros((CHUNK, SSD_STATE), F32)
            for j in range(4):
                p = 4 * g + j
                sl = slice(LANES * p, LANES * (p + 1))
                xs = xbc_ref[0, :, sl]
                dt = dtfull_scr[:, sl]
                ac = acfull_scr[:, sl]
                acl = acfull_scr[CHUNK - 1:CHUNK, sl]
                dyp = dy_ref[0, :, sl]
                xdt = xs * dt
                xdt_bf = xdt.astype(BF16)
                e = jnp.exp(ac)
                dsd = jnp.exp(acl - ac)
                cd = jnp.exp(acl)
                xds_bf = (xdt * dsd).astype(BF16)
                hp = hall_ref[0, 0, p]
                hp_bf = hp.astype(BF16)
                dhn = dh_scr[p]
                dhn_bf = dhn.astype(BF16)
                yo = _bdot(cg_bf, hp_bf) * e
                dw_bf = (dyp * e).astype(BF16)
                dcg = dcg + _bdot_nt(dw_bf, hp_bf)
                dhin = _bdot(cgt_bf, dw_bf)
                dac = dyp * yo
                dacl = jnp.sum(dhn * hp, axis=0, keepdims=True) * cd
                dxds = _bdot(bg_bf, dhn_bf)
                dbg = dbg + _bdot_nt(xds_bf, dhn_bf)
                dxdt = dxds * dsd
                tdec = dxds * xdt * dsd
                dacl = dacl + jnp.sum(tdec, axis=0, keepdims=True)
                dac = dac - tdec
                dh_scr[p] = dhn * cd + dhin
                for hh in range(2):
                    h = 2 * p + hh
                    lm = (lane < HEAD_DIM) if hh == 0 else (lane >= HEAD_DIM)
                    dec = _decay_matrix(acum, acr_scr, h)
                    mmat = gmat * dec
                    dyh_bf = jnp.where(lm, dyp, 0.0).astype(BF16)
                    dm = _bdot_nt(dyh_bf, xdt_bf)
                    dxdt = dxdt + _bdot(mmat.T, dyh_bf)
                    dg = dg + dm * dec
                    q = dm * mmat
                    rs16 = rs16 + jnp.where(lane == h, jnp.sum(q, axis=1, keepdims=True), 0.0)
                    csum_scr[h:h + 1, :] = jnp.sum(q, axis=0, keepdims=True)
                dx_ref[0, :, sl] = dfull_ref[:, sl] * dyp + dxdt * dt
                dacf_scr[:, sl] = dac + jnp.where(last_row, dacl, 0.0)
                ddtf_scr[:, sl] = dxdt * xs
                ddl_scr[:, sl] = jnp.broadcast_to(jnp.sum(dyp * xs, axis=0, keepdims=True), (SUBLANES, LANES))
            dg_bf = dg.astype(BF16)
            dx_ref[0, :, bsl] = dbg + _bdot(dg.T, cg_bf)
            dx_ref[0, :, csl] = dcg + _bdot(dg_bf, bg_bf)
        reduce = _head_reduce()
        triu = (_iota((CHUNK, CHUNK), 0) <= _iota((CHUNK, CHUNK), 1)).astype(F32)
        dac16 = _hdot(dacf_scr[...], reduce) + rs16 - csum_scr[...].T
        da16 = _hdot(triu, dac16)
        ddt16 = da16 * arow_ref[...] + _hdot(ddtf_scr[...], reduce)
        ddtraw = ddt16 * _sigmoid(dtf_ref[0] + bias_ref[...])
        ddt_ref[0] = ddtraw
        acc_ref[0:8, :] += jnp.broadcast_to(jnp.sum(da16 * dtc * arow_ref[...], axis=0, keepdims=True), (SUBLANES, LANES))
        acc_ref[8:16, :] += _hdot(ddl_scr[...], reduce)
        acc_ref[16:24, :] += jnp.broadcast_to(jnp.sum(ddtraw, axis=0, keepdims=True), (SUBLANES, LANES))

    rev = lambda b, c: (b, nc - 1 - c, 0)
    row = lambda w: pl.BlockSpec((1, w), lambda b, c: (0, 0))
    dx, ddt, acc = pl.pallas_call(
        body, name="ssd_bwd", grid=(bsz, nc),
        in_specs=[pl.BlockSpec((1, CHUNK, D_MODEL), rev), pl.BlockSpec((1, CHUNK, CONV_CH), rev),
                  pl.BlockSpec((1, CHUNK, LANES), rev),
                  pl.BlockSpec((1, LANES, CHUNK), lambda b, c: (b, 0, nc - 1 - c)),
                  row(LANES), pl.BlockSpec((LANES, 1), lambda b, c: (0, 0)),
                  row(LANES), pl.BlockSpec((LANES, 1), lambda b, c: (0, 0)), row(D_MODEL),
                  pl.BlockSpec((1, 1, HEAD_PAIRS, SSD_STATE, LANES), lambda b, c: (b, nc - 1 - c, 0, 0, 0))],
        out_specs=[pl.BlockSpec((1, CHUNK, CONV_CH), rev), pl.BlockSpec((1, CHUNK, LANES), rev),
                   pl.BlockSpec((3 * SUBLANES, LANES), lambda b, c: (0, 0))],
        out_shape=[jax.ShapeDtypeStruct((bsz, seq, CONV_CH), F32), jax.ShapeDtypeStruct((bsz, seq, LANES), F32),
                   jax.ShapeDtypeStruct((3 * SUBLANES, LANES), F32)],
        scratch_shapes=[pltpu.VMEM((HEAD_PAIRS, SSD_STATE, LANES), F32), pltpu.VMEM((CHUNK, D_MODEL), F32),
                        pltpu.VMEM((CHUNK, D_MODEL), F32), pltpu.VMEM((LANES, CHUNK), F32),
                        pltpu.VMEM((LANES, CHUNK), F32), pltpu.VMEM((CHUNK, D_MODEL), F32),
                        pltpu.VMEM((CHUNK, D_MODEL), F32), pltpu.VMEM((SUBLANES, D_MODEL), F32)],
        compiler_params=_params(("arbitrary", "arbitrary")),
    )(dy3, x3, dtf.reshape(bsz, seq, LANES), dtft, bias, biast, arow, acol, dfull, hall)
    return dx.reshape(bsz * seq, CONV_CH), ddt.reshape(bsz * seq, LANES), acc


GROUP_W = D_MODEL // 2


def _gnorm_fwd(y, z, o_att, w):
    n = y.shape[0]
    tm = min(512, n)

    def body(y_ref, z_ref, o_ref, w_ref, out_ref):
        zv = z_ref[...]
        g = y_ref[...] * (zv * _sigmoid(zv))
        for gi in range(2):
            sl = slice(GROUP_W * gi, GROUP_W * (gi + 1))
            gs = g[:, sl]
            r = lax.rsqrt(jnp.mean(gs * gs, axis=-1, keepdims=True) + NORM_EPS)
            out_ref[:, sl] = (gs * r * w_ref[:, sl]).astype(BF16)
        out_ref[:, D_MODEL:] = o_ref[...].astype(BF16)

    tok = pl.BlockSpec((tm, D_MODEL), lambda i: (i, 0))
    return pl.pallas_call(
        body, name="gnorm_fwd", grid=(n // tm,),
        in_specs=[tok, tok, tok, pl.BlockSpec((1, D_MODEL), lambda i: (0, 0))],
        out_specs=pl.BlockSpec((tm, MIX_WIDTH), lambda i: (i, 0)),
        out_shape=jax.ShapeDtypeStruct((n, MIX_WIDTH), BF16),
        compiler_params=_params(("parallel",)),
    )(y, z, o_att, w)


def _gnorm_bwd(dycat, y, z, w):
    n = y.shape[0]
    tm = min(256, n)

    def body(dn_ref, y_ref, z_ref, w_ref, dy_ref, dz_ref, gw_ref):
        zv, yv = z_ref[...], y_ref[...]
        sz = _sigmoid(zv)
        sil = zv * sz
        g = yv * sil

        @pl.when(pl.program_id(0) == 0)
        def _():
            gw_ref[...] = jnp.zeros_like(gw_ref)

        for gi in range(2):
            sl = slice(GROUP_W * gi, GROUP_W * (gi + 1))
            gs = g[:, sl]
            r = lax.rsqrt(jnp.mean(gs * gs, axis=-1, keepdims=True) + NORM_EPS)
            nrm = gs * r
            dyn = dn_ref[:, sl]
            dn = dyn * w_ref[:, sl]
            dgs = r * (dn - nrm * jnp.mean(dn * nrm, axis=-1, keepdims=True))
            dy_ref[:, sl] = dgs * sil[:, sl]
            dz_ref[:, sl] = (dgs * yv[:, sl] * (sz[:, sl] * (1.0 + zv[:, sl] * (1.0 - sz[:, sl])))).astype(BF16)
            gw_ref[:, sl] += jnp.sum(dyn * nrm, axis=0, keepdims=True)

    tok = pl.BlockSpec((tm, D_MODEL), lambda i: (i, 0))
    row = pl.BlockSpec((1, D_MODEL), lambda i: (0, 0))
    return pl.pallas_call(
        body, name="gnorm_bwd", grid=(n // tm,),
        in_specs=[tok, tok, tok, row], out_specs=[tok, tok, row],
        out_shape=[jax.ShapeDtypeStruct((n, D_MODEL), F32), jax.ShapeDtypeStruct((n, D_MODEL), BF16),
                   jax.ShapeDtypeStruct((1, D_MODEL), F32)],
        compiler_params=_params(("arbitrary",)),
    )(dycat, y, z, w)


def _fox_prep(dtf, fb, bsz, seq):
    def body(x_ref, b_ref, c_ref):
        tri = (_iota((CHUNK, CHUNK), 1) <= _iota((CHUNK, CHUNK), 0)).astype(F32)
        carry = jnp.zeros((1, LANES), F32)
        for j in range(seq // CHUNK):
            sl = slice(CHUNK * j, CHUNK * (j + 1))
            lf = _log_sigmoid(x_ref[sl, :] + b_ref[...])
            c_ref[sl, :] = _hdot(tri, lf) + carry
            carry = carry + jnp.sum(lf, axis=0, keepdims=True)

    blk = pl.BlockSpec((seq, LANES), lambda b: (b, 0))
    return pl.pallas_call(
        body, name="fox_prep", grid=(bsz,),
        in_specs=[blk, pl.BlockSpec((1, LANES), lambda b: (0, 0))], out_specs=blk,
        out_shape=jax.ShapeDtypeStruct((bsz * seq, LANES), F32),
        compiler_params=_params(("parallel",)),
    )(dtf, fb)


def _fox_prep_bwd(dck, dcq, ddt, dtf, fb, bsz, seq):
    nj = seq // CHUNK

    def body(dck_ref, dcq_ref, ddt_ref, x_ref, b_ref, out_ref, gb_ref):
        triu = (_iota((CHUNK, CHUNK), 0) <= _iota((CHUNK, CHUNK), 1)).astype(F32)
        is_f = (_iota((CHUNK, LANES), 1) >= F_COL) & (_iota((CHUNK, LANES), 1) < 2 * F_COL)
        carry = jnp.zeros((1, LANES), F32)
        total = jnp.zeros((1, LANES), F32)
        for j in range(nj - 1, -1, -1):
            sl = slice(CHUNK * j, CHUNK * (j + 1))
            dcv = dck_ref[sl, :] + dcq_ref[sl, :]
            dlf = _hdot(triu, dcv) + carry
            carry = carry + jnp.sum(dcv, axis=0, keepdims=True)
            df = jnp.where(is_f, dlf * _sigmoid(-(x_ref[sl, :] + b_ref[...])), 0.0)
            out_ref[sl, :] = (df + ddt_ref[sl, :]).astype(BF16)
            total = total + jnp.sum(df, axis=0, keepdims=True)

        @pl.when(pl.program_id(0) == 0)
        def _():
            gb_ref[...] = jnp.zeros_like(gb_ref)

        gb_ref[...] += jnp.broadcast_to(total, (SUBLANES, LANES))

    blk = pl.BlockSpec((seq, LANES), lambda b: (b, 0))
    return pl.pallas_call(
        body, name="fox_prep_bwd", grid=(bsz,),
        in_specs=[blk, blk, blk, blk, pl.BlockSpec((1, LANES), lambda b: (0, 0))],
        out_specs=[blk, pl.BlockSpec((SUBLANES, LANES), lambda b: (0, 0))],
        out_shape=[jax.ShapeDtypeStruct((bsz * seq, LANES), BF16), jax.ShapeDtypeStruct((SUBLANES, LANES), F32)],
        compiler_params=_params(("arbitrary",)),
    )(dck, dcq, ddt, dtf, fb)


AUX_C = 3
AUX_ONE = 3


def _att_block(seq):
    return min(256, seq)


def _att_operands(q_ref, k_ref, c_ref, seq, hh, pair):
    lane = _iota((seq, LANES), 1)
    lm = (lane < HEAD_DIM) if hh == 0 else (lane >= HEAD_DIM)
    base = HEAD_DIM * (1 - hh)
    negc = -jnp.sum(jnp.where(lane == F_COL + 2 * pair + hh, c_ref[...], 0.0), axis=1, keepdims=True)
    c1 = negc.astype(BF16)
    r1 = negc - c1.astype(F32)
    c2 = r1.astype(BF16)
    c3 = (r1 - c2.astype(F32)).astype(BF16)
    zero = jnp.zeros((seq, LANES), BF16)
    one = jnp.ones((seq, LANES), BF16)
    ka = jnp.where(lm, k_ref[...], jnp.where(lane == base, c1, jnp.where(lane == base + 1, c2, jnp.where(
        lane == base + 2, c3, jnp.where(lane == base + AUX_ONE, one, zero)))))
    qa = jnp.where(lm, q_ref[...] * ATT_SCALE, jnp.where((lane >= base) & (lane < base + AUX_C), one, zero))
    return lm, base, qa, ka


def _att_fwd(qkv, ccol, bsz, seq):
    blk = _att_block(seq)
    nb = seq // blk

    def body(q_ref, k_ref, v_ref, c_ref, o_ref, lse_ref, qa_scr, ka_scr, va_scr):
        pair = pl.program_id(1)
        causal = _iota((blk, blk), 0) >= _iota((blk, blk), 1)
        for hh in range(2):
            lm, _, qa, ka = _att_operands(q_ref, k_ref, c_ref, seq, hh, pair)
            qa_scr[hh] = qa
            ka_scr[hh] = ka
            va_scr[hh] = jnp.where(lm, v_ref[...], jnp.zeros((seq, LANES), BF16))

        def q_step(i, carry0):
            rows = pl.ds(pl.multiple_of(i * blk, blk), blk)

            def block(j, carry, masked):
                krows = pl.ds(pl.multiple_of(j * blk, blk), blk)
                out = []
                for hh in range(2):
                    m, l, acc = carry[hh]
                    s = _bdot_nt(qa_scr[hh, rows, :], ka_scr[hh, krows, :])
                    if masked:
                        s = jnp.where(causal, s, NEG)
                    mn = jnp.maximum(m, jnp.max(s, axis=1, keepdims=True))
                    alpha = jnp.exp(m - mn)
                    pr = jnp.exp(s - mn)
                    l = alpha * l + jnp.sum(pr, axis=1, keepdims=True)
                    acc = alpha * acc + _bdot(pr, va_scr[hh, krows, :])
                    out.append((mn, l, acc))
                return tuple(out)

            one = (jnp.full((blk, 1), NEG, F32), jnp.zeros((blk, 1), F32), jnp.zeros((blk, LANES), F32))
            carry = lax.fori_loop(0, i, lambda j, c: block(j, c, False), (one, one))
            (m0, l0, acc0), (m1, l1, acc1) = block(i, carry, True)
            o_ref[rows, :] = acc0 * (1.0 / l0) + acc1 * (1.0 / l1)
            lse_ref[0, 0, rows, :] = m0 + jnp.log(l0)
            lse_ref[0, 1, rows, :] = m1 + jnp.log(l1)
            return carry0

        lax.fori_loop(0, nb, q_step, 0)

    col = lambda off: pl.BlockSpec((seq, LANES), lambda b, p: (b, off + p))
    head2 = pltpu.VMEM((2, seq, LANES), BF16)
    return pl.pallas_call(
        body, name="att_fwd", grid=(bsz, HEAD_PAIRS),
        in_specs=[col(0), col(HEAD_PAIRS), col(2 * HEAD_PAIRS), pl.BlockSpec((seq, LANES), lambda b, p: (b, 0))],
        out_specs=[pl.BlockSpec((seq, LANES), lambda b, p: (b, p)),
                   pl.BlockSpec((1, 2, seq, 1), lambda b, p: (b, p, 0, 0))],
        out_shape=[jax.ShapeDtypeStruct((bsz * seq, D_MODEL), F32), jax.ShapeDtypeStruct((bsz, SSD_HEADS, seq, 1), F32)],
        scratch_shapes=[head2, head2, head2],
        compiler_params=_params(("parallel", "parallel")),
    )(qkv, qkv, qkv, ccol)


def _att_bwd(qkv, dycat, o, lse, ccol, bsz, seq):
    blk = _att_block(seq)
    nb = seq // blk

    def body(q_ref, k_ref, v_ref, do_ref, o_ref, lse_ref, c_ref, dq_ref, dk_ref, dv_ref, dck_ref, dcq_ref,
             qa_scr, ka_scr, va_scr, doa_scr, d_scr, dq_scr, dk_scr, dv_scr):
        pair = pl.program_id(1)
        causal = _iota((blk, blk), 0) >= _iota((blk, blk), 1)
        lane_t = _iota((seq, LANES), 1)
        lane_b = _iota((blk, LANES), 1)
        masks, bases = [], []
        for hh in range(2):
            lm, base, qa, ka = _att_operands(q_ref, k_ref, c_ref, seq, hh, pair)
            masks.append(lm)
            bases.append(base)
            qa_scr[hh] = qa
            ka_scr[hh] = ka
            va_scr[hh] = jnp.where(lm, v_ref[...], jnp.zeros((seq, LANES), BF16))
            doa = jnp.where(lm, do_ref[...], 0.0).astype(BF16)
            doa_scr[hh] = doa
            d_scr[hh] = jnp.sum(doa.astype(F32) * o_ref[...], axis=1, keepdims=True)
        dq_scr[...] = jnp.zeros_like(dq_scr)

        def kv_step(j, carry0):
            krows = pl.ds(pl.multiple_of(j * blk, blk), blk)
            dk_scr[...] = jnp.zeros_like(dk_scr)
            dv_scr[...] = jnp.zeros_like(dv_scr)

            def block(i, masked):
                rows = pl.ds(pl.multiple_of(i * blk, blk), blk)
                for hh in range(2):
                    qb = qa_scr[hh, rows, :]
                    dob = doa_scr[hh, rows, :]
                    kb = ka_scr[hh, krows, :]
                    s = _bdot_nt(qb, kb)
                    if masked:
                        s = jnp.where(causal, s, NEG)
                    pr = jnp.exp(s - lse_ref[0, hh, rows, :])
                    dp = _bdot_nt(dob, va_scr[hh, krows, :])
                    ds = (pr * (dp - d_scr[hh, rows, :])).astype(BF16)
                    dv_scr[hh] += _bdot_tn(pr, dob)
                    dk_scr[hh] += _bdot_tn(ds, qb)
                    dq_scr[hh, rows, :] += _bdot(ds, kb)

            block(j, True)

            def q_step(i, carry):
                block(i, False)
                return carry

            lax.fori_loop(j + 1, nb, q_step, 0)
            lmb0 = lane_b < HEAD_DIM
            dk_ref[krows, :] = jnp.where(lmb0, dk_scr[0], dk_scr[1]).astype(BF16)
            dv_ref[krows, :] = (dv_scr[0] + dv_scr[1]).astype(BF16)
            for hh in range(2):
                dck_ref[0, hh, krows, :] = -jnp.sum(jnp.where(lane_b == bases[hh], dk_scr[hh], 0.0), axis=1, keepdims=True)
            return carry0

        lax.fori_loop(0, nb, kv_step, 0)
        dq_ref[...] = (jnp.where(masks[0], dq_scr[0], dq_scr[1]) * ATT_SCALE).astype(BF16)
        for hh in range(2):
            dcq_ref[0, hh] = jnp.sum(jnp.where(lane_t == bases[hh] + AUX_ONE, dq_scr[hh], 0.0), axis=1, keepdims=True)

    col = lambda off: pl.BlockSpec((seq, LANES), lambda b, p: (b, off + p))
    vec = pl.BlockSpec((1, 2, seq, 1), lambda b, p: (b, p, 0, 0))
    out = jax.ShapeDtypeStruct((bsz * seq, D_MODEL), BF16)
    vec_shape = jax.ShapeDtypeStruct((bsz, SSD_HEADS, seq, 1), F32)
    head2 = pltpu.VMEM((2, seq, LANES), BF16)
    return pl.pallas_call(
        body, name="att_bwd", grid=(bsz, HEAD_PAIRS),
        in_specs=[col(0), col(HEAD_PAIRS), col(2 * HEAD_PAIRS), col(HEAD_PAIRS), col(0), vec,
                  pl.BlockSpec((seq, LANES), lambda b, p: (b, 0))],
        out_specs=[col(0), col(0), col(0), vec, vec],
        out_shape=[out, out, out, vec_shape, vec_shape],
        scratch_shapes=[head2, head2, head2, head2, pltpu.VMEM((2, seq, 1), F32), pltpu.VMEM((2, seq, LANES), F32),
                        pltpu.VMEM((2, blk, LANES), F32), pltpu.VMEM((2, blk, LANES), F32)],
        compiler_params=_params(("parallel", "parallel")),
    )(qkv, qkv, qkv, dycat, o, lse, ccol)


def _adamw_math(w, g, m, v):
    m = ADAM_B1 * m + (1.0 - ADAM_B1) * g
    v = ADAM_B2 * v + (1.0 - ADAM_B2) * jnp.square(g)
    m_hat = m / ADAM_C1
    v_hat = v / ADAM_C2
    delta = -ADAM_LR * (m_hat / (jnp.sqrt(v_hat) + ADAM_EPS) + ADAM_WD * w)
    return delta, m, v


def _row_tile(rows):
    for tr in (256, 128, 64, 32, 16, 8):
        if rows % tr == 0:
            return tr
    return rows


def _sum_parts(parts, name):
    nparts, rows, cols = parts.shape
    tr = rows if rows % SUBLANES else _row_tile(rows)

    def body(p_ref, o_ref):
        g = p_ref[0].astype(F32)
        for s in range(1, nparts):
            g = g + p_ref[s].astype(F32)
        o_ref[...] = g

    return pl.pallas_call(
        body, name=name, grid=(rows // tr,),
        in_specs=[pl.BlockSpec((nparts, tr, cols), lambda i: (0, i, 0))],
        out_specs=pl.BlockSpec((tr, cols), lambda i: (i, 0)),
        out_shape=jax.ShapeDtypeStruct((rows, cols), F32),
        compiler_params=_params(("parallel",)),
    )(parts)


def _adamw_parts(parts, w, m, v, name):
    nparts, rows, cols = parts.shape
    tr = _row_tile(rows)

    def body(p_ref, w_ref, m_ref, v_ref, g_ref, d_ref, mo_ref, vo_ref):
        g = p_ref[0].astype(F32)
        for s in range(1, nparts):
            g = g + p_ref[s].astype(F32)
        delta, mn, vn = _adamw_math(w_ref[...], g, m_ref[...], v_ref[...])
        g_ref[...] = g
        d_ref[...] = delta
        mo_ref[...] = mn
        vo_ref[...] = vn

    blk = pl.BlockSpec((tr, cols), lambda i: (i, 0))
    shp = jax.ShapeDtypeStruct((rows, cols), F32)
    return pl.pallas_call(
        body, name=name, grid=(rows // tr,),
        in_specs=[pl.BlockSpec((nparts, tr, cols), lambda i: (0, i, 0)), blk, blk, blk],
        out_specs=[blk, blk, blk, blk], out_shape=[shp, shp, shp, shp],
        compiler_params=_params(("parallel",)),
    )(parts, w, m, v)


def _me():
    return lax.axis_index("x"), lax.axis_index("y"), lax.axis_index("c")


def _flip(pos, k):
    x, y, c = pos
    kx, ky, kc = (k >> 2) & 1, (k >> 1) & 1, k & 1
    return (x ^ kx if kx else x, y ^ ky if ky else y, c ^ kc if kc else c)


def _logical(pos):
    return 4 * pos[0] + 2 * pos[1] + pos[2]


N_BIG = 4


def _all_gather_big(shards):
    def body(*refs):
        x_refs, out_refs = refs[:N_BIG], refs[N_BIG:2 * N_BIG]
        send_sems, recv_sems, local_sems = refs[2 * N_BIG:]
        x, y, c = _me()
        me, sibling = (x, y, c), (x, y, 1 - c)
        chips = [(1 - x, y), (x, 1 - y), (1 - x, 1 - y)]

        def copy(a, k, block, to, src=None):
            slot = out_refs[a].at[_logical(block)]
            return pltpu.make_async_remote_copy(
                src_ref=slot if src is None else src, dst_ref=slot,
                send_sem=send_sems.at[a, k], recv_sem=recv_sems.at[a, k], device_id=to, device_id_type=MESH)

        mine = [pltpu.make_async_copy(x_refs[a], out_refs[a].at[_logical(me)], local_sems.at[a]) for a in range(N_BIG)]
        for cp in mine:
            cp.start()
        first = []
        for a in range(N_BIG):
            first.append(copy(a, 0, me, sibling, src=x_refs[a]))
            first += [copy(a, 1 + j, me, (*chip, c), src=x_refs[a]) for j, chip in enumerate(chips)]
        for cp in first:
            cp.start()
        passed = []
        for a in range(N_BIG):
            for j, chip in enumerate(chips):
                copy(a, 1 + j, (*chip, c), me).wait_recv()
                fwd = copy(a, 4 + j, (*chip, c), sibling)
                fwd.start()
                passed.append(fwd)
        for a in range(N_BIG):
            copy(a, 0, sibling, me).wait_recv()
            for j, chip in enumerate(chips):
                copy(a, 4 + j, (*chip, 1 - c), me).wait_recv()
        for cp in first + passed:
            cp.wait_send()
        for cp in mine:
            cp.wait()

    hbm = pl.BlockSpec(memory_space=pl.ANY)
    return pl.pallas_call(
        body, name="all_gather_big",
        out_shape=[jax.ShapeDtypeStruct((N_DEV,) + s.shape, s.dtype) for s in shards],
        in_specs=[hbm] * N_BIG, out_specs=[hbm] * N_BIG,
        scratch_shapes=[pltpu.SemaphoreType.DMA((N_BIG, 7)), pltpu.SemaphoreType.DMA((N_BIG, 7)),
                        pltpu.SemaphoreType.DMA((N_BIG,))],
    )(*shards)


def _all_to_all_big(parts):
    def body(*refs):
        g_refs, out_refs = refs[:N_BIG], refs[N_BIG:2 * N_BIG]
        send_sems, recv_sems, local_sems = refs[2 * N_BIG:]
        me = _me()
        mine = [pltpu.make_async_copy(g_refs[a].at[_logical(me)], out_refs[a].at[_logical(me)], local_sems.at[a])
                for a in range(N_BIG)]
        for cp in mine:
            cp.start()
        copies = []
        for a in range(N_BIG):
            for k in range(1, N_DEV):
                peer = _flip(me, k)
                copies.append(pltpu.make_async_remote_copy(
                    src_ref=g_refs[a].at[_logical(peer)], dst_ref=out_refs[a].at[_logical(me)],
                    send_sem=send_sems.at[a, k - 1], recv_sem=recv_sems.at[a, k - 1], device_id=peer, device_id_type=MESH))
        for cp in copies:
            cp.start()
        for cp in copies:
            cp.wait_recv()
        for cp in copies:
            cp.wait_send()
        for cp in mine:
            cp.wait()

    hbm = pl.BlockSpec(memory_space=pl.ANY)
    return pl.pallas_call(
        body, name="all_to_all_big",
        out_shape=[jax.ShapeDtypeStruct(p.shape, p.dtype) for p in parts],
        in_specs=[hbm] * N_BIG, out_specs=[hbm] * N_BIG,
        scratch_shapes=[pltpu.SemaphoreType.DMA((N_BIG, 7)), pltpu.SemaphoreType.DMA((N_BIG, 7)),
                        pltpu.SemaphoreType.DMA((N_BIG,))],
    )(*parts)


def _exchange_rows(x_ref, buf_ref, send_sems, recv_sems):
    me = _me()
    buf_ref[_logical(me)] = x_ref[...]
    copies = []
    for k in range(1, N_DEV):
        peer = _flip(me, k)
        copies.append(pltpu.make_async_remote_copy(
            src_ref=x_ref, dst_ref=buf_ref.at[_logical(me)],
            send_sem=send_sems.at[k - 1], recv_sem=recv_sems.at[k - 1], device_id=peer, device_id_type=MESH))
    for cp in copies:
        cp.start()
    for cp in copies:
        cp.wait_recv()
    for cp in copies:
        cp.wait_send()


def _sum_slots(buf_ref):
    total = buf_ref[0]
    for s in range(1, N_DEV):
        total = total + buf_ref[s]
    return total


def _small_gather(x):
    def body(x_ref, o_ref, buf_ref, send_sems, recv_sems):
        _exchange_rows(x_ref, buf_ref, send_sems, recv_sems)
        o_ref[...] = _sum_slots(buf_ref)

    return pl.pallas_call(
        body, name="small_gather", out_shape=jax.ShapeDtypeStruct(x.shape, F32),
        in_specs=[pl.BlockSpec(memory_space=pltpu.VMEM)], out_specs=pl.BlockSpec(memory_space=pltpu.VMEM),
        scratch_shapes=[pltpu.VMEM((N_DEV,) + x.shape, F32), pltpu.SemaphoreType.DMA((7,)), pltpu.SemaphoreType.DMA((7,))],
    )(x)


def _small_reduce_adamw(g, w, m, v):
    def body(g_ref, w_ref, m_ref, v_ref, go_ref, d_ref, mo_ref, vo_ref, buf_ref, send_sems, recv_sems):
        _exchange_rows(g_ref, buf_ref, send_sems, recv_sems)
        gs = _sum_slots(buf_ref)
        delta, mn, vn = _adamw_math(w_ref[...], gs, m_ref[...], v_ref[...])
        go_ref[...] = gs
        d_ref[...] = delta
        mo_ref[...] = mn
        vo_ref[...] = vn

    vm = pl.BlockSpec(memory_space=pltpu.VMEM)
    shp = jax.ShapeDtypeStruct(g.shape, F32)
    return pl.pallas_call(
        body, name="small_reduce_adamw", out_shape=[shp, shp, shp, shp],
        in_specs=[vm, vm, vm, vm], out_specs=[vm, vm, vm, vm],
        scratch_shapes=[pltpu.VMEM((N_DEV,) + g.shape, F32), pltpu.SemaphoreType.DMA((7,)), pltpu.SemaphoreType.DMA((7,))],
    )(g, w, m, v)


def _pad_cols(a, width):
    return jnp.pad(a, ((0, 0), (0, width - a.shape[1])))


def _local_step(x, target, norm_mix_w, w_in_t, conv_w, conv_b, dt_bias, a_log, d_skip, ssd_norm_w, f_bias,
                w_out, norm_mlp_w, w_up_t, w_down, norm_final_w):
    bsz, seq, _ = x.shape
    n = bsz * seq
    x2 = x.reshape(n, D_MODEL)
    t2 = target.reshape(n, D_MODEL)
    nfw = norm_final_w.reshape(1, D_MODEL)

    bias = _pad_cols(dt_bias, LANES)
    arow = _pad_cols(-jnp.exp(a_log), LANES)
    biast, acol = bias.reshape(LANES, 1), arow.reshape(LANES, 1)
    dfull = jnp.repeat(d_skip, HEAD_DIM, axis=1)
    fb = jnp.pad(f_bias, ((0, 0), (F_COL, LANES - 2 * F_COL)))

    wt_z, wt_xbc, wt_qkv = w_in_t[:ROW_XBC], w_in_t[ROW_XBC:ROW_DT], w_in_t[ROW_Q:ROW_F]
    wt_dtf = jnp.concatenate([w_in_t[ROW_DT:ROW_Q], w_in_t[ROW_F:], jnp.zeros((LANES - 2 * F_COL, D_MODEL), BF16)], axis=0)
    wt_q, wt_k, wt_v = wt_qkv[:D_MODEL], wt_qkv[D_MODEL:2 * D_MODEL], wt_qkv[2 * D_MODEL:]

    h1 = _rms_fwd(x2, norm_mix_w, "rms_fwd_mix")
    z = _mm([(h1, wt_z)], "inproj_z", F32, 512, 512, nt=True)
    xbc_raw = _mm([(h1, wt_xbc)], "inproj_xbc", F32, 512, 512, nt=True)
    qkv = _mm([(h1, wt_qkv)], "inproj_qkv", BF16, 512, 512, nt=True)
    dtf = _mm([(h1, wt_dtf)], "inproj_dtf", F32, 512, LANES, nt=True)
    dtft = dtf.reshape(bsz, seq, LANES).transpose(0, 2, 1)

    xbc = _conv_fwd(xbc_raw, conv_w, conv_b, bsz, seq)
    y_pre, hall = _ssd_fwd(xbc, dtf, dtft, bias, biast, arow, acol, dfull, bsz, seq)

    ccol = _fox_prep(dtf, fb, bsz, seq)
    o_att, lse = _att_fwd(qkv, ccol, bsz, seq)

    ycat = _gnorm_fwd(y_pre, z, o_att, ssd_norm_w)
    h2 = _mm([(ycat, w_out)], "outproj", F32, 512, 512, res=x2)
    h2n = _rms_fwd(h2, norm_mlp_w, "rms_fwd_mlp")
    pre = _mm([(h2n, w_up_t)], "mlp_up", F32, 512, 1024, nt=True)
    h3 = _mlp_down_fwd(pre, w_down, h2)

    dh3, loss_row, g_nfw = _loss_bwd(h3, t2, nfw)
    dpre, u = _mlp_down_bwd(dh3, w_down, pre)
    g_w_down = _mm_tn(u, dh3, "grad_w_down", 1024, 1024, 512)
    g_w_up_t = _mm_tn(dpre, h2n, "grad_w_up", 1024, 1024, 512)
    dh2n = _mm([(dpre, w_up_t)], "mlp_up_bwd", F32, 256, 512)
    dh2, g_nmlp = _rms_bwd(dh2n, h2, norm_mlp_w, dh3, "rms_bwd_mlp")

    g_w_out = _mm_tn(ycat, dh2, "grad_w_out", 1024, 1024, 512)
    dycat = _mm([(dh2, w_out)], "outproj_bwd", F32, 512, 512, nt=True)
    dy_pre, dz, g_ssdn = _gnorm_bwd(dycat, y_pre, z, ssd_norm_w)
    dq, dk, dv, dck, dcq = _att_bwd(qkv, dycat, o_att, lse, ccol, bsz, seq)

    dxbc, ddt, ssd_acc = _ssd_bwd(dy_pre, xbc, dtf, dtft, bias, biast, arow, acol, dfull, hall, bsz, seq)
    dxbc_raw, g_cw, g_cb = _conv_bwd(dxbc, xbc_raw, conv_w, conv_b, bsz, seq)

    def head_cols(vec):
        cols = vec.reshape(bsz, SSD_HEADS, seq).transpose(0, 2, 1).reshape(n, SSD_HEADS)
        return jnp.pad(cols, ((0, 0), (F_COL, LANES - 2 * F_COL)))

    ddtf, g_fb = _fox_prep_bwd(head_cols(dck), head_cols(dcq), ddt, dtf, fb, bsz, seq)

    pieces = [(dz, wt_z), (dxbc_raw, wt_xbc), (dq, wt_q), (dk, wt_k), (dv, wt_v), (ddtf, wt_dtf)]
    dh1 = _mm(pieces, "inproj_bwd", F32, 256, 512)
    dx, g_nmix = _rms_bwd(dh1, x2, norm_mix_w, dh2, "rms_bwd_mix")
    g_dtf = _mm_tn(ddtf, h1, "grad_w_in_dtf", LANES, 1024, 512)
    g_w_in_t = jnp.concatenate([
        _mm_tn(dz, h1, "grad_w_in_z", 1024, 1024, 512), _mm_tn(dxbc_raw, h1, "grad_w_in_xbc", 768, 1024, 512),
        g_dtf[:F_COL], _mm_tn(dq, h1, "grad_w_in_q", 1024, 1024, 512), _mm_tn(dk, h1, "grad_w_in_k", 1024, 1024, 512),
        _mm_tn(dv, h1, "grad_w_in_v", 1024, 1024, 512), g_dtf[F_COL:2 * F_COL]], axis=0)

    small = dict(
        norm_mix_w=g_nmix, norm_mlp_w=g_nmlp, norm_final_w=g_nfw, ssd_norm_w=g_ssdn, conv_b=g_cb, conv_w=g_cw,
        dt_bias=ssd_acc[16:17, :SSD_HEADS], a_log=ssd_acc[0:1, :SSD_HEADS], d_skip=ssd_acc[8:9, :SSD_HEADS],
        f_bias=g_fb[0:1, F_COL:2 * F_COL])
    big = dict(w_in_t=g_w_in_t, w_out=g_w_out, w_up_t=g_w_up_t, w_down=g_w_down)
    return loss_row[0, 0], dx.reshape(bsz, seq, D_MODEL), small, big


SMALL_LAYOUT = (("norm_mix_w", 1, D_MODEL), ("norm_mlp_w", 1, D_MODEL), ("norm_final_w", 1, D_MODEL),
                ("ssd_norm_w", 1, D_MODEL), ("conv_b", 1, CONV_CH), ("conv_w", CONV_K, CONV_CH),
                ("dt_bias", 1, SSD_HEADS), ("a_log", 1, SSD_HEADS), ("d_skip", 1, SSD_HEADS), ("f_bias", 1, SSD_HEADS))


def _pack_small(vals):
    tiles = []
    for name, nrows, width in SMALL_LAYOUT:
        tiles.append(jnp.pad(vals[name].reshape(nrows, width), ((0, SUBLANES - nrows), (0, SMALL_COLS - width))))
    return jnp.concatenate(tiles, axis=0)


def _unpack_small(packed, like):
    out = {}
    for i, (name, nrows, width) in enumerate(SMALL_LAYOUT):
        out[name] = packed[SUBLANES * i:SUBLANES * i + nrows, :width].reshape(like[name].shape)
    return out


def kernel(x, norm_mix_w, w_in, conv_w, conv_b, dt_bias, a_log, d_skip, ssd_norm_w, f_bias, w_out, norm_mlp_w, w_up, w_down, norm_final_w, loss_target, m_norm_mix_w, m_w_in, m_conv_w, m_conv_b, m_dt_bias, m_a_log, m_d_skip, m_ssd_norm_w, m_f_bias, m_w_out, m_norm_mlp_w, m_w_up, m_w_down, m_norm_final_w, v_norm_mix_w, v_w_in, v_conv_w, v_conv_b, v_dt_bias, v_a_log, v_d_skip, v_ssd_norm_w, v_f_bias, v_w_out, v_norm_mlp_w, v_w_up, v_w_down, v_norm_final_w):
    me = 4 * lax.axis_index("x") + 2 * lax.axis_index("y") + lax.axis_index("c")
    conv_shard_w = CONV_CH // N_DEV
    zero = jnp.zeros((), jnp.int32)

    def place_conv(shard):
        return lax.dynamic_update_slice(jnp.zeros((CONV_K, CONV_CH), F32), shard[0], (zero, me * conv_shard_w))

    shards = [w_in[0].T.astype(BF16), w_out[0].astype(BF16), w_up[0].T.astype(BF16), w_down[0].astype(BF16)]
    g_in_t, g_out, g_up_t, g_down = _all_gather_big(shards)
    conv_full = _small_gather(jnp.pad(place_conv(conv_w), ((0, SUBLANES - CONV_K), (0, 0))))[:CONV_K]

    loss_local, grad_x, g_small, g_big = _local_step(
        x, loss_target, norm_mix_w, g_in_t.reshape(IN_WIDTH, D_MODEL), conv_full, conv_b, dt_bias, a_log, d_skip,
        ssd_norm_w, f_bias, g_out.reshape(MIX_WIDTH, D_MODEL), norm_mlp_w, g_up_t.reshape(D_FF, D_MODEL),
        g_down.reshape(D_FF, D_MODEL), norm_final_w)
    loss = lax.psum(loss_local, MESH_AXES)

    recv_in_t, recv_out, recv_up_t, recv_down = _all_to_all_big([
        g_big["w_in_t"].reshape(N_DEV, IN_WIDTH // N_DEV, D_MODEL), g_big["w_out"].reshape(N_DEV, MIX_WIDTH // N_DEV, D_MODEL),
        g_big["w_up_t"].reshape(N_DEV, D_FF // N_DEV, D_MODEL), g_big["w_down"].reshape(N_DEV, D_FF // N_DEV, D_MODEL)])
    grad_in = _sum_parts(recv_in_t, "sum_w_in").T[None]
    grad_up = _sum_parts(recv_up_t, "sum_w_up").T[None]
    big = {
        "w_in": _adamw_parts(grad_in, w_in[0], m_w_in[0], v_w_in[0], "adamw_w_in"),
        "w_out": _adamw_parts(recv_out, w_out[0], m_w_out[0], v_w_out[0], "adamw_w_out"),
        "w_up": _adamw_parts(grad_up, w_up[0], m_w_up[0], v_w_up[0], "adamw_w_up"),
        "w_down": _adamw_parts(recv_down, w_down[0], m_w_down[0], v_w_down[0], "adamw_w_down"),
    }

    like = dict(norm_mix_w=norm_mix_w, norm_mlp_w=norm_mlp_w, norm_final_w=norm_final_w, ssd_norm_w=ssd_norm_w,
                conv_b=conv_b, conv_w=jnp.zeros((1, CONV_K, CONV_CH), F32), dt_bias=dt_bias, a_log=a_log,
                d_skip=d_skip, f_bias=f_bias)
    w_small = dict(like, conv_w=place_conv(conv_w))
    m_small = dict(norm_mix_w=m_norm_mix_w, norm_mlp_w=m_norm_mlp_w, norm_final_w=m_norm_final_w,
                   ssd_norm_w=m_ssd_norm_w, conv_b=m_conv_b, conv_w=place_conv(m_conv_w), dt_bias=m_dt_bias,
                   a_log=m_a_log, d_skip=m_d_skip, f_bias=m_f_bias)
    v_small = dict(norm_mix_w=v_norm_mix_w, norm_mlp_w=v_norm_mlp_w, norm_final_w=v_norm_final_w,
                   ssd_norm_w=v_ssd_norm_w, conv_b=v_conv_b, conv_w=place_conv(v_conv_w), dt_bias=v_dt_bias,
                   a_log=v_a_log, d_skip=v_d_skip, f_bias=v_f_bias)
    small = _small_reduce_adamw(_pack_small(g_small), _pack_small(w_small), _pack_small(m_small), _pack_small(v_small))
    small = [_unpack_small(s, like) for s in small]
    for s in small:
        s["conv_w"] = lax.dynamic_slice(s["conv_w"], (zero, zero, me * conv_shard_w), (1, CONV_K, conv_shard_w))

    order = ["norm_mix_w", "w_in", "conv_w", "conv_b", "dt_bias", "a_log", "d_skip", "ssd_norm_w", "f_bias", "w_out",
             "norm_mlp_w", "w_up", "w_down", "norm_final_w"]
    outs = [loss, grad_x]
    for kind in range(4):
        for name in order:
            outs.append(big[name][kind][None] if name in big else small[kind][name])
    return tuple(outs)
```

```python
import jax
import jax.numpy as jnp
from jax import lax
from jax.experimental import pallas as pl
from jax.experimental.pallas import tpu as pltpu

F32, BF16 = jnp.float32, jnp.bfloat16
HI = lax.Precision.HIGHEST

D_MODEL = 1024
SSD_HEADS = 16
HEAD_DIM = 64
SSD_STATE = 128
CHUNK = 128
CONV_CH = 1536
CONV_K = 4
D_FF = 4096
MIX_WIDTH = 2048
IN_WIDTH = 5664
NORM_EPS = 1e-5
ATT_SCALE = 0.125

LANES = 128
SUBLANES = 8
HEAD_PAIRS = SSD_HEADS // 2
NEG = -1e30
VMEM_LIMIT = 52 * 1024 * 1024

ROW_XBC, ROW_DT, ROW_Q, ROW_F = 1024, 2560, 2576, 5648
F_COL = SSD_HEADS

N_DEV = 8
MESH_AXES = ("x", "y", "c")
MESH = pl.DeviceIdType.MESH

ADAM_LR, ADAM_B1, ADAM_B2, ADAM_EPS, ADAM_WD, ADAM_STEP = 0.001, 0.9, 0.999, 1e-08, 0.01, 10
ADAM_C1 = 1.0 - ADAM_B1 ** ADAM_STEP
ADAM_C2 = 1.0 - ADAM_B2 ** ADAM_STEP

SMALL_COLS = CONV_CH


def _params(sem=None):
    return pltpu.CompilerParams(dimension_semantics=sem, vmem_limit_bytes=VMEM_LIMIT)


def _sigmoid(u):
    return 1.0 / (1.0 + jnp.exp(-u))


def _softplus(u):
    return jnp.maximum(u, 0.0) + jnp.log(1.0 + jnp.exp(-jnp.abs(u)))


def _log_sigmoid(u):
    return jnp.minimum(u, 0.0) - jnp.log(1.0 + jnp.exp(-jnp.abs(u)))


def _bdot(a, b):
    return jnp.dot(a.astype(BF16), b.astype(BF16), preferred_element_type=F32)


def _bdot_nt(a, b):
    return lax.dot_general(a.astype(BF16), b.astype(BF16), (((1,), (1,)), ((), ())), preferred_element_type=F32)


def _bdot_tn(a, b):
    return lax.dot_general(a.astype(BF16), b.astype(BF16), (((0,), (0,)), ((), ())), preferred_element_type=F32)


def _hdot(a, b):
    return jnp.dot(a, b, precision=HI, preferred_element_type=F32)


def _iota(shape, dim):
    return lax.broadcasted_iota(jnp.int32, shape, dim)


def _head_expand():
    return (jnp.right_shift(_iota((LANES, D_MODEL), 1), 6) == _iota((LANES, D_MODEL), 0)).astype(F32)


def _head_reduce():
    return (jnp.right_shift(_iota((D_MODEL, LANES), 0), 6) == _iota((D_MODEL, LANES), 1)).astype(F32)


def _rms_fwd(x, w, name):
    n, d = x.shape
    tm = min(512, n)

    def body(x_ref, w_ref, o_ref):
        xv = x_ref[...]
        r = lax.rsqrt(jnp.mean(xv * xv, axis=-1, keepdims=True) + NORM_EPS)
        o_ref[...] = (xv * r * w_ref[...]).astype(BF16)

    return pl.pallas_call(
        body, name=name, grid=(n // tm,),
        in_specs=[pl.BlockSpec((tm, d), lambda i: (i, 0)), pl.BlockSpec((1, d), lambda i: (0, 0))],
        out_specs=pl.BlockSpec((tm, d), lambda i: (i, 0)),
        out_shape=jax.ShapeDtypeStruct((n, d), BF16),
        compiler_params=_params(("parallel",)),
    )(x, w)


def _rms_bwd(dn, x, w, dres, name):
    n, d = x.shape
    tm = min(256, n)

    def body(dn_ref, x_ref, w_ref, dres_ref, dx_ref, gw_ref):
        xv = x_ref[...]
        r = lax.rsqrt(jnp.mean(xv * xv, axis=-1, keepdims=True) + NORM_EPS)
        nrm = xv * r
        dnv = dn_ref[...]
        g = dnv * w_ref[...]
        dx_ref[...] = dres_ref[...] + r * (g - nrm * jnp.mean(g * nrm, axis=-1, keepdims=True))

        @pl.when(pl.program_id(0) == 0)
        def _():
            gw_ref[...] = jnp.zeros_like(gw_ref)

        gw_ref[...] += jnp.sum(dnv * nrm, axis=0, keepdims=True)

    tok = pl.BlockSpec((tm, d), lambda i: (i, 0))
    row = pl.BlockSpec((1, d), lambda i: (0, 0))
    return pl.pallas_call(
        body, name=name, grid=(n // tm,),
        in_specs=[tok, tok, row, tok], out_specs=[tok, row],
        out_shape=[jax.ShapeDtypeStruct((n, d), F32), jax.ShapeDtypeStruct((1, d), F32)],
        compiler_params=_params(("arbitrary",)),
    )(dn, x, w, dres)


def _mm(pairs, name, out_dtype, tm, tn, nt=False, res=None, exchange=None):
    m = pairs[0][0].shape[0]
    n = pairs[0][1].shape[0 if nt else 1]
    tm, tn = min(tm, m), min(tn, n)
    gm, gn = m // tm, n // tn
    npairs = len(pairs)
    n_in = 2 * npairs + (res is not None)
    kind, xs = (None, []) if exchange is None else exchange
    nx = len(xs)

    def body(*refs):
        x_refs, o_ref = refs[n_in:n_in + nx], refs[n_in + nx]
        xo_refs, sems = refs[n_in + nx + 1:n_in + 2 * nx + 1], refs[n_in + 2 * nx + 1:]
        i, j = pl.program_id(0), pl.program_id(1)
        if nx:
            @pl.when((i == 0) & (j == 0))
            def _():
                _exchange_start(kind, x_refs, xo_refs, *sems)

        acc = None if res is None else refs[2 * npairs][...]
        for p in range(npairs):
            a_v, b_v = refs[2 * p][...], refs[2 * p + 1][...]
            d = _bdot_nt(a_v, b_v) if nt else _bdot(a_v, b_v)
            acc = d if acc is None else acc + d
        o_ref[...] = acc.astype(o_ref.dtype)
        if nx:
            @pl.when((i == gm - 1) & (j == gn - 1))
            def _():
                _exchange_wait(kind, x_refs, xo_refs, *sems)

    in_specs, args = [], []
    for a, b in pairs:
        k = a.shape[1]
        in_specs.append(pl.BlockSpec((tm, k), lambda i, j: (i, 0)))
        in_specs.append(pl.BlockSpec((tn, k), lambda i, j: (j, 0)) if nt else pl.BlockSpec((k, tn), lambda i, j: (0, j)))
        args += [a, b]
    if res is not None:
        in_specs.append(pl.BlockSpec((tm, tn), lambda i, j: (i, j)))
        args.append(res)
    hbm = pl.BlockSpec(memory_space=pl.ANY)
    outs = pl.pallas_call(
        body, name=name, grid=(gm, gn), in_specs=in_specs + [hbm] * nx,
        out_specs=[pl.BlockSpec((tm, tn), lambda i, j: (i, j))] + [hbm] * nx,
        out_shape=[jax.ShapeDtypeStruct((m, n), out_dtype)] + _exchange_shapes(kind, xs),
        scratch_shapes=_exchange_scratch(nx),
        compiler_params=_params(("arbitrary", "arbitrary") if nx else ("parallel", "parallel")),
    )(*args, *xs)
    return outs if nx else outs[0]


def _mm_tn(a, b, name, tm, tn, tk):
    t, m = a.shape
    n = b.shape[1]
    tm, tn, tk = min(tm, m), min(tn, n), min(tk, t)
    nk = t // tk

    def body(a_ref, b_ref, o_ref, acc_ref):
        kk = pl.program_id(2)

        @pl.when(kk == 0)
        def _():
            acc_ref[...] = jnp.zeros_like(acc_ref)

        acc_ref[...] += _bdot_tn(a_ref[...], b_ref[...])

        @pl.when(kk == nk - 1)
        def _():
            o_ref[...] = acc_ref[...].astype(o_ref.dtype)

    return pl.pallas_call(
        body, name=name, grid=(m // tm, n // tn, nk),
        in_specs=[pl.BlockSpec((tk, tm), lambda i, j, kk: (kk, i)), pl.BlockSpec((tk, tn), lambda i, j, kk: (kk, j))],
        out_specs=pl.BlockSpec((tm, tn), lambda i, j, kk: (i, j)),
        out_shape=jax.ShapeDtypeStruct((m, n), BF16),
        scratch_shapes=[pltpu.VMEM((tm, tn), F32)],
        compiler_params=_params(("parallel", "parallel", "arbitrary")),
    )(a, b)


def _mlp_down_fwd(pre, w_down, h2):
    m, k = pre.shape
    n = w_down.shape[1]
    tm, tn = min(256, m), min(512, n)

    def body(p_ref, b_ref, r_ref, o_ref):
        u = jnp.square(jnp.maximum(p_ref[...], 0.0))
        o_ref[...] = r_ref[...] + _bdot(u, b_ref[...])

    return pl.pallas_call(
        body, name="mlp_down_fwd", grid=(m // tm, n // tn),
        in_specs=[pl.BlockSpec((tm, k), lambda i, j: (i, 0)), pl.BlockSpec((k, tn), lambda i, j: (0, j)),
                  pl.BlockSpec((tm, tn), lambda i, j: (i, j))],
        out_specs=pl.BlockSpec((tm, tn), lambda i, j: (i, j)),
        out_shape=jax.ShapeDtypeStruct((m, n), F32),
        compiler_params=_params(("parallel", "parallel")),
    )(pre, w_down, h2)


def _mlp_down_bwd(dh3, w_down, pre):
    m, k = dh3.shape
    n = w_down.shape[0]
    tm, tn = min(512, m), min(1024, n)

    def body(a_ref, b_ref, p_ref, dpre_ref, u_ref):
        r = jnp.maximum(p_ref[...], 0.0)
        du = _bdot_nt(a_ref[...], b_ref[...])
        dpre_ref[...] = (du * (2.0 * r)).astype(BF16)
        u_ref[...] = (r * r).astype(BF16)

    blk = pl.BlockSpec((tm, tn), lambda i, j: (i, j))
    return pl.pallas_call(
        body, name="mlp_down_bwd", grid=(m // tm, n // tn),
        in_specs=[pl.BlockSpec((tm, k), lambda i, j: (i, 0)), pl.BlockSpec((tn, k), lambda i, j: (j, 0)), blk],
        out_specs=[blk, blk],
        out_shape=[jax.ShapeDtypeStruct((m, n), BF16), jax.ShapeDtypeStruct((m, n), BF16)],
        compiler_params=_params(("parallel", "parallel")),
    )(dh3, w_down, pre)


def _loss_bwd(h3, target, w):
    n, d = h3.shape
    tm = min(256, n)

    def body(h_ref, t_ref, w_ref, dh_ref, loss_ref, gw_ref):
        xv = h_ref[...]
        r = lax.rsqrt(jnp.mean(xv * xv, axis=-1, keepdims=True) + NORM_EPS)
        nrm = xv * r
        wv = w_ref[...]
        err = nrm * wv - t_ref[...]
        part = 0.5 * jnp.sum(jnp.mean(err * err, axis=-1, keepdims=True), axis=0, keepdims=True)
        dy = err * (1.0 / d)
        g = dy * wv
        dh_ref[...] = r * (g - nrm * jnp.mean(g * nrm, axis=-1, keepdims=True))

        @pl.when(pl.program_id(0) == 0)
        def _():
            loss_ref[...] = jnp.zeros_like(loss_ref)
            gw_ref[...] = jnp.zeros_like(gw_ref)

        loss_ref[...] += jnp.broadcast_to(part, loss_ref.shape)
        gw_ref[...] += jnp.sum(dy * nrm, axis=0, keepdims=True)

    tok = pl.BlockSpec((tm, d), lambda i: (i, 0))
    row = pl.BlockSpec((1, d), lambda i: (0, 0))
    return pl.pallas_call(
        body, name="loss_bwd", grid=(n // tm,),
        in_specs=[tok, tok, row], out_specs=[tok, pl.BlockSpec((1, LANES), lambda i: (0, 0)), row],
        out_shape=[jax.ShapeDtypeStruct((n, d), F32), jax.ShapeDtypeStruct((1, LANES), F32),
                   jax.ShapeDtypeStruct((1, d), F32)],
        compiler_params=_params(("arbitrary",)),
    )(h3, target, w)


CONV_TC = 512


def _conv_fwd(xbc, cw, cb, bsz, seq):
    tt = min(256, seq)
    nt, hb = seq // tt, tt // SUBLANES
    x3 = xbc.reshape(bsz, seq, CONV_CH)

    def body(xm_ref, xh_ref, w_ref, b_ref, o_ref):
        i = pl.program_id(2)
        x = xm_ref[0]
        halo = jnp.where(i > 0, xh_ref[0], 0.0)
        xx = jnp.concatenate([halo, x], axis=0)
        acc = x * w_ref[3:4, :] + b_ref[...]
        for s in range(1, CONV_K):
            acc = acc + pltpu.roll(xx, s, 0)[SUBLANES:, :] * w_ref[3 - s:4 - s, :]
        o_ref[0] = acc * _sigmoid(acc)

    out = pl.pallas_call(
        body, name="conv_fwd", grid=(CONV_CH // CONV_TC, bsz, nt),
        in_specs=[pl.BlockSpec((1, tt, CONV_TC), lambda c, b, i: (b, i, c)),
                  pl.BlockSpec((1, SUBLANES, CONV_TC), lambda c, b, i: (b, jnp.maximum(i * hb - 1, 0), c)),
                  pl.BlockSpec((CONV_K, CONV_TC), lambda c, b, i: (0, c)),
                  pl.BlockSpec((1, CONV_TC), lambda c, b, i: (0, c))],
        out_specs=pl.BlockSpec((1, tt, CONV_TC), lambda c, b, i: (b, i, c)),
        out_shape=jax.ShapeDtypeStruct((bsz, seq, CONV_CH), F32),
        compiler_params=_params(("parallel", "parallel", "parallel")),
    )(x3, x3, cw, cb)
    return out.reshape(bsz * seq, CONV_CH)


def _conv_bwd(da, xbc, cw, cb, bsz, seq):
    tt = min(256, seq)
    nt, hb = seq // tt, tt // SUBLANES
    x3 = xbc.reshape(bsz, seq, CONV_CH)
    da3 = da.reshape(bsz, seq, CONV_CH)
    n2 = tt + SUBLANES

    def body(dam_ref, daa_ref, xm_ref, xb_ref, xa_ref, w_ref, b_ref, du_ref, gw_ref, gb_ref):
        bi, i = pl.program_id(1), pl.program_id(2)
        before = jnp.where(i > 0, xb_ref[0], 0.0)
        after = jnp.where(i < nt - 1, xa_ref[0], 0.0)
        xx = jnp.concatenate([before, xm_ref[0], after], axis=0)
        rolled = [xx] + [pltpu.roll(xx, s, 0) for s in range(1, CONV_K)]
        pre = b_ref[...]
        for s in range(CONV_K):
            pre = pre + rolled[s][SUBLANES:, :] * w_ref[3 - s:4 - s, :]
        da_ext = jnp.concatenate([dam_ref[0], jnp.where(i < nt - 1, daa_ref[0], 0.0)], axis=0)
        sg = _sigmoid(pre)
        dpre = da_ext * sg * (1.0 + pre * (1.0 - sg))
        du = dpre[:tt, :] * w_ref[3:4, :]
        for s in range(1, CONV_K):
            du = du + pltpu.roll(dpre, n2 - s, 0)[:tt, :] * w_ref[3 - s:4 - s, :]
        du_ref[0] = du.astype(BF16)
        dpm = dpre[:tt, :]
        gws = [jnp.sum(dpm * rolled[3 - kk][SUBLANES:SUBLANES + tt, :], axis=0, keepdims=True) for kk in range(CONV_K)]

        @pl.when((bi == 0) & (i == 0))
        def _():
            gw_ref[...] = jnp.zeros_like(gw_ref)
            gb_ref[...] = jnp.zeros_like(gb_ref)

        gw_ref[...] += jnp.concatenate(gws, axis=0)
        gb_ref[...] += jnp.sum(dpm, axis=0, keepdims=True)

    main = pl.BlockSpec((1, tt, CONV_TC), lambda c, b, i: (b, i, c))
    prev = pl.BlockSpec((1, SUBLANES, CONV_TC), lambda c, b, i: (b, jnp.maximum(i * hb - 1, 0), c))
    nxt = pl.BlockSpec((1, SUBLANES, CONV_TC), lambda c, b, i: (b, jnp.minimum((i + 1) * hb, seq // SUBLANES - 1), c))
    du, gw, gb = pl.pallas_call(
        body, name="conv_bwd", grid=(CONV_CH // CONV_TC, bsz, nt),
        in_specs=[main, nxt, main, prev, nxt,
                  pl.BlockSpec((CONV_K, CONV_TC), lambda c, b, i: (0, c)),
                  pl.BlockSpec((1, CONV_TC), lambda c, b, i: (0, c))],
        out_specs=[main, pl.BlockSpec((CONV_K, CONV_TC), lambda c, b, i: (0, c)),
                   pl.BlockSpec((1, CONV_TC), lambda c, b, i: (0, c))],
        out_shape=[jax.ShapeDtypeStruct((bsz, seq, CONV_CH), BF16), jax.ShapeDtypeStruct((CONV_K, CONV_CH), F32),
                   jax.ShapeDtypeStruct((1, CONV_CH), F32)],
        compiler_params=_params(("parallel", "arbitrary", "arbitrary")),
    )(da3, da3, x3, x3, x3, cw, cb)
    return du.reshape(bsz * seq, CONV_CH), gw, gb


def _ssd_prologue(dtf_ref, dtft_ref, bias_ref, biast_ref, arow_ref, acol_ref, dtfull_scr, acfull_scr, acr_scr):
    tri = (_iota((CHUNK, CHUNK), 1) <= _iota((CHUNK, CHUNK), 0)).astype(F32)
    triu = (_iota((CHUNK, CHUNK), 0) <= _iota((CHUNK, CHUNK), 1)).astype(F32)
    dtc = _softplus(dtf_ref[0] + bias_ref[...])
    a = dtc * arow_ref[...]
    acum = _hdot(tri, a)
    expand = _head_expand()
    dtfull_scr[...] = _hdot(dtc, expand)
    acfull_scr[...] = _hdot(acum, expand)
    dtr = _softplus(dtft_ref[0] + biast_ref[...])
    acr_scr[...] = _hdot(dtr * acol_ref[...], triu)
    return dtc, acum


def _decay_matrix(acum, acr_scr, h):
    lane = _iota((CHUNK, LANES), 1)
    ac_col = jnp.sum(jnp.where(lane == h, acum, 0.0), axis=1, keepdims=True)
    ac_row = acr_scr[h:h + 1, :]
    causal = _iota((CHUNK, CHUNK), 1) <= _iota((CHUNK, CHUNK), 0)
    return jnp.exp(jnp.where(causal, ac_col - ac_row, NEG))


def _ssd_fwd(xbc, dtf, dtft, bias, biast, arow, acol, dfull, bsz, seq):
    nc = seq // CHUNK
    x3 = xbc.reshape(bsz, seq, CONV_CH)

    def body(xbc_ref, dtf_ref, dtft_ref, bias_ref, biast_ref, arow_ref, acol_ref, dfull_ref,
             y_ref, hall_ref, h_scr, dtfull_scr, acfull_scr, acr_scr):
        @pl.when(pl.program_id(1) == 0)
        def _():
            h_scr[...] = jnp.zeros_like(h_scr)

        _, acum = _ssd_prologue(dtf_ref, dtft_ref, bias_ref, biast_ref, arow_ref, acol_ref,
                                dtfull_scr, acfull_scr, acr_scr)
        lane = _iota((CHUNK, LANES), 1)
        for g in range(2):
            bg = xbc_ref[0, :, D_MODEL + LANES * g:D_MODEL + LANES * (g + 1)]
            cg = xbc_ref[0, :, D_MODEL + 2 * LANES + LANES * g:D_MODEL + 2 * LANES + LANES * (g + 1)]
            cg_bf = cg.astype(BF16)
            gmat = _bdot_nt(cg_bf, bg)
            bgt_bf = bg.T.astype(BF16)
            for j in range(4):
                p = 4 * g + j
                sl = slice(LANES * p, LANES * (p + 1))
                xs = xbc_ref[0, :, sl]
                ac = acfull_scr[:, sl]
                acl = acfull_scr[CHUNK - 1:CHUNK, sl]
                xdt = xs * dtfull_scr[:, sl]
                xdt_bf = xdt.astype(BF16)
                hp = h_scr[p]
                hall_ref[0, 0, p] = hp
                yo = _bdot(cg_bf, hp) * jnp.exp(ac)
                yd = []
                for hh in range(2):
                    mmat = gmat * _decay_matrix(acum, acr_scr, 2 * p + hh)
                    yd.append(_bdot(mmat, xdt_bf))
                y_ref[0, :, sl] = jnp.where(lane < HEAD_DIM, yd[0], yd[1]) + yo + dfull_ref[:, sl] * xs
                h_scr[p] = hp * jnp.exp(acl) + _bdot(bgt_bf, xdt * jnp.exp(acl - ac))

    row = lambda w: pl.BlockSpec((1, w), lambda b, c: (0, 0))
    y, hall = pl.pallas_call(
        body, name="ssd_fwd", grid=(bsz, nc),
        in_specs=[pl.BlockSpec((1, CHUNK, CONV_CH), lambda b, c: (b, c, 0)),
                  pl.BlockSpec((1, CHUNK, LANES), lambda b, c: (b, c, 0)),
                  pl.BlockSpec((1, LANES, CHUNK), lambda b, c: (b, 0, c)),
                  row(LANES), pl.BlockSpec((LANES, 1), lambda b, c: (0, 0)),
                  row(LANES), pl.BlockSpec((LANES, 1), lambda b, c: (0, 0)), row(D_MODEL)],
        out_specs=[pl.BlockSpec((1, CHUNK, D_MODEL), lambda b, c: (b, c, 0)),
                   pl.BlockSpec((1, 1, HEAD_PAIRS, SSD_STATE, LANES), lambda b, c: (b, c, 0, 0, 0))],
        out_shape=[jax.ShapeDtypeStruct((bsz, seq, D_MODEL), F32),
                   jax.ShapeDtypeStruct((bsz, nc, HEAD_PAIRS, SSD_STATE, LANES), F32)],
        scratch_shapes=[pltpu.VMEM((HEAD_PAIRS, SSD_STATE, LANES), F32), pltpu.VMEM((CHUNK, D_MODEL), F32),
                        pltpu.VMEM((CHUNK, D_MODEL), F32), pltpu.VMEM((LANES, CHUNK), F32)],
        compiler_params=_params(("parallel", "arbitrary")),
    )(x3, dtf.reshape(bsz, seq, LANES), dtft, bias, biast, arow, acol, dfull)
    return y.reshape(bsz * seq, D_MODEL), hall


def _ssd_bwd(dy, xbc, dtf, dtft, bias, biast, arow, acol, dfull, hall, bsz, seq):
    nc = seq // CHUNK
    x3 = xbc.reshape(bsz, seq, CONV_CH)
    dy3 = dy.reshape(bsz, seq, D_MODEL)

    def body(dy_ref, xbc_ref, dtf_ref, dtft_ref, bias_ref, biast_ref, arow_ref, acol_ref, dfull_ref, hall_ref,
             dx_ref, ddt_ref, acc_ref, dh_scr, dtfull_scr, acfull_scr, acr_scr, csum_scr, dacf_scr, ddtf_scr, ddl_scr):
        first = (pl.program_id(0) == 0) & (pl.program_id(1) == 0)

        @pl.when(first)
        def _():
            acc_ref[...] = jnp.zeros_like(acc_ref)

        @pl.when(pl.program_id(1) == 0)
        def _():
            dh_scr[...] = jnp.zeros_like(dh_scr)

        dtc, acum = _ssd_prologue(dtf_ref, dtft_ref, bias_ref, biast_ref, arow_ref, acol_ref,
                                  dtfull_scr, acfull_scr, acr_scr)
        csum_scr[...] = jnp.zeros_like(csum_scr)
        lane = _iota((CHUNK, LANES), 1)
        last_row = _iota((CHUNK, LANES), 0) == CHUNK - 1
        rs16 = jnp.zeros((CHUNK, LANES), F32)
        for g in range(2):
            bsl = slice(D_MODEL + LANES * g, D_MODEL + LANES * (g + 1))
            csl = slice(D_MODEL + 2 * LANES + LANES * g, D_MODEL + 2 * LANES + LANES * (g + 1))
            bg_bf = xbc_ref[0, :, bsl].astype(BF16)
            cg = xbc_ref[0, :, csl]
            cg_bf = cg.astype(BF16)
            cgt_bf = cg.T.astype(BF16)
            gmat = _bdot_nt(cg_bf, bg_bf)
            dg = jnp.zeros((CHUNK, CHUNK), F32)
            dbg = jnp.zeros((CHUNK, SSD_STATE), F32)
            dcg = jnp.zeros((CHUNK, SSD_STATE), F32)
            for j in range(4):
                p = 4 * g + j
                sl = slice(LANES * p, LANES * (p + 1))
                xs = xbc_ref[0, :, sl]
                dt = dtfull_scr[:, sl]
                ac = acfull_scr[:, sl]
                acl = acfull_scr[CHUNK - 1:CHUNK, sl]
                dyp = dy_ref[0, :, sl]
                xdt = xs * dt
                xdt_bf = xdt.astype(BF16)
                e = jnp.exp(ac)
                dsd = jnp.exp(acl - ac)
                cd = jnp.exp(acl)
                xds_bf = (xdt * dsd).astype(BF16)
                hp = hall_ref[0, 0, p]
                hp_bf = hp.astype(BF16)
                dhn = dh_scr[p]
                dhn_bf = dhn.astype(BF16)
                yo = _bdot(cg_bf, hp_bf) * e
                dw_bf = (dyp * e).astype(BF16)
                dcg = dcg + _bdot_nt(dw_bf, hp_bf)
                dhin = _bdot(cgt_bf, dw_bf)
                dac = dyp * yo
                dacl = jnp.sum(dhn * hp, axis=0, keepdims=True) * cd
                dxds = _bdot(bg_bf, dhn_bf)
                dbg = dbg + _bdot_nt(xds_bf, dhn_bf)
                dxdt = dxds * dsd
                tdec = dxds * xdt * dsd
                dacl = dacl + jnp.sum(tdec, axis=0, keepdims=True)
                dac = dac - tdec
                dh_scr[p] = dhn * cd + dhin
                for hh in range(2):
                    h = 2 * p + hh
                    lm = (lane < HEAD_DIM) if hh == 0 else (lane >= HEAD_DIM)
                    dec = _decay_matrix(acum, acr_scr, h)
                    mmat = gmat * dec
                    dyh_bf = jnp.where(lm, dyp, 0.0).astype(BF16)
                    dm = _bdot_nt(dyh_bf, xdt_bf)
                    dxdt = dxdt + _bdot(mmat.T, dyh_bf)
                    dg = dg + dm * dec
                    q = dm * mmat
                    rs16 = rs16 + jnp.where(lane == h, jnp.sum(q, axis=1, keepdims=True), 0.0)
                    csum_scr[h:h + 1, :] = jnp.sum(q, axis=0, keepdims=True)
                dx_ref[0, :, sl] = dfull_ref[:, sl] * dyp + dxdt * dt
                dacf_scr[:, sl] = dac + jnp.where(last_row, dacl, 0.0)
                ddtf_scr[:, sl] = dxdt * xs
                ddl_scr[:, sl] = jnp.broadcast_to(jnp.sum(dyp * xs, axis=0, keepdims=True), (SUBLANES, LANES))
            dg_bf = dg.astype(BF16)
            dx_ref[0, :, bsl] = dbg + _bdot(dg.T, cg_bf)
            dx_ref[0, :, csl] = dcg + _bdot(dg_bf, bg_bf)
        reduce = _head_reduce()
        triu = (_iota((CHUNK, CHUNK), 0) <= _iota((CHUNK, CHUNK), 1)).astype(F32)
        dac16 = _hdot(dacf_scr[...], reduce) + rs16 - csum_scr[...].T
        da16 = _hdot(triu, dac16)
        ddt16 = da16 * arow_ref[...] + _hdot(ddtf_scr[...], reduce)
        ddtraw = ddt16 * _sigmoid(dtf_ref[0] + bias_ref[...])
        ddt_ref[0] = ddtraw
        acc_ref[0:8, :] += jnp.broadcast_to(jnp.sum(da16 * dtc * arow_ref[...], axis=0, keepdims=True), (SUBLANES, LANES))
        acc_ref[8:16, :] += _hdot(ddl_scr[...], reduce)
        acc_ref[16:24, :] += jnp.broadcast_to(jnp.sum(ddtraw, axis=0, keepdims=True), (SUBLANES, LANES))

    rev = lambda b, c: (b, nc - 1 - c, 0)
    row = lambda w: pl.BlockSpec((1, w), lambda b, c: (0, 0))
    dx, ddt, acc = pl.pallas_call(
        body, name="ssd_bwd", grid=(bsz, nc),
        in_specs=[pl.BlockSpec((1, CHUNK, D_MODEL), rev), pl.BlockSpec((1, CHUNK, CONV_CH), rev),
                  pl.BlockSpec((1, CHUNK, LANES), rev),
                  pl.BlockSpec((1, LANES, CHUNK), lambda b, c: (b, 0, nc - 1 - c)),
                  row(LANES), pl.BlockSpec((LANES, 1), lambda b, c: (0, 0)),
                  row(LANES), pl.BlockSpec((LANES, 1), lambda b, c: (0, 0)), row(D_MODEL),
                  pl.BlockSpec((1, 1, HEAD_PAIRS, SSD_STATE, LANES), lambda b, c: (b, nc - 1 - c, 0, 0, 0))],
        out_specs=[pl.BlockSpec((1, CHUNK, CONV_CH), rev), pl.BlockSpec((1, CHUNK, LANES), rev),
                   pl.BlockSpec((3 * SUBLANES, LANES), lambda b, c: (0, 0))],
        out_shape=[jax.ShapeDtypeStruct((bsz, seq, CONV_CH), F32), jax.ShapeDtypeStruct((bsz, seq, LANES), F32),
                   jax.ShapeDtypeStruct((3 * SUBLANES, LANES), F32)],
        scratch_shapes=[pltpu.VMEM((HEAD_PAIRS, SSD_STATE, LANES), F32), pltpu.VMEM((CHUNK, D_MODEL), F32),
                        pltpu.VMEM((CHUNK, D_MODEL), F32), pltpu.VMEM((LANES, CHUNK), F32),
                        pltpu.VMEM((LANES, CHUNK), F32), pltpu.VMEM((CHUNK, D_MODEL), F32),
                        pltpu.VMEM((CHUNK, D_MODEL), F32), pltpu.VMEM((SUBLANES, D_MODEL), F32)],
        compiler_params=_params(("arbitrary", "arbitrary")),
    )(dy3, x3, dtf.reshape(bsz, seq, LANES), dtft, bias, biast, arow, acol, dfull, hall)
    return dx.reshape(bsz * seq, CONV_CH), ddt.reshape(bsz * seq, LANES), acc


GROUP_W = D_MODEL // 2


def _gnorm_fwd(y, z, o_att, w):
    n = y.shape[0]
    tm = min(512, n)

    def body(y_ref, z_ref, o_ref, w_ref, out_ref):
        zv = z_ref[...]
        g = y_ref[...] * (zv * _sigmoid(zv))
        for gi in range(2):
            sl = slice(GROUP_W * gi, GROUP_W * (gi + 1))
            gs = g[:, sl]
            r = lax.rsqrt(jnp.mean(gs * gs, axis=-1, keepdims=True) + NORM_EPS)
            out_ref[:, sl] = (gs * r * w_ref[:, sl]).astype(BF16)
        out_ref[:, D_MODEL:] = o_ref[...].astype(BF16)

    tok = pl.BlockSpec((tm, D_MODEL), lambda i: (i, 0))
    return pl.pallas_call(
        body, name="gnorm_fwd", grid=(n // tm,),
        in_specs=[tok, tok, tok, pl.BlockSpec((1, D_MODEL), lambda i: (0, 0))],
        out_specs=pl.BlockSpec((tm, MIX_WIDTH), lambda i: (i, 0)),
        out_shape=jax.ShapeDtypeStruct((n, MIX_WIDTH), BF16),
        compiler_params=_params(("parallel",)),
    )(y, z, o_att, w)


def _gnorm_bwd(dycat, y, z, w):
    n = y.shape[0]
    tm = min(256, n)

    def body(dn_ref, y_ref, z_ref, w_ref, dy_ref, dz_ref, gw_ref):
        zv, yv = z_ref[...], y_ref[...]
        sz = _sigmoid(zv)
        sil = zv * sz
        g = yv * sil

        @pl.when(pl.program_id(0) == 0)
        def _():
            gw_ref[...] = jnp.zeros_like(gw_ref)

        for gi in range(2):
            sl = slice(GROUP_W * gi, GROUP_W * (gi + 1))
            gs = g[:, sl]
            r = lax.rsqrt(jnp.mean(gs * gs, axis=-1, keepdims=True) + NORM_EPS)
            nrm = gs * r
            dyn = dn_ref[:, sl]
            dn = dyn * w_ref[:, sl]
            dgs = r * (dn - nrm * jnp.mean(dn * nrm, axis=-1, keepdims=True))
            dy_ref[:, sl] = dgs * sil[:, sl]
            dz_ref[:, sl] = (dgs * yv[:, sl] * (sz[:, sl] * (1.0 + zv[:, sl] * (1.0 - sz[:, sl])))).astype(BF16)
            gw_ref[:, sl] += jnp.sum(dyn * nrm, axis=0, keepdims=True)

    tok = pl.BlockSpec((tm, D_MODEL), lambda i: (i, 0))
    row = pl.BlockSpec((1, D_MODEL), lambda i: (0, 0))
    return pl.pallas_call(
        body, name="gnorm_bwd", grid=(n // tm,),
        in_specs=[tok, tok, tok, row], out_specs=[tok, tok, row],
        out_shape=[jax.ShapeDtypeStruct((n, D_MODEL), F32), jax.ShapeDtypeStruct((n, D_MODEL), BF16),
                   jax.ShapeDtypeStruct((1, D_MODEL), F32)],
        compiler_params=_params(("arbitrary",)),
    )(dycat, y, z, w)


def _fox_prep(dtf, fb, bsz, seq):
    def body(x_ref, b_ref, c_ref):
        tri = (_iota((CHUNK, CHUNK), 1) <= _iota((CHUNK, CHUNK), 0)).astype(F32)
        carry = jnp.zeros((1, LANES), F32)
        for j in range(seq // CHUNK):
            sl = slice(CHUNK * j, CHUNK * (j + 1))
            lf = _log_sigmoid(x_ref[sl, :] + b_ref[...])
            c_ref[sl, :] = _hdot(tri, lf) + carry
            carry = carry + jnp.sum(lf, axis=0, keepdims=True)

    blk = pl.BlockSpec((seq, LANES), lambda b: (b, 0))
    return pl.pallas_call(
        body, name="fox_prep", grid=(bsz,),
        in_specs=[blk, pl.BlockSpec((1, LANES), lambda b: (0, 0))], out_specs=blk,
        out_shape=jax.ShapeDtypeStruct((bsz * seq, LANES), F32),
        compiler_params=_params(("parallel",)),
    )(dtf, fb)


def _fox_prep_bwd(dck, dcq, ddt, dtf, fb, bsz, seq):
    nj = seq // CHUNK

    def body(dck_ref, dcq_ref, ddt_ref, x_ref, b_ref, out_ref, gb_ref):
        triu = (_iota((CHUNK, CHUNK), 0) <= _iota((CHUNK, CHUNK), 1)).astype(F32)
        is_f = (_iota((CHUNK, LANES), 1) >= F_COL) & (_iota((CHUNK, LANES), 1) < 2 * F_COL)
        carry = jnp.zeros((1, LANES), F32)
        total = jnp.zeros((1, LANES), F32)
        for j in range(nj - 1, -1, -1):
            sl = slice(CHUNK * j, CHUNK * (j + 1))
            dcv = dck_ref[sl, :] + dcq_ref[sl, :]
            dlf = _hdot(triu, dcv) + carry
            carry = carry + jnp.sum(dcv, axis=0, keepdims=True)
            df = jnp.where(is_f, dlf * _sigmoid(-(x_ref[sl, :] + b_ref[...])), 0.0)
            out_ref[sl, :] = (df + ddt_ref[sl, :]).astype(BF16)
            total = total + jnp.sum(df, axis=0, keepdims=True)

        @pl.when(pl.program_id(0) == 0)
        def _():
            gb_ref[...] = jnp.zeros_like(gb_ref)

        gb_ref[...] += jnp.broadcast_to(total, (SUBLANES, LANES))

    blk = pl.BlockSpec((seq, LANES), lambda b: (b, 0))
    return pl.pallas_call(
        body, name="fox_prep_bwd", grid=(bsz,),
        in_specs=[blk, blk, blk, blk, pl.BlockSpec((1, LANES), lambda b: (0, 0))],
        out_specs=[blk, pl.BlockSpec((SUBLANES, LANES), lambda b: (0, 0))],
        out_shape=[jax.ShapeDtypeStruct((bsz * seq, LANES), BF16), jax.ShapeDtypeStruct((SUBLANES, LANES), F32)],
        compiler_params=_params(("arbitrary",)),
    )(dck, dcq, ddt, dtf, fb)


AUX_C = 3
AUX_ONE = 3


def _att_block(seq):
    return min(256, seq)


def _att_operands(q_ref, k_ref, c_ref, seq, hh, pair):
    lane = _iota((seq, LANES), 1)
    lm = (lane < HEAD_DIM) if hh == 0 else (lane >= HEAD_DIM)
    base = HEAD_DIM * (1 - hh)
    negc = -jnp.sum(jnp.where(lane == F_COL + 2 * pair + hh, c_ref[...], 0.0), axis=1, keepdims=True)
    c1 = negc.astype(BF16)
    r1 = negc - c1.astype(F32)
    c2 = r1.astype(BF16)
    c3 = (r1 - c2.astype(F32)).astype(BF16)
    zero = jnp.zeros((seq, LANES), BF16)
    one = jnp.ones((seq, LANES), BF16)
    ka = jnp.where(lm, k_ref[...], jnp.where(lane == base, c1, jnp.where(lane == base + 1, c2, jnp.where(
        lane == base + 2, c3, jnp.where(lane == base + AUX_ONE, one, zero)))))
    qa = jnp.where(lm, q_ref[...] * ATT_SCALE, jnp.where((lane >= base) & (lane < base + AUX_C), one, zero))
    return lm, base, qa, ka


def _att_fwd(qkv, ccol, bsz, seq, gather):
    blk = _att_block(seq)
    nb = seq // blk
    nx = len(gather)

    def body(*refs):
        q_ref, k_ref, v_ref, c_ref = refs[:4]
        x_refs = refs[4:4 + nx]
        o_ref, lse_ref = refs[4 + nx:6 + nx]
        xo_refs = refs[6 + nx:6 + 2 * nx]
        qa_scr, ka_scr, va_scr = refs[6 + 2 * nx:9 + 2 * nx]
        sems = refs[9 + 2 * nx:]
        pair = pl.program_id(1)

        @pl.when((pl.program_id(0) == 0) & (pair == 0))
        def _():
            _exchange_start("gather", x_refs, xo_refs, *sems)

        causal = _iota((blk, blk), 0) >= _iota((blk, blk), 1)
        for hh in range(2):
            lm, _, qa, ka = _att_operands(q_ref, k_ref, c_ref, seq, hh, pair)
            qa_scr[hh] = qa
            ka_scr[hh] = ka
            va_scr[hh] = jnp.where(lm, v_ref[...], jnp.zeros((seq, LANES), BF16))

        def q_step(i, carry0):
            rows = pl.ds(pl.multiple_of(i * blk, blk), blk)

            def scores(j):
                krows = pl.ds(pl.multiple_of(j * blk, blk), blk)
                return tuple(_bdot_nt(qa_scr[hh, rows, :], ka_scr[hh, krows, :]) for hh in range(2))

            def update(state, s_pair, j, masked):
                krows = pl.ds(pl.multiple_of(j * blk, blk), blk)
                alphas, ls, pvs, ms = [], [], [], []
                for hh in range(2):
                    m, l, _ = state[hh]
                    s = jnp.where(causal, s_pair[hh], NEG) if masked else s_pair[hh]
                    mn = jnp.maximum(m, jnp.max(s, axis=1, keepdims=True))
                    alpha = jnp.exp(m - mn)
                    pr = jnp.exp(s - mn)
                    ms.append(mn)
                    alphas.append(alpha)
                    ls.append(alpha * l + jnp.sum(pr, axis=1, keepdims=True))
                    pvs.append(_bdot(pr, va_scr[hh, krows, :]))
                return tuple((ms[hh], ls[hh], alphas[hh] * state[hh][2] + pvs[hh]) for hh in range(2))

            def step(j, carry):
                state, s_cur = carry
                s_next = scores(j + 1)
                return update(state, s_cur, j, False), s_next

            one = (jnp.full((blk, 1), NEG, F32), jnp.zeros((blk, 1), F32), jnp.zeros((blk, LANES), F32))
            state, s_last = lax.fori_loop(0, i, step, ((one, one), scores(0)))
            (m0, l0, acc0), (m1, l1, acc1) = update(state, s_last, i, True)
            o_ref[rows, :] = acc0 * (1.0 / l0) + acc1 * (1.0 / l1)
            lse_ref[0, 0, rows, :] = m0 + jnp.log(l0)
            lse_ref[0, 1, rows, :] = m1 + jnp.log(l1)
            return carry0

        lax.fori_loop(0, nb, q_step, 0)

        @pl.when((pl.program_id(0) == bsz - 1) & (pair == HEAD_PAIRS - 1))
        def _():
            _exchange_wait("gather", x_refs, xo_refs, *sems)

    col = lambda off: pl.BlockSpec((seq, LANES), lambda b, p: (b, off + p))
    head2 = pltpu.VMEM((2, seq, LANES), BF16)
    hbm = pl.BlockSpec(memory_space=pl.ANY)
    return pl.pallas_call(
        body, name="att_fwd", grid=(bsz, HEAD_PAIRS),
        in_specs=[col(0), col(HEAD_PAIRS), col(2 * HEAD_PAIRS), pl.BlockSpec((seq, LANES), lambda b, p: (b, 0))] + [hbm] * nx,
        out_specs=[pl.BlockSpec((seq, LANES), lambda b, p: (b, p)),
                   pl.BlockSpec((1, 2, seq, 1), lambda b, p: (b, p, 0, 0))] + [hbm] * nx,
        out_shape=[jax.ShapeDtypeStruct((bsz * seq, D_MODEL), F32), jax.ShapeDtypeStruct((bsz, SSD_HEADS, seq, 1), F32)]
        + _exchange_shapes("gather", gather),
        scratch_shapes=[head2, head2, head2] + _exchange_scratch(nx),
        compiler_params=_params(("arbitrary", "arbitrary")),
    )(qkv, qkv, qkv, ccol, *gather)


def _att_bwd(qkv, dycat, o, lse, ccol, bsz, seq):
    blk = _att_block(seq)
    nb = seq // blk

    def body(q_ref, k_ref, v_ref, do_ref, o_ref, lse_ref, c_ref, dq_ref, dk_ref, dv_ref, dck_ref, dcq_ref,
             qa_scr, ka_scr, va_scr, doa_scr, d_scr, dq_scr, dk_scr, dv_scr):
        pair = pl.program_id(1)
        causal = _iota((blk, blk), 0) >= _iota((blk, blk), 1)
        lane_t = _iota((seq, LANES), 1)
        lane_b = _iota((blk, LANES), 1)
        masks, bases = [], []
        for hh in range(2):
            lm, base, qa, ka = _att_operands(q_ref, k_ref, c_ref, seq, hh, pair)
            masks.append(lm)
            bases.append(base)
            qa_scr[hh] = qa
            ka_scr[hh] = ka
            va_scr[hh] = jnp.where(lm, v_ref[...], jnp.zeros((seq, LANES), BF16))
            doa = jnp.where(lm, do_ref[...], 0.0).astype(BF16)
            doa_scr[hh] = doa
            d_scr[hh] = jnp.sum(doa.astype(F32) * o_ref[...], axis=1, keepdims=True)
        dq_scr[...] = jnp.zeros_like(dq_scr)

        def kv_step(j, carry0):
            krows = pl.ds(pl.multiple_of(j * blk, blk), blk)
            dk_scr[...] = jnp.zeros_like(dk_scr)
            dv_scr[...] = jnp.zeros_like(dv_scr)

            def scores(i):
                rows = pl.ds(pl.multiple_of(i * blk, blk), blk)
                return tuple((_bdot_nt(qa_scr[hh, rows, :], ka_scr[hh, krows, :]),
                              _bdot_nt(doa_scr[hh, rows, :], va_scr[hh, krows, :])) for hh in range(2))

            def update(i, sd, masked):
                rows = pl.ds(pl.multiple_of(i * blk, blk), blk)
                for hh in range(2):
                    s, dp = sd[hh]
                    if masked:
                        s = jnp.where(causal, s, NEG)
                    pr = jnp.exp(s - lse_ref[0, hh, rows, :])
                    ds = (pr * (dp - d_scr[hh, rows, :])).astype(BF16)
                    dv_scr[hh] += _bdot_tn(pr, doa_scr[hh, rows, :])
                    dk_scr[hh] += _bdot_tn(ds, qa_scr[hh, rows, :])
                    dq_scr[hh, rows, :] += _bdot(ds, ka_scr[hh, krows, :])

            def q_step(i, sd_cur):
                sd_next = scores(jnp.minimum(i + 1, nb - 1))
                update(i, sd_cur, False)
                return sd_next

            sd_diag = scores(j)
            sd_first = scores(jnp.minimum(j + 1, nb - 1))
            update(j, sd_diag, True)
            lax.fori_loop(j + 1, nb, q_step, sd_first)
            lmb0 = lane_b < HEAD_DIM
            dk_ref[krows, :] = jnp.where(lmb0, dk_scr[0], dk_scr[1]).astype(BF16)
            dv_ref[krows, :] = (dv_scr[0] + dv_scr[1]).astype(BF16)
            for hh in range(2):
                dck_ref[0, hh, krows, :] = -jnp.sum(jnp.where(lane_b == bases[hh], dk_scr[hh], 0.0), axis=1, keepdims=True)
            return carry0

        lax.fori_loop(0, nb, kv_step, 0)
        dq_ref[...] = (jnp.where(masks[0], dq_scr[0], dq_scr[1]) * ATT_SCALE).astype(BF16)
        for hh in range(2):
            dcq_ref[0, hh] = jnp.sum(jnp.where(lane_t == bases[hh] + AUX_ONE, dq_scr[hh], 0.0), axis=1, keepdims=True)

    col = lambda off: pl.BlockSpec((seq, LANES), lambda b, p: (b, off + p))
    vec = pl.BlockSpec((1, 2, seq, 1), lambda b, p: (b, p, 0, 0))
    out = jax.ShapeDtypeStruct((bsz * seq, D_MODEL), BF16)
    vec_shape = jax.ShapeDtypeStruct((bsz, SSD_HEADS, seq, 1), F32)
    head2 = pltpu.VMEM((2, seq, LANES), BF16)
    return pl.pallas_call(
        body, name="att_bwd", grid=(bsz, HEAD_PAIRS),
        in_specs=[col(0), col(HEAD_PAIRS), col(2 * HEAD_PAIRS), col(HEAD_PAIRS), col(0), vec,
                  pl.BlockSpec((seq, LANES), lambda b, p: (b, 0))],
        out_specs=[col(0), col(0), col(0), vec, vec],
        out_shape=[out, out, out, vec_shape, vec_shape],
        scratch_shapes=[head2, head2, head2, head2, pltpu.VMEM((2, seq, 1), F32), pltpu.VMEM((2, seq, LANES), F32),
                        pltpu.VMEM((2, blk, LANES), F32), pltpu.VMEM((2, blk, LANES), F32)],
        compiler_params=_params(("parallel", "parallel")),
    )(qkv, qkv, qkv, dycat, o, lse, ccol)


def _adamw_math(w, g, m, v):
    m = ADAM_B1 * m + (1.0 - ADAM_B1) * g
    v = ADAM_B2 * v + (1.0 - ADAM_B2) * jnp.square(g)
    m_hat = m / ADAM_C1
    v_hat = v / ADAM_C2
    delta = -ADAM_LR * (m_hat / (jnp.sqrt(v_hat) + ADAM_EPS) + ADAM_WD * w)
    return delta, m, v


def _row_tile(rows):
    for tr in (256, 128, 64, 32, 16, 8):
        if rows % tr == 0:
            return tr
    return rows


def _sum_parts(parts, name):
    nparts, rows, cols = parts.shape
    tr = rows if rows % SUBLANES else _row_tile(rows)

    def body(p_ref, o_ref):
        g = p_ref[0].astype(F32)
        for s in range(1, nparts):
            g = g + p_ref[s].astype(F32)
        o_ref[...] = g

    return pl.pallas_call(
        body, name=name, grid=(rows // tr,),
        in_specs=[pl.BlockSpec((nparts, tr, cols), lambda i: (0, i, 0))],
        out_specs=pl.BlockSpec((tr, cols), lambda i: (i, 0)),
        out_shape=jax.ShapeDtypeStruct((rows, cols), F32),
        compiler_params=_params(("parallel",)),
    )(parts)


def _adamw_parts(parts, w, m, v, name):
    nparts, rows, cols = parts.shape
    tr = _row_tile(rows)

    def body(p_ref, w_ref, m_ref, v_ref, g_ref, d_ref, mo_ref, vo_ref):
        g = p_ref[0].astype(F32)
        for s in range(1, nparts):
            g = g + p_ref[s].astype(F32)
        delta, mn, vn = _adamw_math(w_ref[...], g, m_ref[...], v_ref[...])
        g_ref[...] = g
        d_ref[...] = delta
        mo_ref[...] = mn
        vo_ref[...] = vn

    blk = pl.BlockSpec((tr, cols), lambda i: (i, 0))
    shp = jax.ShapeDtypeStruct((rows, cols), F32)
    return pl.pallas_call(
        body, name=name, grid=(rows // tr,),
        in_specs=[pl.BlockSpec((nparts, tr, cols), lambda i: (0, i, 0)), blk, blk, blk],
        out_specs=[blk, blk, blk, blk], out_shape=[shp, shp, shp, shp],
        compiler_params=_params(("parallel",)),
    )(parts, w, m, v)


def _me():
    return lax.axis_index("x"), lax.axis_index("y"), lax.axis_index("c")


def _flip(pos, k):
    x, y, c = pos
    kx, ky, kc = (k >> 2) & 1, (k >> 1) & 1, k & 1
    return (x ^ kx if kx else x, y ^ ky if ky else y, c ^ kc if kc else c)


def _logical(pos):
    return 4 * pos[0] + 2 * pos[1] + pos[2]


def _all_gather_two_level(shards):
    narr = len(shards)

    def body(*refs):
        x_refs, out_refs = refs[:narr], refs[narr:2 * narr]
        send_sems, recv_sems, local_sems = refs[2 * narr:]
        x, y, c = _me()
        me, sibling = (x, y, c), (x, y, 1 - c)
        chips = [(1 - x, y), (x, 1 - y), (1 - x, 1 - y)]

        def copy(a, k, block, to, src=None):
            slot = out_refs[a].at[_logical(block)]
            return pltpu.make_async_remote_copy(
                src_ref=slot if src is None else src, dst_ref=slot,
                send_sem=send_sems.at[a, k], recv_sem=recv_sems.at[a, k], device_id=to, device_id_type=MESH)

        mine = [pltpu.make_async_copy(x_refs[a], out_refs[a].at[_logical(me)], local_sems.at[a]) for a in range(narr)]
        for cp in mine:
            cp.start()
        first = []
        for a in range(narr):
            first.append(copy(a, 0, me, sibling, src=x_refs[a]))
            first += [copy(a, 1 + j, me, (*chip, c), src=x_refs[a]) for j, chip in enumerate(chips)]
        for cp in first:
            cp.start()
        passed = []
        for a in range(narr):
            for j, chip in enumerate(chips):
                copy(a, 1 + j, (*chip, c), me).wait_recv()
                fwd = copy(a, 4 + j, (*chip, c), sibling)
                fwd.start()
                passed.append(fwd)
        for a in range(narr):
            copy(a, 0, sibling, me).wait_recv()
            for j, chip in enumerate(chips):
                copy(a, 4 + j, (*chip, 1 - c), me).wait_recv()
        for cp in first + passed:
            cp.wait_send()
        for cp in mine:
            cp.wait()

    hbm = pl.BlockSpec(memory_space=pl.ANY)
    return pl.pallas_call(
        body, name="all_gather_big",
        out_shape=[jax.ShapeDtypeStruct((N_DEV,) + s.shape, s.dtype) for s in shards],
        in_specs=[hbm] * narr, out_specs=[hbm] * narr,
        scratch_shapes=[pltpu.SemaphoreType.DMA((narr, 7)), pltpu.SemaphoreType.DMA((narr, 7)),
                        pltpu.SemaphoreType.DMA((narr,))],
    )(*shards)


def _exchange_copies(kind, x_refs, out_refs, send_sems, recv_sems, local_sems):
    me = _me()
    mine = _logical(me)
    local, remote = [], []
    for a, (x_ref, out_ref) in enumerate(zip(x_refs, out_refs)):
        own = x_ref if kind == "gather" else x_ref.at[mine]
        local.append(pltpu.make_async_copy(own, out_ref.at[mine], local_sems.at[a]))
        for k in range(1, N_DEV):
            peer = _flip(me, k)
            src = x_ref if kind == "gather" else x_ref.at[_logical(peer)]
            remote.append(pltpu.make_async_remote_copy(
                src_ref=src, dst_ref=out_ref.at[mine], send_sem=send_sems.at[a, k - 1], recv_sem=recv_sems.at[a, k - 1],
                device_id=peer, device_id_type=MESH))
    return local, remote


def _exchange_start(kind, x_refs, out_refs, send_sems, recv_sems, local_sems):
    local, remote = _exchange_copies(kind, x_refs, out_refs, send_sems, recv_sems, local_sems)
    for cp in local + remote:
        cp.start()


def _exchange_wait(kind, x_refs, out_refs, send_sems, recv_sems, local_sems):
    local, remote = _exchange_copies(kind, x_refs, out_refs, send_sems, recv_sems, local_sems)
    for cp in remote:
        cp.wait_recv()
    for cp in remote:
        cp.wait_send()
    for cp in local:
        cp.wait()


def _exchange_shapes(kind, arrays):
    return [jax.ShapeDtypeStruct(((N_DEV,) if kind == "gather" else ()) + a.shape, a.dtype) for a in arrays]


def _exchange_scratch(narrays):
    if not narrays:
        return []
    return [pltpu.SemaphoreType.DMA((narrays, N_DEV - 1)), pltpu.SemaphoreType.DMA((narrays, N_DEV - 1)),
            pltpu.SemaphoreType.DMA((narrays,))]


def _exchange_rows(x_ref, buf_ref, send_sems, recv_sems):
    me = _me()
    buf_ref[_logical(me)] = x_ref[...]
    copies = []
    for k in range(1, N_DEV):
        peer = _flip(me, k)
        copies.append(pltpu.make_async_remote_copy(
            src_ref=x_ref, dst_ref=buf_ref.at[_logical(me)],
            send_sem=send_sems.at[k - 1], recv_sem=recv_sems.at[k - 1], device_id=peer, device_id_type=MESH))
    for cp in copies:
        cp.start()
    for cp in copies:
        cp.wait_recv()
    for cp in copies:
        cp.wait_send()


def _sum_slots(buf_ref):
    total = buf_ref[0]
    for s in range(1, N_DEV):
        total = total + buf_ref[s]
    return total


def _small_gather(x):
    def body(x_ref, o_ref, buf_ref, send_sems, recv_sems):
        _exchange_rows(x_ref, buf_ref, send_sems, recv_sems)
        o_ref[...] = _sum_slots(buf_ref)

    return pl.pallas_call(
        body, name="small_gather", out_shape=jax.ShapeDtypeStruct(x.shape, F32),
        in_specs=[pl.BlockSpec(memory_space=pltpu.VMEM)], out_specs=pl.BlockSpec(memory_space=pltpu.VMEM),
        scratch_shapes=[pltpu.VMEM((N_DEV,) + x.shape, F32), pltpu.SemaphoreType.DMA((7,)), pltpu.SemaphoreType.DMA((7,))],
    )(x)


def _small_reduce_adamw(g, w, m, v):
    def body(g_ref, w_ref, m_ref, v_ref, go_ref, d_ref, mo_ref, vo_ref, buf_ref, send_sems, recv_sems):
        _exchange_rows(g_ref, buf_ref, send_sems, recv_sems)
        gs = _sum_slots(buf_ref)
        delta, mn, vn = _adamw_math(w_ref[...], gs, m_ref[...], v_ref[...])
        go_ref[...] = gs
        d_ref[...] = delta
        mo_ref[...] = mn
        vo_ref[...] = vn

    vm = pl.BlockSpec(memory_space=pltpu.VMEM)
    shp = jax.ShapeDtypeStruct(g.shape, F32)
    return pl.pallas_call(
        body, name="small_reduce_adamw", out_shape=[shp, shp, shp, shp],
        in_specs=[vm, vm, vm, vm], out_specs=[vm, vm, vm, vm],
        scratch_shapes=[pltpu.VMEM((N_DEV,) + g.shape, F32), pltpu.SemaphoreType.DMA((7,)), pltpu.SemaphoreType.DMA((7,))],
    )(g, w, m, v)


def _pad_cols(a, width):
    return jnp.pad(a, ((0, 0), (0, width - a.shape[1])))


def _local_step(x, target, norm_mix_w, w_in_t, conv_w, conv_b, dt_bias, a_log, d_skip, ssd_norm_w, f_bias,
                late_shards, norm_mlp_w, norm_final_w):
    bsz, seq, _ = x.shape
    n = bsz * seq
    x2 = x.reshape(n, D_MODEL)
    t2 = target.reshape(n, D_MODEL)
    nfw = norm_final_w.reshape(1, D_MODEL)

    bias = _pad_cols(dt_bias, LANES)
    arow = _pad_cols(-jnp.exp(a_log), LANES)
    biast, acol = bias.reshape(LANES, 1), arow.reshape(LANES, 1)
    dfull = jnp.repeat(d_skip, HEAD_DIM, axis=1)
    fb = jnp.pad(f_bias, ((0, 0), (F_COL, LANES - 2 * F_COL)))

    wt_z, wt_xbc, wt_qkv = w_in_t[:ROW_XBC], w_in_t[ROW_XBC:ROW_DT], w_in_t[ROW_Q:ROW_F]
    wt_dtf = jnp.concatenate([w_in_t[ROW_DT:ROW_Q], w_in_t[ROW_F:], jnp.zeros((LANES - 2 * F_COL, D_MODEL), BF16)], axis=0)
    wt_q, wt_k, wt_v = wt_qkv[:D_MODEL], wt_qkv[D_MODEL:2 * D_MODEL], wt_qkv[2 * D_MODEL:]

    h1 = _rms_fwd(x2, norm_mix_w, "rms_fwd_mix")
    z = _mm([(h1, wt_z)], "inproj_z", F32, 512, 512, nt=True)
    xbc_raw = _mm([(h1, wt_xbc)], "inproj_xbc", F32, 512, 512, nt=True)
    qkv = _mm([(h1, wt_qkv)], "inproj_qkv", BF16, 512, 512, nt=True)
    dtf = _mm([(h1, wt_dtf)], "inproj_dtf", F32, 512, LANES, nt=True)
    dtft = dtf.reshape(bsz, seq, LANES).transpose(0, 2, 1)

    xbc = _conv_fwd(xbc_raw, conv_w, conv_b, bsz, seq)
    y_pre, hall = _ssd_fwd(xbc, dtf, dtft, bias, biast, arow, acol, dfull, bsz, seq)

    ccol = _fox_prep(dtf, fb, bsz, seq)
    o_att, lse, w_out, w_up_t, w_down = _att_fwd(qkv, ccol, bsz, seq, late_shards)
    w_out, w_up_t, w_down = (w.reshape(-1, D_MODEL) for w in (w_out, w_up_t, w_down))

    ycat = _gnorm_fwd(y_pre, z, o_att, ssd_norm_w)
    h2 = _mm([(ycat, w_out)], "outproj", F32, 512, 512, res=x2)
    h2n = _rms_fwd(h2, norm_mlp_w, "rms_fwd_mlp")
    pre = _mm([(h2n, w_up_t)], "mlp_up", F32, 512, 1024, nt=True)
    h3 = _mlp_down_fwd(pre, w_down, h2)

    dh3, loss_row, g_nfw = _loss_bwd(h3, t2, nfw)
    dpre, u = _mlp_down_bwd(dh3, w_down, pre)
    g_w_down = _mm_tn(u, dh3, "grad_w_down", 1024, 1024, 512)
    g_w_up_t = _mm_tn(dpre, h2n, "grad_w_up", 1024, 1024, 512)
    by_shard = lambda g: g.reshape(N_DEV, g.shape[0] // N_DEV, D_MODEL)
    dh2n, recv_down, recv_up_t = _mm([(dpre, w_up_t)], "mlp_up_bwd", F32, 256, 512,
                                     exchange=("scatter", [by_shard(g_w_down), by_shard(g_w_up_t)]))
    dh2, g_nmlp = _rms_bwd(dh2n, h2, norm_mlp_w, dh3, "rms_bwd_mlp")

    g_w_out = _mm_tn(ycat, dh2, "grad_w_out", 1024, 1024, 512)
    dycat, recv_out = _mm([(dh2, w_out)], "outproj_bwd", F32, 512, 512, nt=True,
                          exchange=("scatter", [by_shard(g_w_out)]))
    dy_pre, dz, g_ssdn = _gnorm_bwd(dycat, y_pre, z, ssd_norm_w)
    dq, dk, dv, dck, dcq = _att_bwd(qkv, dycat, o_att, lse, ccol, bsz, seq)

    dxbc, ddt, ssd_acc = _ssd_bwd(dy_pre, xbc, dtf, dtft, bias, biast, arow, acol, dfull, hall, bsz, seq)
    dxbc_raw, g_cw, g_cb = _conv_bwd(dxbc, xbc_raw, conv_w, conv_b, bsz, seq)

    def head_cols(vec):
        cols = vec.reshape(bsz, SSD_HEADS, seq).transpose(0, 2, 1).reshape(n, SSD_HEADS)
        return jnp.pad(cols, ((0, 0), (F_COL, LANES - 2 * F_COL)))

    ddtf, g_fb = _fox_prep_bwd(head_cols(dck), head_cols(dcq), ddt, dtf, fb, bsz, seq)

    g_dtf = _mm_tn(ddtf, h1, "grad_w_in_dtf", LANES, 1024, 512)
    g_w_in_t = jnp.concatenate([
        _mm_tn(dz, h1, "grad_w_in_z", 1024, 1024, 512), _mm_tn(dxbc_raw, h1, "grad_w_in_xbc", 768, 1024, 512),
        g_dtf[:F_COL], _mm_tn(dq, h1, "grad_w_in_q", 1024, 1024, 512), _mm_tn(dk, h1, "grad_w_in_k", 1024, 1024, 512),
        _mm_tn(dv, h1, "grad_w_in_v", 1024, 1024, 512), g_dtf[F_COL:2 * F_COL]], axis=0)
    pieces = [(dz, wt_z), (dxbc_raw, wt_xbc), (dq, wt_q), (dk, wt_k), (dv, wt_v), (ddtf, wt_dtf)]
    dh1, recv_in_t = _mm(pieces, "inproj_bwd", F32, 256, 512, exchange=("scatter", [by_shard(g_w_in_t)]))
    dx, g_nmix = _rms_bwd(dh1, x2, norm_mix_w, dh2, "rms_bwd_mix")

    small = dict(
        norm_mix_w=g_nmix, norm_mlp_w=g_nmlp, norm_final_w=g_nfw, ssd_norm_w=g_ssdn, conv_b=g_cb, conv_w=g_cw,
        dt_bias=ssd_acc[16:17, :SSD_HEADS], a_log=ssd_acc[0:1, :SSD_HEADS], d_skip=ssd_acc[8:9, :SSD_HEADS],
        f_bias=g_fb[0:1, F_COL:2 * F_COL])
    recv = dict(w_in_t=recv_in_t, w_out=recv_out, w_up_t=recv_up_t, w_down=recv_down)
    return loss_row[0, 0], dx.reshape(bsz, seq, D_MODEL), small, recv


SMALL_LAYOUT = (("norm_mix_w", 1, D_MODEL), ("norm_mlp_w", 1, D_MODEL), ("norm_final_w", 1, D_MODEL),
                ("ssd_norm_w", 1, D_MODEL), ("conv_b", 1, CONV_CH), ("conv_w", CONV_K, CONV_CH),
                ("dt_bias", 1, SSD_HEADS), ("a_log", 1, SSD_HEADS), ("d_skip", 1, SSD_HEADS), ("f_bias", 1, SSD_HEADS))


def _pack_small(vals):
    tiles = []
    for name, nrows, width in SMALL_LAYOUT:
        tiles.append(jnp.pad(vals[name].reshape(nrows, width), ((0, SUBLANES - nrows), (0, SMALL_COLS - width))))
    return jnp.concatenate(tiles, axis=0)


def _unpack_small(packed, like):
    out = {}
    for i, (name, nrows, width) in enumerate(SMALL_LAYOUT):
        out[name] = packed[SUBLANES * i:SUBLANES * i + nrows, :width].reshape(like[name].shape)
    return out


def kernel(x, norm_mix_w, w_in, conv_w, conv_b, dt_bias, a_log, d_skip, ssd_norm_w, f_bias, w_out, norm_mlp_w, w_up, w_down, norm_final_w, loss_target, m_norm_mix_w, m_w_in, m_conv_w, m_conv_b, m_dt_bias, m_a_log, m_d_skip, m_ssd_norm_w, m_f_bias, m_w_out, m_norm_mlp_w, m_w_up, m_w_down, m_norm_final_w, v_norm_mix_w, v_w_in, v_conv_w, v_conv_b, v_dt_bias, v_a_log, v_d_skip, v_ssd_norm_w, v_f_bias, v_w_out, v_norm_mlp_w, v_w_up, v_w_down, v_norm_final_w):
    me = 4 * lax.axis_index("x") + 2 * lax.axis_index("y") + lax.axis_index("c")
    conv_shard_w = CONV_CH // N_DEV
    zero = jnp.zeros((), jnp.int32)

    def place_conv(shard):
        return lax.dynamic_update_slice(jnp.zeros((CONV_K, CONV_CH), F32), shard[0], (zero, me * conv_shard_w))

    (g_in_t,) = _all_gather_two_level([w_in[0].T.astype(BF16)])
    late_shards = [w_out[0].astype(BF16), w_up[0].T.astype(BF16), w_down[0].astype(BF16)]
    conv_full = _small_gather(jnp.pad(place_conv(conv_w), ((0, SUBLANES - CONV_K), (0, 0))))[:CONV_K]

    loss_local, grad_x, g_small, recv = _local_step(
        x, loss_target, norm_mix_w, g_in_t.reshape(IN_WIDTH, D_MODEL), conv_full, conv_b, dt_bias, a_log, d_skip,
        ssd_norm_w, f_bias, late_shards, norm_mlp_w, norm_final_w)
    loss = lax.psum(loss_local, MESH_AXES)

    grad_in = _sum_parts(recv["w_in_t"], "sum_w_in").T[None]
    grad_up = _sum_parts(recv["w_up_t"], "sum_w_up").T[None]
    big = {
        "w_in": _adamw_parts(grad_in, w_in[0], m_w_in[0], v_w_in[0], "adamw_w_in"),
        "w_out": _adamw_parts(recv["w_out"], w_out[0], m_w_out[0], v_w_out[0], "adamw_w_out"),
        "w_up": _adamw_parts(grad_up, w_up[0], m_w_up[0], v_w_up[0], "adamw_w_up"),
        "w_down": _adamw_parts(recv["w_down"], w_down[0], m_w_down[0], v_w_down[0], "adamw_w_down"),
    }

    like = dict(norm_mix_w=norm_mix_w, norm_mlp_w=norm_mlp_w, norm_final_w=norm_final_w, ssd_norm_w=ssd_norm_w,
                conv_b=conv_b, conv_w=jnp.zeros((1, CONV_K, CONV_CH), F32), dt_bias=dt_bias, a_log=a_log,
                d_skip=d_skip, f_bias=f_bias)
    w_small = dict(like, conv_w=place_conv(conv_w))
    m_small = dict(norm_mix_w=m_norm_mix_w, norm_mlp_w=m_norm_mlp_w, norm_final_w=m_norm_final_w,
                   ssd_norm_w=m_ssd_norm_w, conv_b=m_conv_b, conv_w=place_conv(m_conv_w), dt_bias=m_dt_bias,
                   a_log=m_a_log, d_skip=m_d_skip, f_bias=m_f_bias)
    v_small = dict(norm_mix_w=v_norm_mix_w, norm_mlp_w=v_norm_mlp_w, norm_final_w=v_norm_final_w,
                   ssd_norm_w=v_ssd_norm_w, conv_b=v_conv_b, conv_w=place_conv(v_conv_w), dt_bias=v_dt_bias,
                   a_log=v_a_log, d_skip=v_d_skip, f_bias=v_f_bias)
    small = _small_reduce_adamw(_pack_small(g_small), _pack_small(w_small), _pack_small(m_small), _pack_small(v_small))
    small = [_unpack_small(s, like) for s in small]
    for s in small:
        s["conv_w"] = lax.dynamic_slice(s["conv_w"], (zero, zero, me * conv_shard_w), (1, CONV_K, conv_shard_w))

    order = ["norm_mix_w", "w_in", "conv_w", "conv_b", "dt_bias", "a_log", "d_skip", "ssd_norm_w", "f_bias", "w_out",
             "norm_mlp_w", "w_up", "w_down", "norm_final_w"]
    outs = [loss, grad_x]
    for kind in range(4):
        for name in order:
            outs.append(big[name][kind][None] if name in big else small[kind][name])
    return tuple(outs)
```

```python
import jax
import jax.numpy as jnp
from jax import lax
from jax.experimental import pallas as pl
from jax.experimental.pallas import tpu as pltpu

F32, BF16 = jnp.float32, jnp.bfloat16
HI = lax.Precision.HIGHEST

D_MODEL = 1024
SSD_HEADS = 16
HEAD_DIM = 64
SSD_STATE = 128
CHUNK = 128
CONV_CH = 1536
CONV_K = 4
D_FF = 4096
MIX_WIDTH = 2048
IN_WIDTH = 5664
NORM_EPS = 1e-5
ATT_SCALE = 0.125

LANES = 128
SUBLANES = 8
HEAD_PAIRS = SSD_HEADS // 2
NEG = -1e30
VMEM_LIMIT = 52 * 1024 * 1024

ROW_XBC, ROW_DT, ROW_Q, ROW_F = 1024, 2560, 2576, 5648
F_COL = SSD_HEADS

N_DEV = 8
MESH_AXES = ("x", "y", "c")
MESH = pl.DeviceIdType.MESH

ADAM_LR, ADAM_B1, ADAM_B2, ADAM_EPS, ADAM_WD, ADAM_STEP = 0.001, 0.9, 0.999, 1e-08, 0.01, 10
ADAM_C1 = 1.0 - ADAM_B1 ** ADAM_STEP
ADAM_C2 = 1.0 - ADAM_B2 ** ADAM_STEP

SMALL_COLS = CONV_CH


def _params(sem=None):
    return pltpu.CompilerParams(dimension_semantics=sem, vmem_limit_bytes=VMEM_LIMIT)


def _sigmoid(u):
    return 1.0 / (1.0 + jnp.exp(-u))


def _softplus(u):
    return jnp.maximum(u, 0.0) + jnp.log(1.0 + jnp.exp(-jnp.abs(u)))


def _log_sigmoid(u):
    return jnp.minimum(u, 0.0) - jnp.log(1.0 + jnp.exp(-jnp.abs(u)))


def _bdot(a, b):
    return jnp.dot(a.astype(BF16), b.astype(BF16), preferred_element_type=F32)


def _bdot_nt(a, b):
    return lax.dot_general(a.astype(BF16), b.astype(BF16), (((1,), (1,)), ((), ())), preferred_element_type=F32)


def _bdot_tn(a, b):
    return lax.dot_general(a.astype(BF16), b.astype(BF16), (((0,), (0,)), ((), ())), preferred_element_type=F32)


def _hdot(a, b):
    return jnp.dot(a, b, precision=HI, preferred_element_type=F32)


def _iota(shape, dim):
    return lax.broadcasted_iota(jnp.int32, shape, dim)


def _head_expand():
    return (jnp.right_shift(_iota((LANES, D_MODEL), 1), 6) == _iota((LANES, D_MODEL), 0)).astype(F32)


def _head_reduce():
    return (jnp.right_shift(_iota((D_MODEL, LANES), 0), 6) == _iota((D_MODEL, LANES), 1)).astype(F32)


def _rms_fwd(x, w, name):
    n, d = x.shape
    tm = min(512, n)

    def body(x_ref, w_ref, o_ref):
        xv = x_ref[...]
        r = lax.rsqrt(jnp.mean(xv * xv, axis=-1, keepdims=True) + NORM_EPS)
        o_ref[...] = (xv * r * w_ref[...]).astype(BF16)

    return pl.pallas_call(
        body, name=name, grid=(n // tm,),
        in_specs=[pl.BlockSpec((tm, d), lambda i: (i, 0)), pl.BlockSpec((1, d), lambda i: (0, 0))],
        out_specs=pl.BlockSpec((tm, d), lambda i: (i, 0)),
        out_shape=jax.ShapeDtypeStruct((n, d), BF16),
        compiler_params=_params(("parallel",)),
    )(x, w)


def _rms_bwd(dn, x, w, dres, name):
    n, d = x.shape
    tm = min(256, n)

    def body(dn_ref, x_ref, w_ref, dres_ref, dx_ref, gw_ref):
        xv = x_ref[...]
        r = lax.rsqrt(jnp.mean(xv * xv, axis=-1, keepdims=True) + NORM_EPS)
        nrm = xv * r
        dnv = dn_ref[...]
        g = dnv * w_ref[...]
        dx_ref[...] = dres_ref[...] + r * (g - nrm * jnp.mean(g * nrm, axis=-1, keepdims=True))

        @pl.when(pl.program_id(0) == 0)
        def _():
            gw_ref[...] = jnp.zeros_like(gw_ref)

        gw_ref[...] += jnp.sum(dnv * nrm, axis=0, keepdims=True)

    tok = pl.BlockSpec((tm, d), lambda i: (i, 0))
    row = pl.BlockSpec((1, d), lambda i: (0, 0))
    return pl.pallas_call(
        body, name=name, grid=(n // tm,),
        in_specs=[tok, tok, row, tok], out_specs=[tok, row],
        out_shape=[jax.ShapeDtypeStruct((n, d), F32), jax.ShapeDtypeStruct((1, d), F32)],
        compiler_params=_params(("arbitrary",)),
    )(dn, x, w, dres)


def _mm(pairs, name, out_dtype, tm, tn, nt=False, res=None, exchange=None):
    m = pairs[0][0].shape[0]
    n = pairs[0][1].shape[0 if nt else 1]
    tm, tn = min(tm, m), min(tn, n)
    gm, gn = m // tm, n // tn
    npairs = len(pairs)
    n_in = 2 * npairs + (res is not None)
    kind, xs = (None, []) if exchange is None else exchange
    nx = len(xs)

    def body(*refs):
        x_refs, o_ref = refs[n_in:n_in + nx], refs[n_in + nx]
        xo_refs, sems = refs[n_in + nx + 1:n_in + 2 * nx + 1], refs[n_in + 2 * nx + 1:]
        i, j = pl.program_id(0), pl.program_id(1)
        if nx:
            @pl.when((i == 0) & (j == 0))
            def _():
                _exchange_start(kind, x_refs, xo_refs, *sems)

        acc = None if res is None else refs[2 * npairs][...]
        for p in range(npairs):
            a_v, b_v = refs[2 * p][...], refs[2 * p + 1][...]
            d = _bdot_nt(a_v, b_v) if nt else _bdot(a_v, b_v)
            acc = d if acc is None else acc + d
        o_ref[...] = acc.astype(o_ref.dtype)
        if nx:
            @pl.when((i == gm - 1) & (j == gn - 1))
            def _():
                _exchange_wait(kind, x_refs, xo_refs, *sems)

    in_specs, args = [], []
    for a, b in pairs:
        k = a.shape[1]
        in_specs.append(pl.BlockSpec((tm, k), lambda i, j: (i, 0)))
        in_specs.append(pl.BlockSpec((tn, k), lambda i, j: (j, 0)) if nt else pl.BlockSpec((k, tn), lambda i, j: (0, j)))
        args += [a, b]
    if res is not None:
        in_specs.append(pl.BlockSpec((tm, tn), lambda i, j: (i, j)))
        args.append(res)
    hbm = pl.BlockSpec(memory_space=pl.ANY)
    outs = pl.pallas_call(
        body, name=name, grid=(gm, gn), in_specs=in_specs + [hbm] * nx,
        out_specs=[pl.BlockSpec((tm, tn), lambda i, j: (i, j))] + [hbm] * nx,
        out_shape=[jax.ShapeDtypeStruct((m, n), out_dtype)] + _exchange_shapes(kind, xs),
        scratch_shapes=_exchange_scratch(nx),
        compiler_params=_params(("arbitrary", "arbitrary") if nx else ("parallel", "parallel")),
    )(*args, *xs)
    return outs if nx else outs[0]


def _mm_tn(a, b, name, tm, tn, tk):
    t, m = a.shape
    n = b.shape[1]
    tm, tn, tk = min(tm, m), min(tn, n), min(tk, t)
    nk = t // tk

    def body(a_ref, b_ref, o_ref, acc_ref):
        kk = pl.program_id(2)

        @pl.when(kk == 0)
        def _():
            acc_ref[...] = jnp.zeros_like(acc_ref)

        acc_ref[...] += _bdot_tn(a_ref[...], b_ref[...])

        @pl.when(kk == nk - 1)
        def _():
            o_ref[...] = acc_ref[...].astype(o_ref.dtype)

    return pl.pallas_call(
        body, name=name, grid=(m // tm, n // tn, nk),
        in_specs=[pl.BlockSpec((tk, tm), lambda i, j, kk: (kk, i)), pl.BlockSpec((tk, tn), lambda i, j, kk: (kk, j))],
        out_specs=pl.BlockSpec((tm, tn), lambda i, j, kk: (i, j)),
        out_shape=jax.ShapeDtypeStruct((m, n), BF16),
        scratch_shapes=[pltpu.VMEM((tm, tn), F32)],
        compiler_params=_params(("parallel", "parallel", "arbitrary")),
    )(a, b)


def _mlp_down_fwd(pre, w_down, h2):
    m, k = pre.shape
    n = w_down.shape[1]
    tm, tn = min(256, m), n

    def body(p_ref, b_ref, r_ref, o_ref):
        u = jnp.square(jnp.maximum(p_ref[...], 0.0))
        o_ref[...] = r_ref[...] + _bdot(u, b_ref[...])

    return pl.pallas_call(
        body, name="mlp_down_fwd", grid=(m // tm, n // tn),
        in_specs=[pl.BlockSpec((tm, k), lambda i, j: (i, 0)), pl.BlockSpec((k, tn), lambda i, j: (0, j)),
                  pl.BlockSpec((tm, tn), lambda i, j: (i, j))],
        out_specs=pl.BlockSpec((tm, tn), lambda i, j: (i, j)),
        out_shape=jax.ShapeDtypeStruct((m, n), F32),
        compiler_params=_params(("parallel", "parallel")),
    )(pre, w_down, h2)


def _mlp_down_bwd(dh3, w_down, pre):
    m, k = dh3.shape
    n = w_down.shape[0]
    tm, tn = min(256, m), n

    def body(a_ref, b_ref, p_ref, dpre_ref, u_ref):
        r = jnp.maximum(p_ref[...], 0.0)
        du = _bdot_nt(a_ref[...], b_ref[...])
        dpre_ref[...] = (du * (2.0 * r)).astype(BF16)
        u_ref[...] = (r * r).astype(BF16)

    blk = pl.BlockSpec((tm, tn), lambda i, j: (i, j))
    return pl.pallas_call(
        body, name="mlp_down_bwd", grid=(m // tm, n // tn),
        in_specs=[pl.BlockSpec((tm, k), lambda i, j: (i, 0)), pl.BlockSpec((tn, k), lambda i, j: (j, 0)), blk],
        out_specs=[blk, blk],
        out_shape=[jax.ShapeDtypeStruct((m, n), BF16), jax.ShapeDtypeStruct((m, n), BF16)],
        compiler_params=_params(("parallel", "parallel")),
    )(dh3, w_down, pre)


def _loss_bwd(h3, target, w):
    n, d = h3.shape
    tm = min(256, n)

    def body(h_ref, t_ref, w_ref, dh_ref, loss_ref, gw_ref):
        xv = h_ref[...]
        r = lax.rsqrt(jnp.mean(xv * xv, axis=-1, keepdims=True) + NORM_EPS)
        nrm = xv * r
        wv = w_ref[...]
        err = nrm * wv - t_ref[...]
        part = 0.5 * jnp.sum(jnp.mean(err * err, axis=-1, keepdims=True), axis=0, keepdims=True)
        dy = err * (1.0 / d)
        g = dy * wv
        dh_ref[...] = r * (g - nrm * jnp.mean(g * nrm, axis=-1, keepdims=True))

        @pl.when(pl.program_id(0) == 0)
        def _():
            loss_ref[...] = jnp.zeros_like(loss_ref)
            gw_ref[...] = jnp.zeros_like(gw_ref)

        loss_ref[...] += jnp.broadcast_to(part, loss_ref.shape)
        gw_ref[...] += jnp.sum(dy * nrm, axis=0, keepdims=True)

    tok = pl.BlockSpec((tm, d), lambda i: (i, 0))
    row = pl.BlockSpec((1, d), lambda i: (0, 0))
    return pl.pallas_call(
        body, name="loss_bwd", grid=(n // tm,),
        in_specs=[tok, tok, row], out_specs=[tok, pl.BlockSpec((1, LANES), lambda i: (0, 0)), row],
        out_shape=[jax.ShapeDtypeStruct((n, d), F32), jax.ShapeDtypeStruct((1, LANES), F32),
                   jax.ShapeDtypeStruct((1, d), F32)],
        compiler_params=_params(("arbitrary",)),
    )(h3, target, w)


CONV_TC = 512


def _conv_fwd(xbc, cw, cb, bsz, seq):
    tt = min(256, seq)
    nt, hb = seq // tt, tt // SUBLANES
    x3 = xbc.reshape(bsz, seq, CONV_CH)

    def body(xm_ref, xh_ref, w_ref, b_ref, o_ref):
        i = pl.program_id(2)
        x = xm_ref[0]
        halo = jnp.where(i > 0, xh_ref[0], 0.0)
        xx = jnp.concatenate([halo, x], axis=0)
        acc = x * w_ref[3:4, :] + b_ref[...]
        for s in range(1, CONV_K):
            acc = acc + pltpu.roll(xx, s, 0)[SUBLANES:, :] * w_ref[3 - s:4 - s, :]
        o_ref[0] = acc * _sigmoid(acc)

    out = pl.pallas_call(
        body, name="conv_fwd", grid=(CONV_CH // CONV_TC, bsz, nt),
        in_specs=[pl.BlockSpec((1, tt, CONV_TC), lambda c, b, i: (b, i, c)),
                  pl.BlockSpec((1, SUBLANES, CONV_TC), lambda c, b, i: (b, jnp.maximum(i * hb - 1, 0), c)),
                  pl.BlockSpec((CONV_K, CONV_TC), lambda c, b, i: (0, c)),
                  pl.BlockSpec((1, CONV_TC), lambda c, b, i: (0, c))],
        out_specs=pl.BlockSpec((1, tt, CONV_TC), lambda c, b, i: (b, i, c)),
        out_shape=jax.ShapeDtypeStruct((bsz, seq, CONV_CH), F32),
        compiler_params=_params(("parallel", "parallel", "parallel")),
    )(x3, x3, cw, cb)
    return out.reshape(bsz * seq, CONV_CH)


def _conv_bwd(da, xbc, cw, cb, bsz, seq):
    tt = min(256, seq)
    nt, hb = seq // tt, tt // SUBLANES
    x3 = xbc.reshape(bsz, seq, CONV_CH)
    da3 = da.reshape(bsz, seq, CONV_CH)
    n2 = tt + SUBLANES

    def body(dam_ref, daa_ref, xm_ref, xb_ref, xa_ref, w_ref, b_ref, du_ref, gw_ref, gb_ref):
        bi, i = pl.program_id(1), pl.program_id(2)
        before = jnp.where(i > 0, xb_ref[0], 0.0)
        after = jnp.where(i < nt - 1, xa_ref[0], 0.0)
        xx = jnp.concatenate([before, xm_ref[0], after], axis=0)
        rolled = [xx] + [pltpu.roll(xx, s, 0) for s in range(1, CONV_K)]
        pre = b_ref[...]
        for s in range(CONV_K):
            pre = pre + rolled[s][SUBLANES:, :] * w_ref[3 - s:4 - s, :]
        da_ext = jnp.concatenate([dam_ref[0], jnp.where(i < nt - 1, daa_ref[0], 0.0)], axis=0)
        sg = _sigmoid(pre)
        dpre = da_ext * sg * (1.0 + pre * (1.0 - sg))
        du = dpre[:tt, :] * w_ref[3:4, :]
        for s in range(1, CONV_K):
            du = du + pltpu.roll(dpre, n2 - s, 0)[:tt, :] * w_ref[3 - s:4 - s, :]
        du_ref[0] = du.astype(BF16)
        dpm = dpre[:tt, :]
        gws = [jnp.sum(dpm * rolled[3 - kk][SUBLANES:SUBLANES + tt, :], axis=0, keepdims=True) for kk in range(CONV_K)]

        @pl.when((bi == 0) & (i == 0))
        def _():
            gw_ref[...] = jnp.zeros_like(gw_ref)
            gb_ref[...] = jnp.zeros_like(gb_ref)

        gw_ref[...] += jnp.concatenate(gws, axis=0)
        gb_ref[...] += jnp.sum(dpm, axis=0, keepdims=True)

    main = pl.BlockSpec((1, tt, CONV_TC), lambda c, b, i: (b, i, c))
    prev = pl.BlockSpec((1, SUBLANES, CONV_TC), lambda c, b, i: (b, jnp.maximum(i * hb - 1, 0), c))
    nxt = pl.BlockSpec((1, SUBLANES, CONV_TC), lambda c, b, i: (b, jnp.minimum((i + 1) * hb, seq // SUBLANES - 1), c))
    du, gw, gb = pl.pallas_call(
        body, name="conv_bwd", grid=(CONV_CH // CONV_TC, bsz, nt),
        in_specs=[main, nxt, main, prev, nxt,
                  pl.BlockSpec((CONV_K, CONV_TC), lambda c, b, i: (0, c)),
                  pl.BlockSpec((1, CONV_TC), lambda c, b, i: (0, c))],
        out_specs=[main, pl.BlockSpec((CONV_K, CONV_TC), lambda c, b, i: (0, c)),
                   pl.BlockSpec((1, CONV_TC), lambda c, b, i: (0, c))],
        out_shape=[jax.ShapeDtypeStruct((bsz, seq, CONV_CH), BF16), jax.ShapeDtypeStruct((CONV_K, CONV_CH), F32),
                   jax.ShapeDtypeStruct((1, CONV_CH), F32)],
        compiler_params=_params(("parallel", "arbitrary", "arbitrary")),
    )(da3, da3, x3, x3, x3, cw, cb)
    return du.reshape(bsz * seq, CONV_CH), gw, gb


def _ssd_prologue(dtf_ref, dtft_ref, bias_ref, biast_ref, arow_ref, acol_ref, dtfull_scr, acfull_scr, acr_scr):
    tri = (_iota((CHUNK, CHUNK), 1) <= _iota((CHUNK, CHUNK), 0)).astype(F32)
    triu = (_iota((CHUNK, CHUNK), 0) <= _iota((CHUNK, CHUNK), 1)).astype(F32)
    dtc = _softplus(dtf_ref[0] + bias_ref[...])
    a = dtc * arow_ref[...]
    acum = _hdot(tri, a)
    expand = _head_expand()
    dtfull_scr[...] = _hdot(dtc, expand)
    acfull_scr[...] = _hdot(acum, expand)
    dtr = _softplus(dtft_ref[0] + biast_ref[...])
    acr_scr[...] = _hdot(dtr * acol_ref[...], triu)
    return dtc, acum


def _decay_matrix(acum, acr_scr, h):
    lane = _iota((CHUNK, LANES), 1)
    ac_col = jnp.sum(jnp.where(lane == h, acum, 0.0), axis=1, keepdims=True)
    ac_row = acr_scr[h:h + 1, :]
    causal = _iota((CHUNK, CHUNK), 1) <= _iota((CHUNK, CHUNK), 0)
    return jnp.exp(jnp.where(causal, ac_col - ac_row, NEG))


def _ssd_fwd(xbc, dtf, dtft, bias, biast, arow, acol, dfull, bsz, seq):
    nc = seq // CHUNK
    x3 = xbc.reshape(bsz, seq, CONV_CH)

    def body(xbc_ref, dtf_ref, dtft_ref, bias_ref, biast_ref, arow_ref, acol_ref, dfull_ref,
             y_ref, hall_ref, h_scr, dtfull_scr, acfull_scr, acr_scr):
        @pl.when(pl.program_id(1) == 0)
        def _():
            h_scr[...] = jnp.zeros_like(h_scr)

        _, acum = _ssd_prologue(dtf_ref, dtft_ref, bias_ref, biast_ref, arow_ref, acol_ref,
                                dtfull_scr, acfull_scr, acr_scr)
        lane = _iota((CHUNK, LANES), 1)
        for g in range(2):
            bg = xbc_ref[0, :, D_MODEL + LANES * g:D_MODEL + LANES * (g + 1)]
            cg = xbc_ref[0, :, D_MODEL + 2 * LANES + LANES * g:D_MODEL + 2 * LANES + LANES * (g + 1)]
            cg_bf = cg.astype(BF16)
            gmat = _bdot_nt(cg_bf, bg)
            bgt_bf = bg.T.astype(BF16)
            for j in range(4):
                p = 4 * g + j
                sl = slice(LANES * p, LANES * (p + 1))
                xs = xbc_ref[0, :, sl]
                ac = acfull_scr[:, sl]
                acl = acfull_scr[CHUNK - 1:CHUNK, sl]
                xdt = xs * dtfull_scr[:, sl]
                xdt_bf = xdt.astype(BF16)
                hp = h_scr[p]
                hall_ref[0, 0, p] = hp
                yo = _bdot(cg_bf, hp) * jnp.exp(ac)
                yd = []
                for hh in range(2):
                    mmat = gmat * _decay_matrix(acum, acr_scr, 2 * p + hh)
                    yd.append(_bdot(mmat, xdt_bf))
                y_ref[0, :, sl] = jnp.where(lane < HEAD_DIM, yd[0], yd[1]) + yo + dfull_ref[:, sl] * xs
                h_scr[p] = hp * jnp.exp(acl) + _bdot(bgt_bf, xdt * jnp.exp(acl - ac))

    row = lambda w: pl.BlockSpec((1, w), lambda b, c: (0, 0))
    y, hall = pl.pallas_call(
        body, name="ssd_fwd", grid=(bsz, nc),
        in_specs=[pl.BlockSpec((1, CHUNK, CONV_CH), lambda b, c: (b, c, 0)),
                  pl.BlockSpec((1, CHUNK, LANES), lambda b, c: (b, c, 0)),
                  pl.BlockSpec((1, LANES, CHUNK), lambda b, c: (b, 0, c)),
                  row(LANES), pl.BlockSpec((LANES, 1), lambda b, c: (0, 0)),
                  row(LANES), pl.BlockSpec((LANES, 1), lambda b, c: (0, 0)), row(D_MODEL)],
        out_specs=[pl.BlockSpec((1, CHUNK, D_MODEL), lambda b, c: (b, c, 0)),
                   pl.BlockSpec((1, 1, HEAD_PAIRS, SSD_STATE, LANES), lambda b, c: (b, c, 0, 0, 0))],
        out_shape=[jax.ShapeDtypeStruct((bsz, seq, D_MODEL), F32),
                   jax.ShapeDtypeStruct((bsz, nc, HEAD_PAIRS, SSD_STATE, LANES), F32)],
        scratch_shapes=[pltpu.VMEM((HEAD_PAIRS, SSD_STATE, LANES), F32), pltpu.VMEM((CHUNK, D_MODEL), F32),
                        pltpu.VMEM((CHUNK, D_MODEL), F32), pltpu.VMEM((LANES, CHUNK), F32)],
        compiler_params=_params(("parallel", "arbitrary")),
    )(x3, dtf.reshape(bsz, seq, LANES), dtft, bias, biast, arow, acol, dfull)
    return y.reshape(bsz * seq, D_MODEL), hall


def _ssd_bwd(dy, xbc, dtf, dtft, bias, biast, arow, acol, dfull, hall, bsz, seq):
    nc = seq // CHUNK
    x3 = xbc.reshape(bsz, seq, CONV_CH)
    dy3 = dy.reshape(bsz, seq, D_MODEL)

    def body(dy_ref, xbc_ref, dtf_ref, dtft_ref, bias_ref, biast_ref, arow_ref, acol_ref, dfull_ref, hall_ref,
             dx_ref, ddt_ref, acc_ref, dh_scr, dtfull_scr, acfull_scr, acr_scr, csum_scr, dacf_scr, ddtf_scr, ddl_scr):
        first = (pl.program_id(0) == 0) & (pl.program_id(1) == 0)

        @pl.when(first)
        def _():
            acc_ref[...] = jnp.zeros_like(acc_ref)

        @pl.when(pl.program_id(1) == 0)
        def _():
            dh_scr[...] = jnp.zeros_like(dh_scr)

        dtc, acum = _ssd_prologue(dtf_ref, dtft_ref, bias_ref, biast_ref, arow_ref, acol_ref,
                                  dtfull_scr, acfull_scr, acr_scr)
        csum_scr[...] = jnp.zeros_like(csum_scr)
        lane = _iota((CHUNK, LANES), 1)
        last_row = _iota((CHUNK, LANES), 0) == CHUNK - 1
        rs16 = jnp.zeros((CHUNK, LANES), F32)
        for g in range(2):
            bsl = slice(D_MODEL + LANES * g, D_MODEL + LANES * (g + 1))
            csl = slice(D_MODEL + 2 * LANES + LANES * g, D_MODEL + 2 * LANES + LANES * (g + 1))
            bg_bf = xbc_ref[0, :, bsl].astype(BF16)
            cg = xbc_ref[0, :, csl]
            cg_bf = cg.astype(BF16)
            cgt_bf = cg.T.astype(BF16)
            gmat = _bdot_nt(cg_bf, bg_bf)
            dg = jnp.zeros((CHUNK, CHUNK), F32)
            dbg = jnp.zeros((CHUNK, SSD_STATE), F32)
            dcg = jnp.zeros((CHUNK, SSD_STATE), F32)
            for j in range(4):
                p = 4 * g + j
                sl = slice(LANES * p, LANES * (p + 1))
                xs = xbc_ref[0, :, sl]
                dt = dtfull_scr[:, sl]
                ac = acfull_scr[:, sl]
                acl = acfull_scr[CHUNK - 1:CHUNK, sl]
                dyp = dy_ref[0, :, sl]
                xdt = xs * dt
                xdt_bf = xdt.astype(BF16)
                e = jnp.exp(ac)
                dsd = jnp.exp(acl - ac)
                cd = jnp.exp(acl)
                xds_bf = (xdt * dsd).astype(BF16)
                hp = hall_ref[0, 0, p]
                hp_bf = hp.astype(BF16)
                dhn = dh_scr[p]
                dhn_bf = dhn.astype(BF16)
                yo = _bdot(cg_bf, hp_bf) * e
                dw_bf = (dyp * e).astype(BF16)
                dcg = dcg + _bdot_nt(dw_bf, hp_bf)
                dhin = _bdot(cgt_bf, dw_bf)
                dac = dyp * yo
                dacl = jnp.sum(dhn * hp, axis=0, keepdims=True) * cd
                dxds = _bdot(bg_bf, dhn_bf)
                dbg = dbg + _bdot_nt(xds_bf, dhn_bf)
                dxdt = dxds * dsd
                tdec = dxds * xdt * dsd
                dacl = dacl + jnp.sum(tdec, axis=0, keepdims=True)
                dac = dac - tdec
                dh_scr[p] = dhn * cd + dhin
                for hh in range(2):
                    h = 2 * p + hh
                    lm = (lane < HEAD_DIM) if hh == 0 else (lane >= HEAD_DIM)
                    dec = _decay_matrix(acum, acr_scr, h)
                    mmat = gmat * dec
                    dyh_bf = jnp.where(lm, dyp, 0.0).astype(BF16)
                    dm = _bdot_nt(dyh_bf, xdt_bf)
                    dxdt = dxdt + _bdot(mmat.T, dyh_bf)
                    dg = dg + dm * dec
                    q = dm * mmat
                    rs16 = rs16 + jnp.where(lane == h, jnp.sum(q, axis=1, keepdims=True), 0.0)
                    csum_scr[h:h + 1, :] = jnp.sum(q, axis=0, keepdims=True)
                dx_ref[0, :, sl] = dfull_ref[:, sl] * dyp + dxdt * dt
                dacf_scr[:, sl] = dac + jnp.where(last_row, dacl, 0.0)
                ddtf_scr[:, sl] = dxdt * xs
                ddl_scr[:, sl] = jnp.broadcast_to(jnp.sum(dyp * xs, axis=0, keepdims=True), (SUBLANES, LANES))
            dg_bf = dg.astype(BF16)
            dx_ref[0, :, bsl] = dbg + _bdot(dg.T, cg_bf)
            dx_ref[0, :, csl] = dcg + _bdot(dg_bf, bg_bf)
        reduce = _head_reduce()
        triu = (_iota((CHUNK, CHUNK), 0) <= _iota((CHUNK, CHUNK), 1)).astype(F32)
        dac16 = _hdot(dacf_scr[...], reduce) + rs16 - csum_scr[...].T
        da16 = _hdot(triu, dac16)
        ddt16 = da16 * arow_ref[...] + _hdot(ddtf_scr[...], reduce)
        ddtraw = ddt16 * _sigmoid(dtf_ref[0] + bias_ref[...])
        ddt_ref[0] = ddtraw
        acc_ref[0:8, :] += jnp.broadcast_to(jnp.sum(da16 * dtc * arow_ref[...], axis=0, keepdims=True), (SUBLANES, LANES))
        acc_ref[8:16, :] += _hdot(ddl_scr[...], reduce)
        acc_ref[16:24, :] += jnp.broadcast_to(jnp.sum(ddtraw, axis=0, keepdims=True), (SUBLANES, LANES))

    rev = lambda b, c: (b, nc - 1 - c, 0)
    row = lambda w: pl.BlockSpec((1, w), lambda b, c: (0, 0))
    dx, ddt, acc = pl.pallas_call(
        body, name="ssd_bwd", grid=(bsz, nc),
        in_specs=[pl.BlockSpec((1, CHUNK, D_MODEL), rev), pl.BlockSpec((1, CHUNK, CONV_CH), rev),
                  pl.BlockSpec((1, CHUNK, LANES), rev),
                  pl.BlockSpec((1, LANES, CHUNK), lambda b, c: (b, 0, nc - 1 - c)),
                  row(LANES), pl.BlockSpec((LANES, 1), lambda b, c: (0, 0)),
                  row(LANES), pl.BlockSpec((LANES, 1), lambda b, c: (0, 0)), row(D_MODEL),
                  pl.BlockSpec((1, 1, HEAD_PAIRS, SSD_STATE, LANES), lambda b, c: (b, nc - 1 - c, 0, 0, 0))],
        out_specs=[pl.BlockSpec((1, CHUNK, CONV_CH), rev), pl.BlockSpec((1, CHUNK, LANES), rev),
                   pl.BlockSpec((3 * SUBLANES, LANES), lambda b, c: (0, 0))],
        out_shape=[jax.ShapeDtypeStruct((bsz, seq, CONV_CH), F32), jax.ShapeDtypeStruct((bsz, seq, LANES), F32),
                   jax.ShapeDtypeStruct((3 * SUBLANES, LANES), F32)],
        scratch_shapes=[pltpu.VMEM((HEAD_PAIRS, SSD_STATE, LANES), F32), pltpu.VMEM((CHUNK, D_MODEL), F32),
                        pltpu.VMEM((CHUNK, D_MODEL), F32), pltpu.VMEM((LANES, CHUNK), F32),
                        pltpu.VMEM((LANES, CHUNK), F32), pltpu.VMEM((CHUNK, D_MODEL), F32),
                        pltpu.VMEM((CHUNK, D_MODEL), F32), pltpu.VMEM((SUBLANES, D_MODEL), F32)],
        compiler_params=_params(("arbitrary", "arbitrary")),
    )(dy3, x3, dtf.reshape(bsz, seq, LANES), dtft, bias, biast, arow, acol, dfull, hall)
    return dx.reshape(bsz * seq, CONV_CH), ddt.reshape(bsz * seq, LANES), acc


GROUP_W = D_MODEL // 2


def _gnorm_fwd(y, z, o_att, w):
    n = y.shape[0]
    tm = min(512, n)

    def body(y_ref, z_ref, o_ref, w_ref, out_ref):
        zv = z_ref[...]
        g = y_ref[...] * (zv * _sigmoid(zv))
        for gi in range(2):
            sl = slice(GROUP_W * gi, GROUP_W * (gi + 1))
            gs = g[:, sl]
            r = lax.rsqrt(jnp.mean(gs * gs, axis=-1, keepdims=True) + NORM_EPS)
            out_ref[:, sl] = (gs * r * w_ref[:, sl]).astype(BF16)
        out_ref[:, D_MODEL:] = o_ref[...].astype(BF16)

    tok = pl.BlockSpec((tm, D_MODEL), lambda i: (i, 0))
    return pl.pallas_call(
        body, name="gnorm_fwd", grid=(n // tm,),
        in_specs=[tok, tok, tok, pl.BlockSpec((1, D_MODEL), lambda i: (0, 0))],
        out_specs=pl.BlockSpec((tm, MIX_WIDTH), lambda i: (i, 0)),
        out_shape=jax.ShapeDtypeStruct((n, MIX_WIDTH), BF16),
        compiler_params=_params(("parallel",)),
    )(y, z, o_att, w)


def _gnorm_bwd(dycat, y, z, w):
    n = y.shape[0]
    tm = min(256, n)

    def body(dn_ref, y_ref, z_ref, w_ref, dy_ref, dz_ref, gw_ref):
        zv, yv = z_ref[...], y_ref[...]
        sz = _sigmoid(zv)
        sil = zv * sz
        g = yv * sil

        @pl.when(pl.program_id(0) == 0)
        def _():
            gw_ref[...] = jnp.zeros_like(gw_ref)

        for gi in range(2):
            sl = slice(GROUP_W * gi, GROUP_W * (gi + 1))
            gs = g[:, sl]
            r = lax.rsqrt(jnp.mean(gs * gs, axis=-1, keepdims=True) + NORM_EPS)
            nrm = gs * r
            dyn = dn_ref[:, sl]
            dn = dyn * w_ref[:, sl]
            dgs = r * (dn - nrm * jnp.mean(dn * nrm, axis=-1, keepdims=True))
            dy_ref[:, sl] = dgs * sil[:, sl]
            dz_ref[:, sl] = (dgs * yv[:, sl] * (sz[:, sl] * (1.0 + zv[:, sl] * (1.0 - sz[:, sl])))).astype(BF16)
            gw_ref[:, sl] += jnp.sum(dyn * nrm, axis=0, keepdims=True)

    tok = pl.BlockSpec((tm, D_MODEL), lambda i: (i, 0))
    row = pl.BlockSpec((1, D_MODEL), lambda i: (0, 0))
    return pl.pallas_call(
        body, name="gnorm_bwd", grid=(n // tm,),
        in_specs=[tok, tok, tok, row], out_specs=[tok, tok, row],
        out_shape=[jax.ShapeDtypeStruct((n, D_MODEL), F32), jax.ShapeDtypeStruct((n, D_MODEL), BF16),
                   jax.ShapeDtypeStruct((1, D_MODEL), F32)],
        compiler_params=_params(("arbitrary",)),
    )(dycat, y, z, w)


def _fox_prep(dtf, fb, bsz, seq):
    def body(x_ref, b_ref, c_ref):
        tri = (_iota((CHUNK, CHUNK), 1) <= _iota((CHUNK, CHUNK), 0)).astype(F32)
        carry = jnp.zeros((1, LANES), F32)
        for j in range(seq // CHUNK):
            sl = slice(CHUNK * j, CHUNK * (j + 1))
            lf = _log_sigmoid(x_ref[sl, :] + b_ref[...])
            c_ref[sl, :] = _hdot(tri, lf) + carry
            carry = carry + jnp.sum(lf, axis=0, keepdims=True)

    blk = pl.BlockSpec((seq, LANES), lambda b: (b, 0))
    return pl.pallas_call(
        body, name="fox_prep", grid=(bsz,),
        in_specs=[blk, pl.BlockSpec((1, LANES), lambda b: (0, 0))], out_specs=blk,
        out_shape=jax.ShapeDtypeStruct((bsz * seq, LANES), F32),
        compiler_params=_params(("parallel",)),
    )(dtf, fb)


def _fox_prep_bwd(dck, dcq, ddt, dtf, fb, bsz, seq):
    nj = seq // CHUNK

    def body(dck_ref, dcq_ref, ddt_ref, x_ref, b_ref, out_ref, gb_ref):
        triu = (_iota((CHUNK, CHUNK), 0) <= _iota((CHUNK, CHUNK), 1)).astype(F32)
        is_f = (_iota((CHUNK, LANES), 1) >= F_COL) & (_iota((CHUNK, LANES), 1) < 2 * F_COL)
        carry = jnp.zeros((1, LANES), F32)
        total = jnp.zeros((1, LANES), F32)
        for j in range(nj - 1, -1, -1):
            sl = slice(CHUNK * j, CHUNK * (j + 1))
            dcv = dck_ref[sl, :] + dcq_ref[sl, :]
            dlf = _hdot(triu, dcv) + carry
            carry = carry + jnp.sum(dcv, axis=0, keepdims=True)
            df = jnp.where(is_f, dlf * _sigmoid(-(x_ref[sl, :] + b_ref[...])), 0.0)
            out_ref[sl, :] = (df + ddt_ref[sl, :]).astype(BF16)
            total = total + jnp.sum(df, axis=0, keepdims=True)

        @pl.when(pl.program_id(0) == 0)
        def _():
            gb_ref[...] = jnp.zeros_like(gb_ref)

        gb_ref[...] += jnp.broadcast_to(total, (SUBLANES, LANES))

    blk = pl.BlockSpec((seq, LANES), lambda b: (b, 0))
    return pl.pallas_call(
        body, name="fox_prep_bwd", grid=(bsz,),
        in_specs=[blk, blk, blk, blk, pl.BlockSpec((1, LANES), lambda b: (0, 0))],
        out_specs=[blk, pl.BlockSpec((SUBLANES, LANES), lambda b: (0, 0))],
        out_shape=[jax.ShapeDtypeStruct((bsz * seq, LANES), BF16), jax.ShapeDtypeStruct((SUBLANES, LANES), F32)],
        compiler_params=_params(("arbitrary",)),
    )(dck, dcq, ddt, dtf, fb)


AUX_C = 3
AUX_ONE = 3


def _att_block(seq):
    return min(256, seq)


def _att_operands(q_ref, k_ref, c_ref, seq, hh, pair):
    lane = _iota((seq, LANES), 1)
    lm = (lane < HEAD_DIM) if hh == 0 else (lane >= HEAD_DIM)
    base = HEAD_DIM * (1 - hh)
    negc = -jnp.sum(jnp.where(lane == F_COL + 2 * pair + hh, c_ref[...], 0.0), axis=1, keepdims=True)
    c1 = negc.astype(BF16)
    r1 = negc - c1.astype(F32)
    c2 = r1.astype(BF16)
    c3 = (r1 - c2.astype(F32)).astype(BF16)
    zero = jnp.zeros((seq, LANES), BF16)
    one = jnp.ones((seq, LANES), BF16)
    ka = jnp.where(lm, k_ref[...], jnp.where(lane == base, c1, jnp.where(lane == base + 1, c2, jnp.where(
        lane == base + 2, c3, jnp.where(lane == base + AUX_ONE, one, zero)))))
    qa = jnp.where(lm, q_ref[...] * ATT_SCALE, jnp.where((lane >= base) & (lane < base + AUX_C), one, zero))
    return lm, base, qa, ka


def _att_fwd(qkv, ccol, bsz, seq, gather):
    blk = _att_block(seq)
    nb = seq // blk
    nx = len(gather)

    def body(*refs):
        q_ref, k_ref, v_ref, c_ref = refs[:4]
        x_refs = refs[4:4 + nx]
        o_ref, lse_ref = refs[4 + nx:6 + nx]
        xo_refs = refs[6 + nx:6 + 2 * nx]
        qa_scr, ka_scr, va_scr = refs[6 + 2 * nx:9 + 2 * nx]
        sems = refs[9 + 2 * nx:]
        pair = pl.program_id(1)

        @pl.when((pl.program_id(0) == 0) & (pair == 0))
        def _():
            _exchange_start("gather", x_refs, xo_refs, *sems)

        causal = _iota((blk, blk), 0) >= _iota((blk, blk), 1)
        for hh in range(2):
            lm, _, qa, ka = _att_operands(q_ref, k_ref, c_ref, seq, hh, pair)
            qa_scr[hh] = qa
            ka_scr[hh] = ka
            va_scr[hh] = jnp.where(lm, v_ref[...], jnp.zeros((seq, LANES), BF16))

        def q_step(i, carry0):
            rows = pl.ds(pl.multiple_of(i * blk, blk), blk)

            def scores(j):
                krows = pl.ds(pl.multiple_of(j * blk, blk), blk)
                return tuple(_bdot_nt(qa_scr[hh, rows, :], ka_scr[hh, krows, :]) for hh in range(2))

            def update(state, s_pair, j, masked):
                krows = pl.ds(pl.multiple_of(j * blk, blk), blk)
                alphas, ls, pvs, ms = [], [], [], []
                for hh in range(2):
                    m, l, _ = state[hh]
                    s = jnp.where(causal, s_pair[hh], NEG) if masked else s_pair[hh]
                    mn = jnp.maximum(m, jnp.max(s, axis=1, keepdims=True))
                    alpha = jnp.exp(m - mn)
                    pr = jnp.exp(s - mn)
                    ms.append(mn)
                    alphas.append(alpha)
                    ls.append(alpha * l + jnp.sum(pr, axis=1, keepdims=True))
                    pvs.append(_bdot(pr, va_scr[hh, krows, :]))
                return tuple((ms[hh], ls[hh], alphas[hh] * state[hh][2] + pvs[hh]) for hh in range(2))

            def step(j, carry):
                state, s_cur = carry
                s_next = scores(j + 1)
                return update(state, s_cur, j, False), s_next

            one = (jnp.full((blk, 1), NEG, F32), jnp.zeros((blk, 1), F32), jnp.zeros((blk, LANES), F32))
            state, s_last = lax.fori_loop(0, i, step, ((one, one), scores(0)))
            (m0, l0, acc0), (m1, l1, acc1) = update(state, s_last, i, True)
            o_ref[rows, :] = acc0 * (1.0 / l0) + acc1 * (1.0 / l1)
            lse_ref[0, 0, rows, :] = m0 + jnp.log(l0)
            lse_ref[0, 1, rows, :] = m1 + jnp.log(l1)
            return carry0

        lax.fori_loop(0, nb, q_step, 0)

        @pl.when((pl.program_id(0) == bsz - 1) & (pair == HEAD_PAIRS - 1))
        def _():
            _exchange_wait("gather", x_refs, xo_refs, *sems)

    col = lambda off: pl.BlockSpec((seq, LANES), lambda b, p: (b, off + p))
    head2 = pltpu.VMEM((2, seq, LANES), BF16)
    hbm = pl.BlockSpec(memory_space=pl.ANY)
    return pl.pallas_call(
        body, name="att_fwd", grid=(bsz, HEAD_PAIRS),
        in_specs=[col(0), col(HEAD_PAIRS), col(2 * HEAD_PAIRS), pl.BlockSpec((seq, LANES), lambda b, p: (b, 0))] + [hbm] * nx,
        out_specs=[pl.BlockSpec((seq, LANES), lambda b, p: (b, p)),
                   pl.BlockSpec((1, 2, seq, 1), lambda b, p: (b, p, 0, 0))] + [hbm] * nx,
        out_shape=[jax.ShapeDtypeStruct((bsz * seq, D_MODEL), F32), jax.ShapeDtypeStruct((bsz, SSD_HEADS, seq, 1), F32)]
        + _exchange_shapes("gather", gather),
        scratch_shapes=[head2, head2, head2] + _exchange_scratch(nx),
        compiler_params=_params(("arbitrary", "arbitrary")),
    )(qkv, qkv, qkv, ccol, *gather)


def _att_bwd(qkv, dycat, o, lse, ccol, bsz, seq, scatter):
    blk = _att_block(seq)
    nb = seq // blk
    nx = len(scatter)

    def body(*refs):
        q_ref, k_ref, v_ref, do_ref, o_ref, lse_ref, c_ref = refs[:7]
        x_refs = refs[7:7 + nx]
        dq_ref, dk_ref, dv_ref, dck_ref, dcq_ref = refs[7 + nx:12 + nx]
        xo_refs = refs[12 + nx:12 + 2 * nx]
        qa_scr, ka_scr, va_scr, doa_scr, d_scr, dq_scr, dk_scr, dv_scr = refs[12 + 2 * nx:20 + 2 * nx]
        sems = refs[20 + 2 * nx:]
        pair = pl.program_id(1)

        @pl.when((pl.program_id(0) == 0) & (pair == 0))
        def _():
            _exchange_start("scatter", x_refs, xo_refs, *sems)

        causal = _iota((blk, blk), 0) >= _iota((blk, blk), 1)
        lane_t = _iota((seq, LANES), 1)
        lane_b = _iota((blk, LANES), 1)
        masks, bases = [], []
        for hh in range(2):
            lm, base, qa, ka = _att_operands(q_ref, k_ref, c_ref, seq, hh, pair)
            masks.append(lm)
            bases.append(base)
            qa_scr[hh] = qa
            ka_scr[hh] = ka
            va_scr[hh] = jnp.where(lm, v_ref[...], jnp.zeros((seq, LANES), BF16))
            doa = jnp.where(lm, do_ref[...], 0.0).astype(BF16)
            doa_scr[hh] = doa
            d_scr[hh] = jnp.sum(doa.astype(F32) * o_ref[...], axis=1, keepdims=True)
        dq_scr[...] = jnp.zeros_like(dq_scr)

        def kv_step(j, carry0):
            krows = pl.ds(pl.multiple_of(j * blk, blk), blk)
            dk_scr[...] = jnp.zeros_like(dk_scr)
            dv_scr[...] = jnp.zeros_like(dv_scr)

            def scores(i):
                rows = pl.ds(pl.multiple_of(i * blk, blk), blk)
                return tuple((_bdot_nt(qa_scr[hh, rows, :], ka_scr[hh, krows, :]),
                              _bdot_nt(doa_scr[hh, rows, :], va_scr[hh, krows, :])) for hh in range(2))

            def update(i, sd, masked):
                rows = pl.ds(pl.multiple_of(i * blk, blk), blk)
                for hh in range(2):
                    s, dp = sd[hh]
                    if masked:
                        s = jnp.where(causal, s, NEG)
                    pr = jnp.exp(s - lse_ref[0, hh, rows, :])
                    ds = (pr * (dp - d_scr[hh, rows, :])).astype(BF16)
                    dv_scr[hh] += _bdot_tn(pr, doa_scr[hh, rows, :])
                    dk_scr[hh] += _bdot_tn(ds, qa_scr[hh, rows, :])
                    dq_scr[hh, rows, :] += _bdot(ds, ka_scr[hh, krows, :])

            def q_step(i, sd_cur):
                sd_next = scores(jnp.minimum(i + 1, nb - 1))
                update(i, sd_cur, False)
                return sd_next

            sd_diag = scores(j)
            sd_first = scores(jnp.minimum(j + 1, nb - 1))
            update(j, sd_diag, True)
            lax.fori_loop(j + 1, nb, q_step, sd_first)
            lmb0 = lane_b < HEAD_DIM
            dk_ref[krows, :] = jnp.where(lmb0, dk_scr[0], dk_scr[1]).astype(BF16)
            dv_ref[krows, :] = (dv_scr[0] + dv_scr[1]).astype(BF16)
            for hh in range(2):
                dck_ref[0, hh, krows, :] = -jnp.sum(jnp.where(lane_b == bases[hh], dk_scr[hh], 0.0), axis=1, keepdims=True)
            return carry0

        lax.fori_loop(0, nb, kv_step, 0)
        dq_ref[...] = (jnp.where(masks[0], dq_scr[0], dq_scr[1]) * ATT_SCALE).astype(BF16)
        for hh in range(2):
            dcq_ref[0, hh] = jnp.sum(jnp.where(lane_t == bases[hh] + AUX_ONE, dq_scr[hh], 0.0), axis=1, keepdims=True)

        @pl.when((pl.program_id(0) == bsz - 1) & (pair == HEAD_PAIRS - 1))
        def _():
            _exchange_wait("scatter", x_refs, xo_refs, *sems)

    col = lambda off: pl.BlockSpec((seq, LANES), lambda b, p: (b, off + p))
    vec = pl.BlockSpec((1, 2, seq, 1), lambda b, p: (b, p, 0, 0))
    out = jax.ShapeDtypeStruct((bsz * seq, D_MODEL), BF16)
    vec_shape = jax.ShapeDtypeStruct((bsz, SSD_HEADS, seq, 1), F32)
    head2 = pltpu.VMEM((2, seq, LANES), BF16)
    hbm = pl.BlockSpec(memory_space=pl.ANY)
    return pl.pallas_call(
        body, name="att_bwd", grid=(bsz, HEAD_PAIRS),
        in_specs=[col(0), col(HEAD_PAIRS), col(2 * HEAD_PAIRS), col(HEAD_PAIRS), col(0), vec,
                  pl.BlockSpec((seq, LANES), lambda b, p: (b, 0))] + [hbm] * nx,
        out_specs=[col(0), col(0), col(0), vec, vec] + [hbm] * nx,
        out_shape=[out, out, out, vec_shape, vec_shape] + _exchange_shapes("scatter", scatter),
        scratch_shapes=[head2, head2, head2, head2, pltpu.VMEM((2, seq, 1), F32), pltpu.VMEM((2, seq, LANES), F32),
                        pltpu.VMEM((2, blk, LANES), F32), pltpu.VMEM((2, blk, LANES), F32)] + _exchange_scratch(nx),
        compiler_params=_params(("arbitrary", "arbitrary")),
    )(qkv, qkv, qkv, dycat, o, lse, ccol, *scatter)


def _adamw_math(w, g, m, v):
    m = ADAM_B1 * m + (1.0 - ADAM_B1) * g
    v = ADAM_B2 * v + (1.0 - ADAM_B2) * jnp.square(g)
    m_hat = m / ADAM_C1
    v_hat = v / ADAM_C2
    delta = -ADAM_LR * (m_hat / (jnp.sqrt(v_hat) + ADAM_EPS) + ADAM_WD * w)
    return delta, m, v


def _row_tile(rows):
    for tr in (256, 128, 64, 32, 16, 8):
        if rows % tr == 0:
            return tr
    return rows


def _sum_parts(parts, name):
    nparts, rows, cols = parts.shape
    tr = rows if rows % SUBLANES else _row_tile(rows)

    def body(p_ref, o_ref):
        g = p_ref[0].astype(F32)
        for s in range(1, nparts):
            g = g + p_ref[s].astype(F32)
        o_ref[...] = g

    return pl.pallas_call(
        body, name=name, grid=(rows // tr,),
        in_specs=[pl.BlockSpec((nparts, tr, cols), lambda i: (0, i, 0))],
        out_specs=pl.BlockSpec((tr, cols), lambda i: (i, 0)),
        out_shape=jax.ShapeDtypeStruct((rows, cols), F32),
        compiler_params=_params(("parallel",)),
    )(parts)


def _adamw_parts(parts, w, m, v, name):
    nparts, rows, cols = parts.shape
    tr = _row_tile(rows)

    def body(p_ref, w_ref, m_ref, v_ref, g_ref, d_ref, mo_ref, vo_ref):
        g = p_ref[0].astype(F32)
        for s in range(1, nparts):
            g = g + p_ref[s].astype(F32)
        delta, mn, vn = _adamw_math(w_ref[...], g, m_ref[...], v_ref[...])
        g_ref[...] = g
        d_ref[...] = delta
        mo_ref[...] = mn
        vo_ref[...] = vn

    blk = pl.BlockSpec((tr, cols), lambda i: (i, 0))
    shp = jax.ShapeDtypeStruct((rows, cols), F32)
    return pl.pallas_call(
        body, name=name, grid=(rows // tr,),
        in_specs=[pl.BlockSpec((nparts, tr, cols), lambda i: (0, i, 0)), blk, blk, blk],
        out_specs=[blk, blk, blk, blk], out_shape=[shp, shp, shp, shp],
        compiler_params=_params(("parallel",)),
    )(parts, w, m, v)


def _me():
    return lax.axis_index("x"), lax.axis_index("y"), lax.axis_index("c")


def _flip(pos, k):
    x, y, c = pos
    kx, ky, kc = (k >> 2) & 1, (k >> 1) & 1, k & 1
    return (x ^ kx if kx else x, y ^ ky if ky else y, c ^ kc if kc else c)


def _logical(pos):
    return 4 * pos[0] + 2 * pos[1] + pos[2]


def _all_gather_two_level(shards):
    narr = len(shards)

    def body(*refs):
        x_refs, out_refs = refs[:narr], refs[narr:2 * narr]
        send_sems, recv_sems, local_sems = refs[2 * narr:]
        x, y, c = _me()
        me, sibling = (x, y, c), (x, y, 1 - c)
        chips = [(1 - x, y), (x, 1 - y), (1 - x, 1 - y)]

        def copy(a, k, block, to, src=None):
            slot = out_refs[a].at[_logical(block)]
            return pltpu.make_async_remote_copy(
                src_ref=slot if src is None else src, dst_ref=slot,
                send_sem=send_sems.at[a, k], recv_sem=recv_sems.at[a, k], device_id=to, device_id_type=MESH)

        mine = [pltpu.make_async_copy(x_refs[a], out_refs[a].at[_logical(me)], local_sems.at[a]) for a in range(narr)]
        for cp in mine:
            cp.start()
        first = []
        for a in range(narr):
            first.append(copy(a, 0, me, sibling, src=x_refs[a]))
            first += [copy(a, 1 + j, me, (*chip, c), src=x_refs[a]) for j, chip in enumerate(chips)]
        for cp in first:
            cp.start()
        passed = []
        for a in range(narr):
            for j, chip in enumerate(chips):
                copy(a, 1 + j, (*chip, c), me).wait_recv()
                fwd = copy(a, 4 + j, (*chip, c), sibling)
                fwd.start()
                passed.append(fwd)
        for a in range(narr):
            copy(a, 0, sibling, me).wait_recv()
            for j, chip in enumerate(chips):
                copy(a, 4 + j, (*chip, 1 - c), me).wait_recv()
        for cp in first + passed:
            cp.wait_send()
        for cp in mine:
            cp.wait()

    hbm = pl.BlockSpec(memory_space=pl.ANY)
    return pl.pallas_call(
        body, name="all_gather_big",
        out_shape=[jax.ShapeDtypeStruct((N_DEV,) + s.shape, s.dtype) for s in shards],
        in_specs=[hbm] * narr, out_specs=[hbm] * narr,
        scratch_shapes=[pltpu.SemaphoreType.DMA((narr, 7)), pltpu.SemaphoreType.DMA((narr, 7)),
                        pltpu.SemaphoreType.DMA((narr,))],
    )(*shards)


def _exchange_copies(kind, x_refs, out_refs, send_sems, recv_sems, local_sems):
    me = _me()
    mine = _logical(me)
    local, remote = [], []
    for a, (x_ref, out_ref) in enumerate(zip(x_refs, out_refs)):
        own = x_ref if kind == "gather" else x_ref.at[mine]
        local.append(pltpu.make_async_copy(own, out_ref.at[mine], local_sems.at[a]))
        for k in range(1, N_DEV):
            peer = _flip(me, k)
            src = x_ref if kind == "gather" else x_ref.at[_logical(peer)]
            remote.append(pltpu.make_async_remote_copy(
                src_ref=src, dst_ref=out_ref.at[mine], send_sem=send_sems.at[a, k - 1], recv_sem=recv_sems.at[a, k - 1],
                device_id=peer, device_id_type=MESH))
    return local, remote


def _exchange_start(kind, x_refs, out_refs, send_sems, recv_sems, local_sems):
    local, remote = _exchange_copies(kind, x_refs, out_refs, send_sems, recv_sems, local_sems)
    for cp in local + remote:
        cp.start()


def _exchange_wait(kind, x_refs, out_refs, send_sems, recv_sems, local_sems):
    local, remote = _exchange_copies(kind, x_refs, out_refs, send_sems, recv_sems, local_sems)
    for cp in remote:
        cp.wait_recv()
    for cp in remote:
        cp.wait_send()
    for cp in local:
        cp.wait()


def _exchange_shapes(kind, arrays):
    return [jax.ShapeDtypeStruct(((N_DEV,) if kind == "gather" else ()) + a.shape, a.dtype) for a in arrays]


def _exchange_scratch(narrays):
    if not narrays:
        return []
    return [pltpu.SemaphoreType.DMA((narrays, N_DEV - 1)), pltpu.SemaphoreType.DMA((narrays, N_DEV - 1)),
            pltpu.SemaphoreType.DMA((narrays,))]


def _exchange_rows(x_ref, buf_ref, send_sems, recv_sems):
    me = _me()
    buf_ref[_logical(me)] = x_ref[...]
    copies = []
    for k in range(1, N_DEV):
        peer = _flip(me, k)
        copies.append(pltpu.make_async_remote_copy(
            src_ref=x_ref, dst_ref=buf_ref.at[_logical(me)],
            send_sem=send_sems.at[k - 1], recv_sem=recv_sems.at[k - 1], device_id=peer, device_id_type=MESH))
    for cp in copies:
        cp.start()
    for cp in copies:
        cp.wait_recv()
    for cp in copies:
        cp.wait_send()


def _sum_slots(buf_ref):
    total = buf_ref[0]
    for s in range(1, N_DEV):
        total = total + buf_ref[s]
    return total


def _small_gather(x):
    def body(x_ref, o_ref, buf_ref, send_sems, recv_sems):
        _exchange_rows(x_ref, buf_ref, send_sems, recv_sems)
        o_ref[...] = _sum_slots(buf_ref)

    return pl.pallas_call(
        body, name="small_gather", out_shape=jax.ShapeDtypeStruct(x.shape, F32),
        in_specs=[pl.BlockSpec(memory_space=pltpu.VMEM)], out_specs=pl.BlockSpec(memory_space=pltpu.VMEM),
        scratch_shapes=[pltpu.VMEM((N_DEV,) + x.shape, F32), pltpu.SemaphoreType.DMA((7,)), pltpu.SemaphoreType.DMA((7,))],
    )(x)


def _small_reduce_adamw(g, w, m, v):
    def body(g_ref, w_ref, m_ref, v_ref, go_ref, d_ref, mo_ref, vo_ref, buf_ref, send_sems, recv_sems):
        _exchange_rows(g_ref, buf_ref, send_sems, recv_sems)
        gs = _sum_slots(buf_ref)
        delta, mn, vn = _adamw_math(w_ref[...], gs, m_ref[...], v_ref[...])
        go_ref[...] = gs
        d_ref[...] = delta
        mo_ref[...] = mn
        vo_ref[...] = vn

    vm = pl.BlockSpec(memory_space=pltpu.VMEM)
    shp = jax.ShapeDtypeStruct(g.shape, F32)
    return pl.pallas_call(
        body, name="small_reduce_adamw", out_shape=[shp, shp, shp, shp],
        in_specs=[vm, vm, vm, vm], out_specs=[vm, vm, vm, vm],
        scratch_shapes=[pltpu.VMEM((N_DEV,) + g.shape, F32), pltpu.SemaphoreType.DMA((7,)), pltpu.SemaphoreType.DMA((7,))],
    )(g, w, m, v)


def _pad_cols(a, width):
    return jnp.pad(a, ((0, 0), (0, width - a.shape[1])))


def _local_step(x, target, norm_mix_w, w_in_t, conv_w, conv_b, dt_bias, a_log, d_skip, ssd_norm_w, f_bias,
                late_shards, norm_mlp_w, norm_final_w):
    bsz, seq, _ = x.shape
    n = bsz * seq
    x2 = x.reshape(n, D_MODEL)
    t2 = target.reshape(n, D_MODEL)
    nfw = norm_final_w.reshape(1, D_MODEL)

    bias = _pad_cols(dt_bias, LANES)
    arow = _pad_cols(-jnp.exp(a_log), LANES)
    biast, acol = bias.reshape(LANES, 1), arow.reshape(LANES, 1)
    dfull = jnp.repeat(d_skip, HEAD_DIM, axis=1)
    fb = jnp.pad(f_bias, ((0, 0), (F_COL, LANES - 2 * F_COL)))

    wt_z, wt_xbc, wt_qkv = w_in_t[:ROW_XBC], w_in_t[ROW_XBC:ROW_DT], w_in_t[ROW_Q:ROW_F]
    wt_dtf = jnp.concatenate([w_in_t[ROW_DT:ROW_Q], w_in_t[ROW_F:], jnp.zeros((LANES - 2 * F_COL, D_MODEL), BF16)], axis=0)
    wt_q, wt_k, wt_v = wt_qkv[:D_MODEL], wt_qkv[D_MODEL:2 * D_MODEL], wt_qkv[2 * D_MODEL:]

    h1 = _rms_fwd(x2, norm_mix_w, "rms_fwd_mix")
    z = _mm([(h1, wt_z)], "inproj_z", F32, 1024, 1024, nt=True)
    xbc_raw = _mm([(h1, wt_xbc)], "inproj_xbc", F32, 512, 1536, nt=True)
    qkv = _mm([(h1, wt_qkv)], "inproj_qkv", BF16, 512, 3072, nt=True)
    dtf = _mm([(h1, wt_dtf)], "inproj_dtf", F32, 2048, LANES, nt=True)
    dtft = dtf.reshape(bsz, seq, LANES).transpose(0, 2, 1)

    xbc = _conv_fwd(xbc_raw, conv_w, conv_b, bsz, seq)
    y_pre, hall = _ssd_fwd(xbc, dtf, dtft, bias, biast, arow, acol, dfull, bsz, seq)

    ccol = _fox_prep(dtf, fb, bsz, seq)
    o_att, lse, w_out, w_up_t, w_down = _att_fwd(qkv, ccol, bsz, seq, late_shards)
    w_out, w_up_t, w_down = (w.reshape(-1, D_MODEL) for w in (w_out, w_up_t, w_down))

    ycat = _gnorm_fwd(y_pre, z, o_att, ssd_norm_w)
    h2 = _mm([(ycat, w_out)], "outproj", F32, 512, 1024, res=x2)
    h2n = _rms_fwd(h2, norm_mlp_w, "rms_fwd_mlp")
    pre = _mm([(h2n, w_up_t)], "mlp_up", F32, 256, 4096, nt=True)
    h3 = _mlp_down_fwd(pre, w_down, h2)

    dh3, loss_row, g_nfw = _loss_bwd(h3, t2, nfw)
    dpre, u = _mlp_down_bwd(dh3, w_down, pre)
    g_w_down = _mm_tn(u, dh3, "grad_w_down", 1024, 1024, 2048)
    g_w_up_t = _mm_tn(dpre, h2n, "grad_w_up", 1024, 1024, 2048)
    by_shard = lambda g: g.reshape(N_DEV, g.shape[0] // N_DEV, D_MODEL)
    dh2n = _mm([(dpre, w_up_t)], "mlp_up_bwd", F32, 512, 1024)
    dh2, g_nmlp = _rms_bwd(dh2n, h2, norm_mlp_w, dh3, "rms_bwd_mlp")

    g_w_out = _mm_tn(ycat, dh2, "grad_w_out", 1024, 1024, 2048)
    dycat = _mm([(dh2, w_out)], "outproj_bwd", F32, 512, 2048, nt=True)
    dy_pre, dz, g_ssdn = _gnorm_bwd(dycat, y_pre, z, ssd_norm_w)
    dq, dk, dv, dck, dcq, recv_down, recv_up_t, recv_out = _att_bwd(
        qkv, dycat, o_att, lse, ccol, bsz, seq, [by_shard(g_w_down), by_shard(g_w_up_t), by_shard(g_w_out)])

    dxbc, ddt, ssd_acc = _ssd_bwd(dy_pre, xbc, dtf, dtft, bias, biast, arow, acol, dfull, hall, bsz, seq)
    dxbc_raw, g_cw, g_cb = _conv_bwd(dxbc, xbc_raw, conv_w, conv_b, bsz, seq)

    def head_cols(vec):
        cols = vec.reshape(bsz, SSD_HEADS, seq).transpose(0, 2, 1).reshape(n, SSD_HEADS)
        return jnp.pad(cols, ((0, 0), (F_COL, LANES - 2 * F_COL)))

    ddtf, g_fb = _fox_prep_bwd(head_cols(dck), head_cols(dcq), ddt, dtf, fb, bsz, seq)

    g_dtf = _mm_tn(ddtf, h1, "grad_w_in_dtf", LANES, 1024, 2048)
    g_w_in_t = jnp.concatenate([
        _mm_tn(dz, h1, "grad_w_in_z", 1024, 1024, 2048), _mm_tn(dxbc_raw, h1, "grad_w_in_xbc", 768, 1024, 2048),
        g_dtf[:F_COL], _mm_tn(dq, h1, "grad_w_in_q", 1024, 1024, 2048), _mm_tn(dk, h1, "grad_w_in_k", 1024, 1024, 2048),
        _mm_tn(dv, h1, "grad_w_in_v", 1024, 1024, 2048), g_dtf[F_COL:2 * F_COL]], axis=0)
    pieces = [(dz, wt_z), (dxbc_raw, wt_xbc), (dq, wt_q), (dk, wt_k), (dv, wt_v), (ddtf, wt_dtf)]
    dh1, recv_in_t = _mm(pieces, "inproj_bwd", F32, 256, 1024, exchange=("scatter", [by_shard(g_w_in_t)]))
    dx, g_nmix = _rms_bwd(dh1, x2, norm_mix_w, dh2, "rms_bwd_mix")

    small = dict(
        norm_mix_w=g_nmix, norm_mlp_w=g_nmlp, norm_final_w=g_nfw, ssd_norm_w=g_ssdn, conv_b=g_cb, conv_w=g_cw,
        dt_bias=ssd_acc[16:17, :SSD_HEADS], a_log=ssd_acc[0:1, :SSD_HEADS], d_skip=ssd_acc[8:9, :SSD_HEADS],
        f_bias=g_fb[0:1, F_COL:2 * F_COL])
    recv = dict(w_in_t=recv_in_t, w_out=recv_out, w_up_t=recv_up_t, w_down=recv_down)
    return loss_row[0, 0], dx.reshape(bsz, seq, D_MODEL), small, recv


SMALL_LAYOUT = (("norm_mix_w", 1, D_MODEL), ("norm_mlp_w", 1, D_MODEL), ("norm_final_w", 1, D_MODEL),
                ("ssd_norm_w", 1, D_MODEL), ("conv_b", 1, CONV_CH), ("conv_w", CONV_K, CONV_CH),
                ("dt_bias", 1, SSD_HEADS), ("a_log", 1, SSD_HEADS), ("d_skip", 1, SSD_HEADS), ("f_bias", 1, SSD_HEADS))


def _pack_small(vals):
    tiles = []
    for name, nrows, width in SMALL_LAYOUT:
        tiles.append(jnp.pad(vals[name].reshape(nrows, width), ((0, SUBLANES - nrows), (0, SMALL_COLS - width))))
    return jnp.concatenate(tiles, axis=0)


def _unpack_small(packed, like):
    out = {}
    for i, (name, nrows, width) in enumerate(SMALL_LAYOUT):
        out[name] = packed[SUBLANES * i:SUBLANES * i + nrows, :width].reshape(like[name].shape)
    return out


def kernel(x, norm_mix_w, w_in, conv_w, conv_b, dt_bias, a_log, d_skip, ssd_norm_w, f_bias, w_out, norm_mlp_w, w_up, w_down, norm_final_w, loss_target, m_norm_mix_w, m_w_in, m_conv_w, m_conv_b, m_dt_bias, m_a_log, m_d_skip, m_ssd_norm_w, m_f_bias, m_w_out, m_norm_mlp_w, m_w_up, m_w_down, m_norm_final_w, v_norm_mix_w, v_w_in, v_conv_w, v_conv_b, v_dt_bias, v_a_log, v_d_skip, v_ssd_norm_w, v_f_bias, v_w_out, v_norm_mlp_w, v_w_up, v_w_down, v_norm_final_w):
    me = 4 * lax.axis_index("x") + 2 * lax.axis_index("y") + lax.axis_index("c")
    conv_shard_w = CONV_CH // N_DEV
    zero = jnp.zeros((), jnp.int32)

    def place_conv(shard):
        return lax.dynamic_update_slice(jnp.zeros((CONV_K, CONV_CH), F32), shard[0], (zero, me * conv_shard_w))

    (g_in_t,) = _all_gather_two_level([w_in[0].T.astype(BF16)])
    late_shards = [w_out[0].astype(BF16), w_up[0].T.astype(BF16), w_down[0].astype(BF16)]
    conv_full = _small_gather(jnp.pad(place_conv(conv_w), ((0, SUBLANES - CONV_K), (0, 0))))[:CONV_K]

    loss_local, grad_x, g_small, recv = _local_step(
        x, loss_target, norm_mix_w, g_in_t.reshape(IN_WIDTH, D_MODEL), conv_full, conv_b, dt_bias, a_log, d_skip,
        ssd_norm_w, f_bias, late_shards, norm_mlp_w, norm_final_w)
    loss = lax.psum(loss_local, MESH_AXES)

    grad_in = _sum_parts(recv["w_in_t"], "sum_w_in").T[None]
    grad_up = _sum_parts(recv["w_up_t"], "sum_w_up").T[None]
    big = {
        "w_in": _adamw_parts(grad_in, w_in[0], m_w_in[0], v_w_in[0], "adamw_w_in"),
        "w_out": _adamw_parts(recv["w_out"], w_out[0], m_w_out[0], v_w_out[0], "adamw_w_out"),
        "w_up": _adamw_parts(grad_up, w_up[0], m_w_up[0], v_w_up[0], "adamw_w_up"),
        "w_down": _adamw_parts(recv["w_down"], w_down[0], m_w_down[0], v_w_down[0], "adamw_w_down"),
    }

    like = dict(norm_mix_w=norm_mix_w, norm_mlp_w=norm_mlp_w, norm_final_w=norm_final_w, ssd_norm_w=ssd_norm_w,
                conv_b=conv_b, conv_w=jnp.zeros((1, CONV_K, CONV_CH), F32), dt_bias=dt_bias, a_log=a_log,
                d_skip=d_skip, f_bias=f_bias)
    w_small = dict(like, conv_w=place_conv(conv_w))
    m_small = dict(norm_mix_w=m_norm_mix_w, norm_mlp_w=m_norm_mlp_w, norm_final_w=m_norm_final_w,
                   ssd_norm_w=m_ssd_norm_w, conv_b=m_conv_b, conv_w=place_conv(m_conv_w), dt_bias=m_dt_bias,
                   a_log=m_a_log, d_skip=m_d_skip, f_bias=m_f_bias)
    v_small = dict(norm_mix_w=v_norm_mix_w, norm_mlp_w=v_norm_mlp_w, norm_final_w=v_norm_final_w,
                   ssd_norm_w=v_ssd_norm_w, conv_b=v_conv_b, conv_w=place_conv(v_conv_w), dt_bias=v_dt_bias,
                   a_log=v_a_log, d_skip=v_d_skip, f_bias=v_f_bias)
    small = _small_reduce_adamw(_pack_small(g_small), _pack_small(w_small), _pack_small(m_small), _pack_small(v_small))
    small = [_unpack_small(s, like) for s in small]
    for s in small:
        s["conv_w"] = lax.dynamic_slice(s["conv_w"], (zero, zero, me * conv_shard_w), (1, CONV_K, conv_shard_w))

    order = ["norm_mix_w", "w_in", "conv_w", "conv_b", "dt_bias", "a_log", "d_skip", "ssd_norm_w", "f_bias", "w_out",
             "norm_mlp_w", "w_up", "w_down", "norm_final_w"]
    outs = [loss, grad_x]
    for kind in range(4):
        for name in order:
            outs.append(big[name][kind][None] if name in big else small[kind][name])
    return tuple(outs)
```

```python
import jax
import jax.numpy as jnp
from jax import lax
from jax.experimental import pallas as pl
from jax.experimental.pallas import tpu as pltpu

F32, BF16 = jnp.float32, jnp.bfloat16
HI = lax.Precision.HIGHEST

D_MODEL = 1024
SSD_HEADS = 16
HEAD_DIM = 64
SSD_STATE = 128
CHUNK = 128
CONV_CH = 1536
CONV_K = 4
D_FF = 4096
MIX_WIDTH = 2048
IN_WIDTH = 5664
NORM_EPS = 1e-5
ATT_SCALE = 0.125

LANES = 128
SUBLANES = 8
HEAD_PAIRS = SSD_HEADS // 2
NEG = -1e30
VMEM_LIMIT = 52 * 1024 * 1024

ROW_XBC, ROW_DT, ROW_Q, ROW_F = 1024, 2560, 2576, 5648
F_COL = SSD_HEADS

N_DEV = 8
MESH_AXES = ("x", "y", "c")
MESH = pl.DeviceIdType.MESH

ADAM_LR, ADAM_B1, ADAM_B2, ADAM_EPS, ADAM_WD, ADAM_STEP = 0.001, 0.9, 0.999, 1e-08, 0.01, 10
ADAM_C1 = 1.0 - ADAM_B1 ** ADAM_STEP
ADAM_C2 = 1.0 - ADAM_B2 ** ADAM_STEP

SMALL_COLS = CONV_CH


def _params(sem=None):
    return pltpu.CompilerParams(dimension_semantics=sem, vmem_limit_bytes=VMEM_LIMIT)


def _sigmoid(u):
    return 1.0 / (1.0 + jnp.exp(-u))


def _softplus(u):
    return jnp.maximum(u, 0.0) + jnp.log(1.0 + jnp.exp(-jnp.abs(u)))


def _log_sigmoid(u):
    return jnp.minimum(u, 0.0) - jnp.log(1.0 + jnp.exp(-jnp.abs(u)))


def _bdot(a, b):
    return jnp.dot(a.astype(BF16), b.astype(BF16), preferred_element_type=F32)


def _bdot_nt(a, b):
    return lax.dot_general(a.astype(BF16), b.astype(BF16), (((1,), (1,)), ((), ())), preferred_element_type=F32)


def _bdot_tn(a, b):
    return lax.dot_general(a.astype(BF16), b.astype(BF16), (((0,), (0,)), ((), ())), preferred_element_type=F32)


def _split3(x):
    x1 = x.astype(BF16)
    r1 = x - x1.astype(F32)
    x2 = r1.astype(BF16)
    return x1, x2, (r1 - x2.astype(F32)).astype(BF16)


def _dot_sel(x, sel):
    s = sel.astype(BF16)
    p1, p2, p3 = (jnp.dot(p, s, preferred_element_type=F32) for p in _split3(x))
    return p1 + p2 + p3


def _sel_dot(sel, x):
    s = sel.astype(BF16)
    p1, p2, p3 = (jnp.dot(s, p, preferred_element_type=F32) for p in _split3(x))
    return p1 + p2 + p3


def _iota(shape, dim):
    return lax.broadcasted_iota(jnp.int32, shape, dim)


def _head_expand():
    return (jnp.right_shift(_iota((LANES, D_MODEL), 1), 6) == _iota((LANES, D_MODEL), 0)).astype(F32)


def _head_reduce():
    return (jnp.right_shift(_iota((D_MODEL, LANES), 0), 6) == _iota((D_MODEL, LANES), 1)).astype(F32)


def _rms_fwd(x, w, name):
    n, d = x.shape
    tm = min(512, n)

    def body(x_ref, w_ref, o_ref):
        xv = x_ref[...]
        r = lax.rsqrt(jnp.mean(xv * xv, axis=-1, keepdims=True) + NORM_EPS)
        o_ref[...] = (xv * r * w_ref[...]).astype(BF16)

    return pl.pallas_call(
        body, name=name, grid=(n // tm,),
        in_specs=[pl.BlockSpec((tm, d), lambda i: (i, 0)), pl.BlockSpec((1, d), lambda i: (0, 0))],
        out_specs=pl.BlockSpec((tm, d), lambda i: (i, 0)),
        out_shape=jax.ShapeDtypeStruct((n, d), BF16),
        compiler_params=_params(("parallel",)),
    )(x, w)


def _rms_bwd(dn, x, w, dres, name):
    n, d = x.shape
    tm = min(256, n)

    def body(dn_ref, x_ref, w_ref, dres_ref, dx_ref, gw_ref):
        xv = x_ref[...]
        r = lax.rsqrt(jnp.mean(xv * xv, axis=-1, keepdims=True) + NORM_EPS)
        nrm = xv * r
        dnv = dn_ref[...]
        g = dnv * w_ref[...]
        dx_ref[...] = dres_ref[...] + r * (g - nrm * jnp.mean(g * nrm, axis=-1, keepdims=True))

        @pl.when(pl.program_id(0) == 0)
        def _():
            gw_ref[...] = jnp.zeros_like(gw_ref)

        gw_ref[...] += jnp.sum(dnv * nrm, axis=0, keepdims=True)

    tok = pl.BlockSpec((tm, d), lambda i: (i, 0))
    row = pl.BlockSpec((1, d), lambda i: (0, 0))
    return pl.pallas_call(
        body, name=name, grid=(n // tm,),
        in_specs=[tok, tok, row, tok], out_specs=[tok, row],
        out_shape=[jax.ShapeDtypeStruct((n, d), F32), jax.ShapeDtypeStruct((1, d), F32)],
        compiler_params=_params(("arbitrary",)),
    )(dn, x, w, dres)


def _mm(pairs, name, out_dtype, tm, tn, nt=False, res=None, exchange=None):
    m = pairs[0][0].shape[0]
    n = pairs[0][1].shape[0 if nt else 1]
    tm, tn = min(tm, m), min(tn, n)
    gm, gn = m // tm, n // tn
    npairs = len(pairs)
    n_in = 2 * npairs + (res is not None)
    kind, xs = (None, []) if exchange is None else exchange
    nx = len(xs)

    def body(*refs):
        x_refs, o_ref = refs[n_in:n_in + nx], refs[n_in + nx]
        xo_refs, sems = refs[n_in + nx + 1:n_in + 2 * nx + 1], refs[n_in + 2 * nx + 1:]
        i, j = pl.program_id(0), pl.program_id(1)
        if nx:
            @pl.when((i == 0) & (j == 0))
            def _():
                _exchange_start(kind, x_refs, xo_refs, *sems)

        acc = None if res is None else refs[2 * npairs][...]
        for p in range(npairs):
            a_v, b_v = refs[2 * p][...], refs[2 * p + 1][...]
            d = _bdot_nt(a_v, b_v) if nt else _bdot(a_v, b_v)
            acc = d if acc is None else acc + d
        o_ref[...] = acc.astype(o_ref.dtype)
        if nx:
            @pl.when((i == gm - 1) & (j == gn - 1))
            def _():
                _exchange_wait(kind, x_refs, xo_refs, *sems)

    in_specs, args = [], []
    for a, b in pairs:
        k = a.shape[1]
        in_specs.append(pl.BlockSpec((tm, k), lambda i, j: (i, 0)))
        in_specs.append(pl.BlockSpec((tn, k), lambda i, j: (j, 0)) if nt else pl.BlockSpec((k, tn), lambda i, j: (0, j)))
        args += [a, b]
    if res is not None:
        in_specs.append(pl.BlockSpec((tm, tn), lambda i, j: (i, j)))
        args.append(res)
    hbm = pl.BlockSpec(memory_space=pl.ANY)
    outs = pl.pallas_call(
        body, name=name, grid=(gm, gn), in_specs=in_specs + [hbm] * nx,
        out_specs=[pl.BlockSpec((tm, tn), lambda i, j: (i, j))] + [hbm] * nx,
        out_shape=[jax.ShapeDtypeStruct((m, n), out_dtype)] + _exchange_shapes(kind, xs),
        scratch_shapes=_exchange_scratch(nx),
        compiler_params=_params(("arbitrary", "arbitrary") if nx else ("parallel", "parallel")),
    )(*args, *xs)
    return outs if nx else outs[0]


def _mm_tn(a, b, name, tm, tn, tk):
    t, m = a.shape
    n = b.shape[1]
    tm, tn, tk = min(tm, m), min(tn, n), min(tk, t)
    nk = t // tk

    def body(a_ref, b_ref, o_ref, acc_ref):
        kk = pl.program_id(2)

        @pl.when(kk == 0)
        def _():
            acc_ref[...] = jnp.zeros_like(acc_ref)

        acc_ref[...] += _bdot_tn(a_ref[...], b_ref[...])

        @pl.when(kk == nk - 1)
        def _():
            o_ref[...] = acc_ref[...].astype(o_ref.dtype)

    return pl.pallas_call(
        body, name=name, grid=(m // tm, n // tn, nk),
        in_specs=[pl.BlockSpec((tk, tm), lambda i, j, kk: (kk, i)), pl.BlockSpec((tk, tn), lambda i, j, kk: (kk, j))],
        out_specs=pl.BlockSpec((tm, tn), lambda i, j, kk: (i, j)),
        out_shape=jax.ShapeDtypeStruct((m, n), BF16),
        scratch_shapes=[pltpu.VMEM((tm, tn), F32)],
        compiler_params=_params(("parallel", "parallel", "arbitrary")),
    )(a, b)


def _mlp_down_fwd(pre, w_down, h2):
    m, k = pre.shape
    n = w_down.shape[1]
    tm, tn = min(256, m), n

    def body(p_ref, b_ref, r_ref, o_ref):
        u = jnp.square(jnp.maximum(p_ref[...], 0.0))
        o_ref[...] = r_ref[...] + _bdot(u, b_ref[...])

    return pl.pallas_call(
        body, name="mlp_down_fwd", grid=(m // tm, n // tn),
        in_specs=[pl.BlockSpec((tm, k), lambda i, j: (i, 0)), pl.BlockSpec((k, tn), lambda i, j: (0, j)),
                  pl.BlockSpec((tm, tn), lambda i, j: (i, j))],
        out_specs=pl.BlockSpec((tm, tn), lambda i, j: (i, j)),
        out_shape=jax.ShapeDtypeStruct((m, n), F32),
        compiler_params=_params(("parallel", "parallel")),
    )(pre, w_down, h2)


def _mlp_down_bwd(dh3, w_down, pre):
    m, k = dh3.shape
    n = w_down.shape[0]
    tm, tn = min(256, m), n

    def body(a_ref, b_ref, p_ref, dpre_ref, u_ref):
        r = jnp.maximum(p_ref[...], 0.0)
        du = _bdot_nt(a_ref[...], b_ref[...])
        dpre_ref[...] = (du * (2.0 * r)).astype(BF16)
        u_ref[...] = (r * r).astype(BF16)

    blk = pl.BlockSpec((tm, tn), lambda i, j: (i, j))
    return pl.pallas_call(
        body, name="mlp_down_bwd", grid=(m // tm, n // tn),
        in_specs=[pl.BlockSpec((tm, k), lambda i, j: (i, 0)), pl.BlockSpec((tn, k), lambda i, j: (j, 0)), blk],
        out_specs=[blk, blk],
        out_shape=[jax.ShapeDtypeStruct((m, n), BF16), jax.ShapeDtypeStruct((m, n), BF16)],
        compiler_params=_params(("parallel", "parallel")),
    )(dh3, w_down, pre)


def _loss_bwd(h3, target, w):
    n, d = h3.shape
    tm = min(256, n)

    def body(h_ref, t_ref, w_ref, dh_ref, loss_ref, gw_ref):
        xv = h_ref[...]
        r = lax.rsqrt(jnp.mean(xv * xv, axis=-1, keepdims=True) + NORM_EPS)
        nrm = xv * r
        wv = w_ref[...]
        err = nrm * wv - t_ref[...]
        part = 0.5 * jnp.sum(jnp.mean(err * err, axis=-1, keepdims=True), axis=0, keepdims=True)
        dy = err * (1.0 / d)
        g = dy * wv
        dh_ref[...] = r * (g - nrm * jnp.mean(g * nrm, axis=-1, keepdims=True))

        @pl.when(pl.program_id(0) == 0)
        def _():
            loss_ref[...] = jnp.zeros_like(loss_ref)
            gw_ref[...] = jnp.zeros_like(gw_ref)

        loss_ref[...] += jnp.broadcast_to(part, loss_ref.shape)
        gw_ref[...] += jnp.sum(dy * nrm, axis=0, keepdims=True)

    tok = pl.BlockSpec((tm, d), lambda i: (i, 0))
    row = pl.BlockSpec((1, d), lambda i: (0, 0))
    return pl.pallas_call(
        body, name="loss_bwd", grid=(n // tm,),
        in_specs=[tok, tok, row], out_specs=[tok, pl.BlockSpec((1, LANES), lambda i: (0, 0)), row],
        out_shape=[jax.ShapeDtypeStruct((n, d), F32), jax.ShapeDtypeStruct((1, LANES), F32),
                   jax.ShapeDtypeStruct((1, d), F32)],
        compiler_params=_params(("arbitrary",)),
    )(h3, target, w)


CONV_TC = 512


def _conv_fwd(xbc, cw, cb, bsz, seq):
    tt = min(256, seq)
    nt, hb = seq // tt, tt // SUBLANES
    x3 = xbc.reshape(bsz, seq, CONV_CH)

    def body(xm_ref, xh_ref, w_ref, b_ref, o_ref):
        i = pl.program_id(2)
        x = xm_ref[0]
        halo = jnp.where(i > 0, xh_ref[0], 0.0)
        xx = jnp.concatenate([halo, x], axis=0)
        acc = x * w_ref[3:4, :] + b_ref[...]
        for s in range(1, CONV_K):
            acc = acc + pltpu.roll(xx, s, 0)[SUBLANES:, :] * w_ref[3 - s:4 - s, :]
        o_ref[0] = acc * _sigmoid(acc)

    out = pl.pallas_call(
        body, name="conv_fwd", grid=(CONV_CH // CONV_TC, bsz, nt),
        in_specs=[pl.BlockSpec((1, tt, CONV_TC), lambda c, b, i: (b, i, c)),
                  pl.BlockSpec((1, SUBLANES, CONV_TC), lambda c, b, i: (b, jnp.maximum(i * hb - 1, 0), c)),
                  pl.BlockSpec((CONV_K, CONV_TC), lambda c, b, i: (0, c)),
                  pl.BlockSpec((1, CONV_TC), lambda c, b, i: (0, c))],
        out_specs=pl.BlockSpec((1, tt, CONV_TC), lambda c, b, i: (b, i, c)),
        out_shape=jax.ShapeDtypeStruct((bsz, seq, CONV_CH), F32),
        compiler_params=_params(("parallel", "parallel", "parallel")),
    )(x3, x3, cw, cb)
    return out.reshape(bsz * seq, CONV_CH)


def _conv_bwd(da, xbc, cw, cb, bsz, seq):
    tt = min(256, seq)
    nt, hb = seq // tt, tt // SUBLANES
    x3 = xbc.reshape(bsz, seq, CONV_CH)
    da3 = da.reshape(bsz, seq, CONV_CH)
    n2 = tt + SUBLANES

    def body(dam_ref, daa_ref, xm_ref, xb_ref, xa_ref, w_ref, b_ref, du_ref, gw_ref, gb_ref):
        bi, i = pl.program_id(1), pl.program_id(2)
        before = jnp.where(i > 0, xb_ref[0], 0.0)
        after = jnp.where(i < nt - 1, xa_ref[0], 0.0)
        xx = jnp.concatenate([before, xm_ref[0], after], axis=0)
        rolled = [xx] + [pltpu.roll(xx, s, 0) for s in range(1, CONV_K)]
        pre = b_ref[...]
        for s in range(CONV_K):
            pre = pre + rolled[s][SUBLANES:, :] * w_ref[3 - s:4 - s, :]
        da_ext = jnp.concatenate([dam_ref[0], jnp.where(i < nt - 1, daa_ref[0], 0.0)], axis=0)
        sg = _sigmoid(pre)
        dpre = da_ext * sg * (1.0 + pre * (1.0 - sg))
        du = dpre[:tt, :] * w_ref[3:4, :]
        for s in range(1, CONV_K):
            du = du + pltpu.roll(dpre, n2 - s, 0)[:tt, :] * w_ref[3 - s:4 - s, :]
        du_ref[0] = du.astype(BF16)
        dpm = dpre[:tt, :]
        gws = [jnp.sum(dpm * rolled[3 - kk][SUBLANES:SUBLANES + tt, :], axis=0, keepdims=True) for kk in range(CONV_K)]

        @pl.when((bi == 0) & (i == 0))
        def _():
            gw_ref[...] = jnp.zeros_like(gw_ref)
            gb_ref[...] = jnp.zeros_like(gb_ref)

        gw_ref[...] += jnp.concatenate(gws, axis=0)
        gb_ref[...] += jnp.sum(dpm, axis=0, keepdims=True)

    main = pl.BlockSpec((1, tt, CONV_TC), lambda c, b, i: (b, i, c))
    prev = pl.BlockSpec((1, SUBLANES, CONV_TC), lambda c, b, i: (b, jnp.maximum(i * hb - 1, 0), c))
    nxt = pl.BlockSpec((1, SUBLANES, CONV_TC), lambda c, b, i: (b, jnp.minimum((i + 1) * hb, seq // SUBLANES - 1), c))
    du, gw, gb = pl.pallas_call(
        body, name="conv_bwd", grid=(CONV_CH // CONV_TC, bsz, nt),
        in_specs=[main, nxt, main, prev, nxt,
                  pl.BlockSpec((CONV_K, CONV_TC), lambda c, b, i: (0, c)),
                  pl.BlockSpec((1, CONV_TC), lambda c, b, i: (0, c))],
        out_specs=[main, pl.BlockSpec((CONV_K, CONV_TC), lambda c, b, i: (0, c)),
                   pl.BlockSpec((1, CONV_TC), lambda c, b, i: (0, c))],
        out_shape=[jax.ShapeDtypeStruct((bsz, seq, CONV_CH), BF16), jax.ShapeDtypeStruct((CONV_K, CONV_CH), F32),
                   jax.ShapeDtypeStruct((1, CONV_CH), F32)],
        compiler_params=_params(("parallel", "arbitrary", "arbitrary")),
    )(da3, da3, x3, x3, x3, cw, cb)
    return du.reshape(bsz * seq, CONV_CH), gw, gb


def _ssd_prologue(dtf_ref, dtft_ref, bias_ref, biast_ref, arow_ref, acol_ref, dtfull_scr, acfull_scr, acr_scr):
    tri = (_iota((CHUNK, CHUNK), 1) <= _iota((CHUNK, CHUNK), 0)).astype(F32)
    triu = (_iota((CHUNK, CHUNK), 0) <= _iota((CHUNK, CHUNK), 1)).astype(F32)
    dtc = _softplus(dtf_ref[0] + bias_ref[...])
    a = dtc * arow_ref[...]
    acum = _sel_dot(tri, a)
    expand = _head_expand()
    dtfull_scr[...] = _dot_sel(dtc, expand)
    acfull_scr[...] = _dot_sel(acum, expand)
    dtr = _softplus(dtft_ref[0] + biast_ref[...])
    acr_scr[...] = _dot_sel(dtr * acol_ref[...], triu)
    return dtc, acum


def _decay_matrix(acum, acr_scr, h):
    lane = _iota((CHUNK, LANES), 1)
    ac_col = jnp.sum(jnp.where(lane == h, acum, 0.0), axis=1, keepdims=True)
    ac_row = acr_scr[h:h + 1, :]
    causal = _iota((CHUNK, CHUNK), 1) <= _iota((CHUNK, CHUNK), 0)
    return jnp.exp(jnp.where(causal, ac_col - ac_row, NEG))


def _ssd_fwd(xbc, dtf, dtft, bias, biast, arow, acol, dfull, bsz, seq):
    nc = seq // CHUNK
    x3 = xbc.reshape(bsz, seq, CONV_CH)

    def body(xbc_ref, dtf_ref, dtft_ref, bias_ref, biast_ref, arow_ref, acol_ref, dfull_ref,
             y_ref, hall_ref, h_scr, dtfull_scr, acfull_scr, acr_scr):
        @pl.when(pl.program_id(1) == 0)
        def _():
            h_scr[...] = jnp.zeros_like(h_scr)

        _, acum = _ssd_prologue(dtf_ref, dtft_ref, bias_ref, biast_ref, arow_ref, acol_ref,
                                dtfull_scr, acfull_scr, acr_scr)
        lane = _iota((CHUNK, LANES), 1)
        for g in range(2):
            bg = xbc_ref[0, :, D_MODEL + LANES * g:D_MODEL + LANES * (g + 1)]
            cg = xbc_ref[0, :, D_MODEL + 2 * LANES + LANES * g:D_MODEL + 2 * LANES + LANES * (g + 1)]
            cg_bf = cg.astype(BF16)
            gmat = _bdot_nt(cg_bf, bg)
            bgt_bf = bg.T.astype(BF16)
            for j in range(4):
                p = 4 * g + j
                sl = slice(LANES * p, LANES * (p + 1))
                xs = xbc_ref[0, :, sl]
                ac = acfull_scr[:, sl]
                acl = acfull_scr[CHUNK - 1:CHUNK, sl]
                xdt = xs * dtfull_scr[:, sl]
                xdt_bf = xdt.astype(BF16)
                hp = h_scr[p]
                hall_ref[0, 0, p] = hp
                yo = _bdot(cg_bf, hp) * jnp.exp(ac)
                yd = []
                for hh in range(2):
                    mmat = gmat * _decay_matrix(acum, acr_scr, 2 * p + hh)
                    yd.append(_bdot(mmat, xdt_bf))
                y_ref[0, :, sl] = jnp.where(lane < HEAD_DIM, yd[0], yd[1]) + yo + dfull_ref[:, sl] * xs
                h_scr[p] = hp * jnp.exp(acl) + _bdot(bgt_bf, xdt * jnp.exp(acl - ac))

    row = lambda w: pl.BlockSpec((1, w), lambda b, c: (0, 0))
    y, hall = pl.pallas_call(
        body, name="ssd_fwd", grid=(bsz, nc),
        in_specs=[pl.BlockSpec((1, CHUNK, CONV_CH), lambda b, c: (b, c, 0)),
                  pl.BlockSpec((1, CHUNK, LANES), lambda b, c: (b, c, 0)),
                  pl.BlockSpec((1, LANES, CHUNK), lambda b, c: (b, 0, c)),
                  row(LANES), pl.BlockSpec((LANES, 1), lambda b, c: (0, 0)),
                  row(LANES), pl.BlockSpec((LANES, 1), lambda b, c: (0, 0)), row(D_MODEL)],
        out_specs=[pl.BlockSpec((1, CHUNK, D_MODEL), lambda b, c: (b, c, 0)),
                   pl.BlockSpec((1, 1, HEAD_PAIRS, SSD_STATE, LANES), lambda b, c: (b, c, 0, 0, 0))],
        out_shape=[jax.ShapeDtypeStruct((bsz, seq, D_MODEL), F32),
                   jax.ShapeDtypeStruct((bsz, nc, HEAD_PAIRS, SSD_STATE, LANES), F32)],
        scratch_shapes=[pltpu.VMEM((HEAD_PAIRS, SSD_STATE, LANES), F32), pltpu.VMEM((CHUNK, D_MODEL), F32),
                        pltpu.VMEM((CHUNK, D_MODEL), F32), pltpu.VMEM((LANES, CHUNK), F32)],
        compiler_params=_params(("parallel", "arbitrary")),
    )(x3, dtf.reshape(bsz, seq, LANES), dtft, bias, biast, arow, acol, dfull)
    return y.reshape(bsz * seq, D_MODEL), hall


def _ssd_bwd(dy, xbc, dtf, dtft, bias, biast, arow, acol, dfull, hall, bsz, seq):
    nc = seq // CHUNK
    x3 = xbc.reshape(bsz, seq, CONV_CH)
    dy3 = dy.reshape(bsz, seq, D_MODEL)

    def body(dy_ref, xbc_ref, dtf_ref, dtft_ref, bias_ref, biast_ref, arow_ref, acol_ref, dfull_ref, hall_ref,
             dx_ref, ddt_ref, acc_ref, dh_scr, dtfull_scr, acfull_scr, acr_scr, csum_scr, dacf_scr, ddtf_scr, ddl_scr):
        first = (pl.program_id(0) == 0) & (pl.program_id(1) == 0)

        @pl.when(first)
        def _():
            acc_ref[...] = jnp.zeros_like(acc_ref)

        @pl.when(pl.program_id(1) == 0)
        def _():
            dh_scr[...] = jnp.zeros_like(dh_scr)

        dtc, acum = _ssd_prologue(dtf_ref, dtft_ref, bias_ref, biast_ref, arow_ref, acol_ref,
                                  dtfull_scr, acfull_scr, acr_scr)
        csum_scr[...] = jnp.zeros_like(csum_scr)
        lane = _iota((CHUNK, LANES), 1)
        last_row = _iota((CHUNK, LANES), 0) == CHUNK - 1
        rs16 = jnp.zeros((CHUNK, LANES), F32)
        for g in range(2):
            bsl = slice(D_MODEL + LANES * g, D_MODEL + LANES * (g + 1))
            csl = slice(D_MODEL + 2 * LANES + LANES * g, D_MODEL + 2 * LANES + LANES * (g + 1))
            bg_bf = xbc_ref[0, :, bsl].astype(BF16)
            cg = xbc_ref[0, :, csl]
            cg_bf = cg.astype(BF16)
            cgt_bf = cg.T.astype(BF16)
            gmat = _bdot_nt(cg_bf, bg_bf)
            dg = jnp.zeros((CHUNK, CHUNK), F32)
            dbg = jnp.zeros((CHUNK, SSD_STATE), F32)
            dcg = jnp.zeros((CHUNK, SSD_STATE), F32)
            for j in range(4):
                p = 4 * g + j
                sl = slice(LANES * p, LANES * (p + 1))
                xs = xbc_ref[0, :, sl]
                dt = dtfull_scr[:, sl]
                ac = acfull_scr[:, sl]
                acl = acfull_scr[CHUNK - 1:CHUNK, sl]
                dyp = dy_ref[0, :, sl]
                xdt = xs * dt
                xdt_bf = xdt.astype(BF16)
                e = jnp.exp(ac)
                dsd = jnp.exp(acl - ac)
                cd = jnp.exp(acl)
                xds_bf = (xdt * dsd).astype(BF16)
                hp = hall_ref[0, 0, p]
                hp_bf = hp.astype(BF16)
                dhn = dh_scr[p]
                dhn_bf = dhn.astype(BF16)
                yo = _bdot(cg_bf, hp_bf) * e
                dw_bf = (dyp * e).astype(BF16)
                dcg = dcg + _bdot_nt(dw_bf, hp_bf)
                dhin = _bdot(cgt_bf, dw_bf)
                dac = dyp * yo
                dacl = jnp.sum(dhn * hp, axis=0, keepdims=True) * cd
                dxds = _bdot(bg_bf, dhn_bf)
                dbg = dbg + _bdot_nt(xds_bf, dhn_bf)
                dxdt = dxds * dsd
                tdec = dxds * xdt * dsd
                dacl = dacl + jnp.sum(tdec, axis=0, keepdims=True)
                dac = dac - tdec
                dh_scr[p] = dhn * cd + dhin
                for hh in range(2):
                    h = 2 * p + hh
                    lm = (lane < HEAD_DIM) if hh == 0 else (lane >= HEAD_DIM)
                    dec = _decay_matrix(acum, acr_scr, h)
                    mmat = gmat * dec
                    dyh_bf = jnp.where(lm, dyp, 0.0).astype(BF16)
                    dm = _bdot_nt(dyh_bf, xdt_bf)
                    dxdt = dxdt + _bdot(mmat.T, dyh_bf)
                    dg = dg + dm * dec
                    q = dm * mmat
                    rs16 = rs16 + jnp.where(lane == h, jnp.sum(q, axis=1, keepdims=True), 0.0)
                    csum_scr[h:h + 1, :] = jnp.sum(q, axis=0, keepdims=True)
                dx_ref[0, :, sl] = dfull_ref[:, sl] * dyp + dxdt * dt
                dacf_scr[:, sl] = dac + jnp.where(last_row, dacl, 0.0)
                ddtf_scr[:, sl] = dxdt * xs
                ddl_scr[:, sl] = jnp.broadcast_to(jnp.sum(dyp * xs, axis=0, keepdims=True), (SUBLANES, LANES))
            dg_bf = dg.astype(BF16)
            dx_ref[0, :, bsl] = dbg + _bdot(dg.T, cg_bf)
            dx_ref[0, :, csl] = dcg + _bdot(dg_bf, bg_bf)
        reduce = _head_reduce()
        triu = (_iota((CHUNK, CHUNK), 0) <= _iota((CHUNK, CHUNK), 1)).astype(F32)
        dac16 = _dot_sel(dacf_scr[...], reduce) + rs16 - csum_scr[...].T
        da16 = _sel_dot(triu, dac16)
        ddt16 = da16 * arow_ref[...] + _dot_sel(ddtf_scr[...], reduce)
        ddtraw = ddt16 * _sigmoid(dtf_ref[0] + bias_ref[...])
        ddt_ref[0] = ddtraw
        acc_ref[0:8, :] += jnp.broadcast_to(jnp.sum(da16 * dtc * arow_ref[...], axis=0, keepdims=True), (SUBLANES, LANES))
        acc_ref[8:16, :] += _dot_sel(ddl_scr[...], reduce)
        acc_ref[16:24, :] += jnp.broadcast_to(jnp.sum(ddtraw, axis=0, keepdims=True), (SUBLANES, LANES))

    rev = lambda b, c: (b, nc - 1 - c, 0)
    row = lambda w: pl.BlockSpec((1, w), lambda b, c: (0, 0))
    dx, ddt, acc = pl.pallas_call(
        body, name="ssd_bwd", grid=(bsz, nc),
        in_specs=[pl.BlockSpec((1, CHUNK, D_MODEL), rev), pl.BlockSpec((1, CHUNK, CONV_CH), rev),
                  pl.BlockSpec((1, CHUNK, LANES), rev),
                  pl.BlockSpec((1, LANES, CHUNK), lambda b, c: (b, 0, nc - 1 - c)),
                  row(LANES), pl.BlockSpec((LANES, 1), lambda b, c: (0, 0)),
                  row(LANES), pl.BlockSpec((LANES, 1), lambda b, c: (0, 0)), row(D_MODEL),
                  pl.BlockSpec((1, 1, HEAD_PAIRS, SSD_STATE, LANES), lambda b, c: (b, nc - 1 - c, 0, 0, 0))],
        out_specs=[pl.BlockSpec((1, CHUNK, CONV_CH), rev), pl.BlockSpec((1, CHUNK, LANES), rev),
                   pl.BlockSpec((3 * SUBLANES, LANES), lambda b, c: (0, 0))],
        out_shape=[jax.ShapeDtypeStruct((bsz, seq, CONV_CH), F32), jax.ShapeDtypeStruct((bsz, seq, LANES), F32),
                   jax.ShapeDtypeStruct((3 * SUBLANES, LANES), F32)],
        scratch_shapes=[pltpu.VMEM((HEAD_PAIRS, SSD_STATE, LANES), F32), pltpu.VMEM((CHUNK, D_MODEL), F32),
                        pltpu.VMEM((CHUNK, D_MODEL), F32), pltpu.VMEM((LANES, CHUNK), F32),
                        pltpu.VMEM((LANES, CHUNK), F32), pltpu.VMEM((CHUNK, D_MODEL), F32),
                        pltpu.VMEM((CHUNK, D_MODEL), F32), pltpu.VMEM((SUBLANES, D_MODEL), F32)],
        compiler_params=_params(("arbitrary", "arbitrary")),
    )(dy3, x3, dtf.reshape(bsz, seq, LANES), dtft, bias, biast, arow, acol, dfull, hall)
    return dx.reshape(bsz * seq, CONV_CH), ddt.reshape(bsz * seq, LANES), acc


GROUP_W = D_MODEL // 2


def _gnorm_fwd(y, z, o_att, w):
    n = y.shape[0]
    tm = min(512, n)

    def body(y_ref, z_ref, o_ref, w_ref, out_ref):
        zv = z_ref[...]
        g = y_ref[...] * (zv * _sigmoid(zv))
        for gi in range(2):
            sl = slice(GROUP_W * gi, GROUP_W * (gi + 1))
            gs = g[:, sl]
            r = lax.rsqrt(jnp.mean(gs * gs, axis=-1, keepdims=True) + NORM_EPS)
            out_ref[:, sl] = (gs * r * w_ref[:, sl]).astype(BF16)
        out_ref[:, D_MODEL:] = o_ref[...].astype(BF16)

    tok = pl.BlockSpec((tm, D_MODEL), lambda i: (i, 0))
    return pl.pallas_call(
        body, name="gnorm_fwd", grid=(n // tm,),
        in_specs=[tok, tok, tok, pl.BlockSpec((1, D_MODEL), lambda i: (0, 0))],
        out_specs=pl.BlockSpec((tm, MIX_WIDTH), lambda i: (i, 0)),
        out_shape=jax.ShapeDtypeStruct((n, MIX_WIDTH), BF16),
        compiler_params=_params(("parallel",)),
    )(y, z, o_att, w)


def _gnorm_bwd(dycat, y, z, w):
    n = y.shape[0]
    tm = min(256, n)

    def body(dn_ref, y_ref, z_ref, w_ref, dy_ref, dz_ref, gw_ref):
        zv, yv = z_ref[...], y_ref[...]
        sz = _sigmoid(zv)
        sil = zv * sz
        g = yv * sil

        @pl.when(pl.program_id(0) == 0)
        def _():
            gw_ref[...] = jnp.zeros_like(gw_ref)

        for gi in range(2):
            sl = slice(GROUP_W * gi, GROUP_W * (gi + 1))
            gs = g[:, sl]
            r = lax.rsqrt(jnp.mean(gs * gs, axis=-1, keepdims=True) + NORM_EPS)
            nrm = gs * r
            dyn = dn_ref[:, sl]
            dn = dyn * w_ref[:, sl]
            dgs = r * (dn - nrm * jnp.mean(dn * nrm, axis=-1, keepdims=True))
            dy_ref[:, sl] = dgs * sil[:, sl]
            dz_ref[:, sl] = (dgs * yv[:, sl] * (sz[:, sl] * (1.0 + zv[:, sl] * (1.0 - sz[:, sl])))).astype(BF16)
            gw_ref[:, sl] += jnp.sum(dyn * nrm, axis=0, keepdims=True)

    tok = pl.BlockSpec((tm, D_MODEL), lambda i: (i, 0))
    row = pl.BlockSpec((1, D_MODEL), lambda i: (0, 0))
    return pl.pallas_call(
        body, name="gnorm_bwd", grid=(n // tm,),
        in_specs=[tok, tok, tok, row], out_specs=[tok, tok, row],
        out_shape=[jax.ShapeDtypeStruct((n, D_MODEL), F32), jax.ShapeDtypeStruct((n, D_MODEL), BF16),
                   jax.ShapeDtypeStruct((1, D_MODEL), F32)],
        compiler_params=_params(("arbitrary",)),
    )(dycat, y, z, w)


def _fox_prep(dtf, fb, bsz, seq):
    def body(x_ref, b_ref, c_ref):
        tri = (_iota((CHUNK, CHUNK), 1) <= _iota((CHUNK, CHUNK), 0)).astype(F32)
        carry = jnp.zeros((1, LANES), F32)
        for j in range(seq // CHUNK):
            sl = slice(CHUNK * j, CHUNK * (j + 1))
            lf = _log_sigmoid(x_ref[sl, :] + b_ref[...])
            c_ref[sl, :] = _sel_dot(tri, lf) + carry
            carry = carry + jnp.sum(lf, axis=0, keepdims=True)

    blk = pl.BlockSpec((seq, LANES), lambda b: (b, 0))
    return pl.pallas_call(
        body, name="fox_prep", grid=(bsz,),
        in_specs=[blk, pl.BlockSpec((1, LANES), lambda b: (0, 0))], out_specs=blk,
        out_shape=jax.ShapeDtypeStruct((bsz * seq, LANES), F32),
        compiler_params=_params(("parallel",)),
    )(dtf, fb)


def _fox_prep_bwd(dck, dcq, ddt, dtf, fb, bsz, seq):
    nj = seq // CHUNK

    def body(dck_ref, dcq_ref, ddt_ref, x_ref, b_ref, out_ref, gb_ref):
        triu = (_iota((CHUNK, CHUNK), 0) <= _iota((CHUNK, CHUNK), 1)).astype(F32)
        is_f = (_iota((CHUNK, LANES), 1) >= F_COL) & (_iota((CHUNK, LANES), 1) < 2 * F_COL)
        carry = jnp.zeros((1, LANES), F32)
        total = jnp.zeros((1, LANES), F32)
        for j in range(nj - 1, -1, -1):
            sl = slice(CHUNK * j, CHUNK * (j + 1))
            dcv = dck_ref[sl, :] + dcq_ref[sl, :]
            dlf = _sel_dot(triu, dcv) + carry
            carry = carry + jnp.sum(dcv, axis=0, keepdims=True)
            df = jnp.where(is_f, dlf * _sigmoid(-(x_ref[sl, :] + b_ref[...])), 0.0)
            out_ref[sl, :] = (df + ddt_ref[sl, :]).astype(BF16)
            total = total + jnp.sum(df, axis=0, keepdims=True)

        @pl.when(pl.program_id(0) == 0)
        def _():
            gb_ref[...] = jnp.zeros_like(gb_ref)

        gb_ref[...] += jnp.broadcast_to(total, (SUBLANES, LANES))

    blk = pl.BlockSpec((seq, LANES), lambda b: (b, 0))
    return pl.pallas_call(
        body, name="fox_prep_bwd", grid=(bsz,),
        in_specs=[blk, blk, blk, blk, pl.BlockSpec((1, LANES), lambda b: (0, 0))],
        out_specs=[blk, pl.BlockSpec((SUBLANES, LANES), lambda b: (0, 0))],
        out_shape=[jax.ShapeDtypeStruct((bsz * seq, LANES), BF16), jax.ShapeDtypeStruct((SUBLANES, LANES), F32)],
        compiler_params=_params(("arbitrary",)),
    )(dck, dcq, ddt, dtf, fb)


AUX_C = 3
AUX_ONE = 3


ATT_BLOCK_FWD = 512
ATT_BLOCK_BWD = 256


def _att_operands(q_ref, k_ref, c_ref, seq, hh, pair):
    lane = _iota((seq, LANES), 1)
    lm = (lane < HEAD_DIM) if hh == 0 else (lane >= HEAD_DIM)
    base = HEAD_DIM * (1 - hh)
    negc = -jnp.sum(jnp.where(lane == F_COL + 2 * pair + hh, c_ref[...], 0.0), axis=1, keepdims=True)
    c1 = negc.astype(BF16)
    r1 = negc - c1.astype(F32)
    c2 = r1.astype(BF16)
    c3 = (r1 - c2.astype(F32)).astype(BF16)
    zero = jnp.zeros((seq, LANES), BF16)
    one = jnp.ones((seq, LANES), BF16)
    ka = jnp.where(lm, k_ref[...], jnp.where(lane == base, c1, jnp.where(lane == base + 1, c2, jnp.where(
        lane == base + 2, c3, jnp.where(lane == base + AUX_ONE, one, zero)))))
    qa = jnp.where(lm, q_ref[...] * ATT_SCALE, jnp.where((lane >= base) & (lane < base + AUX_C), one, zero))
    return lm, base, qa, ka


def _att_fwd(qkv, ccol, bsz, seq, gather):
    blk = min(ATT_BLOCK_FWD, seq)
    nb = seq // blk
    nx = len(gather)

    def body(*refs):
        q_ref, k_ref, v_ref, c_ref = refs[:4]
        x_refs = refs[4:4 + nx]
        o_ref, lse_ref = refs[4 + nx:6 + nx]
        xo_refs = refs[6 + nx:6 + 2 * nx]
        qa_scr, ka_scr, va_scr = refs[6 + 2 * nx:9 + 2 * nx]
        sems = refs[9 + 2 * nx:]
        pair = pl.program_id(1)

        @pl.when((pl.program_id(0) == 0) & (pair == 0))
        def _():
            _exchange_start("gather", x_refs, xo_refs, *sems)

        causal = _iota((blk, blk), 0) >= _iota((blk, blk), 1)
        for hh in range(2):
            lm, _, qa, ka = _att_operands(q_ref, k_ref, c_ref, seq, hh, pair)
            qa_scr[hh] = qa
            ka_scr[hh] = ka
            va_scr[hh] = jnp.where(lm, v_ref[...], jnp.zeros((seq, LANES), BF16))

        def q_step(i, carry0):
            rows = pl.ds(pl.multiple_of(i * blk, blk), blk)

            def scores(j):
                krows = pl.ds(pl.multiple_of(j * blk, blk), blk)
                return tuple(_bdot_nt(qa_scr[hh, rows, :], ka_scr[hh, krows, :]) for hh in range(2))

            def update(state, s_pair, j, masked):
                krows = pl.ds(pl.multiple_of(j * blk, blk), blk)
                alphas, ls, pvs, ms = [], [], [], []
                for hh in range(2):
                    m, l, _ = state[hh]
                    s = jnp.where(causal, s_pair[hh], NEG) if masked else s_pair[hh]
                    mn = jnp.maximum(m, jnp.max(s, axis=1, keepdims=True))
                    alpha = jnp.exp(m - mn)
                    pr = jnp.exp(s - mn)
                    ms.append(mn)
                    alphas.append(alpha)
                    ls.append(alpha * l + jnp.sum(pr, axis=1, keepdims=True))
                    pvs.append(_bdot(pr, va_scr[hh, krows, :]))
                return tuple((ms[hh], ls[hh], alphas[hh] * state[hh][2] + pvs[hh]) for hh in range(2))

            def step(j, carry):
                state, s_cur = carry
                s_next = scores(j + 1)
                return update(state, s_cur, j, False), s_next

            one = (jnp.full((blk, 1), NEG, F32), jnp.zeros((blk, 1), F32), jnp.zeros((blk, LANES), F32))
            state, s_last = lax.fori_loop(0, i, step, ((one, one), scores(0)))
            (m0, l0, acc0), (m1, l1, acc1) = update(state, s_last, i, True)
            o_ref[rows, :] = acc0 * (1.0 / l0) + acc1 * (1.0 / l1)
            lse_ref[0, 0, rows, :] = m0 + jnp.log(l0)
            lse_ref[0, 1, rows, :] = m1 + jnp.log(l1)
            return carry0

        lax.fori_loop(0, nb, q_step, 0)

        @pl.when((pl.program_id(0) == bsz - 1) & (pair == HEAD_PAIRS - 1))
        def _():
            _exchange_wait("gather", x_refs, xo_refs, *sems)

    col = lambda off: pl.BlockSpec((seq, LANES), lambda b, p: (b, off + p))
    head2 = pltpu.VMEM((2, seq, LANES), BF16)
    hbm = pl.BlockSpec(memory_space=pl.ANY)
    return pl.pallas_call(
        body, name="att_fwd", grid=(bsz, HEAD_PAIRS),
        in_specs=[col(0), col(HEAD_PAIRS), col(2 * HEAD_PAIRS), pl.BlockSpec((seq, LANES), lambda b, p: (b, 0))] + [hbm] * nx,
        out_specs=[pl.BlockSpec((seq, LANES), lambda b, p: (b, p)),
                   pl.BlockSpec((1, 2, seq, 1), lambda b, p: (b, p, 0, 0))] + [hbm] * nx,
        out_shape=[jax.ShapeDtypeStruct((bsz * seq, D_MODEL), F32), jax.ShapeDtypeStruct((bsz, SSD_HEADS, seq, 1), F32)]
        + _exchange_shapes("gather", gather),
        scratch_shapes=[head2, head2, head2] + _exchange_scratch(nx),
        compiler_params=_params(("arbitrary", "arbitrary")),
    )(qkv, qkv, qkv, ccol, *gather)


def _att_bwd(qkv, dycat, o, lse, ccol, bsz, seq, scatter):
    blk = min(ATT_BLOCK_BWD, seq)
    nb = seq // blk
    nx = len(scatter)

    def body(*refs):
        q_ref, k_ref, v_ref, do_ref, o_ref, lse_ref, c_ref = refs[:7]
        x_refs = refs[7:7 + nx]
        dq_ref, dk_ref, dv_ref, dck_ref, dcq_ref = refs[7 + nx:12 + nx]
        xo_refs = refs[12 + nx:12 + 2 * nx]
        qa_scr, ka_scr, va_scr, doa_scr, kat_scr, d_scr, dqt_scr, dk_scr, dv_scr = refs[12 + 2 * nx:21 + 2 * nx]
        sems = refs[21 + 2 * nx:]
        pair = pl.program_id(1)

        @pl.when((pl.program_id(0) == 0) & (pair == 0))
        def _():
            _exchange_start("scatter", x_refs, xo_refs, *sems)

        causal_t = _iota((blk, blk), 1) >= _iota((blk, blk), 0)
        lane_b = _iota((blk, LANES), 1)
        row_t = _iota((LANES, seq), 0)
        masks, bases = [], []
        ones8 = jnp.ones((SUBLANES, LANES), F32)
        for hh in range(2):
            lm, base, qa, ka = _att_operands(q_ref, k_ref, c_ref, seq, hh, pair)
            masks.append(lm)
            bases.append(base)
            qa_scr[hh] = qa
            ka_scr[hh] = ka
            va_scr[hh] = jnp.where(lm, v_ref[...], jnp.zeros((seq, LANES), BF16))
            doa = jnp.where(lm, do_ref[...], 0.0).astype(BF16)
            doa_scr[hh] = doa
            for t0 in range(0, seq, blk):
                kat_scr[hh, :, t0:t0 + blk] = ka[t0:t0 + blk, :].astype(F32).T.astype(BF16)
            d_scr[hh] = lax.dot_general(ones8, doa.astype(F32) * o_ref[...], (((1,), (1,)), ((), ())),
                                        precision=HI, preferred_element_type=F32)
        dqt_scr[...] = jnp.zeros_like(dqt_scr)

        def kv_step(j, carry0):
            krows = pl.ds(pl.multiple_of(j * blk, blk), blk)
            dk_scr[...] = jnp.zeros_like(dk_scr)
            dv_scr[...] = jnp.zeros_like(dv_scr)

            def scores(i):
                rows = pl.ds(pl.multiple_of(i * blk, blk), blk)
                return tuple((_bdot_nt(ka_scr[hh, krows, :], qa_scr[hh, rows, :]),
                              _bdot_nt(va_scr[hh, krows, :], doa_scr[hh, rows, :])) for hh in range(2))

            def update(i, sd, masked):
                rows = pl.ds(pl.multiple_of(i * blk, blk), blk)
                for hh in range(2):
                    st, dpt = sd[hh]
                    if masked:
                        st = jnp.where(causal_t, st, NEG)
                    pt = jnp.exp(st - lse_ref[0, 0, hh:hh + 1, rows])
                    dst = (pt * (dpt - d_scr[hh, 0:1, rows])).astype(BF16)
                    dv_scr[hh] += _bdot(pt, doa_scr[hh, rows, :])
                    dk_scr[hh] += _bdot(dst, qa_scr[hh, rows, :])
                    dqt_scr[hh, :, rows] += _bdot(kat_scr[hh, :, krows], dst)

            def q_step(i, sd_cur):
                sd_next = scores(jnp.minimum(i + 1, nb - 1))
                update(i, sd_cur, False)
                return sd_next

            sd_diag = scores(j)
            sd_first = scores(jnp.minimum(j + 1, nb - 1))
            update(j, sd_diag, True)
            lax.fori_loop(j + 1, nb, q_step, sd_first)
            lmb0 = lane_b < HEAD_DIM
            dk_ref[krows, :] = jnp.where(lmb0, dk_scr[0], dk_scr[1]).astype(BF16)
            dv_ref[krows, :] = (dv_scr[0] + dv_scr[1]).astype(BF16)
            for hh in range(2):
                dck_ref[0, hh, krows, :] = -jnp.sum(jnp.where(lane_b == bases[hh], dk_scr[hh], 0.0), axis=1, keepdims=True)
            return carry0

        lax.fori_loop(0, nb, kv_step, 0)
        for t0 in range(0, seq, blk):
            dq0, dq1 = dqt_scr[0, :, t0:t0 + blk].T, dqt_scr[1, :, t0:t0 + blk].T
            dq_ref[t0:t0 + blk, :] = (jnp.where(lane_b < HEAD_DIM, dq0, dq1) * ATT_SCALE).astype(BF16)
        dcq_ref[...] = jnp.zeros_like(dcq_ref)
        for hh in range(2):
            dcq_ref[0, 0, hh:hh + 1, :] = jnp.sum(jnp.where(row_t == bases[hh] + AUX_ONE, dqt_scr[hh], 0.0),
                                                   axis=0, keepdims=True)

        @pl.when((pl.program_id(0) == bsz - 1) & (pair == HEAD_PAIRS - 1))
        def _():
            _exchange_wait("scatter", x_refs, xo_refs, *sems)

    col = lambda off: pl.BlockSpec((seq, LANES), lambda b, p: (b, off + p))
    vec = pl.BlockSpec((1, 2, seq, 1), lambda b, p: (b, p, 0, 0))
    rowvec = pl.BlockSpec((1, 1, SUBLANES, seq), lambda b, p: (b, p, 0, 0))
    out = jax.ShapeDtypeStruct((bsz * seq, D_MODEL), BF16)
    head2 = pltpu.VMEM((2, seq, LANES), BF16)
    hbm = pl.BlockSpec(memory_space=pl.ANY)
    return pl.pallas_call(
        body, name="att_bwd", grid=(bsz, HEAD_PAIRS),
        in_specs=[col(0), col(HEAD_PAIRS), col(2 * HEAD_PAIRS), col(HEAD_PAIRS), col(0), rowvec,
                  pl.BlockSpec((seq, LANES), lambda b, p: (b, 0))] + [hbm] * nx,
        out_specs=[col(0), col(0), col(0), vec, rowvec] + [hbm] * nx,
        out_shape=[out, out, out, jax.ShapeDtypeStruct((bsz, SSD_HEADS, seq, 1), F32),
                   jax.ShapeDtypeStruct((bsz, HEAD_PAIRS, SUBLANES, seq), F32)] + _exchange_shapes("scatter", scatter),
        scratch_shapes=[head2, head2, head2, head2, pltpu.VMEM((2, LANES, seq), BF16),
                        pltpu.VMEM((2, SUBLANES, seq), F32), pltpu.VMEM((2, LANES, seq), F32),
                        pltpu.VMEM((2, blk, LANES), F32), pltpu.VMEM((2, blk, LANES), F32)] + _exchange_scratch(nx),
        compiler_params=_params(("arbitrary", "arbitrary")),
    )(qkv, qkv, qkv, dycat, o, lse, ccol, *scatter)


def _adamw_math(w, g, m, v):
    m = ADAM_B1 * m + (1.0 - ADAM_B1) * g
    v = ADAM_B2 * v + (1.0 - ADAM_B2) * jnp.square(g)
    m_hat = m / ADAM_C1
    v_hat = v / ADAM_C2
    delta = -ADAM_LR * (m_hat / (jnp.sqrt(v_hat) + ADAM_EPS) + ADAM_WD * w)
    return delta, m, v


def _row_tile(rows):
    for tr in (256, 128, 64, 32, 16, 8):
        if rows % tr == 0:
            return tr
    return rows


def _sum_parts(parts, name):
    nparts, rows, cols = parts.shape
    tr = rows if rows % SUBLANES else _row_tile(rows)

    def body(p_ref, o_ref):
        g = p_ref[0].astype(F32)
        for s in range(1, nparts):
            g = g + p_ref[s].astype(F32)
        o_ref[...] = g

    return pl.pallas_call(
        body, name=name, grid=(rows // tr,),
        in_specs=[pl.BlockSpec((nparts, tr, cols), lambda i: (0, i, 0))],
        out_specs=pl.BlockSpec((tr, cols), lambda i: (i, 0)),
        out_shape=jax.ShapeDtypeStruct((rows, cols), F32),
        compiler_params=_params(("parallel",)),
    )(parts)


def _adamw_parts(parts, w, m, v, name):
    nparts, rows, cols = parts.shape
    tr = _row_tile(rows)

    def body(p_ref, w_ref, m_ref, v_ref, g_ref, d_ref, mo_ref, vo_ref):
        g = p_ref[0].astype(F32)
        for s in range(1, nparts):
            g = g + p_ref[s].astype(F32)
        delta, mn, vn = _adamw_math(w_ref[...], g, m_ref[...], v_ref[...])
        g_ref[...] = g
        d_ref[...] = delta
        mo_ref[...] = mn
        vo_ref[...] = vn

    blk = pl.BlockSpec((tr, cols), lambda i: (i, 0))
    shp = jax.ShapeDtypeStruct((rows, cols), F32)
    return pl.pallas_call(
        body, name=name, grid=(rows // tr,),
        in_specs=[pl.BlockSpec((nparts, tr, cols), lambda i: (0, i, 0)), blk, blk, blk],
        out_specs=[blk, blk, blk, blk], out_shape=[shp, shp, shp, shp],
        compiler_params=_params(("parallel",)),
    )(parts, w, m, v)


def _me():
    return lax.axis_index("x"), lax.axis_index("y"), lax.axis_index("c")


def _flip(pos, k):
    x, y, c = pos
    kx, ky, kc = (k >> 2) & 1, (k >> 1) & 1, k & 1
    return (x ^ kx if kx else x, y ^ ky if ky else y, c ^ kc if kc else c)


def _logical(pos):
    return 4 * pos[0] + 2 * pos[1] + pos[2]


def _all_gather_two_level(shards):
    narr = len(shards)

    def body(*refs):
        x_refs, out_refs = refs[:narr], refs[narr:2 * narr]
        send_sems, recv_sems, local_sems = refs[2 * narr:]
        x, y, c = _me()
        me, sibling = (x, y, c), (x, y, 1 - c)
        chips = [(1 - x, y), (x, 1 - y), (1 - x, 1 - y)]

        def copy(a, k, block, to, src=None):
            slot = out_refs[a].at[_logical(block)]
            return pltpu.make_async_remote_copy(
                src_ref=slot if src is None else src, dst_ref=slot,
                send_sem=send_sems.at[a, k], recv_sem=recv_sems.at[a, k], device_id=to, device_id_type=MESH)

        mine = [pltpu.make_async_copy(x_refs[a], out_refs[a].at[_logical(me)], local_sems.at[a]) for a in range(narr)]
        for cp in mine:
            cp.start()
        first = []
        for a in range(narr):
            first.append(copy(a, 0, me, sibling, src=x_refs[a]))
            first += [copy(a, 1 + j, me, (*chip, c), src=x_refs[a]) for j, chip in enumerate(chips)]
        for cp in first:
            cp.start()
        passed = []
        for a in range(narr):
            for j, chip in enumerate(chips):
                copy(a, 1 + j, (*chip, c), me).wait_recv()
                fwd = copy(a, 4 + j, (*chip, c), sibling)
                fwd.start()
                passed.append(fwd)
        for a in range(narr):
            copy(a, 0, sibling, me).wait_recv()
            for j, chip in enumerate(chips):
                copy(a, 4 + j, (*chip, 1 - c), me).wait_recv()
        for cp in first + passed:
            cp.wait_send()
        for cp in mine:
            cp.wait()

    hbm = pl.BlockSpec(memory_space=pl.ANY)
    return pl.pallas_call(
        body, name="all_gather_big",
        out_shape=[jax.ShapeDtypeStruct((N_DEV,) + s.shape, s.dtype) for s in shards],
        in_specs=[hbm] * narr, out_specs=[hbm] * narr,
        scratch_shapes=[pltpu.SemaphoreType.DMA((narr, 7)), pltpu.SemaphoreType.DMA((narr, 7)),
                        pltpu.SemaphoreType.DMA((narr,))],
    )(*shards)


def _exchange_copies(kind, x_refs, out_refs, send_sems, recv_sems, local_sems):
    me = _me()
    mine = _logical(me)
    local, remote = [], []
    for a, (x_ref, out_ref) in enumerate(zip(x_refs, out_refs)):
        own = x_ref if kind == "gather" else x_ref.at[mine]
        local.append(pltpu.make_async_copy(own, out_ref.at[mine], local_sems.at[a]))
        for k in range(1, N_DEV):
            peer = _flip(me, k)
            src = x_ref if kind == "gather" else x_ref.at[_logical(peer)]
            remote.append(pltpu.make_async_remote_copy(
                src_ref=src, dst_ref=out_ref.at[mine], send_sem=send_sems.at[a, k - 1], recv_sem=recv_sems.at[a, k - 1],
                device_id=peer, device_id_type=MESH))
    return local, remote


def _exchange_start(kind, x_refs, out_refs, *sems):
    if not x_refs:
        return
    local, remote = _exchange_copies(kind, x_refs, out_refs, *sems)
    for cp in local + remote:
        cp.start()


def _exchange_wait(kind, x_refs, out_refs, *sems):
    if not x_refs:
        return
    local, remote = _exchange_copies(kind, x_refs, out_refs, *sems)
    for cp in remote:
        cp.wait_recv()
    for cp in remote:
        cp.wait_send()
    for cp in local:
        cp.wait()


def _exchange_shapes(kind, arrays):
    return [jax.ShapeDtypeStruct(((N_DEV,) if kind == "gather" else ()) + a.shape, a.dtype) for a in arrays]


def _exchange_scratch(narrays):
    if not narrays:
        return []
    return [pltpu.SemaphoreType.DMA((narrays, N_DEV - 1)), pltpu.SemaphoreType.DMA((narrays, N_DEV - 1)),
            pltpu.SemaphoreType.DMA((narrays,))]


def _exchange_rows(x_ref, buf_ref, send_sems, recv_sems):
    me = _me()
    buf_ref[_logical(me)] = x_ref[...]
    copies = []
    for k in range(1, N_DEV):
        peer = _flip(me, k)
        copies.append(pltpu.make_async_remote_copy(
            src_ref=x_ref, dst_ref=buf_ref.at[_logical(me)],
            send_sem=send_sems.at[k - 1], recv_sem=recv_sems.at[k - 1], device_id=peer, device_id_type=MESH))
    for cp in copies:
        cp.start()
    for cp in copies:
        cp.wait_recv()
    for cp in copies:
        cp.wait_send()


def _sum_slots(buf_ref):
    total = buf_ref[0]
    for s in range(1, N_DEV):
        total = total + buf_ref[s]
    return total


def _small_gather(x):
    def body(x_ref, o_ref, buf_ref, send_sems, recv_sems):
        _exchange_rows(x_ref, buf_ref, send_sems, recv_sems)
        o_ref[...] = _sum_slots(buf_ref)

    return pl.pallas_call(
        body, name="small_gather", out_shape=jax.ShapeDtypeStruct(x.shape, F32),
        in_specs=[pl.BlockSpec(memory_space=pltpu.VMEM)], out_specs=pl.BlockSpec(memory_space=pltpu.VMEM),
        scratch_shapes=[pltpu.VMEM((N_DEV,) + x.shape, F32), pltpu.SemaphoreType.DMA((7,)), pltpu.SemaphoreType.DMA((7,))],
    )(x)


def _small_reduce_adamw(g, w, m, v):
    def body(g_ref, w_ref, m_ref, v_ref, go_ref, d_ref, mo_ref, vo_ref, buf_ref, send_sems, recv_sems):
        _exchange_rows(g_ref, buf_ref, send_sems, recv_sems)
        gs = _sum_slots(buf_ref)
        delta, mn, vn = _adamw_math(w_ref[...], gs, m_ref[...], v_ref[...])
        go_ref[...] = gs
        d_ref[...] = delta
        mo_ref[...] = mn
        vo_ref[...] = vn

    vm = pl.BlockSpec(memory_space=pltpu.VMEM)
    shp = jax.ShapeDtypeStruct(g.shape, F32)
    return pl.pallas_call(
        body, name="small_reduce_adamw", out_shape=[shp, shp, shp, shp],
        in_specs=[vm, vm, vm, vm], out_specs=[vm, vm, vm, vm],
        scratch_shapes=[pltpu.VMEM((N_DEV,) + g.shape, F32), pltpu.SemaphoreType.DMA((7,)), pltpu.SemaphoreType.DMA((7,))],
    )(g, w, m, v)


def _pad_cols(a, width):
    return jnp.pad(a, ((0, 0), (0, width - a.shape[1])))


def _local_step(x, target, norm_mix_w, w_in_t, conv_w, conv_b, dt_bias, a_log, d_skip, ssd_norm_w, f_bias,
                late_shards, norm_mlp_w, norm_final_w):
    bsz, seq, _ = x.shape
    n = bsz * seq
    x2 = x.reshape(n, D_MODEL)
    t2 = target.reshape(n, D_MODEL)
    nfw = norm_final_w.reshape(1, D_MODEL)

    bias = _pad_cols(dt_bias, LANES)
    arow = _pad_cols(-jnp.exp(a_log), LANES)
    biast, acol = bias.reshape(LANES, 1), arow.reshape(LANES, 1)
    dfull = jnp.repeat(d_skip, HEAD_DIM, axis=1)
    fb = jnp.pad(f_bias, ((0, 0), (F_COL, LANES - 2 * F_COL)))

    wt_z, wt_xbc, wt_qkv = w_in_t[:ROW_XBC], w_in_t[ROW_XBC:ROW_DT], w_in_t[ROW_Q:ROW_F]
    wt_dtf = jnp.concatenate([w_in_t[ROW_DT:ROW_Q], w_in_t[ROW_F:], jnp.zeros((LANES - 2 * F_COL, D_MODEL), BF16)], axis=0)
    wt_q, wt_k, wt_v = wt_qkv[:D_MODEL], wt_qkv[D_MODEL:2 * D_MODEL], wt_qkv[2 * D_MODEL:]

    h1 = _rms_fwd(x2, norm_mix_w, "rms_fwd_mix")
    z = _mm([(h1, wt_z)], "inproj_z", F32, 1024, 1024, nt=True)
    xbc_raw = _mm([(h1, wt_xbc)], "inproj_xbc", F32, 512, 1536, nt=True)
    qkv = _mm([(h1, wt_qkv)], "inproj_qkv", BF16, 512, 3072, nt=True)
    dtf = _mm([(h1, wt_dtf)], "inproj_dtf", F32, 2048, LANES, nt=True)
    dtft = dtf.reshape(bsz, seq, LANES).transpose(0, 2, 1)

    xbc = _conv_fwd(xbc_raw, conv_w, conv_b, bsz, seq)
    y_pre, hall = _ssd_fwd(xbc, dtf, dtft, bias, biast, arow, acol, dfull, bsz, seq)

    ccol = _fox_prep(dtf, fb, bsz, seq)
    o_att, lse, w_out, w_up_t, w_down = _att_fwd(qkv, ccol, bsz, seq, late_shards)
    w_out, w_up_t, w_down = (w.reshape(-1, D_MODEL) for w in (w_out, w_up_t, w_down))

    ycat = _gnorm_fwd(y_pre, z, o_att, ssd_norm_w)
    h2 = _mm([(ycat, w_out)], "outproj", F32, 512, 1024, res=x2)
    h2n = _rms_fwd(h2, norm_mlp_w, "rms_fwd_mlp")
    pre = _mm([(h2n, w_up_t)], "mlp_up", F32, 256, 4096, nt=True)
    h3 = _mlp_down_fwd(pre, w_down, h2)

    dh3, loss_row, g_nfw = _loss_bwd(h3, t2, nfw)
    dpre, u = _mlp_down_bwd(dh3, w_down, pre)
    g_w_down = _mm_tn(u, dh3, "grad_w_down", 1024, 1024, 2048)
    g_w_up_t = _mm_tn(dpre, h2n, "grad_w_up", 1024, 1024, 2048)
    by_shard = lambda g: g.reshape(N_DEV, g.shape[0] // N_DEV, D_MODEL)
    dh2n = _mm([(dpre, w_up_t)], "mlp_up_bwd", F32, 512, 1024)
    dh2, g_nmlp = _rms_bwd(dh2n, h2, norm_mlp_w, dh3, "rms_bwd_mlp")

    g_w_out = _mm_tn(ycat, dh2, "grad_w_out", 1024, 1024, 2048)
    dycat = _mm([(dh2, w_out)], "outproj_bwd", F32, 512, 2048, nt=True)
    dy_pre, dz, g_ssdn = _gnorm_bwd(dycat, y_pre, z, ssd_norm_w)
    lse_rows = jnp.pad(lse.reshape(bsz, HEAD_PAIRS, 2, seq), ((0, 0), (0, 0), (0, SUBLANES - 2), (0, 0)))
    dq, dk, dv, dck, dcq, recv_down, recv_up_t, recv_out = _att_bwd(
        qkv, dycat, o_att, lse_rows, ccol, bsz, seq, [by_shard(g_w_down), by_shard(g_w_up_t), by_shard(g_w_out)])
    dcq = dcq[:, :, :2, :]

    dxbc, ddt, ssd_acc = _ssd_bwd(dy_pre, xbc, dtf, dtft, bias, biast, arow, acol, dfull, hall, bsz, seq)
    dxbc_raw, g_cw, g_cb = _conv_bwd(dxbc, xbc_raw, conv_w, conv_b, bsz, seq)

    def head_cols(vec):
        cols = vec.reshape(bsz, SSD_HEADS, seq).transpose(0, 2, 1).reshape(n, SSD_HEADS)
        return jnp.pad(cols, ((0, 0), (F_COL, LANES - 2 * F_COL)))

    ddtf, g_fb = _fox_prep_bwd(head_cols(dck), head_cols(dcq), ddt, dtf, fb, bsz, seq)

    g_dtf = _mm_tn(ddtf, h1, "grad_w_in_dtf", LANES, 1024, 2048)
    g_w_in_t = jnp.concatenate([
        _mm_tn(dz, h1, "grad_w_in_z", 1024, 1024, 2048), _mm_tn(dxbc_raw, h1, "grad_w_in_xbc", 768, 1024, 2048),
        g_dtf[:F_COL], _mm_tn(dq, h1, "grad_w_in_q", 1024, 1024, 2048), _mm_tn(dk, h1, "grad_w_in_k", 1024, 1024, 2048),
        _mm_tn(dv, h1, "grad_w_in_v", 1024, 1024, 2048), g_dtf[F_COL:2 * F_COL]], axis=0)
    pieces = [(dz, wt_z), (dxbc_raw, wt_xbc), (dq, wt_q), (dk, wt_k), (dv, wt_v), (ddtf, wt_dtf)]
    dh1, recv_in_t = _mm(pieces, "inproj_bwd", F32, 256, 1024, exchange=("scatter", [by_shard(g_w_in_t)]))
    dx, g_nmix = _rms_bwd(dh1, x2, norm_mix_w, dh2, "rms_bwd_mix")

    small = dict(
        norm_mix_w=g_nmix, norm_mlp_w=g_nmlp, norm_final_w=g_nfw, ssd_norm_w=g_ssdn, conv_b=g_cb, conv_w=g_cw,
        dt_bias=ssd_acc[16:17, :SSD_HEADS], a_log=ssd_acc[0:1, :SSD_HEADS], d_skip=ssd_acc[8:9, :SSD_HEADS],
        f_bias=g_fb[0:1, F_COL:2 * F_COL])
    recv = dict(w_in_t=recv_in_t, w_out=recv_out, w_up_t=recv_up_t, w_down=recv_down)
    return loss_row[0, 0], dx.reshape(bsz, seq, D_MODEL), small, recv


SMALL_LAYOUT = (("norm_mix_w", 1, D_MODEL), ("norm_mlp_w", 1, D_MODEL), ("norm_final_w", 1, D_MODEL),
                ("ssd_norm_w", 1, D_MODEL), ("conv_b", 1, CONV_CH), ("conv_w", CONV_K, CONV_CH),
                ("dt_bias", 1, SSD_HEADS), ("a_log", 1, SSD_HEADS), ("d_skip", 1, SSD_HEADS), ("f_bias", 1, SSD_HEADS))


def _pack_small(vals):
    tiles = []
    for name, nrows, width in SMALL_LAYOUT:
        tiles.append(jnp.pad(vals[name].reshape(nrows, width), ((0, SUBLANES - nrows), (0, SMALL_COLS - width))))
    return jnp.concatenate(tiles, axis=0)


def _unpack_small(packed, like):
    out = {}
    for i, (name, nrows, width) in enumerate(SMALL_LAYOUT):
        out[name] = packed[SUBLANES * i:SUBLANES * i + nrows, :width].reshape(like[name].shape)
    return out


def kernel(x, norm_mix_w, w_in, conv_w, conv_b, dt_bias, a_log, d_skip, ssd_norm_w, f_bias, w_out, norm_mlp_w, w_up, w_down, norm_final_w, loss_target, m_norm_mix_w, m_w_in, m_conv_w, m_conv_b, m_dt_bias, m_a_log, m_d_skip, m_ssd_norm_w, m_f_bias, m_w_out, m_norm_mlp_w, m_w_up, m_w_down, m_norm_final_w, v_norm_mix_w, v_w_in, v_conv_w, v_conv_b, v_dt_bias, v_a_log, v_d_skip, v_ssd_norm_w, v_f_bias, v_w_out, v_norm_mlp_w, v_w_up, v_w_down, v_norm_final_w):
    me = 4 * lax.axis_index("x") + 2 * lax.axis_index("y") + lax.axis_index("c")
    conv_shard_w = CONV_CH // N_DEV
    zero = jnp.zeros((), jnp.int32)

    def place_conv(shard):
        return lax.dynamic_update_slice(jnp.zeros((CONV_K, CONV_CH), F32), shard[0], (zero, me * conv_shard_w))

    (g_in_t,) = _all_gather_two_level([w_in[0].T.astype(BF16)])
    late_shards = [w_out[0].astype(BF16), w_up[0].T.astype(BF16), w_down[0].astype(BF16)]
    conv_full = _small_gather(jnp.pad(place_conv(conv_w), ((0, SUBLANES - CONV_K), (0, 0))))[:CONV_K]

    loss_local, grad_x, g_small, recv = _local_step(
        x, loss_target, norm_mix_w, g_in_t.reshape(IN_WIDTH, D_MODEL), conv_full, conv_b, dt_bias, a_log, d_skip,
        ssd_norm_w, f_bias, late_shards, norm_mlp_w, norm_final_w)
    loss = lax.psum(loss_local, MESH_AXES)

    grad_in = _sum_parts(recv["w_in_t"], "sum_w_in").T[None]
    grad_up = _sum_parts(recv["w_up_t"], "sum_w_up").T[None]
    big = {
        "w_in": _adamw_parts(grad_in, w_in[0], m_w_in[0], v_w_in[0], "adamw_w_in"),
        "w_out": _adamw_parts(recv["w_out"], w_out[0], m_w_out[0], v_w_out[0], "adamw_w_out"),
        "w_up": _adamw_parts(grad_up, w_up[0], m_w_up[0], v_w_up[0], "adamw_w_up"),
        "w_down": _adamw_parts(recv["w_down"], w_down[0], m_w_down[0], v_w_down[0], "adamw_w_down"),
    }

    like = dict(norm_mix_w=norm_mix_w, norm_mlp_w=norm_mlp_w, norm_final_w=norm_final_w, ssd_norm_w=ssd_norm_w,
                conv_b=conv_b, conv_w=jnp.zeros((1, CONV_K, CONV_CH), F32), dt_bias=dt_bias, a_log=a_log,
                d_skip=d_skip, f_bias=f_bias)
    w_small = dict(like, conv_w=place_conv(conv_w))
    m_small = dict(norm_mix_w=m_norm_mix_w, norm_mlp_w=m_norm_mlp_w, norm_final_w=m_norm_final_w,
                   ssd_norm_w=m_ssd_norm_w, conv_b=m_conv_b, conv_w=place_conv(m_conv_w), dt_bias=m_dt_bias,
                   a_log=m_a_log, d_skip=m_d_skip, f_bias=m_f_bias)
    v_small = dict(norm_mix_w=v_norm_mix_w, norm_mlp_w=v_norm_mlp_w, norm_final_w=v_norm_final_w,
                   ssd_norm_w=v_ssd_norm_w, conv_b=v_conv_b, conv_w=place_conv(v_conv_w), dt_bias=v_dt_bias,
                   a_log=v_a_log, d_skip=v_d_skip, f_bias=v_f_bias)
    small = _small_reduce_adamw(_pack_small(g_small), _pack_small(w_small), _pack_small(m_small), _pack_small(v_small))
    small = [_unpack_small(s, like) for s in small]
    for s in small:
        s["conv_w"] = lax.dynamic_slice(s["conv_w"], (zero, zero, me * conv_shard_w), (1, CONV_K, conv_shard_w))

    order = ["norm_mix_w", "w_in", "conv_w", "conv_b", "dt_bias", "a_log", "d_skip", "ssd_norm_w", "f_bias", "w_out",
             "norm_mlp_w", "w_up", "w_down", "norm_final_w"]
    outs = [loss, grad_x]
    for kind in range(4):
        for name in order:
            outs.append(big[name][kind][None] if name in big else small[kind][name])
    return tuple(outs)
```

```python
import jax
import jax.numpy as jnp
from jax import lax
from jax.experimental import pallas as pl
from jax.experimental.pallas import tpu as pltpu

F32, BF16 = jnp.float32, jnp.bfloat16
HI = lax.Precision.HIGHEST

D_MODEL = 1024
SSD_HEADS = 16
HEAD_DIM = 64
SSD_STATE = 128
CHUNK = 128
CONV_CH = 1536
CONV_K = 4
D_FF = 4096
MIX_WIDTH = 2048
IN_WIDTH = 5664
NORM_EPS = 1e-5
ATT_SCALE = 0.125

LANES = 128
SUBLANES = 8
HEAD_PAIRS = SSD_HEADS // 2
NEG = -1e30
VMEM_LIMIT = 52 * 1024 * 1024

ROW_XBC, ROW_DT, ROW_Q, ROW_F = 1024, 2560, 2576, 5648
F_COL = SSD_HEADS

N_DEV = 8
MESH_AXES = ("x", "y", "c")
MESH = pl.DeviceIdType.MESH

ADAM_LR, ADAM_B1, ADAM_B2, ADAM_EPS, ADAM_WD, ADAM_STEP = 0.001, 0.9, 0.999, 1e-08, 0.01, 10
ADAM_C1 = 1.0 - ADAM_B1 ** ADAM_STEP
ADAM_C2 = 1.0 - ADAM_B2 ** ADAM_STEP

SMALL_COLS = CONV_CH


def _params(sem=None):
    return pltpu.CompilerParams(dimension_semantics=sem, vmem_limit_bytes=VMEM_LIMIT)


def _sigmoid(u):
    return 1.0 / (1.0 + jnp.exp(-u))


def _softplus(u):
    return jnp.maximum(u, 0.0) + jnp.log(1.0 + jnp.exp(-jnp.abs(u)))


def _log_sigmoid(u):
    return jnp.minimum(u, 0.0) - jnp.log(1.0 + jnp.exp(-jnp.abs(u)))


def _bdot(a, b):
    return jnp.dot(a.astype(BF16), b.astype(BF16), preferred_element_type=F32)


def _bdot_nt(a, b):
    return lax.dot_general(a.astype(BF16), b.astype(BF16), (((1,), (1,)), ((), ())), preferred_element_type=F32)


def _bdot_tn(a, b):
    return lax.dot_general(a.astype(BF16), b.astype(BF16), (((0,), (0,)), ((), ())), preferred_element_type=F32)


def _split3(x):
    x1 = x.astype(BF16)
    r1 = x - x1.astype(F32)
    x2 = r1.astype(BF16)
    return x1, x2, (r1 - x2.astype(F32)).astype(BF16)


def _dot_sel(x, sel):
    s = sel.astype(BF16)
    p1, p2, p3 = (jnp.dot(p, s, preferred_element_type=F32) for p in _split3(x))
    return p1 + p2 + p3


def _sel_dot(sel, x):
    s = sel.astype(BF16)
    p1, p2, p3 = (jnp.dot(s, p, preferred_element_type=F32) for p in _split3(x))
    return p1 + p2 + p3


def _iota(shape, dim):
    return lax.broadcasted_iota(jnp.int32, shape, dim)


def _head_expand():
    return (jnp.right_shift(_iota((LANES, D_MODEL), 1), 6) == _iota((LANES, D_MODEL), 0)).astype(F32)


def _head_reduce():
    return (jnp.right_shift(_iota((D_MODEL, LANES), 0), 6) == _iota((D_MODEL, LANES), 1)).astype(F32)


def _rms_fwd(x, w, name):
    n, d = x.shape
    tm = min(1024, n)

    def body(x_ref, w_ref, o_ref):
        xv = x_ref[...]
        r = lax.rsqrt(jnp.mean(xv * xv, axis=-1, keepdims=True) + NORM_EPS)
        o_ref[...] = (xv * r * w_ref[...]).astype(BF16)

    return pl.pallas_call(
        body, name=name, grid=(n // tm,),
        in_specs=[pl.BlockSpec((tm, d), lambda i: (i, 0)), pl.BlockSpec((1, d), lambda i: (0, 0))],
        out_specs=pl.BlockSpec((tm, d), lambda i: (i, 0)),
        out_shape=jax.ShapeDtypeStruct((n, d), BF16),
        compiler_params=_params(("parallel",)),
    )(x, w)


def _mm(pairs, name, out_dtype, tm, tn, nt=False, res=None, exchange=None, rms_fwd=None, rms_bwd=None):
    m = pairs[0][0].shape[0]
    n = pairs[0][1].shape[0 if nt else 1]
    tm, tn = min(tm, m), min(tn, n)
    gm, gn = m // tm, n // tn
    npairs = len(pairs)
    kind, xs = (None, []) if exchange is None else exchange
    nx = len(xs)
    extra_in = [] if res is None else [res]
    if rms_bwd is not None:
        extra_in += list(rms_bwd)
    elif rms_fwd is not None:
        extra_in.append(rms_fwd)
    n_in = 2 * npairs + len(extra_in)
    n_out = 1 + (rms_fwd is not None or rms_bwd is not None)
    assert (rms_fwd is None and rms_bwd is None) or tn == n

    def body(*refs):
        x_refs, o_refs = refs[n_in:n_in + nx], refs[n_in + nx:n_in + nx + n_out]
        xo_refs, sems = refs[n_in + nx + n_out:n_in + 2 * nx + n_out], refs[n_in + 2 * nx + n_out:]
        extra = refs[2 * npairs:n_in]
        i, j = pl.program_id(0), pl.program_id(1)
        if nx:
            @pl.when((i == 0) & (j == 0))
            def _():
                _exchange_start(kind, x_refs, xo_refs, *sems)

        acc = None
        for p in range(npairs):
            a_v, b_v = refs[2 * p][...], refs[2 * p + 1][...]
            d = _bdot_nt(a_v, b_v) if nt else _bdot(a_v, b_v)
            acc = d if acc is None else acc + d
        if rms_bwd is not None:
            xv = extra[1][...]
            r = lax.rsqrt(jnp.mean(xv * xv, axis=-1, keepdims=True) + NORM_EPS)
            nrm = xv * r
            g = acc * extra[2][...]
            o_refs[0][...] = extra[0][...] + r * (g - nrm * jnp.mean(g * nrm, axis=-1, keepdims=True))

            @pl.when(i == 0)
            def _():
                o_refs[1][...] = jnp.zeros_like(o_refs[1])

            o_refs[1][...] += jnp.sum(acc * nrm, axis=0, keepdims=True)
        else:
            if res is not None:
                acc = extra[0][...] + acc
            o_refs[0][...] = acc.astype(o_refs[0].dtype)
            if rms_fwd is not None:
                r = lax.rsqrt(jnp.mean(acc * acc, axis=-1, keepdims=True) + NORM_EPS)
                o_refs[1][...] = (acc * r * extra[-1][...]).astype(BF16)
        if nx:
            @pl.when((i == gm - 1) & (j == gn - 1))
            def _():
                _exchange_wait(kind, x_refs, xo_refs, *sems)

    tile = pl.BlockSpec((tm, tn), lambda i, j: (i, j))
    row = pl.BlockSpec((1, tn), lambda i, j: (0, j))
    in_specs, args = [], []
    for a, b in pairs:
        k = a.shape[1]
        in_specs.append(pl.BlockSpec((tm, k), lambda i, j: (i, 0)))
        in_specs.append(pl.BlockSpec((tn, k), lambda i, j: (j, 0)) if nt else pl.BlockSpec((k, tn), lambda i, j: (0, j)))
        args += [a, b]
    in_specs += [row if e.shape[0] == 1 else tile for e in extra_in]
    out_specs, out_shape = [tile], [jax.ShapeDtypeStruct((m, n), out_dtype)]
    if rms_bwd is not None:
        out_specs.append(row)
        out_shape.append(jax.ShapeDtypeStruct((1, n), F32))
    elif rms_fwd is not None:
        out_specs.append(tile)
        out_shape.append(jax.ShapeDtypeStruct((m, n), BF16))
    hbm = pl.BlockSpec(memory_space=pl.ANY)
    sequential = nx or rms_bwd is not None
    outs = pl.pallas_call(
        body, name=name, grid=(gm, gn), in_specs=in_specs + [hbm] * nx,
        out_specs=out_specs + [hbm] * nx,
        out_shape=out_shape + _exchange_shapes(kind, xs),
        scratch_shapes=_exchange_scratch(nx),
        compiler_params=_params(("arbitrary", "arbitrary") if sequential else ("parallel", "parallel")),
    )(*args, *extra_in, *xs)
    return outs if len(outs) > 1 else outs[0]


def _mm_tn(a, b, name, tm, tn, tk):
    t, m = a.shape
    n = b.shape[1]
    tm, tn, tk = min(tm, m), min(tn, n), min(tk, t)
    nk = t // tk

    def body(a_ref, b_ref, o_ref, acc_ref):
        kk = pl.program_id(2)

        @pl.when(kk == 0)
        def _():
            acc_ref[...] = jnp.zeros_like(acc_ref)

        acc_ref[...] += _bdot_tn(a_ref[...], b_ref[...])

        @pl.when(kk == nk - 1)
        def _():
            o_ref[...] = acc_ref[...].astype(o_ref.dtype)

    return pl.pallas_call(
        body, name=name, grid=(m // tm, n // tn, nk),
        in_specs=[pl.BlockSpec((tk, tm), lambda i, j, kk: (kk, i)), pl.BlockSpec((tk, tn), lambda i, j, kk: (kk, j))],
        out_specs=pl.BlockSpec((tm, tn), lambda i, j, kk: (i, j)),
        out_shape=jax.ShapeDtypeStruct((m, n), BF16),
        scratch_shapes=[pltpu.VMEM((tm, tn), F32)],
        compiler_params=_params(("parallel", "parallel", "arbitrary")),
    )(a, b)


def _mlp_down_loss(pre, w_down, h2, target, w):
    m, k = pre.shape
    d = w_down.shape[1]
    tm = min(256, m)

    def body(p_ref, b_ref, r_ref, t_ref, w_ref, dh_ref, loss_ref, gw_ref):
        u = jnp.square(jnp.maximum(p_ref[...], 0.0))
        xv = r_ref[...] + _bdot(u, b_ref[...])
        r = lax.rsqrt(jnp.mean(xv * xv, axis=-1, keepdims=True) + NORM_EPS)
        nrm = xv * r
        wv = w_ref[...]
        err = nrm * wv - t_ref[...]
        part = 0.5 * jnp.sum(jnp.mean(err * err, axis=-1, keepdims=True), axis=0, keepdims=True)
        dy = err * (1.0 / d)
        g = dy * wv
        dh_ref[...] = r * (g - nrm * jnp.mean(g * nrm, axis=-1, keepdims=True))

        @pl.when(pl.program_id(0) == 0)
        def _():
            loss_ref[...] = jnp.zeros_like(loss_ref)
            gw_ref[...] = jnp.zeros_like(gw_ref)

        loss_ref[...] += jnp.broadcast_to(part, loss_ref.shape)
        gw_ref[...] += jnp.sum(dy * nrm, axis=0, keepdims=True)

    tok = pl.BlockSpec((tm, d), lambda i: (i, 0))
    row = pl.BlockSpec((1, d), lambda i: (0, 0))
    return pl.pallas_call(
        body, name="mlp_down_loss", grid=(m // tm,),
        in_specs=[pl.BlockSpec((tm, k), lambda i: (i, 0)), pl.BlockSpec((k, d), lambda i: (0, 0)), tok, tok, row],
        out_specs=[tok, pl.BlockSpec((1, LANES), lambda i: (0, 0)), row],
        out_shape=[jax.ShapeDtypeStruct((m, d), F32), jax.ShapeDtypeStruct((1, LANES), F32),
                   jax.ShapeDtypeStruct((1, d), F32)],
        compiler_params=_params(("arbitrary",)),
    )(pre, w_down, h2, target, w)


def _mlp_down_bwd(dh3, w_down, pre):
    m, k = dh3.shape
    n = w_down.shape[0]
    tm, tn = min(256, m), n

    def body(a_ref, b_ref, p_ref, dpre_ref, u_ref):
        r = jnp.maximum(p_ref[...], 0.0)
        du = _bdot_nt(a_ref[...], b_ref[...])
        dpre_ref[...] = (du * (2.0 * r)).astype(BF16)
        u_ref[...] = (r * r).astype(BF16)

    blk = pl.BlockSpec((tm, tn), lambda i, j: (i, j))
    return pl.pallas_call(
        body, name="mlp_down_bwd", grid=(m // tm, n // tn),
        in_specs=[pl.BlockSpec((tm, k), lambda i, j: (i, 0)), pl.BlockSpec((tn, k), lambda i, j: (j, 0)), blk],
        out_specs=[blk, blk],
        out_shape=[jax.ShapeDtypeStruct((m, n), BF16), jax.ShapeDtypeStruct((m, n), BF16)],
        compiler_params=_params(("parallel", "parallel")),
    )(dh3, w_down, pre)


CONV_TC = 512


def _conv_fwd(xbc, cw, cb, bsz, seq):
    tt = min(512, seq)
    nt, hb = seq // tt, tt // SUBLANES
    x3 = xbc.reshape(bsz, seq, CONV_CH)

    def body(xm_ref, xh_ref, w_ref, b_ref, o_ref):
        i = pl.program_id(2)
        x = xm_ref[0]
        halo = jnp.where(i > 0, xh_ref[0], 0.0)
        xx = jnp.concatenate([halo, x], axis=0)
        acc = x * w_ref[3:4, :] + b_ref[...]
        for s in range(1, CONV_K):
            acc = acc + pltpu.roll(xx, s, 0)[SUBLANES:, :] * w_ref[3 - s:4 - s, :]
        o_ref[0] = acc * _sigmoid(acc)

    out = pl.pallas_call(
        body, name="conv_fwd", grid=(CONV_CH // CONV_TC, bsz, nt),
        in_specs=[pl.BlockSpec((1, tt, CONV_TC), lambda c, b, i: (b, i, c)),
                  pl.BlockSpec((1, SUBLANES, CONV_TC), lambda c, b, i: (b, jnp.maximum(i * hb - 1, 0), c)),
                  pl.BlockSpec((CONV_K, CONV_TC), lambda c, b, i: (0, c)),
                  pl.BlockSpec((1, CONV_TC), lambda c, b, i: (0, c))],
        out_specs=pl.BlockSpec((1, tt, CONV_TC), lambda c, b, i: (b, i, c)),
        out_shape=jax.ShapeDtypeStruct((bsz, seq, CONV_CH), F32),
        compiler_params=_params(("parallel", "parallel", "parallel")),
    )(x3, x3, cw, cb)
    return out.reshape(bsz * seq, CONV_CH)


def _conv_bwd(da, xbc, cw, cb, bsz, seq):
    tt = min(512, seq)
    nt, hb = seq // tt, tt // SUBLANES
    x3 = xbc.reshape(bsz, seq, CONV_CH)
    da3 = da.reshape(bsz, seq, CONV_CH)
    n2 = tt + SUBLANES

    def body(dam_ref, daa_ref, xm_ref, xb_ref, xa_ref, w_ref, b_ref, du_ref, gw_ref, gb_ref):
        bi, i = pl.program_id(1), pl.program_id(2)
        before = jnp.where(i > 0, xb_ref[0], 0.0)
        after = jnp.where(i < nt - 1, xa_ref[0], 0.0)
        xx = jnp.concatenate([before, xm_ref[0], after], axis=0)
        rolled = [xx] + [pltpu.roll(xx, s, 0) for s in range(1, CONV_K)]
        pre = b_ref[...]
        for s in range(CONV_K):
            pre = pre + rolled[s][SUBLANES:, :] * w_ref[3 - s:4 - s, :]
        da_ext = jnp.concatenate([dam_ref[0], jnp.where(i < nt - 1, daa_ref[0], 0.0)], axis=0)
        sg = _sigmoid(pre)
        dpre = da_ext * sg * (1.0 + pre * (1.0 - sg))
        du = dpre[:tt, :] * w_ref[3:4, :]
        for s in range(1, CONV_K):
            du = du + pltpu.roll(dpre, n2 - s, 0)[:tt, :] * w_ref[3 - s:4 - s, :]
        du_ref[0] = du.astype(BF16)
        dpm = dpre[:tt, :]
        gws = [jnp.sum(dpm * rolled[3 - kk][SUBLANES:SUBLANES + tt, :], axis=0, keepdims=True) for kk in range(CONV_K)]

        @pl.when((bi == 0) & (i == 0))
        def _():
            gw_ref[...] = jnp.zeros_like(gw_ref)
            gb_ref[...] = jnp.zeros_like(gb_ref)

        gw_ref[...] += jnp.concatenate(gws, axis=0)
        gb_ref[...] += jnp.sum(dpm, axis=0, keepdims=True)

    main = pl.BlockSpec((1, tt, CONV_TC), lambda c, b, i: (b, i, c))
    prev = pl.BlockSpec((1, SUBLANES, CONV_TC), lambda c, b, i: (b, jnp.maximum(i * hb - 1, 0), c))
    nxt = pl.BlockSpec((1, SUBLANES, CONV_TC), lambda c, b, i: (b, jnp.minimum((i + 1) * hb, seq // SUBLANES - 1), c))
    du, gw, gb = pl.pallas_call(
        body, name="conv_bwd", grid=(CONV_CH // CONV_TC, bsz, nt),
        in_specs=[main, nxt, main, prev, nxt,
                  pl.BlockSpec((CONV_K, CONV_TC), lambda c, b, i: (0, c)),
                  pl.BlockSpec((1, CONV_TC), lambda c, b, i: (0, c))],
        out_specs=[main, pl.BlockSpec((CONV_K, CONV_TC), lambda c, b, i: (0, c)),
                   pl.BlockSpec((1, CONV_TC), lambda c, b, i: (0, c))],
        out_shape=[jax.ShapeDtypeStruct((bsz, seq, CONV_CH), BF16), jax.ShapeDtypeStruct((CONV_K, CONV_CH), F32),
                   jax.ShapeDtypeStruct((1, CONV_CH), F32)],
        compiler_params=_params(("parallel", "arbitrary", "arbitrary")),
    )(da3, da3, x3, x3, x3, cw, cb)
    return du.reshape(bsz * seq, CONV_CH), gw, gb


def _ssd_prologue(dtf_ref, dtft_ref, bias_ref, biast_ref, arow_ref, acol_ref, dtfull_scr, acfull_scr, acr_scr):
    tri = (_iota((CHUNK, CHUNK), 1) <= _iota((CHUNK, CHUNK), 0)).astype(F32)
    triu = (_iota((CHUNK, CHUNK), 0) <= _iota((CHUNK, CHUNK), 1)).astype(F32)
    dtc = _softplus(dtf_ref[0] + bias_ref[...])
    a = dtc * arow_ref[...]
    acum = _sel_dot(tri, a)
    expand = _head_expand()
    dtfull_scr[...] = _dot_sel(dtc, expand)
    acfull_scr[...] = _dot_sel(acum, expand)
    dtr = _softplus(dtft_ref[0] + biast_ref[...])
    acr_scr[...] = _dot_sel(dtr * acol_ref[...], triu)
    return dtc, acum


def _decay_matrix(acum, acr_scr, h):
    lane = _iota((CHUNK, LANES), 1)
    ac_col = jnp.sum(jnp.where(lane == h, acum, 0.0), axis=1, keepdims=True)
    ac_row = acr_scr[h:h + 1, :]
    causal = _iota((CHUNK, CHUNK), 1) <= _iota((CHUNK, CHUNK), 0)
    return jnp.exp(jnp.where(causal, ac_col - ac_row, NEG))


def _ssd_fwd(xbc, dtf, dtft, bias, biast, arow, acol, dfull, bsz, seq):
    nc = seq // CHUNK
    x3 = xbc.reshape(bsz, seq, CONV_CH)

    def body(xbc_ref, dtf_ref, dtft_ref, bias_ref, biast_ref, arow_ref, acol_ref, dfull_ref,
             y_ref, hall_ref, h_scr, dtfull_scr, acfull_scr, acr_scr):
        @pl.when(pl.program_id(1) == 0)
        def _():
            h_scr[...] = jnp.zeros_like(h_scr)

        _, acum = _ssd_prologue(dtf_ref, dtft_ref, bias_ref, biast_ref, arow_ref, acol_ref,
                                dtfull_scr, acfull_scr, acr_scr)
        lane = _iota((CHUNK, LANES), 1)
        for g in range(2):
            bg = xbc_ref[0, :, D_MODEL + LANES * g:D_MODEL + LANES * (g + 1)]
            cg = xbc_ref[0, :, D_MODEL + 2 * LANES + LANES * g:D_MODEL + 2 * LANES + LANES * (g + 1)]
            cg_bf = cg.astype(BF16)
            gmat = _bdot_nt(cg_bf, bg)
            bgt_bf = bg.T.astype(BF16)
            for j in range(4):
                p = 4 * g + j
                sl = slice(LANES * p, LANES * (p + 1))
                xs = xbc_ref[0, :, sl]
                ac = acfull_scr[:, sl]
                acl = acfull_scr[CHUNK - 1:CHUNK, sl]
                xdt = xs * dtfull_scr[:, sl]
                xdt_bf = xdt.astype(BF16)
                hp = h_scr[p]
                hall_ref[0, 0, p] = hp
                yo = _bdot(cg_bf, hp) * jnp.exp(ac)
                yd = []
                for hh in range(2):
                    mmat = gmat * _decay_matrix(acum, acr_scr, 2 * p + hh)
                    yd.append(_bdot(mmat, xdt_bf))
                y_ref[0, :, sl] = jnp.where(lane < HEAD_DIM, yd[0], yd[1]) + yo + dfull_ref[:, sl] * xs
                h_scr[p] = hp * jnp.exp(acl) + _bdot(bgt_bf, xdt * jnp.exp(acl - ac))

    row = lambda w: pl.BlockSpec((1, w), lambda b, c: (0, 0))
    y, hall = pl.pallas_call(
        body, name="ssd_fwd", grid=(bsz, nc),
        in_specs=[pl.BlockSpec((1, CHUNK, CONV_CH), lambda b, c: (b, c, 0)),
                  pl.BlockSpec((1, CHUNK, LANES), lambda b, c: (b, c, 0)),
                  pl.BlockSpec((1, LANES, CHUNK), lambda b, c: (b, 0, c)),
                  row(LANES), pl.BlockSpec((LANES, 1), lambda b, c: (0, 0)),
                  row(LANES), pl.BlockSpec((LANES, 1), lambda b, c: (0, 0)), row(D_MODEL)],
        out_specs=[pl.BlockSpec((1, CHUNK, D_MODEL), lambda b, c: (b, c, 0)),
                   pl.BlockSpec((1, 1, HEAD_PAIRS, SSD_STATE, LANES), lambda b, c: (b, c, 0, 0, 0))],
        out_shape=[jax.ShapeDtypeStruct((bsz, seq, D_MODEL), F32),
                   jax.ShapeDtypeStruct((bsz, nc, HEAD_PAIRS, SSD_STATE, LANES), F32)],
        scratch_shapes=[pltpu.VMEM((HEAD_PAIRS, SSD_STATE, LANES), F32), pltpu.VMEM((CHUNK, D_MODEL), F32),
                        pltpu.VMEM((CHUNK, D_MODEL), F32), pltpu.VMEM((LANES, CHUNK), F32)],
        compiler_params=_params(("parallel", "arbitrary")),
    )(x3, dtf.reshape(bsz, seq, LANES), dtft, bias, biast, arow, acol, dfull)
    return y.reshape(bsz * seq, D_MODEL), hall


def _ssd_bwd(dy, xbc, dtf, dtft, bias, biast, arow, acol, dfull, hall, bsz, seq):
    nc = seq // CHUNK
    x3 = xbc.reshape(bsz, seq, CONV_CH)
    dy3 = dy.reshape(bsz, seq, D_MODEL)

    def body(dy_ref, xbc_ref, dtf_ref, dtft_ref, bias_ref, biast_ref, arow_ref, acol_ref, dfull_ref, hall_ref,
             dx_ref, ddt_ref, acc_ref, dh_scr, dtfull_scr, acfull_scr, acr_scr, csum_scr, dacf_scr, ddtf_scr, ddl_scr):
        first = (pl.program_id(0) == 0) & (pl.program_id(1) == 0)

        @pl.when(first)
        def _():
            acc_ref[...] = jnp.zeros_like(acc_ref)

        @pl.when(pl.program_id(1) == 0)
        def _():
            dh_scr[...] = jnp.zeros_like(dh_scr)

        dtc, acum = _ssd_prologue(dtf_ref, dtft_ref, bias_ref, biast_ref, arow_ref, acol_ref,
                                  dtfull_scr, acfull_scr, acr_scr)
        csum_scr[...] = jnp.zeros_like(csum_scr)
        lane = _iota((CHUNK, LANES), 1)
        last_row = _iota((CHUNK, LANES), 0) == CHUNK - 1
        rs16 = jnp.zeros((CHUNK, LANES), F32)
        for g in range(2):
            bsl = slice(D_MODEL + LANES * g, D_MODEL + LANES * (g + 1))
            csl = slice(D_MODEL + 2 * LANES + LANES * g, D_MODEL + 2 * LANES + LANES * (g + 1))
            bg_bf = xbc_ref[0, :, bsl].astype(BF16)
            cg = xbc_ref[0, :, csl]
            cg_bf = cg.astype(BF16)
            cgt_bf = cg.T.astype(BF16)
            gmat = _bdot_nt(cg_bf, bg_bf)
            dg = jnp.zeros((CHUNK, CHUNK), F32)
            dbg = jnp.zeros((CHUNK, SSD_STATE), F32)
            dcg = jnp.zeros((CHUNK, SSD_STATE), F32)
            for j in range(4):
                p = 4 * g + j
                sl = slice(LANES * p, LANES * (p + 1))
                xs = xbc_ref[0, :, sl]
                dt = dtfull_scr[:, sl]
                ac = acfull_scr[:, sl]
                acl = acfull_scr[CHUNK - 1:CHUNK, sl]
                dyp = dy_ref[0, :, sl]
                xdt = xs * dt
                xdt_bf = xdt.astype(BF16)
                e = jnp.exp(ac)
                dsd = jnp.exp(acl - ac)
                cd = jnp.exp(acl)
                xds_bf = (xdt * dsd).astype(BF16)
                hp = hall_ref[0, 0, p]
                hp_bf = hp.astype(BF16)
                dhn = dh_scr[p]
                dhn_bf = dhn.astype(BF16)
                yo = _bdot(cg_bf, hp_bf) * e
                dw_bf = (dyp * e).astype(BF16)
                dcg = dcg + _bdot_nt(dw_bf, hp_bf)
                dhin = _bdot(cgt_bf, dw_bf)
                dac = dyp * yo
                dacl = jnp.sum(dhn * hp, axis=0, keepdims=True) * cd
                dxds = _bdot(bg_bf, dhn_bf)
                dbg = dbg + _bdot_nt(xds_bf, dhn_bf)
                dxdt = dxds * dsd
                tdec = dxds * xdt * dsd
                dacl = dacl + jnp.sum(tdec, axis=0, keepdims=True)
                dac = dac - tdec
                dh_scr[p] = dhn * cd + dhin
                for hh in range(2):
                    h = 2 * p + hh
                    lm = (lane < HEAD_DIM) if hh == 0 else (lane >= HEAD_DIM)
                    dec = _decay_matrix(acum, acr_scr, h)
                    mmat = gmat * dec
                    dyh_bf = jnp.where(lm, dyp, 0.0).astype(BF16)
                    dm = _bdot_nt(dyh_bf, xdt_bf)
                    dxdt = dxdt + _bdot(mmat.T, dyh_bf)
                    dg = dg + dm * dec
                    q = dm * mmat
                    rs16 = rs16 + jnp.where(lane == h, jnp.sum(q, axis=1, keepdims=True), 0.0)
                    csum_scr[h:h + 1, :] = jnp.sum(q, axis=0, keepdims=True)
                dx_ref[0, :, sl] = dfull_ref[:, sl] * dyp + dxdt * dt
                dacf_scr[:, sl] = dac + jnp.where(last_row, dacl, 0.0)
                ddtf_scr[:, sl] = dxdt * xs
                ddl_scr[:, sl] = jnp.broadcast_to(jnp.sum(dyp * xs, axis=0, keepdims=True), (SUBLANES, LANES))
            dg_bf = dg.astype(BF16)
            dx_ref[0, :, bsl] = dbg + _bdot(dg.T, cg_bf)
            dx_ref[0, :, csl] = dcg + _bdot(dg_bf, bg_bf)
        reduce = _head_reduce()
        triu = (_iota((CHUNK, CHUNK), 0) <= _iota((CHUNK, CHUNK), 1)).astype(F32)
        dac16 = _dot_sel(dacf_scr[...], reduce) + rs16 - csum_scr[...].T
        da16 = _sel_dot(triu, dac16)
        ddt16 = da16 * arow_ref[...] + _dot_sel(ddtf_scr[...], reduce)
        ddtraw = ddt16 * _sigmoid(dtf_ref[0] + bias_ref[...])
        ddt_ref[0] = ddtraw
        acc_ref[0:8, :] += jnp.broadcast_to(jnp.sum(da16 * dtc * arow_ref[...], axis=0, keepdims=True), (SUBLANES, LANES))
        acc_ref[8:16, :] += _dot_sel(ddl_scr[...], reduce)
        acc_ref[16:24, :] += jnp.broadcast_to(jnp.sum(ddtraw, axis=0, keepdims=True), (SUBLANES, LANES))

    rev = lambda b, c: (b, nc - 1 - c, 0)
    row = lambda w: pl.BlockSpec((1, w), lambda b, c: (0, 0))
    dx, ddt, acc = pl.pallas_call(
        body, name="ssd_bwd", grid=(bsz, nc),
        in_specs=[pl.BlockSpec((1, CHUNK, D_MODEL), rev), pl.BlockSpec((1, CHUNK, CONV_CH), rev),
                  pl.BlockSpec((1, CHUNK, LANES), rev),
                  pl.BlockSpec((1, LANES, CHUNK), lambda b, c: (b, 0, nc - 1 - c)),
                  row(LANES), pl.BlockSpec((LANES, 1), lambda b, c: (0, 0)),
                  row(LANES), pl.BlockSpec((LANES, 1), lambda b, c: (0, 0)), row(D_MODEL),
                  pl.BlockSpec((1, 1, HEAD_PAIRS, SSD_STATE, LANES), lambda b, c: (b, nc - 1 - c, 0, 0, 0))],
        out_specs=[pl.BlockSpec((1, CHUNK, CONV_CH), rev), pl.BlockSpec((1, CHUNK, LANES), rev),
                   pl.BlockSpec((3 * SUBLANES, LANES), lambda b, c: (0, 0))],
        out_shape=[jax.ShapeDtypeStruct((bsz, seq, CONV_CH), F32), jax.ShapeDtypeStruct((bsz, seq, LANES), F32),
                   jax.ShapeDtypeStruct((3 * SUBLANES, LANES), F32)],
        scratch_shapes=[pltpu.VMEM((HEAD_PAIRS, SSD_STATE, LANES), F32), pltpu.VMEM((CHUNK, D_MODEL), F32),
                        pltpu.VMEM((CHUNK, D_MODEL), F32), pltpu.VMEM((LANES, CHUNK), F32),
                        pltpu.VMEM((LANES, CHUNK), F32), pltpu.VMEM((CHUNK, D_MODEL), F32),
                        pltpu.VMEM((CHUNK, D_MODEL), F32), pltpu.VMEM((SUBLANES, D_MODEL), F32)],
        compiler_params=_params(("arbitrary", "arbitrary")),
    )(dy3, x3, dtf.reshape(bsz, seq, LANES), dtft, bias, biast, arow, acol, dfull, hall)
    return dx.reshape(bsz * seq, CONV_CH), ddt.reshape(bsz * seq, LANES), acc


GROUP_W = D_MODEL // 2


def _gnorm_fwd(y, z, o_att, w):
    n = y.shape[0]
    tm = min(1024, n)

    def body(y_ref, z_ref, o_ref, w_ref, out_ref):
        zv = z_ref[...]
        g = y_ref[...] * (zv * _sigmoid(zv))
        for gi in range(2):
            sl = slice(GROUP_W * gi, GROUP_W * (gi + 1))
            gs = g[:, sl]
            r = lax.rsqrt(jnp.mean(gs * gs, axis=-1, keepdims=True) + NORM_EPS)
            out_ref[:, sl] = (gs * r * w_ref[:, sl]).astype(BF16)
        out_ref[:, D_MODEL:] = o_ref[...].astype(BF16)

    tok = pl.BlockSpec((tm, D_MODEL), lambda i: (i, 0))
    return pl.pallas_call(
        body, name="gnorm_fwd", grid=(n // tm,),
        in_specs=[tok, tok, tok, pl.BlockSpec((1, D_MODEL), lambda i: (0, 0))],
        out_specs=pl.BlockSpec((tm, MIX_WIDTH), lambda i: (i, 0)),
        out_shape=jax.ShapeDtypeStruct((n, MIX_WIDTH), BF16),
        compiler_params=_params(("parallel",)),
    )(y, z, o_att, w)


def _gnorm_bwd(dycat, y, z, w):
    n = y.shape[0]
    tm = min(512, n)

    def body(dn_ref, y_ref, z_ref, w_ref, dy_ref, dz_ref, gw_ref):
        zv, yv = z_ref[...], y_ref[...]
        sz = _sigmoid(zv)
        sil = zv * sz
        g = yv * sil

        @pl.when(pl.program_id(0) == 0)
        def _():
            gw_ref[...] = jnp.zeros_like(gw_ref)

        for gi in range(2):
            sl = slice(GROUP_W * gi, GROUP_W * (gi + 1))
            gs = g[:, sl]
            r = lax.rsqrt(jnp.mean(gs * gs, axis=-1, keepdims=True) + NORM_EPS)
            nrm = gs * r
            dyn = dn_ref[:, sl]
            dn = dyn * w_ref[:, sl]
            dgs = r * (dn - nrm * jnp.mean(dn * nrm, axis=-1, keepdims=True))
            dy_ref[:, sl] = dgs * sil[:, sl]
            dz_ref[:, sl] = (dgs * yv[:, sl] * (sz[:, sl] * (1.0 + zv[:, sl] * (1.0 - sz[:, sl])))).astype(BF16)
            gw_ref[:, sl] += jnp.sum(dyn * nrm, axis=0, keepdims=True)

    tok = pl.BlockSpec((tm, D_MODEL), lambda i: (i, 0))
    row = pl.BlockSpec((1, D_MODEL), lambda i: (0, 0))
    return pl.pallas_call(
        body, name="gnorm_bwd", grid=(n // tm,),
        in_specs=[tok, tok, tok, row], out_specs=[tok, tok, row],
        out_shape=[jax.ShapeDtypeStruct((n, D_MODEL), F32), jax.ShapeDtypeStruct((n, D_MODEL), BF16),
                   jax.ShapeDtypeStruct((1, D_MODEL), F32)],
        compiler_params=_params(("arbitrary",)),
    )(dycat, y, z, w)


def _fox_prep(dtf, fb, bsz, seq):
    def body(x_ref, b_ref, c_ref):
        tri = (_iota((CHUNK, CHUNK), 1) <= _iota((CHUNK, CHUNK), 0)).astype(F32)
        carry = jnp.zeros((1, LANES), F32)
        for j in range(seq // CHUNK):
            sl = slice(CHUNK * j, CHUNK * (j + 1))
            lf = _log_sigmoid(x_ref[sl, :] + b_ref[...])
            c_ref[sl, :] = _sel_dot(tri, lf) + carry
            carry = carry + jnp.sum(lf, axis=0, keepdims=True)

    blk = pl.BlockSpec((seq, LANES), lambda b: (b, 0))
    return pl.pallas_call(
        body, name="fox_prep", grid=(bsz,),
        in_specs=[blk, pl.BlockSpec((1, LANES), lambda b: (0, 0))], out_specs=blk,
        out_shape=jax.ShapeDtypeStruct((bsz * seq, LANES), F32),
        compiler_params=_params(("parallel",)),
    )(dtf, fb)


def _fox_prep_bwd(dck, dcq, ddt, dtf, fb, bsz, seq):
    nj = seq // CHUNK

    def body(dck_ref, dcq_ref, ddt_ref, x_ref, b_ref, out_ref, gb_ref):
        triu = (_iota((CHUNK, CHUNK), 0) <= _iota((CHUNK, CHUNK), 1)).astype(F32)
        is_f = (_iota((CHUNK, LANES), 1) >= F_COL) & (_iota((CHUNK, LANES), 1) < 2 * F_COL)
        carry = jnp.zeros((1, LANES), F32)
        total = jnp.zeros((1, LANES), F32)
        for j in range(nj - 1, -1, -1):
            sl = slice(CHUNK * j, CHUNK * (j + 1))
            dcv = dck_ref[sl, :] + dcq_ref[sl, :]
            dlf = _sel_dot(triu, dcv) + carry
            carry = carry + jnp.sum(dcv, axis=0, keepdims=True)
            df = jnp.where(is_f, dlf * _sigmoid(-(x_ref[sl, :] + b_ref[...])), 0.0)
            out_ref[sl, :] = (df + ddt_ref[sl, :]).astype(BF16)
            total = total + jnp.sum(df, axis=0, keepdims=True)

        @pl.when(pl.program_id(0) == 0)
        def _():
            gb_ref[...] = jnp.zeros_like(gb_ref)

        gb_ref[...] += jnp.broadcast_to(total, (SUBLANES, LANES))

    blk = pl.BlockSpec((seq, LANES), lambda b: (b, 0))
    return pl.pallas_call(
        body, name="fox_prep_bwd", grid=(bsz,),
        in_specs=[blk, blk, blk, blk, pl.BlockSpec((1, LANES), lambda b: (0, 0))],
        out_specs=[blk, pl.BlockSpec((SUBLANES, LANES), lambda b: (0, 0))],
        out_shape=[jax.ShapeDtypeStruct((bsz * seq, LANES), BF16), jax.ShapeDtypeStruct((SUBLANES, LANES), F32)],
        compiler_params=_params(("arbitrary",)),
    )(dck, dcq, ddt, dtf, fb)


AUX_C = 3
AUX_ONE = 3


ATT_BLOCK_FWD = 512
ATT_BLOCK_BWD = 256


def _col_to_row(col):
    n = col.shape[0]
    spread = jnp.where(_iota((n, LANES), 1) == 0, col, 0.0)
    return lax.dot_general(jnp.ones((SUBLANES, LANES), F32), spread, (((1,), (1,)), ((), ())),
                           precision=HI, preferred_element_type=F32)


def _att_operands(q_ref, k_ref, c_ref, seq, hh, pair):
    lane = _iota((seq, LANES), 1)
    lm = (lane < HEAD_DIM) if hh == 0 else (lane >= HEAD_DIM)
    base = HEAD_DIM * (1 - hh)
    negc = -jnp.sum(jnp.where(lane == F_COL + 2 * pair + hh, c_ref[...], 0.0), axis=1, keepdims=True)
    c1 = negc.astype(BF16)
    r1 = negc - c1.astype(F32)
    c2 = r1.astype(BF16)
    c3 = (r1 - c2.astype(F32)).astype(BF16)
    zero = jnp.zeros((seq, LANES), BF16)
    one = jnp.ones((seq, LANES), BF16)
    ka = jnp.where(lm, k_ref[...], jnp.where(lane == base, c1, jnp.where(lane == base + 1, c2, jnp.where(
        lane == base + 2, c3, jnp.where(lane == base + AUX_ONE, one, zero)))))
    qa = jnp.where(lm, q_ref[...] * ATT_SCALE, jnp.where((lane >= base) & (lane < base + AUX_C), one, zero))
    return lm, base, qa, ka


def _att_fwd(qkv, ccol, bsz, seq, gather):
    blk = min(ATT_BLOCK_FWD, seq)
    nb = seq // blk
    nx = len(gather)

    def body(*refs):
        q_ref, k_ref, v_ref, c_ref = refs[:4]
        x_refs = refs[4:4 + nx]
        o_ref, lse_ref = refs[4 + nx:6 + nx]
        xo_refs = refs[6 + nx:6 + 2 * nx]
        qa_scr, ka_scr, va_scr = refs[6 + 2 * nx:9 + 2 * nx]
        sems = refs[9 + 2 * nx:]
        pair = pl.program_id(1)

        @pl.when((pl.program_id(0) == 0) & (pair == 0))
        def _():
            _exchange_start("gather", x_refs, xo_refs, *sems)

        lse_ref[...] = jnp.zeros_like(lse_ref)
        causal = _iota((blk, blk), 0) >= _iota((blk, blk), 1)
        for hh in range(2):
            lm, _, qa, ka = _att_operands(q_ref, k_ref, c_ref, seq, hh, pair)
            qa_scr[hh] = qa
            ka_scr[hh] = ka
            va_scr[hh] = jnp.where(lm, v_ref[...], jnp.zeros((seq, LANES), BF16))

        def q_step(i, carry0):
            rows = pl.ds(pl.multiple_of(i * blk, blk), blk)

            def scores(j):
                krows = pl.ds(pl.multiple_of(j * blk, blk), blk)
                return tuple(_bdot_nt(qa_scr[hh, rows, :], ka_scr[hh, krows, :]) for hh in range(2))

            def update(state, s_pair, j, masked):
                krows = pl.ds(pl.multiple_of(j * blk, blk), blk)
                alphas, ls, pvs, ms = [], [], [], []
                for hh in range(2):
                    m, l, _ = state[hh]
                    s = jnp.where(causal, s_pair[hh], NEG) if masked else s_pair[hh]
                    mn = jnp.maximum(m, jnp.max(s, axis=1, keepdims=True))
                    alpha = jnp.exp(m - mn)
                    pr = jnp.exp(s - mn)
                    ms.append(mn)
                    alphas.append(alpha)
                    ls.append(alpha * l + jnp.sum(pr, axis=1, keepdims=True))
                    pvs.append(_bdot(pr, va_scr[hh, krows, :]))
                return tuple((ms[hh], ls[hh], alphas[hh] * state[hh][2] + pvs[hh]) for hh in range(2))

            def step(j, carry):
                state, s_cur = carry
                s_next = scores(j + 1)
                return update(state, s_cur, j, False), s_next

            one = (jnp.full((blk, 1), NEG, F32), jnp.zeros((blk, 1), F32), jnp.zeros((blk, LANES), F32))
            state, s_last = lax.fori_loop(0, i, step, ((one, one), scores(0)))
            (m0, l0, acc0), (m1, l1, acc1) = update(state, s_last, i, True)
            o_ref[rows, :] = acc0 * (1.0 / l0) + acc1 * (1.0 / l1)
            lse_ref[0, 0, 0:1, rows] = _col_to_row(m0 + jnp.log(l0))[0:1]
            lse_ref[0, 0, 1:2, rows] = _col_to_row(m1 + jnp.log(l1))[0:1]
            return carry0

        lax.fori_loop(0, nb, q_step, 0)

        @pl.when((pl.program_id(0) == bsz - 1) & (pair == HEAD_PAIRS - 1))
        def _():
            _exchange_wait("gather", x_refs, xo_refs, *sems)

    col = lambda off: pl.BlockSpec((seq, LANES), lambda b, p: (b, off + p))
    head2 = pltpu.VMEM((2, seq, LANES), BF16)
    hbm = pl.BlockSpec(memory_space=pl.ANY)
    return pl.pallas_call(
        body, name="att_fwd", grid=(bsz, HEAD_PAIRS),
        in_specs=[col(0), col(HEAD_PAIRS), col(2 * HEAD_PAIRS), pl.BlockSpec((seq, LANES), lambda b, p: (b, 0))] + [hbm] * nx,
        out_specs=[pl.BlockSpec((seq, LANES), lambda b, p: (b, p)),
                   pl.BlockSpec((1, 1, SUBLANES, seq), lambda b, p: (b, p, 0, 0))] + [hbm] * nx,
        out_shape=[jax.ShapeDtypeStruct((bsz * seq, D_MODEL), F32),
                   jax.ShapeDtypeStruct((bsz, HEAD_PAIRS, SUBLANES, seq), F32)]
        + _exchange_shapes("gather", gather),
        scratch_shapes=[head2, head2, head2] + _exchange_scratch(nx),
        compiler_params=_params(("arbitrary", "arbitrary")),
    )(qkv, qkv, qkv, ccol, *gather)


def _att_bwd(qkv, dycat, o, lse, ccol, bsz, seq, scatter):
    blk = min(ATT_BLOCK_BWD, seq)
    nb = seq // blk
    nx = len(scatter)

    def body(*refs):
        q_ref, k_ref, v_ref, do_ref, o_ref, lse_ref, c_ref = refs[:7]
        x_refs = refs[7:7 + nx]
        dq_ref, dk_ref, dv_ref, dck_ref, dcq_ref = refs[7 + nx:12 + nx]
        xo_refs = refs[12 + nx:12 + 2 * nx]
        qa_scr, ka_scr, va_scr, doa_scr, kat_scr, d_scr, dqt_scr, dk_scr, dv_scr = refs[12 + 2 * nx:21 + 2 * nx]
        sems = refs[21 + 2 * nx:]
        pair = pl.program_id(1)

        @pl.when((pl.program_id(0) == 0) & (pair == 0))
        def _():
            _exchange_start("scatter", x_refs, xo_refs, *sems)

        causal_t = _iota((blk, blk), 1) >= _iota((blk, blk), 0)
        lane_b = _iota((blk, LANES), 1)
        row_t = _iota((LANES, seq), 0)
        masks, bases = [], []
        ones8 = jnp.ones((SUBLANES, LANES), F32)
        for hh in range(2):
            lm, base, qa, ka = _att_operands(q_ref, k_ref, c_ref, seq, hh, pair)
            masks.append(lm)
            bases.append(base)
            qa_scr[hh] = qa
            ka_scr[hh] = ka
            va_scr[hh] = jnp.where(lm, v_ref[...], jnp.zeros((seq, LANES), BF16))
            doa = jnp.where(lm, do_ref[...], 0.0).astype(BF16)
            doa_scr[hh] = doa
            for t0 in range(0, seq, blk):
                kat_scr[hh, :, t0:t0 + blk] = ka[t0:t0 + blk, :].astype(F32).T.astype(BF16)
            d_scr[hh] = lax.dot_general(ones8, doa.astype(F32) * o_ref[...], (((1,), (1,)), ((), ())),
                                        precision=HI, preferred_element_type=F32)
        dqt_scr[...] = jnp.zeros_like(dqt_scr)
        dck_ref[...] = jnp.zeros_like(dck_ref)

        def kv_step(j, carry0):
            krows = pl.ds(pl.multiple_of(j * blk, blk), blk)
            dk_scr[...] = jnp.zeros_like(dk_scr)
            dv_scr[...] = jnp.zeros_like(dv_scr)

            def scores(i):
                rows = pl.ds(pl.multiple_of(i * blk, blk), blk)
                return tuple((_bdot_nt(ka_scr[hh, krows, :], qa_scr[hh, rows, :]),
                              _bdot_nt(va_scr[hh, krows, :], doa_scr[hh, rows, :])) for hh in range(2))

            def update(i, sd, masked):
                rows = pl.ds(pl.multiple_of(i * blk, blk), blk)
                for hh in range(2):
                    st, dpt = sd[hh]
                    if masked:
                        st = jnp.where(causal_t, st, NEG)
                    pt = jnp.exp(st - lse_ref[0, 0, hh:hh + 1, rows])
                    dst = (pt * (dpt - d_scr[hh, 0:1, rows])).astype(BF16)
                    dv_scr[hh] += _bdot(pt, doa_scr[hh, rows, :])
                    dk_scr[hh] += _bdot(dst, qa_scr[hh, rows, :])
                    dqt_scr[hh, :, rows] += _bdot(kat_scr[hh, :, krows], dst)

            def q_step(i, sd_cur):
                sd_next = scores(jnp.minimum(i + 1, nb - 1))
                update(i, sd_cur, False)
                return sd_next

            sd_diag = scores(j)
            sd_first = scores(jnp.minimum(j + 1, nb - 1))
            update(j, sd_diag, True)
            lax.fori_loop(j + 1, nb, q_step, sd_first)
            lmb0 = lane_b < HEAD_DIM
            dk_ref[krows, :] = jnp.where(lmb0, dk_scr[0], dk_scr[1]).astype(BF16)
            dv_ref[krows, :] = (dv_scr[0] + dv_scr[1]).astype(BF16)
            for hh in range(2):
                colsum = jnp.sum(jnp.where(lane_b == bases[hh], dk_scr[hh], 0.0), axis=1, keepdims=True)
                dck_ref[0, 0, hh:hh + 1, krows] = -_col_to_row(colsum)[0:1]
            return carry0

        lax.fori_loop(0, nb, kv_step, 0)
        for t0 in range(0, seq, blk):
            dq0, dq1 = dqt_scr[0, :, t0:t0 + blk].T, dqt_scr[1, :, t0:t0 + blk].T
            dq_ref[t0:t0 + blk, :] = (jnp.where(lane_b < HEAD_DIM, dq0, dq1) * ATT_SCALE).astype(BF16)
        dcq_ref[...] = jnp.zeros_like(dcq_ref)
        for hh in range(2):
            dcq_ref[0, 0, hh:hh + 1, :] = jnp.sum(jnp.where(row_t == bases[hh] + AUX_ONE, dqt_scr[hh], 0.0),
                                                   axis=0, keepdims=True)

        @pl.when((pl.program_id(0) == bsz - 1) & (pair == HEAD_PAIRS - 1))
        def _():
            _exchange_wait("scatter", x_refs, xo_refs, *sems)

    col = lambda off: pl.BlockSpec((seq, LANES), lambda b, p: (b, off + p))
    rowvec = pl.BlockSpec((1, 1, SUBLANES, seq), lambda b, p: (b, p, 0, 0))
    out = jax.ShapeDtypeStruct((bsz * seq, D_MODEL), BF16)
    rows_shape = jax.ShapeDtypeStruct((bsz, HEAD_PAIRS, SUBLANES, seq), F32)
    head2 = pltpu.VMEM((2, seq, LANES), BF16)
    hbm = pl.BlockSpec(memory_space=pl.ANY)
    return pl.pallas_call(
        body, name="att_bwd", grid=(bsz, HEAD_PAIRS),
        in_specs=[col(0), col(HEAD_PAIRS), col(2 * HEAD_PAIRS), col(HEAD_PAIRS), col(0), rowvec,
                  pl.BlockSpec((seq, LANES), lambda b, p: (b, 0))] + [hbm] * nx,
        out_specs=[col(0), col(0), col(0), rowvec, rowvec] + [hbm] * nx,
        out_shape=[out, out, out, rows_shape, rows_shape] + _exchange_shapes("scatter", scatter),
        scratch_shapes=[head2, head2, head2, head2, pltpu.VMEM((2, LANES, seq), BF16),
                        pltpu.VMEM((2, SUBLANES, seq), F32), pltpu.VMEM((2, LANES, seq), F32),
                        pltpu.VMEM((2, blk, LANES), F32), pltpu.VMEM((2, blk, LANES), F32)] + _exchange_scratch(nx),
        compiler_params=_params(("arbitrary", "arbitrary")),
    )(qkv, qkv, qkv, dycat, o, lse, ccol, *scatter)


def _adamw_math(w, g, m, v):
    m = ADAM_B1 * m + (1.0 - ADAM_B1) * g
    v = ADAM_B2 * v + (1.0 - ADAM_B2) * jnp.square(g)
    m_hat = m / ADAM_C1
    v_hat = v / ADAM_C2
    delta = -ADAM_LR * (m_hat / (jnp.sqrt(v_hat) + ADAM_EPS) + ADAM_WD * w)
    return delta, m, v


def _row_tile(rows):
    for tr in (256, 128, 64, 32, 16, 8):
        if rows % tr == 0:
            return tr
    return rows


def _sum_parts(parts, name):
    nparts, rows, cols = parts.shape
    tr = rows if rows % SUBLANES else _row_tile(rows)

    def body(p_ref, o_ref):
        g = p_ref[0].astype(F32)
        for s in range(1, nparts):
            g = g + p_ref[s].astype(F32)
        o_ref[...] = g

    return pl.pallas_call(
        body, name=name, grid=(rows // tr,),
        in_specs=[pl.BlockSpec((nparts, tr, cols), lambda i: (0, i, 0))],
        out_specs=pl.BlockSpec((tr, cols), lambda i: (i, 0)),
        out_shape=jax.ShapeDtypeStruct((rows, cols), F32),
        compiler_params=_params(("parallel",)),
    )(parts)


def _adamw_parts(parts, w, m, v, name):
    nparts, rows, cols = parts.shape
    tr = _row_tile(rows)

    def body(p_ref, w_ref, m_ref, v_ref, g_ref, d_ref, mo_ref, vo_ref):
        g = p_ref[0].astype(F32)
        for s in range(1, nparts):
            g = g + p_ref[s].astype(F32)
        delta, mn, vn = _adamw_math(w_ref[...], g, m_ref[...], v_ref[...])
        g_ref[...] = g
        d_ref[...] = delta
        mo_ref[...] = mn
        vo_ref[...] = vn

    blk = pl.BlockSpec((tr, cols), lambda i: (i, 0))
    shp = jax.ShapeDtypeStruct((rows, cols), F32)
    return pl.pallas_call(
        body, name=name, grid=(rows // tr,),
        in_specs=[pl.BlockSpec((nparts, tr, cols), lambda i: (0, i, 0)), blk, blk, blk],
        out_specs=[blk, blk, blk, blk], out_shape=[shp, shp, shp, shp],
        compiler_params=_params(("parallel",)),
    )(parts, w, m, v)


def _me():
    return lax.axis_index("x"), lax.axis_index("y"), lax.axis_index("c")


def _flip(pos, k):
    x, y, c = pos
    kx, ky, kc = (k >> 2) & 1, (k >> 1) & 1, k & 1
    return (x ^ kx if kx else x, y ^ ky if ky else y, c ^ kc if kc else c)


def _logical(pos):
    return 4 * pos[0] + 2 * pos[1] + pos[2]


def _all_gather_two_level(shards):
    narr = len(shards)

    def body(*refs):
        x_refs, out_refs = refs[:narr], refs[narr:2 * narr]
        send_sems, recv_sems, local_sems = refs[2 * narr:]
        x, y, c = _me()
        me, sibling = (x, y, c), (x, y, 1 - c)
        chips = [(1 - x, y), (x, 1 - y), (1 - x, 1 - y)]

        def copy(a, k, block, to, src=None):
            slot = out_refs[a].at[_logical(block)]
            return pltpu.make_async_remote_copy(
                src_ref=slot if src is None else src, dst_ref=slot,
                send_sem=send_sems.at[a, k], recv_sem=recv_sems.at[a, k], device_id=to, device_id_type=MESH)

        mine = [pltpu.make_async_copy(x_refs[a], out_refs[a].at[_logical(me)], local_sems.at[a]) for a in range(narr)]
        for cp in mine:
            cp.start()
        first = []
        for a in range(narr):
            first.append(copy(a, 0, me, sibling, src=x_refs[a]))
            first += [copy(a, 1 + j, me, (*chip, c), src=x_refs[a]) for j, chip in enumerate(chips)]
        for cp in first:
            cp.start()
        passed = []
        for a in range(narr):
            for j, chip in enumerate(chips):
                copy(a, 1 + j, (*chip, c), me).wait_recv()
                fwd = copy(a, 4 + j, (*chip, c), sibling)
                fwd.start()
                passed.append(fwd)
        for a in range(narr):
            copy(a, 0, sibling, me).wait_recv()
            for j, chip in enumerate(chips):
                copy(a, 4 + j, (*chip, 1 - c), me).wait_recv()
        for cp in first + passed:
            cp.wait_send()
        for cp in mine:
            cp.wait()

    hbm = pl.BlockSpec(memory_space=pl.ANY)
    return pl.pallas_call(
        body, name="all_gather_big",
        out_shape=[jax.ShapeDtypeStruct((N_DEV,) + s.shape, s.dtype) for s in shards],
        in_specs=[hbm] * narr, out_specs=[hbm] * narr,
        scratch_shapes=[pltpu.SemaphoreType.DMA((narr, 7)), pltpu.SemaphoreType.DMA((narr, 7)),
                        pltpu.SemaphoreType.DMA((narr,))],
    )(*shards)


def _exchange_copies(kind, x_refs, out_refs, send_sems, recv_sems, local_sems):
    me = _me()
    mine = _logical(me)
    local, remote = [], []
    for a, (x_ref, out_ref) in enumerate(zip(x_refs, out_refs)):
        own = x_ref if kind == "gather" else x_ref.at[mine]
        local.append(pltpu.make_async_copy(own, out_ref.at[mine], local_sems.at[a]))
        for k in range(1, N_DEV):
            peer = _flip(me, k)
            src = x_ref if kind == "gather" else x_ref.at[_logical(peer)]
            remote.append(pltpu.make_async_remote_copy(
                src_ref=src, dst_ref=out_ref.at[mine], send_sem=send_sems.at[a, k - 1], recv_sem=recv_sems.at[a, k - 1],
                device_id=peer, device_id_type=MESH))
    return local, remote


def _exchange_start(kind, x_refs, out_refs, *sems):
    if not x_refs:
        return
    local, remote = _exchange_copies(kind, x_refs, out_refs, *sems)
    for cp in local + remote:
        cp.start()


def _exchange_wait(kind, x_refs, out_refs, *sems):
    if not x_refs:
        return
    local, remote = _exchange_copies(kind, x_refs, out_refs, *sems)
    for cp in remote:
        cp.wait_recv()
    for cp in remote:
        cp.wait_send()
    for cp in local:
        cp.wait()


def _exchange_shapes(kind, arrays):
    return [jax.ShapeDtypeStruct(((N_DEV,) if kind == "gather" else ()) + a.shape, a.dtype) for a in arrays]


def _exchange_scratch(narrays):
    if not narrays:
        return []
    return [pltpu.SemaphoreType.DMA((narrays, N_DEV - 1)), pltpu.SemaphoreType.DMA((narrays, N_DEV - 1)),
            pltpu.SemaphoreType.DMA((narrays,))]


def _exchange_rows(x_ref, buf_ref, send_sems, recv_sems):
    me = _me()
    buf_ref[_logical(me)] = x_ref[...]
    copies = []
    for k in range(1, N_DEV):
        peer = _flip(me, k)
        copies.append(pltpu.make_async_remote_copy(
            src_ref=x_ref, dst_ref=buf_ref.at[_logical(me)],
            send_sem=send_sems.at[k - 1], recv_sem=recv_sems.at[k - 1], device_id=peer, device_id_type=MESH))
    for cp in copies:
        cp.start()
    for cp in copies:
        cp.wait_recv()
    for cp in copies:
        cp.wait_send()


def _sum_slots(buf_ref):
    total = buf_ref[0]
    for s in range(1, N_DEV):
        total = total + buf_ref[s]
    return total


def _small_gather(x):
    def body(x_ref, o_ref, buf_ref, send_sems, recv_sems):
        _exchange_rows(x_ref, buf_ref, send_sems, recv_sems)
        o_ref[...] = _sum_slots(buf_ref)

    return pl.pallas_call(
        body, name="small_gather", out_shape=jax.ShapeDtypeStruct(x.shape, F32),
        in_specs=[pl.BlockSpec(memory_space=pltpu.VMEM)], out_specs=pl.BlockSpec(memory_space=pltpu.VMEM),
        scratch_shapes=[pltpu.VMEM((N_DEV,) + x.shape, F32), pltpu.SemaphoreType.DMA((7,)), pltpu.SemaphoreType.DMA((7,))],
    )(x)


def _small_reduce_adamw(g, w, m, v):
    def body(g_ref, w_ref, m_ref, v_ref, go_ref, d_ref, mo_ref, vo_ref, buf_ref, send_sems, recv_sems):
        _exchange_rows(g_ref, buf_ref, send_sems, recv_sems)
        gs = _sum_slots(buf_ref)
        delta, mn, vn = _adamw_math(w_ref[...], gs, m_ref[...], v_ref[...])
        go_ref[...] = gs
        d_ref[...] = delta
        mo_ref[...] = mn
        vo_ref[...] = vn

    vm = pl.BlockSpec(memory_space=pltpu.VMEM)
    shp = jax.ShapeDtypeStruct(g.shape, F32)
    return pl.pallas_call(
        body, name="small_reduce_adamw", out_shape=[shp, shp, shp, shp],
        in_specs=[vm, vm, vm, vm], out_specs=[vm, vm, vm, vm],
        scratch_shapes=[pltpu.VMEM((N_DEV,) + g.shape, F32), pltpu.SemaphoreType.DMA((7,)), pltpu.SemaphoreType.DMA((7,))],
    )(g, w, m, v)


def _pad_cols(a, width):
    return jnp.pad(a, ((0, 0), (0, width - a.shape[1])))


def _local_step(x, target, norm_mix_w, w_in_t, conv_w, conv_b, dt_bias, a_log, d_skip, ssd_norm_w, f_bias,
                late_shards, norm_mlp_w, norm_final_w):
    bsz, seq, _ = x.shape
    n = bsz * seq
    x2 = x.reshape(n, D_MODEL)
    t2 = target.reshape(n, D_MODEL)
    nfw = norm_final_w.reshape(1, D_MODEL)

    bias = _pad_cols(dt_bias, LANES)
    arow = _pad_cols(-jnp.exp(a_log), LANES)
    biast, acol = bias.reshape(LANES, 1), arow.reshape(LANES, 1)
    dfull = jnp.repeat(d_skip, HEAD_DIM, axis=1)
    fb = jnp.pad(f_bias, ((0, 0), (F_COL, LANES - 2 * F_COL)))

    wt_z, wt_xbc, wt_qkv = w_in_t[:ROW_XBC], w_in_t[ROW_XBC:ROW_DT], w_in_t[ROW_Q:ROW_F]
    wt_dtf = jnp.concatenate([w_in_t[ROW_DT:ROW_Q], w_in_t[ROW_F:], jnp.zeros((LANES - 2 * F_COL, D_MODEL), BF16)], axis=0)
    wt_q, wt_k, wt_v = wt_qkv[:D_MODEL], wt_qkv[D_MODEL:2 * D_MODEL], wt_qkv[2 * D_MODEL:]

    h1 = _rms_fwd(x2, norm_mix_w, "rms_fwd_mix")
    z = _mm([(h1, wt_z)], "inproj_z", F32, 1024, 1024, nt=True)
    xbc_raw = _mm([(h1, wt_xbc)], "inproj_xbc", F32, 512, 1536, nt=True)
    qkv = _mm([(h1, wt_qkv)], "inproj_qkv", BF16, 512, 3072, nt=True)
    dtf = _mm([(h1, wt_dtf)], "inproj_dtf", F32, 2048, LANES, nt=True)
    dtft = dtf.reshape(bsz, seq, LANES).transpose(0, 2, 1)

    xbc = _conv_fwd(xbc_raw, conv_w, conv_b, bsz, seq)
    y_pre, hall = _ssd_fwd(xbc, dtf, dtft, bias, biast, arow, acol, dfull, bsz, seq)

    ccol = _fox_prep(dtf, fb, bsz, seq)
    o_att, lse, w_out, w_up_t, w_down = _att_fwd(qkv, ccol, bsz, seq, late_shards)
    w_out, w_up_t, w_down = (w.reshape(-1, D_MODEL) for w in (w_out, w_up_t, w_down))

    ycat = _gnorm_fwd(y_pre, z, o_att, ssd_norm_w)
    h2, h2n = _mm([(ycat, w_out)], "outproj", F32, 512, 1024, res=x2, rms_fwd=norm_mlp_w)
    pre = _mm([(h2n, w_up_t)], "mlp_up", F32, 256, 4096, nt=True)

    dh3, loss_row, g_nfw = _mlp_down_loss(pre, w_down, h2, t2, nfw)
    dpre, u = _mlp_down_bwd(dh3, w_down, pre)
    g_w_down = _mm_tn(u, dh3, "grad_w_down", 1024, 1024, 2048)
    g_w_up_t = _mm_tn(dpre, h2n, "grad_w_up", 1024, 1024, 2048)
    by_shard = lambda g: g.reshape(N_DEV, g.shape[0] // N_DEV, D_MODEL)
    dh2, g_nmlp = _mm([(dpre, w_up_t)], "mlp_up_bwd", F32, 512, 1024, res=dh3, rms_bwd=(h2, norm_mlp_w))

    g_w_out = _mm_tn(ycat, dh2, "grad_w_out", 1024, 1024, 2048)
    dycat = _mm([(dh2, w_out)], "outproj_bwd", F32, 512, 2048, nt=True)
    dy_pre, dz, g_ssdn = _gnorm_bwd(dycat, y_pre, z, ssd_norm_w)
    dq, dk, dv, dck, dcq, recv_down, recv_up_t, recv_out = _att_bwd(
        qkv, dycat, o_att, lse, ccol, bsz, seq, [by_shard(g_w_down), by_shard(g_w_up_t), by_shard(g_w_out)])

    dxbc, ddt, ssd_acc = _ssd_bwd(dy_pre, xbc, dtf, dtft, bias, biast, arow, acol, dfull, hall, bsz, seq)
    dxbc_raw, g_cw, g_cb = _conv_bwd(dxbc, xbc_raw, conv_w, conv_b, bsz, seq)

    def head_cols(rows):
        cols = rows[:, :, :2, :].reshape(bsz, SSD_HEADS, seq).transpose(0, 2, 1).reshape(n, SSD_HEADS)
        return jnp.pad(cols, ((0, 0), (F_COL, LANES - 2 * F_COL)))

    ddtf, g_fb = _fox_prep_bwd(head_cols(dck), head_cols(dcq), ddt, dtf, fb, bsz, seq)

    g_dtf = _mm_tn(ddtf, h1, "grad_w_in_dtf", LANES, 1024, 2048)
    g_w_in_t = jnp.concatenate([
        _mm_tn(dz, h1, "grad_w_in_z", 1024, 1024, 2048), _mm_tn(dxbc_raw, h1, "grad_w_in_xbc", 768, 1024, 2048),
        g_dtf[:F_COL], _mm_tn(dq, h1, "grad_w_in_q", 1024, 1024, 2048), _mm_tn(dk, h1, "grad_w_in_k", 1024, 1024, 2048),
        _mm_tn(dv, h1, "grad_w_in_v", 1024, 1024, 2048), g_dtf[F_COL:2 * F_COL]], axis=0)
    pieces = [(dz, wt_z), (dxbc_raw, wt_xbc), (dq, wt_q), (dk, wt_k), (dv, wt_v), (ddtf, wt_dtf)]
    dx, g_nmix, recv_in_t = _mm(pieces, "inproj_bwd", F32, 256, 1024, res=dh2, rms_bwd=(x2, norm_mix_w),
                                exchange=("scatter", [by_shard(g_w_in_t)]))

    small = dict(
        norm_mix_w=g_nmix, norm_mlp_w=g_nmlp, norm_final_w=g_nfw, ssd_norm_w=g_ssdn, conv_b=g_cb, conv_w=g_cw,
        dt_bias=ssd_acc[16:17, :SSD_HEADS], a_log=ssd_acc[0:1, :SSD_HEADS], d_skip=ssd_acc[8:9, :SSD_HEADS],
        f_bias=g_fb[0:1, F_COL:2 * F_COL])
    recv = dict(w_in_t=recv_in_t, w_out=recv_out, w_up_t=recv_up_t, w_down=recv_down)
    return loss_row[0, 0], dx.reshape(bsz, seq, D_MODEL), small, recv


SMALL_LAYOUT = (("norm_mix_w", 0, 1, 0, D_MODEL), ("norm_mlp_w", 1, 1, 0, D_MODEL), ("norm_final_w", 2, 1, 0, D_MODEL),
                ("ssd_norm_w", 3, 1, 0, D_MODEL), ("conv_b", 4, 1, 0, CONV_CH), ("conv_w", 5, CONV_K, 0, CONV_CH),
                ("dt_bias", 9, 1, 0, SSD_HEADS), ("a_log", 9, 1, LANES, SSD_HEADS), ("d_skip", 9, 1, 2 * LANES, SSD_HEADS),
                ("f_bias", 9, 1, 3 * LANES, SSD_HEADS))
SMALL_ROWS = 2 * SUBLANES


def _pack_small(vals):
    packed = jnp.zeros((SMALL_ROWS, SMALL_COLS), F32)
    for name, r0, nrows, c0, width in SMALL_LAYOUT:
        packed = packed.at[r0:r0 + nrows, c0:c0 + width].set(vals[name].reshape(nrows, width))
    return packed


def _unpack_small(packed, like):
    out = {}
    for name, r0, nrows, c0, width in SMALL_LAYOUT:
        out[name] = packed[r0:r0 + nrows, c0:c0 + width].reshape(like[name].shape)
    return out


def kernel(x, norm_mix_w, w_in, conv_w, conv_b, dt_bias, a_log, d_skip, ssd_norm_w, f_bias, w_out, norm_mlp_w, w_up, w_down, norm_final_w, loss_target, m_norm_mix_w, m_w_in, m_conv_w, m_conv_b, m_dt_bias, m_a_log, m_d_skip, m_ssd_norm_w, m_f_bias, m_w_out, m_norm_mlp_w, m_w_up, m_w_down, m_norm_final_w, v_norm_mix_w, v_w_in, v_conv_w, v_conv_b, v_dt_bias, v_a_log, v_d_skip, v_ssd_norm_w, v_f_bias, v_w_out, v_norm_mlp_w, v_w_up, v_w_down, v_norm_final_w):
    me = 4 * lax.axis_index("x") + 2 * lax.axis_index("y") + lax.axis_index("c")
    conv_shard_w = CONV_CH // N_DEV
    zero = jnp.zeros((), jnp.int32)

    def place_conv(shard):
        return lax.dynamic_update_slice(jnp.zeros((CONV_K, CONV_CH), F32), shard[0], (zero, me * conv_shard_w))

    (g_in_t,) = _all_gather_two_level([w_in[0].T.astype(BF16)])
    late_shards = [w_out[0].astype(BF16), w_up[0].T.astype(BF16), w_down[0].astype(BF16)]
    conv_full = _small_gather(jnp.pad(place_conv(conv_w), ((0, SUBLANES - CONV_K), (0, 0))))[:CONV_K]

    loss_local, grad_x, g_small, recv = _local_step(
        x, loss_target, norm_mix_w, g_in_t.reshape(IN_WIDTH, D_MODEL), conv_full, conv_b, dt_bias, a_log, d_skip,
        ssd_norm_w, f_bias, late_shards, norm_mlp_w, norm_final_w)
    loss = lax.psum(loss_local, MESH_AXES)

    grad_in = _sum_parts(recv["w_in_t"], "sum_w_in").T[None]
    grad_up = _sum_parts(recv["w_up_t"], "sum_w_up").T[None]
    big = {
        "w_in": _adamw_parts(grad_in, w_in[0], m_w_in[0], v_w_in[0], "adamw_w_in"),
        "w_out": _adamw_parts(recv["w_out"], w_out[0], m_w_out[0], v_w_out[0], "adamw_w_out"),
        "w_up": _adamw_parts(grad_up, w_up[0], m_w_up[0], v_w_up[0], "adamw_w_up"),
        "w_down": _adamw_parts(recv["w_down"], w_down[0], m_w_down[0], v_w_down[0], "adamw_w_down"),
    }

    like = dict(norm_mix_w=norm_mix_w, norm_mlp_w=norm_mlp_w, norm_final_w=norm_final_w, ssd_norm_w=ssd_norm_w,
                conv_b=conv_b, conv_w=jnp.zeros((1, CONV_K, CONV_CH), F32), dt_bias=dt_bias, a_log=a_log,
                d_skip=d_skip, f_bias=f_bias)
    w_small = dict(like, conv_w=place_conv(conv_w))
    m_small = dict(norm_mix_w=m_norm_mix_w, norm_mlp_w=m_norm_mlp_w, norm_final_w=m_norm_final_w,
                   ssd_norm_w=m_ssd_norm_w, conv_b=m_conv_b, conv_w=place_conv(m_conv_w), dt_bias=m_dt_bias,
                   a_log=m_a_log, d_skip=m_d_skip, f_bias=m_f_bias)
    v_small = dict(norm_mix_w=v_norm_mix_w, norm_mlp_w=v_norm_mlp_w, norm_final_w=v_norm_final_w,
                   ssd_norm_w=v_ssd_norm_w, conv_b=v_conv_b, conv_w=place_conv(v_conv_w), dt_bias=v_dt_bias,
                   a_log=v_a_log, d_skip=v_d_skip, f_bias=v_f_bias)
    small = _small_reduce_adamw(_pack_small(g_small), _pack_small(w_small), _pack_small(m_small), _pack_small(v_small))
    small = [_unpack_small(s, like) for s in small]
    for s in small:
        s["conv_w"] = lax.dynamic_slice(s["conv_w"], (zero, zero, me * conv_shard_w), (1, CONV_K, conv_shard_w))

    order = ["norm_mix_w", "w_in", "conv_w", "conv_b", "dt_bias", "a_log", "d_skip", "ssd_norm_w", "f_bias", "w_out",
             "norm_mlp_w", "w_up", "w_down", "norm_final_w"]
    outs = [loss, grad_x]
    for kind in range(4):
        for name in order:
            outs.append(big[name][kind][None] if name in big else small[kind][name])
    return tuple(outs)
```

```python
import jax
import jax.numpy as jnp
from jax import lax
from jax.experimental import pallas as pl
from jax.experimental.pallas import tpu as pltpu

F32, BF16 = jnp.float32, jnp.bfloat16
HI = lax.Precision.HIGHEST

D_MODEL = 1024
SSD_HEADS = 16
HEAD_DIM = 64
SSD_STATE = 128
CHUNK = 128
CONV_CH = 1536
CONV_K = 4
D_FF = 4096
MIX_WIDTH = 2048
IN_WIDTH = 5664
NORM_EPS = 1e-5
ATT_SCALE = 0.125

LANES = 128
SUBLANES = 8
HEAD_PAIRS = SSD_HEADS // 2
NEG = -1e30
VMEM_LIMIT = 52 * 1024 * 1024

ROW_XBC, ROW_DT, ROW_Q, ROW_F = 1024, 2560, 2576, 5648
F_COL = SSD_HEADS

N_DEV = 8
MESH_AXES = ("x", "y", "c")
MESH = pl.DeviceIdType.MESH

ADAM_LR, ADAM_B1, ADAM_B2, ADAM_EPS, ADAM_WD, ADAM_STEP = 0.001, 0.9, 0.999, 1e-08, 0.01, 10
ADAM_C1 = 1.0 - ADAM_B1 ** ADAM_STEP
ADAM_C2 = 1.0 - ADAM_B2 ** ADAM_STEP

SMALL_COLS = CONV_CH


def _params(sem=None):
    return pltpu.CompilerParams(dimension_semantics=sem, vmem_limit_bytes=VMEM_LIMIT)


def _sigmoid(u):
    return 1.0 / (1.0 + jnp.exp(-u))


def _softplus(u):
    return jnp.maximum(u, 0.0) + jnp.log(1.0 + jnp.exp(-jnp.abs(u)))


def _log_sigmoid(u):
    return jnp.minimum(u, 0.0) - jnp.log(1.0 + jnp.exp(-jnp.abs(u)))


def _bdot(a, b):
    return jnp.dot(a.astype(BF16), b.astype(BF16), preferred_element_type=F32)


def _bdot_nt(a, b):
    return lax.dot_general(a.astype(BF16), b.astype(BF16), (((1,), (1,)), ((), ())), preferred_element_type=F32)


def _bdot_tn(a, b):
    return lax.dot_general(a.astype(BF16), b.astype(BF16), (((0,), (0,)), ((), ())), preferred_element_type=F32)


def _split3(x):
    x1 = x.astype(BF16)
    r1 = x - x1.astype(F32)
    x2 = r1.astype(BF16)
    return x1, x2, (r1 - x2.astype(F32)).astype(BF16)


def _dot_sel(x, sel):
    s = sel.astype(BF16)
    p1, p2, p3 = (jnp.dot(p, s, preferred_element_type=F32) for p in _split3(x))
    return p1 + p2 + p3


def _sel_dot(sel, x):
    s = sel.astype(BF16)
    p1, p2, p3 = (jnp.dot(s, p, preferred_element_type=F32) for p in _split3(x))
    return p1 + p2 + p3


def _iota(shape, dim):
    return lax.broadcasted_iota(jnp.int32, shape, dim)


def _head_expand():
    return (jnp.right_shift(_iota((LANES, D_MODEL), 1), 6) == _iota((LANES, D_MODEL), 0)).astype(F32)


def _head_reduce():
    return (jnp.right_shift(_iota((D_MODEL, LANES), 0), 6) == _iota((D_MODEL, LANES), 1)).astype(F32)


def _rms_fwd(x, w, name):
    n, d = x.shape
    tm = min(1024, n)

    def body(x_ref, w_ref, o_ref):
        xv = x_ref[...]
        r = lax.rsqrt(jnp.mean(xv * xv, axis=-1, keepdims=True) + NORM_EPS)
        o_ref[...] = (xv * r * w_ref[...]).astype(BF16)

    return pl.pallas_call(
        body, name=name, grid=(n // tm,),
        in_specs=[pl.BlockSpec((tm, d), lambda i: (i, 0)), pl.BlockSpec((1, d), lambda i: (0, 0))],
        out_specs=pl.BlockSpec((tm, d), lambda i: (i, 0)),
        out_shape=jax.ShapeDtypeStruct((n, d), BF16),
        compiler_params=_params(("parallel",)),
    )(x, w)


def _mm(pairs, name, out_dtype, tm, tn, nt=False, res=None, exchange=None, rms_fwd=None, rms_bwd=None):
    m = pairs[0][0].shape[0]
    n = pairs[0][1].shape[0 if nt else 1]
    tm, tn = min(tm, m), min(tn, n)
    gm, gn = m // tm, n // tn
    npairs = len(pairs)
    kind, xs = (None, []) if exchange is None else exchange
    nx = len(xs)
    extra_in = [] if res is None else [res]
    if rms_bwd is not None:
        extra_in += list(rms_bwd)
    elif rms_fwd is not None:
        extra_in.append(rms_fwd)
    n_in = 2 * npairs + len(extra_in)
    n_out = 1 + (rms_fwd is not None or rms_bwd is not None)
    assert (rms_fwd is None and rms_bwd is None) or tn == n

    def body(*refs):
        x_refs, o_refs = refs[n_in:n_in + nx], refs[n_in + nx:n_in + nx + n_out]
        xo_refs, sems = refs[n_in + nx + n_out:n_in + 2 * nx + n_out], refs[n_in + 2 * nx + n_out:]
        extra = refs[2 * npairs:n_in]
        i, j = pl.program_id(0), pl.program_id(1)
        if nx:
            @pl.when((i == 0) & (j == 0))
            def _():
                _exchange_start(kind, x_refs, xo_refs, *sems)

        acc = None
        for p in range(npairs):
            a_v, b_v = refs[2 * p][...], refs[2 * p + 1][...]
            d = _bdot_nt(a_v, b_v) if nt else _bdot(a_v, b_v)
            acc = d if acc is None else acc + d
        if rms_bwd is not None:
            xv = extra[1][...]
            r = lax.rsqrt(jnp.mean(xv * xv, axis=-1, keepdims=True) + NORM_EPS)
            nrm = xv * r
            g = acc * extra[2][...]
            o_refs[0][...] = extra[0][...] + r * (g - nrm * jnp.mean(g * nrm, axis=-1, keepdims=True))

            @pl.when(i == 0)
            def _():
                o_refs[1][...] = jnp.zeros_like(o_refs[1])

            o_refs[1][...] += jnp.sum(acc * nrm, axis=0, keepdims=True)
        else:
            if res is not None:
                acc = extra[0][...] + acc
            o_refs[0][...] = acc.astype(o_refs[0].dtype)
            if rms_fwd is not None:
                r = lax.rsqrt(jnp.mean(acc * acc, axis=-1, keepdims=True) + NORM_EPS)
                o_refs[1][...] = (acc * r * extra[-1][...]).astype(BF16)
        if nx:
            @pl.when((i == gm - 1) & (j == gn - 1))
            def _():
                _exchange_wait(kind, x_refs, xo_refs, *sems)

    tile = pl.BlockSpec((tm, tn), lambda i, j: (i, j))
    row = pl.BlockSpec((1, tn), lambda i, j: (0, j))
    in_specs, args = [], []
    for a, b in pairs:
        k = a.shape[1]
        in_specs.append(pl.BlockSpec((tm, k), lambda i, j: (i, 0)))
        in_specs.append(pl.BlockSpec((tn, k), lambda i, j: (j, 0)) if nt else pl.BlockSpec((k, tn), lambda i, j: (0, j)))
        args += [a, b]
    in_specs += [row if e.shape[0] == 1 else tile for e in extra_in]
    out_specs, out_shape = [tile], [jax.ShapeDtypeStruct((m, n), out_dtype)]
    if rms_bwd is not None:
        out_specs.append(row)
        out_shape.append(jax.ShapeDtypeStruct((1, n), F32))
    elif rms_fwd is not None:
        out_specs.append(tile)
        out_shape.append(jax.ShapeDtypeStruct((m, n), BF16))
    hbm = pl.BlockSpec(memory_space=pl.ANY)
    sequential = nx or rms_bwd is not None
    outs = pl.pallas_call(
        body, name=name, grid=(gm, gn), in_specs=in_specs + [hbm] * nx,
        out_specs=out_specs + [hbm] * nx,
        out_shape=out_shape + _exchange_shapes(kind, xs),
        scratch_shapes=_exchange_scratch(nx),
        compiler_params=_params(("arbitrary", "arbitrary") if sequential else ("parallel", "parallel")),
    )(*args, *extra_in, *xs)
    return outs if len(outs) > 1 else outs[0]


def _mm_tn(a, b, name, tm, tn, tk):
    t, m = a.shape
    n = b.shape[1]
    tm, tn, tk = min(tm, m), min(tn, n), min(tk, t)
    nk = t // tk

    def body(a_ref, b_ref, o_ref, acc_ref):
        kk = pl.program_id(2)

        @pl.when(kk == 0)
        def _():
            acc_ref[...] = jnp.zeros_like(acc_ref)

        acc_ref[...] += _bdot_tn(a_ref[...], b_ref[...])

        @pl.when(kk == nk - 1)
        def _():
            o_ref[...] = acc_ref[...].astype(o_ref.dtype)

    return pl.pallas_call(
        body, name=name, grid=(m // tm, n // tn, nk),
        in_specs=[pl.BlockSpec((tk, tm), lambda i, j, kk: (kk, i)), pl.BlockSpec((tk, tn), lambda i, j, kk: (kk, j))],
        out_specs=pl.BlockSpec((tm, tn), lambda i, j, kk: (i, j)),
        out_shape=jax.ShapeDtypeStruct((m, n), BF16),
        scratch_shapes=[pltpu.VMEM((tm, tn), F32)],
        compiler_params=_params(("parallel", "parallel", "arbitrary")),
    )(a, b)


def _mlp_down_loss(pre, w_down, h2, target, w):
    m, k = pre.shape
    d = w_down.shape[1]
    tm = min(256, m)

    def body(p_ref, b_ref, r_ref, t_ref, w_ref, dh_ref, loss_ref, gw_ref):
        u = jnp.square(jnp.maximum(p_ref[...], 0.0))
        xv = r_ref[...] + _bdot(u, b_ref[...])
        r = lax.rsqrt(jnp.mean(xv * xv, axis=-1, keepdims=True) + NORM_EPS)
        nrm = xv * r
        wv = w_ref[...]
        err = nrm * wv - t_ref[...]
        part = 0.5 * jnp.sum(jnp.mean(err * err, axis=-1, keepdims=True), axis=0, keepdims=True)
        dy = err * (1.0 / d)
        g = dy * wv
        dh_ref[...] = r * (g - nrm * jnp.mean(g * nrm, axis=-1, keepdims=True))

        @pl.when(pl.program_id(0) == 0)
        def _():
            loss_ref[...] = jnp.zeros_like(loss_ref)
            gw_ref[...] = jnp.zeros_like(gw_ref)

        loss_ref[...] += jnp.broadcast_to(part, loss_ref.shape)
        gw_ref[...] += jnp.sum(dy * nrm, axis=0, keepdims=True)

    tok = pl.BlockSpec((tm, d), lambda i: (i, 0))
    row = pl.BlockSpec((1, d), lambda i: (0, 0))
    return pl.pallas_call(
        body, name="mlp_down_loss", grid=(m // tm,),
        in_specs=[pl.BlockSpec((tm, k), lambda i: (i, 0)), pl.BlockSpec((k, d), lambda i: (0, 0)), tok, tok, row],
        out_specs=[tok, pl.BlockSpec((1, LANES), lambda i: (0, 0)), row],
        out_shape=[jax.ShapeDtypeStruct((m, d), F32), jax.ShapeDtypeStruct((1, LANES), F32),
                   jax.ShapeDtypeStruct((1, d), F32)],
        compiler_params=_params(("arbitrary",)),
    )(pre, w_down, h2, target, w)


def _mlp_down_bwd(dh3, w_down, pre):
    m, k = dh3.shape
    n = w_down.shape[0]
    tm, tn = min(256, m), n

    def body(a_ref, b_ref, p_ref, dpre_ref, u_ref):
        r = jnp.maximum(p_ref[...], 0.0)
        du = _bdot_nt(a_ref[...], b_ref[...])
        dpre_ref[...] = (du * (2.0 * r)).astype(BF16)
        u_ref[...] = (r * r).astype(BF16)

    blk = pl.BlockSpec((tm, tn), lambda i, j: (i, j))
    return pl.pallas_call(
        body, name="mlp_down_bwd", grid=(m // tm, n // tn),
        in_specs=[pl.BlockSpec((tm, k), lambda i, j: (i, 0)), pl.BlockSpec((tn, k), lambda i, j: (j, 0)), blk],
        out_specs=[blk, blk],
        out_shape=[jax.ShapeDtypeStruct((m, n), BF16), jax.ShapeDtypeStruct((m, n), BF16)],
        compiler_params=_params(("parallel", "parallel")),
    )(dh3, w_down, pre)


CONV_TC = 512


def _conv_fwd(xbc, cw, cb, bsz, seq):
    tt = min(512, seq)
    nt, hb = seq // tt, tt // SUBLANES
    x3 = xbc.reshape(bsz, seq, CONV_CH)

    def body(xm_ref, xh_ref, w_ref, b_ref, o_ref):
        i = pl.program_id(2)
        x = xm_ref[0]
        halo = jnp.where(i > 0, xh_ref[0], 0.0)
        xx = jnp.concatenate([halo, x], axis=0)
        acc = x * w_ref[3:4, :] + b_ref[...]
        for s in range(1, CONV_K):
            acc = acc + pltpu.roll(xx, s, 0)[SUBLANES:, :] * w_ref[3 - s:4 - s, :]
        o_ref[0] = acc * _sigmoid(acc)

    out = pl.pallas_call(
        body, name="conv_fwd", grid=(CONV_CH // CONV_TC, bsz, nt),
        in_specs=[pl.BlockSpec((1, tt, CONV_TC), lambda c, b, i: (b, i, c)),
                  pl.BlockSpec((1, SUBLANES, CONV_TC), lambda c, b, i: (b, jnp.maximum(i * hb - 1, 0), c)),
                  pl.BlockSpec((CONV_K, CONV_TC), lambda c, b, i: (0, c)),
                  pl.BlockSpec((1, CONV_TC), lambda c, b, i: (0, c))],
        out_specs=pl.BlockSpec((1, tt, CONV_TC), lambda c, b, i: (b, i, c)),
        out_shape=jax.ShapeDtypeStruct((bsz, seq, CONV_CH), F32),
        compiler_params=_params(("parallel", "parallel", "parallel")),
    )(x3, x3, cw, cb)
    return out.reshape(bsz * seq, CONV_CH)


def _conv_bwd(da, xbc, cw, cb, bsz, seq):
    tt = min(512, seq)
    nt, hb = seq // tt, tt // SUBLANES
    x3 = xbc.reshape(bsz, seq, CONV_CH)
    da3 = da.reshape(bsz, seq, CONV_CH)
    n2 = tt + SUBLANES

    def body(dam_ref, daa_ref, xm_ref, xb_ref, xa_ref, w_ref, b_ref, du_ref, gw_ref, gb_ref):
        bi, i = pl.program_id(1), pl.program_id(2)
        before = jnp.where(i > 0, xb_ref[0], 0.0)
        after = jnp.where(i < nt - 1, xa_ref[0], 0.0)
        xx = jnp.concatenate([before, xm_ref[0], after], axis=0)
        rolled = [xx] + [pltpu.roll(xx, s, 0) for s in range(1, CONV_K)]
        pre = b_ref[...]
        for s in range(CONV_K):
            pre = pre + rolled[s][SUBLANES:, :] * w_ref[3 - s:4 - s, :]
        da_ext = jnp.concatenate([dam_ref[0], jnp.where(i < nt - 1, daa_ref[0], 0.0)], axis=0)
        sg = _sigmoid(pre)
        dpre = da_ext * sg * (1.0 + pre * (1.0 - sg))
        du = dpre[:tt, :] * w_ref[3:4, :]
        for s in range(1, CONV_K):
            du = du + pltpu.roll(dpre, n2 - s, 0)[:tt, :] * w_ref[3 - s:4 - s, :]
        du_ref[0] = du.astype(BF16)
        dpm = dpre[:tt, :]
        gws = [jnp.sum(dpm * rolled[3 - kk][SUBLANES:SUBLANES + tt, :], axis=0, keepdims=True) for kk in range(CONV_K)]

        @pl.when((bi == 0) & (i == 0))
        def _():
            gw_ref[...] = jnp.zeros_like(gw_ref)
            gb_ref[...] = jnp.zeros_like(gb_ref)

        gw_ref[...] += jnp.concatenate(gws, axis=0)
        gb_ref[...] += jnp.sum(dpm, axis=0, keepdims=True)

    main = pl.BlockSpec((1, tt, CONV_TC), lambda c, b, i: (b, i, c))
    prev = pl.BlockSpec((1, SUBLANES, CONV_TC), lambda c, b, i: (b, jnp.maximum(i * hb - 1, 0), c))
    nxt = pl.BlockSpec((1, SUBLANES, CONV_TC), lambda c, b, i: (b, jnp.minimum((i + 1) * hb, seq // SUBLANES - 1), c))
    du, gw, gb = pl.pallas_call(
        body, name="conv_bwd", grid=(CONV_CH // CONV_TC, bsz, nt),
        in_specs=[main, nxt, main, prev, nxt,
                  pl.BlockSpec((CONV_K, CONV_TC), lambda c, b, i: (0, c)),
                  pl.BlockSpec((1, CONV_TC), lambda c, b, i: (0, c))],
        out_specs=[main, pl.BlockSpec((CONV_K, CONV_TC), lambda c, b, i: (0, c)),
                   pl.BlockSpec((1, CONV_TC), lambda c, b, i: (0, c))],
        out_shape=[jax.ShapeDtypeStruct((bsz, seq, CONV_CH), BF16), jax.ShapeDtypeStruct((CONV_K, CONV_CH), F32),
                   jax.ShapeDtypeStruct((1, CONV_CH), F32)],
        compiler_params=_params(("parallel", "arbitrary", "arbitrary")),
    )(da3, da3, x3, x3, x3, cw, cb)
    return du.reshape(bsz * seq, CONV_CH), gw, gb


def _ssd_prologue(dtf_ref, dtft_ref, bias_ref, biast_ref, arow_ref, acol_ref, dtfull_scr, acfull_scr, acr_scr):
    tri = (_iota((CHUNK, CHUNK), 1) <= _iota((CHUNK, CHUNK), 0)).astype(F32)
    triu = (_iota((CHUNK, CHUNK), 0) <= _iota((CHUNK, CHUNK), 1)).astype(F32)
    dtc = _softplus(dtf_ref[0] + bias_ref[...])
    a = dtc * arow_ref[...]
    acum = _sel_dot(tri, a)
    expand = _head_expand()
    dtfull_scr[...] = _dot_sel(dtc, expand)
    acfull_scr[...] = _dot_sel(acum, expand)
    dtr = _softplus(dtft_ref[0] + biast_ref[...])
    acr_scr[...] = _dot_sel(dtr * acol_ref[...], triu)
    return dtc, acum


def _decay_matrix(acum, acr_scr, h):
    lane = _iota((CHUNK, LANES), 1)
    ac_col = jnp.sum(jnp.where(lane == h, acum, 0.0), axis=1, keepdims=True)
    ac_row = acr_scr[h:h + 1, :]
    causal = _iota((CHUNK, CHUNK), 1) <= _iota((CHUNK, CHUNK), 0)
    return jnp.exp(jnp.where(causal, ac_col - ac_row, NEG))


def _ssd_fwd(xbc, dtf, dtft, bias, biast, arow, acol, dfull, bsz, seq):
    nc = seq // CHUNK
    x3 = xbc.reshape(bsz, seq, CONV_CH)

    def body(xbc_ref, dtf_ref, dtft_ref, bias_ref, biast_ref, arow_ref, acol_ref, dfull_ref,
             y_ref, hall_ref, h_scr, dtfull_scr, acfull_scr, acr_scr):
        @pl.when(pl.program_id(1) == 0)
        def _():
            h_scr[...] = jnp.zeros_like(h_scr)

        _, acum = _ssd_prologue(dtf_ref, dtft_ref, bias_ref, biast_ref, arow_ref, acol_ref,
                                dtfull_scr, acfull_scr, acr_scr)
        lane = _iota((CHUNK, LANES), 1)
        for g in range(2):
            bg = xbc_ref[0, :, D_MODEL + LANES * g:D_MODEL + LANES * (g + 1)]
            cg = xbc_ref[0, :, D_MODEL + 2 * LANES + LANES * g:D_MODEL + 2 * LANES + LANES * (g + 1)]
            cg_bf = cg.astype(BF16)
            gmat = _bdot_nt(cg_bf, bg)
            bgt_bf = bg.T.astype(BF16)
            for j in range(4):
                p = 4 * g + j
                sl = slice(LANES * p, LANES * (p + 1))
                xs = xbc_ref[0, :, sl]
                ac = acfull_scr[:, sl]
                acl = acfull_scr[CHUNK - 1:CHUNK, sl]
                xdt = xs * dtfull_scr[:, sl]
                xdt_bf = xdt.astype(BF16)
                hp = h_scr[p]
                hall_ref[0, 0, p] = hp
                yo = _bdot(cg_bf, hp) * jnp.exp(ac)
                yd = []
                for hh in range(2):
                    mmat = gmat * _decay_matrix(acum, acr_scr, 2 * p + hh)
                    yd.append(_bdot(mmat, xdt_bf))
                y_ref[0, :, sl] = jnp.where(lane < HEAD_DIM, yd[0], yd[1]) + yo + dfull_ref[:, sl] * xs
                h_scr[p] = hp * jnp.exp(acl) + _bdot(bgt_bf, xdt * jnp.exp(acl - ac))

    row = lambda w: pl.BlockSpec((1, w), lambda b, c: (0, 0))
    y, hall = pl.pallas_call(
        body, name="ssd_fwd", grid=(bsz, nc),
        in_specs=[pl.BlockSpec((1, CHUNK, CONV_CH), lambda b, c: (b, c, 0)),
                  pl.BlockSpec((1, CHUNK, LANES), lambda b, c: (b, c, 0)),
                  pl.BlockSpec((1, LANES, CHUNK), lambda b, c: (b, 0, c)),
                  row(LANES), pl.BlockSpec((LANES, 1), lambda b, c: (0, 0)),
                  row(LANES), pl.BlockSpec((LANES, 1), lambda b, c: (0, 0)), row(D_MODEL)],
        out_specs=[pl.BlockSpec((1, CHUNK, D_MODEL), lambda b, c: (b, c, 0)),
                   pl.BlockSpec((1, 1, HEAD_PAIRS, SSD_STATE, LANES), lambda b, c: (b, c, 0, 0, 0))],
        out_shape=[jax.ShapeDtypeStruct((bsz, seq, D_MODEL), F32),
                   jax.ShapeDtypeStruct((bsz, nc, HEAD_PAIRS, SSD_STATE, LANES), F32)],
        scratch_shapes=[pltpu.VMEM((HEAD_PAIRS, SSD_STATE, LANES), F32), pltpu.VMEM((CHUNK, D_MODEL), F32),
                        pltpu.VMEM((CHUNK, D_MODEL), F32), pltpu.VMEM((LANES, CHUNK), F32)],
        compiler_params=_params(("parallel", "arbitrary")),
    )(x3, dtf.reshape(bsz, seq, LANES), dtft, bias, biast, arow, acol, dfull)
    return y.reshape(bsz * seq, D_MODEL), hall


def _ssd_bwd(dy, xbc, dtf, dtft, bias, biast, arow, acol, dfull, hall, bsz, seq):
    nc = seq // CHUNK
    x3 = xbc.reshape(bsz, seq, CONV_CH)
    dy3 = dy.reshape(bsz, seq, D_MODEL)

    def body(dy_ref, xbc_ref, dtf_ref, dtft_ref, bias_ref, biast_ref, arow_ref, acol_ref, dfull_ref, hall_ref,
             dx_ref, ddt_ref, acc_ref, dh_scr, dtfull_scr, acfull_scr, acr_scr, csum_scr, dacf_scr, ddtf_scr, ddl_scr):
        first = (pl.program_id(0) == 0) & (pl.program_id(1) == 0)

        @pl.when(first)
        def _():
            acc_ref[...] = jnp.zeros_like(acc_ref)

        @pl.when(pl.program_id(1) == 0)
        def _():
            dh_scr[...] = jnp.zeros_like(dh_scr)

        dtc, acum = _ssd_prologue(dtf_ref, dtft_ref, bias_ref, biast_ref, arow_ref, acol_ref,
                                  dtfull_scr, acfull_scr, acr_scr)
        csum_scr[...] = jnp.zeros_like(csum_scr)
        lane = _iota((CHUNK, LANES), 1)
        last_row = _iota((CHUNK, LANES), 0) == CHUNK - 1
        rs16 = jnp.zeros((CHUNK, LANES), F32)
        for g in range(2):
            bsl = slice(D_MODEL + LANES * g, D_MODEL + LANES * (g + 1))
            csl = slice(D_MODEL + 2 * LANES + LANES * g, D_MODEL + 2 * LANES + LANES * (g + 1))
            bg_bf = xbc_ref[0, :, bsl].astype(BF16)
            cg = xbc_ref[0, :, csl]
            cg_bf = cg.astype(BF16)
            cgt_bf = cg.T.astype(BF16)
            gmat = _bdot_nt(cg_bf, bg_bf)
            dg = jnp.zeros((CHUNK, CHUNK), F32)
            dbg = jnp.zeros((CHUNK, SSD_STATE), F32)
            dcg = jnp.zeros((CHUNK, SSD_STATE), F32)
            for j in range(4):
                p = 4 * g + j
                sl = slice(LANES * p, LANES * (p + 1))
                xs = xbc_ref[0, :, sl]
                dt = dtfull_scr[:, sl]
                ac = acfull_scr[:, sl]
                acl = acfull_scr[CHUNK - 1:CHUNK, sl]
                dyp = dy_ref[0, :, sl]
                xdt = xs * dt
                xdt_bf = xdt.astype(BF16)
                e = jnp.exp(ac)
                dsd = jnp.exp(acl - ac)
                cd = jnp.exp(acl)
                xds_bf = (xdt * dsd).astype(BF16)
                hp = hall_ref[0, 0, p]
                hp_bf = hp.astype(BF16)
                dhn = dh_scr[p]
                dhn_bf = dhn.astype(BF16)
                yo = _bdot(cg_bf, hp_bf) * e
                dw_bf = (dyp * e).astype(BF16)
                dcg = dcg + _bdot_nt(dw_bf, hp_bf)
                dhin = _bdot(cgt_bf, dw_bf)
                dac = dyp * yo
                dacl = jnp.sum(dhn * hp, axis=0, keepdims=True) * cd
                dxds = _bdot(bg_bf, dhn_bf)
                dbg = dbg + _bdot_nt(xds_bf, dhn_bf)
                dxdt = dxds * dsd
                tdec = dxds * xdt * dsd
                dacl = dacl + jnp.sum(tdec, axis=0, keepdims=True)
                dac = dac - tdec
                dh_scr[p] = dhn * cd + dhin
                for hh in range(2):
                    h = 2 * p + hh
                    lm = (lane < HEAD_DIM) if hh == 0 else (lane >= HEAD_DIM)
                    dec = _decay_matrix(acum, acr_scr, h)
                    mmat = gmat * dec
                    dyh_bf = jnp.where(lm, dyp, 0.0).astype(BF16)
                    dm = _bdot_nt(dyh_bf, xdt_bf)
                    dxdt = dxdt + _bdot(mmat.T, dyh_bf)
                    dg = dg + dm * dec
                    q = dm * mmat
                    rs16 = rs16 + jnp.where(lane == h, jnp.sum(q, axis=1, keepdims=True), 0.0)
                    csum_scr[h:h + 1, :] = jnp.sum(q, axis=0, keepdims=True)
                dx_ref[0, :, sl] = dfull_ref[:, sl] * dyp + dxdt * dt
                dacf_scr[:, sl] = dac + jnp.where(last_row, dacl, 0.0)
                ddtf_scr[:, sl] = dxdt * xs
                ddl_scr[:, sl] = jnp.broadcast_to(jnp.sum(dyp * xs, axis=0, keepdims=True), (SUBLANES, LANES))
            dg_bf = dg.astype(BF16)
            dx_ref[0, :, bsl] = dbg + _bdot(dg.T, cg_bf)
            dx_ref[0, :, csl] = dcg + _bdot(dg_bf, bg_bf)
        reduce = _head_reduce()
        triu = (_iota((CHUNK, CHUNK), 0) <= _iota((CHUNK, CHUNK), 1)).astype(F32)
        dac16 = _dot_sel(dacf_scr[...], reduce) + rs16 - csum_scr[...].T
        da16 = _sel_dot(triu, dac16)
        ddt16 = da16 * arow_ref[...] + _dot_sel(ddtf_scr[...], reduce)
        ddtraw = ddt16 * _sigmoid(dtf_ref[0] + bias_ref[...])
        ddt_ref[0] = ddtraw
        acc_ref[0:8, :] += jnp.broadcast_to(jnp.sum(da16 * dtc * arow_ref[...], axis=0, keepdims=True), (SUBLANES, LANES))
        acc_ref[8:16, :] += _dot_sel(ddl_scr[...], reduce)
        acc_ref[16:24, :] += jnp.broadcast_to(jnp.sum(ddtraw, axis=0, keepdims=True), (SUBLANES, LANES))

    rev = lambda b, c: (b, nc - 1 - c, 0)
    row = lambda w: pl.BlockSpec((1, w), lambda b, c: (0, 0))
    dx, ddt, acc = pl.pallas_call(
        body, name="ssd_bwd", grid=(bsz, nc),
        in_specs=[pl.BlockSpec((1, CHUNK, D_MODEL), rev), pl.BlockSpec((1, CHUNK, CONV_CH), rev),
                  pl.BlockSpec((1, CHUNK, LANES), rev),
                  pl.BlockSpec((1, LANES, CHUNK), lambda b, c: (b, 0, nc - 1 - c)),
                  row(LANES), pl.BlockSpec((LANES, 1), lambda b, c: (0, 0)),
                  row(LANES), pl.BlockSpec((LANES, 1), lambda b, c: (0, 0)), row(D_MODEL),
                  pl.BlockSpec((1, 1, HEAD_PAIRS, SSD_STATE, LANES), lambda b, c: (b, nc - 1 - c, 0, 0, 0))],
        out_specs=[pl.BlockSpec((1, CHUNK, CONV_CH), rev), pl.BlockSpec((1, CHUNK, LANES), rev),
                   pl.BlockSpec((3 * SUBLANES, LANES), lambda b, c: (0, 0))],
        out_shape=[jax.ShapeDtypeStruct((bsz, seq, CONV_CH), F32), jax.ShapeDtypeStruct((bsz, seq, LANES), F32),
                   jax.ShapeDtypeStruct((3 * SUBLANES, LANES), F32)],
        scratch_shapes=[pltpu.VMEM((HEAD_PAIRS, SSD_STATE, LANES), F32), pltpu.VMEM((CHUNK, D_MODEL), F32),
                        pltpu.VMEM((CHUNK, D_MODEL), F32), pltpu.VMEM((LANES, CHUNK), F32),
                        pltpu.VMEM((LANES, CHUNK), F32), pltpu.VMEM((CHUNK, D_MODEL), F32),
                        pltpu.VMEM((CHUNK, D_MODEL), F32), pltpu.VMEM((SUBLANES, D_MODEL), F32)],
        compiler_params=_params(("arbitrary", "arbitrary")),
    )(dy3, x3, dtf.reshape(bsz, seq, LANES), dtft, bias, biast, arow, acol, dfull, hall)
    return dx.reshape(bsz * seq, CONV_CH), ddt.reshape(bsz * seq, LANES), acc


GROUP_W = D_MODEL // 2


def _gnorm_fwd(y, z, o_att, w):
    n = y.shape[0]
    tm = min(1024, n)

    def body(y_ref, z_ref, o_ref, w_ref, out_ref):
        zv = z_ref[...]
        g = y_ref[...] * (zv * _sigmoid(zv))
        for gi in range(2):
            sl = slice(GROUP_W * gi, GROUP_W * (gi + 1))
            gs = g[:, sl]
            r = lax.rsqrt(jnp.mean(gs * gs, axis=-1, keepdims=True) + NORM_EPS)
            out_ref[:, sl] = (gs * r * w_ref[:, sl]).astype(BF16)
        out_ref[:, D_MODEL:] = o_ref[...].astype(BF16)

    tok = pl.BlockSpec((tm, D_MODEL), lambda i: (i, 0))
    return pl.pallas_call(
        body, name="gnorm_fwd", grid=(n // tm,),
        in_specs=[tok, tok, tok, pl.BlockSpec((1, D_MODEL), lambda i: (0, 0))],
        out_specs=pl.BlockSpec((tm, MIX_WIDTH), lambda i: (i, 0)),
        out_shape=jax.ShapeDtypeStruct((n, MIX_WIDTH), BF16),
        compiler_params=_params(("parallel",)),
    )(y, z, o_att, w)


def _gnorm_bwd(dycat, y, z, w):
    n = y.shape[0]
    tm = min(512, n)

    def body(dn_ref, y_ref, z_ref, w_ref, dy_ref, dz_ref, gw_ref):
        zv, yv = z_ref[...], y_ref[...]
        sz = _sigmoid(zv)
        sil = zv * sz
        g = yv * sil

        @pl.when(pl.program_id(0) == 0)
        def _():
            gw_ref[...] = jnp.zeros_like(gw_ref)

        for gi in range(2):
            sl = slice(GROUP_W * gi, GROUP_W * (gi + 1))
            gs = g[:, sl]
            r = lax.rsqrt(jnp.mean(gs * gs, axis=-1, keepdims=True) + NORM_EPS)
            nrm = gs * r
            dyn = dn_ref[:, sl]
            dn = dyn * w_ref[:, sl]
            dgs = r * (dn - nrm * jnp.mean(dn * nrm, axis=-1, keepdims=True))
            dy_ref[:, sl] = dgs * sil[:, sl]
            dz_ref[:, sl] = (dgs * yv[:, sl] * (sz[:, sl] * (1.0 + zv[:, sl] * (1.0 - sz[:, sl])))).astype(BF16)
            gw_ref[:, sl] += jnp.sum(dyn * nrm, axis=0, keepdims=True)

    tok = pl.BlockSpec((tm, D_MODEL), lambda i: (i, 0))
    row = pl.BlockSpec((1, D_MODEL), lambda i: (0, 0))
    return pl.pallas_call(
        body, name="gnorm_bwd", grid=(n // tm,),
        in_specs=[tok, tok, tok, row], out_specs=[tok, tok, row],
        out_shape=[jax.ShapeDtypeStruct((n, D_MODEL), F32), jax.ShapeDtypeStruct((n, D_MODEL), BF16),
                   jax.ShapeDtypeStruct((1, D_MODEL), F32)],
        compiler_params=_params(("arbitrary",)),
    )(dycat, y, z, w)


def _fox_prep(dtf, fb, bsz, seq):
    def body(x_ref, b_ref, c_ref):
        tri = (_iota((CHUNK, CHUNK), 1) <= _iota((CHUNK, CHUNK), 0)).astype(F32)
        carry = jnp.zeros((1, LANES), F32)
        for j in range(seq // CHUNK):
            sl = slice(CHUNK * j, CHUNK * (j + 1))
            lf = _log_sigmoid(x_ref[sl, :] + b_ref[...])
            c_ref[sl, :] = _sel_dot(tri, lf) + carry
            carry = carry + jnp.sum(lf, axis=0, keepdims=True)

    blk = pl.BlockSpec((seq, LANES), lambda b: (b, 0))
    return pl.pallas_call(
        body, name="fox_prep", grid=(bsz,),
        in_specs=[blk, pl.BlockSpec((1, LANES), lambda b: (0, 0))], out_specs=blk,
        out_shape=jax.ShapeDtypeStruct((bsz * seq, LANES), F32),
        compiler_params=_params(("parallel",)),
    )(dtf, fb)


def _fox_prep_bwd(dck, dcq, ddt, dtf, fb, bsz, seq):
    nj = seq // CHUNK

    def body(dck_ref, dcq_ref, ddt_ref, x_ref, b_ref, out_ref, gb_ref):
        triu = (_iota((CHUNK, CHUNK), 0) <= _iota((CHUNK, CHUNK), 1)).astype(F32)
        is_f = (_iota((CHUNK, LANES), 1) >= F_COL) & (_iota((CHUNK, LANES), 1) < 2 * F_COL)
        carry = jnp.zeros((1, LANES), F32)
        total = jnp.zeros((1, LANES), F32)
        for j in range(nj - 1, -1, -1):
            sl = slice(CHUNK * j, CHUNK * (j + 1))
            dcv = dck_ref[sl, :] + dcq_ref[sl, :]
            dlf = _sel_dot(triu, dcv) + carry
            carry = carry + jnp.sum(dcv, axis=0, keepdims=True)
            df = jnp.where(is_f, dlf * _sigmoid(-(x_ref[sl, :] + b_ref[...])), 0.0)
            out_ref[sl, :] = (df + ddt_ref[sl, :]).astype(BF16)
            total = total + jnp.sum(df, axis=0, keepdims=True)

        @pl.when(pl.program_id(0) == 0)
        def _():
            gb_ref[...] = jnp.zeros_like(gb_ref)

        gb_ref[...] += jnp.broadcast_to(total, (SUBLANES, LANES))

    blk = pl.BlockSpec((seq, LANES), lambda b: (b, 0))
    return pl.pallas_call(
        body, name="fox_prep_bwd", grid=(bsz,),
        in_specs=[blk, blk, blk, blk, pl.BlockSpec((1, LANES), lambda b: (0, 0))],
        out_specs=[blk, pl.BlockSpec((SUBLANES, LANES), lambda b: (0, 0))],
        out_shape=[jax.ShapeDtypeStruct((bsz * seq, LANES), BF16), jax.ShapeDtypeStruct((SUBLANES, LANES), F32)],
        compiler_params=_params(("arbitrary",)),
    )(dck, dcq, ddt, dtf, fb)


AUX_C = 3
AUX_ONE = 3


ATT_BLOCK_FWD = 512
ATT_BLOCK_BWD = 256


def _att_operands(q_ref, k_ref, c_ref, seq, hh, pair):
    lane = _iota((seq, LANES), 1)
    lm = (lane < HEAD_DIM) if hh == 0 else (lane >= HEAD_DIM)
    base = HEAD_DIM * (1 - hh)
    negc = -jnp.sum(jnp.where(lane == F_COL + 2 * pair + hh, c_ref[...], 0.0), axis=1, keepdims=True)
    c1 = negc.astype(BF16)
    r1 = negc - c1.astype(F32)
    c2 = r1.astype(BF16)
    c3 = (r1 - c2.astype(F32)).astype(BF16)
    zero = jnp.zeros((seq, LANES), BF16)
    one = jnp.ones((seq, LANES), BF16)
    ka = jnp.where(lm, k_ref[...], jnp.where(lane == base, c1, jnp.where(lane == base + 1, c2, jnp.where(
        lane == base + 2, c3, jnp.where(lane == base + AUX_ONE, one, zero)))))
    qa = jnp.where(lm, q_ref[...] * ATT_SCALE, jnp.where((lane >= base) & (lane < base + AUX_C), one, zero))
    return lm, base, qa, ka


def _att_fwd(qkv, ccol, bsz, seq, gather):
    blk = min(ATT_BLOCK_FWD, seq)
    nb = seq // blk
    nx = len(gather)

    def body(*refs):
        q_ref, k_ref, v_ref, c_ref = refs[:4]
        x_refs = refs[4:4 + nx]
        o_ref, lse_ref = refs[4 + nx:6 + nx]
        xo_refs = refs[6 + nx:6 + 2 * nx]
        qa_scr, ka_scr, va_scr = refs[6 + 2 * nx:9 + 2 * nx]
        sems = refs[9 + 2 * nx:]
        pair = pl.program_id(1)

        @pl.when((pl.program_id(0) == 0) & (pair == 0))
        def _():
            _exchange_start("gather", x_refs, xo_refs, *sems)

        @pl.when(pair == 0)
        def _():
            lse_ref[...] = jnp.zeros_like(lse_ref)

        lane_b = _iota((blk, LANES), 1)
        causal = _iota((blk, blk), 0) >= _iota((blk, blk), 1)
        for hh in range(2):
            lm, _, qa, ka = _att_operands(q_ref, k_ref, c_ref, seq, hh, pair)
            qa_scr[hh] = qa
            ka_scr[hh] = ka
            va_scr[hh] = jnp.where(lm, v_ref[...], jnp.zeros((seq, LANES), BF16))

        def q_step(i, carry0):
            rows = pl.ds(pl.multiple_of(i * blk, blk), blk)

            def scores(j):
                krows = pl.ds(pl.multiple_of(j * blk, blk), blk)
                return tuple(_bdot_nt(qa_scr[hh, rows, :], ka_scr[hh, krows, :]) for hh in range(2))

            def update(state, s_pair, j, masked):
                krows = pl.ds(pl.multiple_of(j * blk, blk), blk)
                alphas, ls, pvs, ms = [], [], [], []
                for hh in range(2):
                    m, l, _ = state[hh]
                    s = jnp.where(causal, s_pair[hh], NEG) if masked else s_pair[hh]
                    mn = jnp.maximum(m, jnp.max(s, axis=1, keepdims=True))
                    alpha = jnp.exp(m - mn)
                    pr = jnp.exp(s - mn)
                    ms.append(mn)
                    alphas.append(alpha)
                    ls.append(alpha * l + jnp.sum(pr, axis=1, keepdims=True))
                    pvs.append(_bdot(pr, va_scr[hh, krows, :]))
                return tuple((ms[hh], ls[hh], alphas[hh] * state[hh][2] + pvs[hh]) for hh in range(2))

            def step(j, carry):
                state, s_cur = carry
                s_next = scores(j + 1)
                return update(state, s_cur, j, False), s_next

            one = (jnp.full((blk, 1), NEG, F32), jnp.zeros((blk, 1), F32), jnp.zeros((blk, LANES), F32))
            state, s_last = lax.fori_loop(0, i, step, ((one, one), scores(0)))
            (m0, l0, acc0), (m1, l1, acc1) = update(state, s_last, i, True)
            o_ref[rows, :] = acc0 * (1.0 / l0) + acc1 * (1.0 / l1)
            head0 = F_COL + 2 * pair
            lse_ref[rows, :] = jnp.where(lane_b == head0, m0 + jnp.log(l0),
                                         jnp.where(lane_b == head0 + 1, m1 + jnp.log(l1), lse_ref[rows, :]))
            return carry0

        lax.fori_loop(0, nb, q_step, 0)

        @pl.when((pl.program_id(0) == bsz - 1) & (pair == HEAD_PAIRS - 1))
        def _():
            _exchange_wait("gather", x_refs, xo_refs, *sems)

    col = lambda off: pl.BlockSpec((seq, LANES), lambda b, p: (b, off + p))
    head2 = pltpu.VMEM((2, seq, LANES), BF16)
    hbm = pl.BlockSpec(memory_space=pl.ANY)
    return pl.pallas_call(
        body, name="att_fwd", grid=(bsz, HEAD_PAIRS),
        in_specs=[col(0), col(HEAD_PAIRS), col(2 * HEAD_PAIRS), pl.BlockSpec((seq, LANES), lambda b, p: (b, 0))] + [hbm] * nx,
        out_specs=[pl.BlockSpec((seq, LANES), lambda b, p: (b, p)),
                   pl.BlockSpec((seq, LANES), lambda b, p: (b, 0))] + [hbm] * nx,
        out_shape=[jax.ShapeDtypeStruct((bsz * seq, D_MODEL), F32), jax.ShapeDtypeStruct((bsz * seq, LANES), F32)]
        + _exchange_shapes("gather", gather),
        scratch_shapes=[head2, head2, head2] + _exchange_scratch(nx),
        compiler_params=_params(("arbitrary", "arbitrary")),
    )(qkv, qkv, qkv, ccol, *gather)


def _att_bwd(qkv, dycat, o, lse, ccol, bsz, seq, scatter):
    blk = min(ATT_BLOCK_BWD, seq)
    nb = seq // blk
    nx = len(scatter)

    def body(*refs):
        q_ref, k_ref, v_ref, do_ref, o_ref, lse_ref, c_ref = refs[:7]
        x_refs = refs[7:7 + nx]
        dq_ref, dk_ref, dv_ref, dck_ref, dcq_ref = refs[7 + nx:12 + nx]
        xo_refs = refs[12 + nx:12 + 2 * nx]
        qa_scr, ka_scr, va_scr, doa_scr, kat_scr, d_scr, dqt_scr, dk_scr, dv_scr = refs[12 + 2 * nx:21 + 2 * nx]
        sems = refs[21 + 2 * nx:]
        pair = pl.program_id(1)

        @pl.when((pl.program_id(0) == 0) & (pair == 0))
        def _():
            _exchange_start("scatter", x_refs, xo_refs, *sems)

        causal_t = _iota((blk, blk), 1) >= _iota((blk, blk), 0)
        lane_b = _iota((blk, LANES), 1)
        row_t = _iota((LANES, seq), 0)
        masks, bases = [], []
        ones8 = jnp.ones((SUBLANES, LANES), F32)
        for hh in range(2):
            lm, base, qa, ka = _att_operands(q_ref, k_ref, c_ref, seq, hh, pair)
            masks.append(lm)
            bases.append(base)
            qa_scr[hh] = qa
            ka_scr[hh] = ka
            va_scr[hh] = jnp.where(lm, v_ref[...], jnp.zeros((seq, LANES), BF16))
            doa = jnp.where(lm, do_ref[...], 0.0).astype(BF16)
            doa_scr[hh] = doa
            for t0 in range(0, seq, blk):
                kat_scr[hh, :, t0:t0 + blk] = ka[t0:t0 + blk, :].astype(F32).T.astype(BF16)
            d_scr[hh] = lax.dot_general(ones8, doa.astype(F32) * o_ref[...], (((1,), (1,)), ((), ())),
                                        precision=HI, preferred_element_type=F32)
        dqt_scr[...] = jnp.zeros_like(dqt_scr)

        @pl.when(pair == 0)
        def _():
            dck_ref[...] = jnp.zeros_like(dck_ref)

        def kv_step(j, carry0):
            krows = pl.ds(pl.multiple_of(j * blk, blk), blk)
            dk_scr[...] = jnp.zeros_like(dk_scr)
            dv_scr[...] = jnp.zeros_like(dv_scr)

            def scores(i):
                rows = pl.ds(pl.multiple_of(i * blk, blk), blk)
                return tuple((_bdot_nt(ka_scr[hh, krows, :], qa_scr[hh, rows, :]),
                              _bdot_nt(va_scr[hh, krows, :], doa_scr[hh, rows, :])) for hh in range(2))

            def update(i, sd, masked):
                rows = pl.ds(pl.multiple_of(i * blk, blk), blk)
                for hh in range(2):
                    st, dpt = sd[hh]
                    if masked:
                        st = jnp.where(causal_t, st, NEG)
                    pt = jnp.exp(st - lse_ref[0, 0, hh:hh + 1, rows])
                    dst = (pt * (dpt - d_scr[hh, 0:1, rows])).astype(BF16)
                    dv_scr[hh] += _bdot(pt, doa_scr[hh, rows, :])
                    dk_scr[hh] += _bdot(dst, qa_scr[hh, rows, :])
                    dqt_scr[hh, :, rows] += _bdot(kat_scr[hh, :, krows], dst)

            def q_step(i, sd_cur):
                sd_next = scores(jnp.minimum(i + 1, nb - 1))
                update(i, sd_cur, False)
                return sd_next

            sd_diag = scores(j)
            sd_first = scores(jnp.minimum(j + 1, nb - 1))
            update(j, sd_diag, True)
            lax.fori_loop(j + 1, nb, q_step, sd_first)
            lmb0 = lane_b < HEAD_DIM
            dk_ref[krows, :] = jnp.where(lmb0, dk_scr[0], dk_scr[1]).astype(BF16)
            dv_ref[krows, :] = (dv_scr[0] + dv_scr[1]).astype(BF16)
            sums = [jnp.sum(jnp.where(lane_b == bases[hh], dk_scr[hh], 0.0), axis=1, keepdims=True) for hh in range(2)]
            head0 = F_COL + 2 * pair
            dck_ref[krows, :] += jnp.where(lane_b == head0, -sums[0], jnp.where(lane_b == head0 + 1, -sums[1], 0.0))
            return carry0

        lax.fori_loop(0, nb, kv_step, 0)
        for t0 in range(0, seq, blk):
            dq0, dq1 = dqt_scr[0, :, t0:t0 + blk].T, dqt_scr[1, :, t0:t0 + blk].T
            dq_ref[t0:t0 + blk, :] = (jnp.where(lane_b < HEAD_DIM, dq0, dq1) * ATT_SCALE).astype(BF16)
        dcq_ref[...] = jnp.zeros_like(dcq_ref)
        for hh in range(2):
            dcq_ref[0, 0, hh:hh + 1, :] = jnp.sum(jnp.where(row_t == bases[hh] + AUX_ONE, dqt_scr[hh], 0.0),
                                                   axis=0, keepdims=True)

        @pl.when((pl.program_id(0) == bsz - 1) & (pair == HEAD_PAIRS - 1))
        def _():
            _exchange_wait("scatter", x_refs, xo_refs, *sems)

    col = lambda off: pl.BlockSpec((seq, LANES), lambda b, p: (b, off + p))
    rowvec = pl.BlockSpec((1, 1, SUBLANES, seq), lambda b, p: (b, p, 0, 0))
    out = jax.ShapeDtypeStruct((bsz * seq, D_MODEL), BF16)
    rows_shape = jax.ShapeDtypeStruct((bsz, HEAD_PAIRS, SUBLANES, seq), F32)
    head2 = pltpu.VMEM((2, seq, LANES), BF16)
    hbm = pl.BlockSpec(memory_space=pl.ANY)
    return pl.pallas_call(
        body, name="att_bwd", grid=(bsz, HEAD_PAIRS),
        in_specs=[col(0), col(HEAD_PAIRS), col(2 * HEAD_PAIRS), col(HEAD_PAIRS), col(0), rowvec,
                  pl.BlockSpec((seq, LANES), lambda b, p: (b, 0))] + [hbm] * nx,
        out_specs=[col(0), col(0), col(0), pl.BlockSpec((seq, LANES), lambda b, p: (b, 0)), rowvec] + [hbm] * nx,
        out_shape=[out, out, out, jax.ShapeDtypeStruct((bsz * seq, LANES), F32), rows_shape]
        + _exchange_shapes("scatter", scatter),
        scratch_shapes=[head2, head2, head2, head2, pltpu.VMEM((2, LANES, seq), BF16),
                        pltpu.VMEM((2, SUBLANES, seq), F32), pltpu.VMEM((2, LANES, seq), F32),
                        pltpu.VMEM((2, blk, LANES), F32), pltpu.VMEM((2, blk, LANES), F32)] + _exchange_scratch(nx),
        compiler_params=_params(("arbitrary", "arbitrary")),
    )(qkv, qkv, qkv, dycat, o, lse, ccol, *scatter)


def _adamw_math(w, g, m, v):
    m = ADAM_B1 * m + (1.0 - ADAM_B1) * g
    v = ADAM_B2 * v + (1.0 - ADAM_B2) * jnp.square(g)
    m_hat = m / ADAM_C1
    v_hat = v / ADAM_C2
    delta = -ADAM_LR * (m_hat / (jnp.sqrt(v_hat) + ADAM_EPS) + ADAM_WD * w)
    return delta, m, v


def _row_tile(rows):
    for tr in (256, 128, 64, 32, 16, 8):
        if rows % tr == 0:
            return tr
    return rows


def _sum_parts(parts, name):
    nparts, rows, cols = parts.shape
    tr = rows if rows % SUBLANES else _row_tile(rows)

    def body(p_ref, o_ref):
        g = p_ref[0].astype(F32)
        for s in range(1, nparts):
            g = g + p_ref[s].astype(F32)
        o_ref[...] = g

    return pl.pallas_call(
        body, name=name, grid=(rows // tr,),
        in_specs=[pl.BlockSpec((nparts, tr, cols), lambda i: (0, i, 0))],
        out_specs=pl.BlockSpec((tr, cols), lambda i: (i, 0)),
        out_shape=jax.ShapeDtypeStruct((rows, cols), F32),
        compiler_params=_params(("parallel",)),
    )(parts)


def _adamw_parts(parts, w, m, v, name):
    nparts, rows, cols = parts.shape
    tr = _row_tile(rows)

    def body(p_ref, w_ref, m_ref, v_ref, g_ref, d_ref, mo_ref, vo_ref):
        g = p_ref[0].astype(F32)
        for s in range(1, nparts):
            g = g + p_ref[s].astype(F32)
        delta, mn, vn = _adamw_math(w_ref[...], g, m_ref[...], v_ref[...])
        g_ref[...] = g
        d_ref[...] = delta
        mo_ref[...] = mn
        vo_ref[...] = vn

    blk = pl.BlockSpec((tr, cols), lambda i: (i, 0))
    shp = jax.ShapeDtypeStruct((rows, cols), F32)
    return pl.pallas_call(
        body, name=name, grid=(rows // tr,),
        in_specs=[pl.BlockSpec((nparts, tr, cols), lambda i: (0, i, 0)), blk, blk, blk],
        out_specs=[blk, blk, blk, blk], out_shape=[shp, shp, shp, shp],
        compiler_params=_params(("parallel",)),
    )(parts, w, m, v)


def _me():
    return lax.axis_index("x"), lax.axis_index("y"), lax.axis_index("c")


def _flip(pos, k):
    x, y, c = pos
    kx, ky, kc = (k >> 2) & 1, (k >> 1) & 1, k & 1
    return (x ^ kx if kx else x, y ^ ky if ky else y, c ^ kc if kc else c)


def _logical(pos):
    return 4 * pos[0] + 2 * pos[1] + pos[2]


def _all_gather_two_level(shards):
    narr = len(shards)

    def body(*refs):
        x_refs, out_refs = refs[:narr], refs[narr:2 * narr]
        send_sems, recv_sems, local_sems = refs[2 * narr:]
        x, y, c = _me()
        me, sibling = (x, y, c), (x, y, 1 - c)
        chips = [(1 - x, y), (x, 1 - y), (1 - x, 1 - y)]

        def copy(a, k, block, to, src=None):
            slot = out_refs[a].at[_logical(block)]
            return pltpu.make_async_remote_copy(
                src_ref=slot if src is None else src, dst_ref=slot,
                send_sem=send_sems.at[a, k], recv_sem=recv_sems.at[a, k], device_id=to, device_id_type=MESH)

        mine = [pltpu.make_async_copy(x_refs[a], out_refs[a].at[_logical(me)], local_sems.at[a]) for a in range(narr)]
        for cp in mine:
            cp.start()
        first = []
        for a in range(narr):
            first.append(copy(a, 0, me, sibling, src=x_refs[a]))
            first += [copy(a, 1 + j, me, (*chip, c), src=x_refs[a]) for j, chip in enumerate(chips)]
        for cp in first:
            cp.start()
        passed = []
        for a in range(narr):
            for j, chip in enumerate(chips):
                copy(a, 1 + j, (*chip, c), me).wait_recv()
                fwd = copy(a, 4 + j, (*chip, c), sibling)
                fwd.start()
                passed.append(fwd)
        for a in range(narr):
            copy(a, 0, sibling, me).wait_recv()
            for j, chip in enumerate(chips):
                copy(a, 4 + j, (*chip, 1 - c), me).wait_recv()
        for cp in first + passed:
            cp.wait_send()
        for cp in mine:
            cp.wait()

    hbm = pl.BlockSpec(memory_space=pl.ANY)
    return pl.pallas_call(
        body, name="all_gather_big",
        out_shape=[jax.ShapeDtypeStruct((N_DEV,) + s.shape, s.dtype) for s in shards],
        in_specs=[hbm] * narr, out_specs=[hbm] * narr,
        scratch_shapes=[pltpu.SemaphoreType.DMA((narr, 7)), pltpu.SemaphoreType.DMA((narr, 7)),
                        pltpu.SemaphoreType.DMA((narr,))],
    )(*shards)


def _exchange_copies(kind, x_refs, out_refs, send_sems, recv_sems, local_sems):
    me = _me()
    mine = _logical(me)
    local, remote = [], []
    for a, (x_ref, out_ref) in enumerate(zip(x_refs, out_refs)):
        own = x_ref if kind == "gather" else x_ref.at[mine]
        local.append(pltpu.make_async_copy(own, out_ref.at[mine], local_sems.at[a]))
        for k in range(1, N_DEV):
            peer = _flip(me, k)
            src = x_ref if kind == "gather" else x_ref.at[_logical(peer)]
            remote.append(pltpu.make_async_remote_copy(
                src_ref=src, dst_ref=out_ref.at[mine], send_sem=send_sems.at[a, k - 1], recv_sem=recv_sems.at[a, k - 1],
                device_id=peer, device_id_type=MESH))
    return local, remote


def _exchange_start(kind, x_refs, out_refs, *sems):
    if not x_refs:
        return
    local, remote = _exchange_copies(kind, x_refs, out_refs, *sems)
    for cp in local + remote:
        cp.start()


def _exchange_wait(kind, x_refs, out_refs, *sems):
    if not x_refs:
        return
    local, remote = _exchange_copies(kind, x_refs, out_refs, *sems)
    for cp in remote:
        cp.wait_recv()
    for cp in remote:
        cp.wait_send()
    for cp in local:
        cp.wait()


def _exchange_shapes(kind, arrays):
    return [jax.ShapeDtypeStruct(((N_DEV,) if kind == "gather" else ()) + a.shape, a.dtype) for a in arrays]


def _exchange_scratch(narrays):
    if not narrays:
        return []
    return [pltpu.SemaphoreType.DMA((narrays, N_DEV - 1)), pltpu.SemaphoreType.DMA((narrays, N_DEV - 1)),
            pltpu.SemaphoreType.DMA((narrays,))]


def _exchange_rows(x_ref, buf_ref, send_sems, recv_sems):
    me = _me()
    buf_ref[_logical(me)] = x_ref[...]
    copies = []
    for k in range(1, N_DEV):
        peer = _flip(me, k)
        copies.append(pltpu.make_async_remote_copy(
            src_ref=x_ref, dst_ref=buf_ref.at[_logical(me)],
            send_sem=send_sems.at[k - 1], recv_sem=recv_sems.at[k - 1], device_id=peer, device_id_type=MESH))
    for cp in copies:
        cp.start()
    for cp in copies:
        cp.wait_recv()
    for cp in copies:
        cp.wait_send()


def _sum_slots(buf_ref):
    total = buf_ref[0]
    for s in range(1, N_DEV):
        total = total + buf_ref[s]
    return total


def _small_gather(x):
    def body(x_ref, o_ref, buf_ref, send_sems, recv_sems):
        _exchange_rows(x_ref, buf_ref, send_sems, recv_sems)
        o_ref[...] = _sum_slots(buf_ref)

    return pl.pallas_call(
        body, name="small_gather", out_shape=jax.ShapeDtypeStruct(x.shape, F32),
        in_specs=[pl.BlockSpec(memory_space=pltpu.VMEM)], out_specs=pl.BlockSpec(memory_space=pltpu.VMEM),
        scratch_shapes=[pltpu.VMEM((N_DEV,) + x.shape, F32), pltpu.SemaphoreType.DMA((7,)), pltpu.SemaphoreType.DMA((7,))],
    )(x)


def _small_reduce_adamw(g, w, m, v):
    def body(g_ref, w_ref, m_ref, v_ref, go_ref, d_ref, mo_ref, vo_ref, buf_ref, send_sems, recv_sems):
        _exchange_rows(g_ref, buf_ref, send_sems, recv_sems)
        gs = _sum_slots(buf_ref)
        delta, mn, vn = _adamw_math(w_ref[...], gs, m_ref[...], v_ref[...])
        go_ref[...] = gs
        d_ref[...] = delta
        mo_ref[...] = mn
        vo_ref[...] = vn

    vm = pl.BlockSpec(memory_space=pltpu.VMEM)
    shp = jax.ShapeDtypeStruct(g.shape, F32)
    return pl.pallas_call(
        body, name="small_reduce_adamw", out_shape=[shp, shp, shp, shp],
        in_specs=[vm, vm, vm, vm], out_specs=[vm, vm, vm, vm],
        scratch_shapes=[pltpu.VMEM((N_DEV,) + g.shape, F32), pltpu.SemaphoreType.DMA((7,)), pltpu.SemaphoreType.DMA((7,))],
    )(g, w, m, v)


def _pad_cols(a, width):
    return jnp.pad(a, ((0, 0), (0, width - a.shape[1])))


def _local_step(x, target, norm_mix_w, w_in_t, conv_w, conv_b, dt_bias, a_log, d_skip, ssd_norm_w, f_bias,
                late_shards, norm_mlp_w, norm_final_w):
    bsz, seq, _ = x.shape
    n = bsz * seq
    x2 = x.reshape(n, D_MODEL)
    t2 = target.reshape(n, D_MODEL)
    nfw = norm_final_w.reshape(1, D_MODEL)

    bias = _pad_cols(dt_bias, LANES)
    arow = _pad_cols(-jnp.exp(a_log), LANES)
    biast, acol = bias.reshape(LANES, 1), arow.reshape(LANES, 1)
    dfull = jnp.repeat(d_skip, HEAD_DIM, axis=1)
    fb = jnp.pad(f_bias, ((0, 0), (F_COL, LANES - 2 * F_COL)))

    wt_z, wt_xbc, wt_qkv = w_in_t[:ROW_XBC], w_in_t[ROW_XBC:ROW_DT], w_in_t[ROW_Q:ROW_F]
    wt_dtf = jnp.concatenate([w_in_t[ROW_DT:ROW_Q], w_in_t[ROW_F:], jnp.zeros((LANES - 2 * F_COL, D_MODEL), BF16)], axis=0)
    wt_q, wt_k, wt_v = wt_qkv[:D_MODEL], wt_qkv[D_MODEL:2 * D_MODEL], wt_qkv[2 * D_MODEL:]

    h1 = _rms_fwd(x2, norm_mix_w, "rms_fwd_mix")
    z = _mm([(h1, wt_z)], "inproj_z", F32, 1024, 1024, nt=True)
    xbc_raw = _mm([(h1, wt_xbc)], "inproj_xbc", F32, 512, 1536, nt=True)
    qkv = _mm([(h1, wt_qkv)], "inproj_qkv", BF16, 512, 3072, nt=True)
    dtf = _mm([(h1, wt_dtf)], "inproj_dtf", F32, 2048, LANES, nt=True)
    dtft = dtf.reshape(bsz, seq, LANES).transpose(0, 2, 1)

    xbc = _conv_fwd(xbc_raw, conv_w, conv_b, bsz, seq)
    y_pre, hall = _ssd_fwd(xbc, dtf, dtft, bias, biast, arow, acol, dfull, bsz, seq)

    ccol = _fox_prep(dtf, fb, bsz, seq)
    o_att, lse, w_out, w_up_t, w_down = _att_fwd(qkv, ccol, bsz, seq, late_shards)
    w_out, w_up_t, w_down = (w.reshape(-1, D_MODEL) for w in (w_out, w_up_t, w_down))

    ycat = _gnorm_fwd(y_pre, z, o_att, ssd_norm_w)
    h2, h2n = _mm([(ycat, w_out)], "outproj", F32, 512, 1024, res=x2, rms_fwd=norm_mlp_w)
    pre = _mm([(h2n, w_up_t)], "mlp_up", F32, 256, 4096, nt=True)

    dh3, loss_row, g_nfw = _mlp_down_loss(pre, w_down, h2, t2, nfw)
    dpre, u = _mlp_down_bwd(dh3, w_down, pre)
    g_w_down = _mm_tn(u, dh3, "grad_w_down", 1024, 1024, 2048)
    g_w_up_t = _mm_tn(dpre, h2n, "grad_w_up", 1024, 1024, 2048)
    by_shard = lambda g: g.reshape(N_DEV, g.shape[0] // N_DEV, D_MODEL)
    dh2, g_nmlp = _mm([(dpre, w_up_t)], "mlp_up_bwd", F32, 512, 1024, res=dh3, rms_bwd=(h2, norm_mlp_w))

    g_w_out = _mm_tn(ycat, dh2, "grad_w_out", 1024, 1024, 2048)
    dycat = _mm([(dh2, w_out)], "outproj_bwd", F32, 512, 2048, nt=True)
    dy_pre, dz, g_ssdn = _gnorm_bwd(dycat, y_pre, z, ssd_norm_w)
    lse_rows = lse.reshape(bsz, seq, LANES)[:, :, F_COL:2 * F_COL].transpose(0, 2, 1).reshape(bsz, HEAD_PAIRS, 2, seq)
    lse_rows = jnp.pad(lse_rows, ((0, 0), (0, 0), (0, SUBLANES - 2), (0, 0)))
    dq, dk, dv, dck, dcq, recv_down, recv_up_t, recv_out = _att_bwd(
        qkv, dycat, o_att, lse_rows, ccol, bsz, seq, [by_shard(g_w_down), by_shard(g_w_up_t), by_shard(g_w_out)])

    dxbc, ddt, ssd_acc = _ssd_bwd(dy_pre, xbc, dtf, dtft, bias, biast, arow, acol, dfull, hall, bsz, seq)
    dxbc_raw, g_cw, g_cb = _conv_bwd(dxbc, xbc_raw, conv_w, conv_b, bsz, seq)

    def head_cols(rows):
        cols = rows[:, :, :2, :].reshape(bsz, SSD_HEADS, seq).transpose(0, 2, 1).reshape(n, SSD_HEADS)
        return jnp.pad(cols, ((0, 0), (F_COL, LANES - 2 * F_COL)))

    ddtf, g_fb = _fox_prep_bwd(dck, head_cols(dcq), ddt, dtf, fb, bsz, seq)

    g_dtf = _mm_tn(ddtf, h1, "grad_w_in_dtf", LANES, 1024, 2048)
    g_w_in_t = jnp.concatenate([
        _mm_tn(dz, h1, "grad_w_in_z", 1024, 1024, 2048), _mm_tn(dxbc_raw, h1, "grad_w_in_xbc", 768, 1024, 2048),
        g_dtf[:F_COL], _mm_tn(dq, h1, "grad_w_in_q", 1024, 1024, 2048), _mm_tn(dk, h1, "grad_w_in_k", 1024, 1024, 2048),
        _mm_tn(dv, h1, "grad_w_in_v", 1024, 1024, 2048), g_dtf[F_COL:2 * F_COL]], axis=0)
    pieces = [(dz, wt_z), (dxbc_raw, wt_xbc), (dq, wt_q), (dk, wt_k), (dv, wt_v), (ddtf, wt_dtf)]
    dx, g_nmix, recv_in_t = _mm(pieces, "inproj_bwd", F32, 256, 1024, res=dh2, rms_bwd=(x2, norm_mix_w),
                                exchange=("scatter", [by_shard(g_w_in_t)]))

    small = dict(
        norm_mix_w=g_nmix, norm_mlp_w=g_nmlp, norm_final_w=g_nfw, ssd_norm_w=g_ssdn, conv_b=g_cb, conv_w=g_cw,
        dt_bias=ssd_acc[16:17, :SSD_HEADS], a_log=ssd_acc[0:1, :SSD_HEADS], d_skip=ssd_acc[8:9, :SSD_HEADS],
        f_bias=g_fb[0:1, F_COL:2 * F_COL])
    recv = dict(w_in_t=recv_in_t, w_out=recv_out, w_up_t=recv_up_t, w_down=recv_down)
    return loss_row[0, 0], dx.reshape(bsz, seq, D_MODEL), small, recv


SMALL_LAYOUT = (("norm_mix_w", 0, 1, 0, D_MODEL), ("norm_mlp_w", 1, 1, 0, D_MODEL), ("norm_final_w", 2, 1, 0, D_MODEL),
                ("ssd_norm_w", 3, 1, 0, D_MODEL), ("conv_b", 4, 1, 0, CONV_CH), ("conv_w", 5, CONV_K, 0, CONV_CH),
                ("dt_bias", 9, 1, 0, SSD_HEADS), ("a_log", 9, 1, LANES, SSD_HEADS), ("d_skip", 9, 1, 2 * LANES, SSD_HEADS),
                ("f_bias", 9, 1, 3 * LANES, SSD_HEADS))
SMALL_ROWS = 2 * SUBLANES


def _pack_small(vals):
    packed = jnp.zeros((SMALL_ROWS, SMALL_COLS), F32)
    for name, r0, nrows, c0, width in SMALL_LAYOUT:
        packed = packed.at[r0:r0 + nrows, c0:c0 + width].set(vals[name].reshape(nrows, width))
    return packed


def _unpack_small(packed, like):
    out = {}
    for name, r0, nrows, c0, width in SMALL_LAYOUT:
        out[name] = packed[r0:r0 + nrows, c0:c0 + width].reshape(like[name].shape)
    return out


def kernel(x, norm_mix_w, w_in, conv_w, conv_b, dt_bias, a_log, d_skip, ssd_norm_w, f_bias, w_out, norm_mlp_w, w_up, w_down, norm_final_w, loss_target, m_norm_mix_w, m_w_in, m_conv_w, m_conv_b, m_dt_bias, m_a_log, m_d_skip, m_ssd_norm_w, m_f_bias, m_w_out, m_norm_mlp_w, m_w_up, m_w_down, m_norm_final_w, v_norm_mix_w, v_w_in, v_conv_w, v_conv_b, v_dt_bias, v_a_log, v_d_skip, v_ssd_norm_w, v_f_bias, v_w_out, v_norm_mlp_w, v_w_up, v_w_down, v_norm_final_w):
    me = 4 * lax.axis_index("x") + 2 * lax.axis_index("y") + lax.axis_index("c")
    conv_shard_w = CONV_CH // N_DEV
    zero = jnp.zeros((), jnp.int32)

    def place_conv(shard):
        return lax.dynamic_update_slice(jnp.zeros((CONV_K, CONV_CH), F32), shard[0], (zero, me * conv_shard_w))

    (g_in_t,) = _all_gather_two_level([w_in[0].T.astype(BF16)])
    late_shards = [w_out[0].astype(BF16), w_up[0].T.astype(BF16), w_down[0].astype(BF16)]
    conv_full = _small_gather(jnp.pad(place_conv(conv_w), ((0, SUBLANES - CONV_K), (0, 0))))[:CONV_K]

    loss_local, grad_x, g_small, recv = _local_step(
        x, loss_target, norm_mix_w, g_in_t.reshape(IN_WIDTH, D_MODEL), conv_full, conv_b, dt_bias, a_log, d_skip,
        ssd_norm_w, f_bias, late_shards, norm_mlp_w, norm_final_w)
    loss = lax.psum(loss_local, MESH_AXES)

    grad_in = _sum_parts(recv["w_in_t"], "sum_w_in").T[None]
    grad_up = _sum_parts(recv["w_up_t"], "sum_w_up").T[None]
    big = {
        "w_in": _adamw_parts(grad_in, w_in[0], m_w_in[0], v_w_in[0], "adamw_w_in"),
        "w_out": _adamw_parts(recv["w_out"], w_out[0], m_w_out[0], v_w_out[0], "adamw_w_out"),
        "w_up": _adamw_parts(grad_up, w_up[0], m_w_up[0], v_w_up[0], "adamw_w_up"),
        "w_down": _adamw_parts(recv["w_down"], w_down[0], m_w_down[0], v_w_down[0], "adamw_w_down"),
    }

    like = dict(norm_mix_w=norm_mix_w, norm_mlp_w=norm_mlp_w, norm_final_w=norm_final_w, ssd_norm_w=ssd_norm_w,
                conv_b=conv_b, conv_w=jnp.zeros((1, CONV_K, CONV_CH), F32), dt_bias=dt_bias, a_log=a_log,
                d_skip=d_skip, f_bias=f_bias)
    w_small = dict(like, conv_w=place_conv(conv_w))
    m_small = dict(norm_mix_w=m_norm_mix_w, norm_mlp_w=m_norm_mlp_w, norm_final_w=m_norm_final_w,
                   ssd_norm_w=m_ssd_norm_w, conv_b=m_conv_b, conv_w=place_conv(m_conv_w), dt_bias=m_dt_bias,
                   a_log=m_a_log, d_skip=m_d_skip, f_bias=m_f_bias)
    v_small = dict(norm_mix_w=v_norm_mix_w, norm_mlp_w=v_norm_mlp_w, norm_final_w=v_norm_final_w,
                   ssd_norm_w=v_ssd_norm_w, conv_b=v_conv_b, conv_w=place_conv(v_conv_w), dt_bias=v_dt_bias,
                   a_log=v_a_log, d_skip=v_d_skip, f_bias=v_f_bias)
    small = _small_reduce_adamw(_pack_small(g_small), _pack_small(w_small), _pack_small(m_small), _pack_small(v_small))
    small = [_unpack_small(s, like) for s in small]
    for s in small:
        s["conv_w"] = lax.dynamic_slice(s["conv_w"], (zero, zero, me * conv_shard_w), (1, CONV_K, conv_shard_w))

    order = ["norm_mix_w", "w_in", "conv_w", "conv_b", "dt_bias", "a_log", "d_skip", "ssd_norm_w", "f_bias", "w_out",
             "norm_mlp_w", "w_up", "w_down", "norm_final_w"]
    outs = [loss, grad_x]
    for kind in range(4):
        for name in order:
            outs.append(big[name][kind][None] if name in big else small[kind][name])
    return tuple(outs)
```

```python
import jax
import jax.numpy as jnp
from jax import lax
from jax.experimental import pallas as pl
from jax.experimental.pallas import tpu as pltpu

F32, BF16 = jnp.float32, jnp.bfloat16
HI = lax.Precision.HIGHEST

D_MODEL = 1024
SSD_HEADS = 16
HEAD_DIM = 64
SSD_STATE = 128
CHUNK = 128
CONV_CH = 1536
CONV_K = 4
D_FF = 4096
MIX_WIDTH = 2048
IN_WIDTH = 5664
NORM_EPS = 1e-5
ATT_SCALE = 0.125

LANES = 128
SUBLANES = 8
HEAD_PAIRS = SSD_HEADS // 2
NEG = -1e30
VMEM_LIMIT = 52 * 1024 * 1024

ROW_XBC, ROW_DT, ROW_Q, ROW_F = 1024, 2560, 2576, 5648
F_COL = SSD_HEADS

N_DEV = 8
MESH_AXES = ("x", "y", "c")
MESH = pl.DeviceIdType.MESH

ADAM_LR, ADAM_B1, ADAM_B2, ADAM_EPS, ADAM_WD, ADAM_STEP = 0.001, 0.9, 0.999, 1e-08, 0.01, 10
ADAM_C1 = 1.0 - ADAM_B1 ** ADAM_STEP
ADAM_C2 = 1.0 - ADAM_B2 ** ADAM_STEP

SMALL_COLS = CONV_CH


def _params(sem=None):
    return pltpu.CompilerParams(dimension_semantics=sem, vmem_limit_bytes=VMEM_LIMIT)


def _sigmoid(u):
    return 1.0 / (1.0 + jnp.exp(-u))


def _softplus(u):
    return jnp.maximum(u, 0.0) + jnp.log(1.0 + jnp.exp(-jnp.abs(u)))


def _log_sigmoid(u):
    return jnp.minimum(u, 0.0) - jnp.log(1.0 + jnp.exp(-jnp.abs(u)))


def _bdot(a, b):
    return jnp.dot(a.astype(BF16), b.astype(BF16), preferred_element_type=F32)


def _bdot_nt(a, b):
    return lax.dot_general(a.astype(BF16), b.astype(BF16), (((1,), (1,)), ((), ())), preferred_element_type=F32)


def _bdot_tn(a, b):
    return lax.dot_general(a.astype(BF16), b.astype(BF16), (((0,), (0,)), ((), ())), preferred_element_type=F32)


def _split3(x):
    x1 = x.astype(BF16)
    r1 = x - x1.astype(F32)
    x2 = r1.astype(BF16)
    return x1, x2, (r1 - x2.astype(F32)).astype(BF16)


def _dot_sel(x, sel):
    s = sel.astype(BF16)
    p1, p2, p3 = (jnp.dot(p, s, preferred_element_type=F32) for p in _split3(x))
    return p1 + p2 + p3


def _sel_dot(sel, x):
    s = sel.astype(BF16)
    p1, p2, p3 = (jnp.dot(s, p, preferred_element_type=F32) for p in _split3(x))
    return p1 + p2 + p3


def _iota(shape, dim):
    return lax.broadcasted_iota(jnp.int32, shape, dim)


def _head_expand():
    return (jnp.right_shift(_iota((LANES, D_MODEL), 1), 6) == _iota((LANES, D_MODEL), 0)).astype(F32)


def _head_reduce():
    return (jnp.right_shift(_iota((D_MODEL, LANES), 0), 6) == _iota((D_MODEL, LANES), 1)).astype(F32)


def _rms_fwd(x, w, name):
    n, d = x.shape
    tm = min(1024, n)

    def body(x_ref, w_ref, o_ref):
        xv = x_ref[...]
        r = lax.rsqrt(jnp.mean(xv * xv, axis=-1, keepdims=True) + NORM_EPS)
        o_ref[...] = (xv * r * w_ref[...]).astype(BF16)

    return pl.pallas_call(
        body, name=name, grid=(n // tm,),
        in_specs=[pl.BlockSpec((tm, d), lambda i: (i, 0)), pl.BlockSpec((1, d), lambda i: (0, 0))],
        out_specs=pl.BlockSpec((tm, d), lambda i: (i, 0)),
        out_shape=jax.ShapeDtypeStruct((n, d), BF16),
        compiler_params=_params(("parallel",)),
    )(x, w)


def _mm(pairs, name, out_dtype, tm, tn, nt=False, res=None, exchange=None, rms_fwd=None, rms_bwd=None):
    m = pairs[0][0].shape[0]
    n = pairs[0][1].shape[0 if nt else 1]
    tm, tn = min(tm, m), min(tn, n)
    gm, gn = m // tm, n // tn
    npairs = len(pairs)
    kind, xs = (None, []) if exchange is None else exchange
    nx = len(xs)
    extra_in = [] if res is None else [res]
    if rms_bwd is not None:
        extra_in += list(rms_bwd)
    elif rms_fwd is not None:
        extra_in.append(rms_fwd)
    n_in = 2 * npairs + len(extra_in)
    n_out = 1 + (rms_fwd is not None or rms_bwd is not None)
    assert (rms_fwd is None and rms_bwd is None) or tn == n

    def body(*refs):
        x_refs, o_refs = refs[n_in:n_in + nx], refs[n_in + nx:n_in + nx + n_out]
        xo_refs, sems = refs[n_in + nx + n_out:n_in + 2 * nx + n_out], refs[n_in + 2 * nx + n_out:]
        extra = refs[2 * npairs:n_in]
        i, j = pl.program_id(0), pl.program_id(1)
        if nx:
            @pl.when((i == 0) & (j == 0))
            def _():
                _exchange_start(kind, x_refs, xo_refs, *sems)

        acc = None
        for p in range(npairs):
            a_v, b_v = refs[2 * p][...], refs[2 * p + 1][...]
            d = _bdot_nt(a_v, b_v) if nt else _bdot(a_v, b_v)
            acc = d if acc is None else acc + d
        if rms_bwd is not None:
            xv = extra[1][...]
            r = lax.rsqrt(jnp.mean(xv * xv, axis=-1, keepdims=True) + NORM_EPS)
            nrm = xv * r
            g = acc * extra[2][...]
            o_refs[0][...] = extra[0][...] + r * (g - nrm * jnp.mean(g * nrm, axis=-1, keepdims=True))

            @pl.when(i == 0)
            def _():
                o_refs[1][...] = jnp.zeros_like(o_refs[1])

            o_refs[1][...] += jnp.sum(acc * nrm, axis=0, keepdims=True)
        else:
            if res is not None:
                acc = extra[0][...] + acc
            o_refs[0][...] = acc.astype(o_refs[0].dtype)
            if rms_fwd is not None:
                r = lax.rsqrt(jnp.mean(acc * acc, axis=-1, keepdims=True) + NORM_EPS)
                o_refs[1][...] = (acc * r * extra[-1][...]).astype(BF16)
        if nx:
            @pl.when((i == gm - 1) & (j == gn - 1))
            def _():
                _exchange_wait(kind, x_refs, xo_refs, *sems)

    tile = pl.BlockSpec((tm, tn), lambda i, j: (i, j))
    row = pl.BlockSpec((1, tn), lambda i, j: (0, j))
    in_specs, args = [], []
    for a, b in pairs:
        k = a.shape[1]
        in_specs.append(pl.BlockSpec((tm, k), lambda i, j: (i, 0)))
        in_specs.append(pl.BlockSpec((tn, k), lambda i, j: (j, 0)) if nt else pl.BlockSpec((k, tn), lambda i, j: (0, j)))
        args += [a, b]
    in_specs += [row if e.shape[0] == 1 else tile for e in extra_in]
    out_specs, out_shape = [tile], [jax.ShapeDtypeStruct((m, n), out_dtype)]
    if rms_bwd is not None:
        out_specs.append(row)
        out_shape.append(jax.ShapeDtypeStruct((1, n), F32))
    elif rms_fwd is not None:
        out_specs.append(tile)
        out_shape.append(jax.ShapeDtypeStruct((m, n), BF16))
    hbm = pl.BlockSpec(memory_space=pl.ANY)
    sequential = nx or rms_bwd is not None
    outs = pl.pallas_call(
        body, name=name, grid=(gm, gn), in_specs=in_specs + [hbm] * nx,
        out_specs=out_specs + [hbm] * nx,
        out_shape=out_shape + _exchange_shapes(kind, xs),
        scratch_shapes=_exchange_scratch(nx),
        compiler_params=_params(("arbitrary", "arbitrary") if sequential else ("parallel", "parallel")),
    )(*args, *extra_in, *xs)
    return outs if len(outs) > 1 else outs[0]


def _mm_tn(a, b, name, tm, tn, tk):
    t, m = a.shape
    n = b.shape[1]
    tm, tn, tk = min(tm, m), min(tn, n), min(tk, t)
    nk = t // tk

    def body(a_ref, b_ref, o_ref, acc_ref):
        kk = pl.program_id(2)

        @pl.when(kk == 0)
        def _():
            acc_ref[...] = jnp.zeros_like(acc_ref)

        acc_ref[...] += _bdot_tn(a_ref[...], b_ref[...])

        @pl.when(kk == nk - 1)
        def _():
            o_ref[...] = acc_ref[...].astype(o_ref.dtype)

    return pl.pallas_call(
        body, name=name, grid=(m // tm, n // tn, nk),
        in_specs=[pl.BlockSpec((tk, tm), lambda i, j, kk: (kk, i)), pl.BlockSpec((tk, tn), lambda i, j, kk: (kk, j))],
        out_specs=pl.BlockSpec((tm, tn), lambda i, j, kk: (i, j)),
        out_shape=jax.ShapeDtypeStruct((m, n), BF16),
        scratch_shapes=[pltpu.VMEM((tm, tn), F32)],
        compiler_params=_params(("parallel", "parallel", "arbitrary")),
    )(a, b)


def _mlp_down_loss(pre, w_down, h2, target, w):
    m, k = pre.shape
    d = w_down.shape[1]
    tm = min(256, m)

    def body(p_ref, b_ref, r_ref, t_ref, w_ref, dh_ref, loss_ref, gw_ref):
        u = jnp.square(jnp.maximum(p_ref[...], 0.0))
        xv = r_ref[...] + _bdot(u, b_ref[...])
        r = lax.rsqrt(jnp.mean(xv * xv, axis=-1, keepdims=True) + NORM_EPS)
        nrm = xv * r
        wv = w_ref[...]
        err = nrm * wv - t_ref[...]
        part = 0.5 * jnp.sum(jnp.mean(err * err, axis=-1, keepdims=True), axis=0, keepdims=True)
        dy = err * (1.0 / d)
        g = dy * wv
        dh_ref[...] = r * (g - nrm * jnp.mean(g * nrm, axis=-1, keepdims=True))

        @pl.when(pl.program_id(0) == 0)
        def _():
            loss_ref[...] = jnp.zeros_like(loss_ref)
            gw_ref[...] = jnp.zeros_like(gw_ref)

        loss_ref[...] += jnp.broadcast_to(part, loss_ref.shape)
        gw_ref[...] += jnp.sum(dy * nrm, axis=0, keepdims=True)

    tok = pl.BlockSpec((tm, d), lambda i: (i, 0))
    row = pl.BlockSpec((1, d), lambda i: (0, 0))
    return pl.pallas_call(
        body, name="mlp_down_loss", grid=(m // tm,),
        in_specs=[pl.BlockSpec((tm, k), lambda i: (i, 0)), pl.BlockSpec((k, d), lambda i: (0, 0)), tok, tok, row],
        out_specs=[tok, pl.BlockSpec((1, LANES), lambda i: (0, 0)), row],
        out_shape=[jax.ShapeDtypeStruct((m, d), F32), jax.ShapeDtypeStruct((1, LANES), F32),
                   jax.ShapeDtypeStruct((1, d), F32)],
        compiler_params=_params(("arbitrary",)),
    )(pre, w_down, h2, target, w)


def _mlp_down_bwd(dh3, w_down, pre):
    m, k = dh3.shape
    n = w_down.shape[0]
    tm, tn = min(256, m), n

    def body(a_ref, b_ref, p_ref, dpre_ref, u_ref):
        r = jnp.maximum(p_ref[...], 0.0)
        du = _bdot_nt(a_ref[...], b_ref[...])
        dpre_ref[...] = (du * (2.0 * r)).astype(BF16)
        u_ref[...] = (r * r).astype(BF16)

    blk = pl.BlockSpec((tm, tn), lambda i, j: (i, j))
    return pl.pallas_call(
        body, name="mlp_down_bwd", grid=(m // tm, n // tn),
        in_specs=[pl.BlockSpec((tm, k), lambda i, j: (i, 0)), pl.BlockSpec((tn, k), lambda i, j: (j, 0)), blk],
        out_specs=[blk, blk],
        out_shape=[jax.ShapeDtypeStruct((m, n), BF16), jax.ShapeDtypeStruct((m, n), BF16)],
        compiler_params=_params(("parallel", "parallel")),
    )(dh3, w_down, pre)


CONV_TC = 512


def _conv_fwd(xbc, cw, cb, bsz, seq):
    tt = min(512, seq)
    nt, hb = seq // tt, tt // SUBLANES
    x3 = xbc.reshape(bsz, seq, CONV_CH)

    def body(xm_ref, xh_ref, w_ref, b_ref, o_ref):
        i = pl.program_id(2)
        x = xm_ref[0]
        halo = jnp.where(i > 0, xh_ref[0], 0.0)
        xx = jnp.concatenate([halo, x], axis=0)
        acc = x * w_ref[3:4, :] + b_ref[...]
        for s in range(1, CONV_K):
            acc = acc + pltpu.roll(xx, s, 0)[SUBLANES:, :] * w_ref[3 - s:4 - s, :]
        o_ref[0] = acc * _sigmoid(acc)

    out = pl.pallas_call(
        body, name="conv_fwd", grid=(CONV_CH // CONV_TC, bsz, nt),
        in_specs=[pl.BlockSpec((1, tt, CONV_TC), lambda c, b, i: (b, i, c)),
                  pl.BlockSpec((1, SUBLANES, CONV_TC), lambda c, b, i: (b, jnp.maximum(i * hb - 1, 0), c)),
                  pl.BlockSpec((CONV_K, CONV_TC), lambda c, b, i: (0, c)),
                  pl.BlockSpec((1, CONV_TC), lambda c, b, i: (0, c))],
        out_specs=pl.BlockSpec((1, tt, CONV_TC), lambda c, b, i: (b, i, c)),
        out_shape=jax.ShapeDtypeStruct((bsz, seq, CONV_CH), F32),
        compiler_params=_params(("parallel", "parallel", "parallel")),
    )(x3, x3, cw, cb)
    return out.reshape(bsz * seq, CONV_CH)


def _conv_bwd(da, xbc, cw, cb, bsz, seq):
    tt = min(512, seq)
    nt, hb = seq // tt, tt // SUBLANES
    x3 = xbc.reshape(bsz, seq, CONV_CH)
    da3 = da.reshape(bsz, seq, CONV_CH)
    n2 = tt + SUBLANES

    def body(dam_ref, daa_ref, xm_ref, xb_ref, xa_ref, w_ref, b_ref, du_ref, gw_ref, gb_ref):
        bi, i = pl.program_id(1), pl.program_id(2)
        before = jnp.where(i > 0, xb_ref[0], 0.0)
        after = jnp.where(i < nt - 1, xa_ref[0], 0.0)
        xx = jnp.concatenate([before, xm_ref[0], after], axis=0)
        rolled = [xx] + [pltpu.roll(xx, s, 0) for s in range(1, CONV_K)]
        pre = b_ref[...]
        for s in range(CONV_K):
            pre = pre + rolled[s][SUBLANES:, :] * w_ref[3 - s:4 - s, :]
        da_ext = jnp.concatenate([dam_ref[0], jnp.where(i < nt - 1, daa_ref[0], 0.0)], axis=0)
        sg = _sigmoid(pre)
        dpre = da_ext * sg * (1.0 + pre * (1.0 - sg))
        du = dpre[:tt, :] * w_ref[3:4, :]
        for s in range(1, CONV_K):
            du = du + pltpu.roll(dpre, n2 - s, 0)[:tt, :] * w_ref[3 - s:4 - s, :]
        du_ref[0] = du.astype(BF16)
        dpm = dpre[:tt, :]
        gws = [jnp.sum(dpm * rolled[3 - kk][SUBLANES:SUBLANES + tt, :], axis=0, keepdims=True) for kk in range(CONV_K)]

        @pl.when((bi == 0) & (i == 0))
        def _():
            gw_ref[...] = jnp.zeros_like(gw_ref)
            gb_ref[...] = jnp.zeros_like(gb_ref)

        gw_ref[...] += jnp.concatenate(gws, axis=0)
        gb_ref[...] += jnp.sum(dpm, axis=0, keepdims=True)

    main = pl.BlockSpec((1, tt, CONV_TC), lambda c, b, i: (b, i, c))
    prev = pl.BlockSpec((1, SUBLANES, CONV_TC), lambda c, b, i: (b, jnp.maximum(i * hb - 1, 0), c))
    nxt = pl.BlockSpec((1, SUBLANES, CONV_TC), lambda c, b, i: (b, jnp.minimum((i + 1) * hb, seq // SUBLANES - 1), c))
    du, gw, gb = pl.pallas_call(
        body, name="conv_bwd", grid=(CONV_CH // CONV_TC, bsz, nt),
        in_specs=[main, nxt, main, prev, nxt,
                  pl.BlockSpec((CONV_K, CONV_TC), lambda c, b, i: (0, c)),
                  pl.BlockSpec((1, CONV_TC), lambda c, b, i: (0, c))],
        out_specs=[main, pl.BlockSpec((CONV_K, CONV_TC), lambda c, b, i: (0, c)),
                   pl.BlockSpec((1, CONV_TC), lambda c, b, i: (0, c))],
        out_shape=[jax.ShapeDtypeStruct((bsz, seq, CONV_CH), BF16), jax.ShapeDtypeStruct((CONV_K, CONV_CH), F32),
                   jax.ShapeDtypeStruct((1, CONV_CH), F32)],
        compiler_params=_params(("parallel", "arbitrary", "arbitrary")),
    )(da3, da3, x3, x3, x3, cw, cb)
    return du.reshape(bsz * seq, CONV_CH), gw, gb


def _ssd_prologue(dtf_ref, dtft_ref, bias_ref, biast_ref, arow_ref, acol_ref, dtfull_scr, acfull_scr, acr_scr):
    tri = (_iota((CHUNK, CHUNK), 1) <= _iota((CHUNK, CHUNK), 0)).astype(F32)
    triu = (_iota((CHUNK, CHUNK), 0) <= _iota((CHUNK, CHUNK), 1)).astype(F32)
    dtc = _softplus(dtf_ref[0] + bias_ref[...])
    a = dtc * arow_ref[...]
    acum = _sel_dot(tri, a)
    expand = _head_expand()
    dtfull_scr[...] = _dot_sel(dtc, expand)
    acfull_scr[...] = _dot_sel(acum, expand)
    dtr = _softplus(dtft_ref[0] + biast_ref[...])
    acr_scr[...] = _dot_sel(dtr * acol_ref[...], triu)
    return dtc, acum


def _decay_matrix(acum, acr_scr, h):
    lane = _iota((CHUNK, LANES), 1)
    ac_col = jnp.sum(jnp.where(lane == h, acum, 0.0), axis=1, keepdims=True)
    ac_row = acr_scr[h:h + 1, :]
    causal = _iota((CHUNK, CHUNK), 1) <= _iota((CHUNK, CHUNK), 0)
    return jnp.exp(jnp.where(causal, ac_col - ac_row, NEG))


def _ssd_fwd(xbc, dtf, dtft, bias, biast, arow, acol, dfull, bsz, seq):
    nc = seq // CHUNK
    x3 = xbc.reshape(bsz, seq, CONV_CH)

    def body(xbc_ref, dtf_ref, dtft_ref, bias_ref, biast_ref, arow_ref, acol_ref, dfull_ref,
             y_ref, hall_ref, h_scr, dtfull_scr, acfull_scr, acr_scr):
        @pl.when(pl.program_id(1) == 0)
        def _():
            h_scr[...] = jnp.zeros_like(h_scr)

        _, acum = _ssd_prologue(dtf_ref, dtft_ref, bias_ref, biast_ref, arow_ref, acol_ref,
                                dtfull_scr, acfull_scr, acr_scr)
        lane = _iota((CHUNK, LANES), 1)
        for g in range(2):
            bg = xbc_ref[0, :, D_MODEL + LANES * g:D_MODEL + LANES * (g + 1)]
            cg = xbc_ref[0, :, D_MODEL + 2 * LANES + LANES * g:D_MODEL + 2 * LANES + LANES * (g + 1)]
            cg_bf = cg.astype(BF16)
            gmat = _bdot_nt(cg_bf, bg)
            bgt_bf = bg.T.astype(BF16)
            for j in range(4):
                p = 4 * g + j
                sl = slice(LANES * p, LANES * (p + 1))
                xs = xbc_ref[0, :, sl]
                ac = acfull_scr[:, sl]
                acl = acfull_scr[CHUNK - 1:CHUNK, sl]
                xdt = xs * dtfull_scr[:, sl]
                xdt_bf = xdt.astype(BF16)
                hp = h_scr[p]
                hall_ref[0, 0, p] = hp
                yo = _bdot(cg_bf, hp) * jnp.exp(ac)
                yd = []
                for hh in range(2):
                    mmat = gmat * _decay_matrix(acum, acr_scr, 2 * p + hh)
                    yd.append(_bdot(mmat, xdt_bf))
                y_ref[0, :, sl] = jnp.where(lane < HEAD_DIM, yd[0], yd[1]) + yo + dfull_ref[:, sl] * xs
                h_scr[p] = hp * jnp.exp(acl) + _bdot(bgt_bf, xdt * jnp.exp(acl - ac))

    row = lambda w: pl.BlockSpec((1, w), lambda b, c: (0, 0))
    y, hall = pl.pallas_call(
        body, name="ssd_fwd", grid=(bsz, nc),
        in_specs=[pl.BlockSpec((1, CHUNK, CONV_CH), lambda b, c: (b, c, 0)),
                  pl.BlockSpec((1, CHUNK, LANES), lambda b, c: (b, c, 0)),
                  pl.BlockSpec((1, LANES, CHUNK), lambda b, c: (b, 0, c)),
                  row(LANES), pl.BlockSpec((LANES, 1), lambda b, c: (0, 0)),
                  row(LANES), pl.BlockSpec((LANES, 1), lambda b, c: (0, 0)), row(D_MODEL)],
        out_specs=[pl.BlockSpec((1, CHUNK, D_MODEL), lambda b, c: (b, c, 0)),
                   pl.BlockSpec((1, 1, HEAD_PAIRS, SSD_STATE, LANES), lambda b, c: (b, c, 0, 0, 0))],
        out_shape=[jax.ShapeDtypeStruct((bsz, seq, D_MODEL), F32),
                   jax.ShapeDtypeStruct((bsz, nc, HEAD_PAIRS, SSD_STATE, LANES), F32)],
        scratch_shapes=[pltpu.VMEM((HEAD_PAIRS, SSD_STATE, LANES), F32), pltpu.VMEM((CHUNK, D_MODEL), F32),
                        pltpu.VMEM((CHUNK, D_MODEL), F32), pltpu.VMEM((LANES, CHUNK), F32)],
        compiler_params=_params(("parallel", "arbitrary")),
    )(x3, dtf.reshape(bsz, seq, LANES), dtft, bias, biast, arow, acol, dfull)
    return y.reshape(bsz * seq, D_MODEL), hall


def _ssd_bwd(dy, xbc, dtf, dtft, bias, biast, arow, acol, dfull, hall, bsz, seq):
    nc = seq // CHUNK
    x3 = xbc.reshape(bsz, seq, CONV_CH)
    dy3 = dy.reshape(bsz, seq, D_MODEL)

    def body(dy_ref, xbc_ref, dtf_ref, dtft_ref, bias_ref, biast_ref, arow_ref, acol_ref, dfull_ref, hall_ref,
             dx_ref, ddt_ref, acc_ref, dh_scr, dtfull_scr, acfull_scr, acr_scr, csum_scr, dacf_scr, ddtf_scr, ddl_scr):
        first = (pl.program_id(0) == 0) & (pl.program_id(1) == 0)

        @pl.when(first)
        def _():
            acc_ref[...] = jnp.zeros_like(acc_ref)

        @pl.when(pl.program_id(1) == 0)
        def _():
            dh_scr[...] = jnp.zeros_like(dh_scr)

        dtc, acum = _ssd_prologue(dtf_ref, dtft_ref, bias_ref, biast_ref, arow_ref, acol_ref,
                                  dtfull_scr, acfull_scr, acr_scr)
        csum_scr[...] = jnp.zeros_like(csum_scr)
        lane = _iota((CHUNK, LANES), 1)
        last_row = _iota((CHUNK, LANES), 0) == CHUNK - 1
        rs16 = jnp.zeros((CHUNK, LANES), F32)
        for g in range(2):
            bsl = slice(D_MODEL + LANES * g, D_MODEL + LANES * (g + 1))
            csl = slice(D_MODEL + 2 * LANES + LANES * g, D_MODEL + 2 * LANES + LANES * (g + 1))
            bg_bf = xbc_ref[0, :, bsl].astype(BF16)
            cg = xbc_ref[0, :, csl]
            cg_bf = cg.astype(BF16)
            cgt_bf = cg.T.astype(BF16)
            gmat = _bdot_nt(cg_bf, bg_bf)
            dg = jnp.zeros((CHUNK, CHUNK), F32)
            dbg = jnp.zeros((CHUNK, SSD_STATE), F32)
            dcg = jnp.zeros((CHUNK, SSD_STATE), F32)
            for j in range(4):
                p = 4 * g + j
                sl = slice(LANES * p, LANES * (p + 1))
                xs = xbc_ref[0, :, sl]
                dt = dtfull_scr[:, sl]
                ac = acfull_scr[:, sl]
                acl = acfull_scr[CHUNK - 1:CHUNK, sl]
                dyp = dy_ref[0, :, sl]
                xdt = xs * dt
                xdt_bf = xdt.astype(BF16)
                e = jnp.exp(ac)
                dsd = jnp.exp(acl - ac)
                cd = jnp.exp(acl)
                xds_bf = (xdt * dsd).astype(BF16)
                hp = hall_ref[0, 0, p]
                hp_bf = hp.astype(BF16)
                dhn = dh_scr[p]
                dhn_bf = dhn.astype(BF16)
                yo = _bdot(cg_bf, hp_bf) * e
                dw_bf = (dyp * e).astype(BF16)
                dcg = dcg + _bdot_nt(dw_bf, hp_bf)
                dhin = _bdot(cgt_bf, dw_bf)
                dac = dyp * yo
                dacl = jnp.sum(dhn * hp, axis=0, keepdims=True) * cd
                dxds = _bdot(bg_bf, dhn_bf)
                dbg = dbg + _bdot_nt(xds_bf, dhn_bf)
                dxdt = dxds * dsd
                tdec = dxds * xdt * dsd
                dacl = dacl + jnp.sum(tdec, axis=0, keepdims=True)
                dac = dac - tdec
                dh_scr[p] = dhn * cd + dhin
                for hh in range(2):
                    h = 2 * p + hh
                    lm = (lane < HEAD_DIM) if hh == 0 else (lane >= HEAD_DIM)
                    dec = _decay_matrix(acum, acr_scr, h)
                    mmat = gmat * dec
                    dyh_bf = jnp.where(lm, dyp, 0.0).astype(BF16)
                    dm = _bdot_nt(dyh_bf, xdt_bf)
                    dxdt = dxdt + _bdot(mmat.T, dyh_bf)
                    dg = dg + dm * dec
                    q = dm * mmat
                    rs16 = rs16 + jnp.where(lane == h, jnp.sum(q, axis=1, keepdims=True), 0.0)
                    csum_scr[h:h + 1, :] = jnp.sum(q, axis=0, keepdims=True)
                dx_ref[0, :, sl] = dfull_ref[:, sl] * dyp + dxdt * dt
                dacf_scr[:, sl] = dac + jnp.where(last_row, dacl, 0.0)
                ddtf_scr[:, sl] = dxdt * xs
                ddl_scr[:, sl] = jnp.broadcast_to(jnp.sum(dyp * xs, axis=0, keepdims=True), (SUBLANES, LANES))
            dg_bf = dg.astype(BF16)
            dx_ref[0, :, bsl] = dbg + _bdot(dg.T, cg_bf)
            dx_ref[0, :, csl] = dcg + _bdot(dg_bf, bg_bf)
        reduce = _head_reduce()
        triu = (_iota((CHUNK, CHUNK), 0) <= _iota((CHUNK, CHUNK), 1)).astype(F32)
        dac16 = _dot_sel(dacf_scr[...], reduce) + rs16 - csum_scr[...].T
        da16 = _sel_dot(triu, dac16)
        ddt16 = da16 * arow_ref[...] + _dot_sel(ddtf_scr[...], reduce)
        ddtraw = ddt16 * _sigmoid(dtf_ref[0] + bias_ref[...])
        ddt_ref[0] = ddtraw
        acc_ref[0:8, :] += jnp.broadcast_to(jnp.sum(da16 * dtc * arow_ref[...], axis=0, keepdims=True), (SUBLANES, LANES))
        acc_ref[8:16, :] += _dot_sel(ddl_scr[...], reduce)
        acc_ref[16:24, :] += jnp.broadcast_to(jnp.sum(ddtraw, axis=0, keepdims=True), (SUBLANES, LANES))

    rev = lambda b, c: (b, nc - 1 - c, 0)
    row = lambda w: pl.BlockSpec((1, w), lambda b, c: (0, 0))
    dx, ddt, acc = pl.pallas_call(
        body, name="ssd_bwd", grid=(bsz, nc),
        in_specs=[pl.BlockSpec((1, CHUNK, D_MODEL), rev), pl.BlockSpec((1, CHUNK, CONV_CH), rev),
                  pl.BlockSpec((1, CHUNK, LANES), rev),
                  pl.BlockSpec((1, LANES, CHUNK), lambda b, c: (b, 0, nc - 1 - c)),
                  row(LANES), pl.BlockSpec((LANES, 1), lambda b, c: (0, 0)),
                  row(LANES), pl.BlockSpec((LANES, 1), lambda b, c: (0, 0)), row(D_MODEL),
                  pl.BlockSpec((1, 1, HEAD_PAIRS, SSD_STATE, LANES), lambda b, c: (b, nc - 1 - c, 0, 0, 0))],
        out_specs=[pl.BlockSpec((1, CHUNK, CONV_CH), rev), pl.BlockSpec((1, CHUNK, LANES), rev),
                   pl.BlockSpec((3 * SUBLANES, LANES), lambda b, c: (0, 0))],
        out_shape=[jax.ShapeDtypeStruct((bsz, seq, CONV_CH), F32), jax.ShapeDtypeStruct((bsz, seq, LANES), F32),
                   jax.ShapeDtypeStruct((3 * SUBLANES, LANES), F32)],
        scratch_shapes=[pltpu.VMEM((HEAD_PAIRS, SSD_STATE, LANES), F32), pltpu.VMEM((CHUNK, D_MODEL), F32),
                        pltpu.VMEM((CHUNK, D_MODEL), F32), pltpu.VMEM((LANES, CHUNK), F32),
                        pltpu.VMEM((LANES, CHUNK), F32), pltpu.VMEM((CHUNK, D_MODEL), F32),
                        pltpu.VMEM((CHUNK, D_MODEL), F32), pltpu.VMEM((SUBLANES, D_MODEL), F32)],
        compiler_params=_params(("arbitrary", "arbitrary")),
    )(dy3, x3, dtf.reshape(bsz, seq, LANES), dtft, bias, biast, arow, acol, dfull, hall)
    return dx.reshape(bsz * seq, CONV_CH), ddt.reshape(bsz * seq, LANES), acc


GROUP_W = D_MODEL // 2


def _gnorm_fwd(y, z, o_att, w):
    n = y.shape[0]
    tm = min(1024, n)

    def body(y_ref, z_ref, o_ref, w_ref, out_ref):
        zv = z_ref[...]
        g = y_ref[...] * (zv * _sigmoid(zv))
        for gi in range(2):
            sl = slice(GROUP_W * gi, GROUP_W * (gi + 1))
            gs = g[:, sl]
            r = lax.rsqrt(jnp.mean(gs * gs, axis=-1, keepdims=True) + NORM_EPS)
            out_ref[:, sl] = (gs * r * w_ref[:, sl]).astype(BF16)
        out_ref[:, D_MODEL:] = o_ref[...].astype(BF16)

    tok = pl.BlockSpec((tm, D_MODEL), lambda i: (i, 0))
    return pl.pallas_call(
        body, name="gnorm_fwd", grid=(n // tm,),
        in_specs=[tok, tok, tok, pl.BlockSpec((1, D_MODEL), lambda i: (0, 0))],
        out_specs=pl.BlockSpec((tm, MIX_WIDTH), lambda i: (i, 0)),
        out_shape=jax.ShapeDtypeStruct((n, MIX_WIDTH), BF16),
        compiler_params=_params(("parallel",)),
    )(y, z, o_att, w)


def _gnorm_bwd(dycat, y, z, w):
    n = y.shape[0]
    tm = min(512, n)

    def body(dn_ref, y_ref, z_ref, w_ref, dy_ref, dz_ref, gw_ref):
        zv, yv = z_ref[...], y_ref[...]
        sz = _sigmoid(zv)
        sil = zv * sz
        g = yv * sil

        @pl.when(pl.program_id(0) == 0)
        def _():
            gw_ref[...] = jnp.zeros_like(gw_ref)

        for gi in range(2):
            sl = slice(GROUP_W * gi, GROUP_W * (gi + 1))
            gs = g[:, sl]
            r = lax.rsqrt(jnp.mean(gs * gs, axis=-1, keepdims=True) + NORM_EPS)
            nrm = gs * r
            dyn = dn_ref[:, sl]
            dn = dyn * w_ref[:, sl]
            dgs = r * (dn - nrm * jnp.mean(dn * nrm, axis=-1, keepdims=True))
            dy_ref[:, sl] = dgs * sil[:, sl]
            dz_ref[:, sl] = (dgs * yv[:, sl] * (sz[:, sl] * (1.0 + zv[:, sl] * (1.0 - sz[:, sl])))).astype(BF16)
            gw_ref[:, sl] += jnp.sum(dyn * nrm, axis=0, keepdims=True)

    tok = pl.BlockSpec((tm, D_MODEL), lambda i: (i, 0))
    row = pl.BlockSpec((1, D_MODEL), lambda i: (0, 0))
    return pl.pallas_call(
        body, name="gnorm_bwd", grid=(n // tm,),
        in_specs=[tok, tok, tok, row], out_specs=[tok, tok, row],
        out_shape=[jax.ShapeDtypeStruct((n, D_MODEL), F32), jax.ShapeDtypeStruct((n, D_MODEL), BF16),
                   jax.ShapeDtypeStruct((1, D_MODEL), F32)],
        compiler_params=_params(("arbitrary",)),
    )(dycat, y, z, w)


AUX_C = 3
AUX_ONE = 3


def _aux_base(hh):
    return HEAD_DIM * (1 - hh)


def _fox_prep(dtf, fb, bsz, seq):
    def body(x_ref, b_ref, aux_ref):
        tri = (_iota((CHUNK, CHUNK), 1) <= _iota((CHUNK, CHUNK), 0)).astype(F32)
        row, col = _iota((LANES, D_MODEL), 0), _iota((LANES, D_MODEL), 1)
        lane = jnp.bitwise_and(col, LANES - 1)
        other = (lane < HEAD_DIM).astype(jnp.int32)
        term = lane - HEAD_DIM * (1 - other)
        src = F_COL + 2 * jnp.right_shift(col, 7) + other
        place = [jnp.where((row == src) & (term == i), -1.0, 0.0).astype(BF16) for i in range(AUX_C)]
        lane1 = jnp.bitwise_and(_iota((1, D_MODEL), 1), LANES - 1)
        ones_lane = (lane1 - HEAD_DIM * (1 - (lane1 < HEAD_DIM).astype(jnp.int32)) == AUX_ONE).astype(F32)
        carry = jnp.zeros((1, LANES), F32)
        for j in range(seq // CHUNK):
            sl = slice(CHUNK * j, CHUNK * (j + 1))
            lf = _log_sigmoid(x_ref[sl, :] + b_ref[...])
            c = _sel_dot(tri, lf) + carry
            carry = carry + jnp.sum(lf, axis=0, keepdims=True)
            t1, t2, t3 = (jnp.dot(t, p, preferred_element_type=F32) for t, p in zip(_split3(c), place))
            aux_ref[sl, :] = (t1 + t2 + t3 + ones_lane).astype(BF16)

    return pl.pallas_call(
        body, name="fox_prep", grid=(bsz,),
        in_specs=[pl.BlockSpec((seq, LANES), lambda b: (b, 0)), pl.BlockSpec((1, LANES), lambda b: (0, 0))],
        out_specs=pl.BlockSpec((seq, D_MODEL), lambda b: (b, 0)),
        out_shape=jax.ShapeDtypeStruct((bsz * seq, D_MODEL), BF16),
        compiler_params=_params(("parallel",)),
    )(dtf, fb)


def _fox_prep_bwd(dck, dcq, ddt, dtf, fb, bsz, seq):
    nj = seq // CHUNK

    def body(dck_ref, dcq_ref, ddt_ref, x_ref, b_ref, out_ref, gb_ref):
        triu = (_iota((CHUNK, CHUNK), 0) <= _iota((CHUNK, CHUNK), 1)).astype(F32)
        is_f = (_iota((CHUNK, LANES), 1) >= F_COL) & (_iota((CHUNK, LANES), 1) < 2 * F_COL)
        carry = jnp.zeros((1, LANES), F32)
        total = jnp.zeros((1, LANES), F32)
        for j in range(nj - 1, -1, -1):
            sl = slice(CHUNK * j, CHUNK * (j + 1))
            dcv = dck_ref[sl, :] + dcq_ref[sl, :]
            dlf = _sel_dot(triu, dcv) + carry
            carry = carry + jnp.sum(dcv, axis=0, keepdims=True)
            df = jnp.where(is_f, dlf * _sigmoid(-(x_ref[sl, :] + b_ref[...])), 0.0)
            out_ref[sl, :] = (df + ddt_ref[sl, :]).astype(BF16)
            total = total + jnp.sum(df, axis=0, keepdims=True)

        @pl.when(pl.program_id(0) == 0)
        def _():
            gb_ref[...] = jnp.zeros_like(gb_ref)

        gb_ref[...] += jnp.broadcast_to(total, (SUBLANES, LANES))

    blk = pl.BlockSpec((seq, LANES), lambda b: (b, 0))
    return pl.pallas_call(
        body, name="fox_prep_bwd", grid=(bsz,),
        in_specs=[blk, blk, blk, blk, pl.BlockSpec((1, LANES), lambda b: (0, 0))],
        out_specs=[blk, pl.BlockSpec((SUBLANES, LANES), lambda b: (0, 0))],
        out_shape=[jax.ShapeDtypeStruct((bsz * seq, LANES), BF16), jax.ShapeDtypeStruct((SUBLANES, LANES), F32)],
        compiler_params=_params(("arbitrary",)),
    )(dck, dcq, ddt, dtf, fb)


ATT_BLOCK_FWD = 512
ATT_BLOCK_BWD = 256


def _att_operands(q_ref, k_ref, aux_ref, hh):
    lane = _iota((1, LANES), 1)
    lm = (lane < HEAD_DIM) if hh == 0 else (lane >= HEAD_DIM)
    base = _aux_base(hh)
    zero = jnp.zeros((1, LANES), BF16)
    ka = jnp.where(lm, k_ref[...], jnp.where((lane >= base) & (lane <= base + AUX_ONE), aux_ref[...], zero))
    qa = jnp.where(lm, q_ref[...] * ATT_SCALE,
                   jnp.where((lane >= base) & (lane < base + AUX_C), jnp.ones((1, LANES), BF16), zero))
    return lm, base, qa, ka


def _att_fwd(qkv, ccol, bsz, seq, gather):
    blk = min(ATT_BLOCK_FWD, seq)
    nb = seq // blk
    nx = len(gather)

    def body(*refs):
        q_ref, k_ref, v_ref, c_ref = refs[:4]
        x_refs = refs[4:4 + nx]
        o_ref, lse_ref = refs[4 + nx:6 + nx]
        xo_refs = refs[6 + nx:6 + 2 * nx]
        qa_scr, ka_scr, va_scr = refs[6 + 2 * nx:9 + 2 * nx]
        sems = refs[9 + 2 * nx:]
        pair = pl.program_id(1)

        @pl.when((pl.program_id(0) == 0) & (pair == 0))
        def _():
            _exchange_start("gather", x_refs, xo_refs, *sems)

        @pl.when(pair == 0)
        def _():
            lse_ref[...] = jnp.zeros_like(lse_ref)

        lane_b = _iota((blk, LANES), 1)
        causal = _iota((blk, blk), 0) >= _iota((blk, blk), 1)
        for hh in range(2):
            lm, _, qa, ka = _att_operands(q_ref, k_ref, c_ref, hh)
            qa_scr[hh] = qa
            ka_scr[hh] = ka
            va_scr[hh] = jnp.where(lm, v_ref[...], jnp.zeros((seq, LANES), BF16))

        def q_step(i, carry0):
            rows = pl.ds(pl.multiple_of(i * blk, blk), blk)

            def scores(j):
                krows = pl.ds(pl.multiple_of(j * blk, blk), blk)
                return tuple(_bdot_nt(qa_scr[hh, rows, :], ka_scr[hh, krows, :]) for hh in range(2))

            def update(state, s_pair, j, masked):
                krows = pl.ds(pl.multiple_of(j * blk, blk), blk)
                alphas, ls, pvs, ms = [], [], [], []
                for hh in range(2):
                    m, l, _ = state[hh]
                    s = jnp.where(causal, s_pair[hh], NEG) if masked else s_pair[hh]
                    mn = jnp.maximum(m, jnp.max(s, axis=1, keepdims=True))
                    alpha = jnp.exp(m - mn)
                    pr = jnp.exp(s - mn)
                    ms.append(mn)
                    alphas.append(alpha)
                    ls.append(alpha * l + jnp.sum(pr, axis=1, keepdims=True))
                    pvs.append(_bdot(pr, va_scr[hh, krows, :]))
                return tuple((ms[hh], ls[hh], alphas[hh] * state[hh][2] + pvs[hh]) for hh in range(2))

            def step(j, carry):
                state, s_cur = carry
                s_next = scores(j + 1)
                return update(state, s_cur, j, False), s_next

            one = (jnp.full((blk, 1), NEG, F32), jnp.zeros((blk, 1), F32), jnp.zeros((blk, LANES), F32))
            state, s_last = lax.fori_loop(0, i, step, ((one, one), scores(0)))
            (m0, l0, acc0), (m1, l1, acc1) = update(state, s_last, i, True)
            o_ref[rows, :] = acc0 * (1.0 / l0) + acc1 * (1.0 / l1)
            head0 = F_COL + 2 * pair
            lse_ref[rows, :] = jnp.where(lane_b == head0, m0 + jnp.log(l0),
                                         jnp.where(lane_b == head0 + 1, m1 + jnp.log(l1), lse_ref[rows, :]))
            return carry0

        lax.fori_loop(0, nb, q_step, 0)

        @pl.when((pl.program_id(0) == bsz - 1) & (pair == HEAD_PAIRS - 1))
        def _():
            _exchange_wait("gather", x_refs, xo_refs, *sems)

    col = lambda off: pl.BlockSpec((seq, LANES), lambda b, p: (b, off + p))
    head2 = pltpu.VMEM((2, seq, LANES), BF16)
    hbm = pl.BlockSpec(memory_space=pl.ANY)
    return pl.pallas_call(
        body, name="att_fwd", grid=(bsz, HEAD_PAIRS),
        in_specs=[col(0), col(HEAD_PAIRS), col(2 * HEAD_PAIRS), col(0)] + [hbm] * nx,
        out_specs=[pl.BlockSpec((seq, LANES), lambda b, p: (b, p)),
                   pl.BlockSpec((seq, LANES), lambda b, p: (b, 0))] + [hbm] * nx,
        out_shape=[jax.ShapeDtypeStruct((bsz * seq, D_MODEL), F32), jax.ShapeDtypeStruct((bsz * seq, LANES), F32)]
        + _exchange_shapes("gather", gather),
        scratch_shapes=[head2, head2, head2] + _exchange_scratch(nx),
        compiler_params=_params(("arbitrary", "arbitrary")),
    )(qkv, qkv, qkv, ccol, *gather)


def _att_bwd(qkv, dycat, o, lse, ccol, bsz, seq, scatter):
    blk = min(ATT_BLOCK_BWD, seq)
    nb = seq // blk
    nx = len(scatter)

    def body(*refs):
        q_ref, k_ref, v_ref, do_ref, o_ref, lse_ref, c_ref = refs[:7]
        x_refs = refs[7:7 + nx]
        dq_ref, dk_ref, dv_ref, dck_ref, dcq_ref = refs[7 + nx:12 + nx]
        xo_refs = refs[12 + nx:12 + 2 * nx]
        qa_scr, ka_scr, va_scr, doa_scr, kat_scr, d_scr, dqt_scr, dk_scr, dv_scr = refs[12 + 2 * nx:21 + 2 * nx]
        sems = refs[21 + 2 * nx:]
        pair = pl.program_id(1)

        @pl.when((pl.program_id(0) == 0) & (pair == 0))
        def _():
            _exchange_start("scatter", x_refs, xo_refs, *sems)

        causal_t = _iota((blk, blk), 1) >= _iota((blk, blk), 0)
        lane_b = _iota((blk, LANES), 1)
        row_t = _iota((LANES, seq), 0)
        masks, bases = [], []
        ones8 = jnp.ones((SUBLANES, LANES), F32)
        for hh in range(2):
            lm, base, qa, ka = _att_operands(q_ref, k_ref, c_ref, hh)
            masks.append(lm)
            bases.append(base)
            qa_scr[hh] = qa
            ka_scr[hh] = ka
            va_scr[hh] = jnp.where(lm, v_ref[...], jnp.zeros((seq, LANES), BF16))
            doa = jnp.where(lm, do_ref[...], 0.0).astype(BF16)
            doa_scr[hh] = doa
            for t0 in range(0, seq, blk):
                kat_scr[hh, :, t0:t0 + blk] = ka[t0:t0 + blk, :].astype(F32).T.astype(BF16)
            d1, d2, d3 = (_bdot_nt(ones8, term) for term in _split3(doa.astype(F32) * o_ref[...]))
            d_scr[hh] = d1 + d2 + d3
        dqt_scr[...] = jnp.zeros_like(dqt_scr)

        @pl.when(pair == 0)
        def _():
            dck_ref[...] = jnp.zeros_like(dck_ref)

        def kv_step(j, carry0):
            krows = pl.ds(pl.multiple_of(j * blk, blk), blk)
            dk_scr[...] = jnp.zeros_like(dk_scr)
            dv_scr[...] = jnp.zeros_like(dv_scr)

            def scores(i):
                rows = pl.ds(pl.multiple_of(i * blk, blk), blk)
                return tuple((_bdot_nt(ka_scr[hh, krows, :], qa_scr[hh, rows, :]),
                              _bdot_nt(va_scr[hh, krows, :], doa_scr[hh, rows, :])) for hh in range(2))

            def update(i, sd, masked):
                rows = pl.ds(pl.multiple_of(i * blk, blk), blk)
                for hh in range(2):
                    st, dpt = sd[hh]
                    if masked:
                        st = jnp.where(causal_t, st, NEG)
                    pt = jnp.exp(st - lse_ref[0, 0, hh:hh + 1, rows])
                    dst = (pt * (dpt - d_scr[hh, 0:1, rows])).astype(BF16)
                    dv_scr[hh] += _bdot(pt, doa_scr[hh, rows, :])
                    dk_scr[hh] += _bdot(dst, qa_scr[hh, rows, :])
                    dqt_scr[hh, :, rows] += _bdot(kat_scr[hh, :, krows], dst)

            def q_step(i, sd_cur):
                sd_next = scores(jnp.minimum(i + 1, nb - 1))
                update(i, sd_cur, False)
                return sd_next

            sd_diag = scores(j)
            sd_first = scores(jnp.minimum(j + 1, nb - 1))
            update(j, sd_diag, True)
            lax.fori_loop(j + 1, nb, q_step, sd_first)
            lmb0 = lane_b < HEAD_DIM
            dk_ref[krows, :] = jnp.where(lmb0, dk_scr[0], dk_scr[1]).astype(BF16)
            dv_ref[krows, :] = (dv_scr[0] + dv_scr[1]).astype(BF16)
            sums = [jnp.sum(jnp.where(lane_b == bases[hh], dk_scr[hh], 0.0), axis=1, keepdims=True) for hh in range(2)]
            head0 = F_COL + 2 * pair
            dck_ref[krows, :] += jnp.where(lane_b == head0, -sums[0], jnp.where(lane_b == head0 + 1, -sums[1], 0.0))
            return carry0

        lax.fori_loop(0, nb, kv_step, 0)
        for t0 in range(0, seq, blk):
            dq0, dq1 = dqt_scr[0, :, t0:t0 + blk].T, dqt_scr[1, :, t0:t0 + blk].T
            dq_ref[t0:t0 + blk, :] = (jnp.where(lane_b < HEAD_DIM, dq0, dq1) * ATT_SCALE).astype(BF16)
        dcq_ref[...] = jnp.zeros_like(dcq_ref)
        for hh in range(2):
            dcq_ref[0, 0, hh:hh + 1, :] = jnp.sum(jnp.where(row_t == bases[hh] + AUX_ONE, dqt_scr[hh], 0.0),
                                                   axis=0, keepdims=True)

        @pl.when((pl.program_id(0) == bsz - 1) & (pair == HEAD_PAIRS - 1))
        def _():
            _exchange_wait("scatter", x_refs, xo_refs, *sems)

    col = lambda off: pl.BlockSpec((seq, LANES), lambda b, p: (b, off + p))
    rowvec = pl.BlockSpec((1, 1, SUBLANES, seq), lambda b, p: (b, p, 0, 0))
    out = jax.ShapeDtypeStruct((bsz * seq, D_MODEL), BF16)
    rows_shape = jax.ShapeDtypeStruct((bsz, HEAD_PAIRS, SUBLANES, seq), F32)
    head2 = pltpu.VMEM((2, seq, LANES), BF16)
    hbm = pl.BlockSpec(memory_space=pl.ANY)
    return pl.pallas_call(
        body, name="att_bwd", grid=(bsz, HEAD_PAIRS),
        in_specs=[col(0), col(HEAD_PAIRS), col(2 * HEAD_PAIRS), col(HEAD_PAIRS), col(0), rowvec, col(0)] + [hbm] * nx,
        out_specs=[col(0), col(0), col(0), pl.BlockSpec((seq, LANES), lambda b, p: (b, 0)), rowvec] + [hbm] * nx,
        out_shape=[out, out, out, jax.ShapeDtypeStruct((bsz * seq, LANES), F32), rows_shape]
        + _exchange_shapes("scatter", scatter),
        scratch_shapes=[head2, head2, head2, head2, pltpu.VMEM((2, LANES, seq), BF16),
                        pltpu.VMEM((2, SUBLANES, seq), F32), pltpu.VMEM((2, LANES, seq), F32),
                        pltpu.VMEM((2, blk, LANES), F32), pltpu.VMEM((2, blk, LANES), F32)] + _exchange_scratch(nx),
        compiler_params=_params(("arbitrary", "arbitrary")),
    )(qkv, qkv, qkv, dycat, o, lse, ccol, *scatter)


def _adamw_math(w, g, m, v):
    m = ADAM_B1 * m + (1.0 - ADAM_B1) * g
    v = ADAM_B2 * v + (1.0 - ADAM_B2) * jnp.square(g)
    m_hat = m / ADAM_C1
    v_hat = v / ADAM_C2
    delta = -ADAM_LR * (m_hat / (jnp.sqrt(v_hat) + ADAM_EPS) + ADAM_WD * w)
    return delta, m, v


def _row_tile(rows):
    for tr in (256, 128, 64, 32, 16, 8):
        if rows % tr == 0:
            return tr
    return rows


def _sum_parts(parts, name):
    nparts, rows, cols = parts.shape
    tr = rows if rows % SUBLANES else _row_tile(rows)

    def body(p_ref, o_ref):
        g = p_ref[0].astype(F32)
        for s in range(1, nparts):
            g = g + p_ref[s].astype(F32)
        o_ref[...] = g

    return pl.pallas_call(
        body, name=name, grid=(rows // tr,),
        in_specs=[pl.BlockSpec((nparts, tr, cols), lambda i: (0, i, 0))],
        out_specs=pl.BlockSpec((tr, cols), lambda i: (i, 0)),
        out_shape=jax.ShapeDtypeStruct((rows, cols), F32),
        compiler_params=_params(("parallel",)),
    )(parts)


def _adamw_parts(parts, w, m, v, name):
    nparts, rows, cols = parts.shape
    tr = _row_tile(rows)

    def body(p_ref, w_ref, m_ref, v_ref, g_ref, d_ref, mo_ref, vo_ref):
        g = p_ref[0].astype(F32)
        for s in range(1, nparts):
            g = g + p_ref[s].astype(F32)
        delta, mn, vn = _adamw_math(w_ref[...], g, m_ref[...], v_ref[...])
        g_ref[...] = g
        d_ref[...] = delta
        mo_ref[...] = mn
        vo_ref[...] = vn

    blk = pl.BlockSpec((tr, cols), lambda i: (i, 0))
    shp = jax.ShapeDtypeStruct((rows, cols), F32)
    return pl.pallas_call(
        body, name=name, grid=(rows // tr,),
        in_specs=[pl.BlockSpec((nparts, tr, cols), lambda i: (0, i, 0)), blk, blk, blk],
        out_specs=[blk, blk, blk, blk], out_shape=[shp, shp, shp, shp],
        compiler_params=_params(("parallel",)),
    )(parts, w, m, v)


def _me():
    return lax.axis_index("x"), lax.axis_index("y"), lax.axis_index("c")


def _flip(pos, k):
    x, y, c = pos
    kx, ky, kc = (k >> 2) & 1, (k >> 1) & 1, k & 1
    return (x ^ kx if kx else x, y ^ ky if ky else y, c ^ kc if kc else c)


def _logical(pos):
    return 4 * pos[0] + 2 * pos[1] + pos[2]


def _all_gather_two_level(shards):
    narr = len(shards)

    def body(*refs):
        x_refs, out_refs = refs[:narr], refs[narr:2 * narr]
        send_sems, recv_sems, local_sems = refs[2 * narr:]
        x, y, c = _me()
        me, sibling = (x, y, c), (x, y, 1 - c)
        chips = [(1 - x, y), (x, 1 - y), (1 - x, 1 - y)]

        def copy(a, k, block, to, src=None):
            slot = out_refs[a].at[_logical(block)]
            return pltpu.make_async_remote_copy(
                src_ref=slot if src is None else src, dst_ref=slot,
                send_sem=send_sems.at[a, k], recv_sem=recv_sems.at[a, k], device_id=to, device_id_type=MESH)

        mine = [pltpu.make_async_copy(x_refs[a], out_refs[a].at[_logical(me)], local_sems.at[a]) for a in range(narr)]
        for cp in mine:
            cp.start()
        first = []
        for a in range(narr):
            first.append(copy(a, 0, me, sibling, src=x_refs[a]))
            first += [copy(a, 1 + j, me, (*chip, c), src=x_refs[a]) for j, chip in enumerate(chips)]
        for cp in first:
            cp.start()
        passed = []
        for a in range(narr):
            for j, chip in enumerate(chips):
                copy(a, 1 + j, (*chip, c), me).wait_recv()
                fwd = copy(a, 4 + j, (*chip, c), sibling)
                fwd.start()
                passed.append(fwd)
        for a in range(narr):
            copy(a, 0, sibling, me).wait_recv()
            for j, chip in enumerate(chips):
                copy(a, 4 + j, (*chip, 1 - c), me).wait_recv()
        for cp in first + passed:
            cp.wait_send()
        for cp in mine:
            cp.wait()

    hbm = pl.BlockSpec(memory_space=pl.ANY)
    return pl.pallas_call(
        body, name="all_gather_big",
        out_shape=[jax.ShapeDtypeStruct((N_DEV,) + s.shape, s.dtype) for s in shards],
        in_specs=[hbm] * narr, out_specs=[hbm] * narr,
        scratch_shapes=[pltpu.SemaphoreType.DMA((narr, 7)), pltpu.SemaphoreType.DMA((narr, 7)),
                        pltpu.SemaphoreType.DMA((narr,))],
    )(*shards)


def _exchange_copies(kind, x_refs, out_refs, send_sems, recv_sems, local_sems):
    me = _me()
    mine = _logical(me)
    local, remote = [], []
    for a, (x_ref, out_ref) in enumerate(zip(x_refs, out_refs)):
        own = x_ref if kind == "gather" else x_ref.at[mine]
        local.append(pltpu.make_async_copy(own, out_ref.at[mine], local_sems.at[a]))
        for k in range(1, N_DEV):
            peer = _flip(me, k)
            src = x_ref if kind == "gather" else x_ref.at[_logical(peer)]
            remote.append(pltpu.make_async_remote_copy(
                src_ref=src, dst_ref=out_ref.at[mine], send_sem=send_sems.at[a, k - 1], recv_sem=recv_sems.at[a, k - 1],
                device_id=peer, device_id_type=MESH))
    return local, remote


def _exchange_start(kind, x_refs, out_refs, *sems):
    if not x_refs:
        return
    local, remote = _exchange_copies(kind, x_refs, out_refs, *sems)
    for cp in local + remote:
        cp.start()


def _exchange_wait(kind, x_refs, out_refs, *sems):
    if not x_refs:
        return
    local, remote = _exchange_copies(kind, x_refs, out_refs, *sems)
    for cp in remote:
        cp.wait_recv()
    for cp in remote:
        cp.wait_send()
    for cp in local:
        cp.wait()


def _exchange_shapes(kind, arrays):
    return [jax.ShapeDtypeStruct(((N_DEV,) if kind == "gather" else ()) + a.shape, a.dtype) for a in arrays]


def _exchange_scratch(narrays):
    if not narrays:
        return []
    return [pltpu.SemaphoreType.DMA((narrays, N_DEV - 1)), pltpu.SemaphoreType.DMA((narrays, N_DEV - 1)),
            pltpu.SemaphoreType.DMA((narrays,))]


def _exchange_rows(x_ref, buf_ref, send_sems, recv_sems):
    me = _me()
    buf_ref[_logical(me)] = x_ref[...]
    copies = []
    for k in range(1, N_DEV):
        peer = _flip(me, k)
        copies.append(pltpu.make_async_remote_copy(
            src_ref=x_ref, dst_ref=buf_ref.at[_logical(me)],
            send_sem=send_sems.at[k - 1], recv_sem=recv_sems.at[k - 1], device_id=peer, device_id_type=MESH))
    for cp in copies:
        cp.start()
    for cp in copies:
        cp.wait_recv()
    for cp in copies:
        cp.wait_send()


def _sum_slots(buf_ref):
    total = buf_ref[0]
    for s in range(1, N_DEV):
        total = total + buf_ref[s]
    return total


def _small_gather(x):
    def body(x_ref, o_ref, buf_ref, send_sems, recv_sems):
        _exchange_rows(x_ref, buf_ref, send_sems, recv_sems)
        o_ref[...] = _sum_slots(buf_ref)

    return pl.pallas_call(
        body, name="small_gather", out_shape=jax.ShapeDtypeStruct(x.shape, F32),
        in_specs=[pl.BlockSpec(memory_space=pltpu.VMEM)], out_specs=pl.BlockSpec(memory_space=pltpu.VMEM),
        scratch_shapes=[pltpu.VMEM((N_DEV,) + x.shape, F32), pltpu.SemaphoreType.DMA((7,)), pltpu.SemaphoreType.DMA((7,))],
    )(x)


def _small_reduce_adamw(g, w, m, v):
    def body(g_ref, w_ref, m_ref, v_ref, go_ref, d_ref, mo_ref, vo_ref, buf_ref, send_sems, recv_sems):
        _exchange_rows(g_ref, buf_ref, send_sems, recv_sems)
        gs = _sum_slots(buf_ref)
        delta, mn, vn = _adamw_math(w_ref[...], gs, m_ref[...], v_ref[...])
        go_ref[...] = gs
        d_ref[...] = delta
        mo_ref[...] = mn
        vo_ref[...] = vn

    vm = pl.BlockSpec(memory_space=pltpu.VMEM)
    shp = jax.ShapeDtypeStruct(g.shape, F32)
    return pl.pallas_call(
        body, name="small_reduce_adamw", out_shape=[shp, shp, shp, shp],
        in_specs=[vm, vm, vm, vm], out_specs=[vm, vm, vm, vm],
        scratch_shapes=[pltpu.VMEM((N_DEV,) + g.shape, F32), pltpu.SemaphoreType.DMA((7,)), pltpu.SemaphoreType.DMA((7,))],
    )(g, w, m, v)


def _pad_cols(a, width):
    return jnp.pad(a, ((0, 0), (0, width - a.shape[1])))


def _local_step(x, target, norm_mix_w, w_in_t, conv_w, conv_b, dt_bias, a_log, d_skip, ssd_norm_w, f_bias,
                late_shards, norm_mlp_w, norm_final_w):
    bsz, seq, _ = x.shape
    n = bsz * seq
    x2 = x.reshape(n, D_MODEL)
    t2 = target.reshape(n, D_MODEL)
    nfw = norm_final_w.reshape(1, D_MODEL)

    bias = _pad_cols(dt_bias, LANES)
    arow = _pad_cols(-jnp.exp(a_log), LANES)
    biast, acol = bias.reshape(LANES, 1), arow.reshape(LANES, 1)
    dfull = jnp.repeat(d_skip, HEAD_DIM, axis=1)
    fb = jnp.pad(f_bias, ((0, 0), (F_COL, LANES - 2 * F_COL)))

    wt_z, wt_xbc, wt_qkv = w_in_t[:ROW_XBC], w_in_t[ROW_XBC:ROW_DT], w_in_t[ROW_Q:ROW_F]
    wt_dtf = jnp.concatenate([w_in_t[ROW_DT:ROW_Q], w_in_t[ROW_F:], jnp.zeros((LANES - 2 * F_COL, D_MODEL), BF16)], axis=0)
    wt_q, wt_k, wt_v = wt_qkv[:D_MODEL], wt_qkv[D_MODEL:2 * D_MODEL], wt_qkv[2 * D_MODEL:]

    h1 = _rms_fwd(x2, norm_mix_w, "rms_fwd_mix")
    z = _mm([(h1, wt_z)], "inproj_z", F32, 1024, 1024, nt=True)
    xbc_raw = _mm([(h1, wt_xbc)], "inproj_xbc", F32, 512, 1536, nt=True)
    qkv = _mm([(h1, wt_qkv)], "inproj_qkv", BF16, 512, 3072, nt=True)
    dtf = _mm([(h1, wt_dtf)], "inproj_dtf", F32, 2048, LANES, nt=True)
    dtft = dtf.reshape(bsz, seq, LANES).transpose(0, 2, 1)

    xbc = _conv_fwd(xbc_raw, conv_w, conv_b, bsz, seq)
    y_pre, hall = _ssd_fwd(xbc, dtf, dtft, bias, biast, arow, acol, dfull, bsz, seq)

    ccol = _fox_prep(dtf, fb, bsz, seq)
    o_att, lse, w_out, w_up_t, w_down = _att_fwd(qkv, ccol, bsz, seq, late_shards)
    w_out, w_up_t, w_down = (w.reshape(-1, D_MODEL) for w in (w_out, w_up_t, w_down))

    ycat = _gnorm_fwd(y_pre, z, o_att, ssd_norm_w)
    h2, h2n = _mm([(ycat, w_out)], "outproj", F32, 512, 1024, res=x2, rms_fwd=norm_mlp_w)
    pre = _mm([(h2n, w_up_t)], "mlp_up", F32, 256, 4096, nt=True)

    dh3, loss_row, g_nfw = _mlp_down_loss(pre, w_down, h2, t2, nfw)
    dpre, u = _mlp_down_bwd(dh3, w_down, pre)
    g_w_down = _mm_tn(u, dh3, "grad_w_down", 1024, 1024, 2048)
    g_w_up_t = _mm_tn(dpre, h2n, "grad_w_up", 1024, 1024, 2048)
    by_shard = lambda g: g.reshape(N_DEV, g.shape[0] // N_DEV, D_MODEL)
    dh2, g_nmlp = _mm([(dpre, w_up_t)], "mlp_up_bwd", F32, 512, 1024, res=dh3, rms_bwd=(h2, norm_mlp_w))

    g_w_out = _mm_tn(ycat, dh2, "grad_w_out", 1024, 1024, 2048)
    dycat = _mm([(dh2, w_out)], "outproj_bwd", F32, 512, 2048, nt=True)
    dy_pre, dz, g_ssdn = _gnorm_bwd(dycat, y_pre, z, ssd_norm_w)
    lse_rows = lse.reshape(bsz, seq, LANES)[:, :, F_COL:2 * F_COL].transpose(0, 2, 1).reshape(bsz, HEAD_PAIRS, 2, seq)
    lse_rows = jnp.pad(lse_rows, ((0, 0), (0, 0), (0, SUBLANES - 2), (0, 0)))
    dq, dk, dv, dck, dcq, recv_down, recv_up_t, recv_out = _att_bwd(
        qkv, dycat, o_att, lse_rows, ccol, bsz, seq, [by_shard(g_w_down), by_shard(g_w_up_t), by_shard(g_w_out)])

    dxbc, ddt, ssd_acc = _ssd_bwd(dy_pre, xbc, dtf, dtft, bias, biast, arow, acol, dfull, hall, bsz, seq)
    dxbc_raw, g_cw, g_cb = _conv_bwd(dxbc, xbc_raw, conv_w, conv_b, bsz, seq)

    def head_cols(rows):
        cols = rows[:, :, :2, :].reshape(bsz, SSD_HEADS, seq).transpose(0, 2, 1).reshape(n, SSD_HEADS)
        return jnp.pad(cols, ((0, 0), (F_COL, LANES - 2 * F_COL)))

    ddtf, g_fb = _fox_prep_bwd(dck, head_cols(dcq), ddt, dtf, fb, bsz, seq)

    g_dtf = _mm_tn(ddtf, h1, "grad_w_in_dtf", LANES, 1024, 2048)
    g_w_in_t = jnp.concatenate([
        _mm_tn(dz, h1, "grad_w_in_z", 1024, 1024, 2048), _mm_tn(dxbc_raw, h1, "grad_w_in_xbc", 768, 1024, 2048),
        g_dtf[:F_COL], _mm_tn(dq, h1, "grad_w_in_q", 1024, 1024, 2048), _mm_tn(dk, h1, "grad_w_in_k", 1024, 1024, 2048),
        _mm_tn(dv, h1, "grad_w_in_v", 1024, 1024, 2048), g_dtf[F_COL:2 * F_COL]], axis=0)
    pieces = [(dz, wt_z), (dxbc_raw, wt_xbc), (dq, wt_q), (dk, wt_k), (dv, wt_v), (ddtf, wt_dtf)]
    dx, g_nmix, recv_in_t = _mm(pieces, "inproj_bwd", F32, 256, 1024, res=dh2, rms_bwd=(x2, norm_mix_w),
                                exchange=("scatter", [by_shard(g_w_in_t)]))

    small = dict(
        norm_mix_w=g_nmix, norm_mlp_w=g_nmlp, norm_final_w=g_nfw, ssd_norm_w=g_ssdn, conv_b=g_cb, conv_w=g_cw,
        dt_bias=ssd_acc[16:17, :SSD_HEADS], a_log=ssd_acc[0:1, :SSD_HEADS], d_skip=ssd_acc[8:9, :SSD_HEADS],
        f_bias=g_fb[0:1, F_COL:2 * F_COL])
    recv = dict(w_in_t=recv_in_t, w_out=recv_out, w_up_t=recv_up_t, w_down=recv_down)
    return loss_row[0, 0], dx.reshape(bsz, seq, D_MODEL), small, recv


SMALL_LAYOUT = (("norm_mix_w", 0, 1, 0, D_MODEL), ("norm_mlp_w", 1, 1, 0, D_MODEL), ("norm_final_w", 2, 1, 0, D_MODEL),
                ("ssd_norm_w", 3, 1, 0, D_MODEL), ("conv_b", 4, 1, 0, CONV_CH), ("conv_w", 5, CONV_K, 0, CONV_CH),
                ("dt_bias", 9, 1, 0, SSD_HEADS), ("a_log", 9, 1, LANES, SSD_HEADS), ("d_skip", 9, 1, 2 * LANES, SSD_HEADS),
                ("f_bias", 9, 1, 3 * LANES, SSD_HEADS))
SMALL_ROWS = 2 * SUBLANES


def _pack_small(vals):
    packed = jnp.zeros((SMALL_ROWS, SMALL_COLS), F32)
    for name, r0, nrows, c0, width in SMALL_LAYOUT:
        packed = packed.at[r0:r0 + nrows, c0:c0 + width].set(vals[name].reshape(nrows, width))
    return packed


def _unpack_small(packed, like):
    out = {}
    for name, r0, nrows, c0, width in SMALL_LAYOUT:
        out[name] = packed[r0:r0 + nrows, c0:c0 + width].reshape(like[name].shape)
    return out


def kernel(x, norm_mix_w, w_in, conv_w, conv_b, dt_bias, a_log, d_skip, ssd_norm_w, f_bias, w_out, norm_mlp_w, w_up, w_down, norm_final_w, loss_target, m_norm_mix_w, m_w_in, m_conv_w, m_conv_b, m_dt_bias, m_a_log, m_d_skip, m_ssd_norm_w, m_f_bias, m_w_out, m_norm_mlp_w, m_w_up, m_w_down, m_norm_final_w, v_norm_mix_w, v_w_in, v_conv_w, v_conv_b, v_dt_bias, v_a_log, v_d_skip, v_ssd_norm_w, v_f_bias, v_w_out, v_norm_mlp_w, v_w_up, v_w_down, v_norm_final_w):
    me = 4 * lax.axis_index("x") + 2 * lax.axis_index("y") + lax.axis_index("c")
    conv_shard_w = CONV_CH // N_DEV
    zero = jnp.zeros((), jnp.int32)

    def place_conv(shard):
        return lax.dynamic_update_slice(jnp.zeros((CONV_K, CONV_CH), F32), shard[0], (zero, me * conv_shard_w))

    (g_in_t,) = _all_gather_two_level([w_in[0].T.astype(BF16)])
    late_shards = [w_out[0].astype(BF16), w_up[0].T.astype(BF16), w_down[0].astype(BF16)]
    conv_full = _small_gather(jnp.pad(place_conv(conv_w), ((0, SUBLANES - CONV_K), (0, 0))))[:CONV_K]

    loss_local, grad_x, g_small, recv = _local_step(
        x, loss_target, norm_mix_w, g_in_t.reshape(IN_WIDTH, D_MODEL), conv_full, conv_b, dt_bias, a_log, d_skip,
        ssd_norm_w, f_bias, late_shards, norm_mlp_w, norm_final_w)
    loss = lax.psum(loss_local, MESH_AXES)

    grad_in = _sum_parts(recv["w_in_t"], "sum_w_in").T[None]
    grad_up = _sum_parts(recv["w_up_t"], "sum_w_up").T[None]
    big = {
        "w_in": _adamw_parts(grad_in, w_in[0], m_w_in[0], v_w_in[0], "adamw_w_in"),
        "w_out": _adamw_parts(recv["w_out"], w_out[0], m_w_out[0], v_w_out[0], "adamw_w_out"),
        "w_up": _adamw_parts(grad_up, w_up[0], m_w_up[0], v_w_up[0], "adamw_w_up"),
        "w_down": _adamw_parts(recv["w_down"], w_down[0], m_w_down[0], v_w_down[0], "adamw_w_down"),
    }

    like = dict(norm_mix_w=norm_mix_w, norm_mlp_w=norm_mlp_w, norm_final_w=norm_final_w, ssd_norm_w=ssd_norm_w,
                conv_b=conv_b, conv_w=jnp.zeros((1, CONV_K, CONV_CH), F32), dt_bias=dt_bias, a_log=a_log,
                d_skip=d_skip, f_bias=f_bias)
    w_small = dict(like, conv_w=place_conv(conv_w))
    m_small = dict(norm_mix_w=m_norm_mix_w, norm_mlp_w=m_norm_mlp_w, norm_final_w=m_norm_final_w,
                   ssd_norm_w=m_ssd_norm_w, conv_b=m_conv_b, conv_w=place_conv(m_conv_w), dt_bias=m_dt_bias,
                   a_log=m_a_log, d_skip=m_d_skip, f_bias=m_f_bias)
    v_small = dict(norm_mix_w=v_norm_mix_w, norm_mlp_w=v_norm_mlp_w, norm_final_w=v_norm_final_w,
                   ssd_norm_w=v_ssd_norm_w, conv_b=v_conv_b, conv_w=place_conv(v_conv_w), dt_bias=v_dt_bias,
                   a_log=v_a_log, d_skip=v_d_skip, f_bias=v_f_bias)
    small = _small_reduce_adamw(_pack_small(g_small), _pack_small(w_small), _pack_small(m_small), _pack_small(v_small))
    small = [_unpack_small(s, like) for s in small]
    for s in small:
        s["conv_w"] = lax.dynamic_slice(s["conv_w"], (zero, zero, me * conv_shard_w), (1, CONV_K, conv_shard_w))

    order = ["norm_mix_w", "w_in", "conv_w", "conv_b", "dt_bias", "a_log", "d_skip", "ssd_norm_w", "f_bias", "w_out",
             "norm_mlp_w", "w_up", "w_down", "norm_final_w"]
    outs = [loss, grad_x]
    for kind in range(4):
        for name in order:
            outs.append(big[name][kind][None] if name in big else small[kind][name])
    return tuple(outs)
```

```python
import jax
import jax.numpy as jnp
from jax import lax
from jax.experimental import pallas as pl
from jax.experimental.pallas import tpu as pltpu

F32, BF16 = jnp.float32, jnp.bfloat16
HI = lax.Precision.HIGHEST

D_MODEL = 1024
SSD_HEADS = 16
HEAD_DIM = 64
SSD_STATE = 128
CHUNK = 128
CONV_CH = 1536
CONV_K = 4
D_FF = 4096
MIX_WIDTH = 2048
IN_WIDTH = 5664
NORM_EPS = 1e-5
ATT_SCALE = 0.125

LANES = 128
SUBLANES = 8
HEAD_PAIRS = SSD_HEADS // 2
NEG = -1e30
VMEM_LIMIT = 52 * 1024 * 1024

ROW_XBC, ROW_DT, ROW_Q, ROW_F = 1024, 2560, 2576, 5648
F_COL = SSD_HEADS

N_DEV = 8
MESH_AXES = ("x", "y", "c")
MESH = pl.DeviceIdType.MESH

ADAM_LR, ADAM_B1, ADAM_B2, ADAM_EPS, ADAM_WD, ADAM_STEP = 0.001, 0.9, 0.999, 1e-08, 0.01, 10
ADAM_C1 = 1.0 - ADAM_B1 ** ADAM_STEP
ADAM_C2 = 1.0 - ADAM_B2 ** ADAM_STEP

SMALL_COLS = CONV_CH


def _params(sem=None):
    return pltpu.CompilerParams(dimension_semantics=sem, vmem_limit_bytes=VMEM_LIMIT)


def _sigmoid(u):
    return 1.0 / (1.0 + jnp.exp(-u))


def _softplus(u):
    return jnp.maximum(u, 0.0) + jnp.log(1.0 + jnp.exp(-jnp.abs(u)))


def _log_sigmoid(u):
    return jnp.minimum(u, 0.0) - jnp.log(1.0 + jnp.exp(-jnp.abs(u)))


def _bdot(a, b):
    return jnp.dot(a.astype(BF16), b.astype(BF16), preferred_element_type=F32)


def _bdot_nt(a, b):
    return lax.dot_general(a.astype(BF16), b.astype(BF16), (((1,), (1,)), ((), ())), preferred_element_type=F32)


def _bdot_tn(a, b):
    return lax.dot_general(a.astype(BF16), b.astype(BF16), (((0,), (0,)), ((), ())), preferred_element_type=F32)


def _split3(x):
    x1 = x.astype(BF16)
    r1 = x - x1.astype(F32)
    x2 = r1.astype(BF16)
    return x1, x2, (r1 - x2.astype(F32)).astype(BF16)


def _dot_sel(x, sel):
    s = sel.astype(BF16)
    p1, p2, p3 = (jnp.dot(p, s, preferred_element_type=F32) for p in _split3(x))
    return p1 + p2 + p3


def _sel_dot(sel, x):
    s = sel.astype(BF16)
    p1, p2, p3 = (jnp.dot(s, p, preferred_element_type=F32) for p in _split3(x))
    return p1 + p2 + p3


def _iota(shape, dim):
    return lax.broadcasted_iota(jnp.int32, shape, dim)


def _head_expand():
    return (jnp.right_shift(_iota((LANES, D_MODEL), 1), 6) == _iota((LANES, D_MODEL), 0)).astype(F32)


def _head_reduce():
    return (jnp.right_shift(_iota((D_MODEL, LANES), 0), 6) == _iota((D_MODEL, LANES), 1)).astype(F32)


def _rms_fwd(x, w, name):
    n, d = x.shape
    tm = min(1024, n)

    def body(x_ref, w_ref, o_ref):
        xv = x_ref[...]
        r = lax.rsqrt(jnp.mean(xv * xv, axis=-1, keepdims=True) + NORM_EPS)
        o_ref[...] = (xv * r * w_ref[...]).astype(BF16)

    return pl.pallas_call(
        body, name=name, grid=(n // tm,),
        in_specs=[pl.BlockSpec((tm, d), lambda i: (i, 0)), pl.BlockSpec((1, d), lambda i: (0, 0))],
        out_specs=pl.BlockSpec((tm, d), lambda i: (i, 0)),
        out_shape=jax.ShapeDtypeStruct((n, d), BF16),
        compiler_params=_params(("parallel",)),
    )(x, w)


def _mm(pairs, name, out_dtype, tm, tn, nt=False, res=None, exchange=None, rms_fwd=None, rms_bwd=None):
    m = pairs[0][0].shape[0]
    n = pairs[0][1].shape[0 if nt else 1]
    tm, tn = min(tm, m), min(tn, n)
    gm, gn = m // tm, n // tn
    npairs = len(pairs)
    kind, xs = (None, []) if exchange is None else exchange
    nx = len(xs)
    extra_in = [] if res is None else [res]
    if rms_bwd is not None:
        extra_in += list(rms_bwd)
    elif rms_fwd is not None:
        extra_in.append(rms_fwd)
    n_in = 2 * npairs + len(extra_in)
    n_out = 1 + (rms_fwd is not None or rms_bwd is not None)
    assert (rms_fwd is None and rms_bwd is None) or tn == n

    def body(*refs):
        x_refs, o_refs = refs[n_in:n_in + nx], refs[n_in + nx:n_in + nx + n_out]
        xo_refs, sems = refs[n_in + nx + n_out:n_in + 2 * nx + n_out], refs[n_in + 2 * nx + n_out:]
        extra = refs[2 * npairs:n_in]
        i, j = pl.program_id(0), pl.program_id(1)
        if nx:
            @pl.when((i == 0) & (j == 0))
            def _():
                _exchange_start(kind, x_refs, xo_refs, *sems)

        acc = None
        for p in range(npairs):
            a_v, b_v = refs[2 * p][...], refs[2 * p + 1][...]
            d = _bdot_nt(a_v, b_v) if nt else _bdot(a_v, b_v)
            acc = d if acc is None else acc + d
        if rms_bwd is not None:
            xv = extra[1][...]
            r = lax.rsqrt(jnp.mean(xv * xv, axis=-1, keepdims=True) + NORM_EPS)
            nrm = xv * r
            g = acc * extra[2][...]
            o_refs[0][...] = extra[0][...] + r * (g - nrm * jnp.mean(g * nrm, axis=-1, keepdims=True))

            @pl.when(i == 0)
            def _():
                o_refs[1][...] = jnp.zeros_like(o_refs[1])

            o_refs[1][...] += jnp.sum(acc * nrm, axis=0, keepdims=True)
        else:
            if res is not None:
                acc = extra[0][...] + acc
            o_refs[0][...] = acc.astype(o_refs[0].dtype)
            if rms_fwd is not None:
                r = lax.rsqrt(jnp.mean(acc * acc, axis=-1, keepdims=True) + NORM_EPS)
                o_refs[1][...] = (acc * r * extra[-1][...]).astype(BF16)
        if nx:
            @pl.when((i == gm - 1) & (j == gn - 1))
            def _():
                _exchange_wait(kind, x_refs, xo_refs, *sems)

    tile = pl.BlockSpec((tm, tn), lambda i, j: (i, j))
    row = pl.BlockSpec((1, tn), lambda i, j: (0, j))
    in_specs, args = [], []
    for a, b in pairs:
        k = a.shape[1]
        in_specs.append(pl.BlockSpec((tm, k), lambda i, j: (i, 0)))
        in_specs.append(pl.BlockSpec((tn, k), lambda i, j: (j, 0)) if nt else pl.BlockSpec((k, tn), lambda i, j: (0, j)))
        args += [a, b]
    in_specs += [row if e.shape[0] == 1 else tile for e in extra_in]
    out_specs, out_shape = [tile], [jax.ShapeDtypeStruct((m, n), out_dtype)]
    if rms_bwd is not None:
        out_specs.append(row)
        out_shape.append(jax.ShapeDtypeStruct((1, n), F32))
    elif rms_fwd is not None:
        out_specs.append(tile)
        out_shape.append(jax.ShapeDtypeStruct((m, n), BF16))
    hbm = pl.BlockSpec(memory_space=pl.ANY)
    sequential = nx or rms_bwd is not None
    outs = pl.pallas_call(
        body, name=name, grid=(gm, gn), in_specs=in_specs + [hbm] * nx,
        out_specs=out_specs + [hbm] * nx,
        out_shape=out_shape + _exchange_shapes(kind, xs),
        scratch_shapes=_exchange_scratch(nx),
        compiler_params=_params(("arbitrary", "arbitrary") if sequential else ("parallel", "parallel")),
    )(*args, *extra_in, *xs)
    return outs if len(outs) > 1 else outs[0]


def _mm_tn(a, b, name, tm, tn, tk):
    t, m = a.shape
    n = b.shape[1]
    tm, tn, tk = min(tm, m), min(tn, n), min(tk, t)
    nk = t // tk

    def body(a_ref, b_ref, o_ref, acc_ref):
        kk = pl.program_id(2)

        @pl.when(kk == 0)
        def _():
            acc_ref[...] = jnp.zeros_like(acc_ref)

        acc_ref[...] += _bdot_tn(a_ref[...], b_ref[...])

        @pl.when(kk == nk - 1)
        def _():
            o_ref[...] = acc_ref[...].astype(o_ref.dtype)

    return pl.pallas_call(
        body, name=name, grid=(m // tm, n // tn, nk),
        in_specs=[pl.BlockSpec((tk, tm), lambda i, j, kk: (kk, i)), pl.BlockSpec((tk, tn), lambda i, j, kk: (kk, j))],
        out_specs=pl.BlockSpec((tm, tn), lambda i, j, kk: (i, j)),
        out_shape=jax.ShapeDtypeStruct((m, n), BF16),
        scratch_shapes=[pltpu.VMEM((tm, tn), F32)],
        compiler_params=_params(("parallel", "parallel", "arbitrary")),
    )(a, b)


def _mlp_down_loss(pre, w_down, h2, target, w):
    m, k = pre.shape
    d = w_down.shape[1]
    tm = min(512, m)

    def body(p_ref, b_ref, r_ref, t_ref, w_ref, dh_ref, loss_ref, gw_ref):
        u = jnp.square(jnp.maximum(p_ref[...].astype(F32), 0.0))
        xv = r_ref[...] + _bdot(u, b_ref[...])
        r = lax.rsqrt(jnp.mean(xv * xv, axis=-1, keepdims=True) + NORM_EPS)
        nrm = xv * r
        wv = w_ref[...]
        err = nrm * wv - t_ref[...]
        part = 0.5 * jnp.sum(jnp.mean(err * err, axis=-1, keepdims=True), axis=0, keepdims=True)
        dy = err * (1.0 / d)
        g = dy * wv
        dh_ref[...] = r * (g - nrm * jnp.mean(g * nrm, axis=-1, keepdims=True))

        @pl.when(pl.program_id(0) == 0)
        def _():
            loss_ref[...] = jnp.zeros_like(loss_ref)
            gw_ref[...] = jnp.zeros_like(gw_ref)

        loss_ref[...] += jnp.broadcast_to(part, loss_ref.shape)
        gw_ref[...] += jnp.sum(dy * nrm, axis=0, keepdims=True)

    tok = pl.BlockSpec((tm, d), lambda i: (i, 0))
    row = pl.BlockSpec((1, d), lambda i: (0, 0))
    return pl.pallas_call(
        body, name="mlp_down_loss", grid=(m // tm,),
        in_specs=[pl.BlockSpec((tm, k), lambda i: (i, 0)), pl.BlockSpec((k, d), lambda i: (0, 0)), tok, tok, row],
        out_specs=[tok, pl.BlockSpec((1, LANES), lambda i: (0, 0)), row],
        out_shape=[jax.ShapeDtypeStruct((m, d), F32), jax.ShapeDtypeStruct((1, LANES), F32),
                   jax.ShapeDtypeStruct((1, d), F32)],
        compiler_params=_params(("arbitrary",)),
    )(pre, w_down, h2, target, w)


def _mlp_down_bwd(dh3, w_down, pre):
    m, k = dh3.shape
    n = w_down.shape[0]
    tm, tn = min(256, m), n

    def body(a_ref, b_ref, p_ref, dpre_ref, u_ref):
        r = jnp.maximum(p_ref[...].astype(F32), 0.0)
        du = _bdot_nt(a_ref[...], b_ref[...])
        dpre_ref[...] = (du * (2.0 * r)).astype(BF16)
        u_ref[...] = (r * r).astype(BF16)

    blk = pl.BlockSpec((tm, tn), lambda i, j: (i, j))
    return pl.pallas_call(
        body, name="mlp_down_bwd", grid=(m // tm, n // tn),
        in_specs=[pl.BlockSpec((tm, k), lambda i, j: (i, 0)), pl.BlockSpec((tn, k), lambda i, j: (j, 0)), blk],
        out_specs=[blk, blk],
        out_shape=[jax.ShapeDtypeStruct((m, n), BF16), jax.ShapeDtypeStruct((m, n), BF16)],
        compiler_params=_params(("parallel", "parallel")),
    )(dh3, w_down, pre)


CONV_TC = 512


def _conv_fwd(xbc, cw, cb, bsz, seq):
    tt = min(512, seq)
    nt, hb = seq // tt, tt // SUBLANES
    x3 = xbc.reshape(bsz, seq, CONV_CH)

    def body(xm_ref, xh_ref, w_ref, b_ref, o_ref):
        i = pl.program_id(2)
        x = xm_ref[0]
        halo = jnp.where(i > 0, xh_ref[0], 0.0)
        xx = jnp.concatenate([halo, x], axis=0)
        acc = x * w_ref[3:4, :] + b_ref[...]
        for s in range(1, CONV_K):
            acc = acc + pltpu.roll(xx, s, 0)[SUBLANES:, :] * w_ref[3 - s:4 - s, :]
        o_ref[0] = acc * _sigmoid(acc)

    out = pl.pallas_call(
        body, name="conv_fwd", grid=(CONV_CH // CONV_TC, bsz, nt),
        in_specs=[pl.BlockSpec((1, tt, CONV_TC), lambda c, b, i: (b, i, c)),
                  pl.BlockSpec((1, SUBLANES, CONV_TC), lambda c, b, i: (b, jnp.maximum(i * hb - 1, 0), c)),
                  pl.BlockSpec((CONV_K, CONV_TC), lambda c, b, i: (0, c)),
                  pl.BlockSpec((1, CONV_TC), lambda c, b, i: (0, c))],
        out_specs=pl.BlockSpec((1, tt, CONV_TC), lambda c, b, i: (b, i, c)),
        out_shape=jax.ShapeDtypeStruct((bsz, seq, CONV_CH), F32),
        compiler_params=_params(("parallel", "parallel", "parallel")),
    )(x3, x3, cw, cb)
    return out.reshape(bsz * seq, CONV_CH)


def _conv_bwd(da, xbc, cw, cb, bsz, seq):
    tt = min(512, seq)
    nt, hb = seq // tt, tt // SUBLANES
    x3 = xbc.reshape(bsz, seq, CONV_CH)
    da3 = da.reshape(bsz, seq, CONV_CH)
    n2 = tt + SUBLANES

    def body(dam_ref, daa_ref, xm_ref, xb_ref, xa_ref, w_ref, b_ref, du_ref, gw_ref, gb_ref):
        bi, i = pl.program_id(1), pl.program_id(2)
        before = jnp.where(i > 0, xb_ref[0], 0.0)
        after = jnp.where(i < nt - 1, xa_ref[0], 0.0)
        xx = jnp.concatenate([before, xm_ref[0], after], axis=0)
        rolled = [xx] + [pltpu.roll(xx, s, 0) for s in range(1, CONV_K)]
        pre = b_ref[...]
        for s in range(CONV_K):
            pre = pre + rolled[s][SUBLANES:, :] * w_ref[3 - s:4 - s, :]
        da_ext = jnp.concatenate([dam_ref[0], jnp.where(i < nt - 1, daa_ref[0], 0.0)], axis=0)
        sg = _sigmoid(pre)
        dpre = da_ext * sg * (1.0 + pre * (1.0 - sg))
        du = dpre[:tt, :] * w_ref[3:4, :]
        for s in range(1, CONV_K):
            du = du + pltpu.roll(dpre, n2 - s, 0)[:tt, :] * w_ref[3 - s:4 - s, :]
        du_ref[0] = du.astype(BF16)
        dpm = dpre[:tt, :]
        gws = [jnp.sum(dpm * rolled[3 - kk][SUBLANES:SUBLANES + tt, :], axis=0, keepdims=True) for kk in range(CONV_K)]

        @pl.when((bi == 0) & (i == 0))
        def _():
            gw_ref[...] = jnp.zeros_like(gw_ref)
            gb_ref[...] = jnp.zeros_like(gb_ref)

        gw_ref[...] += jnp.concatenate(gws, axis=0)
        gb_ref[...] += jnp.sum(dpm, axis=0, keepdims=True)

    main = pl.BlockSpec((1, tt, CONV_TC), lambda c, b, i: (b, i, c))
    prev = pl.BlockSpec((1, SUBLANES, CONV_TC), lambda c, b, i: (b, jnp.maximum(i * hb - 1, 0), c))
    nxt = pl.BlockSpec((1, SUBLANES, CONV_TC), lambda c, b, i: (b, jnp.minimum((i + 1) * hb, seq // SUBLANES - 1), c))
    du, gw, gb = pl.pallas_call(
        body, name="conv_bwd", grid=(CONV_CH // CONV_TC, bsz, nt),
        in_specs=[main, nxt, main, prev, nxt,
                  pl.BlockSpec((CONV_K, CONV_TC), lambda c, b, i: (0, c)),
                  pl.BlockSpec((1, CONV_TC), lambda c, b, i: (0, c))],
        out_specs=[main, pl.BlockSpec((CONV_K, CONV_TC), lambda c, b, i: (0, c)),
                   pl.BlockSpec((1, CONV_TC), lambda c, b, i: (0, c))],
        out_shape=[jax.ShapeDtypeStruct((bsz, seq, CONV_CH), BF16), jax.ShapeDtypeStruct((CONV_K, CONV_CH), F32),
                   jax.ShapeDtypeStruct((1, CONV_CH), F32)],
        compiler_params=_params(("parallel", "arbitrary", "arbitrary")),
    )(da3, da3, x3, x3, x3, cw, cb)
    return du.reshape(bsz * seq, CONV_CH), gw, gb


def _ssd_prologue(dtf_ref, dtft_ref, bias_ref, biast_ref, arow_ref, acol_ref, dtfull_scr, acfull_scr, acr_scr):
    tri = (_iota((CHUNK, CHUNK), 1) <= _iota((CHUNK, CHUNK), 0)).astype(F32)
    triu = (_iota((CHUNK, CHUNK), 0) <= _iota((CHUNK, CHUNK), 1)).astype(F32)
    dtc = _softplus(dtf_ref[0] + bias_ref[...])
    a = dtc * arow_ref[...]
    acum = _sel_dot(tri, a)
    expand = _head_expand()
    dtfull_scr[...] = _dot_sel(dtc, expand)
    acfull_scr[...] = _dot_sel(acum, expand)
    dtr = _softplus(dtft_ref[0] + biast_ref[...])
    acr_scr[...] = _dot_sel(dtr * acol_ref[...], triu)
    return dtc, acum


def _decay_matrix(acum, acr_scr, h):
    lane = _iota((CHUNK, LANES), 1)
    ac_col = jnp.sum(jnp.where(lane == h, acum, 0.0), axis=1, keepdims=True)
    ac_row = acr_scr[h:h + 1, :]
    causal = _iota((CHUNK, CHUNK), 1) <= _iota((CHUNK, CHUNK), 0)
    return jnp.exp(jnp.where(causal, ac_col - ac_row, NEG))


def _ssd_fwd(xbc, dtf, dtft, bias, biast, arow, acol, dfull, bsz, seq):
    nc = seq // CHUNK
    x3 = xbc.reshape(bsz, seq, CONV_CH)

    def body(xbc_ref, dtf_ref, dtft_ref, bias_ref, biast_ref, arow_ref, acol_ref, dfull_ref,
             y_ref, hall_ref, h_scr, dtfull_scr, acfull_scr, acr_scr):
        @pl.when(pl.program_id(1) == 0)
        def _():
            h_scr[...] = jnp.zeros_like(h_scr)

        _, acum = _ssd_prologue(dtf_ref, dtft_ref, bias_ref, biast_ref, arow_ref, acol_ref,
                                dtfull_scr, acfull_scr, acr_scr)
        lane = _iota((CHUNK, LANES), 1)
        for g in range(2):
            bg = xbc_ref[0, :, D_MODEL + LANES * g:D_MODEL + LANES * (g + 1)]
            cg = xbc_ref[0, :, D_MODEL + 2 * LANES + LANES * g:D_MODEL + 2 * LANES + LANES * (g + 1)]
            cg_bf = cg.astype(BF16)
            gmat = _bdot_nt(cg_bf, bg)
            bgt_bf = bg.T.astype(BF16)
            for j in range(4):
                p = 4 * g + j
                sl = slice(LANES * p, LANES * (p + 1))
                xs = xbc_ref[0, :, sl]
                ac = acfull_scr[:, sl]
                acl = acfull_scr[CHUNK - 1:CHUNK, sl]
                xdt = xs * dtfull_scr[:, sl]
                xdt_bf = xdt.astype(BF16)
                hp = h_scr[p]
                hall_ref[0, 0, p] = hp
                yo = _bdot(cg_bf, hp) * jnp.exp(ac)
                yd = []
                for hh in range(2):
                    mmat = gmat * _decay_matrix(acum, acr_scr, 2 * p + hh)
                    yd.append(_bdot(mmat, xdt_bf))
                y_ref[0, :, sl] = jnp.where(lane < HEAD_DIM, yd[0], yd[1]) + yo + dfull_ref[:, sl] * xs
                h_scr[p] = hp * jnp.exp(acl) + _bdot(bgt_bf, xdt * jnp.exp(acl - ac))

    row = lambda w: pl.BlockSpec((1, w), lambda b, c: (0, 0))
    y, hall = pl.pallas_call(
        body, name="ssd_fwd", grid=(bsz, nc),
        in_specs=[pl.BlockSpec((1, CHUNK, CONV_CH), lambda b, c: (b, c, 0)),
                  pl.BlockSpec((1, CHUNK, LANES), lambda b, c: (b, c, 0)),
                  pl.BlockSpec((1, LANES, CHUNK), lambda b, c: (b, 0, c)),
                  row(LANES), pl.BlockSpec((LANES, 1), lambda b, c: (0, 0)),
                  row(LANES), pl.BlockSpec((LANES, 1), lambda b, c: (0, 0)), row(D_MODEL)],
        out_specs=[pl.BlockSpec((1, CHUNK, D_MODEL), lambda b, c: (b, c, 0)),
                   pl.BlockSpec((1, 1, HEAD_PAIRS, SSD_STATE, LANES), lambda b, c: (b, c, 0, 0, 0))],
        out_shape=[jax.ShapeDtypeStruct((bsz, seq, D_MODEL), F32),
                   jax.ShapeDtypeStruct((bsz, nc, HEAD_PAIRS, SSD_STATE, LANES), F32)],
        scratch_shapes=[pltpu.VMEM((HEAD_PAIRS, SSD_STATE, LANES), F32), pltpu.VMEM((CHUNK, D_MODEL), F32),
                        pltpu.VMEM((CHUNK, D_MODEL), F32), pltpu.VMEM((LANES, CHUNK), F32)],
        compiler_params=_params(("parallel", "arbitrary")),
    )(x3, dtf.reshape(bsz, seq, LANES), dtft, bias, biast, arow, acol, dfull)
    return y.reshape(bsz * seq, D_MODEL), hall


def _ssd_bwd(dy, xbc, dtf, dtft, bias, biast, arow, acol, dfull, hall, bsz, seq):
    nc = seq // CHUNK
    x3 = xbc.reshape(bsz, seq, CONV_CH)
    dy3 = dy.reshape(bsz, seq, D_MODEL)

    def body(dy_ref, xbc_ref, dtf_ref, dtft_ref, bias_ref, biast_ref, arow_ref, acol_ref, dfull_ref, hall_ref,
             dx_ref, ddt_ref, acc_ref, dh_scr, dtfull_scr, acfull_scr, acr_scr, csum_scr, dacf_scr, ddtf_scr, ddl_scr):
        first = (pl.program_id(0) == 0) & (pl.program_id(1) == 0)

        @pl.when(first)
        def _():
            acc_ref[...] = jnp.zeros_like(acc_ref)

        @pl.when(pl.program_id(1) == 0)
        def _():
            dh_scr[...] = jnp.zeros_like(dh_scr)

        dtc, acum = _ssd_prologue(dtf_ref, dtft_ref, bias_ref, biast_ref, arow_ref, acol_ref,
                                  dtfull_scr, acfull_scr, acr_scr)
        csum_scr[...] = jnp.zeros_like(csum_scr)
        lane = _iota((CHUNK, LANES), 1)
        last_row = _iota((CHUNK, LANES), 0) == CHUNK - 1
        rs16 = jnp.zeros((CHUNK, LANES), F32)
        for g in range(2):
            bsl = slice(D_MODEL + LANES * g, D_MODEL + LANES * (g + 1))
            csl = slice(D_MODEL + 2 * LANES + LANES * g, D_MODEL + 2 * LANES + LANES * (g + 1))
            bg_bf = xbc_ref[0, :, bsl].astype(BF16)
            cg = xbc_ref[0, :, csl]
            cg_bf = cg.astype(BF16)
            cgt_bf = cg.T.astype(BF16)
            gmat = _bdot_nt(cg_bf, bg_bf)
            dg = jnp.zeros((CHUNK, CHUNK), F32)
            dbg = jnp.zeros((CHUNK, SSD_STATE), F32)
            dcg = jnp.zeros((CHUNK, SSD_STATE), F32)
            for j in range(4):
                p = 4 * g + j
                sl = slice(LANES * p, LANES * (p + 1))
                xs = xbc_ref[0, :, sl]
                dt = dtfull_scr[:, sl]
                ac = acfull_scr[:, sl]
                acl = acfull_scr[CHUNK - 1:CHUNK, sl]
                dyp = dy_ref[0, :, sl]
                xdt = xs * dt
                xdt_bf = xdt.astype(BF16)
                e = jnp.exp(ac)
                dsd = jnp.exp(acl - ac)
                cd = jnp.exp(acl)
                xds_bf = (xdt * dsd).astype(BF16)
                hp = hall_ref[0, 0, p]
                hp_bf = hp.astype(BF16)
                dhn = dh_scr[p]
                dhn_bf = dhn.astype(BF16)
                yo = _bdot(cg_bf, hp_bf) * e
                dw_bf = (dyp * e).astype(BF16)
                dcg = dcg + _bdot_nt(dw_bf, hp_bf)
                dhin = _bdot(cgt_bf, dw_bf)
                dac = dyp * yo
                dacl = jnp.sum(dhn * hp, axis=0, keepdims=True) * cd
                dxds = _bdot(bg_bf, dhn_bf)
                dbg = dbg + _bdot_nt(xds_bf, dhn_bf)
                dxdt = dxds * dsd
                tdec = dxds * xdt * dsd
                dacl = dacl + jnp.sum(tdec, axis=0, keepdims=True)
                dac = dac - tdec
                dh_scr[p] = dhn * cd + dhin
                for hh in range(2):
                    h = 2 * p + hh
                    lm = (lane < HEAD_DIM) if hh == 0 else (lane >= HEAD_DIM)
                    dec = _decay_matrix(acum, acr_scr, h)
                    mmat = gmat * dec
                    dyh_bf = jnp.where(lm, dyp, 0.0).astype(BF16)
                    dm = _bdot_nt(dyh_bf, xdt_bf)
                    dxdt = dxdt + _bdot(mmat.T, dyh_bf)
                    dg = dg + dm * dec
                    q = dm * mmat
                    rs16 = rs16 + jnp.where(lane == h, jnp.sum(q, axis=1, keepdims=True), 0.0)
                    csum_scr[h:h + 1, :] = jnp.sum(q, axis=0, keepdims=True)
                dx_ref[0, :, sl] = dfull_ref[:, sl] * dyp + dxdt * dt
                dacf_scr[:, sl] = dac + jnp.where(last_row, dacl, 0.0)
                ddtf_scr[:, sl] = dxdt * xs
                ddl_scr[:, sl] = jnp.broadcast_to(jnp.sum(dyp * xs, axis=0, keepdims=True), (SUBLANES, LANES))
            dg_bf = dg.astype(BF16)
            dx_ref[0, :, bsl] = dbg + _bdot(dg.T, cg_bf)
            dx_ref[0, :, csl] = dcg + _bdot(dg_bf, bg_bf)
        reduce = _head_reduce()
        triu = (_iota((CHUNK, CHUNK), 0) <= _iota((CHUNK, CHUNK), 1)).astype(F32)
        dac16 = _dot_sel(dacf_scr[...], reduce) + rs16 - csum_scr[...].T
        da16 = _sel_dot(triu, dac16)
        ddt16 = da16 * arow_ref[...] + _dot_sel(ddtf_scr[...], reduce)
        ddtraw = ddt16 * _sigmoid(dtf_ref[0] + bias_ref[...])
        ddt_ref[0] = ddtraw
        acc_ref[0:8, :] += jnp.broadcast_to(jnp.sum(da16 * dtc * arow_ref[...], axis=0, keepdims=True), (SUBLANES, LANES))
        acc_ref[8:16, :] += _dot_sel(ddl_scr[...], reduce)
        acc_ref[16:24, :] += jnp.broadcast_to(jnp.sum(ddtraw, axis=0, keepdims=True), (SUBLANES, LANES))

    rev = lambda b, c: (b, nc - 1 - c, 0)
    row = lambda w: pl.BlockSpec((1, w), lambda b, c: (0, 0))
    dx, ddt, acc = pl.pallas_call(
        body, name="ssd_bwd", grid=(bsz, nc),
        in_specs=[pl.BlockSpec((1, CHUNK, D_MODEL), rev), pl.BlockSpec((1, CHUNK, CONV_CH), rev),
                  pl.BlockSpec((1, CHUNK, LANES), rev),
                  pl.BlockSpec((1, LANES, CHUNK), lambda b, c: (b, 0, nc - 1 - c)),
                  row(LANES), pl.BlockSpec((LANES, 1), lambda b, c: (0, 0)),
                  row(LANES), pl.BlockSpec((LANES, 1), lambda b, c: (0, 0)), row(D_MODEL),
                  pl.BlockSpec((1, 1, HEAD_PAIRS, SSD_STATE, LANES), lambda b, c: (b, nc - 1 - c, 0, 0, 0))],
        out_specs=[pl.BlockSpec((1, CHUNK, CONV_CH), rev), pl.BlockSpec((1, CHUNK, LANES), rev),
                   pl.BlockSpec((3 * SUBLANES, LANES), lambda b, c: (0, 0))],
        out_shape=[jax.ShapeDtypeStruct((bsz, seq, CONV_CH), F32), jax.ShapeDtypeStruct((bsz, seq, LANES), F32),
                   jax.ShapeDtypeStruct((3 * SUBLANES, LANES), F32)],
        scratch_shapes=[pltpu.VMEM((HEAD_PAIRS, SSD_STATE, LANES), F32), pltpu.VMEM((CHUNK, D_MODEL), F32),
                        pltpu.VMEM((CHUNK, D_MODEL), F32), pltpu.VMEM((LANES, CHUNK), F32),
                        pltpu.VMEM((LANES, CHUNK), F32), pltpu.VMEM((CHUNK, D_MODEL), F32),
                        pltpu.VMEM((CHUNK, D_MODEL), F32), pltpu.VMEM((SUBLANES, D_MODEL), F32)],
        compiler_params=_params(("arbitrary", "arbitrary")),
    )(dy3, x3, dtf.reshape(bsz, seq, LANES), dtft, bias, biast, arow, acol, dfull, hall)
    return dx.reshape(bsz * seq, CONV_CH), ddt.reshape(bsz * seq, LANES), acc


GROUP_W = D_MODEL // 2


def _gnorm_fwd(y, z, o_att, w):
    n = y.shape[0]
    tm = min(1024, n)

    def body(y_ref, z_ref, o_ref, w_ref, out_ref):
        zv = z_ref[...]
        g = y_ref[...] * (zv * _sigmoid(zv))
        for gi in range(2):
            sl = slice(GROUP_W * gi, GROUP_W * (gi + 1))
            gs = g[:, sl]
            r = lax.rsqrt(jnp.mean(gs * gs, axis=-1, keepdims=True) + NORM_EPS)
            out_ref[:, sl] = (gs * r * w_ref[:, sl]).astype(BF16)
        out_ref[:, D_MODEL:] = o_ref[...].astype(BF16)

    tok = pl.BlockSpec((tm, D_MODEL), lambda i: (i, 0))
    return pl.pallas_call(
        body, name="gnorm_fwd", grid=(n // tm,),
        in_specs=[tok, tok, tok, pl.BlockSpec((1, D_MODEL), lambda i: (0, 0))],
        out_specs=pl.BlockSpec((tm, MIX_WIDTH), lambda i: (i, 0)),
        out_shape=jax.ShapeDtypeStruct((n, MIX_WIDTH), BF16),
        compiler_params=_params(("parallel",)),
    )(y, z, o_att, w)


def _gnorm_bwd(dycat, y, z, w):
    n = y.shape[0]
    tm = min(512, n)

    def body(dn_ref, y_ref, z_ref, w_ref, dy_ref, dz_ref, gw_ref):
        zv, yv = z_ref[...], y_ref[...]
        sz = _sigmoid(zv)
        sil = zv * sz
        g = yv * sil

        @pl.when(pl.program_id(0) == 0)
        def _():
            gw_ref[...] = jnp.zeros_like(gw_ref)

        for gi in range(2):
            sl = slice(GROUP_W * gi, GROUP_W * (gi + 1))
            gs = g[:, sl]
            r = lax.rsqrt(jnp.mean(gs * gs, axis=-1, keepdims=True) + NORM_EPS)
            nrm = gs * r
            dyn = dn_ref[:, sl]
            dn = dyn * w_ref[:, sl]
            dgs = r * (dn - nrm * jnp.mean(dn * nrm, axis=-1, keepdims=True))
            dy_ref[:, sl] = dgs * sil[:, sl]
            dz_ref[:, sl] = (dgs * yv[:, sl] * (sz[:, sl] * (1.0 + zv[:, sl] * (1.0 - sz[:, sl])))).astype(BF16)
            gw_ref[:, sl] += jnp.sum(dyn * nrm, axis=0, keepdims=True)

    tok = pl.BlockSpec((tm, D_MODEL), lambda i: (i, 0))
    row = pl.BlockSpec((1, D_MODEL), lambda i: (0, 0))
    return pl.pallas_call(
        body, name="gnorm_bwd", grid=(n // tm,),
        in_specs=[tok, tok, tok, row], out_specs=[tok, tok, row],
        out_shape=[jax.ShapeDtypeStruct((n, D_MODEL), F32), jax.ShapeDtypeStruct((n, D_MODEL), BF16),
                   jax.ShapeDtypeStruct((1, D_MODEL), F32)],
        compiler_params=_params(("arbitrary",)),
    )(dycat, y, z, w)


AUX_C = 3
AUX_ONE = 3


def _aux_base(hh):
    return HEAD_DIM * (1 - hh)


def _fox_prep(dtf, fb, bsz, seq):
    def body(x_ref, b_ref, aux_ref):
        tri = (_iota((CHUNK, CHUNK), 1) <= _iota((CHUNK, CHUNK), 0)).astype(F32)
        row, col = _iota((LANES, D_MODEL), 0), _iota((LANES, D_MODEL), 1)
        lane = jnp.bitwise_and(col, LANES - 1)
        other = (lane < HEAD_DIM).astype(jnp.int32)
        term = lane - HEAD_DIM * (1 - other)
        src = F_COL + 2 * jnp.right_shift(col, 7) + other
        place = [jnp.where((row == src) & (term == i), -1.0, 0.0).astype(BF16) for i in range(AUX_C)]
        lane1 = jnp.bitwise_and(_iota((1, D_MODEL), 1), LANES - 1)
        ones_lane = (lane1 - HEAD_DIM * (1 - (lane1 < HEAD_DIM).astype(jnp.int32)) == AUX_ONE).astype(F32)
        carry = jnp.zeros((1, LANES), F32)
        for j in range(seq // CHUNK):
            sl = slice(CHUNK * j, CHUNK * (j + 1))
            lf = _log_sigmoid(x_ref[sl, :] + b_ref[...])
            c = _sel_dot(tri, lf) + carry
            carry = carry + jnp.sum(lf, axis=0, keepdims=True)
            t1, t2, t3 = (jnp.dot(t, p, preferred_element_type=F32) for t, p in zip(_split3(c), place))
            aux_ref[sl, :] = (t1 + t2 + t3 + ones_lane).astype(BF16)

    return pl.pallas_call(
        body, name="fox_prep", grid=(bsz,),
        in_specs=[pl.BlockSpec((seq, LANES), lambda b: (b, 0)), pl.BlockSpec((1, LANES), lambda b: (0, 0))],
        out_specs=pl.BlockSpec((seq, D_MODEL), lambda b: (b, 0)),
        out_shape=jax.ShapeDtypeStruct((bsz * seq, D_MODEL), BF16),
        compiler_params=_params(("parallel",)),
    )(dtf, fb)


def _fox_prep_bwd(dck, dcq, ddt, dtf, fb, bsz, seq):
    nj = seq // CHUNK

    def body(dck_ref, dcq_ref, ddt_ref, x_ref, b_ref, out_ref, gb_ref):
        triu = (_iota((CHUNK, CHUNK), 0) <= _iota((CHUNK, CHUNK), 1)).astype(F32)
        is_f = (_iota((CHUNK, LANES), 1) >= F_COL) & (_iota((CHUNK, LANES), 1) < 2 * F_COL)
        carry = jnp.zeros((1, LANES), F32)
        total = jnp.zeros((1, LANES), F32)
        for j in range(nj - 1, -1, -1):
            sl = slice(CHUNK * j, CHUNK * (j + 1))
            dcv = dck_ref[sl, :] + dcq_ref[sl, :]
            dlf = _sel_dot(triu, dcv) + carry
            carry = carry + jnp.sum(dcv, axis=0, keepdims=True)
            df = jnp.where(is_f, dlf * _sigmoid(-(x_ref[sl, :] + b_ref[...])), 0.0)
            out_ref[sl, :] = (df + ddt_ref[sl, :]).astype(BF16)
            total = total + jnp.sum(df, axis=0, keepdims=True)

        @pl.when(pl.program_id(0) == 0)
        def _():
            gb_ref[...] = jnp.zeros_like(gb_ref)

        gb_ref[...] += jnp.broadcast_to(total, (SUBLANES, LANES))

    blk = pl.BlockSpec((seq, LANES), lambda b: (b, 0))
    return pl.pallas_call(
        body, name="fox_prep_bwd", grid=(bsz,),
        in_specs=[blk, blk, blk, blk, pl.BlockSpec((1, LANES), lambda b: (0, 0))],
        out_specs=[blk, pl.BlockSpec((SUBLANES, LANES), lambda b: (0, 0))],
        out_shape=[jax.ShapeDtypeStruct((bsz * seq, LANES), BF16), jax.ShapeDtypeStruct((SUBLANES, LANES), F32)],
        compiler_params=_params(("arbitrary",)),
    )(dck, dcq, ddt, dtf, fb)


ATT_BLOCK_FWD = 512
ATT_BLOCK_BWD = 256


def _att_operands(q_ref, k_ref, aux_ref, hh):
    lane = _iota((1, LANES), 1)
    lm = (lane < HEAD_DIM) if hh == 0 else (lane >= HEAD_DIM)
    base = _aux_base(hh)
    zero = jnp.zeros((1, LANES), BF16)
    ka = jnp.where(lm, k_ref[...], jnp.where((lane >= base) & (lane <= base + AUX_ONE), aux_ref[...], zero))
    qa = jnp.where(lm, q_ref[...] * ATT_SCALE,
                   jnp.where((lane >= base) & (lane < base + AUX_C), jnp.ones((1, LANES), BF16), zero))
    return lm, base, qa, ka


def _att_fwd(qkv, ccol, bsz, seq, gather):
    blk = min(ATT_BLOCK_FWD, seq)
    nb = seq // blk
    nx = len(gather)

    def body(*refs):
        q_ref, k_ref, v_ref, c_ref = refs[:4]
        x_refs = refs[4:4 + nx]
        o_ref, lse_ref = refs[4 + nx:6 + nx]
        xo_refs = refs[6 + nx:6 + 2 * nx]
        qa_scr, ka_scr, va_scr = refs[6 + 2 * nx:9 + 2 * nx]
        sems = refs[9 + 2 * nx:]
        pair = pl.program_id(1)

        @pl.when((pl.program_id(0) == 0) & (pair == 0))
        def _():
            _exchange_start("gather", x_refs, xo_refs, *sems)

        @pl.when(pair == 0)
        def _():
            lse_ref[...] = jnp.zeros_like(lse_ref)

        lane_b = _iota((blk, LANES), 1)
        causal = _iota((blk, blk), 0) >= _iota((blk, blk), 1)
        for hh in range(2):
            lm, _, qa, ka = _att_operands(q_ref, k_ref, c_ref, hh)
            qa_scr[hh] = qa
            ka_scr[hh] = ka
            va_scr[hh] = jnp.where(lm, v_ref[...], jnp.zeros((seq, LANES), BF16))

        def q_step(i, carry0):
            rows = pl.ds(pl.multiple_of(i * blk, blk), blk)

            def scores(j):
                krows = pl.ds(pl.multiple_of(j * blk, blk), blk)
                return tuple(_bdot_nt(qa_scr[hh, rows, :], ka_scr[hh, krows, :]) for hh in range(2))

            def update(state, s_pair, j, masked):
                krows = pl.ds(pl.multiple_of(j * blk, blk), blk)
                alphas, ls, pvs, ms = [], [], [], []
                for hh in range(2):
                    m, l, _ = state[hh]
                    s = jnp.where(causal, s_pair[hh], NEG) if masked else s_pair[hh]
                    mn = jnp.maximum(m, jnp.max(s, axis=1, keepdims=True))
                    alpha = jnp.exp(m - mn)
                    pr = jnp.exp(s - mn)
                    ms.append(mn)
                    alphas.append(alpha)
                    ls.append(alpha * l + jnp.sum(pr, axis=1, keepdims=True))
                    pvs.append(_bdot(pr, va_scr[hh, krows, :]))
                return tuple((ms[hh], ls[hh], alphas[hh] * state[hh][2] + pvs[hh]) for hh in range(2))

            def step(j, carry):
                state, s_cur = carry
                s_next = scores(j + 1)
                return update(state, s_cur, j, False), s_next

            one = (jnp.full((blk, 1), NEG, F32), jnp.zeros((blk, 1), F32), jnp.zeros((blk, LANES), F32))
            state, s_last = lax.fori_loop(0, i, step, ((one, one), scores(0)))
            (m0, l0, acc0), (m1, l1, acc1) = update(state, s_last, i, True)
            o_ref[rows, :] = acc0 * (1.0 / l0) + acc1 * (1.0 / l1)
            head0 = F_COL + 2 * pair
            lse_ref[rows, :] = jnp.where(lane_b == head0, m0 + jnp.log(l0),
                                         jnp.where(lane_b == head0 + 1, m1 + jnp.log(l1), lse_ref[rows, :]))
            return carry0

        lax.fori_loop(0, nb, q_step, 0)

        @pl.when((pl.program_id(0) == bsz - 1) & (pair == HEAD_PAIRS - 1))
        def _():
            _exchange_wait("gather", x_refs, xo_refs, *sems)

    col = lambda off: pl.BlockSpec((seq, LANES), lambda b, p: (b, off + p))
    head2 = pltpu.VMEM((2, seq, LANES), BF16)
    hbm = pl.BlockSpec(memory_space=pl.ANY)
    return pl.pallas_call(
        body, name="att_fwd", grid=(bsz, HEAD_PAIRS),
        in_specs=[col(0), col(HEAD_PAIRS), col(2 * HEAD_PAIRS), col(0)] + [hbm] * nx,
        out_specs=[pl.BlockSpec((seq, LANES), lambda b, p: (b, p)),
                   pl.BlockSpec((seq, LANES), lambda b, p: (b, 0))] + [hbm] * nx,
        out_shape=[jax.ShapeDtypeStruct((bsz * seq, D_MODEL), F32), jax.ShapeDtypeStruct((bsz * seq, LANES), F32)]
        + _exchange_shapes("gather", gather),
        scratch_shapes=[head2, head2, head2] + _exchange_scratch(nx),
        compiler_params=_params(("arbitrary", "arbitrary")),
    )(qkv, qkv, qkv, ccol, *gather)


def _att_bwd(qkv, dycat, o, lse, ccol, bsz, seq, scatter):
    blk = min(ATT_BLOCK_BWD, seq)
    nb = seq // blk
    nx = len(scatter)

    def body(*refs):
        q_ref, k_ref, v_ref, do_ref, o_ref, lse_ref, c_ref = refs[:7]
        x_refs = refs[7:7 + nx]
        dq_ref, dk_ref, dv_ref, dck_ref, dcq_ref = refs[7 + nx:12 + nx]
        xo_refs = refs[12 + nx:12 + 2 * nx]
        qa_scr, ka_scr, va_scr, doa_scr, kat_scr, d_scr, dqt_scr, dk_scr, dv_scr = refs[12 + 2 * nx:21 + 2 * nx]
        sems = refs[21 + 2 * nx:]
        pair = pl.program_id(1)

        @pl.when((pl.program_id(0) == 0) & (pair == 0))
        def _():
            _exchange_start("scatter", x_refs, xo_refs, *sems)

        causal_t = _iota((blk, blk), 1) >= _iota((blk, blk), 0)
        lane_b = _iota((blk, LANES), 1)
        row_t = _iota((LANES, seq), 0)
        masks, bases = [], []
        ones8 = jnp.ones((SUBLANES, LANES), F32)
        for hh in range(2):
            lm, base, qa, ka = _att_operands(q_ref, k_ref, c_ref, hh)
            masks.append(lm)
            bases.append(base)
            qa_scr[hh] = qa
            ka_scr[hh] = ka
            va_scr[hh] = jnp.where(lm, v_ref[...], jnp.zeros((seq, LANES), BF16))
            doa = jnp.where(lm, do_ref[...], 0.0).astype(BF16)
            doa_scr[hh] = doa
            for t0 in range(0, seq, blk):
                kat_scr[hh, :, t0:t0 + blk] = ka[t0:t0 + blk, :].astype(F32).T.astype(BF16)
            d1, d2, d3 = (_bdot_nt(ones8, term) for term in _split3(doa.astype(F32) * o_ref[...]))
            d_scr[hh] = d1 + d2 + d3
        dqt_scr[...] = jnp.zeros_like(dqt_scr)

        @pl.when(pair == 0)
        def _():
            dck_ref[...] = jnp.zeros_like(dck_ref)

        def kv_step(j, carry0):
            krows = pl.ds(pl.multiple_of(j * blk, blk), blk)
            dk_scr[...] = jnp.zeros_like(dk_scr)
            dv_scr[...] = jnp.zeros_like(dv_scr)

            def scores(i):
                rows = pl.ds(pl.multiple_of(i * blk, blk), blk)
                return tuple((_bdot_nt(ka_scr[hh, krows, :], qa_scr[hh, rows, :]),
                              _bdot_nt(va_scr[hh, krows, :], doa_scr[hh, rows, :])) for hh in range(2))

            def update(i, sd, masked):
                rows = pl.ds(pl.multiple_of(i * blk, blk), blk)
                for hh in range(2):
                    st, dpt = sd[hh]
                    if masked:
                        st = jnp.where(causal_t, st, NEG)
                    pt = jnp.exp(st - lse_ref[0, 0, hh:hh + 1, rows])
                    dst = (pt * (dpt - d_scr[hh, 0:1, rows])).astype(BF16)
                    dv_scr[hh] += _bdot(pt, doa_scr[hh, rows, :])
                    dk_scr[hh] += _bdot(dst, qa_scr[hh, rows, :])
                    dqt_scr[hh, :, rows] += _bdot(kat_scr[hh, :, krows], dst)

            def q_step(i, sd_cur):
                sd_next = scores(jnp.minimum(i + 1, nb - 1))
                update(i, sd_cur, False)
                return sd_next

            sd_diag = scores(j)
            sd_first = scores(jnp.minimum(j + 1, nb - 1))
            update(j, sd_diag, True)
            lax.fori_loop(j + 1, nb, q_step, sd_first)
            lmb0 = lane_b < HEAD_DIM
            dk_ref[krows, :] = jnp.where(lmb0, dk_scr[0], dk_scr[1]).astype(BF16)
            dv_ref[krows, :] = (dv_scr[0] + dv_scr[1]).astype(BF16)
            sums = [jnp.sum(jnp.where(lane_b == bases[hh], dk_scr[hh], 0.0), axis=1, keepdims=True) for hh in range(2)]
            head0 = F_COL + 2 * pair
            dck_ref[krows, :] += jnp.where(lane_b == head0, -sums[0], jnp.where(lane_b == head0 + 1, -sums[1], 0.0))
            return carry0

        lax.fori_loop(0, nb, kv_step, 0)
        for t0 in range(0, seq, blk):
            dq0, dq1 = dqt_scr[0, :, t0:t0 + blk].T, dqt_scr[1, :, t0:t0 + blk].T
            dq_ref[t0:t0 + blk, :] = (jnp.where(lane_b < HEAD_DIM, dq0, dq1) * ATT_SCALE).astype(BF16)
        dcq_ref[...] = jnp.zeros_like(dcq_ref)
        for hh in range(2):
            dcq_ref[0, 0, hh:hh + 1, :] = jnp.sum(jnp.where(row_t == bases[hh] + AUX_ONE, dqt_scr[hh], 0.0),
                                                   axis=0, keepdims=True)

        @pl.when((pl.program_id(0) == bsz - 1) & (pair == HEAD_PAIRS - 1))
        def _():
            _exchange_wait("scatter", x_refs, xo_refs, *sems)

    col = lambda off: pl.BlockSpec((seq, LANES), lambda b, p: (b, off + p))
    rowvec = pl.BlockSpec((1, 1, SUBLANES, seq), lambda b, p: (b, p, 0, 0))
    out = jax.ShapeDtypeStruct((bsz * seq, D_MODEL), BF16)
    rows_shape = jax.ShapeDtypeStruct((bsz, HEAD_PAIRS, SUBLANES, seq), F32)
    head2 = pltpu.VMEM((2, seq, LANES), BF16)
    hbm = pl.BlockSpec(memory_space=pl.ANY)
    return pl.pallas_call(
        body, name="att_bwd", grid=(bsz, HEAD_PAIRS),
        in_specs=[col(0), col(HEAD_PAIRS), col(2 * HEAD_PAIRS), col(HEAD_PAIRS), col(0), rowvec, col(0)] + [hbm] * nx,
        out_specs=[col(0), col(0), col(0), pl.BlockSpec((seq, LANES), lambda b, p: (b, 0)), rowvec] + [hbm] * nx,
        out_shape=[out, out, out, jax.ShapeDtypeStruct((bsz * seq, LANES), F32), rows_shape]
        + _exchange_shapes("scatter", scatter),
        scratch_shapes=[head2, head2, head2, head2, pltpu.VMEM((2, LANES, seq), BF16),
                        pltpu.VMEM((2, SUBLANES, seq), F32), pltpu.VMEM((2, LANES, seq), F32),
                        pltpu.VMEM((2, blk, LANES), F32), pltpu.VMEM((2, blk, LANES), F32)] + _exchange_scratch(nx),
        compiler_params=_params(("arbitrary", "arbitrary")),
    )(qkv, qkv, qkv, dycat, o, lse, ccol, *scatter)


def _adamw_math(w, g, m, v):
    m = ADAM_B1 * m + (1.0 - ADAM_B1) * g
    v = ADAM_B2 * v + (1.0 - ADAM_B2) * jnp.square(g)
    m_hat = m / ADAM_C1
    v_hat = v / ADAM_C2
    delta = -ADAM_LR * (m_hat / (jnp.sqrt(v_hat) + ADAM_EPS) + ADAM_WD * w)
    return delta, m, v


def _row_tile(rows):
    for tr in (256, 128, 64, 32, 16, 8):
        if rows % tr == 0:
            return tr
    return rows


def _sum_parts(parts, name):
    nparts, rows, cols = parts.shape
    tr = rows if rows % SUBLANES else _row_tile(rows)

    def body(p_ref, o_ref):
        g = p_ref[0].astype(F32)
        for s in range(1, nparts):
            g = g + p_ref[s].astype(F32)
        o_ref[...] = g

    return pl.pallas_call(
        body, name=name, grid=(rows // tr,),
        in_specs=[pl.BlockSpec((nparts, tr, cols), lambda i: (0, i, 0))],
        out_specs=pl.BlockSpec((tr, cols), lambda i: (i, 0)),
        out_shape=jax.ShapeDtypeStruct((rows, cols), F32),
        compiler_params=_params(("parallel",)),
    )(parts)


def _adamw_parts(parts, w, m, v, name):
    nparts, rows, cols = parts.shape
    tr = _row_tile(rows)

    def body(p_ref, w_ref, m_ref, v_ref, g_ref, d_ref, mo_ref, vo_ref):
        g = p_ref[0].astype(F32)
        for s in range(1, nparts):
            g = g + p_ref[s].astype(F32)
        delta, mn, vn = _adamw_math(w_ref[...], g, m_ref[...], v_ref[...])
        g_ref[...] = g
        d_ref[...] = delta
        mo_ref[...] = mn
        vo_ref[...] = vn

    blk = pl.BlockSpec((tr, cols), lambda i: (i, 0))
    shp = jax.ShapeDtypeStruct((rows, cols), F32)
    return pl.pallas_call(
        body, name=name, grid=(rows // tr,),
        in_specs=[pl.BlockSpec((nparts, tr, cols), lambda i: (0, i, 0)), blk, blk, blk],
        out_specs=[blk, blk, blk, blk], out_shape=[shp, shp, shp, shp],
        compiler_params=_params(("parallel",)),
    )(parts, w, m, v)


def _me():
    return lax.axis_index("x"), lax.axis_index("y"), lax.axis_index("c")


def _flip(pos, k):
    x, y, c = pos
    kx, ky, kc = (k >> 2) & 1, (k >> 1) & 1, k & 1
    return (x ^ kx if kx else x, y ^ ky if ky else y, c ^ kc if kc else c)


def _logical(pos):
    return 4 * pos[0] + 2 * pos[1] + pos[2]


def _all_gather_two_level(shards):
    narr = len(shards)

    def body(*refs):
        x_refs, out_refs = refs[:narr], refs[narr:2 * narr]
        send_sems, recv_sems, local_sems = refs[2 * narr:]
        x, y, c = _me()
        me, sibling = (x, y, c), (x, y, 1 - c)
        chips = [(1 - x, y), (x, 1 - y), (1 - x, 1 - y)]

        def copy(a, k, block, to, src=None):
            slot = out_refs[a].at[_logical(block)]
            return pltpu.make_async_remote_copy(
                src_ref=slot if src is None else src, dst_ref=slot,
                send_sem=send_sems.at[a, k], recv_sem=recv_sems.at[a, k], device_id=to, device_id_type=MESH)

        mine = [pltpu.make_async_copy(x_refs[a], out_refs[a].at[_logical(me)], local_sems.at[a]) for a in range(narr)]
        for cp in mine:
            cp.start()
        first = []
        for a in range(narr):
            first.append(copy(a, 0, me, sibling, src=x_refs[a]))
            first += [copy(a, 1 + j, me, (*chip, c), src=x_refs[a]) for j, chip in enumerate(chips)]
        for cp in first:
            cp.start()
        passed = []
        for a in range(narr):
            for j, chip in enumerate(chips):
                copy(a, 1 + j, (*chip, c), me).wait_recv()
                fwd = copy(a, 4 + j, (*chip, c), sibling)
                fwd.start()
                passed.append(fwd)
        for a in range(narr):
            copy(a, 0, sibling, me).wait_recv()
            for j, chip in enumerate(chips):
                copy(a, 4 + j, (*chip, 1 - c), me).wait_recv()
        for cp in first + passed:
            cp.wait_send()
        for cp in mine:
            cp.wait()

    hbm = pl.BlockSpec(memory_space=pl.ANY)
    return pl.pallas_call(
        body, name="all_gather_big",
        out_shape=[jax.ShapeDtypeStruct((N_DEV,) + s.shape, s.dtype) for s in shards],
        in_specs=[hbm] * narr, out_specs=[hbm] * narr,
        scratch_shapes=[pltpu.SemaphoreType.DMA((narr, 7)), pltpu.SemaphoreType.DMA((narr, 7)),
                        pltpu.SemaphoreType.DMA((narr,))],
    )(*shards)


def _exchange_copies(kind, x_refs, out_refs, send_sems, recv_sems, local_sems):
    me = _me()
    mine = _logical(me)
    local, remote = [], []
    for a, (x_ref, out_ref) in enumerate(zip(x_refs, out_refs)):
        own = x_ref if kind == "gather" else x_ref.at[mine]
        local.append(pltpu.make_async_copy(own, out_ref.at[mine], local_sems.at[a]))
        for k in range(1, N_DEV):
            peer = _flip(me, k)
            src = x_ref if kind == "gather" else x_ref.at[_logical(peer)]
            remote.append(pltpu.make_async_remote_copy(
                src_ref=src, dst_ref=out_ref.at[mine], send_sem=send_sems.at[a, k - 1], recv_sem=recv_sems.at[a, k - 1],
                device_id=peer, device_id_type=MESH))
    return local, remote


def _exchange_start(kind, x_refs, out_refs, *sems):
    if not x_refs:
        return
    local, remote = _exchange_copies(kind, x_refs, out_refs, *sems)
    for cp in local + remote:
        cp.start()


def _exchange_wait(kind, x_refs, out_refs, *sems):
    if not x_refs:
        return
    local, remote = _exchange_copies(kind, x_refs, out_refs, *sems)
    for cp in remote:
        cp.wait_recv()
    for cp in remote:
        cp.wait_send()
    for cp in local:
        cp.wait()


def _exchange_shapes(kind, arrays):
    return [jax.ShapeDtypeStruct(((N_DEV,) if kind == "gather" else ()) + a.shape, a.dtype) for a in arrays]


def _exchange_scratch(narrays):
    if not narrays:
        return []
    return [pltpu.SemaphoreType.DMA((narrays, N_DEV - 1)), pltpu.SemaphoreType.DMA((narrays, N_DEV - 1)),
            pltpu.SemaphoreType.DMA((narrays,))]


def _exchange_rows(x_ref, buf_ref, send_sems, recv_sems):
    me = _me()
    buf_ref[_logical(me)] = x_ref[...]
    copies = []
    for k in range(1, N_DEV):
        peer = _flip(me, k)
        copies.append(pltpu.make_async_remote_copy(
            src_ref=x_ref, dst_ref=buf_ref.at[_logical(me)],
            send_sem=send_sems.at[k - 1], recv_sem=recv_sems.at[k - 1], device_id=peer, device_id_type=MESH))
    for cp in copies:
        cp.start()
    for cp in copies:
        cp.wait_recv()
    for cp in copies:
        cp.wait_send()


def _sum_slots(buf_ref):
    total = buf_ref[0]
    for s in range(1, N_DEV):
        total = total + buf_ref[s]
    return total


def _small_gather(x):
    def body(x_ref, o_ref, buf_ref, send_sems, recv_sems):
        _exchange_rows(x_ref, buf_ref, send_sems, recv_sems)
        o_ref[...] = _sum_slots(buf_ref)

    return pl.pallas_call(
        body, name="small_gather", out_shape=jax.ShapeDtypeStruct(x.shape, F32),
        in_specs=[pl.BlockSpec(memory_space=pltpu.VMEM)], out_specs=pl.BlockSpec(memory_space=pltpu.VMEM),
        scratch_shapes=[pltpu.VMEM((N_DEV,) + x.shape, F32), pltpu.SemaphoreType.DMA((7,)), pltpu.SemaphoreType.DMA((7,))],
    )(x)


def _small_reduce_adamw(g, w, m, v):
    def body(g_ref, w_ref, m_ref, v_ref, go_ref, d_ref, mo_ref, vo_ref, buf_ref, send_sems, recv_sems):
        _exchange_rows(g_ref, buf_ref, send_sems, recv_sems)
        gs = _sum_slots(buf_ref)
        delta, mn, vn = _adamw_math(w_ref[...], gs, m_ref[...], v_ref[...])
        go_ref[...] = gs
        d_ref[...] = delta
        mo_ref[...] = mn
        vo_ref[...] = vn

    vm = pl.BlockSpec(memory_space=pltpu.VMEM)
    shp = jax.ShapeDtypeStruct(g.shape, F32)
    return pl.pallas_call(
        body, name="small_reduce_adamw", out_shape=[shp, shp, shp, shp],
        in_specs=[vm, vm, vm, vm], out_specs=[vm, vm, vm, vm],
        scratch_shapes=[pltpu.VMEM((N_DEV,) + g.shape, F32), pltpu.SemaphoreType.DMA((7,)), pltpu.SemaphoreType.DMA((7,))],
    )(g, w, m, v)


def _pad_cols(a, width):
    return jnp.pad(a, ((0, 0), (0, width - a.shape[1])))


def _local_step(x, target, norm_mix_w, w_in_t, conv_w, conv_b, dt_bias, a_log, d_skip, ssd_norm_w, f_bias,
                late_shards, norm_mlp_w, norm_final_w):
    bsz, seq, _ = x.shape
    n = bsz * seq
    x2 = x.reshape(n, D_MODEL)
    t2 = target.reshape(n, D_MODEL)
    nfw = norm_final_w.reshape(1, D_MODEL)

    bias = _pad_cols(dt_bias, LANES)
    arow = _pad_cols(-jnp.exp(a_log), LANES)
    biast, acol = bias.reshape(LANES, 1), arow.reshape(LANES, 1)
    dfull = jnp.repeat(d_skip, HEAD_DIM, axis=1)
    fb = jnp.pad(f_bias, ((0, 0), (F_COL, LANES - 2 * F_COL)))

    wt_z, wt_xbc, wt_qkv = w_in_t[:ROW_XBC], w_in_t[ROW_XBC:ROW_DT], w_in_t[ROW_Q:ROW_F]
    wt_dtf = jnp.concatenate([w_in_t[ROW_DT:ROW_Q], w_in_t[ROW_F:], jnp.zeros((LANES - 2 * F_COL, D_MODEL), BF16)], axis=0)
    wt_q, wt_k, wt_v = wt_qkv[:D_MODEL], wt_qkv[D_MODEL:2 * D_MODEL], wt_qkv[2 * D_MODEL:]

    h1 = _rms_fwd(x2, norm_mix_w, "rms_fwd_mix")
    z = _mm([(h1, wt_z)], "inproj_z", F32, 1024, 1024, nt=True)
    xbc_raw = _mm([(h1, wt_xbc)], "inproj_xbc", F32, 512, 1536, nt=True)
    qkv = _mm([(h1, wt_qkv)], "inproj_qkv", BF16, 512, 3072, nt=True)
    dtf = _mm([(h1, wt_dtf)], "inproj_dtf", F32, 2048, LANES, nt=True)
    dtft = dtf.reshape(bsz, seq, LANES).transpose(0, 2, 1)

    xbc = _conv_fwd(xbc_raw, conv_w, conv_b, bsz, seq)
    y_pre, hall = _ssd_fwd(xbc, dtf, dtft, bias, biast, arow, acol, dfull, bsz, seq)

    ccol = _fox_prep(dtf, fb, bsz, seq)
    o_att, lse, w_out, w_up_t, w_down = _att_fwd(qkv, ccol, bsz, seq, late_shards)
    w_out, w_up_t, w_down = (w.reshape(-1, D_MODEL) for w in (w_out, w_up_t, w_down))

    ycat = _gnorm_fwd(y_pre, z, o_att, ssd_norm_w)
    h2, h2n = _mm([(ycat, w_out)], "outproj", F32, 512, 1024, res=x2, rms_fwd=norm_mlp_w)
    pre = _mm([(h2n, w_up_t)], "mlp_up", BF16, 512, 4096, nt=True)

    dh3, loss_row, g_nfw = _mlp_down_loss(pre, w_down, h2, t2, nfw)
    dpre, u = _mlp_down_bwd(dh3, w_down, pre)
    g_w_down = _mm_tn(u, dh3, "grad_w_down", 1024, 1024, 2048)
    g_w_up_t = _mm_tn(dpre, h2n, "grad_w_up", 1024, 1024, 2048)
    by_shard = lambda g: g.reshape(N_DEV, g.shape[0] // N_DEV, D_MODEL)
    dh2, g_nmlp = _mm([(dpre, w_up_t)], "mlp_up_bwd", F32, 512, 1024, res=dh3, rms_bwd=(h2, norm_mlp_w))

    g_w_out = _mm_tn(ycat, dh2, "grad_w_out", 1024, 1024, 2048)
    dycat = _mm([(dh2, w_out)], "outproj_bwd", F32, 512, 2048, nt=True)
    dy_pre, dz, g_ssdn = _gnorm_bwd(dycat, y_pre, z, ssd_norm_w)
    lse_rows = lse.reshape(bsz, seq, LANES)[:, :, F_COL:2 * F_COL].transpose(0, 2, 1).reshape(bsz, HEAD_PAIRS, 2, seq)
    lse_rows = jnp.pad(lse_rows, ((0, 0), (0, 0), (0, SUBLANES - 2), (0, 0)))
    dq, dk, dv, dck, dcq, recv_down, recv_up_t, recv_out = _att_bwd(
        qkv, dycat, o_att, lse_rows, ccol, bsz, seq, [by_shard(g_w_down), by_shard(g_w_up_t), by_shard(g_w_out)])

    dxbc, ddt, ssd_acc = _ssd_bwd(dy_pre, xbc, dtf, dtft, bias, biast, arow, acol, dfull, hall, bsz, seq)
    dxbc_raw, g_cw, g_cb = _conv_bwd(dxbc, xbc_raw, conv_w, conv_b, bsz, seq)

    def head_cols(rows):
        cols = rows[:, :, :2, :].reshape(bsz, SSD_HEADS, seq).transpose(0, 2, 1).reshape(n, SSD_HEADS)
        return jnp.pad(cols, ((0, 0), (F_COL, LANES - 2 * F_COL)))

    ddtf, g_fb = _fox_prep_bwd(dck, head_cols(dcq), ddt, dtf, fb, bsz, seq)

    g_dtf = _mm_tn(ddtf, h1, "grad_w_in_dtf", LANES, 1024, 2048)
    g_w_in_t = jnp.concatenate([
        _mm_tn(dz, h1, "grad_w_in_z", 1024, 1024, 2048), _mm_tn(dxbc_raw, h1, "grad_w_in_xbc", 768, 1024, 2048),
        g_dtf[:F_COL], _mm_tn(dq, h1, "grad_w_in_q", 1024, 1024, 2048), _mm_tn(dk, h1, "grad_w_in_k", 1024, 1024, 2048),
        _mm_tn(dv, h1, "grad_w_in_v", 1024, 1024, 2048), g_dtf[F_COL:2 * F_COL]], axis=0)
    pieces = [(dz, wt_z), (dxbc_raw, wt_xbc), (dq, wt_q), (dk, wt_k), (dv, wt_v), (ddtf, wt_dtf)]
    dx, g_nmix, recv_in_t = _mm(pieces, "inproj_bwd", F32, 256, 1024, res=dh2, rms_bwd=(x2, norm_mix_w),
                                exchange=("scatter", [by_shard(g_w_in_t)]))

    small = dict(
        norm_mix_w=g_nmix, norm_mlp_w=g_nmlp, norm_final_w=g_nfw, ssd_norm_w=g_ssdn, conv_b=g_cb, conv_w=g_cw,
        dt_bias=ssd_acc[16:17, :SSD_HEADS], a_log=ssd_acc[0:1, :SSD_HEADS], d_skip=ssd_acc[8:9, :SSD_HEADS],
        f_bias=g_fb[0:1, F_COL:2 * F_COL])
    recv = dict(w_in_t=recv_in_t, w_out=recv_out, w_up_t=recv_up_t, w_down=recv_down)
    return loss_row[0, 0], dx.reshape(bsz, seq, D_MODEL), small, recv


SMALL_LAYOUT = (("norm_mix_w", 0, 1, 0, D_MODEL), ("norm_mlp_w", 1, 1, 0, D_MODEL), ("norm_final_w", 2, 1, 0, D_MODEL),
                ("ssd_norm_w", 3, 1, 0, D_MODEL), ("conv_b", 4, 1, 0, CONV_CH), ("conv_w", 5, CONV_K, 0, CONV_CH),
                ("dt_bias", 9, 1, 0, SSD_HEADS), ("a_log", 9, 1, LANES, SSD_HEADS), ("d_skip", 9, 1, 2 * LANES, SSD_HEADS),
                ("f_bias", 9, 1, 3 * LANES, SSD_HEADS))
SMALL_ROWS = 2 * SUBLANES


def _pack_small(vals):
    packed = jnp.zeros((SMALL_ROWS, SMALL_COLS), F32)
    for name, r0, nrows, c0, width in SMALL_LAYOUT:
        packed = packed.at[r0:r0 + nrows, c0:c0 + width].set(vals[name].reshape(nrows, width))
    return packed


def _unpack_small(packed, like):
    out = {}
    for name, r0, nrows, c0, width in SMALL_LAYOUT:
        out[name] = packed[r0:r0 + nrows, c0:c0 + width].reshape(like[name].shape)
    return out


def kernel(x, norm_mix_w, w_in, conv_w, conv_b, dt_bias, a_log, d_skip, ssd_norm_w, f_bias, w_out, norm_mlp_w, w_up, w_down, norm_final_w, loss_target, m_norm_mix_w, m_w_in, m_conv_w, m_conv_b, m_dt_bias, m_a_log, m_d_skip, m_ssd_norm_w, m_f_bias, m_w_out, m_norm_mlp_w, m_w_up, m_w_down, m_norm_final_w, v_norm_mix_w, v_w_in, v_conv_w, v_conv_b, v_dt_bias, v_a_log, v_d_skip, v_ssd_norm_w, v_f_bias, v_w_out, v_norm_mlp_w, v_w_up, v_w_down, v_norm_final_w):
    me = 4 * lax.axis_index("x") + 2 * lax.axis_index("y") + lax.axis_index("c")
    conv_shard_w = CONV_CH // N_DEV
    zero = jnp.zeros((), jnp.int32)

    def place_conv(shard):
        return lax.dynamic_update_slice(jnp.zeros((CONV_K, CONV_CH), F32), shard[0], (zero, me * conv_shard_w))

    (g_in_t,) = _all_gather_two_level([w_in[0].T.astype(BF16)])
    late_shards = [w_out[0].astype(BF16), w_up[0].T.astype(BF16), w_down[0].astype(BF16)]
    conv_full = _small_gather(jnp.pad(place_conv(conv_w), ((0, SUBLANES - CONV_K), (0, 0))))[:CONV_K]

    loss_local, grad_x, g_small, recv = _local_step(
        x, loss_target, norm_mix_w, g_in_t.reshape(IN_WIDTH, D_MODEL), conv_full, conv_b, dt_bias, a_log, d_skip,
        ssd_norm_w, f_bias, late_shards, norm_mlp_w, norm_final_w)
    loss = lax.psum(loss_local, MESH_AXES)

    grad_in = _sum_parts(recv["w_in_t"], "sum_w_in").T[None]
    grad_up = _sum_parts(recv["w_up_t"], "sum_w_up").T[None]
    big = {
        "w_in": _adamw_parts(grad_in, w_in[0], m_w_in[0], v_w_in[0], "adamw_w_in"),
        "w_out": _adamw_parts(recv["w_out"], w_out[0], m_w_out[0], v_w_out[0], "adamw_w_out"),
        "w_up": _adamw_parts(grad_up, w_up[0], m_w_up[0], v_w_up[0], "adamw_w_up"),
        "w_down": _adamw_parts(recv["w_down"], w_down[0], m_w_down[0], v_w_down[0], "adamw_w_down"),
    }

    like = dict(norm_mix_w=norm_mix_w, norm_mlp_w=norm_mlp_w, norm_final_w=norm_final_w, ssd_norm_w=ssd_norm_w,
                conv_b=conv_b, conv_w=jnp.zeros((1, CONV_K, CONV_CH), F32), dt_bias=dt_bias, a_log=a_log,
                d_skip=d_skip, f_bias=f_bias)
    w_small = dict(like, conv_w=place_conv(conv_w))
    m_small = dict(norm_mix_w=m_norm_mix_w, norm_mlp_w=m_norm_mlp_w, norm_final_w=m_norm_final_w,
                   ssd_norm_w=m_ssd_norm_w, conv_b=m_conv_b, conv_w=place_conv(m_conv_w), dt_bias=m_dt_bias,
                   a_log=m_a_log, d_skip=m_d_skip, f_bias=m_f_bias)
    v_small = dict(norm_mix_w=v_norm_mix_w, norm_mlp_w=v_norm_mlp_w, norm_final_w=v_norm_final_w,
                   ssd_norm_w=v_ssd_norm_w, conv_b=v_conv_b, conv_w=place_conv(v_conv_w), dt_bias=v_dt_bias,
                   a_log=v_a_log, d_skip=v_d_skip, f_bias=v_f_bias)
    small = _small_reduce_adamw(_pack_small(g_small), _pack_small(w_small), _pack_small(m_small), _pack_small(v_small))
    small = [_unpack_small(s, like) for s in small]
    for s in small:
        s["conv_w"] = lax.dynamic_slice(s["conv_w"], (zero, zero, me * conv_shard_w), (1, CONV_K, conv_shard_w))

    order = ["norm_mix_w", "w_in", "conv_w", "conv_b", "dt_bias", "a_log", "d_skip", "ssd_norm_w", "f_bias", "w_out",
             "norm_mlp_w", "w_up", "w_down", "norm_final_w"]
    outs = [loss, grad_x]
    for kind in range(4):
        for name in order:
            outs.append(big[name][kind][None] if name in big else small[kind][name])
    return tuple(outs)
```

```python
import jax
import jax.numpy as jnp
from jax import lax
from jax.experimental import pallas as pl
from jax.experimental.pallas import tpu as pltpu

F32, BF16 = jnp.float32, jnp.bfloat16
HI = lax.Precision.HIGHEST

D_MODEL = 1024
SSD_HEADS = 16
HEAD_DIM = 64
SSD_STATE = 128
CHUNK = 128
CONV_CH = 1536
CONV_K = 4
D_FF = 4096
MIX_WIDTH = 2048
IN_WIDTH = 5664
NORM_EPS = 1e-5
ATT_SCALE = 0.125

LANES = 128
SUBLANES = 8
HEAD_PAIRS = SSD_HEADS // 2
NEG = -1e30
VMEM_LIMIT = 52 * 1024 * 1024

ROW_XBC, ROW_DT, ROW_Q, ROW_F = 1024, 2560, 2576, 5648
F_COL = SSD_HEADS

N_DEV = 8
MESH_AXES = ("x", "y", "c")
MESH = pl.DeviceIdType.MESH

ADAM_LR, ADAM_B1, ADAM_B2, ADAM_EPS, ADAM_WD, ADAM_STEP = 0.001, 0.9, 0.999, 1e-08, 0.01, 10
ADAM_C1 = 1.0 - ADAM_B1 ** ADAM_STEP
ADAM_C2 = 1.0 - ADAM_B2 ** ADAM_STEP

SMALL_COLS = CONV_CH


def _params(sem=None):
    return pltpu.CompilerParams(dimension_semantics=sem, vmem_limit_bytes=VMEM_LIMIT)


def _sigmoid(u):
    return 1.0 / (1.0 + jnp.exp(-u))


def _softplus(u):
    return jnp.maximum(u, 0.0) + jnp.log(1.0 + jnp.exp(-jnp.abs(u)))


def _log_sigmoid(u):
    return jnp.minimum(u, 0.0) - jnp.log(1.0 + jnp.exp(-jnp.abs(u)))


def _bdot(a, b):
    return jnp.dot(a.astype(BF16), b.astype(BF16), preferred_element_type=F32)


def _bdot_nt(a, b):
    return lax.dot_general(a.astype(BF16), b.astype(BF16), (((1,), (1,)), ((), ())), preferred_element_type=F32)


def _bdot_tn(a, b):
    return lax.dot_general(a.astype(BF16), b.astype(BF16), (((0,), (0,)), ((), ())), preferred_element_type=F32)


def _split3(x):
    x1 = x.astype(BF16)
    r1 = x - x1.astype(F32)
    x2 = r1.astype(BF16)
    return x1, x2, (r1 - x2.astype(F32)).astype(BF16)


def _dot_sel(x, sel):
    s = sel.astype(BF16)
    p1, p2, p3 = (jnp.dot(p, s, preferred_element_type=F32) for p in _split3(x))
    return p1 + p2 + p3


def _sel_dot(sel, x):
    s = sel.astype(BF16)
    p1, p2, p3 = (jnp.dot(s, p, preferred_element_type=F32) for p in _split3(x))
    return p1 + p2 + p3


def _iota(shape, dim):
    return lax.broadcasted_iota(jnp.int32, shape, dim)


def _head_expand():
    return (jnp.right_shift(_iota((LANES, D_MODEL), 1), 6) == _iota((LANES, D_MODEL), 0)).astype(F32)


def _head_reduce():
    return (jnp.right_shift(_iota((D_MODEL, LANES), 0), 6) == _iota((D_MODEL, LANES), 1)).astype(F32)


def _rms_fwd(x, w, name):
    n, d = x.shape
    tm = min(1024, n)

    def body(x_ref, w_ref, o_ref):
        xv = x_ref[...]
        r = lax.rsqrt(jnp.mean(xv * xv, axis=-1, keepdims=True) + NORM_EPS)
        o_ref[...] = (xv * r * w_ref[...]).astype(BF16)

    return pl.pallas_call(
        body, name=name, grid=(n // tm,),
        in_specs=[pl.BlockSpec((tm, d), lambda i: (i, 0)), pl.BlockSpec((1, d), lambda i: (0, 0))],
        out_specs=pl.BlockSpec((tm, d), lambda i: (i, 0)),
        out_shape=jax.ShapeDtypeStruct((n, d), BF16),
        compiler_params=_params(("parallel",)),
    )(x, w)


def _mm(pairs, name, out_dtype, tm, tn, nt=False, res=None, exchange=None, rms_fwd=None, rms_bwd=None):
    m = pairs[0][0].shape[0]
    n = pairs[0][1].shape[0 if nt else 1]
    tm, tn = min(tm, m), min(tn, n)
    gm, gn = m // tm, n // tn
    npairs = len(pairs)
    kind, xs = (None, []) if exchange is None else exchange
    nx = len(xs)
    extra_in = [] if res is None else [res]
    if rms_bwd is not None:
        extra_in += list(rms_bwd)
    elif rms_fwd is not None:
        extra_in.append(rms_fwd)
    n_in = 2 * npairs + len(extra_in)
    n_out = 1 + (rms_fwd is not None or rms_bwd is not None)
    assert (rms_fwd is None and rms_bwd is None) or tn == n

    def body(*refs):
        x_refs, o_refs = refs[n_in:n_in + nx], refs[n_in + nx:n_in + nx + n_out]
        xo_refs, sems = refs[n_in + nx + n_out:n_in + 2 * nx + n_out], refs[n_in + 2 * nx + n_out:]
        extra = refs[2 * npairs:n_in]
        i, j = pl.program_id(0), pl.program_id(1)
        if nx:
            @pl.when((i == 0) & (j == 0))
            def _():
                _exchange_start(kind, x_refs, xo_refs, *sems)

        acc = None
        for p in range(npairs):
            a_v, b_v = refs[2 * p][...], refs[2 * p + 1][...]
            d = _bdot_nt(a_v, b_v) if nt else _bdot(a_v, b_v)
            acc = d if acc is None else acc + d
        if rms_bwd is not None:
            xv = extra[1][...]
            r = lax.rsqrt(jnp.mean(xv * xv, axis=-1, keepdims=True) + NORM_EPS)
            nrm = xv * r
            g = acc * extra[2][...]
            o_refs[0][...] = extra[0][...] + r * (g - nrm * jnp.mean(g * nrm, axis=-1, keepdims=True))

            @pl.when(i == 0)
            def _():
                o_refs[1][...] = jnp.zeros_like(o_refs[1])

            o_refs[1][...] += jnp.sum(acc * nrm, axis=0, keepdims=True)
        else:
            if res is not None:
                acc = extra[0][...] + acc
            o_refs[0][...] = acc.astype(o_refs[0].dtype)
            if rms_fwd is not None:
                r = lax.rsqrt(jnp.mean(acc * acc, axis=-1, keepdims=True) + NORM_EPS)
                o_refs[1][...] = (acc * r * extra[-1][...]).astype(BF16)
        if nx:
            @pl.when((i == gm - 1) & (j == gn - 1))
            def _():
                _exchange_wait(kind, x_refs, xo_refs, *sems)

    tile = pl.BlockSpec((tm, tn), lambda i, j: (i, j))
    row = pl.BlockSpec((1, tn), lambda i, j: (0, j))
    in_specs, args = [], []
    for a, b in pairs:
        k = a.shape[1]
        in_specs.append(pl.BlockSpec((tm, k), lambda i, j: (i, 0)))
        in_specs.append(pl.BlockSpec((tn, k), lambda i, j: (j, 0)) if nt else pl.BlockSpec((k, tn), lambda i, j: (0, j)))
        args += [a, b]
    in_specs += [row if e.shape[0] == 1 else tile for e in extra_in]
    out_specs, out_shape = [tile], [jax.ShapeDtypeStruct((m, n), out_dtype)]
    if rms_bwd is not None:
        out_specs.append(row)
        out_shape.append(jax.ShapeDtypeStruct((1, n), F32))
    elif rms_fwd is not None:
        out_specs.append(tile)
        out_shape.append(jax.ShapeDtypeStruct((m, n), BF16))
    hbm = pl.BlockSpec(memory_space=pl.ANY)
    sequential = nx or rms_bwd is not None
    outs = pl.pallas_call(
        body, name=name, grid=(gm, gn), in_specs=in_specs + [hbm] * nx,
        out_specs=out_specs + [hbm] * nx,
        out_shape=out_shape + _exchange_shapes(kind, xs),
        scratch_shapes=_exchange_scratch(nx),
        compiler_params=_params(("arbitrary", "arbitrary") if sequential else ("parallel", "parallel")),
    )(*args, *extra_in, *xs)
    return outs if len(outs) > 1 else outs[0]


def _mm_tn(a, b, name, tm, tn, tk):
    t, m = a.shape
    n = b.shape[1]
    tm, tn, tk = min(tm, m), min(tn, n), min(tk, t)
    nk = t // tk

    def body(a_ref, b_ref, o_ref, acc_ref):
        kk = pl.program_id(2)

        @pl.when(kk == 0)
        def _():
            acc_ref[...] = jnp.zeros_like(acc_ref)

        acc_ref[...] += _bdot_tn(a_ref[...], b_ref[...])

        @pl.when(kk == nk - 1)
        def _():
            o_ref[...] = acc_ref[...].astype(o_ref.dtype)

    return pl.pallas_call(
        body, name=name, grid=(m // tm, n // tn, nk),
        in_specs=[pl.BlockSpec((tk, tm), lambda i, j, kk: (kk, i)), pl.BlockSpec((tk, tn), lambda i, j, kk: (kk, j))],
        out_specs=pl.BlockSpec((tm, tn), lambda i, j, kk: (i, j)),
        out_shape=jax.ShapeDtypeStruct((m, n), BF16),
        scratch_shapes=[pltpu.VMEM((tm, tn), F32)],
        compiler_params=_params(("parallel", "parallel", "arbitrary")),
    )(a, b)


def _mlp_down_loss(pre, w_down, h2, target, w):
    m, k = pre.shape
    d = w_down.shape[1]
    tm = min(512, m)

    def body(p_ref, b_ref, r_ref, t_ref, w_ref, dh_ref, loss_ref, gw_ref):
        u = jnp.square(jnp.maximum(p_ref[...].astype(F32), 0.0))
        xv = r_ref[...] + _bdot(u, b_ref[...])
        r = lax.rsqrt(jnp.mean(xv * xv, axis=-1, keepdims=True) + NORM_EPS)
        nrm = xv * r
        wv = w_ref[...]
        err = nrm * wv - t_ref[...]
        part = 0.5 * jnp.sum(jnp.mean(err * err, axis=-1, keepdims=True), axis=0, keepdims=True)
        dy = err * (1.0 / d)
        g = dy * wv
        dh_ref[...] = r * (g - nrm * jnp.mean(g * nrm, axis=-1, keepdims=True))

        @pl.when(pl.program_id(0) == 0)
        def _():
            loss_ref[...] = jnp.zeros_like(loss_ref)
            gw_ref[...] = jnp.zeros_like(gw_ref)

        loss_ref[...] += jnp.broadcast_to(part, loss_ref.shape)
        gw_ref[...] += jnp.sum(dy * nrm, axis=0, keepdims=True)

    tok = pl.BlockSpec((tm, d), lambda i: (i, 0))
    row = pl.BlockSpec((1, d), lambda i: (0, 0))
    return pl.pallas_call(
        body, name="mlp_down_loss", grid=(m // tm,),
        in_specs=[pl.BlockSpec((tm, k), lambda i: (i, 0)), pl.BlockSpec((k, d), lambda i: (0, 0)), tok, tok, row],
        out_specs=[tok, pl.BlockSpec((1, LANES), lambda i: (0, 0)), row],
        out_shape=[jax.ShapeDtypeStruct((m, d), F32), jax.ShapeDtypeStruct((1, LANES), F32),
                   jax.ShapeDtypeStruct((1, d), F32)],
        compiler_params=_params(("arbitrary",)),
    )(pre, w_down, h2, target, w)


def _mlp_down_bwd(dh3, w_down, pre):
    m, k = dh3.shape
    n = w_down.shape[0]
    tm, tn = min(256, m), n

    def body(a_ref, b_ref, p_ref, dpre_ref, u_ref):
        r = jnp.maximum(p_ref[...].astype(F32), 0.0)
        du = _bdot_nt(a_ref[...], b_ref[...])
        dpre_ref[...] = (du * (2.0 * r)).astype(BF16)
        u_ref[...] = (r * r).astype(BF16)

    blk = pl.BlockSpec((tm, tn), lambda i, j: (i, j))
    return pl.pallas_call(
        body, name="mlp_down_bwd", grid=(m // tm, n // tn),
        in_specs=[pl.BlockSpec((tm, k), lambda i, j: (i, 0)), pl.BlockSpec((tn, k), lambda i, j: (j, 0)), blk],
        out_specs=[blk, blk],
        out_shape=[jax.ShapeDtypeStruct((m, n), BF16), jax.ShapeDtypeStruct((m, n), BF16)],
        compiler_params=_params(("parallel", "parallel")),
    )(dh3, w_down, pre)


CONV_TC = 512


def _conv_fwd(xbc, cw, cb, bsz, seq):
    tt = min(512, seq)
    nt, hb = seq // tt, tt // SUBLANES
    x3 = xbc.reshape(bsz, seq, CONV_CH)

    def body(xm_ref, xh_ref, w_ref, b_ref, o_ref):
        i = pl.program_id(2)
        x = xm_ref[0]
        halo = jnp.where(i > 0, xh_ref[0], 0.0)
        xx = jnp.concatenate([halo, x], axis=0)
        acc = x * w_ref[3:4, :] + b_ref[...]
        for s in range(1, CONV_K):
            acc = acc + pltpu.roll(xx, s, 0)[SUBLANES:, :] * w_ref[3 - s:4 - s, :]
        o_ref[0] = acc * _sigmoid(acc)

    out = pl.pallas_call(
        body, name="conv_fwd", grid=(CONV_CH // CONV_TC, bsz, nt),
        in_specs=[pl.BlockSpec((1, tt, CONV_TC), lambda c, b, i: (b, i, c)),
                  pl.BlockSpec((1, SUBLANES, CONV_TC), lambda c, b, i: (b, jnp.maximum(i * hb - 1, 0), c)),
                  pl.BlockSpec((CONV_K, CONV_TC), lambda c, b, i: (0, c)),
                  pl.BlockSpec((1, CONV_TC), lambda c, b, i: (0, c))],
        out_specs=pl.BlockSpec((1, tt, CONV_TC), lambda c, b, i: (b, i, c)),
        out_shape=jax.ShapeDtypeStruct((bsz, seq, CONV_CH), F32),
        compiler_params=_params(("parallel", "parallel", "parallel")),
    )(x3, x3, cw, cb)
    return out.reshape(bsz * seq, CONV_CH)


def _conv_bwd(da, xbc, cw, cb, bsz, seq):
    tt = min(512, seq)
    nt, hb = seq // tt, tt // SUBLANES
    x3 = xbc.reshape(bsz, seq, CONV_CH)
    da3 = da.reshape(bsz, seq, CONV_CH)
    n2 = tt + SUBLANES

    def body(dam_ref, daa_ref, xm_ref, xb_ref, xa_ref, w_ref, b_ref, du_ref, gw_ref, gb_ref):
        bi, i = pl.program_id(1), pl.program_id(2)
        before = jnp.where(i > 0, xb_ref[0], 0.0)
        after = jnp.where(i < nt - 1, xa_ref[0], 0.0)
        xx = jnp.concatenate([before, xm_ref[0], after], axis=0)
        rolled = [xx] + [pltpu.roll(xx, s, 0) for s in range(1, CONV_K)]
        pre = b_ref[...]
        for s in range(CONV_K):
            pre = pre + rolled[s][SUBLANES:, :] * w_ref[3 - s:4 - s, :]
        da_ext = jnp.concatenate([dam_ref[0], jnp.where(i < nt - 1, daa_ref[0], 0.0)], axis=0)
        sg = _sigmoid(pre)
        dpre = da_ext * sg * (1.0 + pre * (1.0 - sg))
        du = dpre[:tt, :] * w_ref[3:4, :]
        for s in range(1, CONV_K):
            du = du + pltpu.roll(dpre, n2 - s, 0)[:tt, :] * w_ref[3 - s:4 - s, :]
        du_ref[0] = du.astype(BF16)
        dpm = dpre[:tt, :]
        gws = [jnp.sum(dpm * rolled[3 - kk][SUBLANES:SUBLANES + tt, :], axis=0, keepdims=True) for kk in range(CONV_K)]

        @pl.when((bi == 0) & (i == 0))
        def _():
            gw_ref[...] = jnp.zeros_like(gw_ref)
            gb_ref[...] = jnp.zeros_like(gb_ref)

        gw_ref[...] += jnp.concatenate(gws, axis=0)
        gb_ref[...] += jnp.sum(dpm, axis=0, keepdims=True)

    main = pl.BlockSpec((1, tt, CONV_TC), lambda c, b, i: (b, i, c))
    prev = pl.BlockSpec((1, SUBLANES, CONV_TC), lambda c, b, i: (b, jnp.maximum(i * hb - 1, 0), c))
    nxt = pl.BlockSpec((1, SUBLANES, CONV_TC), lambda c, b, i: (b, jnp.minimum((i + 1) * hb, seq // SUBLANES - 1), c))
    du, gw, gb = pl.pallas_call(
        body, name="conv_bwd", grid=(CONV_CH // CONV_TC, bsz, nt),
        in_specs=[main, nxt, main, prev, nxt,
                  pl.BlockSpec((CONV_K, CONV_TC), lambda c, b, i: (0, c)),
                  pl.BlockSpec((1, CONV_TC), lambda c, b, i: (0, c))],
        out_specs=[main, pl.BlockSpec((CONV_K, CONV_TC), lambda c, b, i: (0, c)),
                   pl.BlockSpec((1, CONV_TC), lambda c, b, i: (0, c))],
        out_shape=[jax.ShapeDtypeStruct((bsz, seq, CONV_CH), BF16), jax.ShapeDtypeStruct((CONV_K, CONV_CH), F32),
                   jax.ShapeDtypeStruct((1, CONV_CH), F32)],
        compiler_params=_params(("parallel", "arbitrary", "arbitrary")),
    )(da3, da3, x3, x3, x3, cw, cb)
    return du.reshape(bsz * seq, CONV_CH), gw, gb


def _ssd_prologue(dtf_ref, dtft_ref, bias_ref, biast_ref, arow_ref, acol_ref, dtfull_scr, acfull_scr, acr_scr):
    tri = (_iota((CHUNK, CHUNK), 1) <= _iota((CHUNK, CHUNK), 0)).astype(F32)
    triu = (_iota((CHUNK, CHUNK), 0) <= _iota((CHUNK, CHUNK), 1)).astype(F32)
    dtc = _softplus(dtf_ref[0] + bias_ref[...])
    a = dtc * arow_ref[...]
    acum = _sel_dot(tri, a)
    expand = _head_expand()
    dtfull_scr[...] = _dot_sel(dtc, expand)
    acfull_scr[...] = _dot_sel(acum, expand)
    dtr = _softplus(dtft_ref[0] + biast_ref[...])
    acr_scr[...] = _dot_sel(dtr * acol_ref[...], triu)
    return dtc, acum


def _decay_matrix(acum, acr_scr, h):
    lane = _iota((CHUNK, LANES), 1)
    ac_col = jnp.sum(jnp.where(lane == h, acum, 0.0), axis=1, keepdims=True)
    ac_row = acr_scr[h:h + 1, :]
    causal = _iota((CHUNK, CHUNK), 1) <= _iota((CHUNK, CHUNK), 0)
    return jnp.exp(jnp.where(causal, ac_col - ac_row, NEG))


def _ssd_fwd(xbc, dtf, dtft, bias, biast, arow, acol, dfull, bsz, seq):
    nc = seq // CHUNK
    x3 = xbc.reshape(bsz, seq, CONV_CH)

    def body(xbc_ref, dtf_ref, dtft_ref, bias_ref, biast_ref, arow_ref, acol_ref, dfull_ref,
             y_ref, hall_ref, h_scr, dtfull_scr, acfull_scr, acr_scr):
        @pl.when(pl.program_id(1) == 0)
        def _():
            h_scr[...] = jnp.zeros_like(h_scr)

        _, acum = _ssd_prologue(dtf_ref, dtft_ref, bias_ref, biast_ref, arow_ref, acol_ref,
                                dtfull_scr, acfull_scr, acr_scr)
        lane = _iota((CHUNK, LANES), 1)
        for g in range(2):
            bg = xbc_ref[0, :, D_MODEL + LANES * g:D_MODEL + LANES * (g + 1)]
            cg = xbc_ref[0, :, D_MODEL + 2 * LANES + LANES * g:D_MODEL + 2 * LANES + LANES * (g + 1)]
            cg_bf = cg.astype(BF16)
            gmat = _bdot_nt(cg_bf, bg)
            bgt_bf = bg.T.astype(BF16)
            for j in range(4):
                p = 4 * g + j
                sl = slice(LANES * p, LANES * (p + 1))
                xs = xbc_ref[0, :, sl]
                ac = acfull_scr[:, sl]
                acl = acfull_scr[CHUNK - 1:CHUNK, sl]
                xdt = xs * dtfull_scr[:, sl]
                xdt_bf = xdt.astype(BF16)
                hp = h_scr[p]
                hall_ref[0, 0, p] = hp
                yo = _bdot(cg_bf, hp) * jnp.exp(ac)
                yd = []
                for hh in range(2):
                    mmat = gmat * _decay_matrix(acum, acr_scr, 2 * p + hh)
                    yd.append(_bdot(mmat, xdt_bf))
                y_ref[0, :, sl] = jnp.where(lane < HEAD_DIM, yd[0], yd[1]) + yo + dfull_ref[:, sl] * xs
                h_scr[p] = hp * jnp.exp(acl) + _bdot(bgt_bf, xdt * jnp.exp(acl - ac))

    row = lambda w: pl.BlockSpec((1, w), lambda b, c: (0, 0))
    y, hall = pl.pallas_call(
        body, name="ssd_fwd", grid=(bsz, nc),
        in_specs=[pl.BlockSpec((1, CHUNK, CONV_CH), lambda b, c: (b, c, 0)),
                  pl.BlockSpec((1, CHUNK, LANES), lambda b, c: (b, c, 0)),
                  pl.BlockSpec((1, LANES, CHUNK), lambda b, c: (b, 0, c)),
                  row(LANES), pl.BlockSpec((LANES, 1), lambda b, c: (0, 0)),
                  row(LANES), pl.BlockSpec((LANES, 1), lambda b, c: (0, 0)), row(D_MODEL)],
        out_specs=[pl.BlockSpec((1, CHUNK, D_MODEL), lambda b, c: (b, c, 0)),
                   pl.BlockSpec((1, 1, HEAD_PAIRS, SSD_STATE, LANES), lambda b, c: (b, c, 0, 0, 0))],
        out_shape=[jax.ShapeDtypeStruct((bsz, seq, D_MODEL), F32),
                   jax.ShapeDtypeStruct((bsz, nc, HEAD_PAIRS, SSD_STATE, LANES), F32)],
        scratch_shapes=[pltpu.VMEM((HEAD_PAIRS, SSD_STATE, LANES), F32), pltpu.VMEM((CHUNK, D_MODEL), F32),
                        pltpu.VMEM((CHUNK, D_MODEL), F32), pltpu.VMEM((LANES, CHUNK), F32)],
        compiler_params=_params(("parallel", "arbitrary")),
    )(x3, dtf.reshape(bsz, seq, LANES), dtft, bias, biast, arow, acol, dfull)
    return y.reshape(bsz * seq, D_MODEL), hall


def _ssd_bwd(dy, xbc, dtf, dtft, bias, biast, arow, acol, dfull, hall, bsz, seq):
    nc = seq // CHUNK
    x3 = xbc.reshape(bsz, seq, CONV_CH)
    dy3 = dy.reshape(bsz, seq, D_MODEL)

    def body(dy_ref, xbc_ref, dtf_ref, dtft_ref, bias_ref, biast_ref, arow_ref, acol_ref, dfull_ref, hall_ref,
             dx_ref, ddt_ref, acc_ref, dh_scr, dtfull_scr, acfull_scr, acr_scr, csum_scr, dacf_scr, ddtf_scr, ddl_scr):
        first = (pl.program_id(0) == 0) & (pl.program_id(1) == 0)

        @pl.when(first)
        def _():
            acc_ref[...] = jnp.zeros_like(acc_ref)

        @pl.when(pl.program_id(1) == 0)
        def _():
            dh_scr[...] = jnp.zeros_like(dh_scr)

        dtc, acum = _ssd_prologue(dtf_ref, dtft_ref, bias_ref, biast_ref, arow_ref, acol_ref,
                                  dtfull_scr, acfull_scr, acr_scr)
        csum_scr[...] = jnp.zeros_like(csum_scr)
        lane = _iota((CHUNK, LANES), 1)
        last_row = _iota((CHUNK, LANES), 0) == CHUNK - 1
        rs16 = jnp.zeros((CHUNK, LANES), F32)
        for g in range(2):
            bsl = slice(D_MODEL + LANES * g, D_MODEL + LANES * (g + 1))
            csl = slice(D_MODEL + 2 * LANES + LANES * g, D_MODEL + 2 * LANES + LANES * (g + 1))
            bg_bf = xbc_ref[0, :, bsl].astype(BF16)
            cg = xbc_ref[0, :, csl]
            cg_bf = cg.astype(BF16)
            cgt_bf = cg.T.astype(BF16)
            gmat = _bdot_nt(cg_bf, bg_bf)
            dg = jnp.zeros((CHUNK, CHUNK), F32)
            dbg = jnp.zeros((CHUNK, SSD_STATE), F32)
            dcg = jnp.zeros((CHUNK, SSD_STATE), F32)
            for j in range(4):
                p = 4 * g + j
                sl = slice(LANES * p, LANES * (p + 1))
                xs = xbc_ref[0, :, sl]
                dt = dtfull_scr[:, sl]
                ac = acfull_scr[:, sl]
                acl = acfull_scr[CHUNK - 1:CHUNK, sl]
                dyp = dy_ref[0, :, sl]
                xdt = xs * dt
                xdt_bf = xdt.astype(BF16)
                e = jnp.exp(ac)
                dsd = jnp.exp(acl - ac)
                cd = jnp.exp(acl)
                xds_bf = (xdt * dsd).astype(BF16)
                hp = hall_ref[0, 0, p]
                hp_bf = hp.astype(BF16)
                dhn = dh_scr[p]
                dhn_bf = dhn.astype(BF16)
                yo = _bdot(cg_bf, hp_bf) * e
                dw_bf = (dyp * e).astype(BF16)
                dcg = dcg + _bdot_nt(dw_bf, hp_bf)
                dhin = _bdot(cgt_bf, dw_bf)
                dac = dyp * yo
                dacl = jnp.sum(dhn * hp, axis=0, keepdims=True) * cd
                dxds = _bdot(bg_bf, dhn_bf)
                dbg = dbg + _bdot_nt(xds_bf, dhn_bf)
                dxdt = dxds * dsd
                tdec = dxds * xdt * dsd
                dacl = dacl + jnp.sum(tdec, axis=0, keepdims=True)
                dac = dac - tdec
                dh_scr[p] = dhn * cd + dhin
                for hh in range(2):
                    h = 2 * p + hh
                    lm = (lane < HEAD_DIM) if hh == 0 else (lane >= HEAD_DIM)
                    dec = _decay_matrix(acum, acr_scr, h)
                    mmat = gmat * dec
                    dyh_bf = jnp.where(lm, dyp, 0.0).astype(BF16)
                    dm = _bdot_nt(dyh_bf, xdt_bf)
                    dxdt = dxdt + _bdot(mmat.T, dyh_bf)
                    dg = dg + dm * dec
                    q = dm * mmat
                    rs16 = rs16 + jnp.where(lane == h, jnp.sum(q, axis=1, keepdims=True), 0.0)
                    csum_scr[h:h + 1, :] = jnp.sum(q, axis=0, keepdims=True)
                dx_ref[0, :, sl] = dfull_ref[:, sl] * dyp + dxdt * dt
                dacf_scr[:, sl] = dac + jnp.where(last_row, dacl, 0.0)
                ddtf_scr[:, sl] = dxdt * xs
                ddl_scr[:, sl] = jnp.broadcast_to(jnp.sum(dyp * xs, axis=0, keepdims=True), (SUBLANES, LANES))
            dg_bf = dg.astype(BF16)
            dx_ref[0, :, bsl] = dbg + _bdot(dg.T, cg_bf)
            dx_ref[0, :, csl] = dcg + _bdot(dg_bf, bg_bf)
        reduce = _head_reduce()
        triu = (_iota((CHUNK, CHUNK), 0) <= _iota((CHUNK, CHUNK), 1)).astype(F32)
        dac16 = _dot_sel(dacf_scr[...], reduce) + rs16 - csum_scr[...].T
        da16 = _sel_dot(triu, dac16)
        ddt16 = da16 * arow_ref[...] + _dot_sel(ddtf_scr[...], reduce)
        ddtraw = ddt16 * _sigmoid(dtf_ref[0] + bias_ref[...])
        ddt_ref[0] = ddtraw
        acc_ref[0:8, :] += jnp.broadcast_to(jnp.sum(da16 * dtc * arow_ref[...], axis=0, keepdims=True), (SUBLANES, LANES))
        acc_ref[8:16, :] += _dot_sel(ddl_scr[...], reduce)
        acc_ref[16:24, :] += jnp.broadcast_to(jnp.sum(ddtraw, axis=0, keepdims=True), (SUBLANES, LANES))

    rev = lambda b, c: (b, nc - 1 - c, 0)
    row = lambda w: pl.BlockSpec((1, w), lambda b, c: (0, 0))
    dx, ddt, acc = pl.pallas_call(
        body, name="ssd_bwd", grid=(bsz, nc),
        in_specs=[pl.BlockSpec((1, CHUNK, D_MODEL), rev), pl.BlockSpec((1, CHUNK, CONV_CH), rev),
                  pl.BlockSpec((1, CHUNK, LANES), rev),
                  pl.BlockSpec((1, LANES, CHUNK), lambda b, c: (b, 0, nc - 1 - c)),
                  row(LANES), pl.BlockSpec((LANES, 1), lambda b, c: (0, 0)),
                  row(LANES), pl.BlockSpec((LANES, 1), lambda b, c: (0, 0)), row(D_MODEL),
                  pl.BlockSpec((1, 1, HEAD_PAIRS, SSD_STATE, LANES), lambda b, c: (b, nc - 1 - c, 0, 0, 0))],
        out_specs=[pl.BlockSpec((1, CHUNK, CONV_CH), rev), pl.BlockSpec((1, CHUNK, LANES), rev),
                   pl.BlockSpec((3 * SUBLANES, LANES), lambda b, c: (0, 0))],
        out_shape=[jax.ShapeDtypeStruct((bsz, seq, CONV_CH), F32), jax.ShapeDtypeStruct((bsz, seq, LANES), F32),
                   jax.ShapeDtypeStruct((3 * SUBLANES, LANES), F32)],
        scratch_shapes=[pltpu.VMEM((HEAD_PAIRS, SSD_STATE, LANES), F32), pltpu.VMEM((CHUNK, D_MODEL), F32),
                        pltpu.VMEM((CHUNK, D_MODEL), F32), pltpu.VMEM((LANES, CHUNK), F32),
                        pltpu.VMEM((LANES, CHUNK), F32), pltpu.VMEM((CHUNK, D_MODEL), F32),
                        pltpu.VMEM((CHUNK, D_MODEL), F32), pltpu.VMEM((SUBLANES, D_MODEL), F32)],
        compiler_params=_params(("arbitrary", "arbitrary")),
    )(dy3, x3, dtf.reshape(bsz, seq, LANES), dtft, bias, biast, arow, acol, dfull, hall)
    return dx.reshape(bsz * seq, CONV_CH), ddt.reshape(bsz * seq, LANES), acc


GROUP_W = D_MODEL // 2


def _gnorm_fwd(y, z, o_att, w):
    n = y.shape[0]
    tm = min(1024, n)

    def body(y_ref, z_ref, o_ref, w_ref, out_ref):
        zv = z_ref[...]
        g = y_ref[...] * (zv * _sigmoid(zv))
        for gi in range(2):
            sl = slice(GROUP_W * gi, GROUP_W * (gi + 1))
            gs = g[:, sl]
            r = lax.rsqrt(jnp.mean(gs * gs, axis=-1, keepdims=True) + NORM_EPS)
            out_ref[:, sl] = (gs * r * w_ref[:, sl]).astype(BF16)
        out_ref[:, D_MODEL:] = o_ref[...].astype(BF16)

    tok = pl.BlockSpec((tm, D_MODEL), lambda i: (i, 0))
    return pl.pallas_call(
        body, name="gnorm_fwd", grid=(n // tm,),
        in_specs=[tok, tok, tok, pl.BlockSpec((1, D_MODEL), lambda i: (0, 0))],
        out_specs=pl.BlockSpec((tm, MIX_WIDTH), lambda i: (i, 0)),
        out_shape=jax.ShapeDtypeStruct((n, MIX_WIDTH), BF16),
        compiler_params=_params(("parallel",)),
    )(y, z, o_att, w)


def _gnorm_bwd(dycat, y, z, w):
    n = y.shape[0]
    tm = min(512, n)

    def body(dn_ref, y_ref, z_ref, w_ref, dy_ref, dz_ref, gw_ref):
        zv, yv = z_ref[...], y_ref[...]
        sz = _sigmoid(zv)
        sil = zv * sz
        g = yv * sil

        @pl.when(pl.program_id(0) == 0)
        def _():
            gw_ref[...] = jnp.zeros_like(gw_ref)

        for gi in range(2):
            sl = slice(GROUP_W * gi, GROUP_W * (gi + 1))
            gs = g[:, sl]
            r = lax.rsqrt(jnp.mean(gs * gs, axis=-1, keepdims=True) + NORM_EPS)
            nrm = gs * r
            dyn = dn_ref[:, sl]
            dn = dyn * w_ref[:, sl]
            dgs = r * (dn - nrm * jnp.mean(dn * nrm, axis=-1, keepdims=True))
            dy_ref[:, sl] = dgs * sil[:, sl]
            dz_ref[:, sl] = (dgs * yv[:, sl] * (sz[:, sl] * (1.0 + zv[:, sl] * (1.0 - sz[:, sl])))).astype(BF16)
            gw_ref[:, sl] += jnp.sum(dyn * nrm, axis=0, keepdims=True)

    tok = pl.BlockSpec((tm, D_MODEL), lambda i: (i, 0))
    row = pl.BlockSpec((1, D_MODEL), lambda i: (0, 0))
    return pl.pallas_call(
        body, name="gnorm_bwd", grid=(n // tm,),
        in_specs=[tok, tok, tok, row], out_specs=[tok, tok, row],
        out_shape=[jax.ShapeDtypeStruct((n, D_MODEL), F32), jax.ShapeDtypeStruct((n, D_MODEL), BF16),
                   jax.ShapeDtypeStruct((1, D_MODEL), F32)],
        compiler_params=_params(("arbitrary",)),
    )(dycat, y, z, w)


AUX_C = 3
AUX_ONE = 3


def _aux_base(hh):
    return HEAD_DIM * (1 - hh)


def _fox_prep(dtf, fb, bsz, seq):
    def body(x_ref, b_ref, aux_ref):
        tri = (_iota((CHUNK, CHUNK), 1) <= _iota((CHUNK, CHUNK), 0)).astype(F32)
        row, col = _iota((LANES, D_MODEL), 0), _iota((LANES, D_MODEL), 1)
        lane = jnp.bitwise_and(col, LANES - 1)
        other = (lane < HEAD_DIM).astype(jnp.int32)
        term = lane - HEAD_DIM * (1 - other)
        src = F_COL + 2 * jnp.right_shift(col, 7) + other
        place = [jnp.where((row == src) & (term == i), -1.0, 0.0).astype(BF16) for i in range(AUX_C)]
        lane1 = jnp.bitwise_and(_iota((1, D_MODEL), 1), LANES - 1)
        ones_lane = (lane1 - HEAD_DIM * (1 - (lane1 < HEAD_DIM).astype(jnp.int32)) == AUX_ONE).astype(F32)
        carry = jnp.zeros((1, LANES), F32)
        for j in range(seq // CHUNK):
            sl = slice(CHUNK * j, CHUNK * (j + 1))
            lf = _log_sigmoid(x_ref[sl, :] + b_ref[...])
            c = _sel_dot(tri, lf) + carry
            carry = carry + jnp.sum(lf, axis=0, keepdims=True)
            t1, t2, t3 = (jnp.dot(t, p, preferred_element_type=F32) for t, p in zip(_split3(c), place))
            aux_ref[sl, :] = (t1 + t2 + t3 + ones_lane).astype(BF16)

    return pl.pallas_call(
        body, name="fox_prep", grid=(bsz,),
        in_specs=[pl.BlockSpec((seq, LANES), lambda b: (b, 0)), pl.BlockSpec((1, LANES), lambda b: (0, 0))],
        out_specs=pl.BlockSpec((seq, D_MODEL), lambda b: (b, 0)),
        out_shape=jax.ShapeDtypeStruct((bsz * seq, D_MODEL), BF16),
        compiler_params=_params(("parallel",)),
    )(dtf, fb)


def _fox_prep_bwd(dck, dcq, ddt, dtf, fb, bsz, seq):
    nj = seq // CHUNK

    def body(dck_ref, dcq_ref, ddt_ref, x_ref, b_ref, out_ref, gb_ref):
        triu = (_iota((CHUNK, CHUNK), 0) <= _iota((CHUNK, CHUNK), 1)).astype(F32)
        is_f = (_iota((CHUNK, LANES), 1) >= F_COL) & (_iota((CHUNK, LANES), 1) < 2 * F_COL)
        carry = jnp.zeros((1, LANES), F32)
        total = jnp.zeros((1, LANES), F32)
        for j in range(nj - 1, -1, -1):
            sl = slice(CHUNK * j, CHUNK * (j + 1))
            dcv = dck_ref[sl, :] + dcq_ref[sl, :]
            dlf = _sel_dot(triu, dcv) + carry
            carry = carry + jnp.sum(dcv, axis=0, keepdims=True)
            df = jnp.where(is_f, dlf * _sigmoid(-(x_ref[sl, :] + b_ref[...])), 0.0)
            out_ref[sl, :] = (df + ddt_ref[sl, :]).astype(BF16)
            total = total + jnp.sum(df, axis=0, keepdims=True)

        @pl.when(pl.program_id(0) == 0)
        def _():
            gb_ref[...] = jnp.zeros_like(gb_ref)

        gb_ref[...] += jnp.broadcast_to(total, (SUBLANES, LANES))

    blk = pl.BlockSpec((seq, LANES), lambda b: (b, 0))
    return pl.pallas_call(
        body, name="fox_prep_bwd", grid=(bsz,),
        in_specs=[blk, blk, blk, blk, pl.BlockSpec((1, LANES), lambda b: (0, 0))],
        out_specs=[blk, pl.BlockSpec((SUBLANES, LANES), lambda b: (0, 0))],
        out_shape=[jax.ShapeDtypeStruct((bsz * seq, LANES), BF16), jax.ShapeDtypeStruct((SUBLANES, LANES), F32)],
        compiler_params=_params(("arbitrary",)),
    )(dck, dcq, ddt, dtf, fb)


ATT_BLOCK_FWD = 512
ATT_BLOCK_KEYS = 256
ATT_BLOCK_BWD = 256


def _att_operands(q_ref, k_ref, aux_ref, hh):
    lane = _iota((1, LANES), 1)
    lm = (lane < HEAD_DIM) if hh == 0 else (lane >= HEAD_DIM)
    base = _aux_base(hh)
    zero = jnp.zeros((1, LANES), BF16)
    ka = jnp.where(lm, k_ref[...], jnp.where((lane >= base) & (lane <= base + AUX_ONE), aux_ref[...], zero))
    qa = jnp.where(lm, q_ref[...] * ATT_SCALE,
                   jnp.where((lane >= base) & (lane < base + AUX_C), jnp.ones((1, LANES), BF16), zero))
    return lm, base, qa, ka


def _att_fwd(qkv, ccol, bsz, seq, gather):
    bq, bk = min(ATT_BLOCK_FWD, seq), min(ATT_BLOCK_KEYS, seq)
    nq, ratio = seq // bq, bq // bk
    nx = len(gather)

    def body(*refs):
        q_ref, k_ref, v_ref, c_ref = refs[:4]
        x_refs = refs[4:4 + nx]
        o_ref, lse_ref = refs[4 + nx:6 + nx]
        xo_refs = refs[6 + nx:6 + 2 * nx]
        qa_scr, ka_scr, vt_scr = refs[6 + 2 * nx:9 + 2 * nx]
        sems = refs[9 + 2 * nx:]
        pair = pl.program_id(1)

        @pl.when((pl.program_id(0) == 0) & (pair == 0))
        def _():
            _exchange_start("gather", x_refs, xo_refs, *sems)

        lse_ref[...] = jnp.zeros_like(lse_ref)
        for hh in range(2):
            _, _, qa, ka = _att_operands(q_ref, k_ref, c_ref, hh)
            qa_scr[hh] = qa
            ka_scr[hh] = ka
        for t0 in range(0, seq, bq):
            vt_scr[:, t0:t0 + bq] = v_ref[t0:t0 + bq, :].astype(F32).T.astype(BF16)
        key_minus_query = _iota((bk, bq), 0) - _iota((bk, bq), 1)

        def q_step(i, carry0):
            qrows = pl.ds(pl.multiple_of(i * bq, bq), bq)

            def scores(j):
                krows = pl.ds(pl.multiple_of(j * bk, bk), bk)
                return tuple(_bdot_nt(ka_scr[hh, krows, :], qa_scr[hh, qrows, :]) for hh in range(2))

            def update(state, st_pair, j, diag):
                krows = pl.ds(pl.multiple_of(j * bk, bk), bk)
                out = []
                for hh in range(2):
                    m, l, acc = state[hh]
                    st = st_pair[hh]
                    if diag is not None:
                        st = jnp.where(key_minus_query + diag * bk <= 0, st, NEG)
                    mn = jnp.maximum(m, jnp.max(st, axis=0, keepdims=True))
                    alpha = jnp.exp(m - mn)
                    pt = jnp.exp(st - mn)
                    l = alpha * l + jnp.sum(pt, axis=0, keepdims=True)
                    vt = vt_scr[HEAD_DIM * hh:HEAD_DIM * (hh + 1), krows]
                    out.append((mn, l, alpha * acc + _bdot(vt, pt)))
                return tuple(out)

            def step(j, carry):
                state, s_cur = carry
                s_next = scores(j + 1)
                return update(state, s_cur, j, None), s_next

            one = (jnp.full((1, bq), NEG, F32), jnp.zeros((1, bq), F32), jnp.zeros((HEAD_DIM, bq), F32))
            first_diag = i * ratio
            state, s_cur = lax.fori_loop(0, first_diag, step, ((one, one), scores(0)))
            for r in range(ratio):
                s_next = scores(first_diag + r + 1) if r + 1 < ratio else None
                state = update(state, s_cur, first_diag + r, r)
                s_cur = s_next
            (m0, l0, acc0), (m1, l1, acc1) = state
            ot = jnp.concatenate([acc0 * (1.0 / l0), acc1 * (1.0 / l1)], axis=0)
            o_ref[qrows, :] = ot.T
            lse_ref[0, 0, 0:1, qrows] = m0 + jnp.log(l0)
            lse_ref[0, 0, 1:2, qrows] = m1 + jnp.log(l1)
            return carry0

        lax.fori_loop(0, nq, q_step, 0)

        @pl.when((pl.program_id(0) == bsz - 1) & (pair == HEAD_PAIRS - 1))
        def _():
            _exchange_wait("gather", x_refs, xo_refs, *sems)

    col = lambda off: pl.BlockSpec((seq, LANES), lambda b, p: (b, off + p))
    head2 = pltpu.VMEM((2, seq, LANES), BF16)
    hbm = pl.BlockSpec(memory_space=pl.ANY)
    return pl.pallas_call(
        body, name="att_fwd", grid=(bsz, HEAD_PAIRS),
        in_specs=[col(0), col(HEAD_PAIRS), col(2 * HEAD_PAIRS), col(0)] + [hbm] * nx,
        out_specs=[pl.BlockSpec((seq, LANES), lambda b, p: (b, p)),
                   pl.BlockSpec((1, 1, SUBLANES, seq), lambda b, p: (b, p, 0, 0))] + [hbm] * nx,
        out_shape=[jax.ShapeDtypeStruct((bsz * seq, D_MODEL), F32),
                   jax.ShapeDtypeStruct((bsz, HEAD_PAIRS, SUBLANES, seq), F32)] + _exchange_shapes("gather", gather),
        scratch_shapes=[head2, head2, pltpu.VMEM((LANES, seq), BF16)] + _exchange_scratch(nx),
        compiler_params=_params(("arbitrary", "arbitrary")),
    )(qkv, qkv, qkv, ccol, *gather)


def _att_bwd(qkv, dycat, o, lse, ccol, bsz, seq, scatter):
    blk = min(ATT_BLOCK_BWD, seq)
    nb = seq // blk
    nx = len(scatter)

    def body(*refs):
        q_ref, k_ref, v_ref, do_ref, o_ref, lse_ref, c_ref = refs[:7]
        x_refs = refs[7:7 + nx]
        dq_ref, dk_ref, dv_ref, dck_ref, dcq_ref = refs[7 + nx:12 + nx]
        xo_refs = refs[12 + nx:12 + 2 * nx]
        qa_scr, ka_scr, va_scr, doa_scr, kat_scr, d_scr, dqt_scr, dk_scr, dv_scr = refs[12 + 2 * nx:21 + 2 * nx]
        sems = refs[21 + 2 * nx:]
        pair = pl.program_id(1)

        @pl.when((pl.program_id(0) == 0) & (pair == 0))
        def _():
            _exchange_start("scatter", x_refs, xo_refs, *sems)

        causal_t = _iota((blk, blk), 1) >= _iota((blk, blk), 0)
        lane_b = _iota((blk, LANES), 1)
        row_t = _iota((LANES, seq), 0)
        masks, bases = [], []
        ones8 = jnp.ones((SUBLANES, LANES), F32)
        for hh in range(2):
            lm, base, qa, ka = _att_operands(q_ref, k_ref, c_ref, hh)
            masks.append(lm)
            bases.append(base)
            qa_scr[hh] = qa
            ka_scr[hh] = ka
            va_scr[hh] = jnp.where(lm, v_ref[...], jnp.zeros((seq, LANES), BF16))
            doa = jnp.where(lm, do_ref[...], 0.0).astype(BF16)
            doa_scr[hh] = doa
            for t0 in range(0, seq, blk):
                kat_scr[hh, :, t0:t0 + blk] = ka[t0:t0 + blk, :].astype(F32).T.astype(BF16)
            d1, d2, d3 = (_bdot_nt(ones8, term) for term in _split3(doa.astype(F32) * o_ref[...]))
            d_scr[hh] = d1 + d2 + d3
        dqt_scr[...] = jnp.zeros_like(dqt_scr)

        @pl.when(pair == 0)
        def _():
            dck_ref[...] = jnp.zeros_like(dck_ref)

        def kv_step(j, carry0):
            krows = pl.ds(pl.multiple_of(j * blk, blk), blk)
            dk_scr[...] = jnp.zeros_like(dk_scr)
            dv_scr[...] = jnp.zeros_like(dv_scr)

            def scores(i):
                rows = pl.ds(pl.multiple_of(i * blk, blk), blk)
                return tuple((_bdot_nt(ka_scr[hh, krows, :], qa_scr[hh, rows, :]),
                              _bdot_nt(va_scr[hh, krows, :], doa_scr[hh, rows, :])) for hh in range(2))

            def update(i, sd, masked):
                rows = pl.ds(pl.multiple_of(i * blk, blk), blk)
                for hh in range(2):
                    st, dpt = sd[hh]
                    if masked:
                        st = jnp.where(causal_t, st, NEG)
                    pt = jnp.exp(st - lse_ref[0, 0, hh:hh + 1, rows])
                    dst = (pt * (dpt - d_scr[hh, 0:1, rows])).astype(BF16)
                    dv_scr[hh] += _bdot(pt, doa_scr[hh, rows, :])
                    dk_scr[hh] += _bdot(dst, qa_scr[hh, rows, :])
                    dqt_scr[hh, :, rows] += _bdot(kat_scr[hh, :, krows], dst)

            def q_step(i, sd_cur):
                sd_next = scores(jnp.minimum(i + 1, nb - 1))
                update(i, sd_cur, False)
                return sd_next

            sd_diag = scores(j)
            sd_first = scores(jnp.minimum(j + 1, nb - 1))
            update(j, sd_diag, True)
            lax.fori_loop(j + 1, nb, q_step, sd_first)
            lmb0 = lane_b < HEAD_DIM
            dk_ref[krows, :] = jnp.where(lmb0, dk_scr[0], dk_scr[1]).astype(BF16)
            dv_ref[krows, :] = (dv_scr[0] + dv_scr[1]).astype(BF16)
            sums = [jnp.sum(jnp.where(lane_b == bases[hh], dk_scr[hh], 0.0), axis=1, keepdims=True) for hh in range(2)]
            head0 = F_COL + 2 * pair
            dck_ref[krows, :] += jnp.where(lane_b == head0, -sums[0], jnp.where(lane_b == head0 + 1, -sums[1], 0.0))
            return carry0

        lax.fori_loop(0, nb, kv_step, 0)
        for t0 in range(0, seq, blk):
            dq0, dq1 = dqt_scr[0, :, t0:t0 + blk].T, dqt_scr[1, :, t0:t0 + blk].T
            dq_ref[t0:t0 + blk, :] = (jnp.where(lane_b < HEAD_DIM, dq0, dq1) * ATT_SCALE).astype(BF16)
        dcq_ref[...] = jnp.zeros_like(dcq_ref)
        for hh in range(2):
            dcq_ref[0, 0, hh:hh + 1, :] = jnp.sum(jnp.where(row_t == bases[hh] + AUX_ONE, dqt_scr[hh], 0.0),
                                                   axis=0, keepdims=True)

        @pl.when((pl.program_id(0) == bsz - 1) & (pair == HEAD_PAIRS - 1))
        def _():
            _exchange_wait("scatter", x_refs, xo_refs, *sems)

    col = lambda off: pl.BlockSpec((seq, LANES), lambda b, p: (b, off + p))
    rowvec = pl.BlockSpec((1, 1, SUBLANES, seq), lambda b, p: (b, p, 0, 0))
    out = jax.ShapeDtypeStruct((bsz * seq, D_MODEL), BF16)
    rows_shape = jax.ShapeDtypeStruct((bsz, HEAD_PAIRS, SUBLANES, seq), F32)
    head2 = pltpu.VMEM((2, seq, LANES), BF16)
    hbm = pl.BlockSpec(memory_space=pl.ANY)
    return pl.pallas_call(
        body, name="att_bwd", grid=(bsz, HEAD_PAIRS),
        in_specs=[col(0), col(HEAD_PAIRS), col(2 * HEAD_PAIRS), col(HEAD_PAIRS), col(0), rowvec, col(0)] + [hbm] * nx,
        out_specs=[col(0), col(0), col(0), pl.BlockSpec((seq, LANES), lambda b, p: (b, 0)), rowvec] + [hbm] * nx,
        out_shape=[out, out, out, jax.ShapeDtypeStruct((bsz * seq, LANES), F32), rows_shape]
        + _exchange_shapes("scatter", scatter),
        scratch_shapes=[head2, head2, head2, head2, pltpu.VMEM((2, LANES, seq), BF16),
                        pltpu.VMEM((2, SUBLANES, seq), F32), pltpu.VMEM((2, LANES, seq), F32),
                        pltpu.VMEM((2, blk, LANES), F32), pltpu.VMEM((2, blk, LANES), F32)] + _exchange_scratch(nx),
        compiler_params=_params(("arbitrary", "arbitrary")),
    )(qkv, qkv, qkv, dycat, o, lse, ccol, *scatter)


def _adamw_math(w, g, m, v):
    m = ADAM_B1 * m + (1.0 - ADAM_B1) * g
    v = ADAM_B2 * v + (1.0 - ADAM_B2) * jnp.square(g)
    m_hat = m / ADAM_C1
    v_hat = v / ADAM_C2
    delta = -ADAM_LR * (m_hat / (jnp.sqrt(v_hat) + ADAM_EPS) + ADAM_WD * w)
    return delta, m, v


def _row_tile(rows):
    for tr in (256, 128, 64, 32, 16, 8):
        if rows % tr == 0:
            return tr
    return rows


def _sum_parts(parts, name):
    nparts, rows, cols = parts.shape
    tr = rows if rows % SUBLANES else _row_tile(rows)

    def body(p_ref, o_ref):
        g = p_ref[0].astype(F32)
        for s in range(1, nparts):
            g = g + p_ref[s].astype(F32)
        o_ref[...] = g

    return pl.pallas_call(
        body, name=name, grid=(rows // tr,),
        in_specs=[pl.BlockSpec((nparts, tr, cols), lambda i: (0, i, 0))],
        out_specs=pl.BlockSpec((tr, cols), lambda i: (i, 0)),
        out_shape=jax.ShapeDtypeStruct((rows, cols), F32),
        compiler_params=_params(("parallel",)),
    )(parts)


def _adamw_parts(parts, w, m, v, name):
    nparts, rows, cols = parts.shape
    tr = _row_tile(rows)

    def body(p_ref, w_ref, m_ref, v_ref, g_ref, d_ref, mo_ref, vo_ref):
        g = p_ref[0].astype(F32)
        for s in range(1, nparts):
            g = g + p_ref[s].astype(F32)
        delta, mn, vn = _adamw_math(w_ref[...], g, m_ref[...], v_ref[...])
        g_ref[...] = g
        d_ref[...] = delta
        mo_ref[...] = mn
        vo_ref[...] = vn

    blk = pl.BlockSpec((tr, cols), lambda i: (i, 0))
    shp = jax.ShapeDtypeStruct((rows, cols), F32)
    return pl.pallas_call(
        body, name=name, grid=(rows // tr,),
        in_specs=[pl.BlockSpec((nparts, tr, cols), lambda i: (0, i, 0)), blk, blk, blk],
        out_specs=[blk, blk, blk, blk], out_shape=[shp, shp, shp, shp],
        compiler_params=_params(("parallel",)),
    )(parts, w, m, v)


def _me():
    return lax.axis_index("x"), lax.axis_index("y"), lax.axis_index("c")


def _flip(pos, k):
    x, y, c = pos
    kx, ky, kc = (k >> 2) & 1, (k >> 1) & 1, k & 1
    return (x ^ kx if kx else x, y ^ ky if ky else y, c ^ kc if kc else c)


def _logical(pos):
    return 4 * pos[0] + 2 * pos[1] + pos[2]


def _all_gather_two_level(shards):
    narr = len(shards)

    def body(*refs):
        x_refs, out_refs = refs[:narr], refs[narr:2 * narr]
        send_sems, recv_sems, local_sems = refs[2 * narr:]
        x, y, c = _me()
        me, sibling = (x, y, c), (x, y, 1 - c)
        chips = [(1 - x, y), (x, 1 - y), (1 - x, 1 - y)]

        def copy(a, k, block, to, src=None):
            slot = out_refs[a].at[_logical(block)]
            return pltpu.make_async_remote_copy(
                src_ref=slot if src is None else src, dst_ref=slot,
                send_sem=send_sems.at[a, k], recv_sem=recv_sems.at[a, k], device_id=to, device_id_type=MESH)

        mine = [pltpu.make_async_copy(x_refs[a], out_refs[a].at[_logical(me)], local_sems.at[a]) for a in range(narr)]
        for cp in mine:
            cp.start()
        first = []
        for a in range(narr):
            first.append(copy(a, 0, me, sibling, src=x_refs[a]))
            first += [copy(a, 1 + j, me, (*chip, c), src=x_refs[a]) for j, chip in enumerate(chips)]
        for cp in first:
            cp.start()
        passed = []
        for a in range(narr):
            for j, chip in enumerate(chips):
                copy(a, 1 + j, (*chip, c), me).wait_recv()
                fwd = copy(a, 4 + j, (*chip, c), sibling)
                fwd.start()
                passed.append(fwd)
        for a in range(narr):
            copy(a, 0, sibling, me).wait_recv()
            for j, chip in enumerate(chips):
                copy(a, 4 + j, (*chip, 1 - c), me).wait_recv()
        for cp in first + passed:
            cp.wait_send()
        for cp in mine:
            cp.wait()

    hbm = pl.BlockSpec(memory_space=pl.ANY)
    return pl.pallas_call(
        body, name="all_gather_big",
        out_shape=[jax.ShapeDtypeStruct((N_DEV,) + s.shape, s.dtype) for s in shards],
        in_specs=[hbm] * narr, out_specs=[hbm] * narr,
        scratch_shapes=[pltpu.SemaphoreType.DMA((narr, 7)), pltpu.SemaphoreType.DMA((narr, 7)),
                        pltpu.SemaphoreType.DMA((narr,))],
    )(*shards)


def _exchange_copies(kind, x_refs, out_refs, send_sems, recv_sems, local_sems):
    me = _me()
    mine = _logical(me)
    local, remote = [], []
    for a, (x_ref, out_ref) in enumerate(zip(x_refs, out_refs)):
        own = x_ref if kind == "gather" else x_ref.at[mine]
        local.append(pltpu.make_async_copy(own, out_ref.at[mine], local_sems.at[a]))
        for k in range(1, N_DEV):
            peer = _flip(me, k)
            src = x_ref if kind == "gather" else x_ref.at[_logical(peer)]
            remote.append(pltpu.make_async_remote_copy(
                src_ref=src, dst_ref=out_ref.at[mine], send_sem=send_sems.at[a, k - 1], recv_sem=recv_sems.at[a, k - 1],
                device_id=peer, device_id_type=MESH))
    return local, remote


def _exchange_start(kind, x_refs, out_refs, *sems):
    if not x_refs:
        return
    local, remote = _exchange_copies(kind, x_refs, out_refs, *sems)
    for cp in local + remote:
        cp.start()


def _exchange_wait(kind, x_refs, out_refs, *sems):
    if not x_refs:
        return
    local, remote = _exchange_copies(kind, x_refs, out_refs, *sems)
    for cp in remote:
        cp.wait_recv()
    for cp in remote:
        cp.wait_send()
    for cp in local:
        cp.wait()


def _exchange_shapes(kind, arrays):
    return [jax.ShapeDtypeStruct(((N_DEV,) if kind == "gather" else ()) + a.shape, a.dtype) for a in arrays]


def _exchange_scratch(narrays):
    if not narrays:
        return []
    return [pltpu.SemaphoreType.DMA((narrays, N_DEV - 1)), pltpu.SemaphoreType.DMA((narrays, N_DEV - 1)),
            pltpu.SemaphoreType.DMA((narrays,))]


def _exchange_rows(x_ref, buf_ref, send_sems, recv_sems):
    me = _me()
    buf_ref[_logical(me)] = x_ref[...]
    copies = []
    for k in range(1, N_DEV):
        peer = _flip(me, k)
        copies.append(pltpu.make_async_remote_copy(
            src_ref=x_ref, dst_ref=buf_ref.at[_logical(me)],
            send_sem=send_sems.at[k - 1], recv_sem=recv_sems.at[k - 1], device_id=peer, device_id_type=MESH))
    for cp in copies:
        cp.start()
    for cp in copies:
        cp.wait_recv()
    for cp in copies:
        cp.wait_send()


def _sum_slots(buf_ref):
    total = buf_ref[0]
    for s in range(1, N_DEV):
        total = total + buf_ref[s]
    return total


def _small_gather(x):
    def body(x_ref, o_ref, buf_ref, send_sems, recv_sems):
        _exchange_rows(x_ref, buf_ref, send_sems, recv_sems)
        o_ref[...] = _sum_slots(buf_ref)

    return pl.pallas_call(
        body, name="small_gather", out_shape=jax.ShapeDtypeStruct(x.shape, F32),
        in_specs=[pl.BlockSpec(memory_space=pltpu.VMEM)], out_specs=pl.BlockSpec(memory_space=pltpu.VMEM),
        scratch_shapes=[pltpu.VMEM((N_DEV,) + x.shape, F32), pltpu.SemaphoreType.DMA((7,)), pltpu.SemaphoreType.DMA((7,))],
    )(x)


def _small_reduce_adamw(g, w, m, v):
    def body(g_ref, w_ref, m_ref, v_ref, go_ref, d_ref, mo_ref, vo_ref, buf_ref, send_sems, recv_sems):
        _exchange_rows(g_ref, buf_ref, send_sems, recv_sems)
        gs = _sum_slots(buf_ref)
        delta, mn, vn = _adamw_math(w_ref[...], gs, m_ref[...], v_ref[...])
        go_ref[...] = gs
        d_ref[...] = delta
        mo_ref[...] = mn
        vo_ref[...] = vn

    vm = pl.BlockSpec(memory_space=pltpu.VMEM)
    shp = jax.ShapeDtypeStruct(g.shape, F32)
    return pl.pallas_call(
        body, name="small_reduce_adamw", out_shape=[shp, shp, shp, shp],
        in_specs=[vm, vm, vm, vm], out_specs=[vm, vm, vm, vm],
        scratch_shapes=[pltpu.VMEM((N_DEV,) + g.shape, F32), pltpu.SemaphoreType.DMA((7,)), pltpu.SemaphoreType.DMA((7,))],
    )(g, w, m, v)


def _pad_cols(a, width):
    return jnp.pad(a, ((0, 0), (0, width - a.shape[1])))


def _local_step(x, target, norm_mix_w, w_in_t, conv_w, conv_b, dt_bias, a_log, d_skip, ssd_norm_w, f_bias,
                late_shards, norm_mlp_w, norm_final_w):
    bsz, seq, _ = x.shape
    n = bsz * seq
    x2 = x.reshape(n, D_MODEL)
    t2 = target.reshape(n, D_MODEL)
    nfw = norm_final_w.reshape(1, D_MODEL)

    bias = _pad_cols(dt_bias, LANES)
    arow = _pad_cols(-jnp.exp(a_log), LANES)
    biast, acol = bias.reshape(LANES, 1), arow.reshape(LANES, 1)
    dfull = jnp.repeat(d_skip, HEAD_DIM, axis=1)
    fb = jnp.pad(f_bias, ((0, 0), (F_COL, LANES - 2 * F_COL)))

    wt_z, wt_xbc, wt_qkv = w_in_t[:ROW_XBC], w_in_t[ROW_XBC:ROW_DT], w_in_t[ROW_Q:ROW_F]
    wt_dtf = jnp.concatenate([w_in_t[ROW_DT:ROW_Q], w_in_t[ROW_F:], jnp.zeros((LANES - 2 * F_COL, D_MODEL), BF16)], axis=0)
    wt_q, wt_k, wt_v = wt_qkv[:D_MODEL], wt_qkv[D_MODEL:2 * D_MODEL], wt_qkv[2 * D_MODEL:]

    h1 = _rms_fwd(x2, norm_mix_w, "rms_fwd_mix")
    z = _mm([(h1, wt_z)], "inproj_z", F32, 1024, 1024, nt=True)
    xbc_raw = _mm([(h1, wt_xbc)], "inproj_xbc", F32, 512, 1536, nt=True)
    qkv = _mm([(h1, wt_qkv)], "inproj_qkv", BF16, 512, 3072, nt=True)
    dtf = _mm([(h1, wt_dtf)], "inproj_dtf", F32, 2048, LANES, nt=True)
    dtft = dtf.reshape(bsz, seq, LANES).transpose(0, 2, 1)

    xbc = _conv_fwd(xbc_raw, conv_w, conv_b, bsz, seq)
    y_pre, hall = _ssd_fwd(xbc, dtf, dtft, bias, biast, arow, acol, dfull, bsz, seq)

    ccol = _fox_prep(dtf, fb, bsz, seq)
    o_att, lse, w_out, w_up_t, w_down = _att_fwd(qkv, ccol, bsz, seq, late_shards)
    w_out, w_up_t, w_down = (w.reshape(-1, D_MODEL) for w in (w_out, w_up_t, w_down))

    ycat = _gnorm_fwd(y_pre, z, o_att, ssd_norm_w)
    h2, h2n = _mm([(ycat, w_out)], "outproj", F32, 512, 1024, res=x2, rms_fwd=norm_mlp_w)
    pre = _mm([(h2n, w_up_t)], "mlp_up", BF16, 512, 4096, nt=True)

    dh3, loss_row, g_nfw = _mlp_down_loss(pre, w_down, h2, t2, nfw)
    dpre, u = _mlp_down_bwd(dh3, w_down, pre)
    g_w_down = _mm_tn(u, dh3, "grad_w_down", 1024, 1024, 2048)
    g_w_up_t = _mm_tn(dpre, h2n, "grad_w_up", 1024, 1024, 2048)
    by_shard = lambda g: g.reshape(N_DEV, g.shape[0] // N_DEV, D_MODEL)
    dh2, g_nmlp = _mm([(dpre, w_up_t)], "mlp_up_bwd", F32, 512, 1024, res=dh3, rms_bwd=(h2, norm_mlp_w))

    g_w_out = _mm_tn(ycat, dh2, "grad_w_out", 1024, 1024, 2048)
    dycat = _mm([(dh2, w_out)], "outproj_bwd", F32, 512, 2048, nt=True)
    dy_pre, dz, g_ssdn = _gnorm_bwd(dycat, y_pre, z, ssd_norm_w)
    dq, dk, dv, dck, dcq, recv_down, recv_up_t, recv_out = _att_bwd(
        qkv, dycat, o_att, lse, ccol, bsz, seq, [by_shard(g_w_down), by_shard(g_w_up_t), by_shard(g_w_out)])

    dxbc, ddt, ssd_acc = _ssd_bwd(dy_pre, xbc, dtf, dtft, bias, biast, arow, acol, dfull, hall, bsz, seq)
    dxbc_raw, g_cw, g_cb = _conv_bwd(dxbc, xbc_raw, conv_w, conv_b, bsz, seq)

    def head_cols(rows):
        cols = rows[:, :, :2, :].reshape(bsz, SSD_HEADS, seq).transpose(0, 2, 1).reshape(n, SSD_HEADS)
        return jnp.pad(cols, ((0, 0), (F_COL, LANES - 2 * F_COL)))

    ddtf, g_fb = _fox_prep_bwd(dck, head_cols(dcq), ddt, dtf, fb, bsz, seq)

    g_dtf = _mm_tn(ddtf, h1, "grad_w_in_dtf", LANES, 1024, 2048)
    g_w_in_t = jnp.concatenate([
        _mm_tn(dz, h1, "grad_w_in_z", 1024, 1024, 2048), _mm_tn(dxbc_raw, h1, "grad_w_in_xbc", 768, 1024, 2048),
        g_dtf[:F_COL], _mm_tn(dq, h1, "grad_w_in_q", 1024, 1024, 2048), _mm_tn(dk, h1, "grad_w_in_k", 1024, 1024, 2048),
        _mm_tn(dv, h1, "grad_w_in_v", 1024, 1024, 2048), g_dtf[F_COL:2 * F_COL]], axis=0)
    pieces = [(dz, wt_z), (dxbc_raw, wt_xbc), (dq, wt_q), (dk, wt_k), (dv, wt_v), (ddtf, wt_dtf)]
    dx, g_nmix, recv_in_t = _mm(pieces, "inproj_bwd", F32, 256, 1024, res=dh2, rms_bwd=(x2, norm_mix_w),
                                exchange=("scatter", [by_shard(g_w_in_t)]))

    small = dict(
        norm_mix_w=g_nmix, norm_mlp_w=g_nmlp, norm_final_w=g_nfw, ssd_norm_w=g_ssdn, conv_b=g_cb, conv_w=g_cw,
        dt_bias=ssd_acc[16:17, :SSD_HEADS], a_log=ssd_acc[0:1, :SSD_HEADS], d_skip=ssd_acc[8:9, :SSD_HEADS],
        f_bias=g_fb[0:1, F_COL:2 * F_COL])
    recv = dict(w_in_t=recv_in_t, w_out=recv_out, w_up_t=recv_up_t, w_down=recv_down)
    return loss_row[0, 0], dx.reshape(bsz, seq, D_MODEL), small, recv


SMALL_LAYOUT = (("norm_mix_w", 0, 1, 0, D_MODEL), ("norm_mlp_w", 1, 1, 0, D_MODEL), ("norm_final_w", 2, 1, 0, D_MODEL),
                ("ssd_norm_w", 3, 1, 0, D_MODEL), ("conv_b", 4, 1, 0, CONV_CH), ("conv_w", 5, CONV_K, 0, CONV_CH),
                ("dt_bias", 9, 1, 0, SSD_HEADS), ("a_log", 9, 1, LANES, SSD_HEADS), ("d_skip", 9, 1, 2 * LANES, SSD_HEADS),
                ("f_bias", 9, 1, 3 * LANES, SSD_HEADS))
SMALL_ROWS = 2 * SUBLANES


def _pack_small(vals):
    packed = jnp.zeros((SMALL_ROWS, SMALL_COLS), F32)
    for name, r0, nrows, c0, width in SMALL_LAYOUT:
        packed = packed.at[r0:r0 + nrows, c0:c0 + width].set(vals[name].reshape(nrows, width))
    return packed


def _unpack_small(packed, like):
    out = {}
    for name, r0, nrows, c0, width in SMALL_LAYOUT:
        out[name] = packed[r0:r0 + nrows, c0:c0 + width].reshape(like[name].shape)
    return out


def kernel(x, norm_mix_w, w_in, conv_w, conv_b, dt_bias, a_log, d_skip, ssd_norm_w, f_bias, w_out, norm_mlp_w, w_up, w_down, norm_final_w, loss_target, m_norm_mix_w, m_w_in, m_conv_w, m_conv_b, m_dt_bias, m_a_log, m_d_skip, m_ssd_norm_w, m_f_bias, m_w_out, m_norm_mlp_w, m_w_up, m_w_down, m_norm_final_w, v_norm_mix_w, v_w_in, v_conv_w, v_conv_b, v_dt_bias, v_a_log, v_d_skip, v_ssd_norm_w, v_f_bias, v_w_out, v_norm_mlp_w, v_w_up, v_w_down, v_norm_final_w):
    me = 4 * lax.axis_index("x") + 2 * lax.axis_index("y") + lax.axis_index("c")
    conv_shard_w = CONV_CH // N_DEV
    zero = jnp.zeros((), jnp.int32)

    def place_conv(shard):
        return lax.dynamic_update_slice(jnp.zeros((CONV_K, CONV_CH), F32), shard[0], (zero, me * conv_shard_w))

    (g_in_t,) = _all_gather_two_level([w_in[0].T.astype(BF16)])
    late_shards = [w_out[0].astype(BF16), w_up[0].T.astype(BF16), w_down[0].astype(BF16)]
    conv_full = _small_gather(jnp.pad(place_conv(conv_w), ((0, SUBLANES - CONV_K), (0, 0))))[:CONV_K]

    loss_local, grad_x, g_small, recv = _local_step(
        x, loss_target, norm_mix_w, g_in_t.reshape(IN_WIDTH, D_MODEL), conv_full, conv_b, dt_bias, a_log, d_skip,
        ssd_norm_w, f_bias, late_shards, norm_mlp_w, norm_final_w)
    loss = lax.psum(loss_local, MESH_AXES)

    grad_in = _sum_parts(recv["w_in_t"], "sum_w_in").T[None]
    grad_up = _sum_parts(recv["w_up_t"], "sum_w_up").T[None]
    big = {
        "w_in": _adamw_parts(grad_in, w_in[0], m_w_in[0], v_w_in[0], "adamw_w_in"),
        "w_out": _adamw_parts(recv["w_out"], w_out[0], m_w_out[0], v_w_out[0], "adamw_w_out"),
        "w_up": _adamw_parts(grad_up, w_up[0], m_w_up[0], v_w_up[0], "adamw_w_up"),
        "w_down": _adamw_parts(recv["w_down"], w_down[0], m_w_down[0], v_w_down[0], "adamw_w_down"),
    }

    like = dict(norm_mix_w=norm_mix_w, norm_mlp_w=norm_mlp_w, norm_final_w=norm_final_w, ssd_norm_w=ssd_norm_w,
                conv_b=conv_b, conv_w=jnp.zeros((1, CONV_K, CONV_CH), F32), dt_bias=dt_bias, a_log=a_log,
                d_skip=d_skip, f_bias=f_bias)
    w_small = dict(like, conv_w=place_conv(conv_w))
    m_small = dict(norm_mix_w=m_norm_mix_w, norm_mlp_w=m_norm_mlp_w, norm_final_w=m_norm_final_w,
                   ssd_norm_w=m_ssd_norm_w, conv_b=m_conv_b, conv_w=place_conv(m_conv_w), dt_bias=m_dt_bias,
                   a_log=m_a_log, d_skip=m_d_skip, f_bias=m_f_bias)
    v_small = dict(norm_mix_w=v_norm_mix_w, norm_mlp_w=v_norm_mlp_w, norm_final_w=v_norm_final_w,
                   ssd_norm_w=v_ssd_norm_w, conv_b=v_conv_b, conv_w=place_conv(v_conv_w), dt_bias=v_dt_bias,
                   a_log=v_a_log, d_skip=v_d_skip, f_bias=v_f_bias)
    small = _small_reduce_adamw(_pack_small(g_small), _pack_small(w_small), _pack_small(m_small), _pack_small(v_small))
    small = [_unpack_small(s, like) for s in small]
    for s in small:
        s["conv_w"] = lax.dynamic_slice(s["conv_w"], (zero, zero, me * conv_shard_w), (1, CONV_K, conv_shard_w))

    order = ["norm_mix_w", "w_in", "conv_w", "conv_b", "dt_bias", "a_log", "d_skip", "ssd_norm_w", "f_bias", "w_out",
             "norm_mlp_w", "w_up", "w_down", "norm_final_w"]
    outs = [loss, grad_x]
    for kind in range(4):
        for name in order:
            outs.append(big[name][kind][None] if name in big else small[kind][name])
    return tuple(outs)
```

```python
import jax
import jax.numpy as jnp
from jax import lax
from jax.experimental import pallas as pl
from jax.experimental.pallas import tpu as pltpu

F32, BF16 = jnp.float32, jnp.bfloat16
HI = lax.Precision.HIGHEST

D_MODEL = 1024
SSD_HEADS = 16
HEAD_DIM = 64
SSD_STATE = 128
CHUNK = 128
CONV_CH = 1536
CONV_K = 4
D_FF = 4096
MIX_WIDTH = 2048
IN_WIDTH = 5664
NORM_EPS = 1e-5
ATT_SCALE = 0.125

LANES = 128
SUBLANES = 8
HEAD_PAIRS = SSD_HEADS // 2
NEG = -1e30
VMEM_LIMIT = 52 * 1024 * 1024

ROW_XBC, ROW_DT, ROW_Q, ROW_F = 1024, 2560, 2576, 5648
F_COL = SSD_HEADS

N_DEV = 8
MESH_AXES = ("x", "y", "c")
MESH = pl.DeviceIdType.MESH

ADAM_LR, ADAM_B1, ADAM_B2, ADAM_EPS, ADAM_WD, ADAM_STEP = 0.001, 0.9, 0.999, 1e-08, 0.01, 10
ADAM_C1 = 1.0 - ADAM_B1 ** ADAM_STEP
ADAM_C2 = 1.0 - ADAM_B2 ** ADAM_STEP

SMALL_COLS = CONV_CH


def _params(sem=None):
    return pltpu.CompilerParams(dimension_semantics=sem, vmem_limit_bytes=VMEM_LIMIT)


def _sigmoid(u):
    return 1.0 / (1.0 + jnp.exp(-u))


def _softplus(u):
    return jnp.maximum(u, 0.0) + jnp.log(1.0 + jnp.exp(-jnp.abs(u)))


def _log_sigmoid(u):
    return jnp.minimum(u, 0.0) - jnp.log(1.0 + jnp.exp(-jnp.abs(u)))


def _bdot(a, b):
    return jnp.dot(a.astype(BF16), b.astype(BF16), preferred_element_type=F32)


def _bdot_nt(a, b):
    return lax.dot_general(a.astype(BF16), b.astype(BF16), (((1,), (1,)), ((), ())), preferred_element_type=F32)


def _bdot_tn(a, b):
    return lax.dot_general(a.astype(BF16), b.astype(BF16), (((0,), (0,)), ((), ())), preferred_element_type=F32)


def _split3(x):
    x1 = x.astype(BF16)
    r1 = x - x1.astype(F32)
    x2 = r1.astype(BF16)
    return x1, x2, (r1 - x2.astype(F32)).astype(BF16)


def _dot_sel(x, sel):
    s = sel.astype(BF16)
    p1, p2, p3 = (jnp.dot(p, s, preferred_element_type=F32) for p in _split3(x))
    return p1 + p2 + p3


def _sel_dot(sel, x):
    s = sel.astype(BF16)
    p1, p2, p3 = (jnp.dot(s, p, preferred_element_type=F32) for p in _split3(x))
    return p1 + p2 + p3


def _iota(shape, dim):
    return lax.broadcasted_iota(jnp.int32, shape, dim)


def _head_expand():
    return (jnp.right_shift(_iota((LANES, D_MODEL), 1), 6) == _iota((LANES, D_MODEL), 0)).astype(F32)


def _head_reduce():
    return (jnp.right_shift(_iota((D_MODEL, LANES), 0), 6) == _iota((D_MODEL, LANES), 1)).astype(F32)


def _rms_fwd(x, w, name):
    n, d = x.shape
    tm = min(1024, n)

    def body(x_ref, w_ref, o_ref):
        xv = x_ref[...]
        r = lax.rsqrt(jnp.mean(xv * xv, axis=-1, keepdims=True) + NORM_EPS)
        o_ref[...] = (xv * r * w_ref[...]).astype(BF16)

    return pl.pallas_call(
        body, name=name, grid=(n // tm,),
        in_specs=[pl.BlockSpec((tm, d), lambda i: (i, 0)), pl.BlockSpec((1, d), lambda i: (0, 0))],
        out_specs=pl.BlockSpec((tm, d), lambda i: (i, 0)),
        out_shape=jax.ShapeDtypeStruct((n, d), BF16),
        compiler_params=_params(("parallel",)),
    )(x, w)


def _mm(pairs, name, out_dtype, tm, tn, nt=False, res=None, exchange=None, rms_fwd=None, rms_bwd=None):
    m = pairs[0][0].shape[0]
    n = pairs[0][1].shape[0 if nt else 1]
    tm, tn = min(tm, m), min(tn, n)
    gm, gn = m // tm, n // tn
    npairs = len(pairs)
    kind, xs = (None, []) if exchange is None else exchange
    nx = len(xs)
    extra_in = [] if res is None else [res]
    if rms_bwd is not None:
        extra_in += list(rms_bwd)
    elif rms_fwd is not None:
        extra_in.append(rms_fwd)
    n_in = 2 * npairs + len(extra_in)
    n_out = 1 + (rms_fwd is not None or rms_bwd is not None)
    assert (rms_fwd is None and rms_bwd is None) or tn == n

    def body(*refs):
        x_refs, o_refs = refs[n_in:n_in + nx], refs[n_in + nx:n_in + nx + n_out]
        xo_refs, sems = refs[n_in + nx + n_out:n_in + 2 * nx + n_out], refs[n_in + 2 * nx + n_out:]
        extra = refs[2 * npairs:n_in]
        i, j = pl.program_id(0), pl.program_id(1)
        if nx:
            @pl.when((i == 0) & (j == 0))
            def _():
                _exchange_start(kind, x_refs, xo_refs, *sems)

        acc = None
        for p in range(npairs):
            a_v, b_v = refs[2 * p][...], refs[2 * p + 1][...]
            d = _bdot_nt(a_v, b_v) if nt else _bdot(a_v, b_v)
            acc = d if acc is None else acc + d
        if rms_bwd is not None:
            xv = extra[1][...]
            r = lax.rsqrt(jnp.mean(xv * xv, axis=-1, keepdims=True) + NORM_EPS)
            nrm = xv * r
            g = acc * extra[2][...]
            o_refs[0][...] = extra[0][...] + r * (g - nrm * jnp.mean(g * nrm, axis=-1, keepdims=True))

            @pl.when(i == 0)
            def _():
                o_refs[1][...] = jnp.zeros_like(o_refs[1])

            o_refs[1][...] += jnp.sum(acc * nrm, axis=0, keepdims=True)
        else:
            if res is not None:
                acc = extra[0][...] + acc
            o_refs[0][...] = acc.astype(o_refs[0].dtype)
            if rms_fwd is not None:
                r = lax.rsqrt(jnp.mean(acc * acc, axis=-1, keepdims=True) + NORM_EPS)
                o_refs[1][...] = (acc * r * extra[-1][...]).astype(BF16)
        if nx:
            @pl.when((i == gm - 1) & (j == gn - 1))
            def _():
                _exchange_wait(kind, x_refs, xo_refs, *sems)

    tile = pl.BlockSpec((tm, tn), lambda i, j: (i, j))
    row = pl.BlockSpec((1, tn), lambda i, j: (0, j))
    in_specs, args = [], []
    for a, b in pairs:
        k = a.shape[1]
        in_specs.append(pl.BlockSpec((tm, k), lambda i, j: (i, 0)))
        in_specs.append(pl.BlockSpec((tn, k), lambda i, j: (j, 0)) if nt else pl.BlockSpec((k, tn), lambda i, j: (0, j)))
        args += [a, b]
    in_specs += [row if e.shape[0] == 1 else tile for e in extra_in]
    out_specs, out_shape = [tile], [jax.ShapeDtypeStruct((m, n), out_dtype)]
    if rms_bwd is not None:
        out_specs.append(row)
        out_shape.append(jax.ShapeDtypeStruct((1, n), F32))
    elif rms_fwd is not None:
        out_specs.append(tile)
        out_shape.append(jax.ShapeDtypeStruct((m, n), BF16))
    hbm = pl.BlockSpec(memory_space=pl.ANY)
    sequential = nx or rms_bwd is not None
    outs = pl.pallas_call(
        body, name=name, grid=(gm, gn), in_specs=in_specs + [hbm] * nx,
        out_specs=out_specs + [hbm] * nx,
        out_shape=out_shape + _exchange_shapes(kind, xs),
        scratch_shapes=_exchange_scratch(nx),
        compiler_params=_params(("arbitrary", "arbitrary") if sequential else ("parallel", "parallel")),
    )(*args, *extra_in, *xs)
    return outs if len(outs) > 1 else outs[0]


def _mm_tn(a, b, name, tm, tn, tk):
    t, m = a.shape
    n = b.shape[1]
    tm, tn, tk = min(tm, m), min(tn, n), min(tk, t)
    nk = t // tk

    def body(a_ref, b_ref, o_ref, acc_ref):
        kk = pl.program_id(2)

        @pl.when(kk == 0)
        def _():
            acc_ref[...] = jnp.zeros_like(acc_ref)

        acc_ref[...] += _bdot_tn(a_ref[...], b_ref[...])

        @pl.when(kk == nk - 1)
        def _():
            o_ref[...] = acc_ref[...].astype(o_ref.dtype)

    return pl.pallas_call(
        body, name=name, grid=(m // tm, n // tn, nk),
        in_specs=[pl.BlockSpec((tk, tm), lambda i, j, kk: (kk, i)), pl.BlockSpec((tk, tn), lambda i, j, kk: (kk, j))],
        out_specs=pl.BlockSpec((tm, tn), lambda i, j, kk: (i, j)),
        out_shape=jax.ShapeDtypeStruct((m, n), BF16),
        scratch_shapes=[pltpu.VMEM((tm, tn), F32)],
        compiler_params=_params(("parallel", "parallel", "arbitrary")),
    )(a, b)


def _mlp_down_loss(pre, w_down, h2, target, w):
    m, k = pre.shape
    d = w_down.shape[1]
    tm = min(512, m)

    def body(p_ref, b_ref, r_ref, t_ref, w_ref, dh_ref, loss_ref, gw_ref):
        u = jnp.square(jnp.maximum(p_ref[...].astype(F32), 0.0))
        xv = r_ref[...] + _bdot(u, b_ref[...])
        r = lax.rsqrt(jnp.mean(xv * xv, axis=-1, keepdims=True) + NORM_EPS)
        nrm = xv * r
        wv = w_ref[...]
        err = nrm * wv - t_ref[...]
        part = 0.5 * jnp.sum(jnp.mean(err * err, axis=-1, keepdims=True), axis=0, keepdims=True)
        dy = err * (1.0 / d)
        g = dy * wv
        dh_ref[...] = r * (g - nrm * jnp.mean(g * nrm, axis=-1, keepdims=True))

        @pl.when(pl.program_id(0) == 0)
        def _():
            loss_ref[...] = jnp.zeros_like(loss_ref)
            gw_ref[...] = jnp.zeros_like(gw_ref)

        loss_ref[...] += jnp.broadcast_to(part, loss_ref.shape)
        gw_ref[...] += jnp.sum(dy * nrm, axis=0, keepdims=True)

    tok = pl.BlockSpec((tm, d), lambda i: (i, 0))
    row = pl.BlockSpec((1, d), lambda i: (0, 0))
    return pl.pallas_call(
        body, name="mlp_down_loss", grid=(m // tm,),
        in_specs=[pl.BlockSpec((tm, k), lambda i: (i, 0)), pl.BlockSpec((k, d), lambda i: (0, 0)), tok, tok, row],
        out_specs=[tok, pl.BlockSpec((1, LANES), lambda i: (0, 0)), row],
        out_shape=[jax.ShapeDtypeStruct((m, d), F32), jax.ShapeDtypeStruct((1, LANES), F32),
                   jax.ShapeDtypeStruct((1, d), F32)],
        compiler_params=_params(("arbitrary",)),
    )(pre, w_down, h2, target, w)


def _mlp_down_bwd(dh3, w_down, pre):
    m, k = dh3.shape
    n = w_down.shape[0]
    tm, tn = min(256, m), n

    def body(a_ref, b_ref, p_ref, dpre_ref, u_ref):
        r = jnp.maximum(p_ref[...].astype(F32), 0.0)
        du = _bdot_nt(a_ref[...], b_ref[...])
        dpre_ref[...] = (du * (2.0 * r)).astype(BF16)
        u_ref[...] = (r * r).astype(BF16)

    blk = pl.BlockSpec((tm, tn), lambda i, j: (i, j))
    return pl.pallas_call(
        body, name="mlp_down_bwd", grid=(m // tm, n // tn),
        in_specs=[pl.BlockSpec((tm, k), lambda i, j: (i, 0)), pl.BlockSpec((tn, k), lambda i, j: (j, 0)), blk],
        out_specs=[blk, blk],
        out_shape=[jax.ShapeDtypeStruct((m, n), BF16), jax.ShapeDtypeStruct((m, n), BF16)],
        compiler_params=_params(("parallel", "parallel")),
    )(dh3, w_down, pre)


CONV_TC = 512


def _conv_fwd(xbc, cw, cb, bsz, seq):
    tt = min(512, seq)
    nt, hb = seq // tt, tt // SUBLANES
    x3 = xbc.reshape(bsz, seq, CONV_CH)

    def body(xm_ref, xh_ref, w_ref, b_ref, o_ref):
        i = pl.program_id(2)
        x = xm_ref[0]
        halo = jnp.where(i > 0, xh_ref[0], 0.0)
        xx = jnp.concatenate([halo, x], axis=0)
        acc = x * w_ref[3:4, :] + b_ref[...]
        for s in range(1, CONV_K):
            acc = acc + pltpu.roll(xx, s, 0)[SUBLANES:, :] * w_ref[3 - s:4 - s, :]
        o_ref[0] = acc * _sigmoid(acc)

    out = pl.pallas_call(
        body, name="conv_fwd", grid=(CONV_CH // CONV_TC, bsz, nt),
        in_specs=[pl.BlockSpec((1, tt, CONV_TC), lambda c, b, i: (b, i, c)),
                  pl.BlockSpec((1, SUBLANES, CONV_TC), lambda c, b, i: (b, jnp.maximum(i * hb - 1, 0), c)),
                  pl.BlockSpec((CONV_K, CONV_TC), lambda c, b, i: (0, c)),
                  pl.BlockSpec((1, CONV_TC), lambda c, b, i: (0, c))],
        out_specs=pl.BlockSpec((1, tt, CONV_TC), lambda c, b, i: (b, i, c)),
        out_shape=jax.ShapeDtypeStruct((bsz, seq, CONV_CH), F32),
        compiler_params=_params(("parallel", "parallel", "parallel")),
    )(x3, x3, cw, cb)
    return out.reshape(bsz * seq, CONV_CH)


def _conv_bwd(da, xbc, cw, cb, bsz, seq):
    tt = min(512, seq)
    nt, hb = seq // tt, tt // SUBLANES
    x3 = xbc.reshape(bsz, seq, CONV_CH)
    da3 = da.reshape(bsz, seq, CONV_CH)
    n2 = tt + SUBLANES

    def body(dam_ref, daa_ref, xm_ref, xb_ref, xa_ref, w_ref, b_ref, du_ref, gw_ref, gb_ref):
        bi, i = pl.program_id(1), pl.program_id(2)
        before = jnp.where(i > 0, xb_ref[0], 0.0)
        after = jnp.where(i < nt - 1, xa_ref[0], 0.0)
        xx = jnp.concatenate([before, xm_ref[0], after], axis=0)
        rolled = [xx] + [pltpu.roll(xx, s, 0) for s in range(1, CONV_K)]
        pre = b_ref[...]
        for s in range(CONV_K):
            pre = pre + rolled[s][SUBLANES:, :] * w_ref[3 - s:4 - s, :]
        da_ext = jnp.concatenate([dam_ref[0], jnp.where(i < nt - 1, daa_ref[0], 0.0)], axis=0)
        sg = _sigmoid(pre)
        dpre = da_ext * sg * (1.0 + pre * (1.0 - sg))
        du = dpre[:tt, :] * w_ref[3:4, :]
        for s in range(1, CONV_K):
            du = du + pltpu.roll(dpre, n2 - s, 0)[:tt, :] * w_ref[3 - s:4 - s, :]
        du_ref[0] = du.astype(BF16)
        dpm = dpre[:tt, :]
        gws = [jnp.sum(dpm * rolled[3 - kk][SUBLANES:SUBLANES + tt, :], axis=0, keepdims=True) for kk in range(CONV_K)]

        @pl.when((bi == 0) & (i == 0))
        def _():
            gw_ref[...] = jnp.zeros_like(gw_ref)
            gb_ref[...] = jnp.zeros_like(gb_ref)

        gw_ref[...] += jnp.concatenate(gws, axis=0)
        gb_ref[...] += jnp.sum(dpm, axis=0, keepdims=True)

    main = pl.BlockSpec((1, tt, CONV_TC), lambda c, b, i: (b, i, c))
    prev = pl.BlockSpec((1, SUBLANES, CONV_TC), lambda c, b, i: (b, jnp.maximum(i * hb - 1, 0), c))
    nxt = pl.BlockSpec((1, SUBLANES, CONV_TC), lambda c, b, i: (b, jnp.minimum((i + 1) * hb, seq // SUBLANES - 1), c))
    du, gw, gb = pl.pallas_call(
        body, name="conv_bwd", grid=(CONV_CH // CONV_TC, bsz, nt),
        in_specs=[main, nxt, main, prev, nxt,
                  pl.BlockSpec((CONV_K, CONV_TC), lambda c, b, i: (0, c)),
                  pl.BlockSpec((1, CONV_TC), lambda c, b, i: (0, c))],
        out_specs=[main, pl.BlockSpec((CONV_K, CONV_TC), lambda c, b, i: (0, c)),
                   pl.BlockSpec((1, CONV_TC), lambda c, b, i: (0, c))],
        out_shape=[jax.ShapeDtypeStruct((bsz, seq, CONV_CH), BF16), jax.ShapeDtypeStruct((CONV_K, CONV_CH), F32),
                   jax.ShapeDtypeStruct((1, CONV_CH), F32)],
        compiler_params=_params(("parallel", "arbitrary", "arbitrary")),
    )(da3, da3, x3, x3, x3, cw, cb)
    return du.reshape(bsz * seq, CONV_CH), gw, gb


def _ssd_prologue(dtf_ref, dtft_ref, bias_ref, biast_ref, arow_ref, acol_ref, dtfull_scr, acfull_scr, acr_scr):
    tri = (_iota((CHUNK, CHUNK), 1) <= _iota((CHUNK, CHUNK), 0)).astype(F32)
    triu = (_iota((CHUNK, CHUNK), 0) <= _iota((CHUNK, CHUNK), 1)).astype(F32)
    dtc = _softplus(dtf_ref[0] + bias_ref[...])
    a = dtc * arow_ref[...]
    acum = _sel_dot(tri, a)
    expand = _head_expand()
    dtfull_scr[...] = _dot_sel(dtc, expand)
    acfull_scr[...] = _dot_sel(acum, expand)
    dtr = _softplus(dtft_ref[0] + biast_ref[...])
    acr_scr[...] = _dot_sel(dtr * acol_ref[...], triu)
    return dtc, acum


def _decay_matrix(acum, acr_scr, h):
    lane = _iota((CHUNK, LANES), 1)
    ac_col = jnp.sum(jnp.where(lane == h, acum, 0.0), axis=1, keepdims=True)
    ac_row = acr_scr[h:h + 1, :]
    causal = _iota((CHUNK, CHUNK), 1) <= _iota((CHUNK, CHUNK), 0)
    return jnp.exp(jnp.where(causal, ac_col - ac_row, NEG))


def _ssd_fwd(xbc, dtf, dtft, bias, biast, arow, acol, dfull, bsz, seq):
    nc = seq // CHUNK
    x3 = xbc.reshape(bsz, seq, CONV_CH)

    def body(xbc_ref, dtf_ref, dtft_ref, bias_ref, biast_ref, arow_ref, acol_ref, dfull_ref,
             y_ref, hall_ref, h_scr, dtfull_scr, acfull_scr, acr_scr):
        @pl.when(pl.program_id(1) == 0)
        def _():
            h_scr[...] = jnp.zeros_like(h_scr)

        _, acum = _ssd_prologue(dtf_ref, dtft_ref, bias_ref, biast_ref, arow_ref, acol_ref,
                                dtfull_scr, acfull_scr, acr_scr)
        lane = _iota((CHUNK, LANES), 1)
        for g in range(2):
            bg = xbc_ref[0, :, D_MODEL + LANES * g:D_MODEL + LANES * (g + 1)]
            cg = xbc_ref[0, :, D_MODEL + 2 * LANES + LANES * g:D_MODEL + 2 * LANES + LANES * (g + 1)]
            cg_bf = cg.astype(BF16)
            gmat = _bdot_nt(cg_bf, bg)
            bgt_bf = bg.T.astype(BF16)
            for j in range(4):
                p = 4 * g + j
                sl = slice(LANES * p, LANES * (p + 1))
                xs = xbc_ref[0, :, sl]
                ac = acfull_scr[:, sl]
                acl = acfull_scr[CHUNK - 1:CHUNK, sl]
                xdt = xs * dtfull_scr[:, sl]
                xdt_bf = xdt.astype(BF16)
                hp = h_scr[p]
                hall_ref[0, 0, p] = hp
                yo = _bdot(cg_bf, hp) * jnp.exp(ac)
                yd = []
                for hh in range(2):
                    mmat = gmat * _decay_matrix(acum, acr_scr, 2 * p + hh)
                    yd.append(_bdot(mmat, xdt_bf))
                y_ref[0, :, sl] = jnp.where(lane < HEAD_DIM, yd[0], yd[1]) + yo + dfull_ref[:, sl] * xs
                h_scr[p] = hp * jnp.exp(acl) + _bdot(bgt_bf, xdt * jnp.exp(acl - ac))

    row = lambda w: pl.BlockSpec((1, w), lambda b, c: (0, 0))
    y, hall = pl.pallas_call(
        body, name="ssd_fwd", grid=(bsz, nc),
        in_specs=[pl.BlockSpec((1, CHUNK, CONV_CH), lambda b, c: (b, c, 0)),
                  pl.BlockSpec((1, CHUNK, LANES), lambda b, c: (b, c, 0)),
                  pl.BlockSpec((1, LANES, CHUNK), lambda b, c: (b, 0, c)),
                  row(LANES), pl.BlockSpec((LANES, 1), lambda b, c: (0, 0)),
                  row(LANES), pl.BlockSpec((LANES, 1), lambda b, c: (0, 0)), row(D_MODEL)],
        out_specs=[pl.BlockSpec((1, CHUNK, D_MODEL), lambda b, c: (b, c, 0)),
                   pl.BlockSpec((1, 1, HEAD_PAIRS, SSD_STATE, LANES), lambda b, c: (b, c, 0, 0, 0))],
        out_shape=[jax.ShapeDtypeStruct((bsz, seq, D_MODEL), F32),
                   jax.ShapeDtypeStruct((bsz, nc, HEAD_PAIRS, SSD_STATE, LANES), F32)],
        scratch_shapes=[pltpu.VMEM((HEAD_PAIRS, SSD_STATE, LANES), F32), pltpu.VMEM((CHUNK, D_MODEL), F32),
                        pltpu.VMEM((CHUNK, D_MODEL), F32), pltpu.VMEM((LANES, CHUNK), F32)],
        compiler_params=_params(("parallel", "arbitrary")),
    )(x3, dtf.reshape(bsz, seq, LANES), dtft, bias, biast, arow, acol, dfull)
    return y.reshape(bsz * seq, D_MODEL), hall


def _ssd_bwd(dy, xbc, dtf, dtft, bias, biast, arow, acol, dfull, hall, bsz, seq):
    nc = seq // CHUNK
    x3 = xbc.reshape(bsz, seq, CONV_CH)
    dy3 = dy.reshape(bsz, seq, D_MODEL)

    def body(dy_ref, xbc_ref, dtf_ref, dtft_ref, bias_ref, biast_ref, arow_ref, acol_ref, dfull_ref, hall_ref,
             dx_ref, ddt_ref, acc_ref, dh_scr, dtfull_scr, acfull_scr, acr_scr, csum_scr, dacf_scr, ddtf_scr, ddl_scr):
        first = (pl.program_id(0) == 0) & (pl.program_id(1) == 0)

        @pl.when(first)
        def _():
            acc_ref[...] = jnp.zeros_like(acc_ref)

        @pl.when(pl.program_id(1) == 0)
        def _():
            dh_scr[...] = jnp.zeros_like(dh_scr)

        dtc, acum = _ssd_prologue(dtf_ref, dtft_ref, bias_ref, biast_ref, arow_ref, acol_ref,
                                  dtfull_scr, acfull_scr, acr_scr)
        csum_scr[...] = jnp.zeros_like(csum_scr)
        lane = _iota((CHUNK, LANES), 1)
        last_row = _iota((CHUNK, LANES), 0) == CHUNK - 1
        rs16 = jnp.zeros((CHUNK, LANES), F32)
        for g in range(2):
            bsl = slice(D_MODEL + LANES * g, D_MODEL + LANES * (g + 1))
            csl = slice(D_MODEL + 2 * LANES + LANES * g, D_MODEL + 2 * LANES + LANES * (g + 1))
            bg_bf = xbc_ref[0, :, bsl].astype(BF16)
            cg = xbc_ref[0, :, csl]
            cg_bf = cg.astype(BF16)
            cgt_bf = cg.T.astype(BF16)
            gmat = _bdot_nt(cg_bf, bg_bf)
            dg = jnp.zeros((CHUNK, CHUNK), F32)
            dbg = jnp.zeros((CHUNK, SSD_STATE), F32)
            dcg = jnp.zeros((CHUNK, SSD_STATE), F32)
            for j in range(4):
                p = 4 * g + j
                sl = slice(LANES * p, LANES * (p + 1))
                xs = xbc_ref[0, :, sl]
                dt = dtfull_scr[:, sl]
                ac = acfull_scr[:, sl]
                acl = acfull_scr[CHUNK - 1:CHUNK, sl]
                dyp = dy_ref[0, :, sl]
                xdt = xs * dt
                xdt_bf = xdt.astype(BF16)
                e = jnp.exp(ac)
                dsd = jnp.exp(acl - ac)
                cd = jnp.exp(acl)
                xds_bf = (xdt * dsd).astype(BF16)
                hp = hall_ref[0, 0, p]
                hp_bf = hp.astype(BF16)
                dhn = dh_scr[p]
                dhn_bf = dhn.astype(BF16)
                yo = _bdot(cg_bf, hp_bf) * e
                dw_bf = (dyp * e).astype(BF16)
                dcg = dcg + _bdot_nt(dw_bf, hp_bf)
                dhin = _bdot(cgt_bf, dw_bf)
                dac = dyp * yo
                dacl = jnp.sum(dhn * hp, axis=0, keepdims=True) * cd
                dxds = _bdot(bg_bf, dhn_bf)
                dbg = dbg + _bdot_nt(xds_bf, dhn_bf)
                dxdt = dxds * dsd
                tdec = dxds * xdt * dsd
                dacl = dacl + jnp.sum(tdec, axis=0, keepdims=True)
                dac = dac - tdec
                dh_scr[p] = dhn * cd + dhin
                for hh in range(2):
                    h = 2 * p + hh
                    lm = (lane < HEAD_DIM) if hh == 0 else (lane >= HEAD_DIM)
                    dec = _decay_matrix(acum, acr_scr, h)
                    mmat = gmat * dec
                    dyh_bf = jnp.where(lm, dyp, 0.0).astype(BF16)
                    dm = _bdot_nt(dyh_bf, xdt_bf)
                    dxdt = dxdt + _bdot(mmat.T, dyh_bf)
                    dg = dg + dm * dec
                    q = dm * mmat
                    rs16 = rs16 + jnp.where(lane == h, jnp.sum(q, axis=1, keepdims=True), 0.0)
                    csum_scr[h:h + 1, :] = jnp.sum(q, axis=0, keepdims=True)
                dx_ref[0, :, sl] = dfull_ref[:, sl] * dyp + dxdt * dt
                dacf_scr[:, sl] = dac + jnp.where(last_row, dacl, 0.0)
                ddtf_scr[:, sl] = dxdt * xs
                ddl_scr[:, sl] = jnp.broadcast_to(jnp.sum(dyp * xs, axis=0, keepdims=True), (SUBLANES, LANES))
            dg_bf = dg.astype(BF16)
            dx_ref[0, :, bsl] = dbg + _bdot(dg.T, cg_bf)
            dx_ref[0, :, csl] = dcg + _bdot(dg_bf, bg_bf)
        reduce = _head_reduce()
        triu = (_iota((CHUNK, CHUNK), 0) <= _iota((CHUNK, CHUNK), 1)).astype(F32)
        dac16 = _dot_sel(dacf_scr[...], reduce) + rs16 - csum_scr[...].T
        da16 = _sel_dot(triu, dac16)
        ddt16 = da16 * arow_ref[...] + _dot_sel(ddtf_scr[...], reduce)
        ddtraw = ddt16 * _sigmoid(dtf_ref[0] + bias_ref[...])
        ddt_ref[0] = ddtraw
        acc_ref[0:8, :] += jnp.broadcast_to(jnp.sum(da16 * dtc * arow_ref[...], axis=0, keepdims=True), (SUBLANES, LANES))
        acc_ref[8:16, :] += _dot_sel(ddl_scr[...], reduce)
        acc_ref[16:24, :] += jnp.broadcast_to(jnp.sum(ddtraw, axis=0, keepdims=True), (SUBLANES, LANES))

    rev = lambda b, c: (b, nc - 1 - c, 0)
    row = lambda w: pl.BlockSpec((1, w), lambda b, c: (0, 0))
    dx, ddt, acc = pl.pallas_call(
        body, name="ssd_bwd", grid=(bsz, nc),
        in_specs=[pl.BlockSpec((1, CHUNK, D_MODEL), rev), pl.BlockSpec((1, CHUNK, CONV_CH), rev),
                  pl.BlockSpec((1, CHUNK, LANES), rev),
                  pl.BlockSpec((1, LANES, CHUNK), lambda b, c: (b, 0, nc - 1 - c)),
                  row(LANES), pl.BlockSpec((LANES, 1), lambda b, c: (0, 0)),
                  row(LANES), pl.BlockSpec((LANES, 1), lambda b, c: (0, 0)), row(D_MODEL),
                  pl.BlockSpec((1, 1, HEAD_PAIRS, SSD_STATE, LANES), lambda b, c: (b, nc - 1 - c, 0, 0, 0))],
        out_specs=[pl.BlockSpec((1, CHUNK, CONV_CH), rev), pl.BlockSpec((1, CHUNK, LANES), rev),
                   pl.BlockSpec((3 * SUBLANES, LANES), lambda b, c: (0, 0))],
        out_shape=[jax.ShapeDtypeStruct((bsz, seq, CONV_CH), F32), jax.ShapeDtypeStruct((bsz, seq, LANES), F32),
                   jax.ShapeDtypeStruct((3 * SUBLANES, LANES), F32)],
        scratch_shapes=[pltpu.VMEM((HEAD_PAIRS, SSD_STATE, LANES), F32), pltpu.VMEM((CHUNK, D_MODEL), F32),
                        pltpu.VMEM((CHUNK, D_MODEL), F32), pltpu.VMEM((LANES, CHUNK), F32),
                        pltpu.VMEM((LANES, CHUNK), F32), pltpu.VMEM((CHUNK, D_MODEL), F32),
                        pltpu.VMEM((CHUNK, D_MODEL), F32), pltpu.VMEM((SUBLANES, D_MODEL), F32)],
        compiler_params=_params(("arbitrary", "arbitrary")),
    )(dy3, x3, dtf.reshape(bsz, seq, LANES), dtft, bias, biast, arow, acol, dfull, hall)
    return dx.reshape(bsz * seq, CONV_CH), ddt.reshape(bsz * seq, LANES), acc


GROUP_W = D_MODEL // 2


def _gnorm_fwd(y, z, o_att, w):
    n = y.shape[0]
    tm = min(1024, n)

    def body(y_ref, z_ref, o_ref, w_ref, out_ref):
        zv = z_ref[...]
        g = y_ref[...] * (zv * _sigmoid(zv))
        for gi in range(2):
            sl = slice(GROUP_W * gi, GROUP_W * (gi + 1))
            gs = g[:, sl]
            r = lax.rsqrt(jnp.mean(gs * gs, axis=-1, keepdims=True) + NORM_EPS)
            out_ref[:, sl] = (gs * r * w_ref[:, sl]).astype(BF16)
        out_ref[:, D_MODEL:] = o_ref[...].astype(BF16)

    tok = pl.BlockSpec((tm, D_MODEL), lambda i: (i, 0))
    return pl.pallas_call(
        body, name="gnorm_fwd", grid=(n // tm,),
        in_specs=[tok, tok, tok, pl.BlockSpec((1, D_MODEL), lambda i: (0, 0))],
        out_specs=pl.BlockSpec((tm, MIX_WIDTH), lambda i: (i, 0)),
        out_shape=jax.ShapeDtypeStruct((n, MIX_WIDTH), BF16),
        compiler_params=_params(("parallel",)),
    )(y, z, o_att, w)


def _gnorm_bwd(dycat, y, z, w):
    n = y.shape[0]
    tm = min(512, n)

    def body(dn_ref, y_ref, z_ref, w_ref, dy_ref, dz_ref, gw_ref):
        zv, yv = z_ref[...], y_ref[...]
        sz = _sigmoid(zv)
        sil = zv * sz
        g = yv * sil

        @pl.when(pl.program_id(0) == 0)
        def _():
            gw_ref[...] = jnp.zeros_like(gw_ref)

        for gi in range(2):
            sl = slice(GROUP_W * gi, GROUP_W * (gi + 1))
            gs = g[:, sl]
            r = lax.rsqrt(jnp.mean(gs * gs, axis=-1, keepdims=True) + NORM_EPS)
            nrm = gs * r
            dyn = dn_ref[:, sl]
            dn = dyn * w_ref[:, sl]
            dgs = r * (dn - nrm * jnp.mean(dn * nrm, axis=-1, keepdims=True))
            dy_ref[:, sl] = dgs * sil[:, sl]
            dz_ref[:, sl] = (dgs * yv[:, sl] * (sz[:, sl] * (1.0 + zv[:, sl] * (1.0 - sz[:, sl])))).astype(BF16)
            gw_ref[:, sl] += jnp.sum(dyn * nrm, axis=0, keepdims=True)

    tok = pl.BlockSpec((tm, D_MODEL), lambda i: (i, 0))
    row = pl.BlockSpec((1, D_MODEL), lambda i: (0, 0))
    return pl.pallas_call(
        body, name="gnorm_bwd", grid=(n // tm,),
        in_specs=[tok, tok, tok, row], out_specs=[tok, tok, row],
        out_shape=[jax.ShapeDtypeStruct((n, D_MODEL), F32), jax.ShapeDtypeStruct((n, D_MODEL), BF16),
                   jax.ShapeDtypeStruct((1, D_MODEL), F32)],
        compiler_params=_params(("arbitrary",)),
    )(dycat, y, z, w)


AUX_C = 3
AUX_ONE = 3


def _aux_base(hh):
    return HEAD_DIM * (1 - hh)


def _fox_prep(dtf, fb, bsz, seq):
    def body(x_ref, b_ref, aux_ref):
        tri = (_iota((CHUNK, CHUNK), 1) <= _iota((CHUNK, CHUNK), 0)).astype(F32)
        row, col = _iota((LANES, D_MODEL), 0), _iota((LANES, D_MODEL), 1)
        lane = jnp.bitwise_and(col, LANES - 1)
        other = (lane < HEAD_DIM).astype(jnp.int32)
        term = lane - HEAD_DIM * (1 - other)
        src = F_COL + 2 * jnp.right_shift(col, 7) + other
        place = [jnp.where((row == src) & (term == i), -1.0, 0.0).astype(BF16) for i in range(AUX_C)]
        lane1 = jnp.bitwise_and(_iota((1, D_MODEL), 1), LANES - 1)
        ones_lane = (lane1 - HEAD_DIM * (1 - (lane1 < HEAD_DIM).astype(jnp.int32)) == AUX_ONE).astype(F32)
        carry = jnp.zeros((1, LANES), F32)
        for j in range(seq // CHUNK):
            sl = slice(CHUNK * j, CHUNK * (j + 1))
            lf = _log_sigmoid(x_ref[sl, :] + b_ref[...])
            c = _sel_dot(tri, lf) + carry
            carry = carry + jnp.sum(lf, axis=0, keepdims=True)
            t1, t2, t3 = (jnp.dot(t, p, preferred_element_type=F32) for t, p in zip(_split3(c), place))
            aux_ref[sl, :] = (t1 + t2 + t3 + ones_lane).astype(BF16)

    return pl.pallas_call(
        body, name="fox_prep", grid=(bsz,),
        in_specs=[pl.BlockSpec((seq, LANES), lambda b: (b, 0)), pl.BlockSpec((1, LANES), lambda b: (0, 0))],
        out_specs=pl.BlockSpec((seq, D_MODEL), lambda b: (b, 0)),
        out_shape=jax.ShapeDtypeStruct((bsz * seq, D_MODEL), BF16),
        compiler_params=_params(("parallel",)),
    )(dtf, fb)


def _fox_prep_bwd(dck, dcq, ddt, dtf, fb, bsz, seq):
    nj = seq // CHUNK

    def body(dck_ref, dcq_ref, ddt_ref, x_ref, b_ref, out_ref, gb_ref):
        triu = (_iota((CHUNK, CHUNK), 0) <= _iota((CHUNK, CHUNK), 1)).astype(F32)
        is_f = (_iota((CHUNK, LANES), 1) >= F_COL) & (_iota((CHUNK, LANES), 1) < 2 * F_COL)
        carry = jnp.zeros((1, LANES), F32)
        total = jnp.zeros((1, LANES), F32)
        for j in range(nj - 1, -1, -1):
            sl = slice(CHUNK * j, CHUNK * (j + 1))
            dcv = dck_ref[sl, :] + dcq_ref[sl, :]
            dlf = _sel_dot(triu, dcv) + carry
            carry = carry + jnp.sum(dcv, axis=0, keepdims=True)
            df = jnp.where(is_f, dlf * _sigmoid(-(x_ref[sl, :] + b_ref[...])), 0.0)
            out_ref[sl, :] = (df + ddt_ref[sl, :]).astype(BF16)
            total = total + jnp.sum(df, axis=0, keepdims=True)

        @pl.when(pl.program_id(0) == 0)
        def _():
            gb_ref[...] = jnp.zeros_like(gb_ref)

        gb_ref[...] += jnp.broadcast_to(total, (SUBLANES, LANES))

    blk = pl.BlockSpec((seq, LANES), lambda b: (b, 0))
    return pl.pallas_call(
        body, name="fox_prep_bwd", grid=(bsz,),
        in_specs=[blk, blk, blk, blk, pl.BlockSpec((1, LANES), lambda b: (0, 0))],
        out_specs=[blk, pl.BlockSpec((SUBLANES, LANES), lambda b: (0, 0))],
        out_shape=[jax.ShapeDtypeStruct((bsz * seq, LANES), BF16), jax.ShapeDtypeStruct((SUBLANES, LANES), F32)],
        compiler_params=_params(("arbitrary",)),
    )(dck, dcq, ddt, dtf, fb)


ATT_BLOCK_FWD = 512
ATT_BLOCK_KEYS = 256
ATT_BLOCK_BWD = 256
ATT_BLOCK_BWD_Q = 256


def _att_operands(q_ref, k_ref, aux_ref, hh):
    lane = _iota((1, LANES), 1)
    lm = (lane < HEAD_DIM) if hh == 0 else (lane >= HEAD_DIM)
    base = _aux_base(hh)
    zero = jnp.zeros((1, LANES), BF16)
    ka = jnp.where(lm, k_ref[...], jnp.where((lane >= base) & (lane <= base + AUX_ONE), aux_ref[...], zero))
    qa = jnp.where(lm, q_ref[...] * ATT_SCALE,
                   jnp.where((lane >= base) & (lane < base + AUX_C), jnp.ones((1, LANES), BF16), zero))
    return lm, base, qa, ka


def _att_fwd(qkv, ccol, bsz, seq, gather):
    bq, bk = min(ATT_BLOCK_FWD, seq), min(ATT_BLOCK_KEYS, seq)
    nq, ratio = seq // bq, bq // bk
    nx = len(gather)

    def body(*refs):
        q_ref, k_ref, v_ref, c_ref = refs[:4]
        x_refs = refs[4:4 + nx]
        o_ref, lse_ref = refs[4 + nx:6 + nx]
        xo_refs = refs[6 + nx:6 + 2 * nx]
        qa_scr, ka_scr, vt_scr = refs[6 + 2 * nx:9 + 2 * nx]
        sems = refs[9 + 2 * nx:]
        pair = pl.program_id(1)

        @pl.when((pl.program_id(0) == 0) & (pair == 0))
        def _():
            _exchange_start("gather", x_refs, xo_refs, *sems)

        lse_ref[...] = jnp.zeros_like(lse_ref)
        for hh in range(2):
            _, _, qa, ka = _att_operands(q_ref, k_ref, c_ref, hh)
            qa_scr[hh] = qa
            ka_scr[hh] = ka
        for t0 in range(0, seq, bq):
            vt_scr[:, t0:t0 + bq] = v_ref[t0:t0 + bq, :].astype(F32).T.astype(BF16)
        key_minus_query = _iota((bk, bq), 0) - _iota((bk, bq), 1)

        def q_step(i, carry0):
            qrows = pl.ds(pl.multiple_of(i * bq, bq), bq)

            def scores(j):
                krows = pl.ds(pl.multiple_of(j * bk, bk), bk)
                return tuple(_bdot_nt(ka_scr[hh, krows, :], qa_scr[hh, qrows, :]) for hh in range(2))

            def update(state, st_pair, j, diag):
                krows = pl.ds(pl.multiple_of(j * bk, bk), bk)
                out = []
                for hh in range(2):
                    m, l, acc = state[hh]
                    st = st_pair[hh]
                    if diag is not None:
                        st = jnp.where(key_minus_query + diag * bk <= 0, st, NEG)
                    mn = jnp.maximum(m, jnp.max(st, axis=0, keepdims=True))
                    alpha = jnp.exp(m - mn)
                    pt = jnp.exp(st - mn)
                    l = alpha * l + jnp.sum(pt, axis=0, keepdims=True)
                    vt = vt_scr[HEAD_DIM * hh:HEAD_DIM * (hh + 1), krows]
                    out.append((mn, l, alpha * acc + _bdot(vt, pt)))
                return tuple(out)

            def step(j, carry):
                state, s_cur = carry
                s_next = scores(j + 1)
                return update(state, s_cur, j, None), s_next

            one = (jnp.full((1, bq), NEG, F32), jnp.zeros((1, bq), F32), jnp.zeros((HEAD_DIM, bq), F32))
            first_diag = i * ratio
            state, s_cur = lax.fori_loop(0, first_diag, step, ((one, one), scores(0)))
            for r in range(ratio):
                s_next = scores(first_diag + r + 1) if r + 1 < ratio else None
                state = update(state, s_cur, first_diag + r, r)
                s_cur = s_next
            (m0, l0, acc0), (m1, l1, acc1) = state
            ot = jnp.concatenate([acc0 * (1.0 / l0), acc1 * (1.0 / l1)], axis=0)
            o_ref[qrows, :] = ot.T
            lse_ref[0, 0, 0:1, qrows] = m0 + jnp.log(l0)
            lse_ref[0, 0, 1:2, qrows] = m1 + jnp.log(l1)
            return carry0

        lax.fori_loop(0, nq, q_step, 0)

        @pl.when((pl.program_id(0) == bsz - 1) & (pair == HEAD_PAIRS - 1))
        def _():
            _exchange_wait("gather", x_refs, xo_refs, *sems)

    col = lambda off: pl.BlockSpec((seq, LANES), lambda b, p: (b, off + p))
    head2 = pltpu.VMEM((2, seq, LANES), BF16)
    hbm = pl.BlockSpec(memory_space=pl.ANY)
    return pl.pallas_call(
        body, name="att_fwd", grid=(bsz, HEAD_PAIRS),
        in_specs=[col(0), col(HEAD_PAIRS), col(2 * HEAD_PAIRS), col(0)] + [hbm] * nx,
        out_specs=[pl.BlockSpec((seq, LANES), lambda b, p: (b, p)),
                   pl.BlockSpec((1, 1, SUBLANES, seq), lambda b, p: (b, p, 0, 0))] + [hbm] * nx,
        out_shape=[jax.ShapeDtypeStruct((bsz * seq, D_MODEL), F32),
                   jax.ShapeDtypeStruct((bsz, HEAD_PAIRS, SUBLANES, seq), F32)] + _exchange_shapes("gather", gather),
        scratch_shapes=[head2, head2, pltpu.VMEM((LANES, seq), BF16)] + _exchange_scratch(nx),
        compiler_params=_params(("arbitrary", "arbitrary")),
    )(qkv, qkv, qkv, ccol, *gather)


def _att_bwd(qkv, dycat, o, lse, ccol, bsz, seq, scatter):
    blk = min(ATT_BLOCK_BWD, seq)
    bq = min(ATT_BLOCK_BWD_Q, seq)
    nb, nq, ratio = seq // blk, seq // bq, bq // blk
    nx = len(scatter)

    def body(*refs):
        q_ref, k_ref, v_ref, do_ref, o_ref, lse_ref, c_ref = refs[:7]
        x_refs = refs[7:7 + nx]
        dq_ref, dk_ref, dv_ref, dck_ref, dcq_ref = refs[7 + nx:12 + nx]
        xo_refs = refs[12 + nx:12 + 2 * nx]
        (qa_scr, ka_scr, va_scr, doa_scr, kat_scr, qat_scr, dot_scr, d_scr, dqt_scr, dkt_scr,
         dvt_scr) = refs[12 + 2 * nx:23 + 2 * nx]
        sems = refs[23 + 2 * nx:]
        pair = pl.program_id(1)

        @pl.when((pl.program_id(0) == 0) & (pair == 0))
        def _():
            _exchange_start("scatter", x_refs, xo_refs, *sems)

        query_minus_key = _iota((blk, bq), 1) - _iota((blk, bq), 0)
        lane_b = _iota((blk, LANES), 1)
        row_t = _iota((LANES, seq), 0)
        masks, bases = [], []
        ones8 = jnp.ones((SUBLANES, LANES), F32)
        for hh in range(2):
            lm, base, qa, ka = _att_operands(q_ref, k_ref, c_ref, hh)
            masks.append(lm)
            bases.append(base)
            qa_scr[hh] = qa
            ka_scr[hh] = ka
            va_scr[hh] = jnp.where(lm, v_ref[...], jnp.zeros((seq, LANES), BF16))
            doa = jnp.where(lm, do_ref[...], 0.0).astype(BF16)
            doa_scr[hh] = doa
            for t0 in range(0, seq, blk):
                kat_scr[hh, :, t0:t0 + blk] = ka[t0:t0 + blk, :].astype(F32).T.astype(BF16)
                qat_scr[hh, :, t0:t0 + blk] = qa[t0:t0 + blk, :].astype(F32).T.astype(BF16)
            d1, d2, d3 = (_bdot_nt(ones8, term) for term in _split3(doa.astype(F32) * o_ref[...]))
            d_scr[hh] = d1 + d2 + d3
        for t0 in range(0, seq, blk):
            dot_scr[:, t0:t0 + blk] = do_ref[t0:t0 + blk, :].astype(BF16).astype(F32).T.astype(BF16)
        dqt_scr[...] = jnp.zeros_like(dqt_scr)
        dck_ref[...] = jnp.zeros_like(dck_ref)

        def kv_step(j, carry0):
            krows = pl.ds(pl.multiple_of(j * blk, blk), blk)
            dkt_scr[...] = jnp.zeros_like(dkt_scr)
            dvt_scr[...] = jnp.zeros_like(dvt_scr)

            def scores(i):
                rows = pl.ds(pl.multiple_of(i * bq, bq), bq)
                return tuple((_bdot_nt(ka_scr[hh, krows, :], qa_scr[hh, rows, :]),
                              _bdot_nt(va_scr[hh, krows, :], doa_scr[hh, rows, :])) for hh in range(2))

            def update(i, sd, masked):
                rows = pl.ds(pl.multiple_of(i * bq, bq), bq)
                for hh in range(2):
                    st, dpt = sd[hh]
                    if masked:
                        st = jnp.where(query_minus_key >= (j % ratio) * blk, st, NEG)
                    pt = jnp.exp(st - lse_ref[0, 0, hh:hh + 1, rows])
                    dst = (pt * (dpt - d_scr[hh, 0:1, rows])).astype(BF16)
                    dvt_scr[hh] += _bdot_nt(dot_scr[HEAD_DIM * hh:HEAD_DIM * (hh + 1), rows], pt)
                    dkt_scr[hh] += _bdot_nt(qat_scr[hh, :, rows], dst)
                    dqt_scr[hh, :, rows] += _bdot(kat_scr[hh, :, krows], dst)

            def q_step(i, sd_cur):
                sd_next = scores(jnp.minimum(i + 1, nq - 1))
                update(i, sd_cur, False)
                return sd_next

            i_diag = j // ratio
            sd_diag = scores(i_diag)
            sd_first = scores(jnp.minimum(i_diag + 1, nq - 1))
            update(i_diag, sd_diag, True)
            lax.fori_loop(i_diag + 1, nq, q_step, sd_first)
            dkt = jnp.where(_iota((LANES, blk), 0) < HEAD_DIM, dkt_scr[0], dkt_scr[1])
            dk_ref[krows, :] = dkt.T.astype(BF16)
            dv_ref[krows, :] = jnp.concatenate([dvt_scr[0], dvt_scr[1]], axis=0).T.astype(BF16)
            for hh in range(2):
                dck_ref[0, 0, hh:hh + 1, krows] = -dkt_scr[hh, bases[hh]:bases[hh] + 1, :]
            return carry0

        lax.fori_loop(0, nb, kv_step, 0)
        for t0 in range(0, seq, blk):
            dq0, dq1 = dqt_scr[0, :, t0:t0 + blk].T, dqt_scr[1, :, t0:t0 + blk].T
            dq_ref[t0:t0 + blk, :] = (jnp.where(lane_b < HEAD_DIM, dq0, dq1) * ATT_SCALE).astype(BF16)
        dcq_ref[...] = jnp.zeros_like(dcq_ref)
        for hh in range(2):
            dcq_ref[0, 0, hh:hh + 1, :] = jnp.sum(jnp.where(row_t == bases[hh] + AUX_ONE, dqt_scr[hh], 0.0),
                                                   axis=0, keepdims=True)

        @pl.when((pl.program_id(0) == bsz - 1) & (pair == HEAD_PAIRS - 1))
        def _():
            _exchange_wait("scatter", x_refs, xo_refs, *sems)

    col = lambda off: pl.BlockSpec((seq, LANES), lambda b, p: (b, off + p))
    rowvec = pl.BlockSpec((1, 1, SUBLANES, seq), lambda b, p: (b, p, 0, 0))
    out = jax.ShapeDtypeStruct((bsz * seq, D_MODEL), BF16)
    rows_shape = jax.ShapeDtypeStruct((bsz, HEAD_PAIRS, SUBLANES, seq), F32)
    head2 = pltpu.VMEM((2, seq, LANES), BF16)
    hbm = pl.BlockSpec(memory_space=pl.ANY)
    return pl.pallas_call(
        body, name="att_bwd", grid=(bsz, HEAD_PAIRS),
        in_specs=[col(0), col(HEAD_PAIRS), col(2 * HEAD_PAIRS), col(HEAD_PAIRS), col(0), rowvec, col(0)] + [hbm] * nx,
        out_specs=[col(0), col(0), col(0), rowvec, rowvec] + [hbm] * nx,
        out_shape=[out, out, out, rows_shape, rows_shape] + _exchange_shapes("scatter", scatter),
        scratch_shapes=[head2, head2, head2, head2, pltpu.VMEM((2, LANES, seq), BF16),
                        pltpu.VMEM((2, LANES, seq), BF16), pltpu.VMEM((LANES, seq), BF16),
                        pltpu.VMEM((2, SUBLANES, seq), F32), pltpu.VMEM((2, LANES, seq), F32),
                        pltpu.VMEM((2, LANES, blk), F32), pltpu.VMEM((2, HEAD_DIM, blk), F32)] + _exchange_scratch(nx),
        compiler_params=_params(("arbitrary", "arbitrary")),
    )(qkv, qkv, qkv, dycat, o, lse, ccol, *scatter)


def _adamw_math(w, g, m, v):
    m = ADAM_B1 * m + (1.0 - ADAM_B1) * g
    v = ADAM_B2 * v + (1.0 - ADAM_B2) * jnp.square(g)
    m_hat = m / ADAM_C1
    v_hat = v / ADAM_C2
    delta = -ADAM_LR * (m_hat / (jnp.sqrt(v_hat) + ADAM_EPS) + ADAM_WD * w)
    return delta, m, v


def _row_tile(rows):
    for tr in (256, 128, 64, 32, 16, 8):
        if rows % tr == 0:
            return tr
    return rows


def _sum_parts(parts, name):
    nparts, rows, cols = parts.shape
    tr = rows if rows % SUBLANES else _row_tile(rows)

    def body(p_ref, o_ref):
        g = p_ref[0].astype(F32)
        for s in range(1, nparts):
            g = g + p_ref[s].astype(F32)
        o_ref[...] = g

    return pl.pallas_call(
        body, name=name, grid=(rows // tr,),
        in_specs=[pl.BlockSpec((nparts, tr, cols), lambda i: (0, i, 0))],
        out_specs=pl.BlockSpec((tr, cols), lambda i: (i, 0)),
        out_shape=jax.ShapeDtypeStruct((rows, cols), F32),
        compiler_params=_params(("parallel",)),
    )(parts)


def _adamw_parts(parts, w, m, v, name):
    nparts, rows, cols = parts.shape
    tr = _row_tile(rows)

    def body(p_ref, w_ref, m_ref, v_ref, g_ref, d_ref, mo_ref, vo_ref):
        g = p_ref[0].astype(F32)
        for s in range(1, nparts):
            g = g + p_ref[s].astype(F32)
        delta, mn, vn = _adamw_math(w_ref[...], g, m_ref[...], v_ref[...])
        g_ref[...] = g
        d_ref[...] = delta
        mo_ref[...] = mn
        vo_ref[...] = vn

    blk = pl.BlockSpec((tr, cols), lambda i: (i, 0))
    shp = jax.ShapeDtypeStruct((rows, cols), F32)
    return pl.pallas_call(
        body, name=name, grid=(rows // tr,),
        in_specs=[pl.BlockSpec((nparts, tr, cols), lambda i: (0, i, 0)), blk, blk, blk],
        out_specs=[blk, blk, blk, blk], out_shape=[shp, shp, shp, shp],
        compiler_params=_params(("parallel",)),
    )(parts, w, m, v)


def _me():
    return lax.axis_index("x"), lax.axis_index("y"), lax.axis_index("c")


def _flip(pos, k):
    x, y, c = pos
    kx, ky, kc = (k >> 2) & 1, (k >> 1) & 1, k & 1
    return (x ^ kx if kx else x, y ^ ky if ky else y, c ^ kc if kc else c)


def _logical(pos):
    return 4 * pos[0] + 2 * pos[1] + pos[2]


def _all_gather_two_level(shards):
    narr = len(shards)

    def body(*refs):
        x_refs, out_refs = refs[:narr], refs[narr:2 * narr]
        send_sems, recv_sems, local_sems = refs[2 * narr:]
        x, y, c = _me()
        me, sibling = (x, y, c), (x, y, 1 - c)
        chips = [(1 - x, y), (x, 1 - y), (1 - x, 1 - y)]

        def copy(a, k, block, to, src=None):
            slot = out_refs[a].at[_logical(block)]
            return pltpu.make_async_remote_copy(
                src_ref=slot if src is None else src, dst_ref=slot,
                send_sem=send_sems.at[a, k], recv_sem=recv_sems.at[a, k], device_id=to, device_id_type=MESH)

        mine = [pltpu.make_async_copy(x_refs[a], out_refs[a].at[_logical(me)], local_sems.at[a]) for a in range(narr)]
        for cp in mine:
            cp.start()
        first = []
        for a in range(narr):
            first.append(copy(a, 0, me, sibling, src=x_refs[a]))
            first += [copy(a, 1 + j, me, (*chip, c), src=x_refs[a]) for j, chip in enumerate(chips)]
        for cp in first:
            cp.start()
        passed = []
        for a in range(narr):
            for j, chip in enumerate(chips):
                copy(a, 1 + j, (*chip, c), me).wait_recv()
                fwd = copy(a, 4 + j, (*chip, c), sibling)
                fwd.start()
                passed.append(fwd)
        for a in range(narr):
            copy(a, 0, sibling, me).wait_recv()
            for j, chip in enumerate(chips):
                copy(a, 4 + j, (*chip, 1 - c), me).wait_recv()
        for cp in first + passed:
            cp.wait_send()
        for cp in mine:
            cp.wait()

    hbm = pl.BlockSpec(memory_space=pl.ANY)
    return pl.pallas_call(
        body, name="all_gather_big",
        out_shape=[jax.ShapeDtypeStruct((N_DEV,) + s.shape, s.dtype) for s in shards],
        in_specs=[hbm] * narr, out_specs=[hbm] * narr,
        scratch_shapes=[pltpu.SemaphoreType.DMA((narr, 7)), pltpu.SemaphoreType.DMA((narr, 7)),
                        pltpu.SemaphoreType.DMA((narr,))],
    )(*shards)


def _exchange_copies(kind, x_refs, out_refs, send_sems, recv_sems, local_sems):
    me = _me()
    mine = _logical(me)
    local, remote = [], []
    for a, (x_ref, out_ref) in enumerate(zip(x_refs, out_refs)):
        own = x_ref if kind == "gather" else x_ref.at[mine]
        local.append(pltpu.make_async_copy(own, out_ref.at[mine], local_sems.at[a]))
        for k in range(1, N_DEV):
            peer = _flip(me, k)
            src = x_ref if kind == "gather" else x_ref.at[_logical(peer)]
            remote.append(pltpu.make_async_remote_copy(
                src_ref=src, dst_ref=out_ref.at[mine], send_sem=send_sems.at[a, k - 1], recv_sem=recv_sems.at[a, k - 1],
                device_id=peer, device_id_type=MESH))
    return local, remote


def _exchange_start(kind, x_refs, out_refs, *sems):
    if not x_refs:
        return
    local, remote = _exchange_copies(kind, x_refs, out_refs, *sems)
    for cp in local + remote:
        cp.start()


def _exchange_wait(kind, x_refs, out_refs, *sems):
    if not x_refs:
        return
    local, remote = _exchange_copies(kind, x_refs, out_refs, *sems)
    for cp in remote:
        cp.wait_recv()
    for cp in remote:
        cp.wait_send()
    for cp in local:
        cp.wait()


def _exchange_shapes(kind, arrays):
    return [jax.ShapeDtypeStruct(((N_DEV,) if kind == "gather" else ()) + a.shape, a.dtype) for a in arrays]


def _exchange_scratch(narrays):
    if not narrays:
        return []
    return [pltpu.SemaphoreType.DMA((narrays, N_DEV - 1)), pltpu.SemaphoreType.DMA((narrays, N_DEV - 1)),
            pltpu.SemaphoreType.DMA((narrays,))]


def _exchange_rows(x_ref, buf_ref, send_sems, recv_sems):
    me = _me()
    buf_ref[_logical(me)] = x_ref[...]
    copies = []
    for k in range(1, N_DEV):
        peer = _flip(me, k)
        copies.append(pltpu.make_async_remote_copy(
            src_ref=x_ref, dst_ref=buf_ref.at[_logical(me)],
            send_sem=send_sems.at[k - 1], recv_sem=recv_sems.at[k - 1], device_id=peer, device_id_type=MESH))
    for cp in copies:
        cp.start()
    for cp in copies:
        cp.wait_recv()
    for cp in copies:
        cp.wait_send()


def _sum_slots(buf_ref):
    total = buf_ref[0]
    for s in range(1, N_DEV):
        total = total + buf_ref[s]
    return total


def _small_gather(x):
    def body(x_ref, o_ref, buf_ref, send_sems, recv_sems):
        _exchange_rows(x_ref, buf_ref, send_sems, recv_sems)
        o_ref[...] = _sum_slots(buf_ref)

    return pl.pallas_call(
        body, name="small_gather", out_shape=jax.ShapeDtypeStruct(x.shape, F32),
        in_specs=[pl.BlockSpec(memory_space=pltpu.VMEM)], out_specs=pl.BlockSpec(memory_space=pltpu.VMEM),
        scratch_shapes=[pltpu.VMEM((N_DEV,) + x.shape, F32), pltpu.SemaphoreType.DMA((7,)), pltpu.SemaphoreType.DMA((7,))],
    )(x)


def _small_reduce_adamw(g, w, m, v):
    def body(g_ref, w_ref, m_ref, v_ref, go_ref, d_ref, mo_ref, vo_ref, buf_ref, send_sems, recv_sems):
        _exchange_rows(g_ref, buf_ref, send_sems, recv_sems)
        gs = _sum_slots(buf_ref)
        delta, mn, vn = _adamw_math(w_ref[...], gs, m_ref[...], v_ref[...])
        go_ref[...] = gs
        d_ref[...] = delta
        mo_ref[...] = mn
        vo_ref[...] = vn

    vm = pl.BlockSpec(memory_space=pltpu.VMEM)
    shp = jax.ShapeDtypeStruct(g.shape, F32)
    return pl.pallas_call(
        body, name="small_reduce_adamw", out_shape=[shp, shp, shp, shp],
        in_specs=[vm, vm, vm, vm], out_specs=[vm, vm, vm, vm],
        scratch_shapes=[pltpu.VMEM((N_DEV,) + g.shape, F32), pltpu.SemaphoreType.DMA((7,)), pltpu.SemaphoreType.DMA((7,))],
    )(g, w, m, v)


def _pad_cols(a, width):
    return jnp.pad(a, ((0, 0), (0, width - a.shape[1])))


def _local_step(x, target, norm_mix_w, w_in_t, conv_w, conv_b, dt_bias, a_log, d_skip, ssd_norm_w, f_bias,
                late_shards, norm_mlp_w, norm_final_w):
    bsz, seq, _ = x.shape
    n = bsz * seq
    x2 = x.reshape(n, D_MODEL)
    t2 = target.reshape(n, D_MODEL)
    nfw = norm_final_w.reshape(1, D_MODEL)

    bias = _pad_cols(dt_bias, LANES)
    arow = _pad_cols(-jnp.exp(a_log), LANES)
    biast, acol = bias.reshape(LANES, 1), arow.reshape(LANES, 1)
    dfull = jnp.repeat(d_skip, HEAD_DIM, axis=1)
    fb = jnp.pad(f_bias, ((0, 0), (F_COL, LANES - 2 * F_COL)))

    wt_z, wt_xbc, wt_qkv = w_in_t[:ROW_XBC], w_in_t[ROW_XBC:ROW_DT], w_in_t[ROW_Q:ROW_F]
    wt_dtf = jnp.concatenate([w_in_t[ROW_DT:ROW_Q], w_in_t[ROW_F:], jnp.zeros((LANES - 2 * F_COL, D_MODEL), BF16)], axis=0)
    wt_q, wt_k, wt_v = wt_qkv[:D_MODEL], wt_qkv[D_MODEL:2 * D_MODEL], wt_qkv[2 * D_MODEL:]

    h1 = _rms_fwd(x2, norm_mix_w, "rms_fwd_mix")
    z = _mm([(h1, wt_z)], "inproj_z", F32, 1024, 1024, nt=True)
    xbc_raw = _mm([(h1, wt_xbc)], "inproj_xbc", F32, 512, 1536, nt=True)
    qkv = _mm([(h1, wt_qkv)], "inproj_qkv", BF16, 512, 3072, nt=True)
    dtf = _mm([(h1, wt_dtf)], "inproj_dtf", F32, 2048, LANES, nt=True)
    dtft = dtf.reshape(bsz, seq, LANES).transpose(0, 2, 1)

    xbc = _conv_fwd(xbc_raw, conv_w, conv_b, bsz, seq)
    y_pre, hall = _ssd_fwd(xbc, dtf, dtft, bias, biast, arow, acol, dfull, bsz, seq)

    ccol = _fox_prep(dtf, fb, bsz, seq)
    o_att, lse, w_out, w_up_t, w_down = _att_fwd(qkv, ccol, bsz, seq, late_shards)
    w_out, w_up_t, w_down = (w.reshape(-1, D_MODEL) for w in (w_out, w_up_t, w_down))

    ycat = _gnorm_fwd(y_pre, z, o_att, ssd_norm_w)
    h2, h2n = _mm([(ycat, w_out)], "outproj", F32, 512, 1024, res=x2, rms_fwd=norm_mlp_w)
    pre = _mm([(h2n, w_up_t)], "mlp_up", BF16, 512, 4096, nt=True)

    dh3, loss_row, g_nfw = _mlp_down_loss(pre, w_down, h2, t2, nfw)
    dpre, u = _mlp_down_bwd(dh3, w_down, pre)
    g_w_down = _mm_tn(u, dh3, "grad_w_down", 1024, 1024, 2048)
    g_w_up_t = _mm_tn(dpre, h2n, "grad_w_up", 1024, 1024, 2048)
    by_shard = lambda g: g.reshape(N_DEV, g.shape[0] // N_DEV, D_MODEL)
    dh2, g_nmlp = _mm([(dpre, w_up_t)], "mlp_up_bwd", F32, 512, 1024, res=dh3, rms_bwd=(h2, norm_mlp_w))

    g_w_out = _mm_tn(ycat, dh2, "grad_w_out", 1024, 1024, 2048)
    dycat = _mm([(dh2, w_out)], "outproj_bwd", F32, 512, 2048, nt=True)
    dy_pre, dz, g_ssdn = _gnorm_bwd(dycat, y_pre, z, ssd_norm_w)
    dq, dk, dv, dck, dcq, recv_down, recv_up_t, recv_out = _att_bwd(
        qkv, dycat, o_att, lse, ccol, bsz, seq, [by_shard(g_w_down), by_shard(g_w_up_t), by_shard(g_w_out)])

    dxbc, ddt, ssd_acc = _ssd_bwd(dy_pre, xbc, dtf, dtft, bias, biast, arow, acol, dfull, hall, bsz, seq)
    dxbc_raw, g_cw, g_cb = _conv_bwd(dxbc, xbc_raw, conv_w, conv_b, bsz, seq)

    def head_cols(rows):
        cols = rows[:, :, :2, :].reshape(bsz, SSD_HEADS, seq).transpose(0, 2, 1).reshape(n, SSD_HEADS)
        return jnp.pad(cols, ((0, 0), (F_COL, LANES - 2 * F_COL)))

    ddtf, g_fb = _fox_prep_bwd(head_cols(dck), head_cols(dcq), ddt, dtf, fb, bsz, seq)

    g_dtf = _mm_tn(ddtf, h1, "grad_w_in_dtf", LANES, 1024, 2048)
    g_w_in_t = jnp.concatenate([
        _mm_tn(dz, h1, "grad_w_in_z", 1024, 1024, 2048), _mm_tn(dxbc_raw, h1, "grad_w_in_xbc", 768, 1024, 2048),
        g_dtf[:F_COL], _mm_tn(dq, h1, "grad_w_in_q", 1024, 1024, 2048), _mm_tn(dk, h1, "grad_w_in_k", 1024, 1024, 2048),
        _mm_tn(dv, h1, "grad_w_in_v", 1024, 1024, 2048), g_dtf[F_COL:2 * F_COL]], axis=0)
    pieces = [(dz, wt_z), (dxbc_raw, wt_xbc), (dq, wt_q), (dk, wt_k), (dv, wt_v), (ddtf, wt_dtf)]
    dx, g_nmix, recv_in_t = _mm(pieces, "inproj_bwd", F32, 256, 1024, res=dh2, rms_bwd=(x2, norm_mix_w),
                                exchange=("scatter", [by_shard(g_w_in_t)]))

    small = dict(
        norm_mix_w=g_nmix, norm_mlp_w=g_nmlp, norm_final_w=g_nfw, ssd_norm_w=g_ssdn, conv_b=g_cb, conv_w=g_cw,
        dt_bias=ssd_acc[16:17, :SSD_HEADS], a_log=ssd_acc[0:1, :SSD_HEADS], d_skip=ssd_acc[8:9, :SSD_HEADS],
        f_bias=g_fb[0:1, F_COL:2 * F_COL])
    recv = dict(w_in_t=recv_in_t, w_out=recv_out, w_up_t=recv_up_t, w_down=recv_down)
    return loss_row[0, 0], dx.reshape(bsz, seq, D_MODEL), small, recv


SMALL_LAYOUT = (("norm_mix_w", 0, 1, 0, D_MODEL), ("norm_mlp_w", 1, 1, 0, D_MODEL), ("norm_final_w", 2, 1, 0, D_MODEL),
                ("ssd_norm_w", 3, 1, 0, D_MODEL), ("conv_b", 4, 1, 0, CONV_CH), ("conv_w", 5, CONV_K, 0, CONV_CH),
                ("dt_bias", 9, 1, 0, SSD_HEADS), ("a_log", 9, 1, LANES, SSD_HEADS), ("d_skip", 9, 1, 2 * LANES, SSD_HEADS),
                ("f_bias", 9, 1, 3 * LANES, SSD_HEADS))
SMALL_ROWS = 2 * SUBLANES


def _pack_small(vals):
    packed = jnp.zeros((SMALL_ROWS, SMALL_COLS), F32)
    for name, r0, nrows, c0, width in SMALL_LAYOUT:
        packed = packed.at[r0:r0 + nrows, c0:c0 + width].set(vals[name].reshape(nrows, width))
    return packed


def _unpack_small(packed, like):
    out = {}
    for name, r0, nrows, c0, width in SMALL_LAYOUT:
        out[name] = packed[r0:r0 + nrows, c0:c0 + width].reshape(like[name].shape)
    return out


def kernel(x, norm_mix_w, w_in, conv_w, conv_b, dt_bias, a_log, d_skip, ssd_norm_w, f_bias, w_out, norm_mlp_w, w_up, w_down, norm_final_w, loss_target, m_norm_mix_w, m_w_in, m_conv_w, m_conv_b, m_dt_bias, m_a_log, m_d_skip, m_ssd_norm_w, m_f_bias, m_w_out, m_norm_mlp_w, m_w_up, m_w_down, m_norm_final_w, v_norm_mix_w, v_w_in, v_conv_w, v_conv_b, v_dt_bias, v_a_log, v_d_skip, v_ssd_norm_w, v_f_bias, v_w_out, v_norm_mlp_w, v_w_up, v_w_down, v_norm_final_w):
    me = 4 * lax.axis_index("x") + 2 * lax.axis_index("y") + lax.axis_index("c")
    conv_shard_w = CONV_CH // N_DEV
    zero = jnp.zeros((), jnp.int32)

    def place_conv(shard):
        return lax.dynamic_update_slice(jnp.zeros((CONV_K, CONV_CH), F32), shard[0], (zero, me * conv_shard_w))

    (g_in_t,) = _all_gather_two_level([w_in[0].T.astype(BF16)])
    late_shards = [w_out[0].astype(BF16), w_up[0].T.astype(BF16), w_down[0].astype(BF16)]
    conv_full = _small_gather(jnp.pad(place_conv(conv_w), ((0, SUBLANES - CONV_K), (0, 0))))[:CONV_K]

    loss_local, grad_x, g_small, recv = _local_step(
        x, loss_target, norm_mix_w, g_in_t.reshape(IN_WIDTH, D_MODEL), conv_full, conv_b, dt_bias, a_log, d_skip,
        ssd_norm_w, f_bias, late_shards, norm_mlp_w, norm_final_w)
    loss = lax.psum(loss_local, MESH_AXES)

    grad_in = _sum_parts(recv["w_in_t"], "sum_w_in").T[None]
    grad_up = _sum_parts(recv["w_up_t"], "sum_w_up").T[None]
    big = {
        "w_in": _adamw_parts(grad_in, w_in[0], m_w_in[0], v_w_in[0], "adamw_w_in"),
        "w_out": _adamw_parts(recv["w_out"], w_out[0], m_w_out[0], v_w_out[0], "adamw_w_out"),
        "w_up": _adamw_parts(grad_up, w_up[0], m_w_up[0], v_w_up[0], "adamw_w_up"),
        "w_down": _adamw_parts(recv["w_down"], w_down[0], m_w_down[0], v_w_down[0], "adamw_w_down"),
    }

    like = dict(norm_mix_w=norm_mix_w, norm_mlp_w=norm_mlp_w, norm_final_w=norm_final_w, ssd_norm_w=ssd_norm_w,
                conv_b=conv_b, conv_w=jnp.zeros((1, CONV_K, CONV_CH), F32), dt_bias=dt_bias, a_log=a_log,
                d_skip=d_skip, f_bias=f_bias)
    w_small = dict(like, conv_w=place_conv(conv_w))
    m_small = dict(norm_mix_w=m_norm_mix_w, norm_mlp_w=m_norm_mlp_w, norm_final_w=m_norm_final_w,
                   ssd_norm_w=m_ssd_norm_w, conv_b=m_conv_b, conv_w=place_conv(m_conv_w), dt_bias=m_dt_bias,
                   a_log=m_a_log, d_skip=m_d_skip, f_bias=m_f_bias)
    v_small = dict(norm_mix_w=v_norm_mix_w, norm_mlp_w=v_norm_mlp_w, norm_final_w=v_norm_final_w,
                   ssd_norm_w=v_ssd_norm_w, conv_b=v_conv_b, conv_w=place_conv(v_conv_w), dt_bias=v_dt_bias,
                   a_log=v_a_log, d_skip=v_d_skip, f_bias=v_f_bias)
    small = _small_reduce_adamw(_pack_small(g_small), _pack_small(w_small), _pack_small(m_small), _pack_small(v_small))
    small = [_unpack_small(s, like) for s in small]
    for s in small:
        s["conv_w"] = lax.dynamic_slice(s["conv_w"], (zero, zero, me * conv_shard_w), (1, CONV_K, conv_shard_w))

    order = ["norm_mix_w", "w_in", "conv_w", "conv_b", "dt_bias", "a_log", "d_skip", "ssd_norm_w", "f_bias", "w_out",
             "norm_mlp_w", "w_up", "w_down", "norm_final_w"]
    outs = [loss, grad_x]
    for kind in range(4):
        for name in order:
            outs.append(big[name][kind][None] if name in big else small[kind][name])
    return tuple(outs)
```

```python
import jax
import jax.numpy as jnp
from jax import lax
from jax.experimental import pallas as pl
from jax.experimental.pallas import tpu as pltpu

F32, BF16 = jnp.float32, jnp.bfloat16
HI = lax.Precision.HIGHEST

D_MODEL = 1024
SSD_HEADS = 16
HEAD_DIM = 64
SSD_STATE = 128
CHUNK = 128
CONV_CH = 1536
CONV_K = 4
D_FF = 4096
MIX_WIDTH = 2048
IN_WIDTH = 5664
NORM_EPS = 1e-5
ATT_SCALE = 0.125

LANES = 128
SUBLANES = 8
HEAD_PAIRS = SSD_HEADS // 2
NEG = -1e30
VMEM_LIMIT = 52 * 1024 * 1024

ROW_XBC, ROW_DT, ROW_Q, ROW_F = 1024, 2560, 2576, 5648
F_COL = SSD_HEADS

N_DEV = 8
MESH_AXES = ("x", "y", "c")
MESH = pl.DeviceIdType.MESH

ADAM_LR, ADAM_B1, ADAM_B2, ADAM_EPS, ADAM_WD, ADAM_STEP = 0.001, 0.9, 0.999, 1e-08, 0.01, 10
ADAM_C1 = 1.0 - ADAM_B1 ** ADAM_STEP
ADAM_C2 = 1.0 - ADAM_B2 ** ADAM_STEP

SMALL_COLS = CONV_CH


def _params(sem=None):
    return pltpu.CompilerParams(dimension_semantics=sem, vmem_limit_bytes=VMEM_LIMIT)


def _sigmoid(u):
    return 1.0 / (1.0 + jnp.exp(-u))


def _softplus(u):
    return jnp.maximum(u, 0.0) + jnp.log(1.0 + jnp.exp(-jnp.abs(u)))


def _log_sigmoid(u):
    return jnp.minimum(u, 0.0) - jnp.log(1.0 + jnp.exp(-jnp.abs(u)))


def _bdot(a, b):
    return jnp.dot(a.astype(BF16), b.astype(BF16), preferred_element_type=F32)


def _bdot_nt(a, b):
    return lax.dot_general(a.astype(BF16), b.astype(BF16), (((1,), (1,)), ((), ())), preferred_element_type=F32)


def _bdot_tn(a, b):
    return lax.dot_general(a.astype(BF16), b.astype(BF16), (((0,), (0,)), ((), ())), preferred_element_type=F32)


def _split3(x):
    x1 = x.astype(BF16)
    r1 = x - x1.astype(F32)
    x2 = r1.astype(BF16)
    return x1, x2, (r1 - x2.astype(F32)).astype(BF16)


def _dot_sel(x, sel):
    s = sel.astype(BF16)
    p1, p2, p3 = (jnp.dot(p, s, preferred_element_type=F32) for p in _split3(x))
    return p1 + p2 + p3


def _sel_dot(sel, x):
    s = sel.astype(BF16)
    p1, p2, p3 = (jnp.dot(s, p, preferred_element_type=F32) for p in _split3(x))
    return p1 + p2 + p3


def _iota(shape, dim):
    return lax.broadcasted_iota(jnp.int32, shape, dim)


def _head_expand():
    return (jnp.right_shift(_iota((LANES, D_MODEL), 1), 6) == _iota((LANES, D_MODEL), 0)).astype(F32)


def _head_reduce():
    return (jnp.right_shift(_iota((D_MODEL, LANES), 0), 6) == _iota((D_MODEL, LANES), 1)).astype(F32)


def _rms_fwd(x, w, name):
    n, d = x.shape
    tm = min(1024, n)

    def body(x_ref, w_ref, o_ref):
        xv = x_ref[...]
        r = lax.rsqrt(jnp.mean(xv * xv, axis=-1, keepdims=True) + NORM_EPS)
        o_ref[...] = (xv * r * w_ref[...]).astype(BF16)

    return pl.pallas_call(
        body, name=name, grid=(n // tm,),
        in_specs=[pl.BlockSpec((tm, d), lambda i: (i, 0)), pl.BlockSpec((1, d), lambda i: (0, 0))],
        out_specs=pl.BlockSpec((tm, d), lambda i: (i, 0)),
        out_shape=jax.ShapeDtypeStruct((n, d), BF16),
        compiler_params=_params(("parallel",)),
    )(x, w)


def _mm(pairs, name, out_dtype, tm, tn, nt=False, res=None, exchange=None, rms_fwd=None, rms_bwd=None):
    m = pairs[0][0].shape[0]
    n = pairs[0][1].shape[0 if nt else 1]
    tm, tn = min(tm, m), min(tn, n)
    gm, gn = m // tm, n // tn
    npairs = len(pairs)
    kind, xs = (None, []) if exchange is None else exchange
    nx = len(xs)
    extra_in = [] if res is None else [res]
    if rms_bwd is not None:
        extra_in += list(rms_bwd)
    elif rms_fwd is not None:
        extra_in.append(rms_fwd)
    n_in = 2 * npairs + len(extra_in)
    n_out = 1 + (rms_fwd is not None or rms_bwd is not None)
    assert (rms_fwd is None and rms_bwd is None) or tn == n

    def body(*refs):
        x_refs, o_refs = refs[n_in:n_in + nx], refs[n_in + nx:n_in + nx + n_out]
        xo_refs, sems = refs[n_in + nx + n_out:n_in + 2 * nx + n_out], refs[n_in + 2 * nx + n_out:]
        extra = refs[2 * npairs:n_in]
        i, j = pl.program_id(0), pl.program_id(1)
        if nx:
            @pl.when((i == 0) & (j == 0))
            def _():
                _exchange_start(kind, x_refs, xo_refs, *sems)

        acc = None
        for p in range(npairs):
            a_v, b_v = refs[2 * p][...], refs[2 * p + 1][...]
            d = _bdot_nt(a_v, b_v) if nt else _bdot(a_v, b_v)
            acc = d if acc is None else acc + d
        if rms_bwd is not None:
            xv = extra[1][...]
            r = lax.rsqrt(jnp.mean(xv * xv, axis=-1, keepdims=True) + NORM_EPS)
            nrm = xv * r
            g = acc * extra[2][...]
            o_refs[0][...] = extra[0][...] + r * (g - nrm * jnp.mean(g * nrm, axis=-1, keepdims=True))

            @pl.when(i == 0)
            def _():
                o_refs[1][...] = jnp.zeros_like(o_refs[1])

            o_refs[1][...] += jnp.sum(acc * nrm, axis=0, keepdims=True)
        else:
            if res is not None:
                acc = extra[0][...] + acc
            o_refs[0][...] = acc.astype(o_refs[0].dtype)
            if rms_fwd is not None:
                r = lax.rsqrt(jnp.mean(acc * acc, axis=-1, keepdims=True) + NORM_EPS)
                o_refs[1][...] = (acc * r * extra[-1][...]).astype(BF16)
        if nx:
            @pl.when((i == gm - 1) & (j == gn - 1))
            def _():
                _exchange_wait(kind, x_refs, xo_refs, *sems)

    tile = pl.BlockSpec((tm, tn), lambda i, j: (i, j))
    row = pl.BlockSpec((1, tn), lambda i, j: (0, j))
    in_specs, args = [], []
    for a, b in pairs:
        k = a.shape[1]
        in_specs.append(pl.BlockSpec((tm, k), lambda i, j: (i, 0)))
        in_specs.append(pl.BlockSpec((tn, k), lambda i, j: (j, 0)) if nt else pl.BlockSpec((k, tn), lambda i, j: (0, j)))
        args += [a, b]
    in_specs += [row if e.shape[0] == 1 else tile for e in extra_in]
    out_specs, out_shape = [tile], [jax.ShapeDtypeStruct((m, n), out_dtype)]
    if rms_bwd is not None:
        out_specs.append(row)
        out_shape.append(jax.ShapeDtypeStruct((1, n), F32))
    elif rms_fwd is not None:
        out_specs.append(tile)
        out_shape.append(jax.ShapeDtypeStruct((m, n), BF16))
    hbm = pl.BlockSpec(memory_space=pl.ANY)
    sequential = nx or rms_bwd is not None
    outs = pl.pallas_call(
        body, name=name, grid=(gm, gn), in_specs=in_specs + [hbm] * nx,
        out_specs=out_specs + [hbm] * nx,
        out_shape=out_shape + _exchange_shapes(kind, xs),
        scratch_shapes=_exchange_scratch(nx),
        compiler_params=_params(("arbitrary", "arbitrary") if sequential else ("parallel", "parallel")),
    )(*args, *extra_in, *xs)
    return outs if len(outs) > 1 else outs[0]


def _mm_tn(a, b, name, tm, tn, tk):
    t, m = a.shape
    n = b.shape[1]
    tm, tn, tk = min(tm, m), min(tn, n), min(tk, t)
    nk = t // tk

    def body(a_ref, b_ref, o_ref, acc_ref):
        kk = pl.program_id(2)

        @pl.when(kk == 0)
        def _():
            acc_ref[...] = jnp.zeros_like(acc_ref)

        acc_ref[...] += _bdot_tn(a_ref[...], b_ref[...])

        @pl.when(kk == nk - 1)
        def _():
            o_ref[...] = acc_ref[...].astype(o_ref.dtype)

    return pl.pallas_call(
        body, name=name, grid=(m // tm, n // tn, nk),
        in_specs=[pl.BlockSpec((tk, tm), lambda i, j, kk: (kk, i)), pl.BlockSpec((tk, tn), lambda i, j, kk: (kk, j))],
        out_specs=pl.BlockSpec((tm, tn), lambda i, j, kk: (i, j)),
        out_shape=jax.ShapeDtypeStruct((m, n), BF16),
        scratch_shapes=[pltpu.VMEM((tm, tn), F32)],
        compiler_params=_params(("parallel", "parallel", "arbitrary")),
    )(a, b)


def _mlp_down_loss(pre, w_down, h2, target, w):
    m, k = pre.shape
    d = w_down.shape[1]
    tm = min(512, m)

    def body(p_ref, b_ref, r_ref, t_ref, w_ref, dh_ref, loss_ref, gw_ref):
        u = jnp.square(jnp.maximum(p_ref[...].astype(F32), 0.0))
        xv = r_ref[...] + _bdot(u, b_ref[...])
        r = lax.rsqrt(jnp.mean(xv * xv, axis=-1, keepdims=True) + NORM_EPS)
        nrm = xv * r
        wv = w_ref[...]
        err = nrm * wv - t_ref[...]
        part = 0.5 * jnp.sum(jnp.mean(err * err, axis=-1, keepdims=True), axis=0, keepdims=True)
        dy = err * (1.0 / d)
        g = dy * wv
        dh_ref[...] = r * (g - nrm * jnp.mean(g * nrm, axis=-1, keepdims=True))

        @pl.when(pl.program_id(0) == 0)
        def _():
            loss_ref[...] = jnp.zeros_like(loss_ref)
            gw_ref[...] = jnp.zeros_like(gw_ref)

        loss_ref[...] += jnp.broadcast_to(part, loss_ref.shape)
        gw_ref[...] += jnp.sum(dy * nrm, axis=0, keepdims=True)

    tok = pl.BlockSpec((tm, d), lambda i: (i, 0))
    row = pl.BlockSpec((1, d), lambda i: (0, 0))
    return pl.pallas_call(
        body, name="mlp_down_loss", grid=(m // tm,),
        in_specs=[pl.BlockSpec((tm, k), lambda i: (i, 0)), pl.BlockSpec((k, d), lambda i: (0, 0)), tok, tok, row],
        out_specs=[tok, pl.BlockSpec((1, LANES), lambda i: (0, 0)), row],
        out_shape=[jax.ShapeDtypeStruct((m, d), F32), jax.ShapeDtypeStruct((1, LANES), F32),
                   jax.ShapeDtypeStruct((1, d), F32)],
        compiler_params=_params(("arbitrary",)),
    )(pre, w_down, h2, target, w)


def _mlp_down_bwd(dh3, w_down, pre):
    m, k = dh3.shape
    n = w_down.shape[0]
    tm, tn = min(256, m), n

    def body(a_ref, b_ref, p_ref, dpre_ref, u_ref):
        r = jnp.maximum(p_ref[...].astype(F32), 0.0)
        du = _bdot_nt(a_ref[...], b_ref[...])
        dpre_ref[...] = (du * (2.0 * r)).astype(BF16)
        u_ref[...] = (r * r).astype(BF16)

    blk = pl.BlockSpec((tm, tn), lambda i, j: (i, j))
    return pl.pallas_call(
        body, name="mlp_down_bwd", grid=(m // tm, n // tn),
        in_specs=[pl.BlockSpec((tm, k), lambda i, j: (i, 0)), pl.BlockSpec((tn, k), lambda i, j: (j, 0)), blk],
        out_specs=[blk, blk],
        out_shape=[jax.ShapeDtypeStruct((m, n), BF16), jax.ShapeDtypeStruct((m, n), BF16)],
        compiler_params=_params(("parallel", "parallel")),
    )(dh3, w_down, pre)


CONV_TC = 512


def _conv_fwd(xbc, cw, cb, bsz, seq):
    tt = min(512, seq)
    nt, hb = seq // tt, tt // SUBLANES
    x3 = xbc.reshape(bsz, seq, CONV_CH)

    def body(xm_ref, xh_ref, w_ref, b_ref, o_ref):
        i = pl.program_id(2)
        x = xm_ref[0]
        halo = jnp.where(i > 0, xh_ref[0], 0.0)
        xx = jnp.concatenate([halo, x], axis=0)
        acc = x * w_ref[3:4, :] + b_ref[...]
        for s in range(1, CONV_K):
            acc = acc + pltpu.roll(xx, s, 0)[SUBLANES:, :] * w_ref[3 - s:4 - s, :]
        o_ref[0] = acc * _sigmoid(acc)

    out = pl.pallas_call(
        body, name="conv_fwd", grid=(CONV_CH // CONV_TC, bsz, nt),
        in_specs=[pl.BlockSpec((1, tt, CONV_TC), lambda c, b, i: (b, i, c)),
                  pl.BlockSpec((1, SUBLANES, CONV_TC), lambda c, b, i: (b, jnp.maximum(i * hb - 1, 0), c)),
                  pl.BlockSpec((CONV_K, CONV_TC), lambda c, b, i: (0, c)),
                  pl.BlockSpec((1, CONV_TC), lambda c, b, i: (0, c))],
        out_specs=pl.BlockSpec((1, tt, CONV_TC), lambda c, b, i: (b, i, c)),
        out_shape=jax.ShapeDtypeStruct((bsz, seq, CONV_CH), F32),
        compiler_params=_params(("parallel", "parallel", "parallel")),
    )(x3, x3, cw, cb)
    return out.reshape(bsz * seq, CONV_CH)


def _conv_bwd(da, xbc, cw, cb, bsz, seq):
    tt = min(512, seq)
    nt, hb = seq // tt, tt // SUBLANES
    x3 = xbc.reshape(bsz, seq, CONV_CH)
    da3 = da.reshape(bsz, seq, CONV_CH)
    n2 = tt + SUBLANES

    def body(dam_ref, daa_ref, xm_ref, xb_ref, xa_ref, w_ref, b_ref, du_ref, gw_ref, gb_ref):
        bi, i = pl.program_id(1), pl.program_id(2)
        before = jnp.where(i > 0, xb_ref[0], 0.0)
        after = jnp.where(i < nt - 1, xa_ref[0], 0.0)
        xx = jnp.concatenate([before, xm_ref[0], after], axis=0)
        rolled = [xx] + [pltpu.roll(xx, s, 0) for s in range(1, CONV_K)]
        pre = b_ref[...]
        for s in range(CONV_K):
            pre = pre + rolled[s][SUBLANES:, :] * w_ref[3 - s:4 - s, :]
        da_ext = jnp.concatenate([dam_ref[0], jnp.where(i < nt - 1, daa_ref[0], 0.0)], axis=0)
        sg = _sigmoid(pre)
        dpre = da_ext * sg * (1.0 + pre * (1.0 - sg))
        du = dpre[:tt, :] * w_ref[3:4, :]
        for s in range(1, CONV_K):
            du = du + pltpu.roll(dpre, n2 - s, 0)[:tt, :] * w_ref[3 - s:4 - s, :]
        du_ref[0] = du.astype(BF16)
        dpm = dpre[:tt, :]
        gws = [jnp.sum(dpm * rolled[3 - kk][SUBLANES:SUBLANES + tt, :], axis=0, keepdims=True) for kk in range(CONV_K)]

        @pl.when((bi == 0) & (i == 0))
        def _():
            gw_ref[...] = jnp.zeros_like(gw_ref)
            gb_ref[...] = jnp.zeros_like(gb_ref)

        gw_ref[...] += jnp.concatenate(gws, axis=0)
        gb_ref[...] += jnp.sum(dpm, axis=0, keepdims=True)

    main = pl.BlockSpec((1, tt, CONV_TC), lambda c, b, i: (b, i, c))
    prev = pl.BlockSpec((1, SUBLANES, CONV_TC), lambda c, b, i: (b, jnp.maximum(i * hb - 1, 0), c))
    nxt = pl.BlockSpec((1, SUBLANES, CONV_TC), lambda c, b, i: (b, jnp.minimum((i + 1) * hb, seq // SUBLANES - 1), c))
    du, gw, gb = pl.pallas_call(
        body, name="conv_bwd", grid=(CONV_CH // CONV_TC, bsz, nt),
        in_specs=[main, nxt, main, prev, nxt,
                  pl.BlockSpec((CONV_K, CONV_TC), lambda c, b, i: (0, c)),
                  pl.BlockSpec((1, CONV_TC), lambda c, b, i: (0, c))],
        out_specs=[main, pl.BlockSpec((CONV_K, CONV_TC), lambda c, b, i: (0, c)),
                   pl.BlockSpec((1, CONV_TC), lambda c, b, i: (0, c))],
        out_shape=[jax.ShapeDtypeStruct((bsz, seq, CONV_CH), BF16), jax.ShapeDtypeStruct((CONV_K, CONV_CH), F32),
                   jax.ShapeDtypeStruct((1, CONV_CH), F32)],
        compiler_params=_params(("parallel", "arbitrary", "arbitrary")),
    )(da3, da3, x3, x3, x3, cw, cb)
    return du.reshape(bsz * seq, CONV_CH), gw, gb


def _ssd_prologue(dtf_ref, dtft_ref, bias_ref, biast_ref, arow_ref, acol_ref, dtfull_scr, acfull_scr, acr_scr):
    tri = (_iota((CHUNK, CHUNK), 1) <= _iota((CHUNK, CHUNK), 0)).astype(F32)
    triu = (_iota((CHUNK, CHUNK), 0) <= _iota((CHUNK, CHUNK), 1)).astype(F32)
    dtc = _softplus(dtf_ref[0] + bias_ref[...])
    a = dtc * arow_ref[...]
    acum = _sel_dot(tri, a)
    expand = _head_expand()
    dtfull_scr[...] = _dot_sel(dtc, expand)
    acfull_scr[...] = _dot_sel(acum, expand)
    dtr = _softplus(dtft_ref[0] + biast_ref[...])
    acr_scr[...] = _dot_sel(dtr * acol_ref[...], triu)
    return dtc, acum


def _decay_matrix(acum, acr_scr, h):
    lane = _iota((CHUNK, LANES), 1)
    ac_col = jnp.sum(jnp.where(lane == h, acum, 0.0), axis=1, keepdims=True)
    ac_row = acr_scr[h:h + 1, :]
    causal = _iota((CHUNK, CHUNK), 1) <= _iota((CHUNK, CHUNK), 0)
    return jnp.exp(jnp.where(causal, ac_col - ac_row, NEG))


def _ssd_fwd(xbc, dtf, dtft, bias, biast, arow, acol, dfull, bsz, seq):
    nc = seq // CHUNK
    x3 = xbc.reshape(bsz, seq, CONV_CH)

    def body(xbc_ref, dtf_ref, dtft_ref, bias_ref, biast_ref, arow_ref, acol_ref, dfull_ref,
             y_ref, hall_ref, h_scr, dtfull_scr, acfull_scr, acr_scr):
        @pl.when(pl.program_id(1) == 0)
        def _():
            h_scr[...] = jnp.zeros_like(h_scr)

        _, acum = _ssd_prologue(dtf_ref, dtft_ref, bias_ref, biast_ref, arow_ref, acol_ref,
                                dtfull_scr, acfull_scr, acr_scr)
        lane = _iota((CHUNK, LANES), 1)
        for g in range(2):
            bg = xbc_ref[0, :, D_MODEL + LANES * g:D_MODEL + LANES * (g + 1)]
            cg = xbc_ref[0, :, D_MODEL + 2 * LANES + LANES * g:D_MODEL + 2 * LANES + LANES * (g + 1)]
            cg_bf = cg.astype(BF16)
            gmat = _bdot_nt(cg_bf, bg)
            bgt_bf = bg.T.astype(BF16)
            for j in range(4):
                p = 4 * g + j
                sl = slice(LANES * p, LANES * (p + 1))
                xs = xbc_ref[0, :, sl]
                ac = acfull_scr[:, sl]
                acl = acfull_scr[CHUNK - 1:CHUNK, sl]
                xdt = xs * dtfull_scr[:, sl]
                xdt_bf = xdt.astype(BF16)
                hp = h_scr[p]
                hall_ref[0, 0, p] = hp
                yo = _bdot(cg_bf, hp) * jnp.exp(ac)
                yd = []
                for hh in range(2):
                    mmat = gmat * _decay_matrix(acum, acr_scr, 2 * p + hh)
                    yd.append(_bdot(mmat, xdt_bf))
                y_ref[0, :, sl] = jnp.where(lane < HEAD_DIM, yd[0], yd[1]) + yo + dfull_ref[:, sl] * xs
                h_scr[p] = hp * jnp.exp(acl) + _bdot(bgt_bf, xdt * jnp.exp(acl - ac))

    row = lambda w: pl.BlockSpec((1, w), lambda b, c: (0, 0))
    y, hall = pl.pallas_call(
        body, name="ssd_fwd", grid=(bsz, nc),
        in_specs=[pl.BlockSpec((1, CHUNK, CONV_CH), lambda b, c: (b, c, 0)),
                  pl.BlockSpec((1, CHUNK, LANES), lambda b, c: (b, c, 0)),
                  pl.BlockSpec((1, LANES, CHUNK), lambda b, c: (b, 0, c)),
                  row(LANES), pl.BlockSpec((LANES, 1), lambda b, c: (0, 0)),
                  row(LANES), pl.BlockSpec((LANES, 1), lambda b, c: (0, 0)), row(D_MODEL)],
        out_specs=[pl.BlockSpec((1, CHUNK, D_MODEL), lambda b, c: (b, c, 0)),
                   pl.BlockSpec((1, 1, HEAD_PAIRS, SSD_STATE, LANES), lambda b, c: (b, c, 0, 0, 0))],
        out_shape=[jax.ShapeDtypeStruct((bsz, seq, D_MODEL), F32),
                   jax.ShapeDtypeStruct((bsz, nc, HEAD_PAIRS, SSD_STATE, LANES), F32)],
        scratch_shapes=[pltpu.VMEM((HEAD_PAIRS, SSD_STATE, LANES), F32), pltpu.VMEM((CHUNK, D_MODEL), F32),
                        pltpu.VMEM((CHUNK, D_MODEL), F32), pltpu.VMEM((LANES, CHUNK), F32)],
        compiler_params=_params(("parallel", "arbitrary")),
    )(x3, dtf.reshape(bsz, seq, LANES), dtft, bias, biast, arow, acol, dfull)
    return y.reshape(bsz * seq, D_MODEL), hall


def _ssd_bwd(dy, xbc, dtf, dtft, bias, biast, arow, acol, dfull, hall, bsz, seq):
    nc = seq // CHUNK
    x3 = xbc.reshape(bsz, seq, CONV_CH)
    dy3 = dy.reshape(bsz, seq, D_MODEL)

    def body(dy_ref, xbc_ref, dtf_ref, dtft_ref, bias_ref, biast_ref, arow_ref, acol_ref, dfull_ref, hall_ref,
             dx_ref, ddt_ref, acc_ref, dh_scr, dtfull_scr, acfull_scr, acr_scr, csum_scr, dacf_scr, ddtf_scr, ddl_scr):
        first = (pl.program_id(0) == 0) & (pl.program_id(1) == 0)

        @pl.when(first)
        def _():
            acc_ref[...] = jnp.zeros_like(acc_ref)

        @pl.when(pl.program_id(1) == 0)
        def _():
            dh_scr[...] = jnp.zeros_like(dh_scr)

        dtc, acum = _ssd_prologue(dtf_ref, dtft_ref, bias_ref, biast_ref, arow_ref, acol_ref,
                                  dtfull_scr, acfull_scr, acr_scr)
        csum_scr[...] = jnp.zeros_like(csum_scr)
        lane = _iota((CHUNK, LANES), 1)
        last_row = _iota((CHUNK, LANES), 0) == CHUNK - 1
        rs16 = jnp.zeros((CHUNK, LANES), F32)
        for g in range(2):
            bsl = slice(D_MODEL + LANES * g, D_MODEL + LANES * (g + 1))
            csl = slice(D_MODEL + 2 * LANES + LANES * g, D_MODEL + 2 * LANES + LANES * (g + 1))
            bg_bf = xbc_ref[0, :, bsl].astype(BF16)
            cg = xbc_ref[0, :, csl]
            cg_bf = cg.astype(BF16)
            cgt_bf = cg.T.astype(BF16)
            gmat = _bdot_nt(cg_bf, bg_bf)
            dg = jnp.zeros((CHUNK, CHUNK), F32)
            dbg = jnp.zeros((CHUNK, SSD_STATE), F32)
            dcg = jnp.zeros((CHUNK, SSD_STATE), F32)
            for j in range(4):
                p = 4 * g + j
                sl = slice(LANES * p, LANES * (p + 1))
                xs = xbc_ref[0, :, sl]
                dt = dtfull_scr[:, sl]
                ac = acfull_scr[:, sl]
                acl = acfull_scr[CHUNK - 1:CHUNK, sl]
                dyp = dy_ref[0, :, sl]
                xdt = xs * dt
                xdt_bf = xdt.astype(BF16)
                e = jnp.exp(ac)
                dsd = jnp.exp(acl - ac)
                cd = jnp.exp(acl)
                xds_bf = (xdt * dsd).astype(BF16)
                hp = hall_ref[0, 0, p]
                hp_bf = hp.astype(BF16)
                dhn = dh_scr[p]
                dhn_bf = dhn.astype(BF16)
                yo = _bdot(cg_bf, hp_bf) * e
                dw_bf = (dyp * e).astype(BF16)
                dcg = dcg + _bdot_nt(dw_bf, hp_bf)
                dhin = _bdot(cgt_bf, dw_bf)
                dac = dyp * yo
                dacl = jnp.sum(dhn * hp, axis=0, keepdims=True) * cd
                dxds = _bdot(bg_bf, dhn_bf)
                dbg = dbg + _bdot_nt(xds_bf, dhn_bf)
                dxdt = dxds * dsd
                tdec = dxds * xdt * dsd
                dacl = dacl + jnp.sum(tdec, axis=0, keepdims=True)
                dac = dac - tdec
                dh_scr[p] = dhn * cd + dhin
                for hh in range(2):
                    h = 2 * p + hh
                    lm = (lane < HEAD_DIM) if hh == 0 else (lane >= HEAD_DIM)
                    dec = _decay_matrix(acum, acr_scr, h)
                    mmat = gmat * dec
                    dyh_bf = jnp.where(lm, dyp, 0.0).astype(BF16)
                    dm = _bdot_nt(dyh_bf, xdt_bf)
                    dxdt = dxdt + _bdot(mmat.T, dyh_bf)
                    dg = dg + dm * dec
                    q = dm * mmat
                    rs16 = rs16 + jnp.where(lane == h, jnp.sum(q, axis=1, keepdims=True), 0.0)
                    csum_scr[h:h + 1, :] = jnp.sum(q, axis=0, keepdims=True)
                dx_ref[0, :, sl] = dfull_ref[:, sl] * dyp + dxdt * dt
                dacf_scr[:, sl] = dac + jnp.where(last_row, dacl, 0.0)
                ddtf_scr[:, sl] = dxdt * xs
                ddl_scr[:, sl] = jnp.broadcast_to(jnp.sum(dyp * xs, axis=0, keepdims=True), (SUBLANES, LANES))
            dg_bf = dg.astype(BF16)
            dx_ref[0, :, bsl] = dbg + _bdot(dg.T, cg_bf)
            dx_ref[0, :, csl] = dcg + _bdot(dg_bf, bg_bf)
        reduce = _head_reduce()
        triu = (_iota((CHUNK, CHUNK), 0) <= _iota((CHUNK, CHUNK), 1)).astype(F32)
        dac16 = _dot_sel(dacf_scr[...], reduce) + rs16 - csum_scr[...].T
        da16 = _sel_dot(triu, dac16)
        ddt16 = da16 * arow_ref[...] + _dot_sel(ddtf_scr[...], reduce)
        ddtraw = ddt16 * _sigmoid(dtf_ref[0] + bias_ref[...])
        ddt_ref[0] = ddtraw
        acc_ref[0:8, :] += jnp.broadcast_to(jnp.sum(da16 * dtc * arow_ref[...], axis=0, keepdims=True), (SUBLANES, LANES))
        acc_ref[8:16, :] += _dot_sel(ddl_scr[...], reduce)
        acc_ref[16:24, :] += jnp.broadcast_to(jnp.sum(ddtraw, axis=0, keepdims=True), (SUBLANES, LANES))

    rev = lambda b, c: (b, nc - 1 - c, 0)
    row = lambda w: pl.BlockSpec((1, w), lambda b, c: (0, 0))
    dx, ddt, acc = pl.pallas_call(
        body, name="ssd_bwd", grid=(bsz, nc),
        in_specs=[pl.BlockSpec((1, CHUNK, D_MODEL), rev), pl.BlockSpec((1, CHUNK, CONV_CH), rev),
                  pl.BlockSpec((1, CHUNK, LANES), rev),
                  pl.BlockSpec((1, LANES, CHUNK), lambda b, c: (b, 0, nc - 1 - c)),
                  row(LANES), pl.BlockSpec((LANES, 1), lambda b, c: (0, 0)),
                  row(LANES), pl.BlockSpec((LANES, 1), lambda b, c: (0, 0)), row(D_MODEL),
                  pl.BlockSpec((1, 1, HEAD_PAIRS, SSD_STATE, LANES), lambda b, c: (b, nc - 1 - c, 0, 0, 0))],
        out_specs=[pl.BlockSpec((1, CHUNK, CONV_CH), rev), pl.BlockSpec((1, CHUNK, LANES), rev),
                   pl.BlockSpec((3 * SUBLANES, LANES), lambda b, c: (0, 0))],
        out_shape=[jax.ShapeDtypeStruct((bsz, seq, CONV_CH), F32), jax.ShapeDtypeStruct((bsz, seq, LANES), F32),
                   jax.ShapeDtypeStruct((3 * SUBLANES, LANES), F32)],
        scratch_shapes=[pltpu.VMEM((HEAD_PAIRS, SSD_STATE, LANES), F32), pltpu.VMEM((CHUNK, D_MODEL), F32),
                        pltpu.VMEM((CHUNK, D_MODEL), F32), pltpu.VMEM((LANES, CHUNK), F32),
                        pltpu.VMEM((LANES, CHUNK), F32), pltpu.VMEM((CHUNK, D_MODEL), F32),
                        pltpu.VMEM((CHUNK, D_MODEL), F32), pltpu.VMEM((SUBLANES, D_MODEL), F32)],
        compiler_params=_params(("arbitrary", "arbitrary")),
    )(dy3, x3, dtf.reshape(bsz, seq, LANES), dtft, bias, biast, arow, acol, dfull, hall)
    return dx.reshape(bsz * seq, CONV_CH), ddt.reshape(bsz * seq, LANES), acc


GROUP_W = D_MODEL // 2


def _gnorm_fwd(y, z, o_att, w):
    n = y.shape[0]
    tm = min(1024, n)

    def body(y_ref, z_ref, o_ref, w_ref, out_ref):
        zv = z_ref[...]
        g = y_ref[...] * (zv * _sigmoid(zv))
        for gi in range(2):
            sl = slice(GROUP_W * gi, GROUP_W * (gi + 1))
            gs = g[:, sl]
            r = lax.rsqrt(jnp.mean(gs * gs, axis=-1, keepdims=True) + NORM_EPS)
            out_ref[:, sl] = (gs * r * w_ref[:, sl]).astype(BF16)
        out_ref[:, D_MODEL:] = o_ref[...].astype(BF16)

    tok = pl.BlockSpec((tm, D_MODEL), lambda i: (i, 0))
    return pl.pallas_call(
        body, name="gnorm_fwd", grid=(n // tm,),
        in_specs=[tok, tok, tok, pl.BlockSpec((1, D_MODEL), lambda i: (0, 0))],
        out_specs=pl.BlockSpec((tm, MIX_WIDTH), lambda i: (i, 0)),
        out_shape=jax.ShapeDtypeStruct((n, MIX_WIDTH), BF16),
        compiler_params=_params(("parallel",)),
    )(y, z, o_att, w)


def _gnorm_bwd(dycat, y, z, w):
    n = y.shape[0]
    tm = min(512, n)

    def body(dn_ref, y_ref, z_ref, w_ref, dy_ref, dz_ref, gw_ref):
        zv, yv = z_ref[...], y_ref[...]
        sz = _sigmoid(zv)
        sil = zv * sz
        g = yv * sil

        @pl.when(pl.program_id(0) == 0)
        def _():
            gw_ref[...] = jnp.zeros_like(gw_ref)

        for gi in range(2):
            sl = slice(GROUP_W * gi, GROUP_W * (gi + 1))
            gs = g[:, sl]
            r = lax.rsqrt(jnp.mean(gs * gs, axis=-1, keepdims=True) + NORM_EPS)
            nrm = gs * r
            dyn = dn_ref[:, sl]
            dn = dyn * w_ref[:, sl]
            dgs = r * (dn - nrm * jnp.mean(dn * nrm, axis=-1, keepdims=True))
            dy_ref[:, sl] = dgs * sil[:, sl]
            dz_ref[:, sl] = (dgs * yv[:, sl] * (sz[:, sl] * (1.0 + zv[:, sl] * (1.0 - sz[:, sl])))).astype(BF16)
            gw_ref[:, sl] += jnp.sum(dyn * nrm, axis=0, keepdims=True)

    tok = pl.BlockSpec((tm, D_MODEL), lambda i: (i, 0))
    row = pl.BlockSpec((1, D_MODEL), lambda i: (0, 0))
    return pl.pallas_call(
        body, name="gnorm_bwd", grid=(n // tm,),
        in_specs=[tok, tok, tok, row], out_specs=[tok, tok, row],
        out_shape=[jax.ShapeDtypeStruct((n, D_MODEL), F32), jax.ShapeDtypeStruct((n, D_MODEL), BF16),
                   jax.ShapeDtypeStruct((1, D_MODEL), F32)],
        compiler_params=_params(("arbitrary",)),
    )(dycat, y, z, w)


AUX_C = 3
AUX_ONE = 3


def _aux_base(hh):
    return HEAD_DIM * (1 - hh)


def _fox_prep(dtf, fb, bsz, seq):
    def body(x_ref, b_ref, aux_ref):
        tri = (_iota((CHUNK, CHUNK), 1) <= _iota((CHUNK, CHUNK), 0)).astype(F32)
        row, col = _iota((LANES, D_MODEL), 0), _iota((LANES, D_MODEL), 1)
        lane = jnp.bitwise_and(col, LANES - 1)
        other = (lane < HEAD_DIM).astype(jnp.int32)
        term = lane - HEAD_DIM * (1 - other)
        src = F_COL + 2 * jnp.right_shift(col, 7) + other
        place = [jnp.where((row == src) & (term == i), -1.0, 0.0).astype(BF16) for i in range(AUX_C)]
        lane1 = jnp.bitwise_and(_iota((1, D_MODEL), 1), LANES - 1)
        ones_lane = (lane1 - HEAD_DIM * (1 - (lane1 < HEAD_DIM).astype(jnp.int32)) == AUX_ONE).astype(F32)
        carry = jnp.zeros((1, LANES), F32)
        for j in range(seq // CHUNK):
            sl = slice(CHUNK * j, CHUNK * (j + 1))
            lf = _log_sigmoid(x_ref[sl, :] + b_ref[...])
            c = _sel_dot(tri, lf) + carry
            carry = carry + jnp.sum(lf, axis=0, keepdims=True)
            t1, t2, t3 = (jnp.dot(t, p, preferred_element_type=F32) for t, p in zip(_split3(c), place))
            aux_ref[sl, :] = (t1 + t2 + t3 + ones_lane).astype(BF16)

    return pl.pallas_call(
        body, name="fox_prep", grid=(bsz,),
        in_specs=[pl.BlockSpec((seq, LANES), lambda b: (b, 0)), pl.BlockSpec((1, LANES), lambda b: (0, 0))],
        out_specs=pl.BlockSpec((seq, D_MODEL), lambda b: (b, 0)),
        out_shape=jax.ShapeDtypeStruct((bsz * seq, D_MODEL), BF16),
        compiler_params=_params(("parallel",)),
    )(dtf, fb)


def _fox_prep_bwd(dck, dcq, ddt, dtf, fb, bsz, seq):
    nj = seq // CHUNK

    def body(dck_ref, dcq_ref, ddt_ref, x_ref, b_ref, out_ref, gb_ref):
        triu = (_iota((CHUNK, CHUNK), 0) <= _iota((CHUNK, CHUNK), 1)).astype(F32)
        is_f = (_iota((CHUNK, LANES), 1) >= F_COL) & (_iota((CHUNK, LANES), 1) < 2 * F_COL)
        carry = jnp.zeros((1, LANES), F32)
        total = jnp.zeros((1, LANES), F32)
        for j in range(nj - 1, -1, -1):
            sl = slice(CHUNK * j, CHUNK * (j + 1))
            dcv = dck_ref[sl, :] + dcq_ref[sl, :]
            dlf = _sel_dot(triu, dcv) + carry
            carry = carry + jnp.sum(dcv, axis=0, keepdims=True)
            df = jnp.where(is_f, dlf * _sigmoid(-(x_ref[sl, :] + b_ref[...])), 0.0)
            out_ref[sl, :] = (df + ddt_ref[sl, :]).astype(BF16)
            total = total + jnp.sum(df, axis=0, keepdims=True)

        @pl.when(pl.program_id(0) == 0)
        def _():
            gb_ref[...] = jnp.zeros_like(gb_ref)

        gb_ref[...] += jnp.broadcast_to(total, (SUBLANES, LANES))

    blk = pl.BlockSpec((seq, LANES), lambda b: (b, 0))
    return pl.pallas_call(
        body, name="fox_prep_bwd", grid=(bsz,),
        in_specs=[blk, blk, blk, blk, pl.BlockSpec((1, LANES), lambda b: (0, 0))],
        out_specs=[blk, pl.BlockSpec((SUBLANES, LANES), lambda b: (0, 0))],
        out_shape=[jax.ShapeDtypeStruct((bsz * seq, LANES), BF16), jax.ShapeDtypeStruct((SUBLANES, LANES), F32)],
        compiler_params=_params(("arbitrary",)),
    )(dck, dcq, ddt, dtf, fb)


ATT_BLOCK_FWD = 512
ATT_BLOCK_KEYS = 256
ATT_BLOCK_BWD = 256
ATT_BLOCK_BWD_Q = 256


def _att_operands(q_ref, k_ref, aux_ref, hh):
    lane = _iota((1, LANES), 1)
    lm = (lane < HEAD_DIM) if hh == 0 else (lane >= HEAD_DIM)
    base = _aux_base(hh)
    zero = jnp.zeros((1, LANES), BF16)
    ka = jnp.where(lm, k_ref[...], jnp.where((lane >= base) & (lane <= base + AUX_ONE), aux_ref[...], zero))
    qa = jnp.where(lm, q_ref[...] * ATT_SCALE,
                   jnp.where((lane >= base) & (lane < base + AUX_C), jnp.ones((1, LANES), BF16), zero))
    return lm, base, qa, ka


def _att_fwd(qkv, ccol, bsz, seq, gather):
    bq, bk = min(ATT_BLOCK_FWD, seq), min(ATT_BLOCK_KEYS, seq)
    nq, ratio = seq // bq, bq // bk
    nx = len(gather)

    def body(*refs):
        q_ref, k_ref, v_ref, c_ref = refs[:4]
        x_refs = refs[4:4 + nx]
        o_ref, lse_ref = refs[4 + nx:6 + nx]
        xo_refs = refs[6 + nx:6 + 2 * nx]
        qa_scr, ka_scr, vt_scr = refs[6 + 2 * nx:9 + 2 * nx]
        sems = refs[9 + 2 * nx:]
        pair = pl.program_id(1)

        @pl.when((pl.program_id(0) == 0) & (pair == 0))
        def _():
            _exchange_start("gather", x_refs, xo_refs, *sems)

        lse_ref[...] = jnp.zeros_like(lse_ref)
        for hh in range(2):
            _, _, qa, ka = _att_operands(q_ref, k_ref, c_ref, hh)
            qa_scr[hh] = qa
            ka_scr[hh] = ka
        for t0 in range(0, seq, bq):
            vt_scr[:, t0:t0 + bq] = v_ref[t0:t0 + bq, :].astype(F32).T.astype(BF16)
        key_minus_query = _iota((bk, bq), 0) - _iota((bk, bq), 1)

        def q_step(i, carry0):
            qrows = pl.ds(pl.multiple_of(i * bq, bq), bq)

            def scores(j):
                krows = pl.ds(pl.multiple_of(j * bk, bk), bk)
                return tuple(_bdot_nt(ka_scr[hh, krows, :], qa_scr[hh, qrows, :]) for hh in range(2))

            def update(state, st_pair, j, diag):
                krows = pl.ds(pl.multiple_of(j * bk, bk), bk)
                out = []
                for hh in range(2):
                    m, l, acc = state[hh]
                    st = st_pair[hh]
                    if diag is not None:
                        st = jnp.where(key_minus_query + diag * bk <= 0, st, NEG)
                    mn = jnp.maximum(m, jnp.max(st, axis=0, keepdims=True))
                    alpha = jnp.exp(m - mn)
                    pt = jnp.exp(st - mn)
                    l = alpha * l + jnp.sum(pt, axis=0, keepdims=True)
                    vt = vt_scr[HEAD_DIM * hh:HEAD_DIM * (hh + 1), krows]
                    out.append((mn, l, alpha * acc + _bdot(vt, pt)))
                return tuple(out)

            def step(j, carry):
                state, s_cur = carry
                s_next = scores(j + 1)
                return update(state, s_cur, j, None), s_next

            def step_pair(t, carry):
                state, s_cur = carry
                s_second = scores(2 * t + 1)
                s_next = scores(2 * t + 2)
                state = update(state, s_cur, 2 * t, None)
                return update(state, s_second, 2 * t + 1, None), s_next

            one = (jnp.full((1, bq), NEG, F32), jnp.zeros((1, bq), F32), jnp.zeros((HEAD_DIM, bq), F32))
            first_diag = i * ratio
            if ratio % 2 == 0:
                state, s_cur = lax.fori_loop(0, first_diag // 2, step_pair, ((one, one), scores(0)))
            else:
                state, s_cur = lax.fori_loop(0, first_diag, step, ((one, one), scores(0)))
            for r in range(ratio):
                s_next = scores(first_diag + r + 1) if r + 1 < ratio else None
                state = update(state, s_cur, first_diag + r, r)
                s_cur = s_next
            (m0, l0, acc0), (m1, l1, acc1) = state
            ot = jnp.concatenate([acc0 * (1.0 / l0), acc1 * (1.0 / l1)], axis=0)
            o_ref[qrows, :] = ot.T
            lse_ref[0, 0, 0:1, qrows] = m0 + jnp.log(l0)
            lse_ref[0, 0, 1:2, qrows] = m1 + jnp.log(l1)
            return carry0

        lax.fori_loop(0, nq, q_step, 0)

        @pl.when((pl.program_id(0) == bsz - 1) & (pair == HEAD_PAIRS - 1))
        def _():
            _exchange_wait("gather", x_refs, xo_refs, *sems)

    col = lambda off: pl.BlockSpec((seq, LANES), lambda b, p: (b, off + p))
    head2 = pltpu.VMEM((2, seq, LANES), BF16)
    hbm = pl.BlockSpec(memory_space=pl.ANY)
    return pl.pallas_call(
        body, name="att_fwd", grid=(bsz, HEAD_PAIRS),
        in_specs=[col(0), col(HEAD_PAIRS), col(2 * HEAD_PAIRS), col(0)] + [hbm] * nx,
        out_specs=[pl.BlockSpec((seq, LANES), lambda b, p: (b, p)),
                   pl.BlockSpec((1, 1, SUBLANES, seq), lambda b, p: (b, p, 0, 0))] + [hbm] * nx,
        out_shape=[jax.ShapeDtypeStruct((bsz * seq, D_MODEL), F32),
                   jax.ShapeDtypeStruct((bsz, HEAD_PAIRS, SUBLANES, seq), F32)] + _exchange_shapes("gather", gather),
        scratch_shapes=[head2, head2, pltpu.VMEM((LANES, seq), BF16)] + _exchange_scratch(nx),
        compiler_params=_params(("arbitrary", "arbitrary")),
    )(qkv, qkv, qkv, ccol, *gather)


def _att_bwd(qkv, dycat, o, lse, ccol, bsz, seq, scatter):
    blk = min(ATT_BLOCK_BWD, seq)
    bq = min(ATT_BLOCK_BWD_Q, seq)
    nb, nq, ratio = seq // blk, seq // bq, bq // blk
    nx = len(scatter)

    def body(*refs):
        q_ref, k_ref, v_ref, do_ref, o_ref, lse_ref, c_ref = refs[:7]
        x_refs = refs[7:7 + nx]
        dq_ref, dk_ref, dv_ref, dck_ref, dcq_ref = refs[7 + nx:12 + nx]
        xo_refs = refs[12 + nx:12 + 2 * nx]
        (qa_scr, ka_scr, va_scr, doa_scr, kat_scr, qat_scr, dot_scr, d_scr, dqt_scr, dkt_scr,
         dvt_scr) = refs[12 + 2 * nx:23 + 2 * nx]
        sems = refs[23 + 2 * nx:]
        pair = pl.program_id(1)

        @pl.when((pl.program_id(0) == 0) & (pair == 0))
        def _():
            _exchange_start("scatter", x_refs, xo_refs, *sems)

        query_minus_key = _iota((blk, bq), 1) - _iota((blk, bq), 0)
        lane_b = _iota((blk, LANES), 1)
        row_t = _iota((LANES, seq), 0)
        masks, bases = [], []
        ones8 = jnp.ones((SUBLANES, LANES), F32)
        for hh in range(2):
            lm, base, qa, ka = _att_operands(q_ref, k_ref, c_ref, hh)
            masks.append(lm)
            bases.append(base)
            qa_scr[hh] = qa
            ka_scr[hh] = ka
            va_scr[hh] = jnp.where(lm, v_ref[...], jnp.zeros((seq, LANES), BF16))
            doa = jnp.where(lm, do_ref[...], 0.0).astype(BF16)
            doa_scr[hh] = doa
            for t0 in range(0, seq, blk):
                kat_scr[hh, :, t0:t0 + blk] = ka[t0:t0 + blk, :].astype(F32).T.astype(BF16)
                qat_scr[hh, :, t0:t0 + blk] = qa[t0:t0 + blk, :].astype(F32).T.astype(BF16)
            d1, d2, d3 = (_bdot_nt(ones8, term) for term in _split3(doa.astype(F32) * o_ref[...]))
            d_scr[hh] = d1 + d2 + d3
        for t0 in range(0, seq, blk):
            dot_scr[:, t0:t0 + blk] = do_ref[t0:t0 + blk, :].astype(BF16).astype(F32).T.astype(BF16)
        dqt_scr[...] = jnp.zeros_like(dqt_scr)
        dck_ref[...] = jnp.zeros_like(dck_ref)

        def kv_step(j, carry0):
            krows = pl.ds(pl.multiple_of(j * blk, blk), blk)
            dkt_scr[...] = jnp.zeros_like(dkt_scr)
            dvt_scr[...] = jnp.zeros_like(dvt_scr)

            def scores(i):
                rows = pl.ds(pl.multiple_of(i * bq, bq), bq)
                return tuple((_bdot_nt(ka_scr[hh, krows, :], qa_scr[hh, rows, :]),
                              _bdot_nt(va_scr[hh, krows, :], doa_scr[hh, rows, :])) for hh in range(2))

            def update(i, sd, masked):
                rows = pl.ds(pl.multiple_of(i * bq, bq), bq)
                for hh in range(2):
                    st, dpt = sd[hh]
                    if masked:
                        st = jnp.where(query_minus_key >= (j % ratio) * blk, st, NEG)
                    pt = jnp.exp(st - lse_ref[0, 0, hh:hh + 1, rows])
                    dst = (pt * (dpt - d_scr[hh, 0:1, rows])).astype(BF16)
                    dvt_scr[hh] += _bdot_nt(dot_scr[HEAD_DIM * hh:HEAD_DIM * (hh + 1), rows], pt)
                    dkt_scr[hh] += _bdot_nt(qat_scr[hh, :, rows], dst)
                    dqt_scr[hh, :, rows] += _bdot(kat_scr[hh, :, krows], dst)

            i_diag = j // ratio

            def q_pair(t, sd_cur):
                i = i_diag + 1 + 2 * t
                sd_second = scores(i + 1)
                sd_next = scores(jnp.minimum(i + 2, nq - 1))
                update(i, sd_cur, False)
                update(i + 1, sd_second, False)
                return sd_next

            sd_diag = scores(i_diag)
            sd_first = scores(jnp.minimum(i_diag + 1, nq - 1))
            update(i_diag, sd_diag, True)
            rest = nq - 1 - i_diag
            sd_last = lax.fori_loop(0, rest // 2, q_pair, sd_first)

            @pl.when(rest % 2 == 1)
            def _():
                update(jnp.int32(nq - 1), sd_last, False)
            dkt = jnp.where(_iota((LANES, blk), 0) < HEAD_DIM, dkt_scr[0], dkt_scr[1])
            dk_ref[krows, :] = dkt.T.astype(BF16)
            dv_ref[krows, :] = jnp.concatenate([dvt_scr[0], dvt_scr[1]], axis=0).T.astype(BF16)
            for hh in range(2):
                dck_ref[0, 0, hh:hh + 1, krows] = -dkt_scr[hh, bases[hh]:bases[hh] + 1, :]
            return carry0

        lax.fori_loop(0, nb, kv_step, 0)
        for t0 in range(0, seq, blk):
            dq0, dq1 = dqt_scr[0, :, t0:t0 + blk].T, dqt_scr[1, :, t0:t0 + blk].T
            dq_ref[t0:t0 + blk, :] = (jnp.where(lane_b < HEAD_DIM, dq0, dq1) * ATT_SCALE).astype(BF16)
        dcq_ref[...] = jnp.zeros_like(dcq_ref)
        for hh in range(2):
            dcq_ref[0, 0, hh:hh + 1, :] = jnp.sum(jnp.where(row_t == bases[hh] + AUX_ONE, dqt_scr[hh], 0.0),
                                                   axis=0, keepdims=True)

        @pl.when((pl.program_id(0) == bsz - 1) & (pair == HEAD_PAIRS - 1))
        def _():
            _exchange_wait("scatter", x_refs, xo_refs, *sems)

    col = lambda off: pl.BlockSpec((seq, LANES), lambda b, p: (b, off + p))
    rowvec = pl.BlockSpec((1, 1, SUBLANES, seq), lambda b, p: (b, p, 0, 0))
    out = jax.ShapeDtypeStruct((bsz * seq, D_MODEL), BF16)
    rows_shape = jax.ShapeDtypeStruct((bsz, HEAD_PAIRS, SUBLANES, seq), F32)
    head2 = pltpu.VMEM((2, seq, LANES), BF16)
    hbm = pl.BlockSpec(memory_space=pl.ANY)
    return pl.pallas_call(
        body, name="att_bwd", grid=(bsz, HEAD_PAIRS),
        in_specs=[col(0), col(HEAD_PAIRS), col(2 * HEAD_PAIRS), col(HEAD_PAIRS), col(0), rowvec, col(0)] + [hbm] * nx,
        out_specs=[col(0), col(0), col(0), rowvec, rowvec] + [hbm] * nx,
        out_shape=[out, out, out, rows_shape, rows_shape] + _exchange_shapes("scatter", scatter),
        scratch_shapes=[head2, head2, head2, head2, pltpu.VMEM((2, LANES, seq), BF16),
                        pltpu.VMEM((2, LANES, seq), BF16), pltpu.VMEM((LANES, seq), BF16),
                        pltpu.VMEM((2, SUBLANES, seq), F32), pltpu.VMEM((2, LANES, seq), F32),
                        pltpu.VMEM((2, LANES, blk), F32), pltpu.VMEM((2, HEAD_DIM, blk), F32)] + _exchange_scratch(nx),
        compiler_params=_params(("arbitrary", "arbitrary")),
    )(qkv, qkv, qkv, dycat, o, lse, ccol, *scatter)


def _adamw_math(w, g, m, v):
    m = ADAM_B1 * m + (1.0 - ADAM_B1) * g
    v = ADAM_B2 * v + (1.0 - ADAM_B2) * jnp.square(g)
    m_hat = m / ADAM_C1
    v_hat = v / ADAM_C2
    delta = -ADAM_LR * (m_hat / (jnp.sqrt(v_hat) + ADAM_EPS) + ADAM_WD * w)
    return delta, m, v


def _row_tile(rows):
    for tr in (256, 128, 64, 32, 16, 8):
        if rows % tr == 0:
            return tr
    return rows


def _sum_parts(parts, name):
    nparts, rows, cols = parts.shape
    tr = rows if rows % SUBLANES else _row_tile(rows)

    def body(p_ref, o_ref):
        g = p_ref[0].astype(F32)
        for s in range(1, nparts):
            g = g + p_ref[s].astype(F32)
        o_ref[...] = g

    return pl.pallas_call(
        body, name=name, grid=(rows // tr,),
        in_specs=[pl.BlockSpec((nparts, tr, cols), lambda i: (0, i, 0))],
        out_specs=pl.BlockSpec((tr, cols), lambda i: (i, 0)),
        out_shape=jax.ShapeDtypeStruct((rows, cols), F32),
        compiler_params=_params(("parallel",)),
    )(parts)


def _adamw_parts(parts, w, m, v, name):
    nparts, rows, cols = parts.shape
    tr = _row_tile(rows)

    def body(p_ref, w_ref, m_ref, v_ref, g_ref, d_ref, mo_ref, vo_ref):
        g = p_ref[0].astype(F32)
        for s in range(1, nparts):
            g = g + p_ref[s].astype(F32)
        delta, mn, vn = _adamw_math(w_ref[...], g, m_ref[...], v_ref[...])
        g_ref[...] = g
        d_ref[...] = delta
        mo_ref[...] = mn
        vo_ref[...] = vn

    blk = pl.BlockSpec((tr, cols), lambda i: (i, 0))
    shp = jax.ShapeDtypeStruct((rows, cols), F32)
    return pl.pallas_call(
        body, name=name, grid=(rows // tr,),
        in_specs=[pl.BlockSpec((nparts, tr, cols), lambda i: (0, i, 0)), blk, blk, blk],
        out_specs=[blk, blk, blk, blk], out_shape=[shp, shp, shp, shp],
        compiler_params=_params(("parallel",)),
    )(parts, w, m, v)


def _me():
    return lax.axis_index("x"), lax.axis_index("y"), lax.axis_index("c")


def _flip(pos, k):
    x, y, c = pos
    kx, ky, kc = (k >> 2) & 1, (k >> 1) & 1, k & 1
    return (x ^ kx if kx else x, y ^ ky if ky else y, c ^ kc if kc else c)


def _logical(pos):
    return 4 * pos[0] + 2 * pos[1] + pos[2]


def _all_gather_two_level(shards):
    narr = len(shards)

    def body(*refs):
        x_refs, out_refs = refs[:narr], refs[narr:2 * narr]
        send_sems, recv_sems, local_sems = refs[2 * narr:]
        x, y, c = _me()
        me, sibling = (x, y, c), (x, y, 1 - c)
        chips = [(1 - x, y), (x, 1 - y), (1 - x, 1 - y)]

        def copy(a, k, block, to, src=None):
            slot = out_refs[a].at[_logical(block)]
            return pltpu.make_async_remote_copy(
                src_ref=slot if src is None else src, dst_ref=slot,
                send_sem=send_sems.at[a, k], recv_sem=recv_sems.at[a, k], device_id=to, device_id_type=MESH)

        mine = [pltpu.make_async_copy(x_refs[a], out_refs[a].at[_logical(me)], local_sems.at[a]) for a in range(narr)]
        for cp in mine:
            cp.start()
        first = []
        for a in range(narr):
            first.append(copy(a, 0, me, sibling, src=x_refs[a]))
            first += [copy(a, 1 + j, me, (*chip, c), src=x_refs[a]) for j, chip in enumerate(chips)]
        for cp in first:
            cp.start()
        passed = []
        for a in range(narr):
            for j, chip in enumerate(chips):
                copy(a, 1 + j, (*chip, c), me).wait_recv()
                fwd = copy(a, 4 + j, (*chip, c), sibling)
                fwd.start()
                passed.append(fwd)
        for a in range(narr):
            copy(a, 0, sibling, me).wait_recv()
            for j, chip in enumerate(chips):
                copy(a, 4 + j, (*chip, 1 - c), me).wait_recv()
        for cp in first + passed:
            cp.wait_send()
        for cp in mine:
            cp.wait()

    hbm = pl.BlockSpec(memory_space=pl.ANY)
    return pl.pallas_call(
        body, name="all_gather_big",
        out_shape=[jax.ShapeDtypeStruct((N_DEV,) + s.shape, s.dtype) for s in shards],
        in_specs=[hbm] * narr, out_specs=[hbm] * narr,
        scratch_shapes=[pltpu.SemaphoreType.DMA((narr, 7)), pltpu.SemaphoreType.DMA((narr, 7)),
                        pltpu.SemaphoreType.DMA((narr,))],
    )(*shards)


def _exchange_copies(kind, x_refs, out_refs, send_sems, recv_sems, local_sems):
    me = _me()
    mine = _logical(me)
    local, remote = [], []
    for a, (x_ref, out_ref) in enumerate(zip(x_refs, out_refs)):
        own = x_ref if kind == "gather" else x_ref.at[mine]
        local.append(pltpu.make_async_copy(own, out_ref.at[mine], local_sems.at[a]))
        for k in range(1, N_DEV):
            peer = _flip(me, k)
            src = x_ref if kind == "gather" else x_ref.at[_logical(peer)]
            remote.append(pltpu.make_async_remote_copy(
                src_ref=src, dst_ref=out_ref.at[mine], send_sem=send_sems.at[a, k - 1], recv_sem=recv_sems.at[a, k - 1],
                device_id=peer, device_id_type=MESH))
    return local, remote


def _exchange_start(kind, x_refs, out_refs, *sems):
    if not x_refs:
        return
    local, remote = _exchange_copies(kind, x_refs, out_refs, *sems)
    for cp in local + remote:
        cp.start()


def _exchange_wait(kind, x_refs, out_refs, *sems):
    if not x_refs:
        return
    local, remote = _exchange_copies(kind, x_refs, out_refs, *sems)
    for cp in remote:
        cp.wait_recv()
    for cp in remote:
        cp.wait_send()
    for cp in local:
        cp.wait()


def _exchange_shapes(kind, arrays):
    return [jax.ShapeDtypeStruct(((N_DEV,) if kind == "gather" else ()) + a.shape, a.dtype) for a in arrays]


def _exchange_scratch(narrays):
    if not narrays:
        return []
    return [pltpu.SemaphoreType.DMA((narrays, N_DEV - 1)), pltpu.SemaphoreType.DMA((narrays, N_DEV - 1)),
            pltpu.SemaphoreType.DMA((narrays,))]


def _exchange_rows(x_ref, buf_ref, send_sems, recv_sems):
    me = _me()
    buf_ref[_logical(me)] = x_ref[...]
    copies = []
    for k in range(1, N_DEV):
        peer = _flip(me, k)
        copies.append(pltpu.make_async_remote_copy(
            src_ref=x_ref, dst_ref=buf_ref.at[_logical(me)],
            send_sem=send_sems.at[k - 1], recv_sem=recv_sems.at[k - 1], device_id=peer, device_id_type=MESH))
    for cp in copies:
        cp.start()
    for cp in copies:
        cp.wait_recv()
    for cp in copies:
        cp.wait_send()


def _sum_slots(buf_ref):
    total = buf_ref[0]
    for s in range(1, N_DEV):
        total = total + buf_ref[s]
    return total


def _small_gather(x):
    def body(x_ref, o_ref, buf_ref, send_sems, recv_sems):
        _exchange_rows(x_ref, buf_ref, send_sems, recv_sems)
        o_ref[...] = _sum_slots(buf_ref)

    return pl.pallas_call(
        body, name="small_gather", out_shape=jax.ShapeDtypeStruct(x.shape, F32),
        in_specs=[pl.BlockSpec(memory_space=pltpu.VMEM)], out_specs=pl.BlockSpec(memory_space=pltpu.VMEM),
        scratch_shapes=[pltpu.VMEM((N_DEV,) + x.shape, F32), pltpu.SemaphoreType.DMA((7,)), pltpu.SemaphoreType.DMA((7,))],
    )(x)


def _small_reduce_adamw(g, w, m, v):
    def body(g_ref, w_ref, m_ref, v_ref, go_ref, d_ref, mo_ref, vo_ref, buf_ref, send_sems, recv_sems):
        _exchange_rows(g_ref, buf_ref, send_sems, recv_sems)
        gs = _sum_slots(buf_ref)
        delta, mn, vn = _adamw_math(w_ref[...], gs, m_ref[...], v_ref[...])
        go_ref[...] = gs
        d_ref[...] = delta
        mo_ref[...] = mn
        vo_ref[...] = vn

    vm = pl.BlockSpec(memory_space=pltpu.VMEM)
    shp = jax.ShapeDtypeStruct(g.shape, F32)
    return pl.pallas_call(
        body, name="small_reduce_adamw", out_shape=[shp, shp, shp, shp],
        in_specs=[vm, vm, vm, vm], out_specs=[vm, vm, vm, vm],
        scratch_shapes=[pltpu.VMEM((N_DEV,) + g.shape, F32), pltpu.SemaphoreType.DMA((7,)), pltpu.SemaphoreType.DMA((7,))],
    )(g, w, m, v)


def _pad_cols(a, width):
    return jnp.pad(a, ((0, 0), (0, width - a.shape[1])))


def _local_step(x, target, norm_mix_w, w_in_t, conv_w, conv_b, dt_bias, a_log, d_skip, ssd_norm_w, f_bias,
                late_shards, norm_mlp_w, norm_final_w):
    bsz, seq, _ = x.shape
    n = bsz * seq
    x2 = x.reshape(n, D_MODEL)
    t2 = target.reshape(n, D_MODEL)
    nfw = norm_final_w.reshape(1, D_MODEL)

    bias = _pad_cols(dt_bias, LANES)
    arow = _pad_cols(-jnp.exp(a_log), LANES)
    biast, acol = bias.reshape(LANES, 1), arow.reshape(LANES, 1)
    dfull = jnp.repeat(d_skip, HEAD_DIM, axis=1)
    fb = jnp.pad(f_bias, ((0, 0), (F_COL, LANES - 2 * F_COL)))

    wt_z, wt_xbc, wt_qkv = w_in_t[:ROW_XBC], w_in_t[ROW_XBC:ROW_DT], w_in_t[ROW_Q:ROW_F]
    wt_dtf = jnp.concatenate([w_in_t[ROW_DT:ROW_Q], w_in_t[ROW_F:], jnp.zeros((LANES - 2 * F_COL, D_MODEL), BF16)], axis=0)
    wt_q, wt_k, wt_v = wt_qkv[:D_MODEL], wt_qkv[D_MODEL:2 * D_MODEL], wt_qkv[2 * D_MODEL:]

    h1 = _rms_fwd(x2, norm_mix_w, "rms_fwd_mix")
    z = _mm([(h1, wt_z)], "inproj_z", F32, 1024, 1024, nt=True)
    xbc_raw = _mm([(h1, wt_xbc)], "inproj_xbc", F32, 512, 1536, nt=True)
    qkv = _mm([(h1, wt_qkv)], "inproj_qkv", BF16, 512, 3072, nt=True)
    dtf = _mm([(h1, wt_dtf)], "inproj_dtf", F32, 2048, LANES, nt=True)
    dtft = dtf.reshape(bsz, seq, LANES).transpose(0, 2, 1)

    xbc = _conv_fwd(xbc_raw, conv_w, conv_b, bsz, seq)
    y_pre, hall = _ssd_fwd(xbc, dtf, dtft, bias, biast, arow, acol, dfull, bsz, seq)

    ccol = _fox_prep(dtf, fb, bsz, seq)
    o_att, lse, w_out, w_up_t, w_down = _att_fwd(qkv, ccol, bsz, seq, late_shards)
    w_out, w_up_t, w_down = (w.reshape(-1, D_MODEL) for w in (w_out, w_up_t, w_down))

    ycat = _gnorm_fwd(y_pre, z, o_att, ssd_norm_w)
    h2, h2n = _mm([(ycat, w_out)], "outproj", F32, 512, 1024, res=x2, rms_fwd=norm_mlp_w)
    pre = _mm([(h2n, w_up_t)], "mlp_up", BF16, 512, 4096, nt=True)

    dh3, loss_row, g_nfw = _mlp_down_loss(pre, w_down, h2, t2, nfw)
    dpre, u = _mlp_down_bwd(dh3, w_down, pre)
    g_w_down = _mm_tn(u, dh3, "grad_w_down", 1024, 1024, 2048)
    g_w_up_t = _mm_tn(dpre, h2n, "grad_w_up", 1024, 1024, 2048)
    by_shard = lambda g: g.reshape(N_DEV, g.shape[0] // N_DEV, D_MODEL)
    dh2, g_nmlp = _mm([(dpre, w_up_t)], "mlp_up_bwd", F32, 512, 1024, res=dh3, rms_bwd=(h2, norm_mlp_w))

    g_w_out = _mm_tn(ycat, dh2, "grad_w_out", 1024, 1024, 2048)
    dycat = _mm([(dh2, w_out)], "outproj_bwd", F32, 512, 2048, nt=True)
    dy_pre, dz, g_ssdn = _gnorm_bwd(dycat, y_pre, z, ssd_norm_w)
    dq, dk, dv, dck, dcq, recv_down, recv_up_t, recv_out = _att_bwd(
        qkv, dycat, o_att, lse, ccol, bsz, seq, [by_shard(g_w_down), by_shard(g_w_up_t), by_shard(g_w_out)])

    dxbc, ddt, ssd_acc = _ssd_bwd(dy_pre, xbc, dtf, dtft, bias, biast, arow, acol, dfull, hall, bsz, seq)
    dxbc_raw, g_cw, g_cb = _conv_bwd(dxbc, xbc_raw, conv_w, conv_b, bsz, seq)

    def head_cols(rows):
        cols = rows[:, :, :2, :].reshape(bsz, SSD_HEADS, seq).transpose(0, 2, 1).reshape(n, SSD_HEADS)
        return jnp.pad(cols, ((0, 0), (F_COL, LANES - 2 * F_COL)))

    ddtf, g_fb = _fox_prep_bwd(head_cols(dck), head_cols(dcq), ddt, dtf, fb, bsz, seq)

    g_dtf = _mm_tn(ddtf, h1, "grad_w_in_dtf", LANES, 1024, 2048)
    g_w_in_t = jnp.concatenate([
        _mm_tn(dz, h1, "grad_w_in_z", 1024, 1024, 2048), _mm_tn(dxbc_raw, h1, "grad_w_in_xbc", 768, 1024, 2048),
        g_dtf[:F_COL], _mm_tn(dq, h1, "grad_w_in_q", 1024, 1024, 2048), _mm_tn(dk, h1, "grad_w_in_k", 1024, 1024, 2048),
        _mm_tn(dv, h1, "grad_w_in_v", 1024, 1024, 2048), g_dtf[F_COL:2 * F_COL]], axis=0)
    pieces = [(dz, wt_z), (dxbc_raw, wt_xbc), (dq, wt_q), (dk, wt_k), (dv, wt_v), (ddtf, wt_dtf)]
    dx, g_nmix, recv_in_t = _mm(pieces, "inproj_bwd", F32, 256, 1024, res=dh2, rms_bwd=(x2, norm_mix_w),
                                exchange=("scatter", [by_shard(g_w_in_t)]))

    small = dict(
        norm_mix_w=g_nmix, norm_mlp_w=g_nmlp, norm_final_w=g_nfw, ssd_norm_w=g_ssdn, conv_b=g_cb, conv_w=g_cw,
        dt_bias=ssd_acc[16:17, :SSD_HEADS], a_log=ssd_acc[0:1, :SSD_HEADS], d_skip=ssd_acc[8:9, :SSD_HEADS],
        f_bias=g_fb[0:1, F_COL:2 * F_COL])
    recv = dict(w_in_t=recv_in_t, w_out=recv_out, w_up_t=recv_up_t, w_down=recv_down)
    return loss_row[0, 0], dx.reshape(bsz, seq, D_MODEL), small, recv


SMALL_LAYOUT = (("norm_mix_w", 0, 1, 0, D_MODEL), ("norm_mlp_w", 1, 1, 0, D_MODEL), ("norm_final_w", 2, 1, 0, D_MODEL),
                ("ssd_norm_w", 3, 1, 0, D_MODEL), ("conv_b", 4, 1, 0, CONV_CH), ("conv_w", 5, CONV_K, 0, CONV_CH),
                ("dt_bias", 9, 1, 0, SSD_HEADS), ("a_log", 9, 1, LANES, SSD_HEADS), ("d_skip", 9, 1, 2 * LANES, SSD_HEADS),
                ("f_bias", 9, 1, 3 * LANES, SSD_HEADS))
SMALL_ROWS = 2 * SUBLANES


def _pack_small(vals):
    packed = jnp.zeros((SMALL_ROWS, SMALL_COLS), F32)
    for name, r0, nrows, c0, width in SMALL_LAYOUT:
        packed = packed.at[r0:r0 + nrows, c0:c0 + width].set(vals[name].reshape(nrows, width))
    return packed


def _unpack_small(packed, like):
    out = {}
    for name, r0, nrows, c0, width in SMALL_LAYOUT:
        out[name] = packed[r0:r0 + nrows, c0:c0 + width].reshape(like[name].shape)
    return out


def kernel(x, norm_mix_w, w_in, conv_w, conv_b, dt_bias, a_log, d_skip, ssd_norm_w, f_bias, w_out, norm_mlp_w, w_up, w_down, norm_final_w, loss_target, m_norm_mix_w, m_w_in, m_conv_w, m_conv_b, m_dt_bias, m_a_log, m_d_skip, m_ssd_norm_w, m_f_bias, m_w_out, m_norm_mlp_w, m_w_up, m_w_down, m_norm_final_w, v_norm_mix_w, v_w_in, v_conv_w, v_conv_b, v_dt_bias, v_a_log, v_d_skip, v_ssd_norm_w, v_f_bias, v_w_out, v_norm_mlp_w, v_w_up, v_w_down, v_norm_final_w):
    me = 4 * lax.axis_index("x") + 2 * lax.axis_index("y") + lax.axis_index("c")
    conv_shard_w = CONV_CH // N_DEV
    zero = jnp.zeros((), jnp.int32)

    def place_conv(shard):
        return lax.dynamic_update_slice(jnp.zeros((CONV_K, CONV_CH), F32), shard[0], (zero, me * conv_shard_w))

    (g_in_t,) = _all_gather_two_level([w_in[0].T.astype(BF16)])
    late_shards = [w_out[0].astype(BF16), w_up[0].T.astype(BF16), w_down[0].astype(BF16)]
    conv_full = _small_gather(jnp.pad(place_conv(conv_w), ((0, SUBLANES - CONV_K), (0, 0))))[:CONV_K]

    loss_local, grad_x, g_small, recv = _local_step(
        x, loss_target, norm_mix_w, g_in_t.reshape(IN_WIDTH, D_MODEL), conv_full, conv_b, dt_bias, a_log, d_skip,
        ssd_norm_w, f_bias, late_shards, norm_mlp_w, norm_final_w)
    loss = lax.psum(loss_local, MESH_AXES)

    grad_in = _sum_parts(recv["w_in_t"], "sum_w_in").T[None]
    grad_up = _sum_parts(recv["w_up_t"], "sum_w_up").T[None]
    big = {
        "w_in": _adamw_parts(grad_in, w_in[0], m_w_in[0], v_w_in[0], "adamw_w_in"),
        "w_out": _adamw_parts(recv["w_out"], w_out[0], m_w_out[0], v_w_out[0], "adamw_w_out"),
        "w_up": _adamw_parts(grad_up, w_up[0], m_w_up[0], v_w_up[0], "adamw_w_up"),
        "w_down": _adamw_parts(recv["w_down"], w_down[0], m_w_down[0], v_w_down[0], "adamw_w_down"),
    }

    like = dict(norm_mix_w=norm_mix_w, norm_mlp_w=norm_mlp_w, norm_final_w=norm_final_w, ssd_norm_w=ssd_norm_w,
                conv_b=conv_b, conv_w=jnp.zeros((1, CONV_K, CONV_CH), F32), dt_bias=dt_bias, a_log=a_log,
                d_skip=d_skip, f_bias=f_bias)
    w_small = dict(like, conv_w=place_conv(conv_w))
    m_small = dict(norm_mix_w=m_norm_mix_w, norm_mlp_w=m_norm_mlp_w, norm_final_w=m_norm_final_w,
                   ssd_norm_w=m_ssd_norm_w, conv_b=m_conv_b, conv_w=place_conv(m_conv_w), dt_bias=m_dt_bias,
                   a_log=m_a_log, d_skip=m_d_skip, f_bias=m_f_bias)
    v_small = dict(norm_mix_w=v_norm_mix_w, norm_mlp_w=v_norm_mlp_w, norm_final_w=v_norm_final_w,
                   ssd_norm_w=v_ssd_norm_w, conv_b=v_conv_b, conv_w=place_conv(v_conv_w), dt_bias=v_dt_bias,
                   a_log=v_a_log, d_skip=v_d_skip, f_bias=v_f_bias)
    small = _small_reduce_adamw(_pack_small(g_small), _pack_small(w_small), _pack_small(m_small), _pack_small(v_small))
    small = [_unpack_small(s, like) for s in small]
    for s in small:
        s["conv_w"] = lax.dynamic_slice(s["conv_w"], (zero, zero, me * conv_shard_w), (1, CONV_K, conv_shard_w))

    order = ["norm_mix_w", "w_in", "conv_w", "conv_b", "dt_bias", "a_log", "d_skip", "ssd_norm_w", "f_bias", "w_out",
             "norm_mlp_w", "w_up", "w_down", "norm_final_w"]
    outs = [loss, grad_x]
    for kind in range(4):
        for name in order:
            outs.append(big[name][kind][None] if name in big else small[kind][name])
    return tuple(outs)
```

```python
import jax
import jax.numpy as jnp
from jax import lax
from jax.experimental import pallas as pl
from jax.experimental.pallas import tpu as pltpu

F32, BF16 = jnp.float32, jnp.bfloat16

D_MODEL = 1024
SSD_HEADS = 16
HEAD_DIM = 64
SSD_STATE = 128
CHUNK = 128
CONV_CH = 1536
CONV_K = 4
D_FF = 4096
MIX_WIDTH = 2048
IN_WIDTH = 5664
NORM_EPS = 1e-5
ATT_SCALE = 0.125

LANES = 128
SUBLANES = 8
HEAD_PAIRS = SSD_HEADS // 2
NEG = -1e30
VMEM_LIMIT = 52 * 1024 * 1024

ROW_XBC, ROW_DT, ROW_Q, ROW_F = 1024, 2560, 2576, 5648
F_COL = SSD_HEADS

N_DEV = 8
MESH_AXES = ("x", "y", "c")
MESH = pl.DeviceIdType.MESH

ADAM_LR, ADAM_B1, ADAM_B2, ADAM_EPS, ADAM_WD, ADAM_STEP = 0.001, 0.9, 0.999, 1e-08, 0.01, 10
ADAM_C1 = 1.0 - ADAM_B1 ** ADAM_STEP
ADAM_C2 = 1.0 - ADAM_B2 ** ADAM_STEP

SMALL_COLS = CONV_CH


def _params(sem=None):
    return pltpu.CompilerParams(dimension_semantics=sem, vmem_limit_bytes=VMEM_LIMIT)


def _sigmoid(u):
    return 1.0 / (1.0 + jnp.exp(-u))


def _softplus(u):
    return jnp.maximum(u, 0.0) + jnp.log(1.0 + jnp.exp(-jnp.abs(u)))


def _log_sigmoid(u):
    return jnp.minimum(u, 0.0) - jnp.log(1.0 + jnp.exp(-jnp.abs(u)))


def _bdot(a, b):
    return jnp.dot(a.astype(BF16), b.astype(BF16), preferred_element_type=F32)


def _bdot_nt(a, b):
    return lax.dot_general(a.astype(BF16), b.astype(BF16), (((1,), (1,)), ((), ())), preferred_element_type=F32)


def _bdot_tn(a, b):
    return lax.dot_general(a.astype(BF16), b.astype(BF16), (((0,), (0,)), ((), ())), preferred_element_type=F32)


def _split3(x):
    x1 = x.astype(BF16)
    r1 = x - x1.astype(F32)
    x2 = r1.astype(BF16)
    return x1, x2, (r1 - x2.astype(F32)).astype(BF16)


def _dot_sel(x, sel):
    s = sel.astype(BF16)
    p1, p2, p3 = (jnp.dot(p, s, preferred_element_type=F32) for p in _split3(x))
    return p1 + p2 + p3


def _sel_dot(sel, x):
    s = sel.astype(BF16)
    p1, p2, p3 = (jnp.dot(s, p, preferred_element_type=F32) for p in _split3(x))
    return p1 + p2 + p3


def _iota(shape, dim):
    return lax.broadcasted_iota(jnp.int32, shape, dim)


def _head_expand():
    return (jnp.right_shift(_iota((LANES, D_MODEL), 1), 6) == _iota((LANES, D_MODEL), 0)).astype(F32)


def _head_reduce():
    return (jnp.right_shift(_iota((D_MODEL, LANES), 0), 6) == _iota((D_MODEL, LANES), 1)).astype(F32)


def _rms_fwd(x, w, name):
    n, d = x.shape
    tm = min(1024, n)

    def body(x_ref, w_ref, o_ref):
        xv = x_ref[...]
        r = lax.rsqrt(jnp.mean(xv * xv, axis=-1, keepdims=True) + NORM_EPS)
        o_ref[...] = (xv * r * w_ref[...]).astype(BF16)

    return pl.pallas_call(
        body, name=name, grid=(n // tm,),
        in_specs=[pl.BlockSpec((tm, d), lambda i: (i, 0)), pl.BlockSpec((1, d), lambda i: (0, 0))],
        out_specs=pl.BlockSpec((tm, d), lambda i: (i, 0)),
        out_shape=jax.ShapeDtypeStruct((n, d), BF16),
        compiler_params=_params(("parallel",)),
    )(x, w)


def _mm(pairs, name, out_dtype, tm, tn, nt=False, res=None, exchange=None, rms_fwd=None, rms_bwd=None):
    m = pairs[0][0].shape[0]
    n = pairs[0][1].shape[0 if nt else 1]
    tm, tn = min(tm, m), min(tn, n)
    gm, gn = m // tm, n // tn
    npairs = len(pairs)
    kind, xs = (None, []) if exchange is None else exchange
    nx = len(xs)
    extra_in = [] if res is None else [res]
    if rms_bwd is not None:
        extra_in += list(rms_bwd)
    elif rms_fwd is not None:
        extra_in.append(rms_fwd)
    n_in = 2 * npairs + len(extra_in)
    n_out = 1 + (rms_fwd is not None or rms_bwd is not None)
    assert (rms_fwd is None and rms_bwd is None) or tn == n

    def body(*refs):
        x_refs, o_refs = refs[n_in:n_in + nx], refs[n_in + nx:n_in + nx + n_out]
        xo_refs, sems = refs[n_in + nx + n_out:n_in + 2 * nx + n_out], refs[n_in + 2 * nx + n_out:]
        extra = refs[2 * npairs:n_in]
        i, j = pl.program_id(0), pl.program_id(1)
        if nx:
            @pl.when((i == 0) & (j == 0))
            def _():
                _exchange_start(kind, x_refs, xo_refs, *sems)

        acc = None
        for p in range(npairs):
            a_v, b_v = refs[2 * p][...], refs[2 * p + 1][...]
            d = _bdot_nt(a_v, b_v) if nt else _bdot(a_v, b_v)
            acc = d if acc is None else acc + d
        if rms_bwd is not None:
            xv = extra[1][...]
            r = lax.rsqrt(jnp.mean(xv * xv, axis=-1, keepdims=True) + NORM_EPS)
            nrm = xv * r
            g = acc * extra[2][...]
            o_refs[0][...] = extra[0][...] + r * (g - nrm * jnp.mean(g * nrm, axis=-1, keepdims=True))

            @pl.when(i == 0)
            def _():
                o_refs[1][...] = jnp.zeros_like(o_refs[1])

            o_refs[1][...] += jnp.sum(acc * nrm, axis=0, keepdims=True)
        else:
            if res is not None:
                acc = extra[0][...] + acc
            o_refs[0][...] = acc.astype(o_refs[0].dtype)
            if rms_fwd is not None:
                r = lax.rsqrt(jnp.mean(acc * acc, axis=-1, keepdims=True) + NORM_EPS)
                o_refs[1][...] = (acc * r * extra[-1][...]).astype(BF16)
        if nx:
            @pl.when((i == gm - 1) & (j == gn - 1))
            def _():
                _exchange_wait(kind, x_refs, xo_refs, *sems)

    tile = pl.BlockSpec((tm, tn), lambda i, j: (i, j))
    row = pl.BlockSpec((1, tn), lambda i, j: (0, j))
    in_specs, args = [], []
    for a, b in pairs:
        k = a.shape[1]
        in_specs.append(pl.BlockSpec((tm, k), lambda i, j: (i, 0)))
        in_specs.append(pl.BlockSpec((tn, k), lambda i, j: (j, 0)) if nt else pl.BlockSpec((k, tn), lambda i, j: (0, j)))
        args += [a, b]
    in_specs += [row if e.shape[0] == 1 else tile for e in extra_in]
    out_specs, out_shape = [tile], [jax.ShapeDtypeStruct((m, n), out_dtype)]
    if rms_bwd is not None:
        out_specs.append(row)
        out_shape.append(jax.ShapeDtypeStruct((1, n), F32))
    elif rms_fwd is not None:
        out_specs.append(tile)
        out_shape.append(jax.ShapeDtypeStruct((m, n), BF16))
    hbm = pl.BlockSpec(memory_space=pl.ANY)
    sequential = nx or rms_bwd is not None
    outs = pl.pallas_call(
        body, name=name, grid=(gm, gn), in_specs=in_specs + [hbm] * nx,
        out_specs=out_specs + [hbm] * nx,
        out_shape=out_shape + _exchange_shapes(kind, xs),
        scratch_shapes=_exchange_scratch(nx),
        compiler_params=_params(("arbitrary", "arbitrary") if sequential else ("parallel", "parallel")),
    )(*args, *extra_in, *xs)
    return outs if len(outs) > 1 else outs[0]


def _mm_tn(a, b, name, tm, tn, tk):
    t, m = a.shape
    n = b.shape[1]
    tm, tn, tk = min(tm, m), min(tn, n), min(tk, t)
    nk = t // tk

    def body(a_ref, b_ref, o_ref, acc_ref):
        kk = pl.program_id(2)

        @pl.when(kk == 0)
        def _():
            acc_ref[...] = jnp.zeros_like(acc_ref)

        acc_ref[...] += _bdot_tn(a_ref[...], b_ref[...])

        @pl.when(kk == nk - 1)
        def _():
            o_ref[...] = acc_ref[...].astype(o_ref.dtype)

    return pl.pallas_call(
        body, name=name, grid=(m // tm, n // tn, nk),
        in_specs=[pl.BlockSpec((tk, tm), lambda i, j, kk: (kk, i)), pl.BlockSpec((tk, tn), lambda i, j, kk: (kk, j))],
        out_specs=pl.BlockSpec((tm, tn), lambda i, j, kk: (i, j)),
        out_shape=jax.ShapeDtypeStruct((m, n), BF16),
        scratch_shapes=[pltpu.VMEM((tm, tn), F32)],
        compiler_params=_params(("parallel", "parallel", "arbitrary")),
    )(a, b)


def _mlp_down_loss(pre, w_down, h2, target, w):
    m, k = pre.shape
    d = w_down.shape[1]
    tm = min(512, m)

    def body(p_ref, b_ref, r_ref, t_ref, w_ref, dh_ref, loss_ref, gw_ref):
        u = jnp.square(jnp.maximum(p_ref[...].astype(F32), 0.0))
        xv = r_ref[...] + _bdot(u, b_ref[...])
        r = lax.rsqrt(jnp.mean(xv * xv, axis=-1, keepdims=True) + NORM_EPS)
        nrm = xv * r
        wv = w_ref[...]
        err = nrm * wv - t_ref[...]
        part = 0.5 * jnp.sum(jnp.mean(err * err, axis=-1, keepdims=True), axis=0, keepdims=True)
        dy = err * (1.0 / d)
        g = dy * wv
        dh_ref[...] = r * (g - nrm * jnp.mean(g * nrm, axis=-1, keepdims=True))

        @pl.when(pl.program_id(0) == 0)
        def _():
            loss_ref[...] = jnp.zeros_like(loss_ref)
            gw_ref[...] = jnp.zeros_like(gw_ref)

        loss_ref[...] += jnp.broadcast_to(part, loss_ref.shape)
        gw_ref[...] += jnp.sum(dy * nrm, axis=0, keepdims=True)

    tok = pl.BlockSpec((tm, d), lambda i: (i, 0))
    row = pl.BlockSpec((1, d), lambda i: (0, 0))
    return pl.pallas_call(
        body, name="mlp_down_loss", grid=(m // tm,),
        in_specs=[pl.BlockSpec((tm, k), lambda i: (i, 0)), pl.BlockSpec((k, d), lambda i: (0, 0)), tok, tok, row],
        out_specs=[tok, pl.BlockSpec((1, LANES), lambda i: (0, 0)), row],
        out_shape=[jax.ShapeDtypeStruct((m, d), F32), jax.ShapeDtypeStruct((1, LANES), F32),
                   jax.ShapeDtypeStruct((1, d), F32)],
        compiler_params=_params(("arbitrary",)),
    )(pre, w_down, h2, target, w)


def _mlp_down_bwd(dh3, w_down, pre):
    m, k = dh3.shape
    n = w_down.shape[0]
    tm, tn = min(256, m), n

    def body(a_ref, b_ref, p_ref, dpre_ref, u_ref):
        r = jnp.maximum(p_ref[...].astype(F32), 0.0)
        du = _bdot_nt(a_ref[...], b_ref[...])
        dpre_ref[...] = (du * (2.0 * r)).astype(BF16)
        u_ref[...] = (r * r).astype(BF16)

    blk = pl.BlockSpec((tm, tn), lambda i, j: (i, j))
    return pl.pallas_call(
        body, name="mlp_down_bwd", grid=(m // tm, n // tn),
        in_specs=[pl.BlockSpec((tm, k), lambda i, j: (i, 0)), pl.BlockSpec((tn, k), lambda i, j: (j, 0)), blk],
        out_specs=[blk, blk],
        out_shape=[jax.ShapeDtypeStruct((m, n), BF16), jax.ShapeDtypeStruct((m, n), BF16)],
        compiler_params=_params(("parallel", "parallel")),
    )(dh3, w_down, pre)


CONV_TC = 512


def _conv_fwd(xbc, cw, cb, bsz, seq):
    tt = min(512, seq)
    nt, hb = seq // tt, tt // SUBLANES
    x3 = xbc.reshape(bsz, seq, CONV_CH)

    def body(xm_ref, xh_ref, w_ref, b_ref, o_ref):
        i = pl.program_id(2)
        x = xm_ref[0]
        halo = jnp.where(i > 0, xh_ref[0], 0.0)
        xx = jnp.concatenate([halo, x], axis=0)
        acc = x * w_ref[3:4, :] + b_ref[...]
        for s in range(1, CONV_K):
            acc = acc + pltpu.roll(xx, s, 0)[SUBLANES:, :] * w_ref[3 - s:4 - s, :]
        o_ref[0] = acc * _sigmoid(acc)

    out = pl.pallas_call(
        body, name="conv_fwd", grid=(CONV_CH // CONV_TC, bsz, nt),
        in_specs=[pl.BlockSpec((1, tt, CONV_TC), lambda c, b, i: (b, i, c)),
                  pl.BlockSpec((1, SUBLANES, CONV_TC), lambda c, b, i: (b, jnp.maximum(i * hb - 1, 0), c)),
                  pl.BlockSpec((CONV_K, CONV_TC), lambda c, b, i: (0, c)),
                  pl.BlockSpec((1, CONV_TC), lambda c, b, i: (0, c))],
        out_specs=pl.BlockSpec((1, tt, CONV_TC), lambda c, b, i: (b, i, c)),
        out_shape=jax.ShapeDtypeStruct((bsz, seq, CONV_CH), F32),
        compiler_params=_params(("parallel", "parallel", "parallel")),
    )(x3, x3, cw, cb)
    return out.reshape(bsz * seq, CONV_CH)


def _conv_bwd(da, xbc, cw, cb, bsz, seq):
    tt = min(512, seq)
    nt, hb = seq // tt, tt // SUBLANES
    x3 = xbc.reshape(bsz, seq, CONV_CH)
    da3 = da.reshape(bsz, seq, CONV_CH)
    n2 = tt + SUBLANES

    def body(dam_ref, daa_ref, xm_ref, xb_ref, xa_ref, w_ref, b_ref, du_ref, gw_ref, gb_ref):
        bi, i = pl.program_id(1), pl.program_id(2)
        before = jnp.where(i > 0, xb_ref[0], 0.0)
        after = jnp.where(i < nt - 1, xa_ref[0], 0.0)
        xx = jnp.concatenate([before, xm_ref[0], after], axis=0)
        rolled = [xx] + [pltpu.roll(xx, s, 0) for s in range(1, CONV_K)]
        pre = b_ref[...]
        for s in range(CONV_K):
            pre = pre + rolled[s][SUBLANES:, :] * w_ref[3 - s:4 - s, :]
        da_ext = jnp.concatenate([dam_ref[0], jnp.where(i < nt - 1, daa_ref[0], 0.0)], axis=0)
        sg = _sigmoid(pre)
        dpre = da_ext * sg * (1.0 + pre * (1.0 - sg))
        du = dpre[:tt, :] * w_ref[3:4, :]
        for s in range(1, CONV_K):
            du = du + pltpu.roll(dpre, n2 - s, 0)[:tt, :] * w_ref[3 - s:4 - s, :]
        du_ref[0] = du.astype(BF16)
        dpm = dpre[:tt, :]
        gws = [jnp.sum(dpm * rolled[3 - kk][SUBLANES:SUBLANES + tt, :], axis=0, keepdims=True) for kk in range(CONV_K)]

        @pl.when((bi == 0) & (i == 0))
        def _():
            gw_ref[...] = jnp.zeros_like(gw_ref)
            gb_ref[...] = jnp.zeros_like(gb_ref)

        gw_ref[...] += jnp.concatenate(gws, axis=0)
        gb_ref[...] += jnp.sum(dpm, axis=0, keepdims=True)

    main = pl.BlockSpec((1, tt, CONV_TC), lambda c, b, i: (b, i, c))
    prev = pl.BlockSpec((1, SUBLANES, CONV_TC), lambda c, b, i: (b, jnp.maximum(i * hb - 1, 0), c))
    nxt = pl.BlockSpec((1, SUBLANES, CONV_TC), lambda c, b, i: (b, jnp.minimum((i + 1) * hb, seq // SUBLANES - 1), c))
    du, gw, gb = pl.pallas_call(
        body, name="conv_bwd", grid=(CONV_CH // CONV_TC, bsz, nt),
        in_specs=[main, nxt, main, prev, nxt,
                  pl.BlockSpec((CONV_K, CONV_TC), lambda c, b, i: (0, c)),
                  pl.BlockSpec((1, CONV_TC), lambda c, b, i: (0, c))],
        out_specs=[main, pl.BlockSpec((CONV_K, CONV_TC), lambda c, b, i: (0, c)),
                   pl.BlockSpec((1, CONV_TC), lambda c, b, i: (0, c))],
        out_shape=[jax.ShapeDtypeStruct((bsz, seq, CONV_CH), BF16), jax.ShapeDtypeStruct((CONV_K, CONV_CH), F32),
                   jax.ShapeDtypeStruct((1, CONV_CH), F32)],
        compiler_params=_params(("parallel", "arbitrary", "arbitrary")),
    )(da3, da3, x3, x3, x3, cw, cb)
    return du.reshape(bsz * seq, CONV_CH), gw, gb


def _ssd_prologue(dtf_ref, dtft_ref, bias_ref, biast_ref, arow_ref, acol_ref, dtfull_scr, acfull_scr, acr_scr):
    tri = (_iota((CHUNK, CHUNK), 1) <= _iota((CHUNK, CHUNK), 0)).astype(F32)
    triu = (_iota((CHUNK, CHUNK), 0) <= _iota((CHUNK, CHUNK), 1)).astype(F32)
    dtc = _softplus(dtf_ref[0] + bias_ref[...])
    a = dtc * arow_ref[...]
    acum = _sel_dot(tri, a)
    expand = _head_expand()
    dtfull_scr[...] = _dot_sel(dtc, expand)
    acfull_scr[...] = _dot_sel(acum, expand)
    dtr = _softplus(dtft_ref[0] + biast_ref[...])
    acr_scr[...] = _dot_sel(dtr * acol_ref[...], triu)
    return dtc, acum


def _decay_matrix(acum, acr_scr, h):
    lane = _iota((CHUNK, LANES), 1)
    ac_col = jnp.sum(jnp.where(lane == h, acum, 0.0), axis=1, keepdims=True)
    ac_row = acr_scr[h:h + 1, :]
    causal = _iota((CHUNK, CHUNK), 1) <= _iota((CHUNK, CHUNK), 0)
    return jnp.exp(jnp.where(causal, ac_col - ac_row, NEG))


def _ssd_fwd(xbc, dtf, dtft, bias, biast, arow, acol, dfull, bsz, seq):
    nc = seq // CHUNK
    x3 = xbc.reshape(bsz, seq, CONV_CH)

    def body(xbc_ref, dtf_ref, dtft_ref, bias_ref, biast_ref, arow_ref, acol_ref, dfull_ref,
             y_ref, hall_ref, h_scr, dtfull_scr, acfull_scr, acr_scr):
        @pl.when(pl.program_id(1) == 0)
        def _():
            h_scr[...] = jnp.zeros_like(h_scr)

        _, acum = _ssd_prologue(dtf_ref, dtft_ref, bias_ref, biast_ref, arow_ref, acol_ref,
                                dtfull_scr, acfull_scr, acr_scr)
        lane = _iota((CHUNK, LANES), 1)
        for g in range(2):
            bg = xbc_ref[0, :, D_MODEL + LANES * g:D_MODEL + LANES * (g + 1)]
            cg = xbc_ref[0, :, D_MODEL + 2 * LANES + LANES * g:D_MODEL + 2 * LANES + LANES * (g + 1)]
            cg_bf = cg.astype(BF16)
            gmat = _bdot_nt(cg_bf, bg)
            bgt_bf = bg.T.astype(BF16)
            for j in range(4):
                p = 4 * g + j
                sl = slice(LANES * p, LANES * (p + 1))
                xs = xbc_ref[0, :, sl]
                ac = acfull_scr[:, sl]
                acl = acfull_scr[CHUNK - 1:CHUNK, sl]
                xdt = xs * dtfull_scr[:, sl]
                xdt_bf = xdt.astype(BF16)
                hp = h_scr[p]
                hall_ref[0, 0, p] = hp
                yo = _bdot(cg_bf, hp) * jnp.exp(ac)
                yd = []
                for hh in range(2):
                    mmat = gmat * _decay_matrix(acum, acr_scr, 2 * p + hh)
                    yd.append(_bdot(mmat, xdt_bf))
                y_ref[0, :, sl] = jnp.where(lane < HEAD_DIM, yd[0], yd[1]) + yo + dfull_ref[:, sl] * xs
                h_scr[p] = hp * jnp.exp(acl) + _bdot(bgt_bf, xdt * jnp.exp(acl - ac))

    row = lambda w: pl.BlockSpec((1, w), lambda b, c: (0, 0))
    y, hall = pl.pallas_call(
        body, name="ssd_fwd", grid=(bsz, nc),
        in_specs=[pl.BlockSpec((1, CHUNK, CONV_CH), lambda b, c: (b, c, 0)),
                  pl.BlockSpec((1, CHUNK, LANES), lambda b, c: (b, c, 0)),
                  pl.BlockSpec((1, LANES, CHUNK), lambda b, c: (b, 0, c)),
                  row(LANES), pl.BlockSpec((LANES, 1), lambda b, c: (0, 0)),
                  row(LANES), pl.BlockSpec((LANES, 1), lambda b, c: (0, 0)), row(D_MODEL)],
        out_specs=[pl.BlockSpec((1, CHUNK, D_MODEL), lambda b, c: (b, c, 0)),
                   pl.BlockSpec((1, 1, HEAD_PAIRS, SSD_STATE, LANES), lambda b, c: (b, c, 0, 0, 0))],
        out_shape=[jax.ShapeDtypeStruct((bsz, seq, D_MODEL), F32),
                   jax.ShapeDtypeStruct((bsz, nc, HEAD_PAIRS, SSD_STATE, LANES), F32)],
        scratch_shapes=[pltpu.VMEM((HEAD_PAIRS, SSD_STATE, LANES), F32), pltpu.VMEM((CHUNK, D_MODEL), F32),
                        pltpu.VMEM((CHUNK, D_MODEL), F32), pltpu.VMEM((LANES, CHUNK), F32)],
        compiler_params=_params(("parallel", "arbitrary")),
    )(x3, dtf.reshape(bsz, seq, LANES), dtft, bias, biast, arow, acol, dfull)
    return y.reshape(bsz * seq, D_MODEL), hall


def _ssd_bwd(dy, xbc, dtf, dtft, bias, biast, arow, acol, dfull, hall, bsz, seq):
    nc = seq // CHUNK
    x3 = xbc.reshape(bsz, seq, CONV_CH)
    dy3 = dy.reshape(bsz, seq, D_MODEL)

    def body(dy_ref, xbc_ref, dtf_ref, dtft_ref, bias_ref, biast_ref, arow_ref, acol_ref, dfull_ref, hall_ref,
             dx_ref, ddt_ref, acc_ref, dh_scr, dtfull_scr, acfull_scr, acr_scr, csum_scr, dacf_scr, ddtf_scr, ddl_scr):
        first = (pl.program_id(0) == 0) & (pl.program_id(1) == 0)

        @pl.when(first)
        def _():
            acc_ref[...] = jnp.zeros_like(acc_ref)

        @pl.when(pl.program_id(1) == 0)
        def _():
            dh_scr[...] = jnp.zeros_like(dh_scr)

        dtc, acum = _ssd_prologue(dtf_ref, dtft_ref, bias_ref, biast_ref, arow_ref, acol_ref,
                                  dtfull_scr, acfull_scr, acr_scr)
        csum_scr[...] = jnp.zeros_like(csum_scr)
        lane = _iota((CHUNK, LANES), 1)
        last_row = _iota((CHUNK, LANES), 0) == CHUNK - 1
        rs16 = jnp.zeros((CHUNK, LANES), F32)
        for g in range(2):
            bsl = slice(D_MODEL + LANES * g, D_MODEL + LANES * (g + 1))
            csl = slice(D_MODEL + 2 * LANES + LANES * g, D_MODEL + 2 * LANES + LANES * (g + 1))
            bg_bf = xbc_ref[0, :, bsl].astype(BF16)
            cg = xbc_ref[0, :, csl]
            cg_bf = cg.astype(BF16)
            cgt_bf = cg.T.astype(BF16)
            gmat = _bdot_nt(cg_bf, bg_bf)
            dg = jnp.zeros((CHUNK, CHUNK), F32)
            dbg = jnp.zeros((CHUNK, SSD_STATE), F32)
            dcg = jnp.zeros((CHUNK, SSD_STATE), F32)
            for j in range(4):
                p = 4 * g + j
                sl = slice(LANES * p, LANES * (p + 1))
                xs = xbc_ref[0, :, sl]
                dt = dtfull_scr[:, sl]
                ac = acfull_scr[:, sl]
                acl = acfull_scr[CHUNK - 1:CHUNK, sl]
                dyp = dy_ref[0, :, sl]
                xdt = xs * dt
                xdt_bf = xdt.astype(BF16)
                e = jnp.exp(ac)
                dsd = jnp.exp(acl - ac)
                cd = jnp.exp(acl)
                xds_bf = (xdt * dsd).astype(BF16)
                hp = hall_ref[0, 0, p]
                hp_bf = hp.astype(BF16)
                dhn = dh_scr[p]
                dhn_bf = dhn.astype(BF16)
                yo = _bdot(cg_bf, hp_bf) * e
                dw_bf = (dyp * e).astype(BF16)
                dcg = dcg + _bdot_nt(dw_bf, hp_bf)
                dhin = _bdot(cgt_bf, dw_bf)
                dac = dyp * yo
                dacl = jnp.sum(dhn * hp, axis=0, keepdims=True) * cd
                dxds = _bdot(bg_bf, dhn_bf)
                dbg = dbg + _bdot_nt(xds_bf, dhn_bf)
                dxdt = dxds * dsd
                tdec = dxds * xdt * dsd
                dacl = dacl + jnp.sum(tdec, axis=0, keepdims=True)
                dac = dac - tdec
                dh_scr[p] = dhn * cd + dhin
                for hh in range(2):
                    h = 2 * p + hh
                    lm = (lane < HEAD_DIM) if hh == 0 else (lane >= HEAD_DIM)
                    dec = _decay_matrix(acum, acr_scr, h)
                    mmat = gmat * dec
                    dyh_bf = jnp.where(lm, dyp, 0.0).astype(BF16)
                    dm = _bdot_nt(dyh_bf, xdt_bf)
                    dxdt = dxdt + _bdot(mmat.T, dyh_bf)
                    dg = dg + dm * dec
                    q = dm * mmat
                    rs16 = rs16 + jnp.where(lane == h, jnp.sum(q, axis=1, keepdims=True), 0.0)
                    csum_scr[h:h + 1, :] = jnp.sum(q, axis=0, keepdims=True)
                dx_ref[0, :, sl] = dfull_ref[:, sl] * dyp + dxdt * dt
                dacf_scr[:, sl] = dac + jnp.where(last_row, dacl, 0.0)
                ddtf_scr[:, sl] = dxdt * xs
                ddl_scr[:, sl] = jnp.broadcast_to(jnp.sum(dyp * xs, axis=0, keepdims=True), (SUBLANES, LANES))
            dg_bf = dg.astype(BF16)
            dx_ref[0, :, bsl] = dbg + _bdot(dg.T, cg_bf)
            dx_ref[0, :, csl] = dcg + _bdot(dg_bf, bg_bf)
        reduce = _head_reduce()
        triu = (_iota((CHUNK, CHUNK), 0) <= _iota((CHUNK, CHUNK), 1)).astype(F32)
        dac16 = _dot_sel(dacf_scr[...], reduce) + rs16 - csum_scr[...].T
        da16 = _sel_dot(triu, dac16)
        ddt16 = da16 * arow_ref[...] + _dot_sel(ddtf_scr[...], reduce)
        ddtraw = ddt16 * _sigmoid(dtf_ref[0] + bias_ref[...])
        ddt_ref[0] = ddtraw
        acc_ref[0:8, :] += jnp.broadcast_to(jnp.sum(da16 * dtc * arow_ref[...], axis=0, keepdims=True), (SUBLANES, LANES))
        acc_ref[8:16, :] += _dot_sel(ddl_scr[...], reduce)
        acc_ref[16:24, :] += jnp.broadcast_to(jnp.sum(ddtraw, axis=0, keepdims=True), (SUBLANES, LANES))

    rev = lambda b, c: (b, nc - 1 - c, 0)
    row = lambda w: pl.BlockSpec((1, w), lambda b, c: (0, 0))
    dx, ddt, acc = pl.pallas_call(
        body, name="ssd_bwd", grid=(bsz, nc),
        in_specs=[pl.BlockSpec((1, CHUNK, D_MODEL), rev), pl.BlockSpec((1, CHUNK, CONV_CH), rev),
                  pl.BlockSpec((1, CHUNK, LANES), rev),
                  pl.BlockSpec((1, LANES, CHUNK), lambda b, c: (b, 0, nc - 1 - c)),
                  row(LANES), pl.BlockSpec((LANES, 1), lambda b, c: (0, 0)),
                  row(LANES), pl.BlockSpec((LANES, 1), lambda b, c: (0, 0)), row(D_MODEL),
                  pl.BlockSpec((1, 1, HEAD_PAIRS, SSD_STATE, LANES), lambda b, c: (b, nc - 1 - c, 0, 0, 0))],
        out_specs=[pl.BlockSpec((1, CHUNK, CONV_CH), rev), pl.BlockSpec((1, CHUNK, LANES), rev),
                   pl.BlockSpec((3 * SUBLANES, LANES), lambda b, c: (0, 0))],
        out_shape=[jax.ShapeDtypeStruct((bsz, seq, CONV_CH), F32), jax.ShapeDtypeStruct((bsz, seq, LANES), F32),
                   jax.ShapeDtypeStruct((3 * SUBLANES, LANES), F32)],
        scratch_shapes=[pltpu.VMEM((HEAD_PAIRS, SSD_STATE, LANES), F32), pltpu.VMEM((CHUNK, D_MODEL), F32),
                        pltpu.VMEM((CHUNK, D_MODEL), F32), pltpu.VMEM((LANES, CHUNK), F32),
                        pltpu.VMEM((LANES, CHUNK), F32), pltpu.VMEM((CHUNK, D_MODEL), F32),
                        pltpu.VMEM((CHUNK, D_MODEL), F32), pltpu.VMEM((SUBLANES, D_MODEL), F32)],
        compiler_params=_params(("arbitrary", "arbitrary")),
    )(dy3, x3, dtf.reshape(bsz, seq, LANES), dtft, bias, biast, arow, acol, dfull, hall)
    return dx.reshape(bsz * seq, CONV_CH), ddt.reshape(bsz * seq, LANES), acc


GROUP_W = D_MODEL // 2


def _gnorm_fwd(y, z, o_att, w):
    n = y.shape[0]
    tm = min(1024, n)

    def body(y_ref, z_ref, o_ref, w_ref, out_ref):
        zv = z_ref[...]
        g = y_ref[...] * (zv * _sigmoid(zv))
        for gi in range(2):
            sl = slice(GROUP_W * gi, GROUP_W * (gi + 1))
            gs = g[:, sl]
            r = lax.rsqrt(jnp.mean(gs * gs, axis=-1, keepdims=True) + NORM_EPS)
            out_ref[:, sl] = (gs * r * w_ref[:, sl]).astype(BF16)
        out_ref[:, D_MODEL:] = o_ref[...].astype(BF16)

    tok = pl.BlockSpec((tm, D_MODEL), lambda i: (i, 0))
    return pl.pallas_call(
        body, name="gnorm_fwd", grid=(n // tm,),
        in_specs=[tok, tok, tok, pl.BlockSpec((1, D_MODEL), lambda i: (0, 0))],
        out_specs=pl.BlockSpec((tm, MIX_WIDTH), lambda i: (i, 0)),
        out_shape=jax.ShapeDtypeStruct((n, MIX_WIDTH), BF16),
        compiler_params=_params(("parallel",)),
    )(y, z, o_att, w)


def _gnorm_bwd(dycat, y, z, w):
    n = y.shape[0]
    tm = min(512, n)

    def body(dn_ref, y_ref, z_ref, w_ref, dy_ref, dz_ref, gw_ref):
        zv, yv = z_ref[...], y_ref[...]
        sz = _sigmoid(zv)
        sil = zv * sz
        g = yv * sil

        @pl.when(pl.program_id(0) == 0)
        def _():
            gw_ref[...] = jnp.zeros_like(gw_ref)

        for gi in range(2):
            sl = slice(GROUP_W * gi, GROUP_W * (gi + 1))
            gs = g[:, sl]
            r = lax.rsqrt(jnp.mean(gs * gs, axis=-1, keepdims=True) + NORM_EPS)
            nrm = gs * r
            dyn = dn_ref[:, sl]
            dn = dyn * w_ref[:, sl]
            dgs = r * (dn - nrm * jnp.mean(dn * nrm, axis=-1, keepdims=True))
            dy_ref[:, sl] = dgs * sil[:, sl]
            dz_ref[:, sl] = (dgs * yv[:, sl] * (sz[:, sl] * (1.0 + zv[:, sl] * (1.0 - sz[:, sl])))).astype(BF16)
            gw_ref[:, sl] += jnp.sum(dyn * nrm, axis=0, keepdims=True)

    tok = pl.BlockSpec((tm, D_MODEL), lambda i: (i, 0))
    row = pl.BlockSpec((1, D_MODEL), lambda i: (0, 0))
    return pl.pallas_call(
        body, name="gnorm_bwd", grid=(n // tm,),
        in_specs=[tok, tok, tok, row], out_specs=[tok, tok, row],
        out_shape=[jax.ShapeDtypeStruct((n, D_MODEL), F32), jax.ShapeDtypeStruct((n, D_MODEL), BF16),
                   jax.ShapeDtypeStruct((1, D_MODEL), F32)],
        compiler_params=_params(("arbitrary",)),
    )(dycat, y, z, w)


AUX_C = 3
AUX_ONE = 3


def _aux_base(hh):
    return HEAD_DIM * (1 - hh)


def _fox_prep(dtf, fb, bsz, seq):
    def body(x_ref, b_ref, aux_ref):
        tri = (_iota((CHUNK, CHUNK), 1) <= _iota((CHUNK, CHUNK), 0)).astype(F32)
        row, col = _iota((LANES, D_MODEL), 0), _iota((LANES, D_MODEL), 1)
        lane = jnp.bitwise_and(col, LANES - 1)
        other = (lane < HEAD_DIM).astype(jnp.int32)
        term = lane - HEAD_DIM * (1 - other)
        src = F_COL + 2 * jnp.right_shift(col, 7) + other
        place = [jnp.where((row == src) & (term == i), -1.0, 0.0).astype(BF16) for i in range(AUX_C)]
        lane1 = jnp.bitwise_and(_iota((1, D_MODEL), 1), LANES - 1)
        ones_lane = (lane1 - HEAD_DIM * (1 - (lane1 < HEAD_DIM).astype(jnp.int32)) == AUX_ONE).astype(F32)
        carry = jnp.zeros((1, LANES), F32)
        for j in range(seq // CHUNK):
            sl = slice(CHUNK * j, CHUNK * (j + 1))
            lf = _log_sigmoid(x_ref[sl, :] + b_ref[...])
            c = _sel_dot(tri, lf) + carry
            carry = carry + jnp.sum(lf, axis=0, keepdims=True)
            t1, t2, t3 = (jnp.dot(t, p, preferred_element_type=F32) for t, p in zip(_split3(c), place))
            aux_ref[sl, :] = (t1 + t2 + t3 + ones_lane).astype(BF16)

    return pl.pallas_call(
        body, name="fox_prep", grid=(bsz,),
        in_specs=[pl.BlockSpec((seq, LANES), lambda b: (b, 0)), pl.BlockSpec((1, LANES), lambda b: (0, 0))],
        out_specs=pl.BlockSpec((seq, D_MODEL), lambda b: (b, 0)),
        out_shape=jax.ShapeDtypeStruct((bsz * seq, D_MODEL), BF16),
        compiler_params=_params(("parallel",)),
    )(dtf, fb)


def _fox_prep_bwd(dck, dcq, ddt, dtf, fb, bsz, seq):
    nj = seq // CHUNK

    def body(dck_ref, dcq_ref, ddt_ref, x_ref, b_ref, out_ref, gb_ref):
        triu = (_iota((CHUNK, CHUNK), 0) <= _iota((CHUNK, CHUNK), 1)).astype(F32)
        is_f = (_iota((CHUNK, LANES), 1) >= F_COL) & (_iota((CHUNK, LANES), 1) < 2 * F_COL)
        carry = jnp.zeros((1, LANES), F32)
        total = jnp.zeros((1, LANES), F32)
        for j in range(nj - 1, -1, -1):
            sl = slice(CHUNK * j, CHUNK * (j + 1))
            dcv = dck_ref[sl, :] + dcq_ref[sl, :]
            dlf = _sel_dot(triu, dcv) + carry
            carry = carry + jnp.sum(dcv, axis=0, keepdims=True)
            df = jnp.where(is_f, dlf * _sigmoid(-(x_ref[sl, :] + b_ref[...])), 0.0)
            out_ref[sl, :] = (df + ddt_ref[sl, :]).astype(BF16)
            total = total + jnp.sum(df, axis=0, keepdims=True)

        @pl.when(pl.program_id(0) == 0)
        def _():
            gb_ref[...] = jnp.zeros_like(gb_ref)

        gb_ref[...] += jnp.broadcast_to(total, (SUBLANES, LANES))

    blk = pl.BlockSpec((seq, LANES), lambda b: (b, 0))
    return pl.pallas_call(
        body, name="fox_prep_bwd", grid=(bsz,),
        in_specs=[blk, blk, blk, blk, pl.BlockSpec((1, LANES), lambda b: (0, 0))],
        out_specs=[blk, pl.BlockSpec((SUBLANES, LANES), lambda b: (0, 0))],
        out_shape=[jax.ShapeDtypeStruct((bsz * seq, LANES), BF16), jax.ShapeDtypeStruct((SUBLANES, LANES), F32)],
        compiler_params=_params(("arbitrary",)),
    )(dck, dcq, ddt, dtf, fb)


ATT_BLOCK_FWD = 512
ATT_BLOCK_KEYS = 256
ATT_BLOCK_BWD = 256


def _att_operands(q_ref, k_ref, aux_ref, hh):
    lane = _iota((1, LANES), 1)
    lm = (lane < HEAD_DIM) if hh == 0 else (lane >= HEAD_DIM)
    base = _aux_base(hh)
    zero = jnp.zeros((1, LANES), BF16)
    ka = jnp.where(lm, k_ref[...], jnp.where((lane >= base) & (lane <= base + AUX_ONE), aux_ref[...], zero))
    qa = jnp.where(lm, q_ref[...] * ATT_SCALE,
                   jnp.where((lane >= base) & (lane < base + AUX_C), jnp.ones((1, LANES), BF16), zero))
    return lm, base, qa, ka


def _att_fwd(qkv, ccol, bsz, seq, gather):
    bq, bk = min(ATT_BLOCK_FWD, seq), min(ATT_BLOCK_KEYS, seq)
    nq, ratio = seq // bq, bq // bk
    nx = len(gather)

    def body(*refs):
        q_ref, k_ref, v_ref, c_ref = refs[:4]
        x_refs = refs[4:4 + nx]
        o_ref, lse_ref = refs[4 + nx:6 + nx]
        xo_refs = refs[6 + nx:6 + 2 * nx]
        qa_scr, ka_scr, vt_scr = refs[6 + 2 * nx:9 + 2 * nx]
        sems = refs[9 + 2 * nx:]
        pair = pl.program_id(1)

        @pl.when((pl.program_id(0) == 0) & (pair == 0))
        def _():
            _exchange_start("gather", x_refs, xo_refs, *sems)

        lse_ref[...] = jnp.zeros_like(lse_ref)
        for hh in range(2):
            _, _, qa, ka = _att_operands(q_ref, k_ref, c_ref, hh)
            qa_scr[hh] = qa
            ka_scr[hh] = ka
        for t0 in range(0, seq, bq):
            vt_scr[:, t0:t0 + bq] = v_ref[t0:t0 + bq, :].astype(F32).T.astype(BF16)
        key_minus_query = _iota((bk, bq), 0) - _iota((bk, bq), 1)

        def q_step(i, carry0):
            qrows = pl.ds(pl.multiple_of(i * bq, bq), bq)

            def scores(j):
                krows = pl.ds(pl.multiple_of(j * bk, bk), bk)
                return tuple(_bdot_nt(ka_scr[hh, krows, :], qa_scr[hh, qrows, :]) for hh in range(2))

            def update(state, st_pair, j, diag):
                krows = pl.ds(pl.multiple_of(j * bk, bk), bk)
                out = []
                for hh in range(2):
                    m, l, acc = state[hh]
                    st = st_pair[hh]
                    if diag is not None:
                        st = jnp.where(key_minus_query + diag * bk <= 0, st, NEG)
                    mn = jnp.maximum(m, jnp.max(st, axis=0, keepdims=True))
                    alpha = jnp.exp(m - mn)
                    pt = jnp.exp(st - mn)
                    l = alpha * l + jnp.sum(pt, axis=0, keepdims=True)
                    vt = vt_scr[HEAD_DIM * hh:HEAD_DIM * (hh + 1), krows]
                    out.append((mn, l, alpha * acc + _bdot(vt, pt)))
                return tuple(out)

            def step(j, carry):
                state, s_cur = carry
                s_next = scores(j + 1)
                return update(state, s_cur, j, None), s_next

            def step_pair(t, carry):
                state, s_cur = carry
                s_second = scores(2 * t + 1)
                s_next = scores(2 * t + 2)
                state = update(state, s_cur, 2 * t, None)
                return update(state, s_second, 2 * t + 1, None), s_next

            one = (jnp.full((1, bq), NEG, F32), jnp.zeros((1, bq), F32), jnp.zeros((HEAD_DIM, bq), F32))
            first_diag = i * ratio
            if ratio % 2 == 0:
                state, s_cur = lax.fori_loop(0, first_diag // 2, step_pair, ((one, one), scores(0)))
            else:
                state, s_cur = lax.fori_loop(0, first_diag, step, ((one, one), scores(0)))
            for r in range(ratio):
                s_next = scores(first_diag + r + 1) if r + 1 < ratio else None
                state = update(state, s_cur, first_diag + r, r)
                s_cur = s_next
            (m0, l0, acc0), (m1, l1, acc1) = state
            ot = jnp.concatenate([acc0 * (1.0 / l0), acc1 * (1.0 / l1)], axis=0)
            o_ref[qrows, :] = ot.T
            lse_ref[0, 0, 0:1, qrows] = m0 + jnp.log(l0)
            lse_ref[0, 0, 1:2, qrows] = m1 + jnp.log(l1)
            return carry0

        lax.fori_loop(0, nq, q_step, 0)

        @pl.when((pl.program_id(0) == bsz - 1) & (pair == HEAD_PAIRS - 1))
        def _():
            _exchange_wait("gather", x_refs, xo_refs, *sems)

    col = lambda off: pl.BlockSpec((seq, LANES), lambda b, p: (b, off + p))
    head2 = pltpu.VMEM((2, seq, LANES), BF16)
    hbm = pl.BlockSpec(memory_space=pl.ANY)
    return pl.pallas_call(
        body, name="att_fwd", grid=(bsz, HEAD_PAIRS),
        in_specs=[col(0), col(HEAD_PAIRS), col(2 * HEAD_PAIRS), col(0)] + [hbm] * nx,
        out_specs=[pl.BlockSpec((seq, LANES), lambda b, p: (b, p)),
                   pl.BlockSpec((1, 1, SUBLANES, seq), lambda b, p: (b, p, 0, 0))] + [hbm] * nx,
        out_shape=[jax.ShapeDtypeStruct((bsz * seq, D_MODEL), F32),
                   jax.ShapeDtypeStruct((bsz, HEAD_PAIRS, SUBLANES, seq), F32)] + _exchange_shapes("gather", gather),
        scratch_shapes=[head2, head2, pltpu.VMEM((LANES, seq), BF16)] + _exchange_scratch(nx),
        compiler_params=_params(("arbitrary", "arbitrary")),
    )(qkv, qkv, qkv, ccol, *gather)


def _att_bwd(qkv, dycat, o, lse, ccol, bsz, seq, scatter):
    blk = bq = min(ATT_BLOCK_BWD, seq)
    nb = nq = seq // blk
    nx = len(scatter)

    def body(*refs):
        q_ref, k_ref, v_ref, do_ref, o_ref, lse_ref, c_ref = refs[:7]
        x_refs = refs[7:7 + nx]
        dq_ref, dk_ref, dv_ref, dck_ref, dcq_ref = refs[7 + nx:12 + nx]
        xo_refs = refs[12 + nx:12 + 2 * nx]
        (qa_scr, ka_scr, va_scr, doa_scr, kat_scr, qat_scr, dot_scr, d_scr, dqt_scr, dkt_scr,
         dvt_scr) = refs[12 + 2 * nx:23 + 2 * nx]
        sems = refs[23 + 2 * nx:]
        pair = pl.program_id(1)

        @pl.when((pl.program_id(0) == 0) & (pair == 0))
        def _():
            _exchange_start("scatter", x_refs, xo_refs, *sems)

        query_minus_key = _iota((blk, bq), 1) - _iota((blk, bq), 0)
        lane_b = _iota((blk, LANES), 1)
        row_t = _iota((LANES, seq), 0)
        bases = []
        ones8 = jnp.ones((SUBLANES, LANES), F32)
        for hh in range(2):
            lm, base, qa, ka = _att_operands(q_ref, k_ref, c_ref, hh)
            bases.append(base)
            qa_scr[hh] = qa
            ka_scr[hh] = ka
            va_scr[hh] = jnp.where(lm, v_ref[...], jnp.zeros((seq, LANES), BF16))
            doa = jnp.where(lm, do_ref[...], 0.0).astype(BF16)
            doa_scr[hh] = doa
            for t0 in range(0, seq, blk):
                kat_scr[hh, :, t0:t0 + blk] = ka[t0:t0 + blk, :].astype(F32).T.astype(BF16)
                qat_scr[hh, :, t0:t0 + blk] = qa[t0:t0 + blk, :].astype(F32).T.astype(BF16)
            d1, d2, d3 = (_bdot_nt(ones8, term) for term in _split3(doa.astype(F32) * o_ref[...]))
            d_scr[hh] = d1 + d2 + d3
        for t0 in range(0, seq, blk):
            dot_scr[:, t0:t0 + blk] = do_ref[t0:t0 + blk, :].astype(BF16).astype(F32).T.astype(BF16)
        dqt_scr[...] = jnp.zeros_like(dqt_scr)
        dck_ref[...] = jnp.zeros_like(dck_ref)

        def kv_step(j, carry0):
            krows = pl.ds(pl.multiple_of(j * blk, blk), blk)
            dkt_scr[...] = jnp.zeros_like(dkt_scr)
            dvt_scr[...] = jnp.zeros_like(dvt_scr)

            def scores(i):
                rows = pl.ds(pl.multiple_of(i * bq, bq), bq)
                return tuple((_bdot_nt(ka_scr[hh, krows, :], qa_scr[hh, rows, :]),
                              _bdot_nt(va_scr[hh, krows, :], doa_scr[hh, rows, :])) for hh in range(2))

            def update(i, sd, masked):
                rows = pl.ds(pl.multiple_of(i * bq, bq), bq)
                for hh in range(2):
                    st, dpt = sd[hh]
                    if masked:
                        st = jnp.where(query_minus_key >= 0, st, NEG)
                    pt = jnp.exp(st - lse_ref[0, 0, hh:hh + 1, rows])
                    dst = (pt * (dpt - d_scr[hh, 0:1, rows])).astype(BF16)
                    dvt_scr[hh] += _bdot_nt(dot_scr[HEAD_DIM * hh:HEAD_DIM * (hh + 1), rows], pt)
                    dkt_scr[hh] += _bdot_nt(qat_scr[hh, :, rows], dst)
                    dqt_scr[hh, :, rows] += _bdot(kat_scr[hh, :, krows], dst)

            i_diag = j

            def q_pair(t, sd_cur):
                i = i_diag + 1 + 2 * t
                sd_second = scores(i + 1)
                sd_next = scores(jnp.minimum(i + 2, nq - 1))
                update(i, sd_cur, False)
                update(i + 1, sd_second, False)
                return sd_next

            sd_diag = scores(i_diag)
            sd_first = scores(jnp.minimum(i_diag + 1, nq - 1))
            update(i_diag, sd_diag, True)
            rest = nq - 1 - i_diag
            sd_last = lax.fori_loop(0, rest // 2, q_pair, sd_first)

            @pl.when(rest % 2 == 1)
            def _():
                update(jnp.int32(nq - 1), sd_last, False)
            dkt = jnp.where(_iota((LANES, blk), 0) < HEAD_DIM, dkt_scr[0], dkt_scr[1])
            dk_ref[krows, :] = dkt.T.astype(BF16)
            dv_ref[krows, :] = jnp.concatenate([dvt_scr[0], dvt_scr[1]], axis=0).T.astype(BF16)
            for hh in range(2):
                dck_ref[0, 0, hh:hh + 1, krows] = -dkt_scr[hh, bases[hh]:bases[hh] + 1, :]
            return carry0

        lax.fori_loop(0, nb, kv_step, 0)
        for t0 in range(0, seq, blk):
            dq0, dq1 = dqt_scr[0, :, t0:t0 + blk].T, dqt_scr[1, :, t0:t0 + blk].T
            dq_ref[t0:t0 + blk, :] = (jnp.where(lane_b < HEAD_DIM, dq0, dq1) * ATT_SCALE).astype(BF16)
        dcq_ref[...] = jnp.zeros_like(dcq_ref)
        for hh in range(2):
            dcq_ref[0, 0, hh:hh + 1, :] = jnp.sum(jnp.where(row_t == bases[hh] + AUX_ONE, dqt_scr[hh], 0.0),
                                                   axis=0, keepdims=True)

        @pl.when((pl.program_id(0) == bsz - 1) & (pair == HEAD_PAIRS - 1))
        def _():
            _exchange_wait("scatter", x_refs, xo_refs, *sems)

    col = lambda off: pl.BlockSpec((seq, LANES), lambda b, p: (b, off + p))
    rowvec = pl.BlockSpec((1, 1, SUBLANES, seq), lambda b, p: (b, p, 0, 0))
    out = jax.ShapeDtypeStruct((bsz * seq, D_MODEL), BF16)
    rows_shape = jax.ShapeDtypeStruct((bsz, HEAD_PAIRS, SUBLANES, seq), F32)
    head2 = pltpu.VMEM((2, seq, LANES), BF16)
    hbm = pl.BlockSpec(memory_space=pl.ANY)
    return pl.pallas_call(
        body, name="att_bwd", grid=(bsz, HEAD_PAIRS),
        in_specs=[col(0), col(HEAD_PAIRS), col(2 * HEAD_PAIRS), col(HEAD_PAIRS), col(0), rowvec, col(0)] + [hbm] * nx,
        out_specs=[col(0), col(0), col(0), rowvec, rowvec] + [hbm] * nx,
        out_shape=[out, out, out, rows_shape, rows_shape] + _exchange_shapes("scatter", scatter),
        scratch_shapes=[head2, head2, head2, head2, pltpu.VMEM((2, LANES, seq), BF16),
                        pltpu.VMEM((2, LANES, seq), BF16), pltpu.VMEM((LANES, seq), BF16),
                        pltpu.VMEM((2, SUBLANES, seq), F32), pltpu.VMEM((2, LANES, seq), F32),
                        pltpu.VMEM((2, LANES, blk), F32), pltpu.VMEM((2, HEAD_DIM, blk), F32)] + _exchange_scratch(nx),
        compiler_params=_params(("arbitrary", "arbitrary")),
    )(qkv, qkv, qkv, dycat, o, lse, ccol, *scatter)


def _adamw_math(w, g, m, v):
    m = ADAM_B1 * m + (1.0 - ADAM_B1) * g
    v = ADAM_B2 * v + (1.0 - ADAM_B2) * jnp.square(g)
    m_hat = m / ADAM_C1
    v_hat = v / ADAM_C2
    delta = -ADAM_LR * (m_hat / (jnp.sqrt(v_hat) + ADAM_EPS) + ADAM_WD * w)
    return delta, m, v


def _row_tile(rows):
    for tr in (256, 128, 64, 32, 16, 8):
        if rows % tr == 0:
            return tr
    return rows


def _sum_parts(parts, name):
    nparts, rows, cols = parts.shape
    tr = rows if rows % SUBLANES else _row_tile(rows)

    def body(p_ref, o_ref):
        g = p_ref[0].astype(F32)
        for s in range(1, nparts):
            g = g + p_ref[s].astype(F32)
        o_ref[...] = g

    return pl.pallas_call(
        body, name=name, grid=(rows // tr,),
        in_specs=[pl.BlockSpec((nparts, tr, cols), lambda i: (0, i, 0))],
        out_specs=pl.BlockSpec((tr, cols), lambda i: (i, 0)),
        out_shape=jax.ShapeDtypeStruct((rows, cols), F32),
        compiler_params=_params(("parallel",)),
    )(parts)


def _adamw_parts(parts, w, m, v, name):
    nparts, rows, cols = parts.shape
    tr = _row_tile(rows)

    def body(p_ref, w_ref, m_ref, v_ref, g_ref, d_ref, mo_ref, vo_ref):
        g = p_ref[0].astype(F32)
        for s in range(1, nparts):
            g = g + p_ref[s].astype(F32)
        delta, mn, vn = _adamw_math(w_ref[...], g, m_ref[...], v_ref[...])
        g_ref[...] = g
        d_ref[...] = delta
        mo_ref[...] = mn
        vo_ref[...] = vn

    blk = pl.BlockSpec((tr, cols), lambda i: (i, 0))
    shp = jax.ShapeDtypeStruct((rows, cols), F32)
    return pl.pallas_call(
        body, name=name, grid=(rows // tr,),
        in_specs=[pl.BlockSpec((nparts, tr, cols), lambda i: (0, i, 0)), blk, blk, blk],
        out_specs=[blk, blk, blk, blk], out_shape=[shp, shp, shp, shp],
        compiler_params=_params(("parallel",)),
    )(parts, w, m, v)


def _me():
    return lax.axis_index("x"), lax.axis_index("y"), lax.axis_index("c")


def _flip(pos, k):
    x, y, c = pos
    kx, ky, kc = (k >> 2) & 1, (k >> 1) & 1, k & 1
    return (x ^ kx if kx else x, y ^ ky if ky else y, c ^ kc if kc else c)


def _logical(pos):
    return 4 * pos[0] + 2 * pos[1] + pos[2]


def _all_gather_two_level(shards):
    narr = len(shards)

    def body(*refs):
        x_refs, out_refs = refs[:narr], refs[narr:2 * narr]
        send_sems, recv_sems, local_sems = refs[2 * narr:]
        x, y, c = _me()
        me, sibling = (x, y, c), (x, y, 1 - c)
        chips = [(1 - x, y), (x, 1 - y), (1 - x, 1 - y)]

        def copy(a, k, block, to, src=None):
            slot = out_refs[a].at[_logical(block)]
            return pltpu.make_async_remote_copy(
                src_ref=slot if src is None else src, dst_ref=slot,
                send_sem=send_sems.at[a, k], recv_sem=recv_sems.at[a, k], device_id=to, device_id_type=MESH)

        mine = [pltpu.make_async_copy(x_refs[a], out_refs[a].at[_logical(me)], local_sems.at[a]) for a in range(narr)]
        for cp in mine:
            cp.start()
        first = []
        for a in range(narr):
            first.append(copy(a, 0, me, sibling, src=x_refs[a]))
            first += [copy(a, 1 + j, me, (*chip, c), src=x_refs[a]) for j, chip in enumerate(chips)]
        for cp in first:
            cp.start()
        passed = []
        for a in range(narr):
            for j, chip in enumerate(chips):
                copy(a, 1 + j, (*chip, c), me).wait_recv()
                fwd = copy(a, 4 + j, (*chip, c), sibling)
                fwd.start()
                passed.append(fwd)
        for a in range(narr):
            copy(a, 0, sibling, me).wait_recv()
            for j, chip in enumerate(chips):
                copy(a, 4 + j, (*chip, 1 - c), me).wait_recv()
        for cp in first + passed:
            cp.wait_send()
        for cp in mine:
            cp.wait()

    hbm = pl.BlockSpec(memory_space=pl.ANY)
    return pl.pallas_call(
        body, name="all_gather_big",
        out_shape=[jax.ShapeDtypeStruct((N_DEV,) + s.shape, s.dtype) for s in shards],
        in_specs=[hbm] * narr, out_specs=[hbm] * narr,
        scratch_shapes=[pltpu.SemaphoreType.DMA((narr, 7)), pltpu.SemaphoreType.DMA((narr, 7)),
                        pltpu.SemaphoreType.DMA((narr,))],
    )(*shards)


def _exchange_copies(kind, x_refs, out_refs, send_sems, recv_sems, local_sems):
    me = _me()
    mine = _logical(me)
    local, remote = [], []
    for a, (x_ref, out_ref) in enumerate(zip(x_refs, out_refs)):
        own = x_ref if kind == "gather" else x_ref.at[mine]
        local.append(pltpu.make_async_copy(own, out_ref.at[mine], local_sems.at[a]))
        for k in range(1, N_DEV):
            peer = _flip(me, k)
            src = x_ref if kind == "gather" else x_ref.at[_logical(peer)]
            remote.append(pltpu.make_async_remote_copy(
                src_ref=src, dst_ref=out_ref.at[mine], send_sem=send_sems.at[a, k - 1], recv_sem=recv_sems.at[a, k - 1],
                device_id=peer, device_id_type=MESH))
    return local, remote


def _exchange_start(kind, x_refs, out_refs, *sems):
    if not x_refs:
        return
    local, remote = _exchange_copies(kind, x_refs, out_refs, *sems)
    for cp in local + remote:
        cp.start()


def _exchange_wait(kind, x_refs, out_refs, *sems):
    if not x_refs:
        return
    local, remote = _exchange_copies(kind, x_refs, out_refs, *sems)
    for cp in remote:
        cp.wait_recv()
    for cp in remote:
        cp.wait_send()
    for cp in local:
        cp.wait()


def _exchange_shapes(kind, arrays):
    return [jax.ShapeDtypeStruct(((N_DEV,) if kind == "gather" else ()) + a.shape, a.dtype) for a in arrays]


def _exchange_scratch(narrays):
    if not narrays:
        return []
    return [pltpu.SemaphoreType.DMA((narrays, N_DEV - 1)), pltpu.SemaphoreType.DMA((narrays, N_DEV - 1)),
            pltpu.SemaphoreType.DMA((narrays,))]


def _exchange_rows(x_ref, buf_ref, send_sems, recv_sems):
    me = _me()
    buf_ref[_logical(me)] = x_ref[...]
    copies = []
    for k in range(1, N_DEV):
        peer = _flip(me, k)
        copies.append(pltpu.make_async_remote_copy(
            src_ref=x_ref, dst_ref=buf_ref.at[_logical(me)],
            send_sem=send_sems.at[k - 1], recv_sem=recv_sems.at[k - 1], device_id=peer, device_id_type=MESH))
    for cp in copies:
        cp.start()
    for cp in copies:
        cp.wait_recv()
    for cp in copies:
        cp.wait_send()


def _sum_slots(buf_ref):
    total = buf_ref[0]
    for s in range(1, N_DEV):
        total = total + buf_ref[s]
    return total


def _small_gather(x):
    def body(x_ref, o_ref, buf_ref, send_sems, recv_sems):
        _exchange_rows(x_ref, buf_ref, send_sems, recv_sems)
        o_ref[...] = _sum_slots(buf_ref)

    return pl.pallas_call(
        body, name="small_gather", out_shape=jax.ShapeDtypeStruct(x.shape, F32),
        in_specs=[pl.BlockSpec(memory_space=pltpu.VMEM)], out_specs=pl.BlockSpec(memory_space=pltpu.VMEM),
        scratch_shapes=[pltpu.VMEM((N_DEV,) + x.shape, F32), pltpu.SemaphoreType.DMA((7,)), pltpu.SemaphoreType.DMA((7,))],
    )(x)


def _small_reduce_adamw(g, w, m, v):
    def body(g_ref, w_ref, m_ref, v_ref, go_ref, d_ref, mo_ref, vo_ref, buf_ref, send_sems, recv_sems):
        _exchange_rows(g_ref, buf_ref, send_sems, recv_sems)
        gs = _sum_slots(buf_ref)
        delta, mn, vn = _adamw_math(w_ref[...], gs, m_ref[...], v_ref[...])
        go_ref[...] = gs
        d_ref[...] = delta
        mo_ref[...] = mn
        vo_ref[...] = vn

    vm = pl.BlockSpec(memory_space=pltpu.VMEM)
    shp = jax.ShapeDtypeStruct(g.shape, F32)
    return pl.pallas_call(
        body, name="small_reduce_adamw", out_shape=[shp, shp, shp, shp],
        in_specs=[vm, vm, vm, vm], out_specs=[vm, vm, vm, vm],
        scratch_shapes=[pltpu.VMEM((N_DEV,) + g.shape, F32), pltpu.SemaphoreType.DMA((7,)), pltpu.SemaphoreType.DMA((7,))],
    )(g, w, m, v)


def _pad_cols(a, width):
    return jnp.pad(a, ((0, 0), (0, width - a.shape[1])))


def _local_step(x, target, norm_mix_w, w_in_t, conv_w, conv_b, dt_bias, a_log, d_skip, ssd_norm_w, f_bias,
                late_shards, norm_mlp_w, norm_final_w):
    bsz, seq, _ = x.shape
    n = bsz * seq
    x2 = x.reshape(n, D_MODEL)
    t2 = target.reshape(n, D_MODEL)
    nfw = norm_final_w.reshape(1, D_MODEL)

    bias = _pad_cols(dt_bias, LANES)
    arow = _pad_cols(-jnp.exp(a_log), LANES)
    biast, acol = bias.reshape(LANES, 1), arow.reshape(LANES, 1)
    dfull = jnp.repeat(d_skip, HEAD_DIM, axis=1)
    fb = jnp.pad(f_bias, ((0, 0), (F_COL, LANES - 2 * F_COL)))

    wt_z, wt_xbc, wt_qkv = w_in_t[:ROW_XBC], w_in_t[ROW_XBC:ROW_DT], w_in_t[ROW_Q:ROW_F]
    wt_dtf = jnp.concatenate([w_in_t[ROW_DT:ROW_Q], w_in_t[ROW_F:], jnp.zeros((LANES - 2 * F_COL, D_MODEL), BF16)], axis=0)
    wt_q, wt_k, wt_v = wt_qkv[:D_MODEL], wt_qkv[D_MODEL:2 * D_MODEL], wt_qkv[2 * D_MODEL:]

    h1 = _rms_fwd(x2, norm_mix_w, "rms_fwd_mix")
    z = _mm([(h1, wt_z)], "inproj_z", F32, 1024, 1024, nt=True)
    xbc_raw = _mm([(h1, wt_xbc)], "inproj_xbc", F32, 512, 1536, nt=True)
    qkv = _mm([(h1, wt_qkv)], "inproj_qkv", BF16, 512, 3072, nt=True)
    dtf = _mm([(h1, wt_dtf)], "inproj_dtf", F32, 2048, LANES, nt=True)
    dtft = dtf.reshape(bsz, seq, LANES).transpose(0, 2, 1)

    xbc = _conv_fwd(xbc_raw, conv_w, conv_b, bsz, seq)
    y_pre, hall = _ssd_fwd(xbc, dtf, dtft, bias, biast, arow, acol, dfull, bsz, seq)

    ccol = _fox_prep(dtf, fb, bsz, seq)
    o_att, lse, w_out, w_up_t, w_down = _att_fwd(qkv, ccol, bsz, seq, late_shards)
    w_out, w_up_t, w_down = (w.reshape(-1, D_MODEL) for w in (w_out, w_up_t, w_down))

    ycat = _gnorm_fwd(y_pre, z, o_att, ssd_norm_w)
    h2, h2n = _mm([(ycat, w_out)], "outproj", F32, 512, 1024, res=x2, rms_fwd=norm_mlp_w)
    pre = _mm([(h2n, w_up_t)], "mlp_up", BF16, 512, 4096, nt=True)

    dh3, loss_row, g_nfw = _mlp_down_loss(pre, w_down, h2, t2, nfw)
    dpre, u = _mlp_down_bwd(dh3, w_down, pre)
    g_w_down = _mm_tn(u, dh3, "grad_w_down", 1024, 1024, 2048)
    g_w_up_t = _mm_tn(dpre, h2n, "grad_w_up", 1024, 1024, 2048)
    by_shard = lambda g: g.reshape(N_DEV, g.shape[0] // N_DEV, D_MODEL)
    dh2, g_nmlp = _mm([(dpre, w_up_t)], "mlp_up_bwd", F32, 512, 1024, res=dh3, rms_bwd=(h2, norm_mlp_w))

    g_w_out = _mm_tn(ycat, dh2, "grad_w_out", 1024, 1024, 2048)
    dycat = _mm([(dh2, w_out)], "outproj_bwd", F32, 512, 2048, nt=True)
    dy_pre, dz, g_ssdn = _gnorm_bwd(dycat, y_pre, z, ssd_norm_w)
    dq, dk, dv, dck, dcq, recv_down, recv_up_t, recv_out = _att_bwd(
        qkv, dycat, o_att, lse, ccol, bsz, seq, [by_shard(g_w_down), by_shard(g_w_up_t), by_shard(g_w_out)])

    dxbc, ddt, ssd_acc = _ssd_bwd(dy_pre, xbc, dtf, dtft, bias, biast, arow, acol, dfull, hall, bsz, seq)
    dxbc_raw, g_cw, g_cb = _conv_bwd(dxbc, xbc_raw, conv_w, conv_b, bsz, seq)

    def head_cols(rows):
        cols = rows[:, :, :2, :].reshape(bsz, SSD_HEADS, seq).transpose(0, 2, 1).reshape(n, SSD_HEADS)
        return jnp.pad(cols, ((0, 0), (F_COL, LANES - 2 * F_COL)))

    ddtf, g_fb = _fox_prep_bwd(head_cols(dck), head_cols(dcq), ddt, dtf, fb, bsz, seq)

    g_dtf = _mm_tn(ddtf, h1, "grad_w_in_dtf", LANES, 1024, 2048)
    g_w_in_t = jnp.concatenate([
        _mm_tn(dz, h1, "grad_w_in_z", 1024, 1024, 2048), _mm_tn(dxbc_raw, h1, "grad_w_in_xbc", 768, 1024, 2048),
        g_dtf[:F_COL], _mm_tn(dq, h1, "grad_w_in_q", 1024, 1024, 2048), _mm_tn(dk, h1, "grad_w_in_k", 1024, 1024, 2048),
        _mm_tn(dv, h1, "grad_w_in_v", 1024, 1024, 2048), g_dtf[F_COL:2 * F_COL]], axis=0)
    pieces = [(dz, wt_z), (dxbc_raw, wt_xbc), (dq, wt_q), (dk, wt_k), (dv, wt_v), (ddtf, wt_dtf)]
    dx, g_nmix, recv_in_t = _mm(pieces, "inproj_bwd", F32, 256, 1024, res=dh2, rms_bwd=(x2, norm_mix_w),
                                exchange=("scatter", [by_shard(g_w_in_t)]))

    small = dict(
        norm_mix_w=g_nmix, norm_mlp_w=g_nmlp, norm_final_w=g_nfw, ssd_norm_w=g_ssdn, conv_b=g_cb, conv_w=g_cw,
        dt_bias=ssd_acc[16:17, :SSD_HEADS], a_log=ssd_acc[0:1, :SSD_HEADS], d_skip=ssd_acc[8:9, :SSD_HEADS],
        f_bias=g_fb[0:1, F_COL:2 * F_COL])
    recv = dict(w_in_t=recv_in_t, w_out=recv_out, w_up_t=recv_up_t, w_down=recv_down)
    return loss_row[0, 0], dx.reshape(bsz, seq, D_MODEL), small, recv


SMALL_LAYOUT = (("norm_mix_w", 0, 1, 0, D_MODEL), ("norm_mlp_w", 1, 1, 0, D_MODEL), ("norm_final_w", 2, 1, 0, D_MODEL),
                ("ssd_norm_w", 3, 1, 0, D_MODEL), ("conv_b", 4, 1, 0, CONV_CH), ("conv_w", 5, CONV_K, 0, CONV_CH),
                ("dt_bias", 9, 1, 0, SSD_HEADS), ("a_log", 9, 1, LANES, SSD_HEADS), ("d_skip", 9, 1, 2 * LANES, SSD_HEADS),
                ("f_bias", 9, 1, 3 * LANES, SSD_HEADS))
SMALL_ROWS = 2 * SUBLANES


def _pack_small(vals):
    packed = jnp.zeros((SMALL_ROWS, SMALL_COLS), F32)
    for name, r0, nrows, c0, width in SMALL_LAYOUT:
        packed = packed.at[r0:r0 + nrows, c0:c0 + width].set(vals[name].reshape(nrows, width))
    return packed


def _unpack_small(packed, like):
    out = {}
    for name, r0, nrows, c0, width in SMALL_LAYOUT:
        out[name] = packed[r0:r0 + nrows, c0:c0 + width].reshape(like[name].shape)
    return out


def kernel(x, norm_mix_w, w_in, conv_w, conv_b, dt_bias, a_log, d_skip, ssd_norm_w, f_bias, w_out, norm_mlp_w, w_up, w_down, norm_final_w, loss_target, m_norm_mix_w, m_w_in, m_conv_w, m_conv_b, m_dt_bias, m_a_log, m_d_skip, m_ssd_norm_w, m_f_bias, m_w_out, m_norm_mlp_w, m_w_up, m_w_down, m_norm_final_w, v_norm_mix_w, v_w_in, v_conv_w, v_conv_b, v_dt_bias, v_a_log, v_d_skip, v_ssd_norm_w, v_f_bias, v_w_out, v_norm_mlp_w, v_w_up, v_w_down, v_norm_final_w):
    me = 4 * lax.axis_index("x") + 2 * lax.axis_index("y") + lax.axis_index("c")
    conv_shard_w = CONV_CH // N_DEV
    zero = jnp.zeros((), jnp.int32)

    def place_conv(shard):
        return lax.dynamic_update_slice(jnp.zeros((CONV_K, CONV_CH), F32), shard[0], (zero, me * conv_shard_w))

    (g_in_t,) = _all_gather_two_level([w_in[0].T.astype(BF16)])
    late_shards = [w_out[0].astype(BF16), w_up[0].T.astype(BF16), w_down[0].astype(BF16)]
    conv_full = _small_gather(jnp.pad(place_conv(conv_w), ((0, SUBLANES - CONV_K), (0, 0))))[:CONV_K]

    loss_local, grad_x, g_small, recv = _local_step(
        x, loss_target, norm_mix_w, g_in_t.reshape(IN_WIDTH, D_MODEL), conv_full, conv_b, dt_bias, a_log, d_skip,
        ssd_norm_w, f_bias, late_shards, norm_mlp_w, norm_final_w)
    loss = lax.psum(loss_local, MESH_AXES)

    grad_in = _sum_parts(recv["w_in_t"], "sum_w_in").T[None]
    grad_up = _sum_parts(recv["w_up_t"], "sum_w_up").T[None]
    big = {
        "w_in": _adamw_parts(grad_in, w_in[0], m_w_in[0], v_w_in[0], "adamw_w_in"),
        "w_out": _adamw_parts(recv["w_out"], w_out[0], m_w_out[0], v_w_out[0], "adamw_w_out"),
        "w_up": _adamw_parts(grad_up, w_up[0], m_w_up[0], v_w_up[0], "adamw_w_up"),
        "w_down": _adamw_parts(recv["w_down"], w_down[0], m_w_down[0], v_w_down[0], "adamw_w_down"),
    }

    like = dict(norm_mix_w=norm_mix_w, norm_mlp_w=norm_mlp_w, norm_final_w=norm_final_w, ssd_norm_w=ssd_norm_w,
                conv_b=conv_b, conv_w=jnp.zeros((1, CONV_K, CONV_CH), F32), dt_bias=dt_bias, a_log=a_log,
                d_skip=d_skip, f_bias=f_bias)
    w_small = dict(like, conv_w=place_conv(conv_w))
    m_small = dict(norm_mix_w=m_norm_mix_w, norm_mlp_w=m_norm_mlp_w, norm_final_w=m_norm_final_w,
                   ssd_norm_w=m_ssd_norm_w, conv_b=m_conv_b, conv_w=place_conv(m_conv_w), dt_bias=m_dt_bias,
                   a_log=m_a_log, d_skip=m_d_skip, f_bias=m_f_bias)
    v_small = dict(norm_mix_w=v_norm_mix_w, norm_mlp_w=v_norm_mlp_w, norm_final_w=v_norm_final_w,
                   ssd_norm_w=v_ssd_norm_w, conv_b=v_conv_b, conv_w=place_conv(v_conv_w), dt_bias=v_dt_bias,
                   a_log=v_a_log, d_skip=v_d_skip, f_bias=v_f_bias)
    small = _small_reduce_adamw(_pack_small(g_small), _pack_small(w_small), _pack_small(m_small), _pack_small(v_small))
    small = [_unpack_small(s, like) for s in small]
    for s in small:
        s["conv_w"] = lax.dynamic_slice(s["conv_w"], (zero, zero, me * conv_shard_w), (1, CONV_K, conv_shard_w))

    order = ["norm_mix_w", "w_in", "conv_w", "conv_b", "dt_bias", "a_log", "d_skip", "ssd_norm_w", "f_bias", "w_out",
             "norm_mlp_w", "w_up", "w_down", "norm_final_w"]
    outs = [loss, grad_x]
    for kind in range(4):
        for name in order:
            outs.append(big[name][kind][None] if name in big else small[kind][name])
    return tuple(outs)
```

```python
import jax
import jax.numpy as jnp
from jax import lax
from jax.experimental import pallas as pl
from jax.experimental.pallas import tpu as pltpu

F32, BF16 = jnp.float32, jnp.bfloat16

D_MODEL = 1024
SSD_HEADS = 16
HEAD_DIM = 64
SSD_STATE = 128
CHUNK = 128
CONV_CH = 1536
CONV_K = 4
D_FF = 4096
MIX_WIDTH = 2048
IN_WIDTH = 5664
NORM_EPS = 1e-5
ATT_SCALE = 0.125

LANES = 128
SUBLANES = 8
HEAD_PAIRS = SSD_HEADS // 2
NEG = -1e30
VMEM_LIMIT = 52 * 1024 * 1024

ROW_XBC, ROW_DT, ROW_Q, ROW_F = 1024, 2560, 2576, 5648
F_COL = SSD_HEADS

N_DEV = 8
MESH_AXES = ("x", "y", "c")
MESH = pl.DeviceIdType.MESH

ADAM_LR, ADAM_B1, ADAM_B2, ADAM_EPS, ADAM_WD, ADAM_STEP = 0.001, 0.9, 0.999, 1e-08, 0.01, 10
ADAM_C1 = 1.0 - ADAM_B1 ** ADAM_STEP
ADAM_C2 = 1.0 - ADAM_B2 ** ADAM_STEP

SMALL_COLS = CONV_CH


def _params(sem=None):
    return pltpu.CompilerParams(dimension_semantics=sem, vmem_limit_bytes=VMEM_LIMIT)


def _sigmoid(u):
    return 1.0 / (1.0 + jnp.exp(-u))


def _softplus(u):
    return jnp.maximum(u, 0.0) + jnp.log(1.0 + jnp.exp(-jnp.abs(u)))


def _log_sigmoid(u):
    return jnp.minimum(u, 0.0) - jnp.log(1.0 + jnp.exp(-jnp.abs(u)))


def _bdot(a, b):
    return jnp.dot(a.astype(BF16), b.astype(BF16), preferred_element_type=F32)


def _bdot_nt(a, b):
    return lax.dot_general(a.astype(BF16), b.astype(BF16), (((1,), (1,)), ((), ())), preferred_element_type=F32)


def _bdot_tn(a, b):
    return lax.dot_general(a.astype(BF16), b.astype(BF16), (((0,), (0,)), ((), ())), preferred_element_type=F32)


def _split3(x):
    x1 = x.astype(BF16)
    r1 = x - x1.astype(F32)
    x2 = r1.astype(BF16)
    return x1, x2, (r1 - x2.astype(F32)).astype(BF16)


def _dot_sel(x, sel):
    s = sel.astype(BF16)
    p1, p2, p3 = (jnp.dot(p, s, preferred_element_type=F32) for p in _split3(x))
    return p1 + p2 + p3


def _sel_dot(sel, x):
    s = sel.astype(BF16)
    p1, p2, p3 = (jnp.dot(s, p, preferred_element_type=F32) for p in _split3(x))
    return p1 + p2 + p3


def _iota(shape, dim):
    return lax.broadcasted_iota(jnp.int32, shape, dim)


def _head_expand():
    return (jnp.right_shift(_iota((LANES, D_MODEL), 1), 6) == _iota((LANES, D_MODEL), 0)).astype(F32)


def _head_reduce():
    return (jnp.right_shift(_iota((D_MODEL, LANES), 0), 6) == _iota((D_MODEL, LANES), 1)).astype(F32)


def _rms_fwd(x, w, name):
    n, d = x.shape
    tm = min(1024, n)

    def body(x_ref, w_ref, o_ref):
        xv = x_ref[...]
        r = lax.rsqrt(jnp.mean(xv * xv, axis=-1, keepdims=True) + NORM_EPS)
        o_ref[...] = (xv * r * w_ref[...]).astype(BF16)

    return pl.pallas_call(
        body, name=name, grid=(n // tm,),
        in_specs=[pl.BlockSpec((tm, d), lambda i: (i, 0)), pl.BlockSpec((1, d), lambda i: (0, 0))],
        out_specs=pl.BlockSpec((tm, d), lambda i: (i, 0)),
        out_shape=jax.ShapeDtypeStruct((n, d), BF16),
        compiler_params=_params(("parallel",)),
    )(x, w)


def _mm(pairs, name, out_dtype, tm, tn, nt=False, res=None, exchange=None, rms_fwd=None, rms_bwd=None):
    m = pairs[0][0].shape[0]
    n = pairs[0][1].shape[0 if nt else 1]
    tm, tn = min(tm, m), min(tn, n)
    gm, gn = m // tm, n // tn
    npairs = len(pairs)
    kind, xs = (None, []) if exchange is None else exchange
    nx = len(xs)
    extra_in = [] if res is None else [res]
    if rms_bwd is not None:
        extra_in += list(rms_bwd)
    elif rms_fwd is not None:
        extra_in.append(rms_fwd)
    n_in = 2 * npairs + len(extra_in)
    n_out = 1 + (rms_fwd is not None or rms_bwd is not None)
    assert (rms_fwd is None and rms_bwd is None) or tn == n

    def body(*refs):
        x_refs, o_refs = refs[n_in:n_in + nx], refs[n_in + nx:n_in + nx + n_out]
        xo_refs, sems = refs[n_in + nx + n_out:n_in + 2 * nx + n_out], refs[n_in + 2 * nx + n_out:]
        extra = refs[2 * npairs:n_in]
        i, j = pl.program_id(0), pl.program_id(1)
        if nx:
            @pl.when((i == 0) & (j == 0))
            def _():
                _exchange_start(kind, x_refs, xo_refs, *sems)

        acc = None
        for p in range(npairs):
            a_v, b_v = refs[2 * p][...], refs[2 * p + 1][...]
            d = _bdot_nt(a_v, b_v) if nt else _bdot(a_v, b_v)
            acc = d if acc is None else acc + d
        if rms_bwd is not None:
            xv = extra[1][...]
            r = lax.rsqrt(jnp.mean(xv * xv, axis=-1, keepdims=True) + NORM_EPS)
            nrm = xv * r
            g = acc * extra[2][...]
            o_refs[0][...] = extra[0][...] + r * (g - nrm * jnp.mean(g * nrm, axis=-1, keepdims=True))

            @pl.when(i == 0)
            def _():
                o_refs[1][...] = jnp.zeros_like(o_refs[1])

            o_refs[1][...] += jnp.sum(acc * nrm, axis=0, keepdims=True)
        else:
            if res is not None:
                acc = extra[0][...] + acc
            o_refs[0][...] = acc.astype(o_refs[0].dtype)
            if rms_fwd is not None:
                r = lax.rsqrt(jnp.mean(acc * acc, axis=-1, keepdims=True) + NORM_EPS)
                o_refs[1][...] = (acc * r * extra[-1][...]).astype(BF16)
        if nx:
            @pl.when((i == gm - 1) & (j == gn - 1))
            def _():
                _exchange_wait(kind, x_refs, xo_refs, *sems)

    tile = pl.BlockSpec((tm, tn), lambda i, j: (i, j))
    row = pl.BlockSpec((1, tn), lambda i, j: (0, j))
    in_specs, args = [], []
    for a, b in pairs:
        k = a.shape[1]
        in_specs.append(pl.BlockSpec((tm, k), lambda i, j: (i, 0)))
        in_specs.append(pl.BlockSpec((tn, k), lambda i, j: (j, 0)) if nt else pl.BlockSpec((k, tn), lambda i, j: (0, j)))
        args += [a, b]
    in_specs += [row if e.shape[0] == 1 else tile for e in extra_in]
    out_specs, out_shape = [tile], [jax.ShapeDtypeStruct((m, n), out_dtype)]
    if rms_bwd is not None:
        out_specs.append(row)
        out_shape.append(jax.ShapeDtypeStruct((1, n), F32))
    elif rms_fwd is not None:
        out_specs.append(tile)
        out_shape.append(jax.ShapeDtypeStruct((m, n), BF16))
    hbm = pl.BlockSpec(memory_space=pl.ANY)
    sequential = nx or rms_bwd is not None
    outs = pl.pallas_call(
        body, name=name, grid=(gm, gn), in_specs=in_specs + [hbm] * nx,
        out_specs=out_specs + [hbm] * nx,
        out_shape=out_shape + _exchange_shapes(kind, xs),
        scratch_shapes=_exchange_scratch(nx),
        compiler_params=_params(("arbitrary", "arbitrary") if sequential else ("parallel", "parallel")),
    )(*args, *extra_in, *xs)
    return outs if len(outs) > 1 else outs[0]


def _mm_tn(a, b, name, tm, tn, tk):
    t, m = a.shape
    n = b.shape[1]
    tm, tn, tk = min(tm, m), min(tn, n), min(tk, t)
    nk = t // tk

    def body(a_ref, b_ref, o_ref, acc_ref):
        kk = pl.program_id(2)

        @pl.when(kk == 0)
        def _():
            acc_ref[...] = jnp.zeros_like(acc_ref)

        acc_ref[...] += _bdot_tn(a_ref[...], b_ref[...])

        @pl.when(kk == nk - 1)
        def _():
            o_ref[...] = acc_ref[...].astype(o_ref.dtype)

    return pl.pallas_call(
        body, name=name, grid=(m // tm, n // tn, nk),
        in_specs=[pl.BlockSpec((tk, tm), lambda i, j, kk: (kk, i)), pl.BlockSpec((tk, tn), lambda i, j, kk: (kk, j))],
        out_specs=pl.BlockSpec((tm, tn), lambda i, j, kk: (i, j)),
        out_shape=jax.ShapeDtypeStruct((m, n), BF16),
        scratch_shapes=[pltpu.VMEM((tm, tn), F32)],
        compiler_params=_params(("parallel", "parallel", "arbitrary")),
    )(a, b)


def _mlp_down_loss(pre, w_down, h2, target, w):
    m, k = pre.shape
    d = w_down.shape[1]
    tm = min(512, m)

    def body(p_ref, b_ref, r_ref, t_ref, w_ref, dh_ref, loss_ref, gw_ref):
        u = jnp.square(jnp.maximum(p_ref[...].astype(F32), 0.0))
        xv = r_ref[...] + _bdot(u, b_ref[...])
        r = lax.rsqrt(jnp.mean(xv * xv, axis=-1, keepdims=True) + NORM_EPS)
        nrm = xv * r
        wv = w_ref[...]
        err = nrm * wv - t_ref[...]
        part = 0.5 * jnp.sum(jnp.mean(err * err, axis=-1, keepdims=True), axis=0, keepdims=True)
        dy = err * (1.0 / d)
        g = dy * wv
        dh_ref[...] = r * (g - nrm * jnp.mean(g * nrm, axis=-1, keepdims=True))

        @pl.when(pl.program_id(0) == 0)
        def _():
            loss_ref[...] = jnp.zeros_like(loss_ref)
            gw_ref[...] = jnp.zeros_like(gw_ref)

        loss_ref[...] += jnp.broadcast_to(part, loss_ref.shape)
        gw_ref[...] += jnp.sum(dy * nrm, axis=0, keepdims=True)

    tok = pl.BlockSpec((tm, d), lambda i: (i, 0))
    row = pl.BlockSpec((1, d), lambda i: (0, 0))
    return pl.pallas_call(
        body, name="mlp_down_loss", grid=(m // tm,),
        in_specs=[pl.BlockSpec((tm, k), lambda i: (i, 0)), pl.BlockSpec((k, d), lambda i: (0, 0)), tok, tok, row],
        out_specs=[tok, pl.BlockSpec((1, LANES), lambda i: (0, 0)), row],
        out_shape=[jax.ShapeDtypeStruct((m, d), F32), jax.ShapeDtypeStruct((1, LANES), F32),
                   jax.ShapeDtypeStruct((1, d), F32)],
        compiler_params=_params(("arbitrary",)),
    )(pre, w_down, h2, target, w)


def _mlp_down_bwd(dh3, w_down, pre):
    m, k = dh3.shape
    n = w_down.shape[0]
    tm, tn = min(256, m), n

    def body(a_ref, b_ref, p_ref, dpre_ref, u_ref):
        r = jnp.maximum(p_ref[...].astype(F32), 0.0)
        du = _bdot_nt(a_ref[...], b_ref[...])
        dpre_ref[...] = (du * (2.0 * r)).astype(BF16)
        u_ref[...] = (r * r).astype(BF16)

    blk = pl.BlockSpec((tm, tn), lambda i, j: (i, j))
    return pl.pallas_call(
        body, name="mlp_down_bwd", grid=(m // tm, n // tn),
        in_specs=[pl.BlockSpec((tm, k), lambda i, j: (i, 0)), pl.BlockSpec((tn, k), lambda i, j: (j, 0)), blk],
        out_specs=[blk, blk],
        out_shape=[jax.ShapeDtypeStruct((m, n), BF16), jax.ShapeDtypeStruct((m, n), BF16)],
        compiler_params=_params(("parallel", "parallel")),
    )(dh3, w_down, pre)


CONV_TC = 512


def _conv_fwd(xbc, cw, cb, bsz, seq):
    tt = min(512, seq)
    nt, hb = seq // tt, tt // SUBLANES
    x3 = xbc.reshape(bsz, seq, CONV_CH)

    def body(xm_ref, xh_ref, w_ref, b_ref, o_ref):
        i = pl.program_id(2)
        x = xm_ref[0]
        halo = jnp.where(i > 0, xh_ref[0], 0.0)
        xx = jnp.concatenate([halo, x], axis=0)
        acc = x * w_ref[3:4, :] + b_ref[...]
        for s in range(1, CONV_K):
            acc = acc + pltpu.roll(xx, s, 0)[SUBLANES:, :] * w_ref[3 - s:4 - s, :]
        o_ref[0] = acc * _sigmoid(acc)

    out = pl.pallas_call(
        body, name="conv_fwd", grid=(CONV_CH // CONV_TC, bsz, nt),
        in_specs=[pl.BlockSpec((1, tt, CONV_TC), lambda c, b, i: (b, i, c)),
                  pl.BlockSpec((1, SUBLANES, CONV_TC), lambda c, b, i: (b, jnp.maximum(i * hb - 1, 0), c)),
                  pl.BlockSpec((CONV_K, CONV_TC), lambda c, b, i: (0, c)),
                  pl.BlockSpec((1, CONV_TC), lambda c, b, i: (0, c))],
        out_specs=pl.BlockSpec((1, tt, CONV_TC), lambda c, b, i: (b, i, c)),
        out_shape=jax.ShapeDtypeStruct((bsz, seq, CONV_CH), F32),
        compiler_params=_params(("parallel", "parallel", "parallel")),
    )(x3, x3, cw, cb)
    return out.reshape(bsz * seq, CONV_CH)


def _conv_bwd(da, xbc, cw, cb, bsz, seq):
    tt = min(512, seq)
    nt, hb = seq // tt, tt // SUBLANES
    x3 = xbc.reshape(bsz, seq, CONV_CH)
    da3 = da.reshape(bsz, seq, CONV_CH)
    n2 = tt + SUBLANES

    def body(dam_ref, daa_ref, xm_ref, xb_ref, xa_ref, w_ref, b_ref, du_ref, gw_ref, gb_ref):
        bi, i = pl.program_id(1), pl.program_id(2)
        before = jnp.where(i > 0, xb_ref[0], 0.0)
        after = jnp.where(i < nt - 1, xa_ref[0], 0.0)
        xx = jnp.concatenate([before, xm_ref[0], after], axis=0)
        rolled = [xx] + [pltpu.roll(xx, s, 0) for s in range(1, CONV_K)]
        pre = b_ref[...]
        for s in range(CONV_K):
            pre = pre + rolled[s][SUBLANES:, :] * w_ref[3 - s:4 - s, :]
        da_ext = jnp.concatenate([dam_ref[0], jnp.where(i < nt - 1, daa_ref[0], 0.0)], axis=0)
        sg = _sigmoid(pre)
        dpre = da_ext * sg * (1.0 + pre * (1.0 - sg))
        du = dpre[:tt, :] * w_ref[3:4, :]
        for s in range(1, CONV_K):
            du = du + pltpu.roll(dpre, n2 - s, 0)[:tt, :] * w_ref[3 - s:4 - s, :]
        du_ref[0] = du.astype(BF16)
        dpm = dpre[:tt, :]
        gws = [jnp.sum(dpm * rolled[3 - kk][SUBLANES:SUBLANES + tt, :], axis=0, keepdims=True) for kk in range(CONV_K)]

        @pl.when((bi == 0) & (i == 0))
        def _():
            gw_ref[...] = jnp.zeros_like(gw_ref)
            gb_ref[...] = jnp.zeros_like(gb_ref)

        gw_ref[...] += jnp.concatenate(gws, axis=0)
        gb_ref[...] += jnp.sum(dpm, axis=0, keepdims=True)

    main = pl.BlockSpec((1, tt, CONV_TC), lambda c, b, i: (b, i, c))
    prev = pl.BlockSpec((1, SUBLANES, CONV_TC), lambda c, b, i: (b, jnp.maximum(i * hb - 1, 0), c))
    nxt = pl.BlockSpec((1, SUBLANES, CONV_TC), lambda c, b, i: (b, jnp.minimum((i + 1) * hb, seq // SUBLANES - 1), c))
    du, gw, gb = pl.pallas_call(
        body, name="conv_bwd", grid=(CONV_CH // CONV_TC, bsz, nt),
        in_specs=[main, nxt, main, prev, nxt,
                  pl.BlockSpec((CONV_K, CONV_TC), lambda c, b, i: (0, c)),
                  pl.BlockSpec((1, CONV_TC), lambda c, b, i: (0, c))],
        out_specs=[main, pl.BlockSpec((CONV_K, CONV_TC), lambda c, b, i: (0, c)),
                   pl.BlockSpec((1, CONV_TC), lambda c, b, i: (0, c))],
        out_shape=[jax.ShapeDtypeStruct((bsz, seq, CONV_CH), BF16), jax.ShapeDtypeStruct((CONV_K, CONV_CH), F32),
                   jax.ShapeDtypeStruct((1, CONV_CH), F32)],
        compiler_params=_params(("parallel", "arbitrary", "arbitrary")),
    )(da3, da3, x3, x3, x3, cw, cb)
    return du.reshape(bsz * seq, CONV_CH), gw, gb


def _ssd_constants():
    tri = (_iota((CHUNK, CHUNK), 1) <= _iota((CHUNK, CHUNK), 0)).astype(BF16)
    return tri, tri.T, _head_expand().astype(BF16), _head_reduce().astype(BF16)


def _ssd_prologue(dtf_ref, dtft_ref, bias_ref, biast_ref, arow_ref, acol_ref, tri_ref, triu_ref, expand_ref,
                  dtfull_scr, acfull_scr, acr_scr):
    dtc = _softplus(dtf_ref[0] + bias_ref[...])
    a = dtc * arow_ref[...]
    acum = _sel_dot(tri_ref[...], a)
    expand = expand_ref[...]
    dtfull_scr[...] = _dot_sel(dtc, expand)
    acfull_scr[...] = _dot_sel(acum, expand)
    dtr = _softplus(dtft_ref[0] + biast_ref[...])
    acr_scr[...] = _dot_sel(dtr * acol_ref[...], triu_ref[...])
    return dtc, acum


def _decay_matrix(acum, acr_scr, h):
    lane = _iota((CHUNK, LANES), 1)
    ac_col = jnp.sum(jnp.where(lane == h, acum, 0.0), axis=1, keepdims=True)
    ac_row = acr_scr[h:h + 1, :]
    causal = _iota((CHUNK, CHUNK), 1) <= _iota((CHUNK, CHUNK), 0)
    return jnp.exp(jnp.where(causal, ac_col - ac_row, NEG))


def _ssd_fwd(xbc, dtf, dtft, bias, biast, arow, acol, dfull, bsz, seq):
    nc = seq // CHUNK
    x3 = xbc.reshape(bsz, seq, CONV_CH)
    tri, triu, expand, _ = _ssd_constants()

    def body(xbc_ref, dtf_ref, dtft_ref, bias_ref, biast_ref, arow_ref, acol_ref, dfull_ref,
             tri_ref, triu_ref, expand_ref, y_ref, hall_ref, h_scr, dtfull_scr, acfull_scr, acr_scr):
        @pl.when(pl.program_id(1) == 0)
        def _():
            h_scr[...] = jnp.zeros_like(h_scr)

        _, acum = _ssd_prologue(dtf_ref, dtft_ref, bias_ref, biast_ref, arow_ref, acol_ref,
                                tri_ref, triu_ref, expand_ref, dtfull_scr, acfull_scr, acr_scr)
        lane = _iota((CHUNK, LANES), 1)
        for g in range(2):
            bg = xbc_ref[0, :, D_MODEL + LANES * g:D_MODEL + LANES * (g + 1)]
            cg = xbc_ref[0, :, D_MODEL + 2 * LANES + LANES * g:D_MODEL + 2 * LANES + LANES * (g + 1)]
            cg_bf = cg.astype(BF16)
            gmat = _bdot_nt(cg_bf, bg)
            bgt_bf = bg.T.astype(BF16)
            for j in range(4):
                p = 4 * g + j
                sl = slice(LANES * p, LANES * (p + 1))
                xs = xbc_ref[0, :, sl]
                ac = acfull_scr[:, sl]
                acl = acfull_scr[CHUNK - 1:CHUNK, sl]
                xdt = xs * dtfull_scr[:, sl]
                xdt_bf = xdt.astype(BF16)
                hp = h_scr[p]
                hall_ref[0, 0, p] = hp
                yo = _bdot(cg_bf, hp) * jnp.exp(ac)
                yd = []
                for hh in range(2):
                    mmat = gmat * _decay_matrix(acum, acr_scr, 2 * p + hh)
                    yd.append(_bdot(mmat, xdt_bf))
                y_ref[0, :, sl] = jnp.where(lane < HEAD_DIM, yd[0], yd[1]) + yo + dfull_ref[:, sl] * xs
                h_scr[p] = hp * jnp.exp(acl) + _bdot(bgt_bf, xdt * jnp.exp(acl - ac))

    row = lambda w: pl.BlockSpec((1, w), lambda b, c: (0, 0))
    whole = lambda a: pl.BlockSpec(a.shape, lambda b, c: (0, 0))
    y, hall = pl.pallas_call(
        body, name="ssd_fwd", grid=(bsz, nc),
        in_specs=[pl.BlockSpec((1, CHUNK, CONV_CH), lambda b, c: (b, c, 0)),
                  pl.BlockSpec((1, CHUNK, LANES), lambda b, c: (b, c, 0)),
                  pl.BlockSpec((1, LANES, CHUNK), lambda b, c: (b, 0, c)),
                  row(LANES), pl.BlockSpec((LANES, 1), lambda b, c: (0, 0)),
                  row(LANES), pl.BlockSpec((LANES, 1), lambda b, c: (0, 0)), row(D_MODEL),
                  whole(tri), whole(triu), whole(expand)],
        out_specs=[pl.BlockSpec((1, CHUNK, D_MODEL), lambda b, c: (b, c, 0)),
                   pl.BlockSpec((1, 1, HEAD_PAIRS, SSD_STATE, LANES), lambda b, c: (b, c, 0, 0, 0))],
        out_shape=[jax.ShapeDtypeStruct((bsz, seq, D_MODEL), F32),
                   jax.ShapeDtypeStruct((bsz, nc, HEAD_PAIRS, SSD_STATE, LANES), F32)],
        scratch_shapes=[pltpu.VMEM((HEAD_PAIRS, SSD_STATE, LANES), F32), pltpu.VMEM((CHUNK, D_MODEL), F32),
                        pltpu.VMEM((CHUNK, D_MODEL), F32), pltpu.VMEM((LANES, CHUNK), F32)],
        compiler_params=_params(("parallel", "arbitrary")),
    )(x3, dtf.reshape(bsz, seq, LANES), dtft, bias, biast, arow, acol, dfull, tri, triu, expand)
    return y.reshape(bsz * seq, D_MODEL), hall


def _ssd_bwd(dy, xbc, dtf, dtft, bias, biast, arow, acol, dfull, hall, bsz, seq):
    nc = seq // CHUNK
    x3 = xbc.reshape(bsz, seq, CONV_CH)
    dy3 = dy.reshape(bsz, seq, D_MODEL)
    consts = _ssd_constants()

    def body(dy_ref, xbc_ref, dtf_ref, dtft_ref, bias_ref, biast_ref, arow_ref, acol_ref, dfull_ref, hall_ref,
             tri_ref, triu_ref, expand_ref, reduce_ref,
             dx_ref, ddt_ref, acc_ref, dh_scr, dtfull_scr, acfull_scr, acr_scr, csum_scr, dacf_scr, ddtf_scr, ddl_scr):
        first = (pl.program_id(0) == 0) & (pl.program_id(1) == 0)

        @pl.when(first)
        def _():
            acc_ref[...] = jnp.zeros_like(acc_ref)

        @pl.when(pl.program_id(1) == 0)
        def _():
            dh_scr[...] = jnp.zeros_like(dh_scr)

        dtc, acum = _ssd_prologue(dtf_ref, dtft_ref, bias_ref, biast_ref, arow_ref, acol_ref,
                                  tri_ref, triu_ref, expand_ref, dtfull_scr, acfull_scr, acr_scr)
        csum_scr[...] = jnp.zeros_like(csum_scr)
        lane = _iota((CHUNK, LANES), 1)
        last_row = _iota((CHUNK, LANES), 0) == CHUNK - 1
        rs16 = jnp.zeros((CHUNK, LANES), F32)
        for g in range(2):
            bsl = slice(D_MODEL + LANES * g, D_MODEL + LANES * (g + 1))
            csl = slice(D_MODEL + 2 * LANES + LANES * g, D_MODEL + 2 * LANES + LANES * (g + 1))
            bg_bf = xbc_ref[0, :, bsl].astype(BF16)
            cg = xbc_ref[0, :, csl]
            cg_bf = cg.astype(BF16)
            cgt_bf = cg.T.astype(BF16)
            gmat = _bdot_nt(cg_bf, bg_bf)
            dg = jnp.zeros((CHUNK, CHUNK), F32)
            dbg = jnp.zeros((CHUNK, SSD_STATE), F32)
            dcg = jnp.zeros((CHUNK, SSD_STATE), F32)
            for j in range(4):
                p = 4 * g + j
                sl = slice(LANES * p, LANES * (p + 1))
                xs = xbc_ref[0, :, sl]
                dt = dtfull_scr[:, sl]
                ac = acfull_scr[:, sl]
                acl = acfull_scr[CHUNK - 1:CHUNK, sl]
                dyp = dy_ref[0, :, sl]
                xdt = xs * dt
                xdt_bf = xdt.astype(BF16)
                e = jnp.exp(ac)
                dsd = jnp.exp(acl - ac)
                cd = jnp.exp(acl)
                xds_bf = (xdt * dsd).astype(BF16)
                hp = hall_ref[0, 0, p]
                hp_bf = hp.astype(BF16)
                dhn = dh_scr[p]
                dhn_bf = dhn.astype(BF16)
                yo = _bdot(cg_bf, hp_bf) * e
                dw_bf = (dyp * e).astype(BF16)
                dcg = dcg + _bdot_nt(dw_bf, hp_bf)
                dhin = _bdot(cgt_bf, dw_bf)
                dac = dyp * yo
                dacl = jnp.sum(dhn * hp, axis=0, keepdims=True) * cd
                dxds = _bdot(bg_bf, dhn_bf)
                dbg = dbg + _bdot_nt(xds_bf, dhn_bf)
                dxdt = dxds * dsd
                tdec = dxds * xdt * dsd
                dacl = dacl + jnp.sum(tdec, axis=0, keepdims=True)
                dac = dac - tdec
                dh_scr[p] = dhn * cd + dhin
                for hh in range(2):
                    h = 2 * p + hh
                    lm = (lane < HEAD_DIM) if hh == 0 else (lane >= HEAD_DIM)
                    dec = _decay_matrix(acum, acr_scr, h)
                    mmat = gmat * dec
                    dyh_bf = jnp.where(lm, dyp, 0.0).astype(BF16)
                    dm = _bdot_nt(dyh_bf, xdt_bf)
                    dxdt = dxdt + _bdot(mmat.T, dyh_bf)
                    dg = dg + dm * dec
                    q = dm * mmat
                    rs16 = rs16 + jnp.where(lane == h, jnp.sum(q, axis=1, keepdims=True), 0.0)
                    csum_scr[h:h + 1, :] = jnp.sum(q, axis=0, keepdims=True)
                dx_ref[0, :, sl] = dfull_ref[:, sl] * dyp + dxdt * dt
                dacf_scr[:, sl] = dac + jnp.where(last_row, dacl, 0.0)
                ddtf_scr[:, sl] = dxdt * xs
                ddl_scr[:, sl] = jnp.broadcast_to(jnp.sum(dyp * xs, axis=0, keepdims=True), (SUBLANES, LANES))
            dg_bf = dg.astype(BF16)
            dx_ref[0, :, bsl] = dbg + _bdot(dg.T, cg_bf)
            dx_ref[0, :, csl] = dcg + _bdot(dg_bf, bg_bf)
        reduce, triu = reduce_ref[...], triu_ref[...]
        dac16 = _dot_sel(dacf_scr[...], reduce) + rs16 - csum_scr[...].T
        da16 = _sel_dot(triu, dac16)
        ddt16 = da16 * arow_ref[...] + _dot_sel(ddtf_scr[...], reduce)
        ddtraw = ddt16 * _sigmoid(dtf_ref[0] + bias_ref[...])
        ddt_ref[0] = ddtraw
        acc_ref[0:8, :] += jnp.broadcast_to(jnp.sum(da16 * dtc * arow_ref[...], axis=0, keepdims=True), (SUBLANES, LANES))
        acc_ref[8:16, :] += _dot_sel(ddl_scr[...], reduce)
        acc_ref[16:24, :] += jnp.broadcast_to(jnp.sum(ddtraw, axis=0, keepdims=True), (SUBLANES, LANES))

    rev = lambda b, c: (b, nc - 1 - c, 0)
    row = lambda w: pl.BlockSpec((1, w), lambda b, c: (0, 0))
    dx, ddt, acc = pl.pallas_call(
        body, name="ssd_bwd", grid=(bsz, nc),
        in_specs=[pl.BlockSpec((1, CHUNK, D_MODEL), rev), pl.BlockSpec((1, CHUNK, CONV_CH), rev),
                  pl.BlockSpec((1, CHUNK, LANES), rev),
                  pl.BlockSpec((1, LANES, CHUNK), lambda b, c: (b, 0, nc - 1 - c)),
                  row(LANES), pl.BlockSpec((LANES, 1), lambda b, c: (0, 0)),
                  row(LANES), pl.BlockSpec((LANES, 1), lambda b, c: (0, 0)), row(D_MODEL),
                  pl.BlockSpec((1, 1, HEAD_PAIRS, SSD_STATE, LANES), lambda b, c: (b, nc - 1 - c, 0, 0, 0))]
        + [pl.BlockSpec(a.shape, lambda b, c: (0, 0)) for a in consts],
        out_specs=[pl.BlockSpec((1, CHUNK, CONV_CH), rev), pl.BlockSpec((1, CHUNK, LANES), rev),
                   pl.BlockSpec((3 * SUBLANES, LANES), lambda b, c: (0, 0))],
        out_shape=[jax.ShapeDtypeStruct((bsz, seq, CONV_CH), F32), jax.ShapeDtypeStruct((bsz, seq, LANES), F32),
                   jax.ShapeDtypeStruct((3 * SUBLANES, LANES), F32)],
        scratch_shapes=[pltpu.VMEM((HEAD_PAIRS, SSD_STATE, LANES), F32), pltpu.VMEM((CHUNK, D_MODEL), F32),
                        pltpu.VMEM((CHUNK, D_MODEL), F32), pltpu.VMEM((LANES, CHUNK), F32),
                        pltpu.VMEM((LANES, CHUNK), F32), pltpu.VMEM((CHUNK, D_MODEL), F32),
                        pltpu.VMEM((CHUNK, D_MODEL), F32), pltpu.VMEM((SUBLANES, D_MODEL), F32)],
        compiler_params=_params(("arbitrary", "arbitrary")),
    )(dy3, x3, dtf.reshape(bsz, seq, LANES), dtft, bias, biast, arow, acol, dfull, hall, *consts)
    return dx.reshape(bsz * seq, CONV_CH), ddt.reshape(bsz * seq, LANES), acc


GROUP_W = D_MODEL // 2


def _gnorm_fwd(y, z, o_att, w):
    n = y.shape[0]
    tm = min(1024, n)

    def body(y_ref, z_ref, o_ref, w_ref, out_ref):
        zv = z_ref[...]
        g = y_ref[...] * (zv * _sigmoid(zv))
        for gi in range(2):
            sl = slice(GROUP_W * gi, GROUP_W * (gi + 1))
            gs = g[:, sl]
            r = lax.rsqrt(jnp.mean(gs * gs, axis=-1, keepdims=True) + NORM_EPS)
            out_ref[:, sl] = (gs * r * w_ref[:, sl]).astype(BF16)
        out_ref[:, D_MODEL:] = o_ref[...].astype(BF16)

    tok = pl.BlockSpec((tm, D_MODEL), lambda i: (i, 0))
    return pl.pallas_call(
        body, name="gnorm_fwd", grid=(n // tm,),
        in_specs=[tok, tok, tok, pl.BlockSpec((1, D_MODEL), lambda i: (0, 0))],
        out_specs=pl.BlockSpec((tm, MIX_WIDTH), lambda i: (i, 0)),
        out_shape=jax.ShapeDtypeStruct((n, MIX_WIDTH), BF16),
        compiler_params=_params(("parallel",)),
    )(y, z, o_att, w)


def _gnorm_bwd(dycat, y, z, w):
    n = y.shape[0]
    tm = min(512, n)

    def body(dn_ref, y_ref, z_ref, w_ref, dy_ref, dz_ref, gw_ref):
        zv, yv = z_ref[...], y_ref[...]
        sz = _sigmoid(zv)
        sil = zv * sz
        g = yv * sil

        @pl.when(pl.program_id(0) == 0)
        def _():
            gw_ref[...] = jnp.zeros_like(gw_ref)

        for gi in range(2):
            sl = slice(GROUP_W * gi, GROUP_W * (gi + 1))
            gs = g[:, sl]
            r = lax.rsqrt(jnp.mean(gs * gs, axis=-1, keepdims=True) + NORM_EPS)
            nrm = gs * r
            dyn = dn_ref[:, sl]
            dn = dyn * w_ref[:, sl]
            dgs = r * (dn - nrm * jnp.mean(dn * nrm, axis=-1, keepdims=True))
            dy_ref[:, sl] = dgs * sil[:, sl]
            dz_ref[:, sl] = (dgs * yv[:, sl] * (sz[:, sl] * (1.0 + zv[:, sl] * (1.0 - sz[:, sl])))).astype(BF16)
            gw_ref[:, sl] += jnp.sum(dyn * nrm, axis=0, keepdims=True)

    tok = pl.BlockSpec((tm, D_MODEL), lambda i: (i, 0))
    row = pl.BlockSpec((1, D_MODEL), lambda i: (0, 0))
    return pl.pallas_call(
        body, name="gnorm_bwd", grid=(n // tm,),
        in_specs=[tok, tok, tok, row], out_specs=[tok, tok, row],
        out_shape=[jax.ShapeDtypeStruct((n, D_MODEL), F32), jax.ShapeDtypeStruct((n, D_MODEL), BF16),
                   jax.ShapeDtypeStruct((1, D_MODEL), F32)],
        compiler_params=_params(("arbitrary",)),
    )(dycat, y, z, w)


AUX_C = 3
AUX_ONE = 3


def _aux_base(hh):
    return HEAD_DIM * (1 - hh)


def _fox_prep(dtf, fb, bsz, seq):
    def body(x_ref, b_ref, aux_ref):
        tri = (_iota((CHUNK, CHUNK), 1) <= _iota((CHUNK, CHUNK), 0)).astype(F32)
        row, col = _iota((LANES, D_MODEL), 0), _iota((LANES, D_MODEL), 1)
        lane = jnp.bitwise_and(col, LANES - 1)
        other = (lane < HEAD_DIM).astype(jnp.int32)
        term = lane - HEAD_DIM * (1 - other)
        src = F_COL + 2 * jnp.right_shift(col, 7) + other
        place = [jnp.where((row == src) & (term == i), -1.0, 0.0).astype(BF16) for i in range(AUX_C)]
        lane1 = jnp.bitwise_and(_iota((1, D_MODEL), 1), LANES - 1)
        ones_lane = (lane1 - HEAD_DIM * (1 - (lane1 < HEAD_DIM).astype(jnp.int32)) == AUX_ONE).astype(F32)
        carry = jnp.zeros((1, LANES), F32)
        for j in range(seq // CHUNK):
            sl = slice(CHUNK * j, CHUNK * (j + 1))
            lf = _log_sigmoid(x_ref[sl, :] + b_ref[...])
            c = _sel_dot(tri, lf) + carry
            carry = carry + jnp.sum(lf, axis=0, keepdims=True)
            t1, t2, t3 = (jnp.dot(t, p, preferred_element_type=F32) for t, p in zip(_split3(c), place))
            aux_ref[sl, :] = (t1 + t2 + t3 + ones_lane).astype(BF16)

    return pl.pallas_call(
        body, name="fox_prep", grid=(bsz,),
        in_specs=[pl.BlockSpec((seq, LANES), lambda b: (b, 0)), pl.BlockSpec((1, LANES), lambda b: (0, 0))],
        out_specs=pl.BlockSpec((seq, D_MODEL), lambda b: (b, 0)),
        out_shape=jax.ShapeDtypeStruct((bsz * seq, D_MODEL), BF16),
        compiler_params=_params(("parallel",)),
    )(dtf, fb)


def _fox_prep_bwd(dck, dcq, ddt, dtf, fb, bsz, seq):
    nj = seq // CHUNK

    def body(dck_ref, dcq_ref, ddt_ref, x_ref, b_ref, out_ref, gb_ref):
        triu = (_iota((CHUNK, CHUNK), 0) <= _iota((CHUNK, CHUNK), 1)).astype(F32)
        is_f = (_iota((CHUNK, LANES), 1) >= F_COL) & (_iota((CHUNK, LANES), 1) < 2 * F_COL)
        carry = jnp.zeros((1, LANES), F32)
        total = jnp.zeros((1, LANES), F32)
        for j in range(nj - 1, -1, -1):
            sl = slice(CHUNK * j, CHUNK * (j + 1))
            dcv = dck_ref[sl, :] + dcq_ref[sl, :]
            dlf = _sel_dot(triu, dcv) + carry
            carry = carry + jnp.sum(dcv, axis=0, keepdims=True)
            df = jnp.where(is_f, dlf * _sigmoid(-(x_ref[sl, :] + b_ref[...])), 0.0)
            out_ref[sl, :] = (df + ddt_ref[sl, :]).astype(BF16)
            total = total + jnp.sum(df, axis=0, keepdims=True)

        @pl.when(pl.program_id(0) == 0)
        def _():
            gb_ref[...] = jnp.zeros_like(gb_ref)

        gb_ref[...] += jnp.broadcast_to(total, (SUBLANES, LANES))

    blk = pl.BlockSpec((seq, LANES), lambda b: (b, 0))
    return pl.pallas_call(
        body, name="fox_prep_bwd", grid=(bsz,),
        in_specs=[blk, blk, blk, blk, pl.BlockSpec((1, LANES), lambda b: (0, 0))],
        out_specs=[blk, pl.BlockSpec((SUBLANES, LANES), lambda b: (0, 0))],
        out_shape=[jax.ShapeDtypeStruct((bsz * seq, LANES), BF16), jax.ShapeDtypeStruct((SUBLANES, LANES), F32)],
        compiler_params=_params(("arbitrary",)),
    )(dck, dcq, ddt, dtf, fb)


ATT_BLOCK_FWD = 512
ATT_BLOCK_KEYS = 256
ATT_BLOCK_BWD = 256


def _att_operands(q_ref, k_ref, aux_ref, hh):
    lane = _iota((1, LANES), 1)
    lm = (lane < HEAD_DIM) if hh == 0 else (lane >= HEAD_DIM)
    base = _aux_base(hh)
    zero = jnp.zeros((1, LANES), BF16)
    ka = jnp.where(lm, k_ref[...], jnp.where((lane >= base) & (lane <= base + AUX_ONE), aux_ref[...], zero))
    qa = jnp.where(lm, q_ref[...] * ATT_SCALE,
                   jnp.where((lane >= base) & (lane < base + AUX_C), jnp.ones((1, LANES), BF16), zero))
    return lm, base, qa, ka


def _att_fwd(qkv, ccol, bsz, seq, gather):
    bq, bk = min(ATT_BLOCK_FWD, seq), min(ATT_BLOCK_KEYS, seq)
    nq, ratio = seq // bq, bq // bk
    nx = len(gather)

    def body(*refs):
        q_ref, k_ref, v_ref, c_ref = refs[:4]
        x_refs = refs[4:4 + nx]
        o_ref, lse_ref = refs[4 + nx:6 + nx]
        xo_refs = refs[6 + nx:6 + 2 * nx]
        qa_scr, ka_scr, vt_scr = refs[6 + 2 * nx:9 + 2 * nx]
        sems = refs[9 + 2 * nx:]
        pair = pl.program_id(1)

        @pl.when((pl.program_id(0) == 0) & (pair == 0))
        def _():
            _exchange_start("gather", x_refs, xo_refs, *sems)

        lse_ref[...] = jnp.zeros_like(lse_ref)
        for hh in range(2):
            _, _, qa, ka = _att_operands(q_ref, k_ref, c_ref, hh)
            qa_scr[hh] = qa
            ka_scr[hh] = ka
        for t0 in range(0, seq, bq):
            vt_scr[:, t0:t0 + bq] = v_ref[t0:t0 + bq, :].T
        key_minus_query = _iota((bk, bq), 0) - _iota((bk, bq), 1)

        def q_step(i, carry0):
            qrows = pl.ds(pl.multiple_of(i * bq, bq), bq)

            def scores(j):
                krows = pl.ds(pl.multiple_of(j * bk, bk), bk)
                return tuple(_bdot_nt(ka_scr[hh, krows, :], qa_scr[hh, qrows, :]) for hh in range(2))

            def update(state, st_pair, j, diag):
                krows = pl.ds(pl.multiple_of(j * bk, bk), bk)
                out = []
                for hh in range(2):
                    m, l, acc = state[hh]
                    st = st_pair[hh]
                    if diag is not None:
                        st = jnp.where(key_minus_query + diag * bk <= 0, st, NEG)
                    mn = jnp.maximum(m, jnp.max(st, axis=0, keepdims=True))
                    alpha = jnp.exp(m - mn)
                    pt = jnp.exp(st - mn)
                    l = alpha * l + jnp.sum(pt, axis=0, keepdims=True)
                    vt = vt_scr[HEAD_DIM * hh:HEAD_DIM * (hh + 1), krows]
                    out.append((mn, l, alpha * acc + _bdot(vt, pt)))
                return tuple(out)

            def step(j, carry):
                state, s_cur = carry
                s_next = scores(j + 1)
                return update(state, s_cur, j, None), s_next

            def step_pair(t, carry):
                state, s_cur = carry
                s_second = scores(2 * t + 1)
                s_next = scores(2 * t + 2)
                state = update(state, s_cur, 2 * t, None)
                return update(state, s_second, 2 * t + 1, None), s_next

            one = (jnp.full((1, bq), NEG, F32), jnp.zeros((1, bq), F32), jnp.zeros((HEAD_DIM, bq), F32))
            first_diag = i * ratio
            if ratio % 2 == 0:
                state, s_cur = lax.fori_loop(0, first_diag // 2, step_pair, ((one, one), scores(0)))
            else:
                state, s_cur = lax.fori_loop(0, first_diag, step, ((one, one), scores(0)))
            for r in range(ratio):
                s_next = scores(first_diag + r + 1) if r + 1 < ratio else None
                state = update(state, s_cur, first_diag + r, r)
                s_cur = s_next
            (m0, l0, acc0), (m1, l1, acc1) = state
            ot = jnp.concatenate([acc0 * (1.0 / l0), acc1 * (1.0 / l1)], axis=0)
            o_ref[qrows, :] = ot.T
            lse_ref[0, 0, 0:1, qrows] = m0 + jnp.log(l0)
            lse_ref[0, 0, 1:2, qrows] = m1 + jnp.log(l1)
            return carry0

        lax.fori_loop(0, nq, q_step, 0)

        @pl.when((pl.program_id(0) == bsz - 1) & (pair == HEAD_PAIRS - 1))
        def _():
            _exchange_wait("gather", x_refs, xo_refs, *sems)

    col = lambda off: pl.BlockSpec((seq, LANES), lambda b, p: (b, off + p))
    head2 = pltpu.VMEM((2, seq, LANES), BF16)
    hbm = pl.BlockSpec(memory_space=pl.ANY)
    return pl.pallas_call(
        body, name="att_fwd", grid=(bsz, HEAD_PAIRS),
        in_specs=[col(0), col(HEAD_PAIRS), col(2 * HEAD_PAIRS), col(0)] + [hbm] * nx,
        out_specs=[pl.BlockSpec((seq, LANES), lambda b, p: (b, p)),
                   pl.BlockSpec((1, 1, SUBLANES, seq), lambda b, p: (b, p, 0, 0))] + [hbm] * nx,
        out_shape=[jax.ShapeDtypeStruct((bsz * seq, D_MODEL), F32),
                   jax.ShapeDtypeStruct((bsz, HEAD_PAIRS, SUBLANES, seq), F32)] + _exchange_shapes("gather", gather),
        scratch_shapes=[head2, head2, pltpu.VMEM((LANES, seq), BF16)] + _exchange_scratch(nx),
        compiler_params=_params(("arbitrary", "arbitrary")),
    )(qkv, qkv, qkv, ccol, *gather)


def _att_bwd(qkv, dycat, o, lse, ccol, bsz, seq, scatter):
    blk = bq = min(ATT_BLOCK_BWD, seq)
    nb = nq = seq // blk
    nx = len(scatter)

    def body(*refs):
        q_ref, k_ref, v_ref, do_ref, o_ref, lse_ref, c_ref = refs[:7]
        x_refs = refs[7:7 + nx]
        dq_ref, dk_ref, dv_ref, dck_ref, dcq_ref = refs[7 + nx:12 + nx]
        xo_refs = refs[12 + nx:12 + 2 * nx]
        (qa_scr, ka_scr, va_scr, doa_scr, kat_scr, qat_scr, dot_scr, d_scr, dqt_scr, dkt_scr,
         dvt_scr) = refs[12 + 2 * nx:23 + 2 * nx]
        sems = refs[23 + 2 * nx:]
        pair = pl.program_id(1)

        @pl.when((pl.program_id(0) == 0) & (pair == 0))
        def _():
            _exchange_start("scatter", x_refs, xo_refs, *sems)

        query_minus_key = _iota((blk, bq), 1) - _iota((blk, bq), 0)
        lane_b = _iota((blk, LANES), 1)
        row_t = _iota((LANES, seq), 0)
        bases = []
        ones8 = jnp.ones((SUBLANES, LANES), F32)
        for hh in range(2):
            lm, base, qa, ka = _att_operands(q_ref, k_ref, c_ref, hh)
            bases.append(base)
            qa_scr[hh] = qa
            ka_scr[hh] = ka
            va_scr[hh] = jnp.where(lm, v_ref[...], jnp.zeros((seq, LANES), BF16))
            doa = jnp.where(lm, do_ref[...], 0.0).astype(BF16)
            doa_scr[hh] = doa
            for t0 in range(0, seq, blk):
                kat_scr[hh, :, t0:t0 + blk] = ka[t0:t0 + blk, :].T
                qat_scr[hh, :, t0:t0 + blk] = qa[t0:t0 + blk, :].T
            d1, d2, d3 = (_bdot_nt(ones8, term) for term in _split3(doa.astype(F32) * o_ref[...]))
            d_scr[hh] = d1 + d2 + d3
        for t0 in range(0, seq, blk):
            dot_scr[:, t0:t0 + blk] = do_ref[t0:t0 + blk, :].astype(BF16).T
        dqt_scr[...] = jnp.zeros_like(dqt_scr)
        dck_ref[...] = jnp.zeros_like(dck_ref)

        def kv_step(j, carry0):
            krows = pl.ds(pl.multiple_of(j * blk, blk), blk)
            dkt_scr[...] = jnp.zeros_like(dkt_scr)
            dvt_scr[...] = jnp.zeros_like(dvt_scr)

            def scores(i):
                rows = pl.ds(pl.multiple_of(i * bq, bq), bq)
                return tuple((_bdot_nt(ka_scr[hh, krows, :], qa_scr[hh, rows, :]),
                              _bdot_nt(va_scr[hh, krows, :], doa_scr[hh, rows, :])) for hh in range(2))

            def update(i, sd, masked):
                rows = pl.ds(pl.multiple_of(i * bq, bq), bq)
                for hh in range(2):
                    st, dpt = sd[hh]
                    if masked:
                        st = jnp.where(query_minus_key >= 0, st, NEG)
                    pt = jnp.exp(st - lse_ref[0, 0, hh:hh + 1, rows])
                    dst = (pt * (dpt - d_scr[hh, 0:1, rows])).astype(BF16)
                    dvt_scr[hh] += _bdot_nt(dot_scr[HEAD_DIM * hh:HEAD_DIM * (hh + 1), rows], pt)
                    dkt_scr[hh] += _bdot_nt(qat_scr[hh, :, rows], dst)
                    dqt_scr[hh, :, rows] += _bdot(kat_scr[hh, :, krows], dst)

            i_diag = j

            def q_pair(t, sd_cur):
                i = i_diag + 1 + 2 * t
                sd_second = scores(i + 1)
                sd_next = scores(jnp.minimum(i + 2, nq - 1))
                update(i, sd_cur, False)
                update(i + 1, sd_second, False)
                return sd_next

            sd_diag = scores(i_diag)
            sd_first = scores(jnp.minimum(i_diag + 1, nq - 1))
            update(i_diag, sd_diag, True)
            rest = nq - 1 - i_diag
            sd_last = lax.fori_loop(0, rest // 2, q_pair, sd_first)

            @pl.when(rest % 2 == 1)
            def _():
                update(jnp.int32(nq - 1), sd_last, False)
            dkt = jnp.where(_iota((LANES, blk), 0) < HEAD_DIM, dkt_scr[0], dkt_scr[1])
            dk_ref[krows, :] = dkt.T.astype(BF16)
            dv_ref[krows, :] = jnp.concatenate([dvt_scr[0], dvt_scr[1]], axis=0).T.astype(BF16)
            for hh in range(2):
                dck_ref[0, 0, hh:hh + 1, krows] = -dkt_scr[hh, bases[hh]:bases[hh] + 1, :]
            return carry0

        lax.fori_loop(0, nb, kv_step, 0)
        for t0 in range(0, seq, blk):
            dq0, dq1 = dqt_scr[0, :, t0:t0 + blk].T, dqt_scr[1, :, t0:t0 + blk].T
            dq_ref[t0:t0 + blk, :] = (jnp.where(lane_b < HEAD_DIM, dq0, dq1) * ATT_SCALE).astype(BF16)
        dcq_ref[...] = jnp.zeros_like(dcq_ref)
        for hh in range(2):
            dcq_ref[0, 0, hh:hh + 1, :] = jnp.sum(jnp.where(row_t == bases[hh] + AUX_ONE, dqt_scr[hh], 0.0),
                                                   axis=0, keepdims=True)

        @pl.when((pl.program_id(0) == bsz - 1) & (pair == HEAD_PAIRS - 1))
        def _():
            _exchange_wait("scatter", x_refs, xo_refs, *sems)

    col = lambda off: pl.BlockSpec((seq, LANES), lambda b, p: (b, off + p))
    rowvec = pl.BlockSpec((1, 1, SUBLANES, seq), lambda b, p: (b, p, 0, 0))
    out = jax.ShapeDtypeStruct((bsz * seq, D_MODEL), BF16)
    rows_shape = jax.ShapeDtypeStruct((bsz, HEAD_PAIRS, SUBLANES, seq), F32)
    head2 = pltpu.VMEM((2, seq, LANES), BF16)
    hbm = pl.BlockSpec(memory_space=pl.ANY)
    return pl.pallas_call(
        body, name="att_bwd", grid=(bsz, HEAD_PAIRS),
        in_specs=[col(0), col(HEAD_PAIRS), col(2 * HEAD_PAIRS), col(HEAD_PAIRS), col(0), rowvec, col(0)] + [hbm] * nx,
        out_specs=[col(0), col(0), col(0), rowvec, rowvec] + [hbm] * nx,
        out_shape=[out, out, out, rows_shape, rows_shape] + _exchange_shapes("scatter", scatter),
        scratch_shapes=[head2, head2, head2, head2, pltpu.VMEM((2, LANES, seq), BF16),
                        pltpu.VMEM((2, LANES, seq), BF16), pltpu.VMEM((LANES, seq), BF16),
                        pltpu.VMEM((2, SUBLANES, seq), F32), pltpu.VMEM((2, LANES, seq), F32),
                        pltpu.VMEM((2, LANES, blk), F32), pltpu.VMEM((2, HEAD_DIM, blk), F32)] + _exchange_scratch(nx),
        compiler_params=_params(("arbitrary", "arbitrary")),
    )(qkv, qkv, qkv, dycat, o, lse, ccol, *scatter)


def _adamw_math(w, g, m, v):
    m = ADAM_B1 * m + (1.0 - ADAM_B1) * g
    v = ADAM_B2 * v + (1.0 - ADAM_B2) * jnp.square(g)
    m_hat = m / ADAM_C1
    v_hat = v / ADAM_C2
    delta = -ADAM_LR * (m_hat / (jnp.sqrt(v_hat) + ADAM_EPS) + ADAM_WD * w)
    return delta, m, v


def _row_tile(rows):
    for tr in (256, 128, 64, 32, 16, 8):
        if rows % tr == 0:
            return tr
    return rows


def _sum_parts(parts, name):
    nparts, rows, cols = parts.shape
    tr = rows if rows % SUBLANES else _row_tile(rows)

    def body(p_ref, o_ref):
        g = p_ref[0].astype(F32)
        for s in range(1, nparts):
            g = g + p_ref[s].astype(F32)
        o_ref[...] = g

    return pl.pallas_call(
        body, name=name, grid=(rows // tr,),
        in_specs=[pl.BlockSpec((nparts, tr, cols), lambda i: (0, i, 0))],
        out_specs=pl.BlockSpec((tr, cols), lambda i: (i, 0)),
        out_shape=jax.ShapeDtypeStruct((rows, cols), F32),
        compiler_params=_params(("parallel",)),
    )(parts)


def _adamw_parts(parts, w, m, v, name):
    nparts, rows, cols = parts.shape
    tr = _row_tile(rows)

    def body(p_ref, w_ref, m_ref, v_ref, g_ref, d_ref, mo_ref, vo_ref):
        g = p_ref[0].astype(F32)
        for s in range(1, nparts):
            g = g + p_ref[s].astype(F32)
        delta, mn, vn = _adamw_math(w_ref[...], g, m_ref[...], v_ref[...])
        g_ref[...] = g
        d_ref[...] = delta
        mo_ref[...] = mn
        vo_ref[...] = vn

    blk = pl.BlockSpec((tr, cols), lambda i: (i, 0))
    shp = jax.ShapeDtypeStruct((rows, cols), F32)
    return pl.pallas_call(
        body, name=name, grid=(rows // tr,),
        in_specs=[pl.BlockSpec((nparts, tr, cols), lambda i: (0, i, 0)), blk, blk, blk],
        out_specs=[blk, blk, blk, blk], out_shape=[shp, shp, shp, shp],
        compiler_params=_params(("parallel",)),
    )(parts, w, m, v)


def _me():
    return lax.axis_index("x"), lax.axis_index("y"), lax.axis_index("c")


def _flip(pos, k):
    x, y, c = pos
    kx, ky, kc = (k >> 2) & 1, (k >> 1) & 1, k & 1
    return (x ^ kx if kx else x, y ^ ky if ky else y, c ^ kc if kc else c)


def _logical(pos):
    return 4 * pos[0] + 2 * pos[1] + pos[2]


def _all_gather_two_level(shards):
    narr = len(shards)

    def body(*refs):
        x_refs, out_refs = refs[:narr], refs[narr:2 * narr]
        send_sems, recv_sems, local_sems = refs[2 * narr:]
        x, y, c = _me()
        me, sibling = (x, y, c), (x, y, 1 - c)
        chips = [(1 - x, y), (x, 1 - y), (1 - x, 1 - y)]

        def copy(a, k, block, to, src=None):
            slot = out_refs[a].at[_logical(block)]
            return pltpu.make_async_remote_copy(
                src_ref=slot if src is None else src, dst_ref=slot,
                send_sem=send_sems.at[a, k], recv_sem=recv_sems.at[a, k], device_id=to, device_id_type=MESH)

        mine = [pltpu.make_async_copy(x_refs[a], out_refs[a].at[_logical(me)], local_sems.at[a]) for a in range(narr)]
        for cp in mine:
            cp.start()
        first = []
        for a in range(narr):
            first.append(copy(a, 0, me, sibling, src=x_refs[a]))
            first += [copy(a, 1 + j, me, (*chip, c), src=x_refs[a]) for j, chip in enumerate(chips)]
        for cp in first:
            cp.start()
        passed = []
        for a in range(narr):
            for j, chip in enumerate(chips):
                copy(a, 1 + j, (*chip, c), me).wait_recv()
                fwd = copy(a, 4 + j, (*chip, c), sibling)
                fwd.start()
                passed.append(fwd)
        for a in range(narr):
            copy(a, 0, sibling, me).wait_recv()
            for j, chip in enumerate(chips):
                copy(a, 4 + j, (*chip, 1 - c), me).wait_recv()
        for cp in first + passed:
            cp.wait_send()
        for cp in mine:
            cp.wait()

    hbm = pl.BlockSpec(memory_space=pl.ANY)
    return pl.pallas_call(
        body, name="all_gather_big",
        out_shape=[jax.ShapeDtypeStruct((N_DEV,) + s.shape, s.dtype) for s in shards],
        in_specs=[hbm] * narr, out_specs=[hbm] * narr,
        scratch_shapes=[pltpu.SemaphoreType.DMA((narr, 7)), pltpu.SemaphoreType.DMA((narr, 7)),
                        pltpu.SemaphoreType.DMA((narr,))],
    )(*shards)


def _exchange_copies(kind, x_refs, out_refs, send_sems, recv_sems, local_sems):
    me = _me()
    mine = _logical(me)
    local, remote = [], []
    for a, (x_ref, out_ref) in enumerate(zip(x_refs, out_refs)):
        own = x_ref if kind == "gather" else x_ref.at[mine]
        local.append(pltpu.make_async_copy(own, out_ref.at[mine], local_sems.at[a]))
        for k in range(1, N_DEV):
            peer = _flip(me, k)
            src = x_ref if kind == "gather" else x_ref.at[_logical(peer)]
            remote.append(pltpu.make_async_remote_copy(
                src_ref=src, dst_ref=out_ref.at[mine], send_sem=send_sems.at[a, k - 1], recv_sem=recv_sems.at[a, k - 1],
                device_id=peer, device_id_type=MESH))
    return local, remote


def _exchange_start(kind, x_refs, out_refs, *sems):
    if not x_refs:
        return
    local, remote = _exchange_copies(kind, x_refs, out_refs, *sems)
    for cp in local + remote:
        cp.start()


def _exchange_wait(kind, x_refs, out_refs, *sems):
    if not x_refs:
        return
    local, remote = _exchange_copies(kind, x_refs, out_refs, *sems)
    for cp in remote:
        cp.wait_recv()
    for cp in remote:
        cp.wait_send()
    for cp in local:
        cp.wait()


def _exchange_shapes(kind, arrays):
    return [jax.ShapeDtypeStruct(((N_DEV,) if kind == "gather" else ()) + a.shape, a.dtype) for a in arrays]


def _exchange_scratch(narrays):
    if not narrays:
        return []
    return [pltpu.SemaphoreType.DMA((narrays, N_DEV - 1)), pltpu.SemaphoreType.DMA((narrays, N_DEV - 1)),
            pltpu.SemaphoreType.DMA((narrays,))]


def _exchange_rows(x_ref, buf_ref, send_sems, recv_sems):
    me = _me()
    buf_ref[_logical(me)] = x_ref[...]
    copies = []
    for k in range(1, N_DEV):
        peer = _flip(me, k)
        copies.append(pltpu.make_async_remote_copy(
            src_ref=x_ref, dst_ref=buf_ref.at[_logical(me)],
            send_sem=send_sems.at[k - 1], recv_sem=recv_sems.at[k - 1], device_id=peer, device_id_type=MESH))
    for cp in copies:
        cp.start()
    for cp in copies:
        cp.wait_recv()
    for cp in copies:
        cp.wait_send()


def _sum_slots(buf_ref):
    total = buf_ref[0]
    for s in range(1, N_DEV):
        total = total + buf_ref[s]
    return total


def _small_gather(x):
    def body(x_ref, o_ref, buf_ref, send_sems, recv_sems):
        _exchange_rows(x_ref, buf_ref, send_sems, recv_sems)
        o_ref[...] = _sum_slots(buf_ref)

    return pl.pallas_call(
        body, name="small_gather", out_shape=jax.ShapeDtypeStruct(x.shape, F32),
        in_specs=[pl.BlockSpec(memory_space=pltpu.VMEM)], out_specs=pl.BlockSpec(memory_space=pltpu.VMEM),
        scratch_shapes=[pltpu.VMEM((N_DEV,) + x.shape, F32), pltpu.SemaphoreType.DMA((7,)), pltpu.SemaphoreType.DMA((7,))],
    )(x)


def _small_reduce_adamw(g, w, m, v):
    def body(g_ref, w_ref, m_ref, v_ref, go_ref, d_ref, mo_ref, vo_ref, buf_ref, send_sems, recv_sems):
        _exchange_rows(g_ref, buf_ref, send_sems, recv_sems)
        gs = _sum_slots(buf_ref)
        delta, mn, vn = _adamw_math(w_ref[...], gs, m_ref[...], v_ref[...])
        go_ref[...] = gs
        d_ref[...] = delta
        mo_ref[...] = mn
        vo_ref[...] = vn

    vm = pl.BlockSpec(memory_space=pltpu.VMEM)
    shp = jax.ShapeDtypeStruct(g.shape, F32)
    return pl.pallas_call(
        body, name="small_reduce_adamw", out_shape=[shp, shp, shp, shp],
        in_specs=[vm, vm, vm, vm], out_specs=[vm, vm, vm, vm],
        scratch_shapes=[pltpu.VMEM((N_DEV,) + g.shape, F32), pltpu.SemaphoreType.DMA((7,)), pltpu.SemaphoreType.DMA((7,))],
    )(g, w, m, v)


def _pad_cols(a, width):
    return jnp.pad(a, ((0, 0), (0, width - a.shape[1])))


def _local_step(x, target, norm_mix_w, w_in_t, conv_w, conv_b, dt_bias, a_log, d_skip, ssd_norm_w, f_bias,
                late_shards, norm_mlp_w, norm_final_w):
    bsz, seq, _ = x.shape
    n = bsz * seq
    x2 = x.reshape(n, D_MODEL)
    t2 = target.reshape(n, D_MODEL)
    nfw = norm_final_w.reshape(1, D_MODEL)

    bias = _pad_cols(dt_bias, LANES)
    arow = _pad_cols(-jnp.exp(a_log), LANES)
    biast, acol = bias.reshape(LANES, 1), arow.reshape(LANES, 1)
    dfull = jnp.repeat(d_skip, HEAD_DIM, axis=1)
    fb = jnp.pad(f_bias, ((0, 0), (F_COL, LANES - 2 * F_COL)))

    wt_z, wt_xbc, wt_qkv = w_in_t[:ROW_XBC], w_in_t[ROW_XBC:ROW_DT], w_in_t[ROW_Q:ROW_F]
    wt_dtf = jnp.concatenate([w_in_t[ROW_DT:ROW_Q], w_in_t[ROW_F:], jnp.zeros((LANES - 2 * F_COL, D_MODEL), BF16)], axis=0)
    wt_q, wt_k, wt_v = wt_qkv[:D_MODEL], wt_qkv[D_MODEL:2 * D_MODEL], wt_qkv[2 * D_MODEL:]

    h1 = _rms_fwd(x2, norm_mix_w, "rms_fwd_mix")
    z = _mm([(h1, wt_z)], "inproj_z", F32, 1024, 1024, nt=True)
    xbc_raw = _mm([(h1, wt_xbc)], "inproj_xbc", F32, 512, 1536, nt=True)
    qkv = _mm([(h1, wt_qkv)], "inproj_qkv", BF16, 512, 3072, nt=True)
    dtf = _mm([(h1, wt_dtf)], "inproj_dtf", F32, 2048, LANES, nt=True)
    dtft = dtf.reshape(bsz, seq, LANES).transpose(0, 2, 1)

    xbc = _conv_fwd(xbc_raw, conv_w, conv_b, bsz, seq)
    y_pre, hall = _ssd_fwd(xbc, dtf, dtft, bias, biast, arow, acol, dfull, bsz, seq)

    ccol = _fox_prep(dtf, fb, bsz, seq)
    o_att, lse, w_out, w_up_t, w_down = _att_fwd(qkv, ccol, bsz, seq, late_shards)
    w_out, w_up_t, w_down = (w.reshape(-1, D_MODEL) for w in (w_out, w_up_t, w_down))

    ycat = _gnorm_fwd(y_pre, z, o_att, ssd_norm_w)
    h2, h2n = _mm([(ycat, w_out)], "outproj", F32, 512, 1024, res=x2, rms_fwd=norm_mlp_w)
    pre = _mm([(h2n, w_up_t)], "mlp_up", BF16, 512, 4096, nt=True)

    dh3, loss_row, g_nfw = _mlp_down_loss(pre, w_down, h2, t2, nfw)
    dpre, u = _mlp_down_bwd(dh3, w_down, pre)
    g_w_down = _mm_tn(u, dh3, "grad_w_down", 1024, 1024, 2048)
    g_w_up_t = _mm_tn(dpre, h2n, "grad_w_up", 1024, 1024, 2048)
    by_shard = lambda g: g.reshape(N_DEV, g.shape[0] // N_DEV, D_MODEL)
    dh2, g_nmlp = _mm([(dpre, w_up_t)], "mlp_up_bwd", F32, 512, 1024, res=dh3, rms_bwd=(h2, norm_mlp_w))

    g_w_out = _mm_tn(ycat, dh2, "grad_w_out", 1024, 1024, 2048)
    dycat = _mm([(dh2, w_out)], "outproj_bwd", F32, 512, 2048, nt=True)
    dy_pre, dz, g_ssdn = _gnorm_bwd(dycat, y_pre, z, ssd_norm_w)
    dq, dk, dv, dck, dcq, recv_down, recv_up_t, recv_out = _att_bwd(
        qkv, dycat, o_att, lse, ccol, bsz, seq, [by_shard(g_w_down), by_shard(g_w_up_t), by_shard(g_w_out)])

    dxbc, ddt, ssd_acc = _ssd_bwd(dy_pre, xbc, dtf, dtft, bias, biast, arow, acol, dfull, hall, bsz, seq)
    dxbc_raw, g_cw, g_cb = _conv_bwd(dxbc, xbc_raw, conv_w, conv_b, bsz, seq)

    def head_cols(rows):
        cols = rows[:, :, :2, :].reshape(bsz, SSD_HEADS, seq).transpose(0, 2, 1).reshape(n, SSD_HEADS)
        return jnp.pad(cols, ((0, 0), (F_COL, LANES - 2 * F_COL)))

    ddtf, g_fb = _fox_prep_bwd(head_cols(dck), head_cols(dcq), ddt, dtf, fb, bsz, seq)

    g_dtf = _mm_tn(ddtf, h1, "grad_w_in_dtf", LANES, 1024, 2048)
    g_w_in_t = jnp.concatenate([
        _mm_tn(dz, h1, "grad_w_in_z", 1024, 1024, 2048), _mm_tn(dxbc_raw, h1, "grad_w_in_xbc", 768, 1024, 2048),
        g_dtf[:F_COL], _mm_tn(dq, h1, "grad_w_in_q", 1024, 1024, 2048), _mm_tn(dk, h1, "grad_w_in_k", 1024, 1024, 2048),
        _mm_tn(dv, h1, "grad_w_in_v", 1024, 1024, 2048), g_dtf[F_COL:2 * F_COL]], axis=0)
    pieces = [(dz, wt_z), (dxbc_raw, wt_xbc), (dq, wt_q), (dk, wt_k), (dv, wt_v), (ddtf, wt_dtf)]
    dx, g_nmix, recv_in_t = _mm(pieces, "inproj_bwd", F32, 256, 1024, res=dh2, rms_bwd=(x2, norm_mix_w),
                                exchange=("scatter", [by_shard(g_w_in_t)]))

    small = dict(
        norm_mix_w=g_nmix, norm_mlp_w=g_nmlp, norm_final_w=g_nfw, ssd_norm_w=g_ssdn, conv_b=g_cb, conv_w=g_cw,
        dt_bias=ssd_acc[16:17, :SSD_HEADS], a_log=ssd_acc[0:1, :SSD_HEADS], d_skip=ssd_acc[8:9, :SSD_HEADS],
        f_bias=g_fb[0:1, F_COL:2 * F_COL])
    recv = dict(w_in_t=recv_in_t, w_out=recv_out, w_up_t=recv_up_t, w_down=recv_down)
    return loss_row[0, 0], dx.reshape(bsz, seq, D_MODEL), small, recv


SMALL_LAYOUT = (("norm_mix_w", 0, 1, 0, D_MODEL), ("norm_mlp_w", 1, 1, 0, D_MODEL), ("norm_final_w", 2, 1, 0, D_MODEL),
                ("ssd_norm_w", 3, 1, 0, D_MODEL), ("conv_b", 4, 1, 0, CONV_CH), ("conv_w", 5, CONV_K, 0, CONV_CH),
                ("dt_bias", 9, 1, 0, SSD_HEADS), ("a_log", 9, 1, LANES, SSD_HEADS), ("d_skip", 9, 1, 2 * LANES, SSD_HEADS),
                ("f_bias", 9, 1, 3 * LANES, SSD_HEADS))
SMALL_ROWS = 2 * SUBLANES


def _pack_small(vals):
    packed = jnp.zeros((SMALL_ROWS, SMALL_COLS), F32)
    for name, r0, nrows, c0, width in SMALL_LAYOUT:
        packed = packed.at[r0:r0 + nrows, c0:c0 + width].set(vals[name].reshape(nrows, width))
    return packed


def _unpack_small(packed, like):
    out = {}
    for name, r0, nrows, c0, width in SMALL_LAYOUT:
        out[name] = packed[r0:r0 + nrows, c0:c0 + width].reshape(like[name].shape)
    return out


def kernel(x, norm_mix_w, w_in, conv_w, conv_b, dt_bias, a_log, d_skip, ssd_norm_w, f_bias, w_out, norm_mlp_w, w_up, w_down, norm_final_w, loss_target, m_norm_mix_w, m_w_in, m_conv_w, m_conv_b, m_dt_bias, m_a_log, m_d_skip, m_ssd_norm_w, m_f_bias, m_w_out, m_norm_mlp_w, m_w_up, m_w_down, m_norm_final_w, v_norm_mix_w, v_w_in, v_conv_w, v_conv_b, v_dt_bias, v_a_log, v_d_skip, v_ssd_norm_w, v_f_bias, v_w_out, v_norm_mlp_w, v_w_up, v_w_down, v_norm_final_w):
    me = 4 * lax.axis_index("x") + 2 * lax.axis_index("y") + lax.axis_index("c")
    conv_shard_w = CONV_CH // N_DEV
    zero = jnp.zeros((), jnp.int32)

    def place_conv(shard):
        return lax.dynamic_update_slice(jnp.zeros((CONV_K, CONV_CH), F32), shard[0], (zero, me * conv_shard_w))

    (g_in_t,) = _all_gather_two_level([w_in[0].T.astype(BF16)])
    late_shards = [w_out[0].astype(BF16), w_up[0].T.astype(BF16), w_down[0].astype(BF16)]
    conv_full = _small_gather(jnp.pad(place_conv(conv_w), ((0, SUBLANES - CONV_K), (0, 0))))[:CONV_K]

    loss_local, grad_x, g_small, recv = _local_step(
        x, loss_target, norm_mix_w, g_in_t.reshape(IN_WIDTH, D_MODEL), conv_full, conv_b, dt_bias, a_log, d_skip,
        ssd_norm_w, f_bias, late_shards, norm_mlp_w, norm_final_w)
    loss = lax.psum(loss_local, MESH_AXES)

    grad_in = _sum_parts(recv["w_in_t"], "sum_w_in").T[None]
    grad_up = _sum_parts(recv["w_up_t"], "sum_w_up").T[None]
    big = {
        "w_in": _adamw_parts(grad_in, w_in[0], m_w_in[0], v_w_in[0], "adamw_w_in"),
        "w_out": _adamw_parts(recv["w_out"], w_out[0], m_w_out[0], v_w_out[0], "adamw_w_out"),
        "w_up": _adamw_parts(grad_up, w_up[0], m_w_up[0], v_w_up[0], "adamw_w_up"),
        "w_down": _adamw_parts(recv["w_down"], w_down[0], m_w_down[0], v_w_down[0], "adamw_w_down"),
    }

    like = dict(norm_mix_w=norm_mix_w, norm_mlp_w=norm_mlp_w, norm_final_w=norm_final_w, ssd_norm_w=ssd_norm_w,
                conv_b=conv_b, conv_w=jnp.zeros((1, CONV_K, CONV_CH), F32), dt_bias=dt_bias, a_log=a_log,
                d_skip=d_skip, f_bias=f_bias)
    w_small = dict(like, conv_w=place_conv(conv_w))
    m_small = dict(norm_mix_w=m_norm_mix_w, norm_mlp_w=m_norm_mlp_w, norm_final_w=m_norm_final_w,
                   ssd_norm_w=m_ssd_norm_w, conv_b=m_conv_b, conv_w=place_conv(m_conv_w), dt_bias=m_dt_bias,
                   a_log=m_a_log, d_skip=m_d_skip, f_bias=m_f_bias)
    v_small = dict(norm_mix_w=v_norm_mix_w, norm_mlp_w=v_norm_mlp_w, norm_final_w=v_norm_final_w,
                   ssd_norm_w=v_ssd_norm_w, conv_b=v_conv_b, conv_w=place_conv(v_conv_w), dt_bias=v_dt_bias,
                   a_log=v_a_log, d_skip=v_d_skip, f_bias=v_f_bias)
    small = _small_reduce_adamw(_pack_small(g_small), _pack_small(w_small), _pack_small(m_small), _pack_small(v_small))
    small = [_unpack_small(s, like) for s in small]
    for s in small:
        s["conv_w"] = lax.dynamic_slice(s["conv_w"], (zero, zero, me * conv_shard_w), (1, CONV_K, conv_shard_w))

    order = ["norm_mix_w", "w_in", "conv_w", "conv_b", "dt_bias", "a_log", "d_skip", "ssd_norm_w", "f_bias", "w_out",
             "norm_mlp_w", "w_up", "w_down", "norm_final_w"]
    outs = [loss, grad_x]
    for kind in range(4):
        for name in order:
            outs.append(big[name][kind][None] if name in big else small[kind][name])
    return tuple(outs)
```

```python
import jax
import jax.numpy as jnp
from jax import lax
from jax.experimental import pallas as pl
from jax.experimental.pallas import tpu as pltpu

F32, BF16 = jnp.float32, jnp.bfloat16

D_MODEL = 1024
SSD_HEADS = 16
HEAD_DIM = 64
SSD_STATE = 128
CHUNK = 128
CONV_CH = 1536
CONV_K = 4
D_FF = 4096
MIX_WIDTH = 2048
IN_WIDTH = 5664
NORM_EPS = 1e-5
ATT_SCALE = 0.125

LANES = 128
SUBLANES = 8
HEAD_PAIRS = SSD_HEADS // 2
NEG = -1e30
VMEM_LIMIT = 52 * 1024 * 1024

ROW_XBC, ROW_DT, ROW_Q, ROW_F = 1024, 2560, 2576, 5648
F_COL = SSD_HEADS

N_DEV = 8
MESH_AXES = ("x", "y", "c")
MESH = pl.DeviceIdType.MESH

ADAM_LR, ADAM_B1, ADAM_B2, ADAM_EPS, ADAM_WD, ADAM_STEP = 0.001, 0.9, 0.999, 1e-08, 0.01, 10
ADAM_C1 = 1.0 - ADAM_B1 ** ADAM_STEP
ADAM_C2 = 1.0 - ADAM_B2 ** ADAM_STEP

SMALL_COLS = CONV_CH


def _params(sem=None):
    return pltpu.CompilerParams(dimension_semantics=sem, vmem_limit_bytes=VMEM_LIMIT)


def _sigmoid(u):
    return 1.0 / (1.0 + jnp.exp(-u))


def _softplus(u):
    return jnp.maximum(u, 0.0) + jnp.log(1.0 + jnp.exp(-jnp.abs(u)))


def _log_sigmoid(u):
    return jnp.minimum(u, 0.0) - jnp.log(1.0 + jnp.exp(-jnp.abs(u)))


def _bdot(a, b):
    return jnp.dot(a.astype(BF16), b.astype(BF16), preferred_element_type=F32)


def _bdot_nt(a, b):
    return lax.dot_general(a.astype(BF16), b.astype(BF16), (((1,), (1,)), ((), ())), preferred_element_type=F32)


def _bdot_tn(a, b):
    return lax.dot_general(a.astype(BF16), b.astype(BF16), (((0,), (0,)), ((), ())), preferred_element_type=F32)


def _split3(x):
    x1 = x.astype(BF16)
    r1 = x - x1.astype(F32)
    x2 = r1.astype(BF16)
    return x1, x2, (r1 - x2.astype(F32)).astype(BF16)


def _dot_sel(x, sel):
    s = sel.astype(BF16)
    p1, p2, p3 = (jnp.dot(p, s, preferred_element_type=F32) for p in _split3(x))
    return p1 + p2 + p3


def _sel_dot(sel, x):
    s = sel.astype(BF16)
    p1, p2, p3 = (jnp.dot(s, p, preferred_element_type=F32) for p in _split3(x))
    return p1 + p2 + p3


def _iota(shape, dim):
    return lax.broadcasted_iota(jnp.int32, shape, dim)


def _head_expand():
    return (jnp.right_shift(_iota((LANES, D_MODEL), 1), 6) == _iota((LANES, D_MODEL), 0)).astype(F32)


def _head_reduce():
    return (jnp.right_shift(_iota((D_MODEL, LANES), 0), 6) == _iota((D_MODEL, LANES), 1)).astype(F32)


def _rms_fwd(x, w, name):
    n, d = x.shape
    tm = min(1024, n)

    def body(x_ref, w_ref, o_ref):
        xv = x_ref[...]
        r = lax.rsqrt(jnp.mean(xv * xv, axis=-1, keepdims=True) + NORM_EPS)
        o_ref[...] = (xv * r * w_ref[...]).astype(BF16)

    return pl.pallas_call(
        body, name=name, grid=(n // tm,),
        in_specs=[pl.BlockSpec((tm, d), lambda i: (i, 0)), pl.BlockSpec((1, d), lambda i: (0, 0))],
        out_specs=pl.BlockSpec((tm, d), lambda i: (i, 0)),
        out_shape=jax.ShapeDtypeStruct((n, d), BF16),
        compiler_params=_params(("parallel",)),
    )(x, w)


def _mm(pairs, name, out_dtype, tm, tn, nt=False, res=None, exchange=None, rms_fwd=None, rms_bwd=None):
    m = pairs[0][0].shape[0]
    n = pairs[0][1].shape[0 if nt else 1]
    tm, tn = min(tm, m), min(tn, n)
    gm, gn = m // tm, n // tn
    npairs = len(pairs)
    kind, xs = (None, []) if exchange is None else exchange
    nx = len(xs)
    extra_in = [] if res is None else [res]
    if rms_bwd is not None:
        extra_in += list(rms_bwd)
    elif rms_fwd is not None:
        extra_in.append(rms_fwd)
    n_in = 2 * npairs + len(extra_in)
    n_out = 1 + (rms_fwd is not None or rms_bwd is not None)
    assert (rms_fwd is None and rms_bwd is None) or tn == n

    def body(*refs):
        x_refs, o_refs = refs[n_in:n_in + nx], refs[n_in + nx:n_in + nx + n_out]
        xo_refs, sems = refs[n_in + nx + n_out:n_in + 2 * nx + n_out], refs[n_in + 2 * nx + n_out:]
        extra = refs[2 * npairs:n_in]
        i, j = pl.program_id(0), pl.program_id(1)
        if nx:
            @pl.when((i == 0) & (j == 0))
            def _():
                _exchange_start(kind, x_refs, xo_refs, *sems)

        acc = None
        for p in range(npairs):
            a_v, b_v = refs[2 * p][...], refs[2 * p + 1][...]
            d = _bdot_nt(a_v, b_v) if nt else _bdot(a_v, b_v)
            acc = d if acc is None else acc + d
        if rms_bwd is not None:
            xv = extra[1][...]
            r = lax.rsqrt(jnp.mean(xv * xv, axis=-1, keepdims=True) + NORM_EPS)
            nrm = xv * r
            g = acc * extra[2][...]
            o_refs[0][...] = extra[0][...] + r * (g - nrm * jnp.mean(g * nrm, axis=-1, keepdims=True))

            @pl.when(i == 0)
            def _():
                o_refs[1][...] = jnp.zeros_like(o_refs[1])

            o_refs[1][...] += jnp.sum(acc * nrm, axis=0, keepdims=True)
        else:
            if res is not None:
                acc = extra[0][...] + acc
            o_refs[0][...] = acc.astype(o_refs[0].dtype)
            if rms_fwd is not None:
                r = lax.rsqrt(jnp.mean(acc * acc, axis=-1, keepdims=True) + NORM_EPS)
                o_refs[1][...] = (acc * r * extra[-1][...]).astype(BF16)
        if nx:
            @pl.when((i == gm - 1) & (j == gn - 1))
            def _():
                _exchange_wait(kind, x_refs, xo_refs, *sems)

    tile = pl.BlockSpec((tm, tn), lambda i, j: (i, j))
    row = pl.BlockSpec((1, tn), lambda i, j: (0, j))
    in_specs, args = [], []
    for a, b in pairs:
        k = a.shape[1]
        in_specs.append(pl.BlockSpec((tm, k), lambda i, j: (i, 0)))
        in_specs.append(pl.BlockSpec((tn, k), lambda i, j: (j, 0)) if nt else pl.BlockSpec((k, tn), lambda i, j: (0, j)))
        args += [a, b]
    in_specs += [row if e.shape[0] == 1 else tile for e in extra_in]
    out_specs, out_shape = [tile], [jax.ShapeDtypeStruct((m, n), out_dtype)]
    if rms_bwd is not None:
        out_specs.append(row)
        out_shape.append(jax.ShapeDtypeStruct((1, n), F32))
    elif rms_fwd is not None:
        out_specs.append(tile)
        out_shape.append(jax.ShapeDtypeStruct((m, n), BF16))
    hbm = pl.BlockSpec(memory_space=pl.ANY)
    sequential = nx or rms_bwd is not None
    outs = pl.pallas_call(
        body, name=name, grid=(gm, gn), in_specs=in_specs + [hbm] * nx,
        out_specs=out_specs + [hbm] * nx,
        out_shape=out_shape + _exchange_shapes(kind, xs),
        scratch_shapes=_exchange_scratch(nx),
        compiler_params=_params(("arbitrary", "arbitrary") if sequential else ("parallel", "parallel")),
    )(*args, *extra_in, *xs)
    return outs if len(outs) > 1 else outs[0]


def _mm_tn(a, b, name, tm, tn, tk):
    t, m = a.shape
    n = b.shape[1]
    tm, tn, tk = min(tm, m), min(tn, n), min(tk, t)
    nk = t // tk

    def body(a_ref, b_ref, o_ref, acc_ref):
        kk = pl.program_id(2)

        @pl.when(kk == 0)
        def _():
            acc_ref[...] = jnp.zeros_like(acc_ref)

        acc_ref[...] += _bdot_tn(a_ref[...], b_ref[...])

        @pl.when(kk == nk - 1)
        def _():
            o_ref[...] = acc_ref[...].astype(o_ref.dtype)

    return pl.pallas_call(
        body, name=name, grid=(m // tm, n // tn, nk),
        in_specs=[pl.BlockSpec((tk, tm), lambda i, j, kk: (kk, i)), pl.BlockSpec((tk, tn), lambda i, j, kk: (kk, j))],
        out_specs=pl.BlockSpec((tm, tn), lambda i, j, kk: (i, j)),
        out_shape=jax.ShapeDtypeStruct((m, n), BF16),
        scratch_shapes=[pltpu.VMEM((tm, tn), F32)],
        compiler_params=_params(("parallel", "parallel", "arbitrary")),
    )(a, b)


def _mlp_down_loss(pre, w_down, h2, target, w):
    m, k = pre.shape
    d = w_down.shape[1]
    tm = min(512, m)

    def body(p_ref, b_ref, r_ref, t_ref, w_ref, dh_ref, loss_ref, gw_ref):
        u = jnp.square(jnp.maximum(p_ref[...].astype(F32), 0.0))
        xv = r_ref[...] + _bdot(u, b_ref[...])
        r = lax.rsqrt(jnp.mean(xv * xv, axis=-1, keepdims=True) + NORM_EPS)
        nrm = xv * r
        wv = w_ref[...]
        err = nrm * wv - t_ref[...]
        part = 0.5 * jnp.sum(jnp.mean(err * err, axis=-1, keepdims=True), axis=0, keepdims=True)
        dy = err * (1.0 / d)
        g = dy * wv
        dh_ref[...] = r * (g - nrm * jnp.mean(g * nrm, axis=-1, keepdims=True))

        @pl.when(pl.program_id(0) == 0)
        def _():
            loss_ref[...] = jnp.zeros_like(loss_ref)
            gw_ref[...] = jnp.zeros_like(gw_ref)

        loss_ref[...] += jnp.broadcast_to(part, loss_ref.shape)
        gw_ref[...] += jnp.sum(dy * nrm, axis=0, keepdims=True)

    tok = pl.BlockSpec((tm, d), lambda i: (i, 0))
    row = pl.BlockSpec((1, d), lambda i: (0, 0))
    return pl.pallas_call(
        body, name="mlp_down_loss", grid=(m // tm,),
        in_specs=[pl.BlockSpec((tm, k), lambda i: (i, 0)), pl.BlockSpec((k, d), lambda i: (0, 0)), tok, tok, row],
        out_specs=[tok, pl.BlockSpec((1, LANES), lambda i: (0, 0)), row],
        out_shape=[jax.ShapeDtypeStruct((m, d), F32), jax.ShapeDtypeStruct((1, LANES), F32),
                   jax.ShapeDtypeStruct((1, d), F32)],
        compiler_params=_params(("arbitrary",)),
    )(pre, w_down, h2, target, w)


def _mlp_down_bwd(dh3, w_down, pre):
    m, k = dh3.shape
    n = w_down.shape[0]
    tm, tn = min(256, m), n

    def body(a_ref, b_ref, p_ref, dpre_ref, u_ref):
        r = jnp.maximum(p_ref[...].astype(F32), 0.0)
        du = _bdot_nt(a_ref[...], b_ref[...])
        dpre_ref[...] = (du * (2.0 * r)).astype(BF16)
        u_ref[...] = (r * r).astype(BF16)

    blk = pl.BlockSpec((tm, tn), lambda i, j: (i, j))
    return pl.pallas_call(
        body, name="mlp_down_bwd", grid=(m // tm, n // tn),
        in_specs=[pl.BlockSpec((tm, k), lambda i, j: (i, 0)), pl.BlockSpec((tn, k), lambda i, j: (j, 0)), blk],
        out_specs=[blk, blk],
        out_shape=[jax.ShapeDtypeStruct((m, n), BF16), jax.ShapeDtypeStruct((m, n), BF16)],
        compiler_params=_params(("parallel", "parallel")),
    )(dh3, w_down, pre)


CONV_TC = 512


def _conv_fwd(xbc, cw, cb, bsz, seq):
    tt = min(512, seq)
    nt, hb = seq // tt, tt // SUBLANES
    x3 = xbc.reshape(bsz, seq, CONV_CH)

    def body(xm_ref, xh_ref, w_ref, b_ref, o_ref):
        i = pl.program_id(2)
        x = xm_ref[0]
        halo = jnp.where(i > 0, xh_ref[0], 0.0)
        xx = jnp.concatenate([halo, x], axis=0)
        acc = x * w_ref[3:4, :] + b_ref[...]
        for s in range(1, CONV_K):
            acc = acc + pltpu.roll(xx, s, 0)[SUBLANES:, :] * w_ref[3 - s:4 - s, :]
        o_ref[0] = acc * _sigmoid(acc)

    out = pl.pallas_call(
        body, name="conv_fwd", grid=(CONV_CH // CONV_TC, bsz, nt),
        in_specs=[pl.BlockSpec((1, tt, CONV_TC), lambda c, b, i: (b, i, c)),
                  pl.BlockSpec((1, SUBLANES, CONV_TC), lambda c, b, i: (b, jnp.maximum(i * hb - 1, 0), c)),
                  pl.BlockSpec((CONV_K, CONV_TC), lambda c, b, i: (0, c)),
                  pl.BlockSpec((1, CONV_TC), lambda c, b, i: (0, c))],
        out_specs=pl.BlockSpec((1, tt, CONV_TC), lambda c, b, i: (b, i, c)),
        out_shape=jax.ShapeDtypeStruct((bsz, seq, CONV_CH), F32),
        compiler_params=_params(("parallel", "parallel", "parallel")),
    )(x3, x3, cw, cb)
    return out.reshape(bsz * seq, CONV_CH)


def _conv_bwd(da, xbc, cw, cb, bsz, seq):
    tt = min(512, seq)
    nt, hb = seq // tt, tt // SUBLANES
    x3 = xbc.reshape(bsz, seq, CONV_CH)
    da3 = da.reshape(bsz, seq, CONV_CH)
    n2 = tt + SUBLANES

    def body(dam_ref, daa_ref, xm_ref, xb_ref, xa_ref, w_ref, b_ref, du_ref, gw_ref, gb_ref):
        bi, i = pl.program_id(1), pl.program_id(2)
        before = jnp.where(i > 0, xb_ref[0], 0.0)
        after = jnp.where(i < nt - 1, xa_ref[0], 0.0)
        xx = jnp.concatenate([before, xm_ref[0], after], axis=0)
        rolled = [xx] + [pltpu.roll(xx, s, 0) for s in range(1, CONV_K)]
        pre = b_ref[...]
        for s in range(CONV_K):
            pre = pre + rolled[s][SUBLANES:, :] * w_ref[3 - s:4 - s, :]
        da_ext = jnp.concatenate([dam_ref[0], jnp.where(i < nt - 1, daa_ref[0], 0.0)], axis=0)
        sg = _sigmoid(pre)
        dpre = da_ext * sg * (1.0 + pre * (1.0 - sg))
        du = dpre[:tt, :] * w_ref[3:4, :]
        for s in range(1, CONV_K):
            du = du + pltpu.roll(dpre, n2 - s, 0)[:tt, :] * w_ref[3 - s:4 - s, :]
        du_ref[0] = du.astype(BF16)
        dpm = dpre[:tt, :]
        gws = [jnp.sum(dpm * rolled[3 - kk][SUBLANES:SUBLANES + tt, :], axis=0, keepdims=True) for kk in range(CONV_K)]

        @pl.when((bi == 0) & (i == 0))
        def _():
            gw_ref[...] = jnp.zeros_like(gw_ref)
            gb_ref[...] = jnp.zeros_like(gb_ref)

        gw_ref[...] += jnp.concatenate(gws, axis=0)
        gb_ref[...] += jnp.sum(dpm, axis=0, keepdims=True)

    main = pl.BlockSpec((1, tt, CONV_TC), lambda c, b, i: (b, i, c))
    prev = pl.BlockSpec((1, SUBLANES, CONV_TC), lambda c, b, i: (b, jnp.maximum(i * hb - 1, 0), c))
    nxt = pl.BlockSpec((1, SUBLANES, CONV_TC), lambda c, b, i: (b, jnp.minimum((i + 1) * hb, seq // SUBLANES - 1), c))
    du, gw, gb = pl.pallas_call(
        body, name="conv_bwd", grid=(CONV_CH // CONV_TC, bsz, nt),
        in_specs=[main, nxt, main, prev, nxt,
                  pl.BlockSpec((CONV_K, CONV_TC), lambda c, b, i: (0, c)),
                  pl.BlockSpec((1, CONV_TC), lambda c, b, i: (0, c))],
        out_specs=[main, pl.BlockSpec((CONV_K, CONV_TC), lambda c, b, i: (0, c)),
                   pl.BlockSpec((1, CONV_TC), lambda c, b, i: (0, c))],
        out_shape=[jax.ShapeDtypeStruct((bsz, seq, CONV_CH), BF16), jax.ShapeDtypeStruct((CONV_K, CONV_CH), F32),
                   jax.ShapeDtypeStruct((1, CONV_CH), F32)],
        compiler_params=_params(("parallel", "arbitrary", "arbitrary")),
    )(da3, da3, x3, x3, x3, cw, cb)
    return du.reshape(bsz * seq, CONV_CH), gw, gb


def _ssd_constants():
    tri = (_iota((CHUNK, CHUNK), 1) <= _iota((CHUNK, CHUNK), 0)).astype(BF16)
    return tri, tri.T, _head_expand().astype(BF16), _head_reduce().astype(BF16)


def _ssd_prologue(dtf_ref, dtft_ref, bias_ref, biast_ref, arow_ref, acol_ref, tri_ref, triu_ref, expand_ref,
                  dtfull_scr, acfull_scr, acr_scr):
    dtc = _softplus(dtf_ref[0] + bias_ref[...])
    a = dtc * arow_ref[...]
    acum = _sel_dot(tri_ref[...], a)
    expand = expand_ref[...]
    dtfull_scr[...] = _dot_sel(dtc, expand)
    acfull_scr[...] = _dot_sel(acum, expand)
    dtr = _softplus(dtft_ref[...] + biast_ref[...])
    acr_scr[...] = _dot_sel(dtr * acol_ref[...], triu_ref[...])
    return dtc, acum


def _decay_matrix(acum, acr_scr, h):
    lane = _iota((CHUNK, LANES), 1)
    ac_col = jnp.sum(jnp.where(lane == h, acum, 0.0), axis=1, keepdims=True)
    ac_row = acr_scr[h:h + 1, :]
    causal = _iota((CHUNK, CHUNK), 1) <= _iota((CHUNK, CHUNK), 0)
    return jnp.exp(jnp.where(causal, ac_col - ac_row, NEG))


def _ssd_fwd(xbc, dtf, dtft, bias, biast, arow, acol, dfull, bsz, seq):
    nc = seq // CHUNK
    x3 = xbc.reshape(bsz, seq, CONV_CH)
    tri, triu, expand, _ = _ssd_constants()

    def body(xbc_ref, dtf_ref, dtft_ref, bias_ref, biast_ref, arow_ref, acol_ref, dfull_ref,
             tri_ref, triu_ref, expand_ref, y_ref, hall_ref, h_scr, dtfull_scr, acfull_scr, acr_scr):
        @pl.when(pl.program_id(1) == 0)
        def _():
            h_scr[...] = jnp.zeros_like(h_scr)

        _, acum = _ssd_prologue(dtf_ref, dtft_ref, bias_ref, biast_ref, arow_ref, acol_ref,
                                tri_ref, triu_ref, expand_ref, dtfull_scr, acfull_scr, acr_scr)
        lane = _iota((CHUNK, LANES), 1)
        for g in range(2):
            bg = xbc_ref[0, :, D_MODEL + LANES * g:D_MODEL + LANES * (g + 1)]
            cg = xbc_ref[0, :, D_MODEL + 2 * LANES + LANES * g:D_MODEL + 2 * LANES + LANES * (g + 1)]
            cg_bf = cg.astype(BF16)
            gmat = _bdot_nt(cg_bf, bg)
            bgt_bf = bg.T.astype(BF16)
            for j in range(4):
                p = 4 * g + j
                sl = slice(LANES * p, LANES * (p + 1))
                xs = xbc_ref[0, :, sl]
                ac = acfull_scr[:, sl]
                acl = acfull_scr[CHUNK - 1:CHUNK, sl]
                xdt = xs * dtfull_scr[:, sl]
                xdt_bf = xdt.astype(BF16)
                hp = h_scr[p]
                hall_ref[0, 0, p] = hp
                yo = _bdot(cg_bf, hp) * jnp.exp(ac)
                yd = []
                for hh in range(2):
                    mmat = gmat * _decay_matrix(acum, acr_scr, 2 * p + hh)
                    yd.append(_bdot(mmat, xdt_bf))
                y_ref[0, :, sl] = jnp.where(lane < HEAD_DIM, yd[0], yd[1]) + yo + dfull_ref[:, sl] * xs
                h_scr[p] = hp * jnp.exp(acl) + _bdot(bgt_bf, xdt * jnp.exp(acl - ac))

    row = lambda w: pl.BlockSpec((1, w), lambda b, c: (0, 0))
    whole = lambda a: pl.BlockSpec(a.shape, lambda b, c: (0, 0))
    y, hall = pl.pallas_call(
        body, name="ssd_fwd", grid=(bsz, nc),
        in_specs=[pl.BlockSpec((1, CHUNK, CONV_CH), lambda b, c: (b, c, 0)),
                  pl.BlockSpec((1, CHUNK, LANES), lambda b, c: (b, c, 0)),
                  pl.BlockSpec((LANES, CHUNK), lambda b, c: (0, b * nc + c)),
                  row(LANES), pl.BlockSpec((LANES, 1), lambda b, c: (0, 0)),
                  row(LANES), pl.BlockSpec((LANES, 1), lambda b, c: (0, 0)), row(D_MODEL),
                  whole(tri), whole(triu), whole(expand)],
        out_specs=[pl.BlockSpec((1, CHUNK, D_MODEL), lambda b, c: (b, c, 0)),
                   pl.BlockSpec((1, 1, HEAD_PAIRS, SSD_STATE, LANES), lambda b, c: (b, c, 0, 0, 0))],
        out_shape=[jax.ShapeDtypeStruct((bsz, seq, D_MODEL), F32),
                   jax.ShapeDtypeStruct((bsz, nc, HEAD_PAIRS, SSD_STATE, LANES), F32)],
        scratch_shapes=[pltpu.VMEM((HEAD_PAIRS, SSD_STATE, LANES), F32), pltpu.VMEM((CHUNK, D_MODEL), F32),
                        pltpu.VMEM((CHUNK, D_MODEL), F32), pltpu.VMEM((LANES, CHUNK), F32)],
        compiler_params=_params(("parallel", "arbitrary")),
    )(x3, dtf.reshape(bsz, seq, LANES), dtft, bias, biast, arow, acol, dfull, tri, triu, expand)
    return y.reshape(bsz * seq, D_MODEL), hall


def _ssd_bwd(dy, xbc, dtf, dtft, bias, biast, arow, acol, dfull, hall, bsz, seq):
    nc = seq // CHUNK
    x3 = xbc.reshape(bsz, seq, CONV_CH)
    dy3 = dy.reshape(bsz, seq, D_MODEL)
    consts = _ssd_constants()

    def body(dy_ref, xbc_ref, dtf_ref, dtft_ref, bias_ref, biast_ref, arow_ref, acol_ref, dfull_ref, hall_ref,
             tri_ref, triu_ref, expand_ref, reduce_ref,
             dx_ref, ddt_ref, acc_ref, dh_scr, dtfull_scr, acfull_scr, acr_scr, csum_scr, dacf_scr, ddtf_scr, ddl_scr):
        first = (pl.program_id(0) == 0) & (pl.program_id(1) == 0)

        @pl.when(first)
        def _():
            acc_ref[...] = jnp.zeros_like(acc_ref)

        @pl.when(pl.program_id(1) == 0)
        def _():
            dh_scr[...] = jnp.zeros_like(dh_scr)

        dtc, acum = _ssd_prologue(dtf_ref, dtft_ref, bias_ref, biast_ref, arow_ref, acol_ref,
                                  tri_ref, triu_ref, expand_ref, dtfull_scr, acfull_scr, acr_scr)
        csum_scr[...] = jnp.zeros_like(csum_scr)
        lane = _iota((CHUNK, LANES), 1)
        last_row = _iota((CHUNK, LANES), 0) == CHUNK - 1
        rs16 = jnp.zeros((CHUNK, LANES), F32)
        for g in range(2):
            bsl = slice(D_MODEL + LANES * g, D_MODEL + LANES * (g + 1))
            csl = slice(D_MODEL + 2 * LANES + LANES * g, D_MODEL + 2 * LANES + LANES * (g + 1))
            bg_bf = xbc_ref[0, :, bsl].astype(BF16)
            cg = xbc_ref[0, :, csl]
            cg_bf = cg.astype(BF16)
            cgt_bf = cg.T.astype(BF16)
            gmat = _bdot_nt(cg_bf, bg_bf)
            dg = jnp.zeros((CHUNK, CHUNK), F32)
            dbg = jnp.zeros((CHUNK, SSD_STATE), F32)
            dcg = jnp.zeros((CHUNK, SSD_STATE), F32)
            for j in range(4):
                p = 4 * g + j
                sl = slice(LANES * p, LANES * (p + 1))
                xs = xbc_ref[0, :, sl]
                dt = dtfull_scr[:, sl]
                ac = acfull_scr[:, sl]
                acl = acfull_scr[CHUNK - 1:CHUNK, sl]
                dyp = dy_ref[0, :, sl]
                xdt = xs * dt
                xdt_bf = xdt.astype(BF16)
                e = jnp.exp(ac)
                dsd = jnp.exp(acl - ac)
                cd = jnp.exp(acl)
                xds_bf = (xdt * dsd).astype(BF16)
                hp = hall_ref[0, 0, p]
                hp_bf = hp.astype(BF16)
                dhn = dh_scr[p]
                dhn_bf = dhn.astype(BF16)
                yo = _bdot(cg_bf, hp_bf) * e
                dw_bf = (dyp * e).astype(BF16)
                dcg = dcg + _bdot_nt(dw_bf, hp_bf)
                dhin = _bdot(cgt_bf, dw_bf)
                dac = dyp * yo
                dacl = jnp.sum(dhn * hp, axis=0, keepdims=True) * cd
                dxds = _bdot(bg_bf, dhn_bf)
                dbg = dbg + _bdot_nt(xds_bf, dhn_bf)
                dxdt = dxds * dsd
                tdec = dxds * xdt * dsd
                dacl = dacl + jnp.sum(tdec, axis=0, keepdims=True)
                dac = dac - tdec
                dh_scr[p] = dhn * cd + dhin
                for hh in range(2):
                    h = 2 * p + hh
                    lm = (lane < HEAD_DIM) if hh == 0 else (lane >= HEAD_DIM)
                    dec = _decay_matrix(acum, acr_scr, h)
                    mmat = gmat * dec
                    dyh_bf = jnp.where(lm, dyp, 0.0).astype(BF16)
                    dm = _bdot_nt(dyh_bf, xdt_bf)
                    dxdt = dxdt + _bdot(mmat.T, dyh_bf)
                    dg = dg + dm * dec
                    q = dm * mmat
                    rs16 = rs16 + jnp.where(lane == h, jnp.sum(q, axis=1, keepdims=True), 0.0)
                    csum_scr[h:h + 1, :] = jnp.sum(q, axis=0, keepdims=True)
                dx_ref[0, :, sl] = dfull_ref[:, sl] * dyp + dxdt * dt
                dacf_scr[:, sl] = dac + jnp.where(last_row, dacl, 0.0)
                ddtf_scr[:, sl] = dxdt * xs
                ddl_scr[:, sl] = jnp.broadcast_to(jnp.sum(dyp * xs, axis=0, keepdims=True), (SUBLANES, LANES))
            dg_bf = dg.astype(BF16)
            dx_ref[0, :, bsl] = dbg + _bdot(dg.T, cg_bf)
            dx_ref[0, :, csl] = dcg + _bdot(dg_bf, bg_bf)
        reduce, triu = reduce_ref[...], triu_ref[...]
        dac16 = _dot_sel(dacf_scr[...], reduce) + rs16 - csum_scr[...].T
        da16 = _sel_dot(triu, dac16)
        ddt16 = da16 * arow_ref[...] + _dot_sel(ddtf_scr[...], reduce)
        ddtraw = ddt16 * _sigmoid(dtf_ref[0] + bias_ref[...])
        ddt_ref[0] = ddtraw
        acc_ref[0:8, :] += jnp.broadcast_to(jnp.sum(da16 * dtc * arow_ref[...], axis=0, keepdims=True), (SUBLANES, LANES))
        acc_ref[8:16, :] += _dot_sel(ddl_scr[...], reduce)
        acc_ref[16:24, :] += jnp.broadcast_to(jnp.sum(ddtraw, axis=0, keepdims=True), (SUBLANES, LANES))

    rev = lambda b, c: (b, nc - 1 - c, 0)
    row = lambda w: pl.BlockSpec((1, w), lambda b, c: (0, 0))
    dx, ddt, acc = pl.pallas_call(
        body, name="ssd_bwd", grid=(bsz, nc),
        in_specs=[pl.BlockSpec((1, CHUNK, D_MODEL), rev), pl.BlockSpec((1, CHUNK, CONV_CH), rev),
                  pl.BlockSpec((1, CHUNK, LANES), rev),
                  pl.BlockSpec((LANES, CHUNK), lambda b, c: (0, b * nc + nc - 1 - c)),
                  row(LANES), pl.BlockSpec((LANES, 1), lambda b, c: (0, 0)),
                  row(LANES), pl.BlockSpec((LANES, 1), lambda b, c: (0, 0)), row(D_MODEL),
                  pl.BlockSpec((1, 1, HEAD_PAIRS, SSD_STATE, LANES), lambda b, c: (b, nc - 1 - c, 0, 0, 0))]
        + [pl.BlockSpec(a.shape, lambda b, c: (0, 0)) for a in consts],
        out_specs=[pl.BlockSpec((1, CHUNK, CONV_CH), rev), pl.BlockSpec((1, CHUNK, LANES), rev),
                   pl.BlockSpec((3 * SUBLANES, LANES), lambda b, c: (0, 0))],
        out_shape=[jax.ShapeDtypeStruct((bsz, seq, CONV_CH), F32), jax.ShapeDtypeStruct((bsz, seq, LANES), F32),
                   jax.ShapeDtypeStruct((3 * SUBLANES, LANES), F32)],
        scratch_shapes=[pltpu.VMEM((HEAD_PAIRS, SSD_STATE, LANES), F32), pltpu.VMEM((CHUNK, D_MODEL), F32),
                        pltpu.VMEM((CHUNK, D_MODEL), F32), pltpu.VMEM((LANES, CHUNK), F32),
                        pltpu.VMEM((LANES, CHUNK), F32), pltpu.VMEM((CHUNK, D_MODEL), F32),
                        pltpu.VMEM((CHUNK, D_MODEL), F32), pltpu.VMEM((SUBLANES, D_MODEL), F32)],
        compiler_params=_params(("arbitrary", "arbitrary")),
    )(dy3, x3, dtf.reshape(bsz, seq, LANES), dtft, bias, biast, arow, acol, dfull, hall, *consts)
    return dx.reshape(bsz * seq, CONV_CH), ddt.reshape(bsz * seq, LANES), acc


GROUP_W = D_MODEL // 2


def _gnorm_fwd(y, z, o_att, w):
    n = y.shape[0]
    tm = min(1024, n)

    def body(y_ref, z_ref, o_ref, w_ref, out_ref):
        zv = z_ref[...]
        g = y_ref[...] * (zv * _sigmoid(zv))
        for gi in range(2):
            sl = slice(GROUP_W * gi, GROUP_W * (gi + 1))
            gs = g[:, sl]
            r = lax.rsqrt(jnp.mean(gs * gs, axis=-1, keepdims=True) + NORM_EPS)
            out_ref[:, sl] = (gs * r * w_ref[:, sl]).astype(BF16)
        out_ref[:, D_MODEL:] = o_ref[...].astype(BF16)

    tok = pl.BlockSpec((tm, D_MODEL), lambda i: (i, 0))
    return pl.pallas_call(
        body, name="gnorm_fwd", grid=(n // tm,),
        in_specs=[tok, tok, tok, pl.BlockSpec((1, D_MODEL), lambda i: (0, 0))],
        out_specs=pl.BlockSpec((tm, MIX_WIDTH), lambda i: (i, 0)),
        out_shape=jax.ShapeDtypeStruct((n, MIX_WIDTH), BF16),
        compiler_params=_params(("parallel",)),
    )(y, z, o_att, w)


def _gnorm_bwd(dycat, y, z, w):
    n = y.shape[0]
    tm = min(512, n)

    def body(dn_ref, y_ref, z_ref, w_ref, dy_ref, dz_ref, gw_ref):
        zv, yv = z_ref[...], y_ref[...]
        sz = _sigmoid(zv)
        sil = zv * sz
        g = yv * sil

        @pl.when(pl.program_id(0) == 0)
        def _():
            gw_ref[...] = jnp.zeros_like(gw_ref)

        for gi in range(2):
            sl = slice(GROUP_W * gi, GROUP_W * (gi + 1))
            gs = g[:, sl]
            r = lax.rsqrt(jnp.mean(gs * gs, axis=-1, keepdims=True) + NORM_EPS)
            nrm = gs * r
            dyn = dn_ref[:, sl]
            dn = dyn * w_ref[:, sl]
            dgs = r * (dn - nrm * jnp.mean(dn * nrm, axis=-1, keepdims=True))
            dy_ref[:, sl] = dgs * sil[:, sl]
            dz_ref[:, sl] = (dgs * yv[:, sl] * (sz[:, sl] * (1.0 + zv[:, sl] * (1.0 - sz[:, sl])))).astype(BF16)
            gw_ref[:, sl] += jnp.sum(dyn * nrm, axis=0, keepdims=True)

    tok = pl.BlockSpec((tm, D_MODEL), lambda i: (i, 0))
    row = pl.BlockSpec((1, D_MODEL), lambda i: (0, 0))
    return pl.pallas_call(
        body, name="gnorm_bwd", grid=(n // tm,),
        in_specs=[tok, tok, tok, row], out_specs=[tok, tok, row],
        out_shape=[jax.ShapeDtypeStruct((n, D_MODEL), F32), jax.ShapeDtypeStruct((n, D_MODEL), BF16),
                   jax.ShapeDtypeStruct((1, D_MODEL), F32)],
        compiler_params=_params(("arbitrary",)),
    )(dycat, y, z, w)


AUX_C = 3
AUX_ONE = 3


def _aux_base(hh):
    return HEAD_DIM * (1 - hh)


def _fox_prep(dtf, fb, bsz, seq):
    def body(x_ref, b_ref, aux_ref):
        tri = (_iota((CHUNK, CHUNK), 1) <= _iota((CHUNK, CHUNK), 0)).astype(F32)
        row, col = _iota((LANES, D_MODEL), 0), _iota((LANES, D_MODEL), 1)
        lane = jnp.bitwise_and(col, LANES - 1)
        other = (lane < HEAD_DIM).astype(jnp.int32)
        term = lane - HEAD_DIM * (1 - other)
        src = F_COL + 2 * jnp.right_shift(col, 7) + other
        place = [jnp.where((row == src) & (term == i), -1.0, 0.0).astype(BF16) for i in range(AUX_C)]
        lane1 = jnp.bitwise_and(_iota((1, D_MODEL), 1), LANES - 1)
        ones_lane = (lane1 - HEAD_DIM * (1 - (lane1 < HEAD_DIM).astype(jnp.int32)) == AUX_ONE).astype(F32)
        carry = jnp.zeros((1, LANES), F32)
        for j in range(seq // CHUNK):
            sl = slice(CHUNK * j, CHUNK * (j + 1))
            lf = _log_sigmoid(x_ref[sl, :] + b_ref[...])
            c = _sel_dot(tri, lf) + carry
            carry = carry + jnp.sum(lf, axis=0, keepdims=True)
            t1, t2, t3 = (jnp.dot(t, p, preferred_element_type=F32) for t, p in zip(_split3(c), place))
            aux_ref[sl, :] = (t1 + t2 + t3 + ones_lane).astype(BF16)

    return pl.pallas_call(
        body, name="fox_prep", grid=(bsz,),
        in_specs=[pl.BlockSpec((seq, LANES), lambda b: (b, 0)), pl.BlockSpec((1, LANES), lambda b: (0, 0))],
        out_specs=pl.BlockSpec((seq, D_MODEL), lambda b: (b, 0)),
        out_shape=jax.ShapeDtypeStruct((bsz * seq, D_MODEL), BF16),
        compiler_params=_params(("parallel",)),
    )(dtf, fb)


def _fox_prep_bwd(dck, dcq, ddt, dtf, fb, bsz, seq):
    nj = seq // CHUNK

    def body(dck_ref, dcq_ref, ddt_ref, x_ref, b_ref, out_ref, gb_ref):
        triu = (_iota((CHUNK, CHUNK), 0) <= _iota((CHUNK, CHUNK), 1)).astype(F32)
        is_f = (_iota((CHUNK, LANES), 1) >= F_COL) & (_iota((CHUNK, LANES), 1) < 2 * F_COL)
        carry = jnp.zeros((1, LANES), F32)
        total = jnp.zeros((1, LANES), F32)
        for j in range(nj - 1, -1, -1):
            sl = slice(CHUNK * j, CHUNK * (j + 1))
            dcv = dck_ref[sl, :] + dcq_ref[sl, :]
            dlf = _sel_dot(triu, dcv) + carry
            carry = carry + jnp.sum(dcv, axis=0, keepdims=True)
            df = jnp.where(is_f, dlf * _sigmoid(-(x_ref[sl, :] + b_ref[...])), 0.0)
            out_ref[sl, :] = (df + ddt_ref[sl, :]).astype(BF16)
            total = total + jnp.sum(df, axis=0, keepdims=True)

        @pl.when(pl.program_id(0) == 0)
        def _():
            gb_ref[...] = jnp.zeros_like(gb_ref)

        gb_ref[...] += jnp.broadcast_to(total, (SUBLANES, LANES))

    blk = pl.BlockSpec((seq, LANES), lambda b: (b, 0))
    return pl.pallas_call(
        body, name="fox_prep_bwd", grid=(bsz,),
        in_specs=[blk, blk, blk, blk, pl.BlockSpec((1, LANES), lambda b: (0, 0))],
        out_specs=[blk, pl.BlockSpec((SUBLANES, LANES), lambda b: (0, 0))],
        out_shape=[jax.ShapeDtypeStruct((bsz * seq, LANES), BF16), jax.ShapeDtypeStruct((SUBLANES, LANES), F32)],
        compiler_params=_params(("arbitrary",)),
    )(dck, dcq, ddt, dtf, fb)


ATT_BLOCK_FWD = 512
ATT_BLOCK_KEYS = 256
ATT_BLOCK_BWD = 256


def _att_operands(q_ref, k_ref, aux_ref, hh):
    lane = _iota((1, LANES), 1)
    lm = (lane < HEAD_DIM) if hh == 0 else (lane >= HEAD_DIM)
    base = _aux_base(hh)
    zero = jnp.zeros((1, LANES), BF16)
    ka = jnp.where(lm, k_ref[...], jnp.where((lane >= base) & (lane <= base + AUX_ONE), aux_ref[...], zero))
    qa = jnp.where(lm, q_ref[...] * ATT_SCALE,
                   jnp.where((lane >= base) & (lane < base + AUX_C), jnp.ones((1, LANES), BF16), zero))
    return lm, base, qa, ka


def _att_fwd(qkv, ccol, bsz, seq, gather):
    bq, bk = min(ATT_BLOCK_FWD, seq), min(ATT_BLOCK_KEYS, seq)
    nq, ratio = seq // bq, bq // bk
    nx = len(gather)

    def body(*refs):
        q_ref, k_ref, v_ref, c_ref = refs[:4]
        x_refs = refs[4:4 + nx]
        o_ref, lse_ref = refs[4 + nx:6 + nx]
        xo_refs = refs[6 + nx:6 + 2 * nx]
        qa_scr, ka_scr, vt_scr = refs[6 + 2 * nx:9 + 2 * nx]
        sems = refs[9 + 2 * nx:]
        pair = pl.program_id(1)

        @pl.when((pl.program_id(0) == 0) & (pair == 0))
        def _():
            _exchange_start("gather", x_refs, xo_refs, *sems)

        lse_ref[...] = jnp.zeros_like(lse_ref)
        for hh in range(2):
            _, _, qa, ka = _att_operands(q_ref, k_ref, c_ref, hh)
            qa_scr[hh] = qa
            ka_scr[hh] = ka
        for t0 in range(0, seq, bq):
            vt_scr[:, t0:t0 + bq] = v_ref[t0:t0 + bq, :].T
        key_minus_query = _iota((bk, bq), 0) - _iota((bk, bq), 1)

        def q_step(i, carry0):
            qrows = pl.ds(pl.multiple_of(i * bq, bq), bq)

            def scores(j):
                krows = pl.ds(pl.multiple_of(j * bk, bk), bk)
                return tuple(_bdot_nt(ka_scr[hh, krows, :], qa_scr[hh, qrows, :]) for hh in range(2))

            def update(state, st_pair, j, diag):
                krows = pl.ds(pl.multiple_of(j * bk, bk), bk)
                out = []
                for hh in range(2):
                    m, l, acc = state[hh]
                    st = st_pair[hh]
                    if diag is not None:
                        st = jnp.where(key_minus_query + diag * bk <= 0, st, NEG)
                    mn = jnp.maximum(m, jnp.max(st, axis=0, keepdims=True))
                    alpha = jnp.exp(m - mn)
                    pt = jnp.exp(st - mn)
                    l = alpha * l + jnp.sum(pt, axis=0, keepdims=True)
                    vt = vt_scr[HEAD_DIM * hh:HEAD_DIM * (hh + 1), krows]
                    out.append((mn, l, alpha * acc + _bdot(vt, pt)))
                return tuple(out)

            def step(j, carry):
                state, s_cur = carry
                s_next = scores(j + 1)
                return update(state, s_cur, j, None), s_next

            def step_pair(t, carry):
                state, s_cur = carry
                s_second = scores(2 * t + 1)
                s_next = scores(2 * t + 2)
                state = update(state, s_cur, 2 * t, None)
                return update(state, s_second, 2 * t + 1, None), s_next

            one = (jnp.full((1, bq), NEG, F32), jnp.zeros((1, bq), F32), jnp.zeros((HEAD_DIM, bq), F32))
            first_diag = i * ratio
            if ratio % 2 == 0:
                state, s_cur = lax.fori_loop(0, first_diag // 2, step_pair, ((one, one), scores(0)))
            else:
                state, s_cur = lax.fori_loop(0, first_diag, step, ((one, one), scores(0)))
            for r in range(ratio):
                s_next = scores(first_diag + r + 1) if r + 1 < ratio else None
                state = update(state, s_cur, first_diag + r, r)
                s_cur = s_next
            (m0, l0, acc0), (m1, l1, acc1) = state
            ot = jnp.concatenate([acc0 * (1.0 / l0), acc1 * (1.0 / l1)], axis=0)
            o_ref[qrows, :] = ot.T
            lse_ref[0, 0, 0:1, qrows] = m0 + jnp.log(l0)
            lse_ref[0, 0, 1:2, qrows] = m1 + jnp.log(l1)
            return carry0

        lax.fori_loop(0, nq, q_step, 0)

        @pl.when((pl.program_id(0) == bsz - 1) & (pair == HEAD_PAIRS - 1))
        def _():
            _exchange_wait("gather", x_refs, xo_refs, *sems)

    col = lambda off: pl.BlockSpec((seq, LANES), lambda b, p: (b, off + p))
    head2 = pltpu.VMEM((2, seq, LANES), BF16)
    hbm = pl.BlockSpec(memory_space=pl.ANY)
    return pl.pallas_call(
        body, name="att_fwd", grid=(bsz, HEAD_PAIRS),
        in_specs=[col(0), col(HEAD_PAIRS), col(2 * HEAD_PAIRS), col(0)] + [hbm] * nx,
        out_specs=[pl.BlockSpec((seq, LANES), lambda b, p: (b, p)),
                   pl.BlockSpec((1, 1, SUBLANES, seq), lambda b, p: (b, p, 0, 0))] + [hbm] * nx,
        out_shape=[jax.ShapeDtypeStruct((bsz * seq, D_MODEL), F32),
                   jax.ShapeDtypeStruct((bsz, HEAD_PAIRS, SUBLANES, seq), F32)] + _exchange_shapes("gather", gather),
        scratch_shapes=[head2, head2, pltpu.VMEM((LANES, seq), BF16)] + _exchange_scratch(nx),
        compiler_params=_params(("arbitrary", "arbitrary")),
    )(qkv, qkv, qkv, ccol, *gather)


def _att_bwd(qkv, dycat, o, lse, ccol, bsz, seq, scatter):
    blk = bq = min(ATT_BLOCK_BWD, seq)
    nb = nq = seq // blk
    nx = len(scatter)

    def body(*refs):
        q_ref, k_ref, v_ref, do_ref, o_ref, lse_ref, c_ref = refs[:7]
        x_refs = refs[7:7 + nx]
        dq_ref, dk_ref, dv_ref, dck_ref, dcq_ref = refs[7 + nx:12 + nx]
        xo_refs = refs[12 + nx:12 + 2 * nx]
        (qa_scr, ka_scr, va_scr, doa_scr, kat_scr, qat_scr, dot_scr, d_scr, dqt_scr, dkt_scr,
         dvt_scr) = refs[12 + 2 * nx:23 + 2 * nx]
        sems = refs[23 + 2 * nx:]
        pair = pl.program_id(1)

        @pl.when((pl.program_id(0) == 0) & (pair == 0))
        def _():
            _exchange_start("scatter", x_refs, xo_refs, *sems)

        query_minus_key = _iota((blk, bq), 1) - _iota((blk, bq), 0)
        lane_b = _iota((blk, LANES), 1)
        row_t = _iota((LANES, seq), 0)
        bases = []
        ones8 = jnp.ones((SUBLANES, LANES), F32)
        for hh in range(2):
            lm, base, qa, ka = _att_operands(q_ref, k_ref, c_ref, hh)
            bases.append(base)
            qa_scr[hh] = qa
            ka_scr[hh] = ka
            va_scr[hh] = jnp.where(lm, v_ref[...], jnp.zeros((seq, LANES), BF16))
            doa = jnp.where(lm, do_ref[...], 0.0).astype(BF16)
            doa_scr[hh] = doa
            for t0 in range(0, seq, blk):
                kat_scr[hh, :, t0:t0 + blk] = ka[t0:t0 + blk, :].T
                qat_scr[hh, :, t0:t0 + blk] = qa[t0:t0 + blk, :].T
            d1, d2, d3 = (_bdot_nt(ones8, term) for term in _split3(doa.astype(F32) * o_ref[...]))
            d_scr[hh] = d1 + d2 + d3
        for t0 in range(0, seq, blk):
            dot_scr[:, t0:t0 + blk] = do_ref[t0:t0 + blk, :].astype(BF16).T
        dqt_scr[...] = jnp.zeros_like(dqt_scr)
        dck_ref[...] = jnp.zeros_like(dck_ref)

        def kv_step(j, carry0):
            krows = pl.ds(pl.multiple_of(j * blk, blk), blk)
            dkt_scr[...] = jnp.zeros_like(dkt_scr)
            dvt_scr[...] = jnp.zeros_like(dvt_scr)

            def scores(i):
                rows = pl.ds(pl.multiple_of(i * bq, bq), bq)
                return tuple((_bdot_nt(ka_scr[hh, krows, :], qa_scr[hh, rows, :]),
                              _bdot_nt(va_scr[hh, krows, :], doa_scr[hh, rows, :])) for hh in range(2))

            def update(i, sd, masked):
                rows = pl.ds(pl.multiple_of(i * bq, bq), bq)
                for hh in range(2):
                    st, dpt = sd[hh]
                    if masked:
                        st = jnp.where(query_minus_key >= 0, st, NEG)
                    pt = jnp.exp(st - lse_ref[0, 0, hh:hh + 1, rows])
                    dst = (pt * (dpt - d_scr[hh, 0:1, rows])).astype(BF16)
                    dvt_scr[hh] += _bdot_nt(dot_scr[HEAD_DIM * hh:HEAD_DIM * (hh + 1), rows], pt)
                    dkt_scr[hh] += _bdot_nt(qat_scr[hh, :, rows], dst)
                    dqt_scr[hh, :, rows] += _bdot(kat_scr[hh, :, krows], dst)

            i_diag = j

            def q_pair(t, sd_cur):
                i = i_diag + 1 + 2 * t
                sd_second = scores(i + 1)
                sd_next = scores(jnp.minimum(i + 2, nq - 1))
                update(i, sd_cur, False)
                update(i + 1, sd_second, False)
                return sd_next

            sd_diag = scores(i_diag)
            sd_first = scores(jnp.minimum(i_diag + 1, nq - 1))
            update(i_diag, sd_diag, True)
            rest = nq - 1 - i_diag
            sd_last = lax.fori_loop(0, rest // 2, q_pair, sd_first)

            @pl.when(rest % 2 == 1)
            def _():
                update(jnp.int32(nq - 1), sd_last, False)
            dkt = jnp.where(_iota((LANES, blk), 0) < HEAD_DIM, dkt_scr[0], dkt_scr[1])
            dk_ref[krows, :] = dkt.T.astype(BF16)
            dv_ref[krows, :] = jnp.concatenate([dvt_scr[0], dvt_scr[1]], axis=0).T.astype(BF16)
            for hh in range(2):
                dck_ref[0, 0, hh:hh + 1, krows] = -dkt_scr[hh, bases[hh]:bases[hh] + 1, :]
            return carry0

        lax.fori_loop(0, nb, kv_step, 0)
        for t0 in range(0, seq, blk):
            dq0, dq1 = dqt_scr[0, :, t0:t0 + blk].T, dqt_scr[1, :, t0:t0 + blk].T
            dq_ref[t0:t0 + blk, :] = (jnp.where(lane_b < HEAD_DIM, dq0, dq1) * ATT_SCALE).astype(BF16)
        dcq_ref[...] = jnp.zeros_like(dcq_ref)
        for hh in range(2):
            dcq_ref[0, 0, hh:hh + 1, :] = jnp.sum(jnp.where(row_t == bases[hh] + AUX_ONE, dqt_scr[hh], 0.0),
                                                   axis=0, keepdims=True)

        @pl.when((pl.program_id(0) == bsz - 1) & (pair == HEAD_PAIRS - 1))
        def _():
            _exchange_wait("scatter", x_refs, xo_refs, *sems)

    col = lambda off: pl.BlockSpec((seq, LANES), lambda b, p: (b, off + p))
    rowvec = pl.BlockSpec((1, 1, SUBLANES, seq), lambda b, p: (b, p, 0, 0))
    out = jax.ShapeDtypeStruct((bsz * seq, D_MODEL), BF16)
    rows_shape = jax.ShapeDtypeStruct((bsz, HEAD_PAIRS, SUBLANES, seq), F32)
    head2 = pltpu.VMEM((2, seq, LANES), BF16)
    hbm = pl.BlockSpec(memory_space=pl.ANY)
    return pl.pallas_call(
        body, name="att_bwd", grid=(bsz, HEAD_PAIRS),
        in_specs=[col(0), col(HEAD_PAIRS), col(2 * HEAD_PAIRS), col(HEAD_PAIRS), col(0), rowvec, col(0)] + [hbm] * nx,
        out_specs=[col(0), col(0), col(0), rowvec, rowvec] + [hbm] * nx,
        out_shape=[out, out, out, rows_shape, rows_shape] + _exchange_shapes("scatter", scatter),
        scratch_shapes=[head2, head2, head2, head2, pltpu.VMEM((2, LANES, seq), BF16),
                        pltpu.VMEM((2, LANES, seq), BF16), pltpu.VMEM((LANES, seq), BF16),
                        pltpu.VMEM((2, SUBLANES, seq), F32), pltpu.VMEM((2, LANES, seq), F32),
                        pltpu.VMEM((2, LANES, blk), F32), pltpu.VMEM((2, HEAD_DIM, blk), F32)] + _exchange_scratch(nx),
        compiler_params=_params(("arbitrary", "arbitrary")),
    )(qkv, qkv, qkv, dycat, o, lse, ccol, *scatter)


def _adamw_math(w, g, m, v):
    m = ADAM_B1 * m + (1.0 - ADAM_B1) * g
    v = ADAM_B2 * v + (1.0 - ADAM_B2) * jnp.square(g)
    m_hat = m / ADAM_C1
    v_hat = v / ADAM_C2
    delta = -ADAM_LR * (m_hat / (jnp.sqrt(v_hat) + ADAM_EPS) + ADAM_WD * w)
    return delta, m, v


def _row_tile(rows):
    for tr in (256, 128, 64, 32, 16, 8):
        if rows % tr == 0:
            return tr
    return rows


def _sum_parts(parts, name):
    nparts, rows, cols = parts.shape
    tr = rows if rows % SUBLANES else _row_tile(rows)

    def body(p_ref, o_ref):
        g = p_ref[0].astype(F32)
        for s in range(1, nparts):
            g = g + p_ref[s].astype(F32)
        o_ref[...] = g

    return pl.pallas_call(
        body, name=name, grid=(rows // tr,),
        in_specs=[pl.BlockSpec((nparts, tr, cols), lambda i: (0, i, 0))],
        out_specs=pl.BlockSpec((tr, cols), lambda i: (i, 0)),
        out_shape=jax.ShapeDtypeStruct((rows, cols), F32),
        compiler_params=_params(("parallel",)),
    )(parts)


def _adamw_parts(parts, w, m, v, name):
    nparts, rows, cols = parts.shape
    tr = _row_tile(rows)

    def body(p_ref, w_ref, m_ref, v_ref, g_ref, d_ref, mo_ref, vo_ref):
        g = p_ref[0].astype(F32)
        for s in range(1, nparts):
            g = g + p_ref[s].astype(F32)
        delta, mn, vn = _adamw_math(w_ref[...], g, m_ref[...], v_ref[...])
        g_ref[...] = g
        d_ref[...] = delta
        mo_ref[...] = mn
        vo_ref[...] = vn

    blk = pl.BlockSpec((tr, cols), lambda i: (i, 0))
    shp = jax.ShapeDtypeStruct((rows, cols), F32)
    return pl.pallas_call(
        body, name=name, grid=(rows // tr,),
        in_specs=[pl.BlockSpec((nparts, tr, cols), lambda i: (0, i, 0)), blk, blk, blk],
        out_specs=[blk, blk, blk, blk], out_shape=[shp, shp, shp, shp],
        compiler_params=_params(("parallel",)),
    )(parts, w, m, v)


def _me():
    return lax.axis_index("x"), lax.axis_index("y"), lax.axis_index("c")


def _flip(pos, k):
    x, y, c = pos
    kx, ky, kc = (k >> 2) & 1, (k >> 1) & 1, k & 1
    return (x ^ kx if kx else x, y ^ ky if ky else y, c ^ kc if kc else c)


def _logical(pos):
    return 4 * pos[0] + 2 * pos[1] + pos[2]


def _all_gather_two_level(shards):
    narr = len(shards)

    def body(*refs):
        x_refs, out_refs = refs[:narr], refs[narr:2 * narr]
        send_sems, recv_sems, local_sems = refs[2 * narr:]
        x, y, c = _me()
        me, sibling = (x, y, c), (x, y, 1 - c)
        chips = [(1 - x, y), (x, 1 - y), (1 - x, 1 - y)]

        def copy(a, k, block, to, src=None):
            slot = out_refs[a].at[_logical(block)]
            return pltpu.make_async_remote_copy(
                src_ref=slot if src is None else src, dst_ref=slot,
                send_sem=send_sems.at[a, k], recv_sem=recv_sems.at[a, k], device_id=to, device_id_type=MESH)

        mine = [pltpu.make_async_copy(x_refs[a], out_refs[a].at[_logical(me)], local_sems.at[a]) for a in range(narr)]
        for cp in mine:
            cp.start()
        first = []
        for a in range(narr):
            first.append(copy(a, 0, me, sibling, src=x_refs[a]))
            first += [copy(a, 1 + j, me, (*chip, c), src=x_refs[a]) for j, chip in enumerate(chips)]
        for cp in first:
            cp.start()
        passed = []
        for a in range(narr):
            for j, chip in enumerate(chips):
                copy(a, 1 + j, (*chip, c), me).wait_recv()
                fwd = copy(a, 4 + j, (*chip, c), sibling)
                fwd.start()
                passed.append(fwd)
        for a in range(narr):
            copy(a, 0, sibling, me).wait_recv()
            for j, chip in enumerate(chips):
                copy(a, 4 + j, (*chip, 1 - c), me).wait_recv()
        for cp in first + passed:
            cp.wait_send()
        for cp in mine:
            cp.wait()

    hbm = pl.BlockSpec(memory_space=pl.ANY)
    return pl.pallas_call(
        body, name="all_gather_big",
        out_shape=[jax.ShapeDtypeStruct((N_DEV,) + s.shape, s.dtype) for s in shards],
        in_specs=[hbm] * narr, out_specs=[hbm] * narr,
        scratch_shapes=[pltpu.SemaphoreType.DMA((narr, 7)), pltpu.SemaphoreType.DMA((narr, 7)),
                        pltpu.SemaphoreType.DMA((narr,))],
    )(*shards)


def _exchange_copies(kind, x_refs, out_refs, send_sems, recv_sems, local_sems):
    me = _me()
    mine = _logical(me)
    local, remote = [], []
    for a, (x_ref, out_ref) in enumerate(zip(x_refs, out_refs)):
        own = x_ref if kind == "gather" else x_ref.at[mine]
        local.append(pltpu.make_async_copy(own, out_ref.at[mine], local_sems.at[a]))
        for k in range(1, N_DEV):
            peer = _flip(me, k)
            src = x_ref if kind == "gather" else x_ref.at[_logical(peer)]
            remote.append(pltpu.make_async_remote_copy(
                src_ref=src, dst_ref=out_ref.at[mine], send_sem=send_sems.at[a, k - 1], recv_sem=recv_sems.at[a, k - 1],
                device_id=peer, device_id_type=MESH))
    return local, remote


def _exchange_start(kind, x_refs, out_refs, *sems):
    if not x_refs:
        return
    local, remote = _exchange_copies(kind, x_refs, out_refs, *sems)
    for cp in local + remote:
        cp.start()


def _exchange_wait(kind, x_refs, out_refs, *sems):
    if not x_refs:
        return
    local, remote = _exchange_copies(kind, x_refs, out_refs, *sems)
    for cp in remote:
        cp.wait_recv()
    for cp in remote:
        cp.wait_send()
    for cp in local:
        cp.wait()


def _exchange_shapes(kind, arrays):
    return [jax.ShapeDtypeStruct(((N_DEV,) if kind == "gather" else ()) + a.shape, a.dtype) for a in arrays]


def _exchange_scratch(narrays):
    if not narrays:
        return []
    return [pltpu.SemaphoreType.DMA((narrays, N_DEV - 1)), pltpu.SemaphoreType.DMA((narrays, N_DEV - 1)),
            pltpu.SemaphoreType.DMA((narrays,))]


def _exchange_rows(x_ref, buf_ref, send_sems, recv_sems):
    me = _me()
    buf_ref[_logical(me)] = x_ref[...]
    copies = []
    for k in range(1, N_DEV):
        peer = _flip(me, k)
        copies.append(pltpu.make_async_remote_copy(
            src_ref=x_ref, dst_ref=buf_ref.at[_logical(me)],
            send_sem=send_sems.at[k - 1], recv_sem=recv_sems.at[k - 1], device_id=peer, device_id_type=MESH))
    for cp in copies:
        cp.start()
    for cp in copies:
        cp.wait_recv()
    for cp in copies:
        cp.wait_send()


def _sum_slots(buf_ref):
    total = buf_ref[0]
    for s in range(1, N_DEV):
        total = total + buf_ref[s]
    return total


def _small_gather(x):
    def body(x_ref, o_ref, buf_ref, send_sems, recv_sems):
        _exchange_rows(x_ref, buf_ref, send_sems, recv_sems)
        o_ref[...] = _sum_slots(buf_ref)

    return pl.pallas_call(
        body, name="small_gather", out_shape=jax.ShapeDtypeStruct(x.shape, F32),
        in_specs=[pl.BlockSpec(memory_space=pltpu.VMEM)], out_specs=pl.BlockSpec(memory_space=pltpu.VMEM),
        scratch_shapes=[pltpu.VMEM((N_DEV,) + x.shape, F32), pltpu.SemaphoreType.DMA((7,)), pltpu.SemaphoreType.DMA((7,))],
    )(x)


def _small_reduce_adamw(g, w, m, v):
    def body(g_ref, w_ref, m_ref, v_ref, go_ref, d_ref, mo_ref, vo_ref, buf_ref, send_sems, recv_sems):
        _exchange_rows(g_ref, buf_ref, send_sems, recv_sems)
        gs = _sum_slots(buf_ref)
        delta, mn, vn = _adamw_math(w_ref[...], gs, m_ref[...], v_ref[...])
        go_ref[...] = gs
        d_ref[...] = delta
        mo_ref[...] = mn
        vo_ref[...] = vn

    vm = pl.BlockSpec(memory_space=pltpu.VMEM)
    shp = jax.ShapeDtypeStruct(g.shape, F32)
    return pl.pallas_call(
        body, name="small_reduce_adamw", out_shape=[shp, shp, shp, shp],
        in_specs=[vm, vm, vm, vm], out_specs=[vm, vm, vm, vm],
        scratch_shapes=[pltpu.VMEM((N_DEV,) + g.shape, F32), pltpu.SemaphoreType.DMA((7,)), pltpu.SemaphoreType.DMA((7,))],
    )(g, w, m, v)


def _pad_cols(a, width):
    return jnp.pad(a, ((0, 0), (0, width - a.shape[1])))


def _local_step(x, target, norm_mix_w, w_in_t, conv_w, conv_b, dt_bias, a_log, d_skip, ssd_norm_w, f_bias,
                late_shards, norm_mlp_w, norm_final_w):
    bsz, seq, _ = x.shape
    n = bsz * seq
    x2 = x.reshape(n, D_MODEL)
    t2 = target.reshape(n, D_MODEL)
    nfw = norm_final_w.reshape(1, D_MODEL)

    bias = _pad_cols(dt_bias, LANES)
    arow = _pad_cols(-jnp.exp(a_log), LANES)
    biast, acol = bias.reshape(LANES, 1), arow.reshape(LANES, 1)
    dfull = jnp.repeat(d_skip, HEAD_DIM, axis=1)
    fb = jnp.pad(f_bias, ((0, 0), (F_COL, LANES - 2 * F_COL)))

    wt_z, wt_xbc, wt_qkv = w_in_t[:ROW_XBC], w_in_t[ROW_XBC:ROW_DT], w_in_t[ROW_Q:ROW_F]
    wt_dtf = jnp.concatenate([w_in_t[ROW_DT:ROW_Q], w_in_t[ROW_F:], jnp.zeros((LANES - 2 * F_COL, D_MODEL), BF16)], axis=0)
    wt_q, wt_k, wt_v = wt_qkv[:D_MODEL], wt_qkv[D_MODEL:2 * D_MODEL], wt_qkv[2 * D_MODEL:]

    h1 = _rms_fwd(x2, norm_mix_w, "rms_fwd_mix")
    z = _mm([(h1, wt_z)], "inproj_z", F32, 1024, 1024, nt=True)
    xbc_raw = _mm([(h1, wt_xbc)], "inproj_xbc", F32, 512, 1536, nt=True)
    qkv = _mm([(h1, wt_qkv)], "inproj_qkv", BF16, 512, 3072, nt=True)
    dtf = _mm([(h1, wt_dtf)], "inproj_dtf", F32, 2048, LANES, nt=True)
    dtft = _mm([(wt_dtf, h1)], "inproj_dtf_t", F32, LANES, 2048, nt=True)

    xbc = _conv_fwd(xbc_raw, conv_w, conv_b, bsz, seq)
    y_pre, hall = _ssd_fwd(xbc, dtf, dtft, bias, biast, arow, acol, dfull, bsz, seq)

    ccol = _fox_prep(dtf, fb, bsz, seq)
    o_att, lse, w_out, w_up_t, w_down = _att_fwd(qkv, ccol, bsz, seq, late_shards)
    w_out, w_up_t, w_down = (w.reshape(-1, D_MODEL) for w in (w_out, w_up_t, w_down))

    ycat = _gnorm_fwd(y_pre, z, o_att, ssd_norm_w)
    h2, h2n = _mm([(ycat, w_out)], "outproj", F32, 512, 1024, res=x2, rms_fwd=norm_mlp_w)
    pre = _mm([(h2n, w_up_t)], "mlp_up", BF16, 512, 4096, nt=True)

    dh3, loss_row, g_nfw = _mlp_down_loss(pre, w_down, h2, t2, nfw)
    dpre, u = _mlp_down_bwd(dh3, w_down, pre)
    g_w_down = _mm_tn(u, dh3, "grad_w_down", 1024, 1024, 2048)
    g_w_up_t = _mm_tn(dpre, h2n, "grad_w_up", 1024, 1024, 2048)
    by_shard = lambda g: g.reshape(N_DEV, g.shape[0] // N_DEV, D_MODEL)
    dh2, g_nmlp = _mm([(dpre, w_up_t)], "mlp_up_bwd", F32, 512, 1024, res=dh3, rms_bwd=(h2, norm_mlp_w))

    g_w_out = _mm_tn(ycat, dh2, "grad_w_out", 1024, 1024, 2048)
    dycat = _mm([(dh2, w_out)], "outproj_bwd", F32, 512, 2048, nt=True)
    dy_pre, dz, g_ssdn = _gnorm_bwd(dycat, y_pre, z, ssd_norm_w)
    dq, dk, dv, dck, dcq, recv_down, recv_up_t, recv_out = _att_bwd(
        qkv, dycat, o_att, lse, ccol, bsz, seq, [by_shard(g_w_down), by_shard(g_w_up_t), by_shard(g_w_out)])

    dxbc, ddt, ssd_acc = _ssd_bwd(dy_pre, xbc, dtf, dtft, bias, biast, arow, acol, dfull, hall, bsz, seq)
    dxbc_raw, g_cw, g_cb = _conv_bwd(dxbc, xbc_raw, conv_w, conv_b, bsz, seq)

    def head_cols(rows):
        cols = rows[:, :, :2, :].reshape(bsz, SSD_HEADS, seq).transpose(0, 2, 1).reshape(n, SSD_HEADS)
        return jnp.pad(cols, ((0, 0), (F_COL, LANES - 2 * F_COL)))

    ddtf, g_fb = _fox_prep_bwd(head_cols(dck), head_cols(dcq), ddt, dtf, fb, bsz, seq)

    g_dtf = _mm_tn(ddtf, h1, "grad_w_in_dtf", LANES, 1024, 2048)
    g_w_in_t = jnp.concatenate([
        _mm_tn(dz, h1, "grad_w_in_z", 1024, 1024, 2048), _mm_tn(dxbc_raw, h1, "grad_w_in_xbc", 768, 1024, 2048),
        g_dtf[:F_COL], _mm_tn(dq, h1, "grad_w_in_q", 1024, 1024, 2048), _mm_tn(dk, h1, "grad_w_in_k", 1024, 1024, 2048),
        _mm_tn(dv, h1, "grad_w_in_v", 1024, 1024, 2048), g_dtf[F_COL:2 * F_COL]], axis=0)
    pieces = [(dz, wt_z), (dxbc_raw, wt_xbc), (dq, wt_q), (dk, wt_k), (dv, wt_v), (ddtf, wt_dtf)]
    dx, g_nmix, recv_in_t = _mm(pieces, "inproj_bwd", F32, 256, 1024, res=dh2, rms_bwd=(x2, norm_mix_w),
                                exchange=("scatter", [by_shard(g_w_in_t)]))

    small = dict(
        norm_mix_w=g_nmix, norm_mlp_w=g_nmlp, norm_final_w=g_nfw, ssd_norm_w=g_ssdn, conv_b=g_cb, conv_w=g_cw,
        dt_bias=ssd_acc[16:17, :SSD_HEADS], a_log=ssd_acc[0:1, :SSD_HEADS], d_skip=ssd_acc[8:9, :SSD_HEADS],
        f_bias=g_fb[0:1, F_COL:2 * F_COL])
    recv = dict(w_in_t=recv_in_t, w_out=recv_out, w_up_t=recv_up_t, w_down=recv_down)
    return loss_row[0, 0], dx.reshape(bsz, seq, D_MODEL), small, recv


SMALL_LAYOUT = (("norm_mix_w", 0, 1, 0, D_MODEL), ("norm_mlp_w", 1, 1, 0, D_MODEL), ("norm_final_w", 2, 1, 0, D_MODEL),
                ("ssd_norm_w", 3, 1, 0, D_MODEL), ("conv_b", 4, 1, 0, CONV_CH), ("conv_w", 5, CONV_K, 0, CONV_CH),
                ("dt_bias", 9, 1, 0, SSD_HEADS), ("a_log", 9, 1, LANES, SSD_HEADS), ("d_skip", 9, 1, 2 * LANES, SSD_HEADS),
                ("f_bias", 9, 1, 3 * LANES, SSD_HEADS))
SMALL_ROWS = 2 * SUBLANES


def _pack_small(vals):
    packed = jnp.zeros((SMALL_ROWS, SMALL_COLS), F32)
    for name, r0, nrows, c0, width in SMALL_LAYOUT:
        packed = packed.at[r0:r0 + nrows, c0:c0 + width].set(vals[name].reshape(nrows, width))
    return packed


def _unpack_small(packed, like):
    out = {}
    for name, r0, nrows, c0, width in SMALL_LAYOUT:
        out[name] = packed[r0:r0 + nrows, c0:c0 + width].reshape(like[name].shape)
    return out


def kernel(x, norm_mix_w, w_in, conv_w, conv_b, dt_bias, a_log, d_skip, ssd_norm_w, f_bias, w_out, norm_mlp_w, w_up, w_down, norm_final_w, loss_target, m_norm_mix_w, m_w_in, m_conv_w, m_conv_b, m_dt_bias, m_a_log, m_d_skip, m_ssd_norm_w, m_f_bias, m_w_out, m_norm_mlp_w, m_w_up, m_w_down, m_norm_final_w, v_norm_mix_w, v_w_in, v_conv_w, v_conv_b, v_dt_bias, v_a_log, v_d_skip, v_ssd_norm_w, v_f_bias, v_w_out, v_norm_mlp_w, v_w_up, v_w_down, v_norm_final_w):
    me = 4 * lax.axis_index("x") + 2 * lax.axis_index("y") + lax.axis_index("c")
    conv_shard_w = CONV_CH // N_DEV
    zero = jnp.zeros((), jnp.int32)

    def place_conv(shard):
        return lax.dynamic_update_slice(jnp.zeros((CONV_K, CONV_CH), F32), shard[0], (zero, me * conv_shard_w))

    (g_in_t,) = _all_gather_two_level([w_in[0].T.astype(BF16)])
    late_shards = [w_out[0].astype(BF16), w_up[0].T.astype(BF16), w_down[0].astype(BF16)]
    conv_full = _small_gather(jnp.pad(place_conv(conv_w), ((0, SUBLANES - CONV_K), (0, 0))))[:CONV_K]

    loss_local, grad_x, g_small, recv = _local_step(
        x, loss_target, norm_mix_w, g_in_t.reshape(IN_WIDTH, D_MODEL), conv_full, conv_b, dt_bias, a_log, d_skip,
        ssd_norm_w, f_bias, late_shards, norm_mlp_w, norm_final_w)
    loss = lax.psum(loss_local, MESH_AXES)

    grad_in = _sum_parts(recv["w_in_t"], "sum_w_in").T[None]
    grad_up = _sum_parts(recv["w_up_t"], "sum_w_up").T[None]
    big = {
        "w_in": _adamw_parts(grad_in, w_in[0], m_w_in[0], v_w_in[0], "adamw_w_in"),
        "w_out": _adamw_parts(recv["w_out"], w_out[0], m_w_out[0], v_w_out[0], "adamw_w_out"),
        "w_up": _adamw_parts(grad_up, w_up[0], m_w_up[0], v_w_up[0], "adamw_w_up"),
        "w_down": _adamw_parts(recv["w_down"], w_down[0], m_w_down[0], v_w_down[0], "adamw_w_down"),
    }

    like = dict(norm_mix_w=norm_mix_w, norm_mlp_w=norm_mlp_w, norm_final_w=norm_final_w, ssd_norm_w=ssd_norm_w,
                conv_b=conv_b, conv_w=jnp.zeros((1, CONV_K, CONV_CH), F32), dt_bias=dt_bias, a_log=a_log,
                d_skip=d_skip, f_bias=f_bias)
    w_small = dict(like, conv_w=place_conv(conv_w))
    m_small = dict(norm_mix_w=m_norm_mix_w, norm_mlp_w=m_norm_mlp_w, norm_final_w=m_norm_final_w,
                   ssd_norm_w=m_ssd_norm_w, conv_b=m_conv_b, conv_w=place_conv(m_conv_w), dt_bias=m_dt_bias,
                   a_log=m_a_log, d_skip=m_d_skip, f_bias=m_f_bias)
    v_small = dict(norm_mix_w=v_norm_mix_w, norm_mlp_w=v_norm_mlp_w, norm_final_w=v_norm_final_w,
                   ssd_norm_w=v_ssd_norm_w, conv_b=v_conv_b, conv_w=place_conv(v_conv_w), dt_bias=v_dt_bias,
                   a_log=v_a_log, d_skip=v_d_skip, f_bias=v_f_bias)
    small = _small_reduce_adamw(_pack_small(g_small), _pack_small(w_small), _pack_small(m_small), _pack_small(v_small))
    small = [_unpack_small(s, like) for s in small]
    for s in small:
        s["conv_w"] = lax.dynamic_slice(s["conv_w"], (zero, zero, me * conv_shard_w), (1, CONV_K, conv_shard_w))

    order = ["norm_mix_w", "w_in", "conv_w", "conv_b", "dt_bias", "a_log", "d_skip", "ssd_norm_w", "f_bias", "w_out",
             "norm_mlp_w", "w_up", "w_down", "norm_final_w"]
    outs = [loss, grad_x]
    for kind in range(4):
        for name in order:
            outs.append(big[name][kind][None] if name in big else small[kind][name])
    return tuple(outs)
```

```python
import jax
import jax.numpy as jnp
from jax import lax
from jax.experimental import pallas as pl
from jax.experimental.pallas import tpu as pltpu

F32, BF16 = jnp.float32, jnp.bfloat16

D_MODEL = 1024
SSD_HEADS = 16
HEAD_DIM = 64
SSD_STATE = 128
CHUNK = 128
CONV_CH = 1536
CONV_K = 4
D_FF = 4096
MIX_WIDTH = 2048
IN_WIDTH = 5664
NORM_EPS = 1e-5
ATT_SCALE = 0.125

LANES = 128
SUBLANES = 8
HEAD_PAIRS = SSD_HEADS // 2
NEG = -1e30
VMEM_LIMIT = 52 * 1024 * 1024

ROW_XBC, ROW_DT, ROW_Q, ROW_F = 1024, 2560, 2576, 5648
F_COL = SSD_HEADS

N_DEV = 8
MESH_AXES = ("x", "y", "c")
MESH = pl.DeviceIdType.MESH

ADAM_LR, ADAM_B1, ADAM_B2, ADAM_EPS, ADAM_WD, ADAM_STEP = 0.001, 0.9, 0.999, 1e-08, 0.01, 10
ADAM_C1 = 1.0 - ADAM_B1 ** ADAM_STEP
ADAM_C2 = 1.0 - ADAM_B2 ** ADAM_STEP

SMALL_COLS = CONV_CH


def _params(sem=None):
    return pltpu.CompilerParams(dimension_semantics=sem, vmem_limit_bytes=VMEM_LIMIT)


def _sigmoid(u):
    return 1.0 / (1.0 + jnp.exp(-u))


def _softplus(u):
    return jnp.maximum(u, 0.0) + jnp.log(1.0 + jnp.exp(-jnp.abs(u)))


def _log_sigmoid(u):
    return jnp.minimum(u, 0.0) - jnp.log(1.0 + jnp.exp(-jnp.abs(u)))


def _bdot(a, b):
    return jnp.dot(a.astype(BF16), b.astype(BF16), preferred_element_type=F32)


def _bdot_nt(a, b):
    return lax.dot_general(a.astype(BF16), b.astype(BF16), (((1,), (1,)), ((), ())), preferred_element_type=F32)


def _bdot_tn(a, b):
    return lax.dot_general(a.astype(BF16), b.astype(BF16), (((0,), (0,)), ((), ())), preferred_element_type=F32)


def _split3(x):
    x1 = x.astype(BF16)
    r1 = x - x1.astype(F32)
    x2 = r1.astype(BF16)
    return x1, x2, (r1 - x2.astype(F32)).astype(BF16)


def _dot_sel(x, sel):
    s = sel.astype(BF16)
    p1, p2, p3 = (jnp.dot(p, s, preferred_element_type=F32) for p in _split3(x))
    return p1 + p2 + p3


def _sel_dot(sel, x):
    s = sel.astype(BF16)
    p1, p2, p3 = (jnp.dot(s, p, preferred_element_type=F32) for p in _split3(x))
    return p1 + p2 + p3


def _iota(shape, dim):
    return lax.broadcasted_iota(jnp.int32, shape, dim)


def _head_expand():
    return (jnp.right_shift(_iota((LANES, D_MODEL), 1), 6) == _iota((LANES, D_MODEL), 0)).astype(F32)


def _head_reduce():
    return (jnp.right_shift(_iota((D_MODEL, LANES), 0), 6) == _iota((D_MODEL, LANES), 1)).astype(F32)


def _rms_fwd(x, w, name):
    n, d = x.shape
    tm = min(1024, n)

    def body(x_ref, w_ref, o_ref):
        xv = x_ref[...]
        r = lax.rsqrt(jnp.mean(xv * xv, axis=-1, keepdims=True) + NORM_EPS)
        o_ref[...] = (xv * r * w_ref[...]).astype(BF16)

    return pl.pallas_call(
        body, name=name, grid=(n // tm,),
        in_specs=[pl.BlockSpec((tm, d), lambda i: (i, 0)), pl.BlockSpec((1, d), lambda i: (0, 0))],
        out_specs=pl.BlockSpec((tm, d), lambda i: (i, 0)),
        out_shape=jax.ShapeDtypeStruct((n, d), BF16),
        compiler_params=_params(("parallel",)),
    )(x, w)


def _mm(pairs, name, out_dtype, tm, tn, nt=False, res=None, exchange=None, rms_fwd=None, rms_bwd=None):
    m = pairs[0][0].shape[0]
    n = pairs[0][1].shape[0 if nt else 1]
    tm, tn = min(tm, m), min(tn, n)
    gm, gn = m // tm, n // tn
    npairs = len(pairs)
    kind, xs = (None, []) if exchange is None else exchange
    nx = len(xs)
    extra_in = [] if res is None else [res]
    if rms_bwd is not None:
        extra_in += list(rms_bwd)
    elif rms_fwd is not None:
        extra_in.append(rms_fwd)
    n_in = 2 * npairs + len(extra_in)
    n_out = 1 + (rms_fwd is not None or rms_bwd is not None)
    assert (rms_fwd is None and rms_bwd is None) or tn == n

    def body(*refs):
        x_refs, o_refs = refs[n_in:n_in + nx], refs[n_in + nx:n_in + nx + n_out]
        xo_refs, sems = refs[n_in + nx + n_out:n_in + 2 * nx + n_out], refs[n_in + 2 * nx + n_out:]
        extra = refs[2 * npairs:n_in]
        i, j = pl.program_id(0), pl.program_id(1)
        if nx:
            @pl.when((i == 0) & (j == 0))
            def _():
                _exchange_start(kind, x_refs, xo_refs, *sems)

        acc = None
        for p in range(npairs):
            a_v, b_v = refs[2 * p][...], refs[2 * p + 1][...]
            d = _bdot_nt(a_v, b_v) if nt else _bdot(a_v, b_v)
            acc = d if acc is None else acc + d
        if rms_bwd is not None:
            xv = extra[1][...]
            r = lax.rsqrt(jnp.mean(xv * xv, axis=-1, keepdims=True) + NORM_EPS)
            nrm = xv * r
            g = acc * extra[2][...]
            o_refs[0][...] = extra[0][...] + r * (g - nrm * jnp.mean(g * nrm, axis=-1, keepdims=True))

            @pl.when(i == 0)
            def _():
                o_refs[1][...] = jnp.zeros_like(o_refs[1])

            o_refs[1][...] += jnp.sum(acc * nrm, axis=0, keepdims=True)
        else:
            if res is not None:
                acc = extra[0][...] + acc
            o_refs[0][...] = acc.astype(o_refs[0].dtype)
            if rms_fwd is not None:
                r = lax.rsqrt(jnp.mean(acc * acc, axis=-1, keepdims=True) + NORM_EPS)
                o_refs[1][...] = (acc * r * extra[-1][...]).astype(BF16)
        if nx:
            @pl.when((i == gm - 1) & (j == gn - 1))
            def _():
                _exchange_wait(kind, x_refs, xo_refs, *sems)

    tile = pl.BlockSpec((tm, tn), lambda i, j: (i, j))
    row = pl.BlockSpec((1, tn), lambda i, j: (0, j))
    in_specs, args = [], []
    for a, b in pairs:
        k = a.shape[1]
        in_specs.append(pl.BlockSpec((tm, k), lambda i, j: (i, 0)))
        in_specs.append(pl.BlockSpec((tn, k), lambda i, j: (j, 0)) if nt else pl.BlockSpec((k, tn), lambda i, j: (0, j)))
        args += [a, b]
    in_specs += [row if e.shape[0] == 1 else tile for e in extra_in]
    out_specs, out_shape = [tile], [jax.ShapeDtypeStruct((m, n), out_dtype)]
    if rms_bwd is not None:
        out_specs.append(row)
        out_shape.append(jax.ShapeDtypeStruct((1, n), F32))
    elif rms_fwd is not None:
        out_specs.append(tile)
        out_shape.append(jax.ShapeDtypeStruct((m, n), BF16))
    hbm = pl.BlockSpec(memory_space=pl.ANY)
    sequential = nx or rms_bwd is not None
    outs = pl.pallas_call(
        body, name=name, grid=(gm, gn), in_specs=in_specs + [hbm] * nx,
        out_specs=out_specs + [hbm] * nx,
        out_shape=out_shape + _exchange_shapes(kind, xs),
        scratch_shapes=_exchange_scratch(nx),
        compiler_params=_params(("arbitrary", "arbitrary") if sequential else ("parallel", "parallel")),
    )(*args, *extra_in, *xs)
    return outs if len(outs) > 1 else outs[0]


def _mm_tn(a, b, name, tm, tn, tk):
    t, m = a.shape
    n = b.shape[1]
    tm, tn, tk = min(tm, m), min(tn, n), min(tk, t)
    nk = t // tk

    def body(a_ref, b_ref, o_ref, acc_ref):
        kk = pl.program_id(2)

        @pl.when(kk == 0)
        def _():
            acc_ref[...] = jnp.zeros_like(acc_ref)

        acc_ref[...] += _bdot_tn(a_ref[...], b_ref[...])

        @pl.when(kk == nk - 1)
        def _():
            o_ref[...] = acc_ref[...].astype(o_ref.dtype)

    return pl.pallas_call(
        body, name=name, grid=(m // tm, n // tn, nk),
        in_specs=[pl.BlockSpec((tk, tm), lambda i, j, kk: (kk, i)), pl.BlockSpec((tk, tn), lambda i, j, kk: (kk, j))],
        out_specs=pl.BlockSpec((tm, tn), lambda i, j, kk: (i, j)),
        out_shape=jax.ShapeDtypeStruct((m, n), BF16),
        scratch_shapes=[pltpu.VMEM((tm, tn), F32)],
        compiler_params=_params(("parallel", "parallel", "arbitrary")),
    )(a, b)


def _mlp_down_loss(pre, w_down, h2, target, w):
    m, k = pre.shape
    d = w_down.shape[1]
    tm = min(512, m)

    def body(p_ref, b_ref, r_ref, t_ref, w_ref, dh_ref, loss_ref, gw_ref):
        u = jnp.square(jnp.maximum(p_ref[...].astype(F32), 0.0))
        xv = r_ref[...] + _bdot(u, b_ref[...])
        r = lax.rsqrt(jnp.mean(xv * xv, axis=-1, keepdims=True) + NORM_EPS)
        nrm = xv * r
        wv = w_ref[...]
        err = nrm * wv - t_ref[...]
        part = 0.5 * jnp.sum(jnp.mean(err * err, axis=-1, keepdims=True), axis=0, keepdims=True)
        dy = err * (1.0 / d)
        g = dy * wv
        dh_ref[...] = r * (g - nrm * jnp.mean(g * nrm, axis=-1, keepdims=True))

        @pl.when(pl.program_id(0) == 0)
        def _():
            loss_ref[...] = jnp.zeros_like(loss_ref)
            gw_ref[...] = jnp.zeros_like(gw_ref)

        loss_ref[...] += jnp.broadcast_to(part, loss_ref.shape)
        gw_ref[...] += jnp.sum(dy * nrm, axis=0, keepdims=True)

    tok = pl.BlockSpec((tm, d), lambda i: (i, 0))
    row = pl.BlockSpec((1, d), lambda i: (0, 0))
    return pl.pallas_call(
        body, name="mlp_down_loss", grid=(m // tm,),
        in_specs=[pl.BlockSpec((tm, k), lambda i: (i, 0)), pl.BlockSpec((k, d), lambda i: (0, 0)), tok, tok, row],
        out_specs=[tok, pl.BlockSpec((1, LANES), lambda i: (0, 0)), row],
        out_shape=[jax.ShapeDtypeStruct((m, d), F32), jax.ShapeDtypeStruct((1, LANES), F32),
                   jax.ShapeDtypeStruct((1, d), F32)],
        compiler_params=_params(("arbitrary",)),
    )(pre, w_down, h2, target, w)


def _mlp_down_bwd(dh3, w_down, pre):
    m, k = dh3.shape
    n = w_down.shape[0]
    tm, tn = min(256, m), n

    def body(a_ref, b_ref, p_ref, dpre_ref, u_ref):
        r = jnp.maximum(p_ref[...].astype(F32), 0.0)
        du = _bdot_nt(a_ref[...], b_ref[...])
        dpre_ref[...] = (du * (2.0 * r)).astype(BF16)
        u_ref[...] = (r * r).astype(BF16)

    blk = pl.BlockSpec((tm, tn), lambda i, j: (i, j))
    return pl.pallas_call(
        body, name="mlp_down_bwd", grid=(m // tm, n // tn),
        in_specs=[pl.BlockSpec((tm, k), lambda i, j: (i, 0)), pl.BlockSpec((tn, k), lambda i, j: (j, 0)), blk],
        out_specs=[blk, blk],
        out_shape=[jax.ShapeDtypeStruct((m, n), BF16), jax.ShapeDtypeStruct((m, n), BF16)],
        compiler_params=_params(("parallel", "parallel")),
    )(dh3, w_down, pre)


CONV_TC = 512


def _conv_fwd(xbc, cw, cb, bsz, seq):
    tt = min(512, seq)
    nt, hb = seq // tt, tt // SUBLANES
    x3 = xbc.reshape(bsz, seq, CONV_CH)

    def body(xm_ref, xh_ref, w_ref, b_ref, o_ref):
        i = pl.program_id(2)
        x = xm_ref[0]
        halo = jnp.where(i > 0, xh_ref[0], 0.0)
        xx = jnp.concatenate([halo, x], axis=0)
        acc = x * w_ref[3:4, :] + b_ref[...]
        for s in range(1, CONV_K):
            acc = acc + pltpu.roll(xx, s, 0)[SUBLANES:, :] * w_ref[3 - s:4 - s, :]
        o_ref[0] = acc * _sigmoid(acc)

    out = pl.pallas_call(
        body, name="conv_fwd", grid=(CONV_CH // CONV_TC, bsz, nt),
        in_specs=[pl.BlockSpec((1, tt, CONV_TC), lambda c, b, i: (b, i, c)),
                  pl.BlockSpec((1, SUBLANES, CONV_TC), lambda c, b, i: (b, jnp.maximum(i * hb - 1, 0), c)),
                  pl.BlockSpec((CONV_K, CONV_TC), lambda c, b, i: (0, c)),
                  pl.BlockSpec((1, CONV_TC), lambda c, b, i: (0, c))],
        out_specs=pl.BlockSpec((1, tt, CONV_TC), lambda c, b, i: (b, i, c)),
        out_shape=jax.ShapeDtypeStruct((bsz, seq, CONV_CH), F32),
        compiler_params=_params(("parallel", "parallel", "parallel")),
    )(x3, x3, cw, cb)
    return out.reshape(bsz * seq, CONV_CH)


def _conv_bwd(da, xbc, cw, cb, bsz, seq):
    tt = min(512, seq)
    nt, hb = seq // tt, tt // SUBLANES
    x3 = xbc.reshape(bsz, seq, CONV_CH)
    da3 = da.reshape(bsz, seq, CONV_CH)
    n2 = tt + SUBLANES

    def body(dam_ref, daa_ref, xm_ref, xb_ref, xa_ref, w_ref, b_ref, du_ref, gw_ref, gb_ref):
        bi, i = pl.program_id(1), pl.program_id(2)
        before = jnp.where(i > 0, xb_ref[0], 0.0)
        after = jnp.where(i < nt - 1, xa_ref[0], 0.0)
        xx = jnp.concatenate([before, xm_ref[0], after], axis=0)
        rolled = [xx] + [pltpu.roll(xx, s, 0) for s in range(1, CONV_K)]
        pre = b_ref[...]
        for s in range(CONV_K):
            pre = pre + rolled[s][SUBLANES:, :] * w_ref[3 - s:4 - s, :]
        da_ext = jnp.concatenate([dam_ref[0], jnp.where(i < nt - 1, daa_ref[0], 0.0)], axis=0)
        sg = _sigmoid(pre)
        dpre = da_ext * sg * (1.0 + pre * (1.0 - sg))
        du = dpre[:tt, :] * w_ref[3:4, :]
        for s in range(1, CONV_K):
            du = du + pltpu.roll(dpre, n2 - s, 0)[:tt, :] * w_ref[3 - s:4 - s, :]
        du_ref[0] = du.astype(BF16)
        dpm = dpre[:tt, :]
        gws = [jnp.sum(dpm * rolled[3 - kk][SUBLANES:SUBLANES + tt, :], axis=0, keepdims=True) for kk in range(CONV_K)]

        @pl.when((bi == 0) & (i == 0))
        def _():
            gw_ref[...] = jnp.zeros_like(gw_ref)
            gb_ref[...] = jnp.zeros_like(gb_ref)

        gw_ref[...] += jnp.concatenate(gws, axis=0)
        gb_ref[...] += jnp.sum(dpm, axis=0, keepdims=True)

    main = pl.BlockSpec((1, tt, CONV_TC), lambda c, b, i: (b, i, c))
    prev = pl.BlockSpec((1, SUBLANES, CONV_TC), lambda c, b, i: (b, jnp.maximum(i * hb - 1, 0), c))
    nxt = pl.BlockSpec((1, SUBLANES, CONV_TC), lambda c, b, i: (b, jnp.minimum((i + 1) * hb, seq // SUBLANES - 1), c))
    du, gw, gb = pl.pallas_call(
        body, name="conv_bwd", grid=(CONV_CH // CONV_TC, bsz, nt),
        in_specs=[main, nxt, main, prev, nxt,
                  pl.BlockSpec((CONV_K, CONV_TC), lambda c, b, i: (0, c)),
                  pl.BlockSpec((1, CONV_TC), lambda c, b, i: (0, c))],
        out_specs=[main, pl.BlockSpec((CONV_K, CONV_TC), lambda c, b, i: (0, c)),
                   pl.BlockSpec((1, CONV_TC), lambda c, b, i: (0, c))],
        out_shape=[jax.ShapeDtypeStruct((bsz, seq, CONV_CH), BF16), jax.ShapeDtypeStruct((CONV_K, CONV_CH), F32),
                   jax.ShapeDtypeStruct((1, CONV_CH), F32)],
        compiler_params=_params(("parallel", "arbitrary", "arbitrary")),
    )(da3, da3, x3, x3, x3, cw, cb)
    return du.reshape(bsz * seq, CONV_CH), gw, gb


def _ssd_constants():
    tri = (_iota((CHUNK, CHUNK), 1) <= _iota((CHUNK, CHUNK), 0)).astype(BF16)
    return tri, tri.T, _head_expand().astype(BF16), _head_reduce().astype(BF16)


def _ssd_prologue(dtf_ref, dtft_ref, bias_ref, biast_ref, arow_ref, acol_ref, tri_ref, triu_ref, expand_ref,
                  dtfull_scr, acfull_scr, acr_scr):
    dtc = _softplus(dtf_ref[0] + bias_ref[...])
    a = dtc * arow_ref[...]
    acum = _sel_dot(tri_ref[...], a)
    expand = expand_ref[...]
    dtfull_scr[...] = _dot_sel(dtc, expand)
    acfull_scr[...] = _dot_sel(acum, expand)
    dtr = _softplus(dtft_ref[0] + biast_ref[...])
    acr_scr[...] = _dot_sel(dtr * acol_ref[...], triu_ref[...])
    return dtc, acum


def _decay_matrix(acum, acr_scr, h):
    lane = _iota((CHUNK, LANES), 1)
    ac_col = jnp.sum(jnp.where(lane == h, acum, 0.0), axis=1, keepdims=True)
    ac_row = acr_scr[h:h + 1, :]
    causal = _iota((CHUNK, CHUNK), 1) <= _iota((CHUNK, CHUNK), 0)
    return jnp.exp(jnp.where(causal, ac_col - ac_row, NEG))


def _ssd_fwd(xbc, dtf, dtft, bias, biast, arow, acol, dfull, bsz, seq):
    nc = seq // CHUNK
    x3 = xbc.reshape(bsz, seq, CONV_CH)
    tri, triu, expand, _ = _ssd_constants()

    def body(xbc_ref, dtf_ref, dtft_ref, bias_ref, biast_ref, arow_ref, acol_ref, dfull_ref,
             tri_ref, triu_ref, expand_ref, y_ref, hall_ref, h_scr, dtfull_scr, acfull_scr, acr_scr):
        @pl.when(pl.program_id(1) == 0)
        def _():
            h_scr[...] = jnp.zeros_like(h_scr)

        _, acum = _ssd_prologue(dtf_ref, dtft_ref, bias_ref, biast_ref, arow_ref, acol_ref,
                                tri_ref, triu_ref, expand_ref, dtfull_scr, acfull_scr, acr_scr)
        lane = _iota((CHUNK, LANES), 1)
        for g in range(2):
            bg = xbc_ref[0, :, D_MODEL + LANES * g:D_MODEL + LANES * (g + 1)]
            cg = xbc_ref[0, :, D_MODEL + 2 * LANES + LANES * g:D_MODEL + 2 * LANES + LANES * (g + 1)]
            cg_bf = cg.astype(BF16)
            gmat = _bdot_nt(cg_bf, bg)
            bgt_bf = bg.T.astype(BF16)
            for j in range(4):
                p = 4 * g + j
                sl = slice(LANES * p, LANES * (p + 1))
                xs = xbc_ref[0, :, sl]
                ac = acfull_scr[:, sl]
                acl = acfull_scr[CHUNK - 1:CHUNK, sl]
                xdt = xs * dtfull_scr[:, sl]
                xdt_bf = xdt.astype(BF16)
                hp = h_scr[p]
                hall_ref[0, 0, p] = hp
                yo = _bdot(cg_bf, hp) * jnp.exp(ac)
                yd = []
                for hh in range(2):
                    mmat = gmat * _decay_matrix(acum, acr_scr, 2 * p + hh)
                    yd.append(_bdot(mmat, xdt_bf))
                y_ref[0, :, sl] = jnp.where(lane < HEAD_DIM, yd[0], yd[1]) + yo + dfull_ref[:, sl] * xs
                h_scr[p] = hp * jnp.exp(acl) + _bdot(bgt_bf, xdt * jnp.exp(acl - ac))

    row = lambda w: pl.BlockSpec((1, w), lambda b, c: (0, 0))
    whole = lambda a: pl.BlockSpec(a.shape, lambda b, c: (0, 0))
    y, hall = pl.pallas_call(
        body, name="ssd_fwd", grid=(bsz, nc),
        in_specs=[pl.BlockSpec((1, CHUNK, CONV_CH), lambda b, c: (b, c, 0)),
                  pl.BlockSpec((1, CHUNK, LANES), lambda b, c: (b, c, 0)),
                  pl.BlockSpec((1, LANES, CHUNK), lambda b, c: (b, 0, c)),
                  row(LANES), pl.BlockSpec((LANES, 1), lambda b, c: (0, 0)),
                  row(LANES), pl.BlockSpec((LANES, 1), lambda b, c: (0, 0)), row(D_MODEL),
                  whole(tri), whole(triu), whole(expand)],
        out_specs=[pl.BlockSpec((1, CHUNK, D_MODEL), lambda b, c: (b, c, 0)),
                   pl.BlockSpec((1, 1, HEAD_PAIRS, SSD_STATE, LANES), lambda b, c: (b, c, 0, 0, 0))],
        out_shape=[jax.ShapeDtypeStruct((bsz, seq, D_MODEL), F32),
                   jax.ShapeDtypeStruct((bsz, nc, HEAD_PAIRS, SSD_STATE, LANES), F32)],
        scratch_shapes=[pltpu.VMEM((HEAD_PAIRS, SSD_STATE, LANES), F32), pltpu.VMEM((CHUNK, D_MODEL), F32),
                        pltpu.VMEM((CHUNK, D_MODEL), F32), pltpu.VMEM((LANES, CHUNK), F32)],
        compiler_params=_params(("parallel", "arbitrary")),
    )(x3, dtf.reshape(bsz, seq, LANES), dtft, bias, biast, arow, acol, dfull, tri, triu, expand)
    return y.reshape(bsz * seq, D_MODEL), hall


def _ssd_bwd(dy, xbc, dtf, dtft, bias, biast, arow, acol, dfull, hall, bsz, seq):
    nc = seq // CHUNK
    x3 = xbc.reshape(bsz, seq, CONV_CH)
    dy3 = dy.reshape(bsz, seq, D_MODEL)
    consts = _ssd_constants()

    def body(dy_ref, xbc_ref, dtf_ref, dtft_ref, bias_ref, biast_ref, arow_ref, acol_ref, dfull_ref, hall_ref,
             tri_ref, triu_ref, expand_ref, reduce_ref,
             dx_ref, ddt_ref, acc_ref, dh_scr, dtfull_scr, acfull_scr, acr_scr, csum_scr, dacf_scr, ddtf_scr, ddl_scr):
        first = (pl.program_id(0) == 0) & (pl.program_id(1) == 0)

        @pl.when(first)
        def _():
            acc_ref[...] = jnp.zeros_like(acc_ref)

        @pl.when(pl.program_id(1) == 0)
        def _():
            dh_scr[...] = jnp.zeros_like(dh_scr)

        dtc, acum = _ssd_prologue(dtf_ref, dtft_ref, bias_ref, biast_ref, arow_ref, acol_ref,
                                  tri_ref, triu_ref, expand_ref, dtfull_scr, acfull_scr, acr_scr)
        csum_scr[...] = jnp.zeros_like(csum_scr)
        lane = _iota((CHUNK, LANES), 1)
        last_row = _iota((CHUNK, LANES), 0) == CHUNK - 1
        rs16 = jnp.zeros((CHUNK, LANES), F32)
        for g in range(2):
            bsl = slice(D_MODEL + LANES * g, D_MODEL + LANES * (g + 1))
            csl = slice(D_MODEL + 2 * LANES + LANES * g, D_MODEL + 2 * LANES + LANES * (g + 1))
            bg_bf = xbc_ref[0, :, bsl].astype(BF16)
            cg = xbc_ref[0, :, csl]
            cg_bf = cg.astype(BF16)
            cgt_bf = cg.T.astype(BF16)
            gmat = _bdot_nt(cg_bf, bg_bf)
            dg = jnp.zeros((CHUNK, CHUNK), F32)
            dbg = jnp.zeros((CHUNK, SSD_STATE), F32)
            dcg = jnp.zeros((CHUNK, SSD_STATE), F32)
            for j in range(4):
                p = 4 * g + j
                sl = slice(LANES * p, LANES * (p + 1))
                xs = xbc_ref[0, :, sl]
                dt = dtfull_scr[:, sl]
                ac = acfull_scr[:, sl]
                acl = acfull_scr[CHUNK - 1:CHUNK, sl]
                dyp = dy_ref[0, :, sl]
                xdt = xs * dt
                xdt_bf = xdt.astype(BF16)
                e = jnp.exp(ac)
                dsd = jnp.exp(acl - ac)
                cd = jnp.exp(acl)
                xds_bf = (xdt * dsd).astype(BF16)
                hp = hall_ref[0, 0, p]
                hp_bf = hp.astype(BF16)
                dhn = dh_scr[p]
                dhn_bf = dhn.astype(BF16)
                yo = _bdot(cg_bf, hp_bf) * e
                dw_bf = (dyp * e).astype(BF16)
                dcg = dcg + _bdot_nt(dw_bf, hp_bf)
                dhin = _bdot(cgt_bf, dw_bf)
                dac = dyp * yo
                dacl = jnp.sum(dhn * hp, axis=0, keepdims=True) * cd
                dxds = _bdot(bg_bf, dhn_bf)
                dbg = dbg + _bdot_nt(xds_bf, dhn_bf)
                dxdt = dxds * dsd
                tdec = dxds * xdt * dsd
                dacl = dacl + jnp.sum(tdec, axis=0, keepdims=True)
                dac = dac - tdec
                dh_scr[p] = dhn * cd + dhin
                for hh in range(2):
                    h = 2 * p + hh
                    lm = (lane < HEAD_DIM) if hh == 0 else (lane >= HEAD_DIM)
                    dec = _decay_matrix(acum, acr_scr, h)
                    mmat = gmat * dec
                    dyh_bf = jnp.where(lm, dyp, 0.0).astype(BF16)
                    dm = _bdot_nt(dyh_bf, xdt_bf)
                    dxdt = dxdt + _bdot(mmat.T, dyh_bf)
                    dg = dg + dm * dec
                    q = dm * mmat
                    rs16 = rs16 + jnp.where(lane == h, jnp.sum(q, axis=1, keepdims=True), 0.0)
                    csum_scr[h:h + 1, :] = jnp.sum(q, axis=0, keepdims=True)
                dx_ref[0, :, sl] = dfull_ref[:, sl] * dyp + dxdt * dt
                dacf_scr[:, sl] = dac + jnp.where(last_row, dacl, 0.0)
                ddtf_scr[:, sl] = dxdt * xs
                ddl_scr[:, sl] = jnp.broadcast_to(jnp.sum(dyp * xs, axis=0, keepdims=True), (SUBLANES, LANES))
            dg_bf = dg.astype(BF16)
            dx_ref[0, :, bsl] = dbg + _bdot(dg.T, cg_bf)
            dx_ref[0, :, csl] = dcg + _bdot(dg_bf, bg_bf)
        reduce, triu = reduce_ref[...], triu_ref[...]
        dac16 = _dot_sel(dacf_scr[...], reduce) + rs16 - csum_scr[...].T
        da16 = _sel_dot(triu, dac16)
        ddt16 = da16 * arow_ref[...] + _dot_sel(ddtf_scr[...], reduce)
        ddtraw = ddt16 * _sigmoid(dtf_ref[0] + bias_ref[...])
        ddt_ref[0] = ddtraw
        acc_ref[0:8, :] += jnp.broadcast_to(jnp.sum(da16 * dtc * arow_ref[...], axis=0, keepdims=True), (SUBLANES, LANES))
        acc_ref[8:16, :] += _dot_sel(ddl_scr[...], reduce)
        acc_ref[16:24, :] += jnp.broadcast_to(jnp.sum(ddtraw, axis=0, keepdims=True), (SUBLANES, LANES))

    rev = lambda b, c: (b, nc - 1 - c, 0)
    row = lambda w: pl.BlockSpec((1, w), lambda b, c: (0, 0))
    dx, ddt, acc = pl.pallas_call(
        body, name="ssd_bwd", grid=(bsz, nc),
        in_specs=[pl.BlockSpec((1, CHUNK, D_MODEL), rev), pl.BlockSpec((1, CHUNK, CONV_CH), rev),
                  pl.BlockSpec((1, CHUNK, LANES), rev),
                  pl.BlockSpec((1, LANES, CHUNK), lambda b, c: (b, 0, nc - 1 - c)),
                  row(LANES), pl.BlockSpec((LANES, 1), lambda b, c: (0, 0)),
                  row(LANES), pl.BlockSpec((LANES, 1), lambda b, c: (0, 0)), row(D_MODEL),
                  pl.BlockSpec((1, 1, HEAD_PAIRS, SSD_STATE, LANES), lambda b, c: (b, nc - 1 - c, 0, 0, 0))]
        + [pl.BlockSpec(a.shape, lambda b, c: (0, 0)) for a in consts],
        out_specs=[pl.BlockSpec((1, CHUNK, CONV_CH), rev), pl.BlockSpec((1, CHUNK, LANES), rev),
                   pl.BlockSpec((3 * SUBLANES, LANES), lambda b, c: (0, 0))],
        out_shape=[jax.ShapeDtypeStruct((bsz, seq, CONV_CH), F32), jax.ShapeDtypeStruct((bsz, seq, LANES), F32),
                   jax.ShapeDtypeStruct((3 * SUBLANES, LANES), F32)],
        scratch_shapes=[pltpu.VMEM((HEAD_PAIRS, SSD_STATE, LANES), F32), pltpu.VMEM((CHUNK, D_MODEL), F32),
                        pltpu.VMEM((CHUNK, D_MODEL), F32), pltpu.VMEM((LANES, CHUNK), F32),
                        pltpu.VMEM((LANES, CHUNK), F32), pltpu.VMEM((CHUNK, D_MODEL), F32),
                        pltpu.VMEM((CHUNK, D_MODEL), F32), pltpu.VMEM((SUBLANES, D_MODEL), F32)],
        compiler_params=_params(("arbitrary", "arbitrary")),
    )(dy3, x3, dtf.reshape(bsz, seq, LANES), dtft, bias, biast, arow, acol, dfull, hall, *consts)
    return dx.reshape(bsz * seq, CONV_CH), ddt.reshape(bsz * seq, LANES), acc


GROUP_W = D_MODEL // 2


def _gnorm_fwd(y, z, o_att, w):
    n = y.shape[0]
    tm = min(1024, n)

    def body(y_ref, z_ref, o_ref, w_ref, out_ref):
        zv = z_ref[...]
        g = y_ref[...] * (zv * _sigmoid(zv))
        for gi in range(2):
            sl = slice(GROUP_W * gi, GROUP_W * (gi + 1))
            gs = g[:, sl]
            r = lax.rsqrt(jnp.mean(gs * gs, axis=-1, keepdims=True) + NORM_EPS)
            out_ref[:, sl] = (gs * r * w_ref[:, sl]).astype(BF16)
        out_ref[:, D_MODEL:] = o_ref[...].astype(BF16)

    tok = pl.BlockSpec((tm, D_MODEL), lambda i: (i, 0))
    return pl.pallas_call(
        body, name="gnorm_fwd", grid=(n // tm,),
        in_specs=[tok, tok, tok, pl.BlockSpec((1, D_MODEL), lambda i: (0, 0))],
        out_specs=pl.BlockSpec((tm, MIX_WIDTH), lambda i: (i, 0)),
        out_shape=jax.ShapeDtypeStruct((n, MIX_WIDTH), BF16),
        compiler_params=_params(("parallel",)),
    )(y, z, o_att, w)


def _gnorm_bwd(dycat, y, z, w):
    n = y.shape[0]
    tm = min(512, n)

    def body(dn_ref, y_ref, z_ref, w_ref, dy_ref, dz_ref, gw_ref):
        zv, yv = z_ref[...], y_ref[...]
        sz = _sigmoid(zv)
        sil = zv * sz
        g = yv * sil

        @pl.when(pl.program_id(0) == 0)
        def _():
            gw_ref[...] = jnp.zeros_like(gw_ref)

        for gi in range(2):
            sl = slice(GROUP_W * gi, GROUP_W * (gi + 1))
            gs = g[:, sl]
            r = lax.rsqrt(jnp.mean(gs * gs, axis=-1, keepdims=True) + NORM_EPS)
            nrm = gs * r
            dyn = dn_ref[:, sl]
            dn = dyn * w_ref[:, sl]
            dgs = r * (dn - nrm * jnp.mean(dn * nrm, axis=-1, keepdims=True))
            dy_ref[:, sl] = dgs * sil[:, sl]
            dz_ref[:, sl] = (dgs * yv[:, sl] * (sz[:, sl] * (1.0 + zv[:, sl] * (1.0 - sz[:, sl])))).astype(BF16)
            gw_ref[:, sl] += jnp.sum(dyn * nrm, axis=0, keepdims=True)

    tok = pl.BlockSpec((tm, D_MODEL), lambda i: (i, 0))
    row = pl.BlockSpec((1, D_MODEL), lambda i: (0, 0))
    return pl.pallas_call(
        body, name="gnorm_bwd", grid=(n // tm,),
        in_specs=[tok, tok, tok, row], out_specs=[tok, tok, row],
        out_shape=[jax.ShapeDtypeStruct((n, D_MODEL), F32), jax.ShapeDtypeStruct((n, D_MODEL), BF16),
                   jax.ShapeDtypeStruct((1, D_MODEL), F32)],
        compiler_params=_params(("arbitrary",)),
    )(dycat, y, z, w)


AUX_C = 3
AUX_ONE = 3


def _aux_base(hh):
    return HEAD_DIM * (1 - hh)


def _fox_prep(dtf, fb, bsz, seq):
    def body(x_ref, b_ref, aux_ref):
        tri = (_iota((CHUNK, CHUNK), 1) <= _iota((CHUNK, CHUNK), 0)).astype(F32)
        row, col = _iota((LANES, D_MODEL), 0), _iota((LANES, D_MODEL), 1)
        lane = jnp.bitwise_and(col, LANES - 1)
        other = (lane < HEAD_DIM).astype(jnp.int32)
        term = lane - HEAD_DIM * (1 - other)
        src = F_COL + 2 * jnp.right_shift(col, 7) + other
        place = [jnp.where((row == src) & (term == i), -1.0, 0.0).astype(BF16) for i in range(AUX_C)]
        lane1 = jnp.bitwise_and(_iota((1, D_MODEL), 1), LANES - 1)
        ones_lane = (lane1 - HEAD_DIM * (1 - (lane1 < HEAD_DIM).astype(jnp.int32)) == AUX_ONE).astype(F32)
        carry = jnp.zeros((1, LANES), F32)
        for j in range(seq // CHUNK):
            sl = slice(CHUNK * j, CHUNK * (j + 1))
            lf = _log_sigmoid(x_ref[sl, :] + b_ref[...])
            c = _sel_dot(tri, lf) + carry
            carry = carry + jnp.sum(lf, axis=0, keepdims=True)
            t1, t2, t3 = (jnp.dot(t, p, preferred_element_type=F32) for t, p in zip(_split3(c), place))
            aux_ref[sl, :] = (t1 + t2 + t3 + ones_lane).astype(BF16)

    return pl.pallas_call(
        body, name="fox_prep", grid=(bsz,),
        in_specs=[pl.BlockSpec((seq, LANES), lambda b: (b, 0)), pl.BlockSpec((1, LANES), lambda b: (0, 0))],
        out_specs=pl.BlockSpec((seq, D_MODEL), lambda b: (b, 0)),
        out_shape=jax.ShapeDtypeStruct((bsz * seq, D_MODEL), BF16),
        compiler_params=_params(("parallel",)),
    )(dtf, fb)


def _fox_prep_bwd(dck, dcq, ddt, dtf, fb, bsz, seq):
    nj = seq // CHUNK

    def body(dck_ref, dcq_ref, ddt_ref, x_ref, b_ref, out_ref, gb_ref):
        triu = (_iota((CHUNK, CHUNK), 0) <= _iota((CHUNK, CHUNK), 1)).astype(F32)
        is_f = (_iota((CHUNK, LANES), 1) >= F_COL) & (_iota((CHUNK, LANES), 1) < 2 * F_COL)
        carry = jnp.zeros((1, LANES), F32)
        total = jnp.zeros((1, LANES), F32)
        for j in range(nj - 1, -1, -1):
            sl = slice(CHUNK * j, CHUNK * (j + 1))
            dcv = dck_ref[sl, :] + dcq_ref[sl, :]
            dlf = _sel_dot(triu, dcv) + carry
            carry = carry + jnp.sum(dcv, axis=0, keepdims=True)
            df = jnp.where(is_f, dlf * _sigmoid(-(x_ref[sl, :] + b_ref[...])), 0.0)
            out_ref[sl, :] = (df + ddt_ref[sl, :]).astype(BF16)
            total = total + jnp.sum(df, axis=0, keepdims=True)

        @pl.when(pl.program_id(0) == 0)
        def _():
            gb_ref[...] = jnp.zeros_like(gb_ref)

        gb_ref[...] += jnp.broadcast_to(total, (SUBLANES, LANES))

    blk = pl.BlockSpec((seq, LANES), lambda b: (b, 0))
    return pl.pallas_call(
        body, name="fox_prep_bwd", grid=(bsz,),
        in_specs=[blk, blk, blk, blk, pl.BlockSpec((1, LANES), lambda b: (0, 0))],
        out_specs=[blk, pl.BlockSpec((SUBLANES, LANES), lambda b: (0, 0))],
        out_shape=[jax.ShapeDtypeStruct((bsz * seq, LANES), BF16), jax.ShapeDtypeStruct((SUBLANES, LANES), F32)],
        compiler_params=_params(("arbitrary",)),
    )(dck, dcq, ddt, dtf, fb)


ATT_BLOCK_FWD = 1024
ATT_BLOCK_KEYS = 256
ATT_BLOCK_BWD = 256


def _att_operands(q_ref, k_ref, aux_ref, hh):
    lane = _iota((1, LANES), 1)
    lm = (lane < HEAD_DIM) if hh == 0 else (lane >= HEAD_DIM)
    base = _aux_base(hh)
    zero = jnp.zeros((1, LANES), BF16)
    ka = jnp.where(lm, k_ref[...], jnp.where((lane >= base) & (lane <= base + AUX_ONE), aux_ref[...], zero))
    qa = jnp.where(lm, q_ref[...] * ATT_SCALE,
                   jnp.where((lane >= base) & (lane < base + AUX_C), jnp.ones((1, LANES), BF16), zero))
    return lm, base, qa, ka


def _att_fwd(qkv, ccol, bsz, seq, gather):
    bq, bk = min(ATT_BLOCK_FWD, seq), min(ATT_BLOCK_KEYS, seq)
    nq, ratio = seq // bq, bq // bk
    nx = len(gather)

    def body(*refs):
        q_ref, k_ref, v_ref, c_ref = refs[:4]
        x_refs = refs[4:4 + nx]
        o_ref, lse_ref = refs[4 + nx:6 + nx]
        xo_refs = refs[6 + nx:6 + 2 * nx]
        qa_scr, ka_scr, vt_scr = refs[6 + 2 * nx:9 + 2 * nx]
        sems = refs[9 + 2 * nx:]
        pair = pl.program_id(1)

        @pl.when((pl.program_id(0) == 0) & (pair == 0))
        def _():
            _exchange_start("gather", x_refs, xo_refs, *sems)

        lse_ref[...] = jnp.zeros_like(lse_ref)
        for hh in range(2):
            _, _, qa, ka = _att_operands(q_ref, k_ref, c_ref, hh)
            qa_scr[hh] = qa
            ka_scr[hh] = ka
        for t0 in range(0, seq, bq):
            vt_scr[:, t0:t0 + bq] = v_ref[t0:t0 + bq, :].T
        key_minus_query = _iota((bk, bq), 0) - _iota((bk, bq), 1)

        def q_step(i, carry0):
            qrows = pl.ds(pl.multiple_of(i * bq, bq), bq)

            def scores(j):
                krows = pl.ds(pl.multiple_of(j * bk, bk), bk)
                return tuple(_bdot_nt(ka_scr[hh, krows, :], qa_scr[hh, qrows, :]) for hh in range(2))

            def scores_tail(r):
                krows = pl.ds(pl.multiple_of((i * ratio + r) * bk, bk), bk)
                qsub = pl.ds(pl.multiple_of(i * bq + r * bk, bk), bq - r * bk)
                return tuple(_bdot_nt(ka_scr[hh, krows, :], qa_scr[hh, qsub, :]) for hh in range(2))

            def update(state, st_pair, j, masked):
                krows = pl.ds(pl.multiple_of(j * bk, bk), bk)
                out = []
                for hh in range(2):
                    m_all, l_all, acc_all = state[hh]
                    st = st_pair[hh]
                    width = st.shape[1]
                    lo = bq - width
                    m, l, acc = m_all[:, lo:], l_all[:, lo:], acc_all[:, lo:]
                    if masked:
                        st = jnp.where(key_minus_query[:, :width] <= 0, st, NEG)
                    mn = jnp.maximum(m, jnp.max(st, axis=0, keepdims=True))
                    alpha = jnp.exp(m - mn)
                    pt = jnp.exp(st - mn)
                    l = alpha * l + jnp.sum(pt, axis=0, keepdims=True)
                    vt = vt_scr[HEAD_DIM * hh:HEAD_DIM * (hh + 1), krows]
                    acc = alpha * acc + _bdot(vt, pt)
                    if lo:
                        mn, l, acc = (jnp.concatenate([old[:, :lo], new], axis=1)
                                      for old, new in ((m_all, mn), (l_all, l), (acc_all, acc)))
                    out.append((mn, l, acc))
                return tuple(out)

            def step(j, carry):
                state, s_cur = carry
                s_next = scores(j + 1)
                return update(state, s_cur, j, False), s_next

            def step_pair(t, carry):
                state, s_cur = carry
                s_second = scores(2 * t + 1)
                s_next = scores(2 * t + 2)
                state = update(state, s_cur, 2 * t, False)
                return update(state, s_second, 2 * t + 1, False), s_next

            one = (jnp.full((1, bq), NEG, F32), jnp.zeros((1, bq), F32), jnp.zeros((HEAD_DIM, bq), F32))
            first_diag = i * ratio
            if ratio % 2 == 0:
                state, s_cur = lax.fori_loop(0, first_diag // 2, step_pair, ((one, one), scores(0)))
            else:
                state, s_cur = lax.fori_loop(0, first_diag, step, ((one, one), scores(0)))
            for r in range(ratio):
                s_next = scores_tail(r + 1) if r + 1 < ratio else None
                state = update(state, s_cur, first_diag + r, True)
                s_cur = s_next
            (m0, l0, acc0), (m1, l1, acc1) = state
            ot = jnp.concatenate([acc0 * (1.0 / l0), acc1 * (1.0 / l1)], axis=0)
            o_ref[qrows, :] = ot.T
            lse_ref[0, 0, 0:1, qrows] = m0 + jnp.log(l0)
            lse_ref[0, 0, 1:2, qrows] = m1 + jnp.log(l1)
            return carry0

        lax.fori_loop(0, nq, q_step, 0)

        @pl.when((pl.program_id(0) == bsz - 1) & (pair == HEAD_PAIRS - 1))
        def _():
            _exchange_wait("gather", x_refs, xo_refs, *sems)

    col = lambda off: pl.BlockSpec((seq, LANES), lambda b, p: (b, off + p))
    head2 = pltpu.VMEM((2, seq, LANES), BF16)
    hbm = pl.BlockSpec(memory_space=pl.ANY)
    return pl.pallas_call(
        body, name="att_fwd", grid=(bsz, HEAD_PAIRS),
        in_specs=[col(0), col(HEAD_PAIRS), col(2 * HEAD_PAIRS), col(0)] + [hbm] * nx,
        out_specs=[pl.BlockSpec((seq, LANES), lambda b, p: (b, p)),
                   pl.BlockSpec((1, 1, SUBLANES, seq), lambda b, p: (b, p, 0, 0))] + [hbm] * nx,
        out_shape=[jax.ShapeDtypeStruct((bsz * seq, D_MODEL), F32),
                   jax.ShapeDtypeStruct((bsz, HEAD_PAIRS, SUBLANES, seq), F32)] + _exchange_shapes("gather", gather),
        scratch_shapes=[head2, head2, pltpu.VMEM((LANES, seq), BF16)] + _exchange_scratch(nx),
        compiler_params=_params(("arbitrary", "arbitrary")),
    )(qkv, qkv, qkv, ccol, *gather)


def _att_bwd(qkv, dycat, o, lse, ccol, bsz, seq, scatter):
    blk = bq = min(ATT_BLOCK_BWD, seq)
    nb = nq = seq // blk
    nx = len(scatter)

    def body(*refs):
        q_ref, k_ref, v_ref, do_ref, o_ref, lse_ref, c_ref = refs[:7]
        x_refs = refs[7:7 + nx]
        dq_ref, dk_ref, dv_ref, dck_ref, dcq_ref = refs[7 + nx:12 + nx]
        xo_refs = refs[12 + nx:12 + 2 * nx]
        (qa_scr, ka_scr, va_scr, doa_scr, kat_scr, qat_scr, dot_scr, d_scr, dqt_scr, dkt_scr,
         dvt_scr) = refs[12 + 2 * nx:23 + 2 * nx]
        sems = refs[23 + 2 * nx:]
        pair = pl.program_id(1)

        @pl.when((pl.program_id(0) == 0) & (pair == 0))
        def _():
            _exchange_start("scatter", x_refs, xo_refs, *sems)

        query_minus_key = _iota((blk, bq), 1) - _iota((blk, bq), 0)
        lane_b = _iota((blk, LANES), 1)
        row_t = _iota((LANES, seq), 0)
        bases = []
        ones8 = jnp.ones((SUBLANES, LANES), F32)
        for hh in range(2):
            lm, base, qa, ka = _att_operands(q_ref, k_ref, c_ref, hh)
            bases.append(base)
            qa_scr[hh] = qa
            ka_scr[hh] = ka
            va_scr[hh] = jnp.where(lm, v_ref[...], jnp.zeros((seq, LANES), BF16))
            doa = jnp.where(lm, do_ref[...], 0.0).astype(BF16)
            doa_scr[hh] = doa
            for t0 in range(0, seq, blk):
                kat_scr[hh, :, t0:t0 + blk] = ka[t0:t0 + blk, :].T
                qat_scr[hh, :, t0:t0 + blk] = qa[t0:t0 + blk, :].T
            d1, d2, d3 = (_bdot_nt(ones8, term) for term in _split3(doa.astype(F32) * o_ref[...]))
            d_scr[hh] = d1 + d2 + d3
        for t0 in range(0, seq, blk):
            dot_scr[:, t0:t0 + blk] = do_ref[t0:t0 + blk, :].astype(BF16).T
        dqt_scr[...] = jnp.zeros_like(dqt_scr)
        dck_ref[...] = jnp.zeros_like(dck_ref)

        def kv_step(j, carry0):
            krows = pl.ds(pl.multiple_of(j * blk, blk), blk)
            dkt_scr[...] = jnp.zeros_like(dkt_scr)
            dvt_scr[...] = jnp.zeros_like(dvt_scr)

            def scores(i):
                rows = pl.ds(pl.multiple_of(i * bq, bq), bq)
                return tuple((_bdot_nt(ka_scr[hh, krows, :], qa_scr[hh, rows, :]),
                              _bdot_nt(va_scr[hh, krows, :], doa_scr[hh, rows, :])) for hh in range(2))

            def update(i, sd, masked):
                rows = pl.ds(pl.multiple_of(i * bq, bq), bq)
                for hh in range(2):
                    st, dpt = sd[hh]
                    if masked:
                        st = jnp.where(query_minus_key >= 0, st, NEG)
                    pt = jnp.exp(st - lse_ref[0, 0, hh:hh + 1, rows])
                    dst = (pt * (dpt - d_scr[hh, 0:1, rows])).astype(BF16)
                    dvt_scr[hh] += _bdot_nt(dot_scr[HEAD_DIM * hh:HEAD_DIM * (hh + 1), rows], pt)
                    dkt_scr[hh] += _bdot_nt(qat_scr[hh, :, rows], dst)
                    dqt_scr[hh, :, rows] += _bdot(kat_scr[hh, :, krows], dst)

            i_diag = j

            def q_pair(t, sd_cur):
                i = i_diag + 1 + 2 * t
                sd_second = scores(i + 1)
                sd_next = scores(jnp.minimum(i + 2, nq - 1))
                update(i, sd_cur, False)
                update(i + 1, sd_second, False)
                return sd_next

            sd_diag = scores(i_diag)
            sd_first = scores(jnp.minimum(i_diag + 1, nq - 1))
            update(i_diag, sd_diag, True)
            rest = nq - 1 - i_diag
            sd_last = lax.fori_loop(0, rest // 2, q_pair, sd_first)

            @pl.when(rest % 2 == 1)
            def _():
                update(jnp.int32(nq - 1), sd_last, False)
            dkt = jnp.where(_iota((LANES, blk), 0) < HEAD_DIM, dkt_scr[0], dkt_scr[1])
            dk_ref[krows, :] = dkt.T.astype(BF16)
            dv_ref[krows, :] = jnp.concatenate([dvt_scr[0], dvt_scr[1]], axis=0).T.astype(BF16)
            for hh in range(2):
                dck_ref[0, 0, hh:hh + 1, krows] = -dkt_scr[hh, bases[hh]:bases[hh] + 1, :]
            return carry0

        lax.fori_loop(0, nb, kv_step, 0)
        for t0 in range(0, seq, blk):
            dq0, dq1 = dqt_scr[0, :, t0:t0 + blk].T, dqt_scr[1, :, t0:t0 + blk].T
            dq_ref[t0:t0 + blk, :] = (jnp.where(lane_b < HEAD_DIM, dq0, dq1) * ATT_SCALE).astype(BF16)
        dcq_ref[...] = jnp.zeros_like(dcq_ref)
        for hh in range(2):
            dcq_ref[0, 0, hh:hh + 1, :] = jnp.sum(jnp.where(row_t == bases[hh] + AUX_ONE, dqt_scr[hh], 0.0),
                                                   axis=0, keepdims=True)

        @pl.when((pl.program_id(0) == bsz - 1) & (pair == HEAD_PAIRS - 1))
        def _():
            _exchange_wait("scatter", x_refs, xo_refs, *sems)

    col = lambda off: pl.BlockSpec((seq, LANES), lambda b, p: (b, off + p))
    rowvec = pl.BlockSpec((1, 1, SUBLANES, seq), lambda b, p: (b, p, 0, 0))
    out = jax.ShapeDtypeStruct((bsz * seq, D_MODEL), BF16)
    rows_shape = jax.ShapeDtypeStruct((bsz, HEAD_PAIRS, SUBLANES, seq), F32)
    head2 = pltpu.VMEM((2, seq, LANES), BF16)
    hbm = pl.BlockSpec(memory_space=pl.ANY)
    return pl.pallas_call(
        body, name="att_bwd", grid=(bsz, HEAD_PAIRS),
        in_specs=[col(0), col(HEAD_PAIRS), col(2 * HEAD_PAIRS), col(HEAD_PAIRS), col(0), rowvec, col(0)] + [hbm] * nx,
        out_specs=[col(0), col(0), col(0), rowvec, rowvec] + [hbm] * nx,
        out_shape=[out, out, out, rows_shape, rows_shape] + _exchange_shapes("scatter", scatter),
        scratch_shapes=[head2, head2, head2, head2, pltpu.VMEM((2, LANES, seq), BF16),
                        pltpu.VMEM((2, LANES, seq), BF16), pltpu.VMEM((LANES, seq), BF16),
                        pltpu.VMEM((2, SUBLANES, seq), F32), pltpu.VMEM((2, LANES, seq), F32),
                        pltpu.VMEM((2, LANES, blk), F32), pltpu.VMEM((2, HEAD_DIM, blk), F32)] + _exchange_scratch(nx),
        compiler_params=_params(("arbitrary", "arbitrary")),
    )(qkv, qkv, qkv, dycat, o, lse, ccol, *scatter)


def _adamw_math(w, g, m, v):
    m = ADAM_B1 * m + (1.0 - ADAM_B1) * g
    v = ADAM_B2 * v + (1.0 - ADAM_B2) * jnp.square(g)
    m_hat = m / ADAM_C1
    v_hat = v / ADAM_C2
    delta = -ADAM_LR * (m_hat / (jnp.sqrt(v_hat) + ADAM_EPS) + ADAM_WD * w)
    return delta, m, v


def _row_tile(rows):
    for tr in (256, 128, 64, 32, 16, 8):
        if rows % tr == 0:
            return tr
    return rows


def _sum_parts(parts, name):
    nparts, rows, cols = parts.shape
    tr = rows if rows % SUBLANES else _row_tile(rows)

    def body(p_ref, o_ref):
        g = p_ref[0].astype(F32)
        for s in range(1, nparts):
            g = g + p_ref[s].astype(F32)
        o_ref[...] = g

    return pl.pallas_call(
        body, name=name, grid=(rows // tr,),
        in_specs=[pl.BlockSpec((nparts, tr, cols), lambda i: (0, i, 0))],
        out_specs=pl.BlockSpec((tr, cols), lambda i: (i, 0)),
        out_shape=jax.ShapeDtypeStruct((rows, cols), F32),
        compiler_params=_params(("parallel",)),
    )(parts)


def _adamw_parts(parts, w, m, v, name):
    nparts, rows, cols = parts.shape
    tr = _row_tile(rows)

    def body(p_ref, w_ref, m_ref, v_ref, g_ref, d_ref, mo_ref, vo_ref):
        g = p_ref[0].astype(F32)
        for s in range(1, nparts):
            g = g + p_ref[s].astype(F32)
        delta, mn, vn = _adamw_math(w_ref[...], g, m_ref[...], v_ref[...])
        g_ref[...] = g
        d_ref[...] = delta
        mo_ref[...] = mn
        vo_ref[...] = vn

    blk = pl.BlockSpec((tr, cols), lambda i: (i, 0))
    shp = jax.ShapeDtypeStruct((rows, cols), F32)
    return pl.pallas_call(
        body, name=name, grid=(rows // tr,),
        in_specs=[pl.BlockSpec((nparts, tr, cols), lambda i: (0, i, 0)), blk, blk, blk],
        out_specs=[blk, blk, blk, blk], out_shape=[shp, shp, shp, shp],
        compiler_params=_params(("parallel",)),
    )(parts, w, m, v)


def _me():
    return lax.axis_index("x"), lax.axis_index("y"), lax.axis_index("c")


def _flip(pos, k):
    x, y, c = pos
    kx, ky, kc = (k >> 2) & 1, (k >> 1) & 1, k & 1
    return (x ^ kx if kx else x, y ^ ky if ky else y, c ^ kc if kc else c)


def _logical(pos):
    return 4 * pos[0] + 2 * pos[1] + pos[2]


def _all_gather_two_level(shards):
    narr = len(shards)

    def body(*refs):
        x_refs, out_refs = refs[:narr], refs[narr:2 * narr]
        send_sems, recv_sems, local_sems = refs[2 * narr:]
        x, y, c = _me()
        me, sibling = (x, y, c), (x, y, 1 - c)
        chips = [(1 - x, y), (x, 1 - y), (1 - x, 1 - y)]

        def copy(a, k, block, to, src=None):
            slot = out_refs[a].at[_logical(block)]
            return pltpu.make_async_remote_copy(
                src_ref=slot if src is None else src, dst_ref=slot,
                send_sem=send_sems.at[a, k], recv_sem=recv_sems.at[a, k], device_id=to, device_id_type=MESH)

        mine = [pltpu.make_async_copy(x_refs[a], out_refs[a].at[_logical(me)], local_sems.at[a]) for a in range(narr)]
        for cp in mine:
            cp.start()
        first = []
        for a in range(narr):
            first.append(copy(a, 0, me, sibling, src=x_refs[a]))
            first += [copy(a, 1 + j, me, (*chip, c), src=x_refs[a]) for j, chip in enumerate(chips)]
        for cp in first:
            cp.start()
        passed = []
        for a in range(narr):
            for j, chip in enumerate(chips):
                copy(a, 1 + j, (*chip, c), me).wait_recv()
                fwd = copy(a, 4 + j, (*chip, c), sibling)
                fwd.start()
                passed.append(fwd)
        for a in range(narr):
            copy(a, 0, sibling, me).wait_recv()
            for j, chip in enumerate(chips):
                copy(a, 4 + j, (*chip, 1 - c), me).wait_recv()
        for cp in first + passed:
            cp.wait_send()
        for cp in mine:
            cp.wait()

    hbm = pl.BlockSpec(memory_space=pl.ANY)
    return pl.pallas_call(
        body, name="all_gather_big",
        out_shape=[jax.ShapeDtypeStruct((N_DEV,) + s.shape, s.dtype) for s in shards],
        in_specs=[hbm] * narr, out_specs=[hbm] * narr,
        scratch_shapes=[pltpu.SemaphoreType.DMA((narr, 7)), pltpu.SemaphoreType.DMA((narr, 7)),
                        pltpu.SemaphoreType.DMA((narr,))],
    )(*shards)


def _exchange_copies(kind, x_refs, out_refs, send_sems, recv_sems, local_sems):
    me = _me()
    mine = _logical(me)
    local, remote = [], []
    for a, (x_ref, out_ref) in enumerate(zip(x_refs, out_refs)):
        own = x_ref if kind == "gather" else x_ref.at[mine]
        local.append(pltpu.make_async_copy(own, out_ref.at[mine], local_sems.at[a]))
        for k in range(1, N_DEV):
            peer = _flip(me, k)
            src = x_ref if kind == "gather" else x_ref.at[_logical(peer)]
            remote.append(pltpu.make_async_remote_copy(
                src_ref=src, dst_ref=out_ref.at[mine], send_sem=send_sems.at[a, k - 1], recv_sem=recv_sems.at[a, k - 1],
                device_id=peer, device_id_type=MESH))
    return local, remote


def _exchange_start(kind, x_refs, out_refs, *sems):
    if not x_refs:
        return
    local, remote = _exchange_copies(kind, x_refs, out_refs, *sems)
    for cp in local + remote:
        cp.start()


def _exchange_wait(kind, x_refs, out_refs, *sems):
    if not x_refs:
        return
    local, remote = _exchange_copies(kind, x_refs, out_refs, *sems)
    for cp in remote:
        cp.wait_recv()
    for cp in remote:
        cp.wait_send()
    for cp in local:
        cp.wait()


def _exchange_shapes(kind, arrays):
    return [jax.ShapeDtypeStruct(((N_DEV,) if kind == "gather" else ()) + a.shape, a.dtype) for a in arrays]


def _exchange_scratch(narrays):
    if not narrays:
        return []
    return [pltpu.SemaphoreType.DMA((narrays, N_DEV - 1)), pltpu.SemaphoreType.DMA((narrays, N_DEV - 1)),
            pltpu.SemaphoreType.DMA((narrays,))]


def _exchange_rows(x_ref, buf_ref, send_sems, recv_sems):
    me = _me()
    buf_ref[_logical(me)] = x_ref[...]
    copies = []
    for k in range(1, N_DEV):
        peer = _flip(me, k)
        copies.append(pltpu.make_async_remote_copy(
            src_ref=x_ref, dst_ref=buf_ref.at[_logical(me)],
            send_sem=send_sems.at[k - 1], recv_sem=recv_sems.at[k - 1], device_id=peer, device_id_type=MESH))
    for cp in copies:
        cp.start()
    for cp in copies:
        cp.wait_recv()
    for cp in copies:
        cp.wait_send()


def _sum_slots(buf_ref):
    total = buf_ref[0]
    for s in range(1, N_DEV):
        total = total + buf_ref[s]
    return total


def _small_gather(x):
    def body(x_ref, o_ref, buf_ref, send_sems, recv_sems):
        _exchange_rows(x_ref, buf_ref, send_sems, recv_sems)
        o_ref[...] = _sum_slots(buf_ref)

    return pl.pallas_call(
        body, name="small_gather", out_shape=jax.ShapeDtypeStruct(x.shape, F32),
        in_specs=[pl.BlockSpec(memory_space=pltpu.VMEM)], out_specs=pl.BlockSpec(memory_space=pltpu.VMEM),
        scratch_shapes=[pltpu.VMEM((N_DEV,) + x.shape, F32), pltpu.SemaphoreType.DMA((7,)), pltpu.SemaphoreType.DMA((7,))],
    )(x)


def _small_reduce_adamw(g, w, m, v):
    def body(g_ref, w_ref, m_ref, v_ref, go_ref, d_ref, mo_ref, vo_ref, buf_ref, send_sems, recv_sems):
        _exchange_rows(g_ref, buf_ref, send_sems, recv_sems)
        gs = _sum_slots(buf_ref)
        delta, mn, vn = _adamw_math(w_ref[...], gs, m_ref[...], v_ref[...])
        go_ref[...] = gs
        d_ref[...] = delta
        mo_ref[...] = mn
        vo_ref[...] = vn

    vm = pl.BlockSpec(memory_space=pltpu.VMEM)
    shp = jax.ShapeDtypeStruct(g.shape, F32)
    return pl.pallas_call(
        body, name="small_reduce_adamw", out_shape=[shp, shp, shp, shp],
        in_specs=[vm, vm, vm, vm], out_specs=[vm, vm, vm, vm],
        scratch_shapes=[pltpu.VMEM((N_DEV,) + g.shape, F32), pltpu.SemaphoreType.DMA((7,)), pltpu.SemaphoreType.DMA((7,))],
    )(g, w, m, v)


def _pad_cols(a, width):
    return jnp.pad(a, ((0, 0), (0, width - a.shape[1])))


def _local_step(x, target, norm_mix_w, w_in_t, conv_w, conv_b, dt_bias, a_log, d_skip, ssd_norm_w, f_bias,
                late_shards, norm_mlp_w, norm_final_w):
    bsz, seq, _ = x.shape
    n = bsz * seq
    x2 = x.reshape(n, D_MODEL)
    t2 = target.reshape(n, D_MODEL)
    nfw = norm_final_w.reshape(1, D_MODEL)

    bias = _pad_cols(dt_bias, LANES)
    arow = _pad_cols(-jnp.exp(a_log), LANES)
    biast, acol = bias.reshape(LANES, 1), arow.reshape(LANES, 1)
    dfull = jnp.repeat(d_skip, HEAD_DIM, axis=1)
    fb = jnp.pad(f_bias, ((0, 0), (F_COL, LANES - 2 * F_COL)))

    wt_z, wt_xbc, wt_qkv = w_in_t[:ROW_XBC], w_in_t[ROW_XBC:ROW_DT], w_in_t[ROW_Q:ROW_F]
    wt_dtf = jnp.concatenate([w_in_t[ROW_DT:ROW_Q], w_in_t[ROW_F:], jnp.zeros((LANES - 2 * F_COL, D_MODEL), BF16)], axis=0)
    wt_q, wt_k, wt_v = wt_qkv[:D_MODEL], wt_qkv[D_MODEL:2 * D_MODEL], wt_qkv[2 * D_MODEL:]

    h1 = _rms_fwd(x2, norm_mix_w, "rms_fwd_mix")
    z = _mm([(h1, wt_z)], "inproj_z", F32, 1024, 1024, nt=True)
    xbc_raw = _mm([(h1, wt_xbc)], "inproj_xbc", F32, 512, 1536, nt=True)
    qkv = _mm([(h1, wt_qkv)], "inproj_qkv", BF16, 512, 3072, nt=True)
    dtf = _mm([(h1, wt_dtf)], "inproj_dtf", F32, 2048, LANES, nt=True)
    dtft = dtf.reshape(bsz, seq, LANES).transpose(0, 2, 1)

    xbc = _conv_fwd(xbc_raw, conv_w, conv_b, bsz, seq)
    y_pre, hall = _ssd_fwd(xbc, dtf, dtft, bias, biast, arow, acol, dfull, bsz, seq)

    ccol = _fox_prep(dtf, fb, bsz, seq)
    o_att, lse, w_out, w_up_t, w_down = _att_fwd(qkv, ccol, bsz, seq, late_shards)
    w_out, w_up_t, w_down = (w.reshape(-1, D_MODEL) for w in (w_out, w_up_t, w_down))

    ycat = _gnorm_fwd(y_pre, z, o_att, ssd_norm_w)
    h2, h2n = _mm([(ycat, w_out)], "outproj", F32, 512, 1024, res=x2, rms_fwd=norm_mlp_w)
    pre = _mm([(h2n, w_up_t)], "mlp_up", BF16, 512, 4096, nt=True)

    dh3, loss_row, g_nfw = _mlp_down_loss(pre, w_down, h2, t2, nfw)
    dpre, u = _mlp_down_bwd(dh3, w_down, pre)
    g_w_down = _mm_tn(u, dh3, "grad_w_down", 1024, 1024, 2048)
    g_w_up_t = _mm_tn(dpre, h2n, "grad_w_up", 1024, 1024, 2048)
    by_shard = lambda g: g.reshape(N_DEV, g.shape[0] // N_DEV, D_MODEL)
    dh2, g_nmlp = _mm([(dpre, w_up_t)], "mlp_up_bwd", F32, 512, 1024, res=dh3, rms_bwd=(h2, norm_mlp_w))

    g_w_out = _mm_tn(ycat, dh2, "grad_w_out", 1024, 1024, 2048)
    dycat = _mm([(dh2, w_out)], "outproj_bwd", F32, 512, 2048, nt=True)
    dy_pre, dz, g_ssdn = _gnorm_bwd(dycat, y_pre, z, ssd_norm_w)
    dq, dk, dv, dck, dcq, recv_down, recv_up_t, recv_out = _att_bwd(
        qkv, dycat, o_att, lse, ccol, bsz, seq, [by_shard(g_w_down), by_shard(g_w_up_t), by_shard(g_w_out)])

    dxbc, ddt, ssd_acc = _ssd_bwd(dy_pre, xbc, dtf, dtft, bias, biast, arow, acol, dfull, hall, bsz, seq)
    dxbc_raw, g_cw, g_cb = _conv_bwd(dxbc, xbc_raw, conv_w, conv_b, bsz, seq)

    def head_cols(rows):
        cols = rows[:, :, :2, :].reshape(bsz, SSD_HEADS, seq).transpose(0, 2, 1).reshape(n, SSD_HEADS)
        return jnp.pad(cols, ((0, 0), (F_COL, LANES - 2 * F_COL)))

    ddtf, g_fb = _fox_prep_bwd(head_cols(dck), head_cols(dcq), ddt, dtf, fb, bsz, seq)

    g_dtf = _mm_tn(ddtf, h1, "grad_w_in_dtf", LANES, 1024, 2048)
    g_w_in_t = jnp.concatenate([
        _mm_tn(dz, h1, "grad_w_in_z", 1024, 1024, 2048), _mm_tn(dxbc_raw, h1, "grad_w_in_xbc", 768, 1024, 2048),
        g_dtf[:F_COL], _mm_tn(dq, h1, "grad_w_in_q", 1024, 1024, 2048), _mm_tn(dk, h1, "grad_w_in_k", 1024, 1024, 2048),
        _mm_tn(dv, h1, "grad_w_in_v", 1024, 1024, 2048), g_dtf[F_COL:2 * F_COL]], axis=0)
    pieces = [(dz, wt_z), (dxbc_raw, wt_xbc), (dq, wt_q), (dk, wt_k), (dv, wt_v), (ddtf, wt_dtf)]
    dx, g_nmix, recv_in_t = _mm(pieces, "inproj_bwd", F32, 256, 1024, res=dh2, rms_bwd=(x2, norm_mix_w),
                                exchange=("scatter", [by_shard(g_w_in_t)]))

    small = dict(
        norm_mix_w=g_nmix, norm_mlp_w=g_nmlp, norm_final_w=g_nfw, ssd_norm_w=g_ssdn, conv_b=g_cb, conv_w=g_cw,
        dt_bias=ssd_acc[16:17, :SSD_HEADS], a_log=ssd_acc[0:1, :SSD_HEADS], d_skip=ssd_acc[8:9, :SSD_HEADS],
        f_bias=g_fb[0:1, F_COL:2 * F_COL])
    recv = dict(w_in_t=recv_in_t, w_out=recv_out, w_up_t=recv_up_t, w_down=recv_down)
    return loss_row[0, 0], dx.reshape(bsz, seq, D_MODEL), small, recv


SMALL_LAYOUT = (("norm_mix_w", 0, 1, 0, D_MODEL), ("norm_mlp_w", 1, 1, 0, D_MODEL), ("norm_final_w", 2, 1, 0, D_MODEL),
                ("ssd_norm_w", 3, 1, 0, D_MODEL), ("conv_b", 4, 1, 0, CONV_CH), ("conv_w", 5, CONV_K, 0, CONV_CH),
                ("dt_bias", 9, 1, 0, SSD_HEADS), ("a_log", 9, 1, LANES, SSD_HEADS), ("d_skip", 9, 1, 2 * LANES, SSD_HEADS),
                ("f_bias", 9, 1, 3 * LANES, SSD_HEADS))
SMALL_ROWS = 2 * SUBLANES


def _pack_small(vals):
    packed = jnp.zeros((SMALL_ROWS, SMALL_COLS), F32)
    for name, r0, nrows, c0, width in SMALL_LAYOUT:
        packed = packed.at[r0:r0 + nrows, c0:c0 + width].set(vals[name].reshape(nrows, width))
    return packed


def _unpack_small(packed, like):
    out = {}
    for name, r0, nrows, c0, width in SMALL_LAYOUT:
        out[name] = packed[r0:r0 + nrows, c0:c0 + width].reshape(like[name].shape)
    return out


def kernel(x, norm_mix_w, w_in, conv_w, conv_b, dt_bias, a_log, d_skip, ssd_norm_w, f_bias, w_out, norm_mlp_w, w_up, w_down, norm_final_w, loss_target, m_norm_mix_w, m_w_in, m_conv_w, m_conv_b, m_dt_bias, m_a_log, m_d_skip, m_ssd_norm_w, m_f_bias, m_w_out, m_norm_mlp_w, m_w_up, m_w_down, m_norm_final_w, v_norm_mix_w, v_w_in, v_conv_w, v_conv_b, v_dt_bias, v_a_log, v_d_skip, v_ssd_norm_w, v_f_bias, v_w_out, v_norm_mlp_w, v_w_up, v_w_down, v_norm_final_w):
    me = 4 * lax.axis_index("x") + 2 * lax.axis_index("y") + lax.axis_index("c")
    conv_shard_w = CONV_CH // N_DEV
    zero = jnp.zeros((), jnp.int32)

    def place_conv(shard):
        return lax.dynamic_update_slice(jnp.zeros((CONV_K, CONV_CH), F32), shard[0], (zero, me * conv_shard_w))

    (g_in_t,) = _all_gather_two_level([w_in[0].T.astype(BF16)])
    late_shards = [w_out[0].astype(BF16), w_up[0].T.astype(BF16), w_down[0].astype(BF16)]
    conv_full = _small_gather(jnp.pad(place_conv(conv_w), ((0, SUBLANES - CONV_K), (0, 0))))[:CONV_K]

    loss_local, grad_x, g_small, recv = _local_step(
        x, loss_target, norm_mix_w, g_in_t.reshape(IN_WIDTH, D_MODEL), conv_full, conv_b, dt_bias, a_log, d_skip,
        ssd_norm_w, f_bias, late_shards, norm_mlp_w, norm_final_w)
    loss = lax.psum(loss_local, MESH_AXES)

    grad_in = _sum_parts(recv["w_in_t"], "sum_w_in").T[None]
    grad_up = _sum_parts(recv["w_up_t"], "sum_w_up").T[None]
    big = {
        "w_in": _adamw_parts(grad_in, w_in[0], m_w_in[0], v_w_in[0], "adamw_w_in"),
        "w_out": _adamw_parts(recv["w_out"], w_out[0], m_w_out[0], v_w_out[0], "adamw_w_out"),
        "w_up": _adamw_parts(grad_up, w_up[0], m_w_up[0], v_w_up[0], "adamw_w_up"),
        "w_down": _adamw_parts(recv["w_down"], w_down[0], m_w_down[0], v_w_down[0], "adamw_w_down"),
    }

    like = dict(norm_mix_w=norm_mix_w, norm_mlp_w=norm_mlp_w, norm_final_w=norm_final_w, ssd_norm_w=ssd_norm_w,
                conv_b=conv_b, conv_w=jnp.zeros((1, CONV_K, CONV_CH), F32), dt_bias=dt_bias, a_log=a_log,
                d_skip=d_skip, f_bias=f_bias)
    w_small = dict(like, conv_w=place_conv(conv_w))
    m_small = dict(norm_mix_w=m_norm_mix_w, norm_mlp_w=m_norm_mlp_w, norm_final_w=m_norm_final_w,
                   ssd_norm_w=m_ssd_norm_w, conv_b=m_conv_b, conv_w=place_conv(m_conv_w), dt_bias=m_dt_bias,
                   a_log=m_a_log, d_skip=m_d_skip, f_bias=m_f_bias)
    v_small = dict(norm_mix_w=v_norm_mix_w, norm_mlp_w=v_norm_mlp_w, norm_final_w=v_norm_final_w,
                   ssd_norm_w=v_ssd_norm_w, conv_b=v_conv_b, conv_w=place_conv(v_conv_w), dt_bias=v_dt_bias,
                   a_log=v_a_log, d_skip=v_d_skip, f_bias=v_f_bias)
    small = _small_reduce_adamw(_pack_small(g_small), _pack_small(w_small), _pack_small(m_small), _pack_small(v_small))
    small = [_unpack_small(s, like) for s in small]
    for s in small:
        s["conv_w"] = lax.dynamic_slice(s["conv_w"], (zero, zero, me * conv_shard_w), (1, CONV_K, conv_shard_w))

    order = ["norm_mix_w", "w_in", "conv_w", "conv_b", "dt_bias", "a_log", "d_skip", "ssd_norm_w", "f_bias", "w_out",
             "norm_mlp_w", "w_up", "w_down", "norm_final_w"]
    outs = [loss, grad_x]
    for kind in range(4):
        for name in order:
            outs.append(big[name][kind][None] if name in big else small[kind][name])
    return tuple(outs)
```

```python
import jax
import jax.numpy as jnp
from jax import lax
from jax.experimental import pallas as pl
from jax.experimental.pallas import tpu as pltpu

F32, BF16 = jnp.float32, jnp.bfloat16

D_MODEL = 1024
SSD_HEADS = 16
HEAD_DIM = 64
SSD_STATE = 128
CHUNK = 128
CONV_CH = 1536
CONV_K = 4
D_FF = 4096
MIX_WIDTH = 2048
IN_WIDTH = 5664
NORM_EPS = 1e-5
ATT_SCALE = 0.125

LANES = 128
SUBLANES = 8
HEAD_PAIRS = SSD_HEADS // 2
NEG = -1e30
VMEM_LIMIT = 52 * 1024 * 1024

ROW_XBC, ROW_DT, ROW_Q, ROW_F = 1024, 2560, 2576, 5648
F_COL = SSD_HEADS

N_DEV = 8
MESH_AXES = ("x", "y", "c")
MESH = pl.DeviceIdType.MESH

ADAM_LR, ADAM_B1, ADAM_B2, ADAM_EPS, ADAM_WD, ADAM_STEP = 0.001, 0.9, 0.999, 1e-08, 0.01, 10
ADAM_C1 = 1.0 - ADAM_B1 ** ADAM_STEP
ADAM_C2 = 1.0 - ADAM_B2 ** ADAM_STEP

SMALL_COLS = CONV_CH


def _params(sem=None):
    return pltpu.CompilerParams(dimension_semantics=sem, vmem_limit_bytes=VMEM_LIMIT)


def _sigmoid(u):
    return 1.0 / (1.0 + jnp.exp(-u))


def _softplus(u):
    return jnp.maximum(u, 0.0) + jnp.log(1.0 + jnp.exp(-jnp.abs(u)))


def _log_sigmoid(u):
    return jnp.minimum(u, 0.0) - jnp.log(1.0 + jnp.exp(-jnp.abs(u)))


def _bdot(a, b):
    return jnp.dot(a.astype(BF16), b.astype(BF16), preferred_element_type=F32)


def _bdot_nt(a, b):
    return lax.dot_general(a.astype(BF16), b.astype(BF16), (((1,), (1,)), ((), ())), preferred_element_type=F32)


def _bdot_tn(a, b):
    return lax.dot_general(a.astype(BF16), b.astype(BF16), (((0,), (0,)), ((), ())), preferred_element_type=F32)


def _split3(x):
    x1 = x.astype(BF16)
    r1 = x - x1.astype(F32)
    x2 = r1.astype(BF16)
    return x1, x2, (r1 - x2.astype(F32)).astype(BF16)


def _dot_sel(x, sel):
    s = sel.astype(BF16)
    p1, p2, p3 = (jnp.dot(p, s, preferred_element_type=F32) for p in _split3(x))
    return p1 + p2 + p3


def _sel_dot(sel, x):
    s = sel.astype(BF16)
    p1, p2, p3 = (jnp.dot(s, p, preferred_element_type=F32) for p in _split3(x))
    return p1 + p2 + p3


def _iota(shape, dim):
    return lax.broadcasted_iota(jnp.int32, shape, dim)


def _head_expand():
    return (jnp.right_shift(_iota((LANES, D_MODEL), 1), 6) == _iota((LANES, D_MODEL), 0)).astype(F32)


def _head_reduce():
    return (jnp.right_shift(_iota((D_MODEL, LANES), 0), 6) == _iota((D_MODEL, LANES), 1)).astype(F32)


def _rms_fwd(x, w, name):
    n, d = x.shape
    tm = min(1024, n)

    def body(x_ref, w_ref, o_ref):
        xv = x_ref[...]
        r = lax.rsqrt(jnp.mean(xv * xv, axis=-1, keepdims=True) + NORM_EPS)
        o_ref[...] = (xv * r * w_ref[...]).astype(BF16)

    return pl.pallas_call(
        body, name=name, grid=(n // tm,),
        in_specs=[pl.BlockSpec((tm, d), lambda i: (i, 0)), pl.BlockSpec((1, d), lambda i: (0, 0))],
        out_specs=pl.BlockSpec((tm, d), lambda i: (i, 0)),
        out_shape=jax.ShapeDtypeStruct((n, d), BF16),
        compiler_params=_params(("parallel",)),
    )(x, w)


def _mm(pairs, name, out_dtype, tm, tn, nt=False, res=None, exchange=None, rms_fwd=None, rms_bwd=None):
    m = pairs[0][0].shape[0]
    n = pairs[0][1].shape[0 if nt else 1]
    tm, tn = min(tm, m), min(tn, n)
    gm, gn = m // tm, n // tn
    npairs = len(pairs)
    kind, xs = (None, []) if exchange is None else exchange
    nx = len(xs)
    extra_in = [] if res is None else [res]
    if rms_bwd is not None:
        extra_in += list(rms_bwd)
    elif rms_fwd is not None:
        extra_in.append(rms_fwd)
    n_in = 2 * npairs + len(extra_in)
    n_out = 1 + (rms_fwd is not None or rms_bwd is not None)
    assert (rms_fwd is None and rms_bwd is None) or tn == n

    def body(*refs):
        x_refs, o_refs = refs[n_in:n_in + nx], refs[n_in + nx:n_in + nx + n_out]
        xo_refs, sems = refs[n_in + nx + n_out:n_in + 2 * nx + n_out], refs[n_in + 2 * nx + n_out:]
        extra = refs[2 * npairs:n_in]
        i, j = pl.program_id(0), pl.program_id(1)
        if nx:
            @pl.when((i == 0) & (j == 0))
            def _():
                _exchange_start(kind, x_refs, xo_refs, *sems)

        acc = None
        for p in range(npairs):
            a_v, b_v = refs[2 * p][...], refs[2 * p + 1][...]
            d = _bdot_nt(a_v, b_v) if nt else _bdot(a_v, b_v)
            acc = d if acc is None else acc + d
        if rms_bwd is not None:
            xv = extra[1][...]
            r = lax.rsqrt(jnp.mean(xv * xv, axis=-1, keepdims=True) + NORM_EPS)
            nrm = xv * r
            g = acc * extra[2][...]
            o_refs[0][...] = extra[0][...] + r * (g - nrm * jnp.mean(g * nrm, axis=-1, keepdims=True))

            @pl.when(i == 0)
            def _():
                o_refs[1][...] = jnp.zeros_like(o_refs[1])

            o_refs[1][...] += jnp.sum(acc * nrm, axis=0, keepdims=True)
        else:
            if res is not None:
                acc = extra[0][...] + acc
            o_refs[0][...] = acc.astype(o_refs[0].dtype)
            if rms_fwd is not None:
                r = lax.rsqrt(jnp.mean(acc * acc, axis=-1, keepdims=True) + NORM_EPS)
                o_refs[1][...] = (acc * r * extra[-1][...]).astype(BF16)
        if nx:
            @pl.when((i == gm - 1) & (j == gn - 1))
            def _():
                _exchange_wait(kind, x_refs, xo_refs, *sems)

    tile = pl.BlockSpec((tm, tn), lambda i, j: (i, j))
    row = pl.BlockSpec((1, tn), lambda i, j: (0, j))
    in_specs, args = [], []
    for a, b in pairs:
        k = a.shape[1]
        in_specs.append(pl.BlockSpec((tm, k), lambda i, j: (i, 0)))
        in_specs.append(pl.BlockSpec((tn, k), lambda i, j: (j, 0)) if nt else pl.BlockSpec((k, tn), lambda i, j: (0, j)))
        args += [a, b]
    in_specs += [row if e.shape[0] == 1 else tile for e in extra_in]
    out_specs, out_shape = [tile], [jax.ShapeDtypeStruct((m, n), out_dtype)]
    if rms_bwd is not None:
        out_specs.append(row)
        out_shape.append(jax.ShapeDtypeStruct((1, n), F32))
    elif rms_fwd is not None:
        out_specs.append(tile)
        out_shape.append(jax.ShapeDtypeStruct((m, n), BF16))
    hbm = pl.BlockSpec(memory_space=pl.ANY)
    sequential = nx or rms_bwd is not None
    outs = pl.pallas_call(
        body, name=name, grid=(gm, gn), in_specs=in_specs + [hbm] * nx,
        out_specs=out_specs + [hbm] * nx,
        out_shape=out_shape + _exchange_shapes(kind, xs),
        scratch_shapes=_exchange_scratch(nx),
        compiler_params=_params(("arbitrary", "arbitrary") if sequential else ("parallel", "parallel")),
    )(*args, *extra_in, *xs)
    return outs if len(outs) > 1 else outs[0]


def _mm_tn(a, b, name, tm, tn, tk):
    t, m = a.shape
    n = b.shape[1]
    tm, tn, tk = min(tm, m), min(tn, n), min(tk, t)
    nk = t // tk

    def body(a_ref, b_ref, o_ref, acc_ref):
        kk = pl.program_id(2)

        @pl.when(kk == 0)
        def _():
            acc_ref[...] = jnp.zeros_like(acc_ref)

        acc_ref[...] += _bdot_tn(a_ref[...], b_ref[...])

        @pl.when(kk == nk - 1)
        def _():
            o_ref[...] = acc_ref[...].astype(o_ref.dtype)

    return pl.pallas_call(
        body, name=name, grid=(m // tm, n // tn, nk),
        in_specs=[pl.BlockSpec((tk, tm), lambda i, j, kk: (kk, i)), pl.BlockSpec((tk, tn), lambda i, j, kk: (kk, j))],
        out_specs=pl.BlockSpec((tm, tn), lambda i, j, kk: (i, j)),
        out_shape=jax.ShapeDtypeStruct((m, n), BF16),
        scratch_shapes=[pltpu.VMEM((tm, tn), F32)],
        compiler_params=_params(("parallel", "parallel", "arbitrary")),
    )(a, b)


def _mlp_down_loss(pre, w_down, h2, target, w):
    m, k = pre.shape
    d = w_down.shape[1]
    tm = min(512, m)

    def body(p_ref, b_ref, r_ref, t_ref, w_ref, dh_ref, loss_ref, gw_ref):
        u = jnp.square(jnp.maximum(p_ref[...].astype(F32), 0.0))
        xv = r_ref[...] + _bdot(u, b_ref[...])
        r = lax.rsqrt(jnp.mean(xv * xv, axis=-1, keepdims=True) + NORM_EPS)
        nrm = xv * r
        wv = w_ref[...]
        err = nrm * wv - t_ref[...]
        part = 0.5 * jnp.sum(jnp.mean(err * err, axis=-1, keepdims=True), axis=0, keepdims=True)
        dy = err * (1.0 / d)
        g = dy * wv
        dh_ref[...] = r * (g - nrm * jnp.mean(g * nrm, axis=-1, keepdims=True))

        @pl.when(pl.program_id(0) == 0)
        def _():
            loss_ref[...] = jnp.zeros_like(loss_ref)
            gw_ref[...] = jnp.zeros_like(gw_ref)

        loss_ref[...] += jnp.broadcast_to(part, loss_ref.shape)
        gw_ref[...] += jnp.sum(dy * nrm, axis=0, keepdims=True)

    tok = pl.BlockSpec((tm, d), lambda i: (i, 0))
    row = pl.BlockSpec((1, d), lambda i: (0, 0))
    return pl.pallas_call(
        body, name="mlp_down_loss", grid=(m // tm,),
        in_specs=[pl.BlockSpec((tm, k), lambda i: (i, 0)), pl.BlockSpec((k, d), lambda i: (0, 0)), tok, tok, row],
        out_specs=[tok, pl.BlockSpec((1, LANES), lambda i: (0, 0)), row],
        out_shape=[jax.ShapeDtypeStruct((m, d), F32), jax.ShapeDtypeStruct((1, LANES), F32),
                   jax.ShapeDtypeStruct((1, d), F32)],
        compiler_params=_params(("arbitrary",)),
    )(pre, w_down, h2, target, w)


def _mlp_down_bwd(dh3, w_down, pre):
    m, k = dh3.shape
    n = w_down.shape[0]
    tm, tn = min(256, m), n

    def body(a_ref, b_ref, p_ref, dpre_ref, u_ref):
        r = jnp.maximum(p_ref[...].astype(F32), 0.0)
        du = _bdot_nt(a_ref[...], b_ref[...])
        dpre_ref[...] = (du * (2.0 * r)).astype(BF16)
        u_ref[...] = (r * r).astype(BF16)

    blk = pl.BlockSpec((tm, tn), lambda i, j: (i, j))
    return pl.pallas_call(
        body, name="mlp_down_bwd", grid=(m // tm, n // tn),
        in_specs=[pl.BlockSpec((tm, k), lambda i, j: (i, 0)), pl.BlockSpec((tn, k), lambda i, j: (j, 0)), blk],
        out_specs=[blk, blk],
        out_shape=[jax.ShapeDtypeStruct((m, n), BF16), jax.ShapeDtypeStruct((m, n), BF16)],
        compiler_params=_params(("parallel", "parallel")),
    )(dh3, w_down, pre)


CONV_TC = 512


def _conv_fwd(xbc, cw, cb, bsz, seq):
    tt = min(512, seq)
    nt, hb = seq // tt, tt // SUBLANES
    x3 = xbc.reshape(bsz, seq, CONV_CH)

    def body(xm_ref, xh_ref, w_ref, b_ref, o_ref):
        i = pl.program_id(2)
        x = xm_ref[0]
        halo = jnp.where(i > 0, xh_ref[0], 0.0)
        xx = jnp.concatenate([halo, x], axis=0)
        acc = x * w_ref[3:4, :] + b_ref[...]
        for s in range(1, CONV_K):
            acc = acc + pltpu.roll(xx, s, 0)[SUBLANES:, :] * w_ref[3 - s:4 - s, :]
        o_ref[0] = acc * _sigmoid(acc)

    out = pl.pallas_call(
        body, name="conv_fwd", grid=(CONV_CH // CONV_TC, bsz, nt),
        in_specs=[pl.BlockSpec((1, tt, CONV_TC), lambda c, b, i: (b, i, c)),
                  pl.BlockSpec((1, SUBLANES, CONV_TC), lambda c, b, i: (b, jnp.maximum(i * hb - 1, 0), c)),
                  pl.BlockSpec((CONV_K, CONV_TC), lambda c, b, i: (0, c)),
                  pl.BlockSpec((1, CONV_TC), lambda c, b, i: (0, c))],
        out_specs=pl.BlockSpec((1, tt, CONV_TC), lambda c, b, i: (b, i, c)),
        out_shape=jax.ShapeDtypeStruct((bsz, seq, CONV_CH), F32),
        compiler_params=_params(("parallel", "parallel", "parallel")),
    )(x3, x3, cw, cb)
    return out.reshape(bsz * seq, CONV_CH)


def _conv_bwd(da, xbc, cw, cb, bsz, seq):
    tt = min(512, seq)
    nt, hb = seq // tt, tt // SUBLANES
    x3 = xbc.reshape(bsz, seq, CONV_CH)
    da3 = da.reshape(bsz, seq, CONV_CH)
    n2 = tt + SUBLANES

    def body(dam_ref, daa_ref, xm_ref, xb_ref, xa_ref, w_ref, b_ref, du_ref, gw_ref, gb_ref):
        bi, i = pl.program_id(1), pl.program_id(2)
        before = jnp.where(i > 0, xb_ref[0], 0.0)
        after = jnp.where(i < nt - 1, xa_ref[0], 0.0)
        xx = jnp.concatenate([before, xm_ref[0], after], axis=0)
        rolled = [xx] + [pltpu.roll(xx, s, 0) for s in range(1, CONV_K)]
        pre = b_ref[...]
        for s in range(CONV_K):
            pre = pre + rolled[s][SUBLANES:, :] * w_ref[3 - s:4 - s, :]
        da_ext = jnp.concatenate([dam_ref[0], jnp.where(i < nt - 1, daa_ref[0], 0.0)], axis=0)
        sg = _sigmoid(pre)
        dpre = da_ext * sg * (1.0 + pre * (1.0 - sg))
        du = dpre[:tt, :] * w_ref[3:4, :]
        for s in range(1, CONV_K):
            du = du + pltpu.roll(dpre, n2 - s, 0)[:tt, :] * w_ref[3 - s:4 - s, :]
        du_ref[0] = du.astype(BF16)
        dpm = dpre[:tt, :]
        gws = [jnp.sum(dpm * rolled[3 - kk][SUBLANES:SUBLANES + tt, :], axis=0, keepdims=True) for kk in range(CONV_K)]

        @pl.when((bi == 0) & (i == 0))
        def _():
            gw_ref[...] = jnp.zeros_like(gw_ref)
            gb_ref[...] = jnp.zeros_like(gb_ref)

        gw_ref[...] += jnp.concatenate(gws, axis=0)
        gb_ref[...] += jnp.sum(dpm, axis=0, keepdims=True)

    main = pl.BlockSpec((1, tt, CONV_TC), lambda c, b, i: (b, i, c))
    prev = pl.BlockSpec((1, SUBLANES, CONV_TC), lambda c, b, i: (b, jnp.maximum(i * hb - 1, 0), c))
    nxt = pl.BlockSpec((1, SUBLANES, CONV_TC), lambda c, b, i: (b, jnp.minimum((i + 1) * hb, seq // SUBLANES - 1), c))
    du, gw, gb = pl.pallas_call(
        body, name="conv_bwd", grid=(CONV_CH // CONV_TC, bsz, nt),
        in_specs=[main, nxt, main, prev, nxt,
                  pl.BlockSpec((CONV_K, CONV_TC), lambda c, b, i: (0, c)),
                  pl.BlockSpec((1, CONV_TC), lambda c, b, i: (0, c))],
        out_specs=[main, pl.BlockSpec((CONV_K, CONV_TC), lambda c, b, i: (0, c)),
                   pl.BlockSpec((1, CONV_TC), lambda c, b, i: (0, c))],
        out_shape=[jax.ShapeDtypeStruct((bsz, seq, CONV_CH), BF16), jax.ShapeDtypeStruct((CONV_K, CONV_CH), F32),
                   jax.ShapeDtypeStruct((1, CONV_CH), F32)],
        compiler_params=_params(("parallel", "arbitrary", "arbitrary")),
    )(da3, da3, x3, x3, x3, cw, cb)
    return du.reshape(bsz * seq, CONV_CH), gw, gb


def _ssd_constants():
    tri = (_iota((CHUNK, CHUNK), 1) <= _iota((CHUNK, CHUNK), 0)).astype(BF16)
    return tri, tri.T, _head_expand().astype(BF16), _head_reduce().astype(BF16)


def _ssd_prologue(dtf_ref, dtft_ref, bias_ref, biast_ref, arow_ref, acol_ref, tri_ref, triu_ref, expand_ref,
                  dtfull_scr, acfull_scr, acr_scr):
    dtc = _softplus(dtf_ref[0] + bias_ref[...])
    a = dtc * arow_ref[...]
    acum = _sel_dot(tri_ref[...], a)
    expand = expand_ref[...]
    dtfull_scr[...] = _dot_sel(dtc, expand)
    acfull_scr[...] = _dot_sel(acum, expand)
    dtr = _softplus(dtft_ref[0] + biast_ref[...])
    acr_scr[...] = _dot_sel(dtr * acol_ref[...], triu_ref[...])
    return dtc, acum


def _decay_matrix(acum, acr_scr, h):
    lane = _iota((CHUNK, LANES), 1)
    ac_col = jnp.sum(jnp.where(lane == h, acum, 0.0), axis=1, keepdims=True)
    ac_row = acr_scr[h:h + 1, :]
    causal = _iota((CHUNK, CHUNK), 1) <= _iota((CHUNK, CHUNK), 0)
    return jnp.exp(jnp.where(causal, ac_col - ac_row, NEG))


def _ssd_fwd(xbc, dtf, dtft, bias, biast, arow, acol, dfull, bsz, seq):
    nc = seq // CHUNK
    x3 = xbc.reshape(bsz, seq, CONV_CH)
    tri, triu, expand, _ = _ssd_constants()

    def body(xbc_ref, dtf_ref, dtft_ref, bias_ref, biast_ref, arow_ref, acol_ref, dfull_ref,
             tri_ref, triu_ref, expand_ref, y_ref, hall_ref, h_scr, dtfull_scr, acfull_scr, acr_scr):
        @pl.when(pl.program_id(1) == 0)
        def _():
            h_scr[...] = jnp.zeros_like(h_scr)

        _, acum = _ssd_prologue(dtf_ref, dtft_ref, bias_ref, biast_ref, arow_ref, acol_ref,
                                tri_ref, triu_ref, expand_ref, dtfull_scr, acfull_scr, acr_scr)
        lane = _iota((CHUNK, LANES), 1)
        for g in range(2):
            bg = xbc_ref[0, :, D_MODEL + LANES * g:D_MODEL + LANES * (g + 1)]
            cg = xbc_ref[0, :, D_MODEL + 2 * LANES + LANES * g:D_MODEL + 2 * LANES + LANES * (g + 1)]
            cg_bf = cg.astype(BF16)
            gmat = _bdot_nt(cg_bf, bg)
            bgt_bf = bg.T.astype(BF16)
            for j in range(4):
                p = 4 * g + j
                sl = slice(LANES * p, LANES * (p + 1))
                xs = xbc_ref[0, :, sl]
                ac = acfull_scr[:, sl]
                acl = acfull_scr[CHUNK - 1:CHUNK, sl]
                xdt = xs * dtfull_scr[:, sl]
                xdt_bf = xdt.astype(BF16)
                hp = h_scr[p]
                hall_ref[0, 0, p] = hp
                yo = _bdot(cg_bf, hp) * jnp.exp(ac)
                yd = []
                for hh in range(2):
                    mmat = gmat * _decay_matrix(acum, acr_scr, 2 * p + hh)
                    yd.append(_bdot(mmat, xdt_bf))
                y_ref[0, :, sl] = jnp.where(lane < HEAD_DIM, yd[0], yd[1]) + yo + dfull_ref[:, sl] * xs
                h_scr[p] = hp * jnp.exp(acl) + _bdot(bgt_bf, xdt * jnp.exp(acl - ac))

    row = lambda w: pl.BlockSpec((1, w), lambda b, c: (0, 0))
    whole = lambda a: pl.BlockSpec(a.shape, lambda b, c: (0, 0))
    y, hall = pl.pallas_call(
        body, name="ssd_fwd", grid=(bsz, nc),
        in_specs=[pl.BlockSpec((1, CHUNK, CONV_CH), lambda b, c: (b, c, 0)),
                  pl.BlockSpec((1, CHUNK, LANES), lambda b, c: (b, c, 0)),
                  pl.BlockSpec((1, LANES, CHUNK), lambda b, c: (b, 0, c)),
                  row(LANES), pl.BlockSpec((LANES, 1), lambda b, c: (0, 0)),
                  row(LANES), pl.BlockSpec((LANES, 1), lambda b, c: (0, 0)), row(D_MODEL),
                  whole(tri), whole(triu), whole(expand)],
        out_specs=[pl.BlockSpec((1, CHUNK, D_MODEL), lambda b, c: (b, c, 0)),
                   pl.BlockSpec((1, 1, HEAD_PAIRS, SSD_STATE, LANES), lambda b, c: (b, c, 0, 0, 0))],
        out_shape=[jax.ShapeDtypeStruct((bsz, seq, D_MODEL), F32),
                   jax.ShapeDtypeStruct((bsz, nc, HEAD_PAIRS, SSD_STATE, LANES), F32)],
        scratch_shapes=[pltpu.VMEM((HEAD_PAIRS, SSD_STATE, LANES), F32), pltpu.VMEM((CHUNK, D_MODEL), F32),
                        pltpu.VMEM((CHUNK, D_MODEL), F32), pltpu.VMEM((LANES, CHUNK), F32)],
        compiler_params=_params(("parallel", "arbitrary")),
    )(x3, dtf.reshape(bsz, seq, LANES), dtft, bias, biast, arow, acol, dfull, tri, triu, expand)
    return y.reshape(bsz * seq, D_MODEL), hall


def _ssd_bwd(dy, xbc, dtf, dtft, bias, biast, arow, acol, dfull, hall, bsz, seq):
    nc = seq // CHUNK
    x3 = xbc.reshape(bsz, seq, CONV_CH)
    dy3 = dy.reshape(bsz, seq, D_MODEL)
    consts = _ssd_constants()

    def body(dy_ref, xbc_ref, dtf_ref, dtft_ref, bias_ref, biast_ref, arow_ref, acol_ref, dfull_ref, hall_ref,
             tri_ref, triu_ref, expand_ref, reduce_ref,
             dx_ref, ddt_ref, acc_ref, dh_scr, dtfull_scr, acfull_scr, acr_scr, csum_scr, dacf_scr, ddtf_scr, ddl_scr):
        first = (pl.program_id(0) == 0) & (pl.program_id(1) == 0)

        @pl.when(first)
        def _():
            acc_ref[...] = jnp.zeros_like(acc_ref)

        @pl.when(pl.program_id(1) == 0)
        def _():
            dh_scr[...] = jnp.zeros_like(dh_scr)

        dtc, acum = _ssd_prologue(dtf_ref, dtft_ref, bias_ref, biast_ref, arow_ref, acol_ref,
                                  tri_ref, triu_ref, expand_ref, dtfull_scr, acfull_scr, acr_scr)
        csum_scr[...] = jnp.zeros_like(csum_scr)
        lane = _iota((CHUNK, LANES), 1)
        last_row = _iota((CHUNK, LANES), 0) == CHUNK - 1
        rs16 = jnp.zeros((CHUNK, LANES), F32)
        for g in range(2):
            bsl = slice(D_MODEL + LANES * g, D_MODEL + LANES * (g + 1))
            csl = slice(D_MODEL + 2 * LANES + LANES * g, D_MODEL + 2 * LANES + LANES * (g + 1))
            bg_bf = xbc_ref[0, :, bsl].astype(BF16)
            cg = xbc_ref[0, :, csl]
            cg_bf = cg.astype(BF16)
            cgt_bf = cg.T.astype(BF16)
            gmat = _bdot_nt(cg_bf, bg_bf)
            dg = jnp.zeros((CHUNK, CHUNK), F32)
            dbg = jnp.zeros((CHUNK, SSD_STATE), F32)
            dcg = jnp.zeros((CHUNK, SSD_STATE), F32)
            for j in range(4):
                p = 4 * g + j
                sl = slice(LANES * p, LANES * (p + 1))
                xs = xbc_ref[0, :, sl]
                dt = dtfull_scr[:, sl]
                ac = acfull_scr[:, sl]
                acl = acfull_scr[CHUNK - 1:CHUNK, sl]
                dyp = dy_ref[0, :, sl]
                xdt = xs * dt
                xdt_bf = xdt.astype(BF16)
                e = jnp.exp(ac)
                dsd = jnp.exp(acl - ac)
                cd = jnp.exp(acl)
                xds_bf = (xdt * dsd).astype(BF16)
                hp = hall_ref[0, 0, p]
                hp_bf = hp.astype(BF16)
                dhn = dh_scr[p]
                dhn_bf = dhn.astype(BF16)
                yo = _bdot(cg_bf, hp_bf) * e
                dw_bf = (dyp * e).astype(BF16)
                dcg = dcg + _bdot_nt(dw_bf, hp_bf)
                dhin = _bdot(cgt_bf, dw_bf)
                dac = dyp * yo
                dacl = jnp.sum(dhn * hp, axis=0, keepdims=True) * cd
                dxds = _bdot(bg_bf, dhn_bf)
                dbg = dbg + _bdot_nt(xds_bf, dhn_bf)
                dxdt = dxds * dsd
                tdec = dxds * xdt * dsd
                dacl = dacl + jnp.sum(tdec, axis=0, keepdims=True)
                dac = dac - tdec
                dh_scr[p] = dhn * cd + dhin
                for hh in range(2):
                    h = 2 * p + hh
                    lm = (lane < HEAD_DIM) if hh == 0 else (lane >= HEAD_DIM)
                    dec = _decay_matrix(acum, acr_scr, h)
                    mmat = gmat * dec
                    dyh_bf = jnp.where(lm, dyp, 0.0).astype(BF16)
                    dm = _bdot_nt(dyh_bf, xdt_bf)
                    dxdt = dxdt + _bdot(mmat.T, dyh_bf)
                    dg = dg + dm * dec
                    q = dm * mmat
                    rs16 = rs16 + jnp.where(lane == h, jnp.sum(q, axis=1, keepdims=True), 0.0)
                    csum_scr[h:h + 1, :] = jnp.sum(q, axis=0, keepdims=True)
                dx_ref[0, :, sl] = dfull_ref[:, sl] * dyp + dxdt * dt
                dacf_scr[:, sl] = dac + jnp.where(last_row, dacl, 0.0)
                ddtf_scr[:, sl] = dxdt * xs
                ddl_scr[:, sl] = jnp.broadcast_to(jnp.sum(dyp * xs, axis=0, keepdims=True), (SUBLANES, LANES))
            dg_bf = dg.astype(BF16)
            dx_ref[0, :, bsl] = dbg + _bdot(dg.T, cg_bf)
            dx_ref[0, :, csl] = dcg + _bdot(dg_bf, bg_bf)
        reduce, triu = reduce_ref[...], triu_ref[...]
        dac16 = _dot_sel(dacf_scr[...], reduce) + rs16 - csum_scr[...].T
        da16 = _sel_dot(triu, dac16)
        ddt16 = da16 * arow_ref[...] + _dot_sel(ddtf_scr[...], reduce)
        ddtraw = ddt16 * _sigmoid(dtf_ref[0] + bias_ref[...])
        ddt_ref[0] = ddtraw
        acc_ref[0:8, :] += jnp.broadcast_to(jnp.sum(da16 * dtc * arow_ref[...], axis=0, keepdims=True), (SUBLANES, LANES))
        acc_ref[8:16, :] += _dot_sel(ddl_scr[...], reduce)
        acc_ref[16:24, :] += jnp.broadcast_to(jnp.sum(ddtraw, axis=0, keepdims=True), (SUBLANES, LANES))

    rev = lambda b, c: (b, nc - 1 - c, 0)
    row = lambda w: pl.BlockSpec((1, w), lambda b, c: (0, 0))
    dx, ddt, acc = pl.pallas_call(
        body, name="ssd_bwd", grid=(bsz, nc),
        in_specs=[pl.BlockSpec((1, CHUNK, D_MODEL), rev), pl.BlockSpec((1, CHUNK, CONV_CH), rev),
                  pl.BlockSpec((1, CHUNK, LANES), rev),
                  pl.BlockSpec((1, LANES, CHUNK), lambda b, c: (b, 0, nc - 1 - c)),
                  row(LANES), pl.BlockSpec((LANES, 1), lambda b, c: (0, 0)),
                  row(LANES), pl.BlockSpec((LANES, 1), lambda b, c: (0, 0)), row(D_MODEL),
                  pl.BlockSpec((1, 1, HEAD_PAIRS, SSD_STATE, LANES), lambda b, c: (b, nc - 1 - c, 0, 0, 0))]
        + [pl.BlockSpec(a.shape, lambda b, c: (0, 0)) for a in consts],
        out_specs=[pl.BlockSpec((1, CHUNK, CONV_CH), rev), pl.BlockSpec((1, CHUNK, LANES), rev),
                   pl.BlockSpec((3 * SUBLANES, LANES), lambda b, c: (0, 0))],
        out_shape=[jax.ShapeDtypeStruct((bsz, seq, CONV_CH), F32), jax.ShapeDtypeStruct((bsz, seq, LANES), F32),
                   jax.ShapeDtypeStruct((3 * SUBLANES, LANES), F32)],
        scratch_shapes=[pltpu.VMEM((HEAD_PAIRS, SSD_STATE, LANES), F32), pltpu.VMEM((CHUNK, D_MODEL), F32),
                        pltpu.VMEM((CHUNK, D_MODEL), F32), pltpu.VMEM((LANES, CHUNK), F32),
                        pltpu.VMEM((LANES, CHUNK), F32), pltpu.VMEM((CHUNK, D_MODEL), F32),
                        pltpu.VMEM((CHUNK, D_MODEL), F32), pltpu.VMEM((SUBLANES, D_MODEL), F32)],
        compiler_params=_params(("arbitrary", "arbitrary")),
    )(dy3, x3, dtf.reshape(bsz, seq, LANES), dtft, bias, biast, arow, acol, dfull, hall, *consts)
    return dx.reshape(bsz * seq, CONV_CH), ddt.reshape(bsz * seq, LANES), acc


GROUP_W = D_MODEL // 2


def _gnorm_fwd(y, z, o_att, w):
    n = y.shape[0]
    tm = min(1024, n)

    def body(y_ref, z_ref, o_ref, w_ref, out_ref):
        zv = z_ref[...]
        g = y_ref[...] * (zv * _sigmoid(zv))
        for gi in range(2):
            sl = slice(GROUP_W * gi, GROUP_W * (gi + 1))
            gs = g[:, sl]
            r = lax.rsqrt(jnp.mean(gs * gs, axis=-1, keepdims=True) + NORM_EPS)
            out_ref[:, sl] = (gs * r * w_ref[:, sl]).astype(BF16)
        out_ref[:, D_MODEL:] = o_ref[...].astype(BF16)

    tok = pl.BlockSpec((tm, D_MODEL), lambda i: (i, 0))
    return pl.pallas_call(
        body, name="gnorm_fwd", grid=(n // tm,),
        in_specs=[tok, tok, tok, pl.BlockSpec((1, D_MODEL), lambda i: (0, 0))],
        out_specs=pl.BlockSpec((tm, MIX_WIDTH), lambda i: (i, 0)),
        out_shape=jax.ShapeDtypeStruct((n, MIX_WIDTH), BF16),
        compiler_params=_params(("parallel",)),
    )(y, z, o_att, w)


def _gnorm_bwd(dycat, y, z, w):
    n = y.shape[0]
    tm = min(512, n)

    def body(dn_ref, y_ref, z_ref, w_ref, dy_ref, dz_ref, gw_ref):
        zv, yv = z_ref[...], y_ref[...]
        sz = _sigmoid(zv)
        sil = zv * sz
        g = yv * sil

        @pl.when(pl.program_id(0) == 0)
        def _():
            gw_ref[...] = jnp.zeros_like(gw_ref)

        for gi in range(2):
            sl = slice(GROUP_W * gi, GROUP_W * (gi + 1))
            gs = g[:, sl]
            r = lax.rsqrt(jnp.mean(gs * gs, axis=-1, keepdims=True) + NORM_EPS)
            nrm = gs * r
            dyn = dn_ref[:, sl]
            dn = dyn * w_ref[:, sl]
            dgs = r * (dn - nrm * jnp.mean(dn * nrm, axis=-1, keepdims=True))
            dy_ref[:, sl] = dgs * sil[:, sl]
            dz_ref[:, sl] = (dgs * yv[:, sl] * (sz[:, sl] * (1.0 + zv[:, sl] * (1.0 - sz[:, sl])))).astype(BF16)
            gw_ref[:, sl] += jnp.sum(dyn * nrm, axis=0, keepdims=True)

    tok = pl.BlockSpec((tm, D_MODEL), lambda i: (i, 0))
    row = pl.BlockSpec((1, D_MODEL), lambda i: (0, 0))
    return pl.pallas_call(
        body, name="gnorm_bwd", grid=(n // tm,),
        in_specs=[tok, tok, tok, row], out_specs=[tok, tok, row],
        out_shape=[jax.ShapeDtypeStruct((n, D_MODEL), F32), jax.ShapeDtypeStruct((n, D_MODEL), BF16),
                   jax.ShapeDtypeStruct((1, D_MODEL), F32)],
        compiler_params=_params(("arbitrary",)),
    )(dycat, y, z, w)


AUX_C = 3
AUX_ONE = 3


def _aux_base(hh):
    return HEAD_DIM * (1 - hh)


def _fox_prep(dtf, fb, bsz, seq):
    def body(x_ref, b_ref, aux_ref):
        tri = (_iota((CHUNK, CHUNK), 1) <= _iota((CHUNK, CHUNK), 0)).astype(F32)
        row, col = _iota((LANES, D_MODEL), 0), _iota((LANES, D_MODEL), 1)
        lane = jnp.bitwise_and(col, LANES - 1)
        other = (lane < HEAD_DIM).astype(jnp.int32)
        term = lane - HEAD_DIM * (1 - other)
        src = F_COL + 2 * jnp.right_shift(col, 7) + other
        place = [jnp.where((row == src) & (term == i), -1.0, 0.0).astype(BF16) for i in range(AUX_C)]
        lane1 = jnp.bitwise_and(_iota((1, D_MODEL), 1), LANES - 1)
        ones_lane = (lane1 - HEAD_DIM * (1 - (lane1 < HEAD_DIM).astype(jnp.int32)) == AUX_ONE).astype(F32)
        carry = jnp.zeros((1, LANES), F32)
        for j in range(seq // CHUNK):
            sl = slice(CHUNK * j, CHUNK * (j + 1))
            lf = _log_sigmoid(x_ref[sl, :] + b_ref[...])
            c = _sel_dot(tri, lf) + carry
            carry = carry + jnp.sum(lf, axis=0, keepdims=True)
            t1, t2, t3 = (jnp.dot(t, p, preferred_element_type=F32) for t, p in zip(_split3(c), place))
            aux_ref[sl, :] = (t1 + t2 + t3 + ones_lane).astype(BF16)

    return pl.pallas_call(
        body, name="fox_prep", grid=(bsz,),
        in_specs=[pl.BlockSpec((seq, LANES), lambda b: (b, 0)), pl.BlockSpec((1, LANES), lambda b: (0, 0))],
        out_specs=pl.BlockSpec((seq, D_MODEL), lambda b: (b, 0)),
        out_shape=jax.ShapeDtypeStruct((bsz * seq, D_MODEL), BF16),
        compiler_params=_params(("parallel",)),
    )(dtf, fb)


def _fox_prep_bwd(dck, dcq, ddt, dtf, fb, bsz, seq):
    nj = seq // CHUNK

    def body(dck_ref, dcq_ref, ddt_ref, x_ref, b_ref, out_ref, gb_ref):
        triu = (_iota((CHUNK, CHUNK), 0) <= _iota((CHUNK, CHUNK), 1)).astype(F32)
        is_f = (_iota((CHUNK, LANES), 1) >= F_COL) & (_iota((CHUNK, LANES), 1) < 2 * F_COL)
        carry = jnp.zeros((1, LANES), F32)
        total = jnp.zeros((1, LANES), F32)
        for j in range(nj - 1, -1, -1):
            sl = slice(CHUNK * j, CHUNK * (j + 1))
            dcv = dck_ref[sl, :] + dcq_ref[sl, :]
            dlf = _sel_dot(triu, dcv) + carry
            carry = carry + jnp.sum(dcv, axis=0, keepdims=True)
            df = jnp.where(is_f, dlf * _sigmoid(-(x_ref[sl, :] + b_ref[...])), 0.0)
            out_ref[sl, :] = (df + ddt_ref[sl, :]).astype(BF16)
            total = total + jnp.sum(df, axis=0, keepdims=True)

        @pl.when(pl.program_id(0) == 0)
        def _():
            gb_ref[...] = jnp.zeros_like(gb_ref)

        gb_ref[...] += jnp.broadcast_to(total, (SUBLANES, LANES))

    blk = pl.BlockSpec((seq, LANES), lambda b: (b, 0))
    return pl.pallas_call(
        body, name="fox_prep_bwd", grid=(bsz,),
        in_specs=[blk, blk, blk, blk, pl.BlockSpec((1, LANES), lambda b: (0, 0))],
        out_specs=[blk, pl.BlockSpec((SUBLANES, LANES), lambda b: (0, 0))],
        out_shape=[jax.ShapeDtypeStruct((bsz * seq, LANES), BF16), jax.ShapeDtypeStruct((SUBLANES, LANES), F32)],
        compiler_params=_params(("arbitrary",)),
    )(dck, dcq, ddt, dtf, fb)


ATT_BLOCK_FWD = 2048
ATT_BLOCK_KEYS = 256
ATT_BLOCK_BWD = 256


def _att_operands(q_ref, k_ref, aux_ref, hh):
    lane = _iota((1, LANES), 1)
    lm = (lane < HEAD_DIM) if hh == 0 else (lane >= HEAD_DIM)
    base = _aux_base(hh)
    zero = jnp.zeros((1, LANES), BF16)
    ka = jnp.where(lm, k_ref[...], jnp.where((lane >= base) & (lane <= base + AUX_ONE), aux_ref[...], zero))
    qa = jnp.where(lm, q_ref[...] * ATT_SCALE,
                   jnp.where((lane >= base) & (lane < base + AUX_C), jnp.ones((1, LANES), BF16), zero))
    return lm, base, qa, ka


def _att_fwd(qkv, ccol, bsz, seq, gather):
    bq, bk = min(ATT_BLOCK_FWD, seq), min(ATT_BLOCK_KEYS, seq)
    nq, ratio = seq // bq, bq // bk
    nx = len(gather)

    def body(*refs):
        q_ref, k_ref, v_ref, c_ref = refs[:4]
        x_refs = refs[4:4 + nx]
        o_ref, lse_ref = refs[4 + nx:6 + nx]
        xo_refs = refs[6 + nx:6 + 2 * nx]
        qa_scr, ka_scr, vt_scr = refs[6 + 2 * nx:9 + 2 * nx]
        sems = refs[9 + 2 * nx:]
        pair = pl.program_id(1)

        @pl.when((pl.program_id(0) == 0) & (pair == 0))
        def _():
            _exchange_start("gather", x_refs, xo_refs, *sems)

        lse_ref[...] = jnp.zeros_like(lse_ref)
        for hh in range(2):
            _, _, qa, ka = _att_operands(q_ref, k_ref, c_ref, hh)
            qa_scr[hh] = qa
            ka_scr[hh] = ka
        for t0 in range(0, seq, bq):
            vt_scr[:, t0:t0 + bq] = v_ref[t0:t0 + bq, :].T
        key_minus_query = _iota((bk, bq), 0) - _iota((bk, bq), 1)

        def q_step(i, carry0):
            qrows = pl.ds(pl.multiple_of(i * bq, bq), bq)

            def scores(j):
                krows = pl.ds(pl.multiple_of(j * bk, bk), bk)
                return tuple(_bdot_nt(ka_scr[hh, krows, :], qa_scr[hh, qrows, :]) for hh in range(2))

            def scores_tail(r):
                krows = pl.ds(pl.multiple_of((i * ratio + r) * bk, bk), bk)
                qsub = pl.ds(pl.multiple_of(i * bq + r * bk, bk), bq - r * bk)
                return tuple(_bdot_nt(ka_scr[hh, krows, :], qa_scr[hh, qsub, :]) for hh in range(2))

            def update(state, st_pair, j, masked):
                krows = pl.ds(pl.multiple_of(j * bk, bk), bk)
                out = []
                for hh in range(2):
                    m_all, l_all, acc_all = state[hh]
                    st = st_pair[hh]
                    width = st.shape[1]
                    lo = bq - width
                    m, l, acc = m_all[:, lo:], l_all[:, lo:], acc_all[:, lo:]
                    if masked:
                        st = jnp.where(key_minus_query[:, :width] <= 0, st, NEG)
                    mn = jnp.maximum(m, jnp.max(st, axis=0, keepdims=True))
                    alpha = jnp.exp(m - mn)
                    pt = jnp.exp(st - mn)
                    l = alpha * l + jnp.sum(pt, axis=0, keepdims=True)
                    vt = vt_scr[HEAD_DIM * hh:HEAD_DIM * (hh + 1), krows]
                    acc = alpha * acc + _bdot(vt, pt)
                    if lo:
                        mn, l, acc = (jnp.concatenate([old[:, :lo], new], axis=1)
                                      for old, new in ((m_all, mn), (l_all, l), (acc_all, acc)))
                    out.append((mn, l, acc))
                return tuple(out)

            def step(j, carry):
                state, s_cur = carry
                s_next = scores(j + 1)
                return update(state, s_cur, j, False), s_next

            def step_pair(t, carry):
                state, s_cur = carry
                s_second = scores(2 * t + 1)
                s_next = scores(2 * t + 2)
                state = update(state, s_cur, 2 * t, False)
                return update(state, s_second, 2 * t + 1, False), s_next

            one = (jnp.full((1, bq), NEG, F32), jnp.zeros((1, bq), F32), jnp.zeros((HEAD_DIM, bq), F32))
            first_diag = i * ratio
            if ratio % 2 == 0:
                state, s_cur = lax.fori_loop(0, first_diag // 2, step_pair, ((one, one), scores(0)))
            else:
                state, s_cur = lax.fori_loop(0, first_diag, step, ((one, one), scores(0)))
            for r in range(ratio):
                s_next = scores_tail(r + 1) if r + 1 < ratio else None
                state = update(state, s_cur, first_diag + r, True)
                s_cur = s_next
            (m0, l0, acc0), (m1, l1, acc1) = state
            ot = jnp.concatenate([acc0 * (1.0 / l0), acc1 * (1.0 / l1)], axis=0)
            o_ref[qrows, :] = ot.T
            lse_ref[0, 0, 0:1, qrows] = m0 + jnp.log(l0)
            lse_ref[0, 0, 1:2, qrows] = m1 + jnp.log(l1)
            return carry0

        lax.fori_loop(0, nq, q_step, 0)

        @pl.when((pl.program_id(0) == bsz - 1) & (pair == HEAD_PAIRS - 1))
        def _():
            _exchange_wait("gather", x_refs, xo_refs, *sems)

    col = lambda off: pl.BlockSpec((seq, LANES), lambda b, p: (b, off + p))
    head2 = pltpu.VMEM((2, seq, LANES), BF16)
    hbm = pl.BlockSpec(memory_space=pl.ANY)
    return pl.pallas_call(
        body, name="att_fwd", grid=(bsz, HEAD_PAIRS),
        in_specs=[col(0), col(HEAD_PAIRS), col(2 * HEAD_PAIRS), col(0)] + [hbm] * nx,
        out_specs=[pl.BlockSpec((seq, LANES), lambda b, p: (b, p)),
                   pl.BlockSpec((1, 1, SUBLANES, seq), lambda b, p: (b, p, 0, 0))] + [hbm] * nx,
        out_shape=[jax.ShapeDtypeStruct((bsz * seq, D_MODEL), F32),
                   jax.ShapeDtypeStruct((bsz, HEAD_PAIRS, SUBLANES, seq), F32)] + _exchange_shapes("gather", gather),
        scratch_shapes=[head2, head2, pltpu.VMEM((LANES, seq), BF16)] + _exchange_scratch(nx),
        compiler_params=_params(("arbitrary", "arbitrary")),
    )(qkv, qkv, qkv, ccol, *gather)


def _att_bwd(qkv, dycat, o, lse, ccol, bsz, seq, scatter):
    blk = bq = min(ATT_BLOCK_BWD, seq)
    nb = nq = seq // blk
    nx = len(scatter)

    def body(*refs):
        q_ref, k_ref, v_ref, do_ref, o_ref, lse_ref, c_ref = refs[:7]
        x_refs = refs[7:7 + nx]
        dq_ref, dk_ref, dv_ref, dck_ref, dcq_ref = refs[7 + nx:12 + nx]
        xo_refs = refs[12 + nx:12 + 2 * nx]
        (qa_scr, ka_scr, va_scr, doa_scr, kat_scr, qat_scr, dot_scr, d_scr, dqt_scr, dkt_scr,
         dvt_scr) = refs[12 + 2 * nx:23 + 2 * nx]
        sems = refs[23 + 2 * nx:]
        pair = pl.program_id(1)

        @pl.when((pl.program_id(0) == 0) & (pair == 0))
        def _():
            _exchange_start("scatter", x_refs, xo_refs, *sems)

        query_minus_key = _iota((blk, bq), 1) - _iota((blk, bq), 0)
        lane_b = _iota((blk, LANES), 1)
        row_t = _iota((LANES, seq), 0)
        bases = []
        ones8 = jnp.ones((SUBLANES, LANES), F32)
        for hh in range(2):
            lm, base, qa, ka = _att_operands(q_ref, k_ref, c_ref, hh)
            bases.append(base)
            qa_scr[hh] = qa
            ka_scr[hh] = ka
            va_scr[hh] = jnp.where(lm, v_ref[...], jnp.zeros((seq, LANES), BF16))
            doa = jnp.where(lm, do_ref[...], 0.0).astype(BF16)
            doa_scr[hh] = doa
            for t0 in range(0, seq, blk):
                kat_scr[hh, :, t0:t0 + blk] = ka[t0:t0 + blk, :].T
                qat_scr[hh, :, t0:t0 + blk] = qa[t0:t0 + blk, :].T
            d1, d2, d3 = (_bdot_nt(ones8, term) for term in _split3(doa.astype(F32) * o_ref[...]))
            d_scr[hh] = d1 + d2 + d3
        for t0 in range(0, seq, blk):
            dot_scr[:, t0:t0 + blk] = do_ref[t0:t0 + blk, :].astype(BF16).T
        dqt_scr[...] = jnp.zeros_like(dqt_scr)
        dck_ref[...] = jnp.zeros_like(dck_ref)

        def kv_step(j, carry0):
            krows = pl.ds(pl.multiple_of(j * blk, blk), blk)
            dkt_scr[...] = jnp.zeros_like(dkt_scr)
            dvt_scr[...] = jnp.zeros_like(dvt_scr)

            def scores(i):
                rows = pl.ds(pl.multiple_of(i * bq, bq), bq)
                return tuple((_bdot_nt(ka_scr[hh, krows, :], qa_scr[hh, rows, :]),
                              _bdot_nt(va_scr[hh, krows, :], doa_scr[hh, rows, :])) for hh in range(2))

            def update(i, sd, masked):
                rows = pl.ds(pl.multiple_of(i * bq, bq), bq)
                for hh in range(2):
                    st, dpt = sd[hh]
                    if masked:
                        st = jnp.where(query_minus_key >= 0, st, NEG)
                    pt = jnp.exp(st - lse_ref[0, 0, hh:hh + 1, rows])
                    dst = (pt * (dpt - d_scr[hh, 0:1, rows])).astype(BF16)
                    dvt_scr[hh] += _bdot_nt(dot_scr[HEAD_DIM * hh:HEAD_DIM * (hh + 1), rows], pt)
                    dkt_scr[hh] += _bdot_nt(qat_scr[hh, :, rows], dst)
                    dqt_scr[hh, :, rows] += _bdot(kat_scr[hh, :, krows], dst)

            i_diag = j

            def q_pair(t, sd_cur):
                i = i_diag + 1 + 2 * t
                sd_second = scores(i + 1)
                sd_next = scores(jnp.minimum(i + 2, nq - 1))
                update(i, sd_cur, False)
                update(i + 1, sd_second, False)
                return sd_next

            sd_diag = scores(i_diag)
            sd_first = scores(jnp.minimum(i_diag + 1, nq - 1))
            update(i_diag, sd_diag, True)
            rest = nq - 1 - i_diag
            sd_last = lax.fori_loop(0, rest // 2, q_pair, sd_first)

            @pl.when(rest % 2 == 1)
            def _():
                update(jnp.int32(nq - 1), sd_last, False)
            dkt = jnp.where(_iota((LANES, blk), 0) < HEAD_DIM, dkt_scr[0], dkt_scr[1])
            dk_ref[krows, :] = dkt.T.astype(BF16)
            dv_ref[krows, :] = jnp.concatenate([dvt_scr[0], dvt_scr[1]], axis=0).T.astype(BF16)
            for hh in range(2):
                dck_ref[0, 0, hh:hh + 1, krows] = -dkt_scr[hh, bases[hh]:bases[hh] + 1, :]
            return carry0

        lax.fori_loop(0, nb, kv_step, 0)
        for t0 in range(0, seq, blk):
            dq0, dq1 = dqt_scr[0, :, t0:t0 + blk].T, dqt_scr[1, :, t0:t0 + blk].T
            dq_ref[t0:t0 + blk, :] = (jnp.where(lane_b < HEAD_DIM, dq0, dq1) * ATT_SCALE).astype(BF16)
        dcq_ref[...] = jnp.zeros_like(dcq_ref)
        for hh in range(2):
            dcq_ref[0, 0, hh:hh + 1, :] = jnp.sum(jnp.where(row_t == bases[hh] + AUX_ONE, dqt_scr[hh], 0.0),
                                                   axis=0, keepdims=True)

        @pl.when((pl.program_id(0) == bsz - 1) & (pair == HEAD_PAIRS - 1))
        def _():
            _exchange_wait("scatter", x_refs, xo_refs, *sems)

    col = lambda off: pl.BlockSpec((seq, LANES), lambda b, p: (b, off + p))
    rowvec = pl.BlockSpec((1, 1, SUBLANES, seq), lambda b, p: (b, p, 0, 0))
    out = jax.ShapeDtypeStruct((bsz * seq, D_MODEL), BF16)
    rows_shape = jax.ShapeDtypeStruct((bsz, HEAD_PAIRS, SUBLANES, seq), F32)
    head2 = pltpu.VMEM((2, seq, LANES), BF16)
    hbm = pl.BlockSpec(memory_space=pl.ANY)
    return pl.pallas_call(
        body, name="att_bwd", grid=(bsz, HEAD_PAIRS),
        in_specs=[col(0), col(HEAD_PAIRS), col(2 * HEAD_PAIRS), col(HEAD_PAIRS), col(0), rowvec, col(0)] + [hbm] * nx,
        out_specs=[col(0), col(0), col(0), rowvec, rowvec] + [hbm] * nx,
        out_shape=[out, out, out, rows_shape, rows_shape] + _exchange_shapes("scatter", scatter),
        scratch_shapes=[head2, head2, head2, head2, pltpu.VMEM((2, LANES, seq), BF16),
                        pltpu.VMEM((2, LANES, seq), BF16), pltpu.VMEM((LANES, seq), BF16),
                        pltpu.VMEM((2, SUBLANES, seq), F32), pltpu.VMEM((2, LANES, seq), F32),
                        pltpu.VMEM((2, LANES, blk), F32), pltpu.VMEM((2, HEAD_DIM, blk), F32)] + _exchange_scratch(nx),
        compiler_params=_params(("arbitrary", "arbitrary")),
    )(qkv, qkv, qkv, dycat, o, lse, ccol, *scatter)


def _adamw_math(w, g, m, v):
    m = ADAM_B1 * m + (1.0 - ADAM_B1) * g
    v = ADAM_B2 * v + (1.0 - ADAM_B2) * jnp.square(g)
    m_hat = m / ADAM_C1
    v_hat = v / ADAM_C2
    delta = -ADAM_LR * (m_hat / (jnp.sqrt(v_hat) + ADAM_EPS) + ADAM_WD * w)
    return delta, m, v


def _row_tile(rows):
    for tr in (256, 128, 64, 32, 16, 8):
        if rows % tr == 0:
            return tr
    return rows


def _sum_parts(parts, name):
    nparts, rows, cols = parts.shape
    tr = rows if rows % SUBLANES else _row_tile(rows)

    def body(p_ref, o_ref):
        g = p_ref[0].astype(F32)
        for s in range(1, nparts):
            g = g + p_ref[s].astype(F32)
        o_ref[...] = g

    return pl.pallas_call(
        body, name=name, grid=(rows // tr,),
        in_specs=[pl.BlockSpec((nparts, tr, cols), lambda i: (0, i, 0))],
        out_specs=pl.BlockSpec((tr, cols), lambda i: (i, 0)),
        out_shape=jax.ShapeDtypeStruct((rows, cols), F32),
        compiler_params=_params(("parallel",)),
    )(parts)


def _adamw_parts(parts, w, m, v, name):
    nparts, rows, cols = parts.shape
    tr = _row_tile(rows)

    def body(p_ref, w_ref, m_ref, v_ref, g_ref, d_ref, mo_ref, vo_ref):
        g = p_ref[0].astype(F32)
        for s in range(1, nparts):
            g = g + p_ref[s].astype(F32)
        delta, mn, vn = _adamw_math(w_ref[...], g, m_ref[...], v_ref[...])
        g_ref[...] = g
        d_ref[...] = delta
        mo_ref[...] = mn
        vo_ref[...] = vn

    blk = pl.BlockSpec((tr, cols), lambda i: (i, 0))
    shp = jax.ShapeDtypeStruct((rows, cols), F32)
    return pl.pallas_call(
        body, name=name, grid=(rows // tr,),
        in_specs=[pl.BlockSpec((nparts, tr, cols), lambda i: (0, i, 0)), blk, blk, blk],
        out_specs=[blk, blk, blk, blk], out_shape=[shp, shp, shp, shp],
        compiler_params=_params(("parallel",)),
    )(parts, w, m, v)


def _me():
    return lax.axis_index("x"), lax.axis_index("y"), lax.axis_index("c")


def _flip(pos, k):
    x, y, c = pos
    kx, ky, kc = (k >> 2) & 1, (k >> 1) & 1, k & 1
    return (x ^ kx if kx else x, y ^ ky if ky else y, c ^ kc if kc else c)


def _logical(pos):
    return 4 * pos[0] + 2 * pos[1] + pos[2]


def _all_gather_two_level(shards):
    narr = len(shards)

    def body(*refs):
        x_refs, out_refs = refs[:narr], refs[narr:2 * narr]
        send_sems, recv_sems, local_sems = refs[2 * narr:]
        x, y, c = _me()
        me, sibling = (x, y, c), (x, y, 1 - c)
        chips = [(1 - x, y), (x, 1 - y), (1 - x, 1 - y)]

        def copy(a, k, block, to, src=None):
            slot = out_refs[a].at[_logical(block)]
            return pltpu.make_async_remote_copy(
                src_ref=slot if src is None else src, dst_ref=slot,
                send_sem=send_sems.at[a, k], recv_sem=recv_sems.at[a, k], device_id=to, device_id_type=MESH)

        mine = [pltpu.make_async_copy(x_refs[a], out_refs[a].at[_logical(me)], local_sems.at[a]) for a in range(narr)]
        for cp in mine:
            cp.start()
        first = []
        for a in range(narr):
            first.append(copy(a, 0, me, sibling, src=x_refs[a]))
            first += [copy(a, 1 + j, me, (*chip, c), src=x_refs[a]) for j, chip in enumerate(chips)]
        for cp in first:
            cp.start()
        passed = []
        for a in range(narr):
            for j, chip in enumerate(chips):
                copy(a, 1 + j, (*chip, c), me).wait_recv()
                fwd = copy(a, 4 + j, (*chip, c), sibling)
                fwd.start()
                passed.append(fwd)
        for a in range(narr):
            copy(a, 0, sibling, me).wait_recv()
            for j, chip in enumerate(chips):
                copy(a, 4 + j, (*chip, 1 - c), me).wait_recv()
        for cp in first + passed:
            cp.wait_send()
        for cp in mine:
            cp.wait()

    hbm = pl.BlockSpec(memory_space=pl.ANY)
    return pl.pallas_call(
        body, name="all_gather_big",
        out_shape=[jax.ShapeDtypeStruct((N_DEV,) + s.shape, s.dtype) for s in shards],
        in_specs=[hbm] * narr, out_specs=[hbm] * narr,
        scratch_shapes=[pltpu.SemaphoreType.DMA((narr, 7)), pltpu.SemaphoreType.DMA((narr, 7)),
                        pltpu.SemaphoreType.DMA((narr,))],
    )(*shards)


def _exchange_copies(kind, x_refs, out_refs, send_sems, recv_sems, local_sems):
    me = _me()
    mine = _logical(me)
    local, remote = [], []
    for a, (x_ref, out_ref) in enumerate(zip(x_refs, out_refs)):
        own = x_ref if kind == "gather" else x_ref.at[mine]
        local.append(pltpu.make_async_copy(own, out_ref.at[mine], local_sems.at[a]))
        for k in range(1, N_DEV):
            peer = _flip(me, k)
            src = x_ref if kind == "gather" else x_ref.at[_logical(peer)]
            remote.append(pltpu.make_async_remote_copy(
                src_ref=src, dst_ref=out_ref.at[mine], send_sem=send_sems.at[a, k - 1], recv_sem=recv_sems.at[a, k - 1],
                device_id=peer, device_id_type=MESH))
    return local, remote


def _exchange_start(kind, x_refs, out_refs, *sems):
    if not x_refs:
        return
    local, remote = _exchange_copies(kind, x_refs, out_refs, *sems)
    for cp in local + remote:
        cp.start()


def _exchange_wait(kind, x_refs, out_refs, *sems):
    if not x_refs:
        return
    local, remote = _exchange_copies(kind, x_refs, out_refs, *sems)
    for cp in remote:
        cp.wait_recv()
    for cp in remote:
        cp.wait_send()
    for cp in local:
        cp.wait()


def _exchange_shapes(kind, arrays):
    return [jax.ShapeDtypeStruct(((N_DEV,) if kind == "gather" else ()) + a.shape, a.dtype) for a in arrays]


def _exchange_scratch(narrays):
    if not narrays:
        return []
    return [pltpu.SemaphoreType.DMA((narrays, N_DEV - 1)), pltpu.SemaphoreType.DMA((narrays, N_DEV - 1)),
            pltpu.SemaphoreType.DMA((narrays,))]


def _exchange_rows(x_ref, buf_ref, send_sems, recv_sems):
    me = _me()
    buf_ref[_logical(me)] = x_ref[...]
    copies = []
    for k in range(1, N_DEV):
        peer = _flip(me, k)
        copies.append(pltpu.make_async_remote_copy(
            src_ref=x_ref, dst_ref=buf_ref.at[_logical(me)],
            send_sem=send_sems.at[k - 1], recv_sem=recv_sems.at[k - 1], device_id=peer, device_id_type=MESH))
    for cp in copies:
        cp.start()
    for cp in copies:
        cp.wait_recv()
    for cp in copies:
        cp.wait_send()


def _sum_slots(buf_ref):
    total = buf_ref[0]
    for s in range(1, N_DEV):
        total = total + buf_ref[s]
    return total


def _small_gather(x):
    def body(x_ref, o_ref, buf_ref, send_sems, recv_sems):
        _exchange_rows(x_ref, buf_ref, send_sems, recv_sems)
        o_ref[...] = _sum_slots(buf_ref)

    return pl.pallas_call(
        body, name="small_gather", out_shape=jax.ShapeDtypeStruct(x.shape, F32),
        in_specs=[pl.BlockSpec(memory_space=pltpu.VMEM)], out_specs=pl.BlockSpec(memory_space=pltpu.VMEM),
        scratch_shapes=[pltpu.VMEM((N_DEV,) + x.shape, F32), pltpu.SemaphoreType.DMA((7,)), pltpu.SemaphoreType.DMA((7,))],
    )(x)


def _small_reduce_adamw(g, w, m, v):
    def body(g_ref, w_ref, m_ref, v_ref, go_ref, d_ref, mo_ref, vo_ref, buf_ref, send_sems, recv_sems):
        _exchange_rows(g_ref, buf_ref, send_sems, recv_sems)
        gs = _sum_slots(buf_ref)
        delta, mn, vn = _adamw_math(w_ref[...], gs, m_ref[...], v_ref[...])
        go_ref[...] = gs
        d_ref[...] = delta
        mo_ref[...] = mn
        vo_ref[...] = vn

    vm = pl.BlockSpec(memory_space=pltpu.VMEM)
    shp = jax.ShapeDtypeStruct(g.shape, F32)
    return pl.pallas_call(
        body, name="small_reduce_adamw", out_shape=[shp, shp, shp, shp],
        in_specs=[vm, vm, vm, vm], out_specs=[vm, vm, vm, vm],
        scratch_shapes=[pltpu.VMEM((N_DEV,) + g.shape, F32), pltpu.SemaphoreType.DMA((7,)), pltpu.SemaphoreType.DMA((7,))],
    )(g, w, m, v)


def _pad_cols(a, width):
    return jnp.pad(a, ((0, 0), (0, width - a.shape[1])))


def _local_step(x, target, norm_mix_w, w_in_t, conv_w, conv_b, dt_bias, a_log, d_skip, ssd_norm_w, f_bias,
                late_shards, norm_mlp_w, norm_final_w):
    bsz, seq, _ = x.shape
    n = bsz * seq
    x2 = x.reshape(n, D_MODEL)
    t2 = target.reshape(n, D_MODEL)
    nfw = norm_final_w.reshape(1, D_MODEL)

    bias = _pad_cols(dt_bias, LANES)
    arow = _pad_cols(-jnp.exp(a_log), LANES)
    biast, acol = bias.reshape(LANES, 1), arow.reshape(LANES, 1)
    dfull = jnp.repeat(d_skip, HEAD_DIM, axis=1)
    fb = jnp.pad(f_bias, ((0, 0), (F_COL, LANES - 2 * F_COL)))

    wt_z, wt_xbc, wt_qkv = w_in_t[:ROW_XBC], w_in_t[ROW_XBC:ROW_DT], w_in_t[ROW_Q:ROW_F]
    wt_dtf = jnp.concatenate([w_in_t[ROW_DT:ROW_Q], w_in_t[ROW_F:], jnp.zeros((LANES - 2 * F_COL, D_MODEL), BF16)], axis=0)
    wt_q, wt_k, wt_v = wt_qkv[:D_MODEL], wt_qkv[D_MODEL:2 * D_MODEL], wt_qkv[2 * D_MODEL:]

    h1 = _rms_fwd(x2, norm_mix_w, "rms_fwd_mix")
    z = _mm([(h1, wt_z)], "inproj_z", F32, 1024, 1024, nt=True)
    xbc_raw = _mm([(h1, wt_xbc)], "inproj_xbc", F32, 512, 1536, nt=True)
    qkv = _mm([(h1, wt_qkv)], "inproj_qkv", BF16, 512, 3072, nt=True)
    dtf = _mm([(h1, wt_dtf)], "inproj_dtf", F32, 2048, LANES, nt=True)
    dtft = dtf.reshape(bsz, seq, LANES).transpose(0, 2, 1)

    xbc = _conv_fwd(xbc_raw, conv_w, conv_b, bsz, seq)
    y_pre, hall = _ssd_fwd(xbc, dtf, dtft, bias, biast, arow, acol, dfull, bsz, seq)

    ccol = _fox_prep(dtf, fb, bsz, seq)
    o_att, lse, w_out, w_up_t, w_down = _att_fwd(qkv, ccol, bsz, seq, late_shards)
    w_out, w_up_t, w_down = (w.reshape(-1, D_MODEL) for w in (w_out, w_up_t, w_down))

    ycat = _gnorm_fwd(y_pre, z, o_att, ssd_norm_w)
    h2, h2n = _mm([(ycat, w_out)], "outproj", F32, 512, 1024, res=x2, rms_fwd=norm_mlp_w)
    pre = _mm([(h2n, w_up_t)], "mlp_up", BF16, 512, 4096, nt=True)

    dh3, loss_row, g_nfw = _mlp_down_loss(pre, w_down, h2, t2, nfw)
    dpre, u = _mlp_down_bwd(dh3, w_down, pre)
    g_w_down = _mm_tn(u, dh3, "grad_w_down", 1024, 1024, 2048)
    g_w_up_t = _mm_tn(dpre, h2n, "grad_w_up", 1024, 1024, 2048)
    by_shard = lambda g: g.reshape(N_DEV, g.shape[0] // N_DEV, D_MODEL)
    dh2, g_nmlp = _mm([(dpre, w_up_t)], "mlp_up_bwd", F32, 512, 1024, res=dh3, rms_bwd=(h2, norm_mlp_w))

    g_w_out = _mm_tn(ycat, dh2, "grad_w_out", 1024, 1024, 2048)
    dycat = _mm([(dh2, w_out)], "outproj_bwd", F32, 512, 2048, nt=True)
    dy_pre, dz, g_ssdn = _gnorm_bwd(dycat, y_pre, z, ssd_norm_w)
    dq, dk, dv, dck, dcq, recv_down, recv_up_t, recv_out = _att_bwd(
        qkv, dycat, o_att, lse, ccol, bsz, seq, [by_shard(g_w_down), by_shard(g_w_up_t), by_shard(g_w_out)])

    dxbc, ddt, ssd_acc = _ssd_bwd(dy_pre, xbc, dtf, dtft, bias, biast, arow, acol, dfull, hall, bsz, seq)
    dxbc_raw, g_cw, g_cb = _conv_bwd(dxbc, xbc_raw, conv_w, conv_b, bsz, seq)

    def head_cols(rows):
        cols = rows[:, :, :2, :].reshape(bsz, SSD_HEADS, seq).transpose(0, 2, 1).reshape(n, SSD_HEADS)
        return jnp.pad(cols, ((0, 0), (F_COL, LANES - 2 * F_COL)))

    ddtf, g_fb = _fox_prep_bwd(head_cols(dck), head_cols(dcq), ddt, dtf, fb, bsz, seq)

    g_dtf = _mm_tn(ddtf, h1, "grad_w_in_dtf", LANES, 1024, 2048)
    g_w_in_t = jnp.concatenate([
        _mm_tn(dz, h1, "grad_w_in_z", 1024, 1024, 2048), _mm_tn(dxbc_raw, h1, "grad_w_in_xbc", 768, 1024, 2048),
        g_dtf[:F_COL], _mm_tn(dq, h1, "grad_w_in_q", 1024, 1024, 2048), _mm_tn(dk, h1, "grad_w_in_k", 1024, 1024, 2048),
        _mm_tn(dv, h1, "grad_w_in_v", 1024, 1024, 2048), g_dtf[F_COL:2 * F_COL]], axis=0)
    pieces = [(dz, wt_z), (dxbc_raw, wt_xbc), (dq, wt_q), (dk, wt_k), (dv, wt_v), (ddtf, wt_dtf)]
    dx, g_nmix, recv_in_t = _mm(pieces, "inproj_bwd", F32, 256, 1024, res=dh2, rms_bwd=(x2, norm_mix_w),
                                exchange=("scatter", [by_shard(g_w_in_t)]))

    small = dict(
        norm_mix_w=g_nmix, norm_mlp_w=g_nmlp, norm_final_w=g_nfw, ssd_norm_w=g_ssdn, conv_b=g_cb, conv_w=g_cw,
        dt_bias=ssd_acc[16:17, :SSD_HEADS], a_log=ssd_acc[0:1, :SSD_HEADS], d_skip=ssd_acc[8:9, :SSD_HEADS],
        f_bias=g_fb[0:1, F_COL:2 * F_COL])
    recv = dict(w_in_t=recv_in_t, w_out=recv_out, w_up_t=recv_up_t, w_down=recv_down)
    return loss_row[0, 0], dx.reshape(bsz, seq, D_MODEL), small, recv


SMALL_LAYOUT = (("norm_mix_w", 0, 1, 0, D_MODEL), ("norm_mlp_w", 1, 1, 0, D_MODEL), ("norm_final_w", 2, 1, 0, D_MODEL),
                ("ssd_norm_w", 3, 1, 0, D_MODEL), ("conv_b", 4, 1, 0, CONV_CH), ("conv_w", 5, CONV_K, 0, CONV_CH),
                ("dt_bias", 9, 1, 0, SSD_HEADS), ("a_log", 9, 1, LANES, SSD_HEADS), ("d_skip", 9, 1, 2 * LANES, SSD_HEADS),
                ("f_bias", 9, 1, 3 * LANES, SSD_HEADS))
SMALL_ROWS = 2 * SUBLANES


def _pack_small(vals):
    packed = jnp.zeros((SMALL_ROWS, SMALL_COLS), F32)
    for name, r0, nrows, c0, width in SMALL_LAYOUT:
        packed = packed.at[r0:r0 + nrows, c0:c0 + width].set(vals[name].reshape(nrows, width))
    return packed


def _unpack_small(packed, like):
    out = {}
    for name, r0, nrows, c0, width in SMALL_LAYOUT:
        out[name] = packed[r0:r0 + nrows, c0:c0 + width].reshape(like[name].shape)
    return out


def kernel(x, norm_mix_w, w_in, conv_w, conv_b, dt_bias, a_log, d_skip, ssd_norm_w, f_bias, w_out, norm_mlp_w, w_up, w_down, norm_final_w, loss_target, m_norm_mix_w, m_w_in, m_conv_w, m_conv_b, m_dt_bias, m_a_log, m_d_skip, m_ssd_norm_w, m_f_bias, m_w_out, m_norm_mlp_w, m_w_up, m_w_down, m_norm_final_w, v_norm_mix_w, v_w_in, v_conv_w, v_conv_b, v_dt_bias, v_a_log, v_d_skip, v_ssd_norm_w, v_f_bias, v_w_out, v_norm_mlp_w, v_w_up, v_w_down, v_norm_final_w):
    me = 4 * lax.axis_index("x") + 2 * lax.axis_index("y") + lax.axis_index("c")
    conv_shard_w = CONV_CH // N_DEV
    zero = jnp.zeros((), jnp.int32)

    def place_conv(shard):
        return lax.dynamic_update_slice(jnp.zeros((CONV_K, CONV_CH), F32), shard[0], (zero, me * conv_shard_w))

    (g_in_t,) = _all_gather_two_level([w_in[0].T.astype(BF16)])
    late_shards = [w_out[0].astype(BF16), w_up[0].T.astype(BF16), w_down[0].astype(BF16)]
    conv_full = _small_gather(jnp.pad(place_conv(conv_w), ((0, SUBLANES - CONV_K), (0, 0))))[:CONV_K]

    loss_local, grad_x, g_small, recv = _local_step(
        x, loss_target, norm_mix_w, g_in_t.reshape(IN_WIDTH, D_MODEL), conv_full, conv_b, dt_bias, a_log, d_skip,
        ssd_norm_w, f_bias, late_shards, norm_mlp_w, norm_final_w)
    loss = lax.psum(loss_local, MESH_AXES)

    grad_in = _sum_parts(recv["w_in_t"], "sum_w_in").T[None]
    grad_up = _sum_parts(recv["w_up_t"], "sum_w_up").T[None]
    big = {
        "w_in": _adamw_parts(grad_in, w_in[0], m_w_in[0], v_w_in[0], "adamw_w_in"),
        "w_out": _adamw_parts(recv["w_out"], w_out[0], m_w_out[0], v_w_out[0], "adamw_w_out"),
        "w_up": _adamw_parts(grad_up, w_up[0], m_w_up[0], v_w_up[0], "adamw_w_up"),
        "w_down": _adamw_parts(recv["w_down"], w_down[0], m_w_down[0], v_w_down[0], "adamw_w_down"),
    }

    like = dict(norm_mix_w=norm_mix_w, norm_mlp_w=norm_mlp_w, norm_final_w=norm_final_w, ssd_norm_w=ssd_norm_w,
                conv_b=conv_b, conv_w=jnp.zeros((1, CONV_K, CONV_CH), F32), dt_bias=dt_bias, a_log=a_log,
                d_skip=d_skip, f_bias=f_bias)
    w_small = dict(like, conv_w=place_conv(conv_w))
    m_small = dict(norm_mix_w=m_norm_mix_w, norm_mlp_w=m_norm_mlp_w, norm_final_w=m_norm_final_w,
                   ssd_norm_w=m_ssd_norm_w, conv_b=m_conv_b, conv_w=place_conv(m_conv_w), dt_bias=m_dt_bias,
                   a_log=m_a_log, d_skip=m_d_skip, f_bias=m_f_bias)
    v_small = dict(norm_mix_w=v_norm_mix_w, norm_mlp_w=v_norm_mlp_w, norm_final_w=v_norm_final_w,
                   ssd_norm_w=v_ssd_norm_w, conv_b=v_conv_b, conv_w=place_conv(v_conv_w), dt_bias=v_dt_bias,
                   a_log=v_a_log, d_skip=v_d_skip, f_bias=v_f_bias)
    small = _small_reduce_adamw(_pack_small(g_small), _pack_small(w_small), _pack_small(m_small), _pack_small(v_small))
    small = [_unpack_small(s, like) for s in small]
    for s in small:
        s["conv_w"] = lax.dynamic_slice(s["conv_w"], (zero, zero, me * conv_shard_w), (1, CONV_K, conv_shard_w))

    order = ["norm_mix_w", "w_in", "conv_w", "conv_b", "dt_bias", "a_log", "d_skip", "ssd_norm_w", "f_bias", "w_out",
             "norm_mlp_w", "w_up", "w_down", "norm_final_w"]
    outs = [loss, grad_x]
    for kind in range(4):
        for name in order:
            outs.append(big[name][kind][None] if name in big else small[kind][name])
    return tuple(outs)
```

```python
import jax
import jax.numpy as jnp
from jax import lax
from jax.experimental import pallas as pl
from jax.experimental.pallas import tpu as pltpu

F32, BF16 = jnp.float32, jnp.bfloat16

D_MODEL = 1024
SSD_HEADS = 16
HEAD_DIM = 64
SSD_STATE = 128
CHUNK = 128
CONV_CH = 1536
CONV_K = 4
D_FF = 4096
MIX_WIDTH = 2048
IN_WIDTH = 5664
NORM_EPS = 1e-5
ATT_SCALE = 0.125

LANES = 128
SUBLANES = 8
HEAD_PAIRS = SSD_HEADS // 2
NEG = -1e30
VMEM_LIMIT = 52 * 1024 * 1024

ROW_XBC, ROW_DT, ROW_Q, ROW_F = 1024, 2560, 2576, 5648
F_COL = SSD_HEADS

N_DEV = 8
MESH_AXES = ("x", "y", "c")
MESH = pl.DeviceIdType.MESH

ADAM_LR, ADAM_B1, ADAM_B2, ADAM_EPS, ADAM_WD, ADAM_STEP = 0.001, 0.9, 0.999, 1e-08, 0.01, 10
ADAM_C1 = 1.0 - ADAM_B1 ** ADAM_STEP
ADAM_C2 = 1.0 - ADAM_B2 ** ADAM_STEP

SMALL_COLS = CONV_CH


def _params(sem=None):
    return pltpu.CompilerParams(dimension_semantics=sem, vmem_limit_bytes=VMEM_LIMIT)


def _sigmoid(u):
    return 1.0 / (1.0 + jnp.exp(-u))


def _softplus(u):
    return jnp.maximum(u, 0.0) + jnp.log(1.0 + jnp.exp(-jnp.abs(u)))


def _log_sigmoid(u):
    return jnp.minimum(u, 0.0) - jnp.log(1.0 + jnp.exp(-jnp.abs(u)))


def _bdot(a, b):
    return jnp.dot(a.astype(BF16), b.astype(BF16), preferred_element_type=F32)


def _bdot_nt(a, b):
    return lax.dot_general(a.astype(BF16), b.astype(BF16), (((1,), (1,)), ((), ())), preferred_element_type=F32)


def _bdot_tn(a, b):
    return lax.dot_general(a.astype(BF16), b.astype(BF16), (((0,), (0,)), ((), ())), preferred_element_type=F32)


def _split3(x):
    x1 = x.astype(BF16)
    r1 = x - x1.astype(F32)
    x2 = r1.astype(BF16)
    return x1, x2, (r1 - x2.astype(F32)).astype(BF16)


def _dot_sel(x, sel):
    s = sel.astype(BF16)
    p1, p2, p3 = (jnp.dot(p, s, preferred_element_type=F32) for p in _split3(x))
    return p1 + p2 + p3


def _sel_dot(sel, x):
    s = sel.astype(BF16)
    p1, p2, p3 = (jnp.dot(s, p, preferred_element_type=F32) for p in _split3(x))
    return p1 + p2 + p3


def _iota(shape, dim):
    return lax.broadcasted_iota(jnp.int32, shape, dim)


def _head_expand():
    return (jnp.right_shift(_iota((LANES, D_MODEL), 1), 6) == _iota((LANES, D_MODEL), 0)).astype(F32)


def _head_reduce():
    return (jnp.right_shift(_iota((D_MODEL, LANES), 0), 6) == _iota((D_MODEL, LANES), 1)).astype(F32)


def _rms_fwd(x, w, name):
    n, d = x.shape
    tm = min(1024, n)

    def body(x_ref, w_ref, o_ref):
        xv = x_ref[...]
        r = lax.rsqrt(jnp.mean(xv * xv, axis=-1, keepdims=True) + NORM_EPS)
        o_ref[...] = (xv * r * w_ref[...]).astype(BF16)

    return pl.pallas_call(
        body, name=name, grid=(n // tm,),
        in_specs=[pl.BlockSpec((tm, d), lambda i: (i, 0)), pl.BlockSpec((1, d), lambda i: (0, 0))],
        out_specs=pl.BlockSpec((tm, d), lambda i: (i, 0)),
        out_shape=jax.ShapeDtypeStruct((n, d), BF16),
        compiler_params=_params(("parallel",)),
    )(x, w)


def _mm(pairs, name, out_dtype, tm, tn, nt=False, res=None, exchange=None, rms_fwd=None, rms_bwd=None):
    m = pairs[0][0].shape[0]
    n = pairs[0][1].shape[0 if nt else 1]
    tm, tn = min(tm, m), min(tn, n)
    gm, gn = m // tm, n // tn
    npairs = len(pairs)
    kind, xs = (None, []) if exchange is None else exchange
    nx = len(xs)
    extra_in = [] if res is None else [res]
    if rms_bwd is not None:
        extra_in += list(rms_bwd)
    elif rms_fwd is not None:
        extra_in.append(rms_fwd)
    n_in = 2 * npairs + len(extra_in)
    n_out = 1 + (rms_fwd is not None or rms_bwd is not None)
    assert (rms_fwd is None and rms_bwd is None) or tn == n

    def body(*refs):
        x_refs, o_refs = refs[n_in:n_in + nx], refs[n_in + nx:n_in + nx + n_out]
        xo_refs, sems = refs[n_in + nx + n_out:n_in + 2 * nx + n_out], refs[n_in + 2 * nx + n_out:]
        extra = refs[2 * npairs:n_in]
        i, j = pl.program_id(0), pl.program_id(1)
        if nx:
            @pl.when((i == 0) & (j == 0))
            def _():
                _exchange_start(kind, x_refs, xo_refs, *sems)

        acc = None
        for p in range(npairs):
            a_v, b_v = refs[2 * p][...], refs[2 * p + 1][...]
            d = _bdot_nt(a_v, b_v) if nt else _bdot(a_v, b_v)
            acc = d if acc is None else acc + d
        if rms_bwd is not None:
            xv = extra[1][...]
            r = lax.rsqrt(jnp.mean(xv * xv, axis=-1, keepdims=True) + NORM_EPS)
            nrm = xv * r
            g = acc * extra[2][...]
            o_refs[0][...] = extra[0][...] + r * (g - nrm * jnp.mean(g * nrm, axis=-1, keepdims=True))

            @pl.when(i == 0)
            def _():
                o_refs[1][...] = jnp.zeros_like(o_refs[1])

            o_refs[1][...] += jnp.sum(acc * nrm, axis=0, keepdims=True)
        else:
            if res is not None:
                acc = extra[0][...] + acc
            o_refs[0][...] = acc.astype(o_refs[0].dtype)
            if rms_fwd is not None:
                r = lax.rsqrt(jnp.mean(acc * acc, axis=-1, keepdims=True) + NORM_EPS)
                o_refs[1][...] = (acc * r * extra[-1][...]).astype(BF16)
        if nx:
            @pl.when((i == gm - 1) & (j == gn - 1))
            def _():
                _exchange_wait(kind, x_refs, xo_refs, *sems)

    tile = pl.BlockSpec((tm, tn), lambda i, j: (i, j))
    row = pl.BlockSpec((1, tn), lambda i, j: (0, j))
    in_specs, args = [], []
    for a, b in pairs:
        k = a.shape[1]
        in_specs.append(pl.BlockSpec((tm, k), lambda i, j: (i, 0)))
        in_specs.append(pl.BlockSpec((tn, k), lambda i, j: (j, 0)) if nt else pl.BlockSpec((k, tn), lambda i, j: (0, j)))
        args += [a, b]
    in_specs += [row if e.shape[0] == 1 else tile for e in extra_in]
    out_specs, out_shape = [tile], [jax.ShapeDtypeStruct((m, n), out_dtype)]
    if rms_bwd is not None:
        out_specs.append(row)
        out_shape.append(jax.ShapeDtypeStruct((1, n), F32))
    elif rms_fwd is not None:
        out_specs.append(tile)
        out_shape.append(jax.ShapeDtypeStruct((m, n), BF16))
    hbm = pl.BlockSpec(memory_space=pl.ANY)
    sequential = nx or rms_bwd is not None
    outs = pl.pallas_call(
        body, name=name, grid=(gm, gn), in_specs=in_specs + [hbm] * nx,
        out_specs=out_specs + [hbm] * nx,
        out_shape=out_shape + _exchange_shapes(kind, xs),
        scratch_shapes=_exchange_scratch(nx),
        compiler_params=_params(("arbitrary", "arbitrary") if sequential else ("parallel", "parallel")),
    )(*args, *extra_in, *xs)
    return outs if len(outs) > 1 else outs[0]


def _mm_tn(a, b, name, tm, tn, tk):
    t, m = a.shape
    n = b.shape[1]
    tm, tn, tk = min(tm, m), min(tn, n), min(tk, t)
    nk = t // tk

    def body(a_ref, b_ref, o_ref, acc_ref):
        kk = pl.program_id(2)

        @pl.when(kk == 0)
        def _():
            acc_ref[...] = jnp.zeros_like(acc_ref)

        acc_ref[...] += _bdot_tn(a_ref[...], b_ref[...])

        @pl.when(kk == nk - 1)
        def _():
            o_ref[...] = acc_ref[...].astype(o_ref.dtype)

    return pl.pallas_call(
        body, name=name, grid=(m // tm, n // tn, nk),
        in_specs=[pl.BlockSpec((tk, tm), lambda i, j, kk: (kk, i)), pl.BlockSpec((tk, tn), lambda i, j, kk: (kk, j))],
        out_specs=pl.BlockSpec((tm, tn), lambda i, j, kk: (i, j)),
        out_shape=jax.ShapeDtypeStruct((m, n), BF16),
        scratch_shapes=[pltpu.VMEM((tm, tn), F32)],
        compiler_params=_params(("parallel", "parallel", "arbitrary")),
    )(a, b)


def _mlp_down_loss(pre, w_down, h2, target, w):
    m, k = pre.shape
    d = w_down.shape[1]
    tm = min(512, m)

    def body(p_ref, b_ref, r_ref, t_ref, w_ref, dh_ref, loss_ref, gw_ref):
        u = jnp.square(jnp.maximum(p_ref[...].astype(F32), 0.0))
        xv = r_ref[...] + _bdot(u, b_ref[...])
        r = lax.rsqrt(jnp.mean(xv * xv, axis=-1, keepdims=True) + NORM_EPS)
        nrm = xv * r
        wv = w_ref[...]
        err = nrm * wv - t_ref[...]
        part = 0.5 * jnp.sum(jnp.mean(err * err, axis=-1, keepdims=True), axis=0, keepdims=True)
        dy = err * (1.0 / d)
        g = dy * wv
        dh_ref[...] = r * (g - nrm * jnp.mean(g * nrm, axis=-1, keepdims=True))

        @pl.when(pl.program_id(0) == 0)
        def _():
            loss_ref[...] = jnp.zeros_like(loss_ref)
            gw_ref[...] = jnp.zeros_like(gw_ref)

        loss_ref[...] += jnp.broadcast_to(part, loss_ref.shape)
        gw_ref[...] += jnp.sum(dy * nrm, axis=0, keepdims=True)

    tok = pl.BlockSpec((tm, d), lambda i: (i, 0))
    row = pl.BlockSpec((1, d), lambda i: (0, 0))
    return pl.pallas_call(
        body, name="mlp_down_loss", grid=(m // tm,),
        in_specs=[pl.BlockSpec((tm, k), lambda i: (i, 0)), pl.BlockSpec((k, d), lambda i: (0, 0)), tok, tok, row],
        out_specs=[tok, pl.BlockSpec((1, LANES), lambda i: (0, 0)), row],
        out_shape=[jax.ShapeDtypeStruct((m, d), F32), jax.ShapeDtypeStruct((1, LANES), F32),
                   jax.ShapeDtypeStruct((1, d), F32)],
        compiler_params=_params(("arbitrary",)),
    )(pre, w_down, h2, target, w)


def _mlp_down_bwd(dh3, w_down, pre):
    m, k = dh3.shape
    n = w_down.shape[0]
    tm, tn = min(256, m), n

    def body(a_ref, b_ref, p_ref, dpre_ref, u_ref):
        r = jnp.maximum(p_ref[...].astype(F32), 0.0)
        du = _bdot_nt(a_ref[...], b_ref[...])
        dpre_ref[...] = (du * (2.0 * r)).astype(BF16)
        u_ref[...] = (r * r).astype(BF16)

    blk = pl.BlockSpec((tm, tn), lambda i, j: (i, j))
    return pl.pallas_call(
        body, name="mlp_down_bwd", grid=(m // tm, n // tn),
        in_specs=[pl.BlockSpec((tm, k), lambda i, j: (i, 0)), pl.BlockSpec((tn, k), lambda i, j: (j, 0)), blk],
        out_specs=[blk, blk],
        out_shape=[jax.ShapeDtypeStruct((m, n), BF16), jax.ShapeDtypeStruct((m, n), BF16)],
        compiler_params=_params(("parallel", "parallel")),
    )(dh3, w_down, pre)


CONV_TC = 512


def _conv_fwd(xbc, cw, cb, bsz, seq):
    tt = min(512, seq)
    nt, hb = seq // tt, tt // SUBLANES
    x3 = xbc.reshape(bsz, seq, CONV_CH)

    def body(xm_ref, xh_ref, w_ref, b_ref, o_ref):
        i = pl.program_id(2)
        x = xm_ref[0]
        halo = jnp.where(i > 0, xh_ref[0], 0.0)
        xx = jnp.concatenate([halo, x], axis=0)
        acc = x * w_ref[3:4, :] + b_ref[...]
        for s in range(1, CONV_K):
            acc = acc + pltpu.roll(xx, s, 0)[SUBLANES:, :] * w_ref[3 - s:4 - s, :]
        o_ref[0] = acc * _sigmoid(acc)

    out = pl.pallas_call(
        body, name="conv_fwd", grid=(CONV_CH // CONV_TC, bsz, nt),
        in_specs=[pl.BlockSpec((1, tt, CONV_TC), lambda c, b, i: (b, i, c)),
                  pl.BlockSpec((1, SUBLANES, CONV_TC), lambda c, b, i: (b, jnp.maximum(i * hb - 1, 0), c)),
                  pl.BlockSpec((CONV_K, CONV_TC), lambda c, b, i: (0, c)),
                  pl.BlockSpec((1, CONV_TC), lambda c, b, i: (0, c))],
        out_specs=pl.BlockSpec((1, tt, CONV_TC), lambda c, b, i: (b, i, c)),
        out_shape=jax.ShapeDtypeStruct((bsz, seq, CONV_CH), F32),
        compiler_params=_params(("parallel", "parallel", "parallel")),
    )(x3, x3, cw, cb)
    return out.reshape(bsz * seq, CONV_CH)


def _conv_bwd(da, xbc, cw, cb, bsz, seq):
    tt = min(512, seq)
    nt, hb = seq // tt, tt // SUBLANES
    x3 = xbc.reshape(bsz, seq, CONV_CH)
    da3 = da.reshape(bsz, seq, CONV_CH)
    n2 = tt + SUBLANES

    def body(dam_ref, daa_ref, xm_ref, xb_ref, xa_ref, w_ref, b_ref, du_ref, gw_ref, gb_ref):
        bi, i = pl.program_id(1), pl.program_id(2)
        before = jnp.where(i > 0, xb_ref[0], 0.0)
        after = jnp.where(i < nt - 1, xa_ref[0], 0.0)
        xx = jnp.concatenate([before, xm_ref[0], after], axis=0)
        rolled = [xx] + [pltpu.roll(xx, s, 0) for s in range(1, CONV_K)]
        pre = b_ref[...]
        for s in range(CONV_K):
            pre = pre + rolled[s][SUBLANES:, :] * w_ref[3 - s:4 - s, :]
        da_ext = jnp.concatenate([dam_ref[0], jnp.where(i < nt - 1, daa_ref[0], 0.0)], axis=0)
        sg = _sigmoid(pre)
        dpre = da_ext * sg * (1.0 + pre * (1.0 - sg))
        du = dpre[:tt, :] * w_ref[3:4, :]
        for s in range(1, CONV_K):
            du = du + pltpu.roll(dpre, n2 - s, 0)[:tt, :] * w_ref[3 - s:4 - s, :]
        du_ref[0] = du.astype(BF16)
        dpm = dpre[:tt, :]
        gws = [jnp.sum(dpm * rolled[3 - kk][SUBLANES:SUBLANES + tt, :], axis=0, keepdims=True) for kk in range(CONV_K)]

        @pl.when((bi == 0) & (i == 0))
        def _():
            gw_ref[...] = jnp.zeros_like(gw_ref)
            gb_ref[...] = jnp.zeros_like(gb_ref)

        gw_ref[...] += jnp.concatenate(gws, axis=0)
        gb_ref[...] += jnp.sum(dpm, axis=0, keepdims=True)

    main = pl.BlockSpec((1, tt, CONV_TC), lambda c, b, i: (b, i, c))
    prev = pl.BlockSpec((1, SUBLANES, CONV_TC), lambda c, b, i: (b, jnp.maximum(i * hb - 1, 0), c))
    nxt = pl.BlockSpec((1, SUBLANES, CONV_TC), lambda c, b, i: (b, jnp.minimum((i + 1) * hb, seq // SUBLANES - 1), c))
    du, gw, gb = pl.pallas_call(
        body, name="conv_bwd", grid=(CONV_CH // CONV_TC, bsz, nt),
        in_specs=[main, nxt, main, prev, nxt,
                  pl.BlockSpec((CONV_K, CONV_TC), lambda c, b, i: (0, c)),
                  pl.BlockSpec((1, CONV_TC), lambda c, b, i: (0, c))],
        out_specs=[main, pl.BlockSpec((CONV_K, CONV_TC), lambda c, b, i: (0, c)),
                   pl.BlockSpec((1, CONV_TC), lambda c, b, i: (0, c))],
        out_shape=[jax.ShapeDtypeStruct((bsz, seq, CONV_CH), BF16), jax.ShapeDtypeStruct((CONV_K, CONV_CH), F32),
                   jax.ShapeDtypeStruct((1, CONV_CH), F32)],
        compiler_params=_params(("parallel", "arbitrary", "arbitrary")),
    )(da3, da3, x3, x3, x3, cw, cb)
    return du.reshape(bsz * seq, CONV_CH), gw, gb


def _ssd_constants():
    tri = (_iota((CHUNK, CHUNK), 1) <= _iota((CHUNK, CHUNK), 0)).astype(BF16)
    return tri, tri.T, _head_expand().astype(BF16), _head_reduce().astype(BF16)


def _ssd_prologue(dtf_ref, dtft_ref, bias_ref, biast_ref, arow_ref, acol_ref, tri_ref, triu_ref, expand_ref,
                  dtfull_scr, acfull_scr, acr_scr):
    dtc = _softplus(dtf_ref[0] + bias_ref[...])
    a = dtc * arow_ref[...]
    acum = _sel_dot(tri_ref[...], a)
    expand = expand_ref[...]
    dtfull_scr[...] = _dot_sel(dtc, expand)
    acfull_scr[...] = _dot_sel(acum, expand)
    dtr = _softplus(dtft_ref[0] + biast_ref[...])
    acr_scr[...] = _dot_sel(dtr * acol_ref[...], triu_ref[...])
    return dtc, acum


def _decay_matrix(acum, acr_scr, h):
    lane = _iota((CHUNK, LANES), 1)
    ac_col = jnp.sum(jnp.where(lane == h, acum, 0.0), axis=1, keepdims=True)
    ac_row = acr_scr[h:h + 1, :]
    causal = _iota((CHUNK, CHUNK), 1) <= _iota((CHUNK, CHUNK), 0)
    return jnp.exp(jnp.where(causal, ac_col - ac_row, NEG))


def _ssd_fwd(xbc, dtf, dtft, bias, biast, arow, acol, dfull, bsz, seq):
    nc = seq // CHUNK
    x3 = xbc.reshape(bsz, seq, CONV_CH)
    tri, triu, expand, _ = _ssd_constants()

    def body(xbc_ref, dtf_ref, dtft_ref, bias_ref, biast_ref, arow_ref, acol_ref, dfull_ref,
             tri_ref, triu_ref, expand_ref, y_ref, hall_ref, h_scr, dtfull_scr, acfull_scr, acr_scr):
        @pl.when(pl.program_id(1) == 0)
        def _():
            h_scr[...] = jnp.zeros_like(h_scr)

        _, acum = _ssd_prologue(dtf_ref, dtft_ref, bias_ref, biast_ref, arow_ref, acol_ref,
                                tri_ref, triu_ref, expand_ref, dtfull_scr, acfull_scr, acr_scr)
        lane = _iota((CHUNK, LANES), 1)
        for g in range(2):
            bg = xbc_ref[0, :, D_MODEL + LANES * g:D_MODEL + LANES * (g + 1)]
            cg = xbc_ref[0, :, D_MODEL + 2 * LANES + LANES * g:D_MODEL + 2 * LANES + LANES * (g + 1)]
            cg_bf = cg.astype(BF16)
            gmat = _bdot_nt(cg_bf, bg)
            bgt_bf = bg.T.astype(BF16)
            for j in range(4):
                p = 4 * g + j
                sl = slice(LANES * p, LANES * (p + 1))
                xs = xbc_ref[0, :, sl]
                ac = acfull_scr[:, sl]
                acl = acfull_scr[CHUNK - 1:CHUNK, sl]
                xdt = xs * dtfull_scr[:, sl]
                xdt_bf = xdt.astype(BF16)
                hp = h_scr[p]
                hall_ref[0, 0, p] = hp
                yo = _bdot(cg_bf, hp) * jnp.exp(ac)
                yd = []
                for hh in range(2):
                    mmat = gmat * _decay_matrix(acum, acr_scr, 2 * p + hh)
                    yd.append(_bdot(mmat, xdt_bf))
                y_ref[0, :, sl] = jnp.where(lane < HEAD_DIM, yd[0], yd[1]) + yo + dfull_ref[:, sl] * xs
                h_scr[p] = hp * jnp.exp(acl) + _bdot(bgt_bf, xdt * jnp.exp(acl - ac))

    row = lambda w: pl.BlockSpec((1, w), lambda b, c: (0, 0))
    whole = lambda a: pl.BlockSpec(a.shape, lambda b, c: (0, 0))
    y, hall = pl.pallas_call(
        body, name="ssd_fwd", grid=(bsz, nc),
        in_specs=[pl.BlockSpec((1, CHUNK, CONV_CH), lambda b, c: (b, c, 0)),
                  pl.BlockSpec((1, CHUNK, LANES), lambda b, c: (b, c, 0)),
                  pl.BlockSpec((1, LANES, CHUNK), lambda b, c: (b, 0, c)),
                  row(LANES), pl.BlockSpec((LANES, 1), lambda b, c: (0, 0)),
                  row(LANES), pl.BlockSpec((LANES, 1), lambda b, c: (0, 0)), row(D_MODEL),
                  whole(tri), whole(triu), whole(expand)],
        out_specs=[pl.BlockSpec((1, CHUNK, D_MODEL), lambda b, c: (b, c, 0)),
                   pl.BlockSpec((1, 1, HEAD_PAIRS, SSD_STATE, LANES), lambda b, c: (b, c, 0, 0, 0))],
        out_shape=[jax.ShapeDtypeStruct((bsz, seq, D_MODEL), F32),
                   jax.ShapeDtypeStruct((bsz, nc, HEAD_PAIRS, SSD_STATE, LANES), F32)],
        scratch_shapes=[pltpu.VMEM((HEAD_PAIRS, SSD_STATE, LANES), F32), pltpu.VMEM((CHUNK, D_MODEL), F32),
                        pltpu.VMEM((CHUNK, D_MODEL), F32), pltpu.VMEM((LANES, CHUNK), F32)],
        compiler_params=_params(("parallel", "arbitrary")),
    )(x3, dtf.reshape(bsz, seq, LANES), dtft, bias, biast, arow, acol, dfull, tri, triu, expand)
    return y.reshape(bsz * seq, D_MODEL), hall


def _ssd_bwd(dy, xbc, dtf, dtft, bias, biast, arow, acol, dfull, hall, bsz, seq):
    nc = seq // CHUNK
    x3 = xbc.reshape(bsz, seq, CONV_CH)
    dy3 = dy.reshape(bsz, seq, D_MODEL)
    consts = _ssd_constants()

    def body(dy_ref, xbc_ref, dtf_ref, dtft_ref, bias_ref, biast_ref, arow_ref, acol_ref, dfull_ref, hall_ref,
             tri_ref, triu_ref, expand_ref, reduce_ref,
             dx_ref, ddt_ref, acc_ref, dh_scr, dtfull_scr, acfull_scr, acr_scr, csum_scr, dacf_scr, ddtf_scr, ddl_scr):
        first = (pl.program_id(0) == 0) & (pl.program_id(1) == 0)

        @pl.when(first)
        def _():
            acc_ref[...] = jnp.zeros_like(acc_ref)

        @pl.when(pl.program_id(1) == 0)
        def _():
            dh_scr[...] = jnp.zeros_like(dh_scr)

        dtc, acum = _ssd_prologue(dtf_ref, dtft_ref, bias_ref, biast_ref, arow_ref, acol_ref,
                                  tri_ref, triu_ref, expand_ref, dtfull_scr, acfull_scr, acr_scr)
        csum_scr[...] = jnp.zeros_like(csum_scr)
        lane = _iota((CHUNK, LANES), 1)
        last_row = _iota((CHUNK, LANES), 0) == CHUNK - 1
        rs16 = jnp.zeros((CHUNK, LANES), F32)
        for g in range(2):
            bsl = slice(D_MODEL + LANES * g, D_MODEL + LANES * (g + 1))
            csl = slice(D_MODEL + 2 * LANES + LANES * g, D_MODEL + 2 * LANES + LANES * (g + 1))
            bg_bf = xbc_ref[0, :, bsl].astype(BF16)
            cg = xbc_ref[0, :, csl]
            cg_bf = cg.astype(BF16)
            cgt_bf = cg.T.astype(BF16)
            gmat = _bdot_nt(cg_bf, bg_bf)
            dg = jnp.zeros((CHUNK, CHUNK), F32)
            dbg = jnp.zeros((CHUNK, SSD_STATE), F32)
            dcg = jnp.zeros((CHUNK, SSD_STATE), F32)
            for j in range(4):
                p = 4 * g + j
                sl = slice(LANES * p, LANES * (p + 1))
                xs = xbc_ref[0, :, sl]
                dt = dtfull_scr[:, sl]
                ac = acfull_scr[:, sl]
                acl = acfull_scr[CHUNK - 1:CHUNK, sl]
                dyp = dy_ref[0, :, sl]
                xdt = xs * dt
                xdt_bf = xdt.astype(BF16)
                e = jnp.exp(ac)
                dsd = jnp.exp(acl - ac)
                cd = jnp.exp(acl)
                xds_bf = (xdt * dsd).astype(BF16)
                hp = hall_ref[0, 0, p]
                hp_bf = hp.astype(BF16)
                dhn = dh_scr[p]
                dhn_bf = dhn.astype(BF16)
                yo = _bdot(cg_bf, hp_bf) * e
                dw_bf = (dyp * e).astype(BF16)
                dcg = dcg + _bdot_nt(dw_bf, hp_bf)
                dhin = _bdot(cgt_bf, dw_bf)
                dac = dyp * yo
                dacl = jnp.sum(dhn * hp, axis=0, keepdims=True) * cd
                dxds = _bdot(bg_bf, dhn_bf)
                dbg = dbg + _bdot_nt(xds_bf, dhn_bf)
                dxdt = dxds * dsd
                tdec = dxds * xdt * dsd
                dacl = dacl + jnp.sum(tdec, axis=0, keepdims=True)
                dac = dac - tdec
                dh_scr[p] = dhn * cd + dhin
                for hh in range(2):
                    h = 2 * p + hh
                    lm = (lane < HEAD_DIM) if hh == 0 else (lane >= HEAD_DIM)
                    dec = _decay_matrix(acum, acr_scr, h)
                    mmat = gmat * dec
                    dyh_bf = jnp.where(lm, dyp, 0.0).astype(BF16)
                    dm = _bdot_nt(dyh_bf, xdt_bf)
                    dxdt = dxdt + _bdot(mmat.T, dyh_bf)
                    dg = dg + dm * dec
                    q = dm * mmat
                    rs16 = rs16 + jnp.where(lane == h, jnp.sum(q, axis=1, keepdims=True), 0.0)
                    csum_scr[h:h + 1, :] = jnp.sum(q, axis=0, keepdims=True)
                dx_ref[0, :, sl] = dfull_ref[:, sl] * dyp + dxdt * dt
                dacf_scr[:, sl] = dac + jnp.where(last_row, dacl, 0.0)
                ddtf_scr[:, sl] = dxdt * xs
                ddl_scr[:, sl] = jnp.broadcast_to(jnp.sum(dyp * xs, axis=0, keepdims=True), (SUBLANES, LANES))
            dg_bf = dg.astype(BF16)
            dx_ref[0, :, bsl] = dbg + _bdot(dg.T, cg_bf)
            dx_ref[0, :, csl] = dcg + _bdot(dg_bf, bg_bf)
        reduce, triu = reduce_ref[...], triu_ref[...]
        dac16 = _dot_sel(dacf_scr[...], reduce) + rs16 - csum_scr[...].T
        da16 = _sel_dot(triu, dac16)
        ddt16 = da16 * arow_ref[...] + _dot_sel(ddtf_scr[...], reduce)
        ddtraw = ddt16 * _sigmoid(dtf_ref[0] + bias_ref[...])
        ddt_ref[0] = ddtraw
        acc_ref[0:8, :] += jnp.broadcast_to(jnp.sum(da16 * dtc * arow_ref[...], axis=0, keepdims=True), (SUBLANES, LANES))
        acc_ref[8:16, :] += _dot_sel(ddl_scr[...], reduce)
        acc_ref[16:24, :] += jnp.broadcast_to(jnp.sum(ddtraw, axis=0, keepdims=True), (SUBLANES, LANES))

    rev = lambda b, c: (b, nc - 1 - c, 0)
    row = lambda w: pl.BlockSpec((1, w), lambda b, c: (0, 0))
    dx, ddt, acc = pl.pallas_call(
        body, name="ssd_bwd", grid=(bsz, nc),
        in_specs=[pl.BlockSpec((1, CHUNK, D_MODEL), rev), pl.BlockSpec((1, CHUNK, CONV_CH), rev),
                  pl.BlockSpec((1, CHUNK, LANES), rev),
                  pl.BlockSpec((1, LANES, CHUNK), lambda b, c: (b, 0, nc - 1 - c)),
                  row(LANES), pl.BlockSpec((LANES, 1), lambda b, c: (0, 0)),
                  row(LANES), pl.BlockSpec((LANES, 1), lambda b, c: (0, 0)), row(D_MODEL),
                  pl.BlockSpec((1, 1, HEAD_PAIRS, SSD_STATE, LANES), lambda b, c: (b, nc - 1 - c, 0, 0, 0))]
        + [pl.BlockSpec(a.shape, lambda b, c: (0, 0)) for a in consts],
        out_specs=[pl.BlockSpec((1, CHUNK, CONV_CH), rev), pl.BlockSpec((1, CHUNK, LANES), rev),
                   pl.BlockSpec((3 * SUBLANES, LANES), lambda b, c: (0, 0))],
        out_shape=[jax.ShapeDtypeStruct((bsz, seq, CONV_CH), F32), jax.ShapeDtypeStruct((bsz, seq, LANES), F32),
                   jax.ShapeDtypeStruct((3 * SUBLANES, LANES), F32)],
        scratch_shapes=[pltpu.VMEM((HEAD_PAIRS, SSD_STATE, LANES), F32), pltpu.VMEM((CHUNK, D_MODEL), F32),
                        pltpu.VMEM((CHUNK, D_MODEL), F32), pltpu.VMEM((LANES, CHUNK), F32),
                        pltpu.VMEM((LANES, CHUNK), F32), pltpu.VMEM((CHUNK, D_MODEL), F32),
                        pltpu.VMEM((CHUNK, D_MODEL), F32), pltpu.VMEM((SUBLANES, D_MODEL), F32)],
        compiler_params=_params(("arbitrary", "arbitrary")),
    )(dy3, x3, dtf.reshape(bsz, seq, LANES), dtft, bias, biast, arow, acol, dfull, hall, *consts)
    return dx.reshape(bsz * seq, CONV_CH), ddt.reshape(bsz * seq, LANES), acc


GROUP_W = D_MODEL // 2


def _gnorm_fwd(y, z, o_att, w):
    n = y.shape[0]
    tm = min(1024, n)

    def body(y_ref, z_ref, o_ref, w_ref, out_ref):
        zv = z_ref[...]
        g = y_ref[...] * (zv * _sigmoid(zv))
        for gi in range(2):
            sl = slice(GROUP_W * gi, GROUP_W * (gi + 1))
            gs = g[:, sl]
            r = lax.rsqrt(jnp.mean(gs * gs, axis=-1, keepdims=True) + NORM_EPS)
            out_ref[:, sl] = (gs * r * w_ref[:, sl]).astype(BF16)
        out_ref[:, D_MODEL:] = o_ref[...].astype(BF16)

    tok = pl.BlockSpec((tm, D_MODEL), lambda i: (i, 0))
    return pl.pallas_call(
        body, name="gnorm_fwd", grid=(n // tm,),
        in_specs=[tok, tok, tok, pl.BlockSpec((1, D_MODEL), lambda i: (0, 0))],
        out_specs=pl.BlockSpec((tm, MIX_WIDTH), lambda i: (i, 0)),
        out_shape=jax.ShapeDtypeStruct((n, MIX_WIDTH), BF16),
        compiler_params=_params(("parallel",)),
    )(y, z, o_att, w)


def _gnorm_bwd(dycat, y, z, w):
    n = y.shape[0]
    tm = min(512, n)

    def body(dn_ref, y_ref, z_ref, w_ref, dy_ref, dz_ref, gw_ref):
        zv, yv = z_ref[...], y_ref[...]
        sz = _sigmoid(zv)
        sil = zv * sz
        g = yv * sil

        @pl.when(pl.program_id(0) == 0)
        def _():
            gw_ref[...] = jnp.zeros_like(gw_ref)

        for gi in range(2):
            sl = slice(GROUP_W * gi, GROUP_W * (gi + 1))
            gs = g[:, sl]
            r = lax.rsqrt(jnp.mean(gs * gs, axis=-1, keepdims=True) + NORM_EPS)
            nrm = gs * r
            dyn = dn_ref[:, sl]
            dn = dyn * w_ref[:, sl]
            dgs = r * (dn - nrm * jnp.mean(dn * nrm, axis=-1, keepdims=True))
            dy_ref[:, sl] = dgs * sil[:, sl]
            dz_ref[:, sl] = (dgs * yv[:, sl] * (sz[:, sl] * (1.0 + zv[:, sl] * (1.0 - sz[:, sl])))).astype(BF16)
            gw_ref[:, sl] += jnp.sum(dyn * nrm, axis=0, keepdims=True)

    tok = pl.BlockSpec((tm, D_MODEL), lambda i: (i, 0))
    row = pl.BlockSpec((1, D_MODEL), lambda i: (0, 0))
    return pl.pallas_call(
        body, name="gnorm_bwd", grid=(n // tm,),
        in_specs=[tok, tok, tok, row], out_specs=[tok, tok, row],
        out_shape=[jax.ShapeDtypeStruct((n, D_MODEL), F32), jax.ShapeDtypeStruct((n, D_MODEL), BF16),
                   jax.ShapeDtypeStruct((1, D_MODEL), F32)],
        compiler_params=_params(("arbitrary",)),
    )(dycat, y, z, w)


AUX_C = 3
AUX_ONE = 3


def _aux_base(hh):
    return HEAD_DIM * (1 - hh)


def _fox_prep(dtf, fb, bsz, seq):
    def body(x_ref, b_ref, aux_ref):
        tri = (_iota((CHUNK, CHUNK), 1) <= _iota((CHUNK, CHUNK), 0)).astype(F32)
        row, col = _iota((LANES, D_MODEL), 0), _iota((LANES, D_MODEL), 1)
        lane = jnp.bitwise_and(col, LANES - 1)
        other = (lane < HEAD_DIM).astype(jnp.int32)
        term = lane - HEAD_DIM * (1 - other)
        src = F_COL + 2 * jnp.right_shift(col, 7) + other
        place = [jnp.where((row == src) & (term == i), -1.0, 0.0).astype(BF16) for i in range(AUX_C)]
        lane1 = jnp.bitwise_and(_iota((1, D_MODEL), 1), LANES - 1)
        ones_lane = (lane1 - HEAD_DIM * (1 - (lane1 < HEAD_DIM).astype(jnp.int32)) == AUX_ONE).astype(F32)
        carry = jnp.zeros((1, LANES), F32)
        for j in range(seq // CHUNK):
            sl = slice(CHUNK * j, CHUNK * (j + 1))
            lf = _log_sigmoid(x_ref[sl, :] + b_ref[...])
            c = _sel_dot(tri, lf) + carry
            carry = carry + jnp.sum(lf, axis=0, keepdims=True)
            t1, t2, t3 = (jnp.dot(t, p, preferred_element_type=F32) for t, p in zip(_split3(c), place))
            aux_ref[sl, :] = (t1 + t2 + t3 + ones_lane).astype(BF16)

    return pl.pallas_call(
        body, name="fox_prep", grid=(bsz,),
        in_specs=[pl.BlockSpec((seq, LANES), lambda b: (b, 0)), pl.BlockSpec((1, LANES), lambda b: (0, 0))],
        out_specs=pl.BlockSpec((seq, D_MODEL), lambda b: (b, 0)),
        out_shape=jax.ShapeDtypeStruct((bsz * seq, D_MODEL), BF16),
        compiler_params=_params(("parallel",)),
    )(dtf, fb)


def _fox_prep_bwd(dck, dcq, ddt, dtf, fb, bsz, seq):
    nj = seq // CHUNK

    def body(dck_ref, dcq_ref, ddt_ref, x_ref, b_ref, out_ref, gb_ref):
        triu = (_iota((CHUNK, CHUNK), 0) <= _iota((CHUNK, CHUNK), 1)).astype(F32)
        is_f = (_iota((CHUNK, LANES), 1) >= F_COL) & (_iota((CHUNK, LANES), 1) < 2 * F_COL)
        carry = jnp.zeros((1, LANES), F32)
        total = jnp.zeros((1, LANES), F32)
        for j in range(nj - 1, -1, -1):
            sl = slice(CHUNK * j, CHUNK * (j + 1))
            dcv = dck_ref[sl, :] + dcq_ref[sl, :]
            dlf = _sel_dot(triu, dcv) + carry
            carry = carry + jnp.sum(dcv, axis=0, keepdims=True)
            df = jnp.where(is_f, dlf * _sigmoid(-(x_ref[sl, :] + b_ref[...])), 0.0)
            out_ref[sl, :] = (df + ddt_ref[sl, :]).astype(BF16)
            total = total + jnp.sum(df, axis=0, keepdims=True)

        @pl.when(pl.program_id(0) == 0)
        def _():
            gb_ref[...] = jnp.zeros_like(gb_ref)

        gb_ref[...] += jnp.broadcast_to(total, (SUBLANES, LANES))

    blk = pl.BlockSpec((seq, LANES), lambda b: (b, 0))
    return pl.pallas_call(
        body, name="fox_prep_bwd", grid=(bsz,),
        in_specs=[blk, blk, blk, blk, pl.BlockSpec((1, LANES), lambda b: (0, 0))],
        out_specs=[blk, pl.BlockSpec((SUBLANES, LANES), lambda b: (0, 0))],
        out_shape=[jax.ShapeDtypeStruct((bsz * seq, LANES), BF16), jax.ShapeDtypeStruct((SUBLANES, LANES), F32)],
        compiler_params=_params(("arbitrary",)),
    )(dck, dcq, ddt, dtf, fb)


ATT_BLOCK_FWD = 2048
ATT_BLOCK_KEYS = 256
ATT_BLOCK_BWD = 256


def _att_operands(q_ref, k_ref, aux_ref, hh):
    lane = _iota((1, LANES), 1)
    lm = (lane < HEAD_DIM) if hh == 0 else (lane >= HEAD_DIM)
    base = _aux_base(hh)
    zero = jnp.zeros((1, LANES), BF16)
    ka = jnp.where(lm, k_ref[...], jnp.where((lane >= base) & (lane <= base + AUX_ONE), aux_ref[...], zero))
    qa = jnp.where(lm, q_ref[...] * ATT_SCALE,
                   jnp.where((lane >= base) & (lane < base + AUX_C), jnp.ones((1, LANES), BF16), zero))
    return lm, base, qa, ka


def _att_fwd(qkv, ccol, bsz, seq, gather):
    bq, bk = min(ATT_BLOCK_FWD, seq), min(ATT_BLOCK_KEYS, seq)
    nq, ratio = seq // bq, bq // bk
    nx = len(gather)

    def body(*refs):
        q_ref, k_ref, v_ref, c_ref = refs[:4]
        x_refs = refs[4:4 + nx]
        o_ref, lse_ref = refs[4 + nx:6 + nx]
        xo_refs = refs[6 + nx:6 + 2 * nx]
        qa_scr, ka_scr, vt_scr = refs[6 + 2 * nx:9 + 2 * nx]
        sems = refs[9 + 2 * nx:]
        pair = pl.program_id(1)

        @pl.when((pl.program_id(0) == 0) & (pair == 0))
        def _():
            _exchange_start("gather", x_refs, xo_refs, *sems)

        lse_ref[...] = jnp.zeros_like(lse_ref)
        for hh in range(2):
            _, _, qa, ka = _att_operands(q_ref, k_ref, c_ref, hh)
            qa_scr[hh] = qa
            ka_scr[hh] = ka
        for t0 in range(0, seq, bq):
            vt_scr[:, t0:t0 + bq] = v_ref[t0:t0 + bq, :].T
        key_minus_query = _iota((bk, bq), 0) - _iota((bk, bq), 1)

        def q_step(i, carry0):
            qrows = pl.ds(pl.multiple_of(i * bq, bq), bq)

            def scores(j):
                krows = pl.ds(pl.multiple_of(j * bk, bk), bk)
                return tuple(_bdot_nt(ka_scr[hh, krows, :], qa_scr[hh, qrows, :]) for hh in range(2))

            def scores_tail(r):
                krows = pl.ds(pl.multiple_of((i * ratio + r) * bk, bk), bk)
                qsub = pl.ds(pl.multiple_of(i * bq + r * bk, bk), bq - r * bk)
                return tuple(_bdot_nt(ka_scr[hh, krows, :], qa_scr[hh, qsub, :]) for hh in range(2))

            def update(state, st_pair, j, masked):
                krows = pl.ds(pl.multiple_of(j * bk, bk), bk)
                out = []
                for hh in range(2):
                    m_all, l_all, acc_all = state[hh]
                    st = st_pair[hh]
                    width = st.shape[1]
                    lo = bq - width
                    m, l, acc = m_all[:, lo:], l_all[:, lo:], acc_all[:, lo:]
                    if masked:
                        st = jnp.where(key_minus_query[:, :width] <= 0, st, NEG)
                    mn = jnp.maximum(m, jnp.max(st, axis=0, keepdims=True))
                    alpha = jnp.exp(m - mn)
                    pt = jnp.exp(st - mn)
                    l = alpha * l + jnp.sum(pt, axis=0, keepdims=True)
                    vt = vt_scr[HEAD_DIM * hh:HEAD_DIM * (hh + 1), krows]
                    acc = alpha * acc + _bdot(vt, pt)
                    if lo:
                        mn, l, acc = (jnp.concatenate([old[:, :lo], new], axis=1)
                                      for old, new in ((m_all, mn), (l_all, l), (acc_all, acc)))
                    out.append((mn, l, acc))
                return tuple(out)

            def step(j, carry):
                state, s_cur = carry
                s_next = scores(j + 1)
                return update(state, s_cur, j, False), s_next

            def step_pair(t, carry):
                state, s_cur = carry
                s_second = scores(2 * t + 1)
                s_next = scores(2 * t + 2)
                state = update(state, s_cur, 2 * t, False)
                return update(state, s_second, 2 * t + 1, False), s_next

            one = (jnp.full((1, bq), NEG, F32), jnp.zeros((1, bq), F32), jnp.zeros((HEAD_DIM, bq), F32))
            first_diag = i * ratio
            if ratio % 2 == 0:
                state, s_cur = lax.fori_loop(0, first_diag // 2, step_pair, ((one, one), scores(0)))
            else:
                state, s_cur = lax.fori_loop(0, first_diag, step, ((one, one), scores(0)))
            for r in range(ratio):
                s_next = scores_tail(r + 1) if r + 1 < ratio else None
                state = update(state, s_cur, first_diag + r, True)
                s_cur = s_next
            (m0, l0, acc0), (m1, l1, acc1) = state
            ot = jnp.concatenate([acc0 * (1.0 / l0), acc1 * (1.0 / l1)], axis=0)
            o_ref[qrows, :] = ot.T
            lse_ref[0, 0, 0:1, qrows] = m0 + jnp.log(l0)
            lse_ref[0, 0, 1:2, qrows] = m1 + jnp.log(l1)
            return carry0

        lax.fori_loop(0, nq, q_step, 0)

        @pl.when((pl.program_id(0) == bsz - 1) & (pair == HEAD_PAIRS - 1))
        def _():
            _exchange_wait("gather", x_refs, xo_refs, *sems)

    col = lambda off: pl.BlockSpec((seq, LANES), lambda b, p: (b, off + p))
    head2 = pltpu.VMEM((2, seq, LANES), BF16)
    hbm = pl.BlockSpec(memory_space=pl.ANY)
    return pl.pallas_call(
        body, name="att_fwd", grid=(bsz, HEAD_PAIRS),
        in_specs=[col(0), col(HEAD_PAIRS), col(2 * HEAD_PAIRS), col(0)] + [hbm] * nx,
        out_specs=[pl.BlockSpec((seq, LANES), lambda b, p: (b, p)),
                   pl.BlockSpec((1, 1, SUBLANES, seq), lambda b, p: (b, p, 0, 0))] + [hbm] * nx,
        out_shape=[jax.ShapeDtypeStruct((bsz * seq, D_MODEL), F32),
                   jax.ShapeDtypeStruct((bsz, HEAD_PAIRS, SUBLANES, seq), F32)] + _exchange_shapes("gather", gather),
        scratch_shapes=[head2, head2, pltpu.VMEM((LANES, seq), BF16)] + _exchange_scratch(nx),
        compiler_params=_params(("arbitrary", "arbitrary")),
    )(qkv, qkv, qkv, ccol, *gather)


def _att_bwd(qkv, dycat, o, lse, ccol, bsz, seq, scatter):
    blk = bq = min(ATT_BLOCK_BWD, seq)
    nb = nq = seq // blk
    nx = len(scatter)

    def body(*refs):
        q_ref, k_ref, v_ref, do_ref, o_ref, lse_ref, c_ref = refs[:7]
        x_refs = refs[7:7 + nx]
        dq_ref, dk_ref, dv_ref, dck_ref, dcq_ref = refs[7 + nx:12 + nx]
        xo_refs = refs[12 + nx:12 + 2 * nx]
        (qa_scr, ka_scr, va_scr, doa_scr, kat_scr, qat_scr, dot_scr, d_scr, dqt_scr,
         dkt_scr) = refs[12 + 2 * nx:22 + 2 * nx]
        sems = refs[22 + 2 * nx:]
        pair = pl.program_id(1)

        @pl.when((pl.program_id(0) == 0) & (pair == 0))
        def _():
            _exchange_start("scatter", x_refs, xo_refs, *sems)

        query_minus_key = _iota((blk, bq), 1) - _iota((blk, bq), 0)
        lane_b = _iota((blk, LANES), 1)
        row_t = _iota((LANES, seq), 0)
        bases = []
        ones8 = jnp.ones((SUBLANES, LANES), F32)
        for hh in range(2):
            lm, base, qa, ka = _att_operands(q_ref, k_ref, c_ref, hh)
            bases.append(base)
            qa_scr[hh] = qa
            ka_scr[hh] = ka
            va_scr[hh] = jnp.where(lm, v_ref[...], jnp.zeros((seq, LANES), BF16))
            doa = jnp.where(lm, do_ref[...], 0.0).astype(BF16)
            doa_scr[hh] = doa
            for t0 in range(0, seq, blk):
                kat_scr[hh, :, t0:t0 + blk] = ka[t0:t0 + blk, :].T
                qat_scr[hh, :, t0:t0 + blk] = qa[t0:t0 + blk, :].T
            d1, d2, d3 = (_bdot_nt(ones8, term) for term in _split3(doa.astype(F32) * o_ref[...]))
            d_scr[hh] = d1 + d2 + d3
        for t0 in range(0, seq, blk):
            dot_scr[:, t0:t0 + blk] = do_ref[t0:t0 + blk, :].astype(BF16).T
        dqt_scr[...] = jnp.zeros_like(dqt_scr)
        dck_ref[...] = jnp.zeros_like(dck_ref)

        for j in range(nb):
            krows = pl.ds(j * blk, blk)
            dkt, dvt = [None, None], [None, None]
            for i in range(j, nq):
                rows = pl.ds(i * bq, bq)
                for hh in range(2):
                    st = _bdot_nt(ka_scr[hh, krows, :], qa_scr[hh, rows, :])
                    dpt = _bdot_nt(va_scr[hh, krows, :], doa_scr[hh, rows, :])
                    if i == j:
                        st = jnp.where(query_minus_key >= 0, st, NEG)
                    pt = jnp.exp(st - lse_ref[0, 0, hh:hh + 1, rows])
                    dst = (pt * (dpt - d_scr[hh, 0:1, rows])).astype(BF16)
                    dv_part = _bdot_nt(dot_scr[HEAD_DIM * hh:HEAD_DIM * (hh + 1), rows], pt)
                    dk_part = _bdot_nt(qat_scr[hh, :, rows], dst)
                    dvt[hh] = dv_part if dvt[hh] is None else dvt[hh] + dv_part
                    dkt[hh] = dk_part if dkt[hh] is None else dkt[hh] + dk_part
                    dqt_scr[hh, :, rows] += _bdot(kat_scr[hh, :, krows], dst)
            dk_ref[krows, :] = jnp.where(_iota((LANES, blk), 0) < HEAD_DIM, dkt[0], dkt[1]).T.astype(BF16)
            dv_ref[krows, :] = jnp.concatenate(dvt, axis=0).T.astype(BF16)
            for hh in range(2):
                dkt_scr[hh] = dkt[hh]
                dck_ref[0, 0, hh:hh + 1, krows] = -dkt_scr[hh, bases[hh]:bases[hh] + 1, :]
        for t0 in range(0, seq, blk):
            dq0, dq1 = dqt_scr[0, :, t0:t0 + blk].T, dqt_scr[1, :, t0:t0 + blk].T
            dq_ref[t0:t0 + blk, :] = (jnp.where(lane_b < HEAD_DIM, dq0, dq1) * ATT_SCALE).astype(BF16)
        dcq_ref[...] = jnp.zeros_like(dcq_ref)
        for hh in range(2):
            dcq_ref[0, 0, hh:hh + 1, :] = jnp.sum(jnp.where(row_t == bases[hh] + AUX_ONE, dqt_scr[hh], 0.0),
                                                   axis=0, keepdims=True)

        @pl.when((pl.program_id(0) == bsz - 1) & (pair == HEAD_PAIRS - 1))
        def _():
            _exchange_wait("scatter", x_refs, xo_refs, *sems)

    col = lambda off: pl.BlockSpec((seq, LANES), lambda b, p: (b, off + p))
    rowvec = pl.BlockSpec((1, 1, SUBLANES, seq), lambda b, p: (b, p, 0, 0))
    out = jax.ShapeDtypeStruct((bsz * seq, D_MODEL), BF16)
    rows_shape = jax.ShapeDtypeStruct((bsz, HEAD_PAIRS, SUBLANES, seq), F32)
    head2 = pltpu.VMEM((2, seq, LANES), BF16)
    hbm = pl.BlockSpec(memory_space=pl.ANY)
    return pl.pallas_call(
        body, name="att_bwd", grid=(bsz, HEAD_PAIRS),
        in_specs=[col(0), col(HEAD_PAIRS), col(2 * HEAD_PAIRS), col(HEAD_PAIRS), col(0), rowvec, col(0)] + [hbm] * nx,
        out_specs=[col(0), col(0), col(0), rowvec, rowvec] + [hbm] * nx,
        out_shape=[out, out, out, rows_shape, rows_shape] + _exchange_shapes("scatter", scatter),
        scratch_shapes=[head2, head2, head2, head2, pltpu.VMEM((2, LANES, seq), BF16),
                        pltpu.VMEM((2, LANES, seq), BF16), pltpu.VMEM((LANES, seq), BF16),
                        pltpu.VMEM((2, SUBLANES, seq), F32), pltpu.VMEM((2, LANES, seq), F32),
                        pltpu.VMEM((2, LANES, blk), F32)] + _exchange_scratch(nx),
        compiler_params=_params(("arbitrary", "arbitrary")),
    )(qkv, qkv, qkv, dycat, o, lse, ccol, *scatter)


def _adamw_math(w, g, m, v):
    m = ADAM_B1 * m + (1.0 - ADAM_B1) * g
    v = ADAM_B2 * v + (1.0 - ADAM_B2) * jnp.square(g)
    m_hat = m / ADAM_C1
    v_hat = v / ADAM_C2
    delta = -ADAM_LR * (m_hat / (jnp.sqrt(v_hat) + ADAM_EPS) + ADAM_WD * w)
    return delta, m, v


def _row_tile(rows):
    for tr in (256, 128, 64, 32, 16, 8):
        if rows % tr == 0:
            return tr
    return rows


def _sum_parts(parts, name):
    nparts, rows, cols = parts.shape
    tr = rows if rows % SUBLANES else _row_tile(rows)

    def body(p_ref, o_ref):
        g = p_ref[0].astype(F32)
        for s in range(1, nparts):
            g = g + p_ref[s].astype(F32)
        o_ref[...] = g

    return pl.pallas_call(
        body, name=name, grid=(rows // tr,),
        in_specs=[pl.BlockSpec((nparts, tr, cols), lambda i: (0, i, 0))],
        out_specs=pl.BlockSpec((tr, cols), lambda i: (i, 0)),
        out_shape=jax.ShapeDtypeStruct((rows, cols), F32),
        compiler_params=_params(("parallel",)),
    )(parts)


def _adamw_parts(parts, w, m, v, name):
    nparts, rows, cols = parts.shape
    tr = _row_tile(rows)

    def body(p_ref, w_ref, m_ref, v_ref, g_ref, d_ref, mo_ref, vo_ref):
        g = p_ref[0].astype(F32)
        for s in range(1, nparts):
            g = g + p_ref[s].astype(F32)
        delta, mn, vn = _adamw_math(w_ref[...], g, m_ref[...], v_ref[...])
        g_ref[...] = g
        d_ref[...] = delta
        mo_ref[...] = mn
        vo_ref[...] = vn

    blk = pl.BlockSpec((tr, cols), lambda i: (i, 0))
    shp = jax.ShapeDtypeStruct((rows, cols), F32)
    return pl.pallas_call(
        body, name=name, grid=(rows // tr,),
        in_specs=[pl.BlockSpec((nparts, tr, cols), lambda i: (0, i, 0)), blk, blk, blk],
        out_specs=[blk, blk, blk, blk], out_shape=[shp, shp, shp, shp],
        compiler_params=_params(("parallel",)),
    )(parts, w, m, v)


def _me():
    return lax.axis_index("x"), lax.axis_index("y"), lax.axis_index("c")


def _flip(pos, k):
    x, y, c = pos
    kx, ky, kc = (k >> 2) & 1, (k >> 1) & 1, k & 1
    return (x ^ kx if kx else x, y ^ ky if ky else y, c ^ kc if kc else c)


def _logical(pos):
    return 4 * pos[0] + 2 * pos[1] + pos[2]


def _all_gather_two_level(shards):
    narr = len(shards)

    def body(*refs):
        x_refs, out_refs = refs[:narr], refs[narr:2 * narr]
        send_sems, recv_sems, local_sems = refs[2 * narr:]
        x, y, c = _me()
        me, sibling = (x, y, c), (x, y, 1 - c)
        chips = [(1 - x, y), (x, 1 - y), (1 - x, 1 - y)]

        def copy(a, k, block, to, src=None):
            slot = out_refs[a].at[_logical(block)]
            return pltpu.make_async_remote_copy(
                src_ref=slot if src is None else src, dst_ref=slot,
                send_sem=send_sems.at[a, k], recv_sem=recv_sems.at[a, k], device_id=to, device_id_type=MESH)

        mine = [pltpu.make_async_copy(x_refs[a], out_refs[a].at[_logical(me)], local_sems.at[a]) for a in range(narr)]
        for cp in mine:
            cp.start()
        first = []
        for a in range(narr):
            first.append(copy(a, 0, me, sibling, src=x_refs[a]))
            first += [copy(a, 1 + j, me, (*chip, c), src=x_refs[a]) for j, chip in enumerate(chips)]
        for cp in first:
            cp.start()
        passed = []
        for a in range(narr):
            for j, chip in enumerate(chips):
                copy(a, 1 + j, (*chip, c), me).wait_recv()
                fwd = copy(a, 4 + j, (*chip, c), sibling)
                fwd.start()
                passed.append(fwd)
        for a in range(narr):
            copy(a, 0, sibling, me).wait_recv()
            for j, chip in enumerate(chips):
                copy(a, 4 + j, (*chip, 1 - c), me).wait_recv()
        for cp in first + passed:
            cp.wait_send()
        for cp in mine:
            cp.wait()

    hbm = pl.BlockSpec(memory_space=pl.ANY)
    return pl.pallas_call(
        body, name="all_gather_big",
        out_shape=[jax.ShapeDtypeStruct((N_DEV,) + s.shape, s.dtype) for s in shards],
        in_specs=[hbm] * narr, out_specs=[hbm] * narr,
        scratch_shapes=[pltpu.SemaphoreType.DMA((narr, 7)), pltpu.SemaphoreType.DMA((narr, 7)),
                        pltpu.SemaphoreType.DMA((narr,))],
    )(*shards)


def _exchange_copies(kind, x_refs, out_refs, send_sems, recv_sems, local_sems):
    me = _me()
    mine = _logical(me)
    local, remote = [], []
    for a, (x_ref, out_ref) in enumerate(zip(x_refs, out_refs)):
        own = x_ref if kind == "gather" else x_ref.at[mine]
        local.append(pltpu.make_async_copy(own, out_ref.at[mine], local_sems.at[a]))
        for k in range(1, N_DEV):
            peer = _flip(me, k)
            src = x_ref if kind == "gather" else x_ref.at[_logical(peer)]
            remote.append(pltpu.make_async_remote_copy(
                src_ref=src, dst_ref=out_ref.at[mine], send_sem=send_sems.at[a, k - 1], recv_sem=recv_sems.at[a, k - 1],
                device_id=peer, device_id_type=MESH))
    return local, remote


def _exchange_start(kind, x_refs, out_refs, *sems):
    if not x_refs:
        return
    local, remote = _exchange_copies(kind, x_refs, out_refs, *sems)
    for cp in local + remote:
        cp.start()


def _exchange_wait(kind, x_refs, out_refs, *sems):
    if not x_refs:
        return
    local, remote = _exchange_copies(kind, x_refs, out_refs, *sems)
    for cp in remote:
        cp.wait_recv()
    for cp in remote:
        cp.wait_send()
    for cp in local:
        cp.wait()


def _exchange_shapes(kind, arrays):
    return [jax.ShapeDtypeStruct(((N_DEV,) if kind == "gather" else ()) + a.shape, a.dtype) for a in arrays]


def _exchange_scratch(narrays):
    if not narrays:
        return []
    return [pltpu.SemaphoreType.DMA((narrays, N_DEV - 1)), pltpu.SemaphoreType.DMA((narrays, N_DEV - 1)),
            pltpu.SemaphoreType.DMA((narrays,))]


def _exchange_rows(x_ref, buf_ref, send_sems, recv_sems):
    me = _me()
    buf_ref[_logical(me)] = x_ref[...]
    copies = []
    for k in range(1, N_DEV):
        peer = _flip(me, k)
        copies.append(pltpu.make_async_remote_copy(
            src_ref=x_ref, dst_ref=buf_ref.at[_logical(me)],
            send_sem=send_sems.at[k - 1], recv_sem=recv_sems.at[k - 1], device_id=peer, device_id_type=MESH))
    for cp in copies:
        cp.start()
    for cp in copies:
        cp.wait_recv()
    for cp in copies:
        cp.wait_send()


def _sum_slots(buf_ref):
    total = buf_ref[0]
    for s in range(1, N_DEV):
        total = total + buf_ref[s]
    return total


def _small_gather(x):
    def body(x_ref, o_ref, buf_ref, send_sems, recv_sems):
        _exchange_rows(x_ref, buf_ref, send_sems, recv_sems)
        o_ref[...] = _sum_slots(buf_ref)

    return pl.pallas_call(
        body, name="small_gather", out_shape=jax.ShapeDtypeStruct(x.shape, F32),
        in_specs=[pl.BlockSpec(memory_space=pltpu.VMEM)], out_specs=pl.BlockSpec(memory_space=pltpu.VMEM),
        scratch_shapes=[pltpu.VMEM((N_DEV,) + x.shape, F32), pltpu.SemaphoreType.DMA((7,)), pltpu.SemaphoreType.DMA((7,))],
    )(x)


def _small_reduce_adamw(g, w, m, v):
    def body(g_ref, w_ref, m_ref, v_ref, go_ref, d_ref, mo_ref, vo_ref, buf_ref, send_sems, recv_sems):
        _exchange_rows(g_ref, buf_ref, send_sems, recv_sems)
        gs = _sum_slots(buf_ref)
        delta, mn, vn = _adamw_math(w_ref[...], gs, m_ref[...], v_ref[...])
        go_ref[...] = gs
        d_ref[...] = delta
        mo_ref[...] = mn
        vo_ref[...] = vn

    vm = pl.BlockSpec(memory_space=pltpu.VMEM)
    shp = jax.ShapeDtypeStruct(g.shape, F32)
    return pl.pallas_call(
        body, name="small_reduce_adamw", out_shape=[shp, shp, shp, shp],
        in_specs=[vm, vm, vm, vm], out_specs=[vm, vm, vm, vm],
        scratch_shapes=[pltpu.VMEM((N_DEV,) + g.shape, F32), pltpu.SemaphoreType.DMA((7,)), pltpu.SemaphoreType.DMA((7,))],
    )(g, w, m, v)


def _pad_cols(a, width):
    return jnp.pad(a, ((0, 0), (0, width - a.shape[1])))


def _local_step(x, target, norm_mix_w, w_in_t, conv_w, conv_b, dt_bias, a_log, d_skip, ssd_norm_w, f_bias,
                late_shards, norm_mlp_w, norm_final_w):
    bsz, seq, _ = x.shape
    n = bsz * seq
    x2 = x.reshape(n, D_MODEL)
    t2 = target.reshape(n, D_MODEL)
    nfw = norm_final_w.reshape(1, D_MODEL)

    bias = _pad_cols(dt_bias, LANES)
    arow = _pad_cols(-jnp.exp(a_log), LANES)
    biast, acol = bias.reshape(LANES, 1), arow.reshape(LANES, 1)
    dfull = jnp.repeat(d_skip, HEAD_DIM, axis=1)
    fb = jnp.pad(f_bias, ((0, 0), (F_COL, LANES - 2 * F_COL)))

    wt_z, wt_xbc, wt_qkv = w_in_t[:ROW_XBC], w_in_t[ROW_XBC:ROW_DT], w_in_t[ROW_Q:ROW_F]
    wt_dtf = jnp.concatenate([w_in_t[ROW_DT:ROW_Q], w_in_t[ROW_F:], jnp.zeros((LANES - 2 * F_COL, D_MODEL), BF16)], axis=0)
    wt_q, wt_k, wt_v = wt_qkv[:D_MODEL], wt_qkv[D_MODEL:2 * D_MODEL], wt_qkv[2 * D_MODEL:]

    h1 = _rms_fwd(x2, norm_mix_w, "rms_fwd_mix")
    z = _mm([(h1, wt_z)], "inproj_z", F32, 1024, 1024, nt=True)
    xbc_raw = _mm([(h1, wt_xbc)], "inproj_xbc", F32, 512, 1536, nt=True)
    qkv = _mm([(h1, wt_qkv)], "inproj_qkv", BF16, 512, 3072, nt=True)
    dtf = _mm([(h1, wt_dtf)], "inproj_dtf", F32, 2048, LANES, nt=True)
    dtft = dtf.reshape(bsz, seq, LANES).transpose(0, 2, 1)

    xbc = _conv_fwd(xbc_raw, conv_w, conv_b, bsz, seq)
    y_pre, hall = _ssd_fwd(xbc, dtf, dtft, bias, biast, arow, acol, dfull, bsz, seq)

    ccol = _fox_prep(dtf, fb, bsz, seq)
    o_att, lse, w_out, w_up_t, w_down = _att_fwd(qkv, ccol, bsz, seq, late_shards)
    w_out, w_up_t, w_down = (w.reshape(-1, D_MODEL) for w in (w_out, w_up_t, w_down))

    ycat = _gnorm_fwd(y_pre, z, o_att, ssd_norm_w)
    h2, h2n = _mm([(ycat, w_out)], "outproj", F32, 512, 1024, res=x2, rms_fwd=norm_mlp_w)
    pre = _mm([(h2n, w_up_t)], "mlp_up", BF16, 512, 4096, nt=True)

    dh3, loss_row, g_nfw = _mlp_down_loss(pre, w_down, h2, t2, nfw)
    dpre, u = _mlp_down_bwd(dh3, w_down, pre)
    g_w_down = _mm_tn(u, dh3, "grad_w_down", 1024, 1024, 2048)
    g_w_up_t = _mm_tn(dpre, h2n, "grad_w_up", 1024, 1024, 2048)
    by_shard = lambda g: g.reshape(N_DEV, g.shape[0] // N_DEV, D_MODEL)
    dh2, g_nmlp = _mm([(dpre, w_up_t)], "mlp_up_bwd", F32, 512, 1024, res=dh3, rms_bwd=(h2, norm_mlp_w))

    g_w_out = _mm_tn(ycat, dh2, "grad_w_out", 1024, 1024, 2048)
    dycat = _mm([(dh2, w_out)], "outproj_bwd", F32, 512, 2048, nt=True)
    dy_pre, dz, g_ssdn = _gnorm_bwd(dycat, y_pre, z, ssd_norm_w)
    dq, dk, dv, dck, dcq, recv_down, recv_up_t, recv_out = _att_bwd(
        qkv, dycat, o_att, lse, ccol, bsz, seq, [by_shard(g_w_down), by_shard(g_w_up_t), by_shard(g_w_out)])

    dxbc, ddt, ssd_acc = _ssd_bwd(dy_pre, xbc, dtf, dtft, bias, biast, arow, acol, dfull, hall, bsz, seq)
    dxbc_raw, g_cw, g_cb = _conv_bwd(dxbc, xbc_raw, conv_w, conv_b, bsz, seq)

    def head_cols(rows):
        cols = rows[:, :, :2, :].reshape(bsz, SSD_HEADS, seq).transpose(0, 2, 1).reshape(n, SSD_HEADS)
        return jnp.pad(cols, ((0, 0), (F_COL, LANES - 2 * F_COL)))

    ddtf, g_fb = _fox_prep_bwd(head_cols(dck), head_cols(dcq), ddt, dtf, fb, bsz, seq)

    g_dtf = _mm_tn(ddtf, h1, "grad_w_in_dtf", LANES, 1024, 2048)
    g_w_in_t = jnp.concatenate([
        _mm_tn(dz, h1, "grad_w_in_z", 1024, 1024, 2048), _mm_tn(dxbc_raw, h1, "grad_w_in_xbc", 768, 1024, 2048),
        g_dtf[:F_COL], _mm_tn(dq, h1, "grad_w_in_q", 1024, 1024, 2048), _mm_tn(dk, h1, "grad_w_in_k", 1024, 1024, 2048),
        _mm_tn(dv, h1, "grad_w_in_v", 1024, 1024, 2048), g_dtf[F_COL:2 * F_COL]], axis=0)
    pieces = [(dz, wt_z), (dxbc_raw, wt_xbc), (dq, wt_q), (dk, wt_k), (dv, wt_v), (ddtf, wt_dtf)]
    dx, g_nmix, recv_in_t = _mm(pieces, "inproj_bwd", F32, 256, 1024, res=dh2, rms_bwd=(x2, norm_mix_w),
                                exchange=("scatter", [by_shard(g_w_in_t)]))

    small = dict(
        norm_mix_w=g_nmix, norm_mlp_w=g_nmlp, norm_final_w=g_nfw, ssd_norm_w=g_ssdn, conv_b=g_cb, conv_w=g_cw,
        dt_bias=ssd_acc[16:17, :SSD_HEADS], a_log=ssd_acc[0:1, :SSD_HEADS], d_skip=ssd_acc[8:9, :SSD_HEADS],
        f_bias=g_fb[0:1, F_COL:2 * F_COL])
    recv = dict(w_in_t=recv_in_t, w_out=recv_out, w_up_t=recv_up_t, w_down=recv_down)
    return loss_row[0, 0], dx.reshape(bsz, seq, D_MODEL), small, recv


SMALL_LAYOUT = (("norm_mix_w", 0, 1, 0, D_MODEL), ("norm_mlp_w", 1, 1, 0, D_MODEL), ("norm_final_w", 2, 1, 0, D_MODEL),
                ("ssd_norm_w", 3, 1, 0, D_MODEL), ("conv_b", 4, 1, 0, CONV_CH), ("conv_w", 5, CONV_K, 0, CONV_CH),
                ("dt_bias", 9, 1, 0, SSD_HEADS), ("a_log", 9, 1, LANES, SSD_HEADS), ("d_skip", 9, 1, 2 * LANES, SSD_HEADS),
                ("f_bias", 9, 1, 3 * LANES, SSD_HEADS))
SMALL_ROWS = 2 * SUBLANES


def _pack_small(vals):
    packed = jnp.zeros((SMALL_ROWS, SMALL_COLS), F32)
    for name, r0, nrows, c0, width in SMALL_LAYOUT:
        packed = packed.at[r0:r0 + nrows, c0:c0 + width].set(vals[name].reshape(nrows, width))
    return packed


def _unpack_small(packed, like):
    out = {}
    for name, r0, nrows, c0, width in SMALL_LAYOUT:
        out[name] = packed[r0:r0 + nrows, c0:c0 + width].reshape(like[name].shape)
    return out


def kernel(x, norm_mix_w, w_in, conv_w, conv_b, dt_bias, a_log, d_skip, ssd_norm_w, f_bias, w_out, norm_mlp_w, w_up, w_down, norm_final_w, loss_target, m_norm_mix_w, m_w_in, m_conv_w, m_conv_b, m_dt_bias, m_a_log, m_d_skip, m_ssd_norm_w, m_f_bias, m_w_out, m_norm_mlp_w, m_w_up, m_w_down, m_norm_final_w, v_norm_mix_w, v_w_in, v_conv_w, v_conv_b, v_dt_bias, v_a_log, v_d_skip, v_ssd_norm_w, v_f_bias, v_w_out, v_norm_mlp_w, v_w_up, v_w_down, v_norm_final_w):
    me = 4 * lax.axis_index("x") + 2 * lax.axis_index("y") + lax.axis_index("c")
    conv_shard_w = CONV_CH // N_DEV
    zero = jnp.zeros((), jnp.int32)

    def place_conv(shard):
        return lax.dynamic_update_slice(jnp.zeros((CONV_K, CONV_CH), F32), shard[0], (zero, me * conv_shard_w))

    (g_in_t,) = _all_gather_two_level([w_in[0].T.astype(BF16)])
    late_shards = [w_out[0].astype(BF16), w_up[0].T.astype(BF16), w_down[0].astype(BF16)]
    conv_full = _small_gather(jnp.pad(place_conv(conv_w), ((0, SUBLANES - CONV_K), (0, 0))))[:CONV_K]

    loss_local, grad_x, g_small, recv = _local_step(
        x, loss_target, norm_mix_w, g_in_t.reshape(IN_WIDTH, D_MODEL), conv_full, conv_b, dt_bias, a_log, d_skip,
        ssd_norm_w, f_bias, late_shards, norm_mlp_w, norm_final_w)
    loss = lax.psum(loss_local, MESH_AXES)

    grad_in = _sum_parts(recv["w_in_t"], "sum_w_in").T[None]
    grad_up = _sum_parts(recv["w_up_t"], "sum_w_up").T[None]
    big = {
        "w_in": _adamw_parts(grad_in, w_in[0], m_w_in[0], v_w_in[0], "adamw_w_in"),
        "w_out": _adamw_parts(recv["w_out"], w_out[0], m_w_out[0], v_w_out[0], "adamw_w_out"),
        "w_up": _adamw_parts(grad_up, w_up[0], m_w_up[0], v_w_up[0], "adamw_w_up"),
        "w_down": _adamw_parts(recv["w_down"], w_down[0], m_w_down[0], v_w_down[0], "adamw_w_down"),
    }

    like = dict(norm_mix_w=norm_mix_w, norm_mlp_w=norm_mlp_w, norm_final_w=norm_final_w, ssd_norm_w=ssd_norm_w,
                conv_b=conv_b, conv_w=jnp.zeros((1, CONV_K, CONV_CH), F32), dt_bias=dt_bias, a_log=a_log,
                d_skip=d_skip, f_bias=f_bias)
    w_small = dict(like, conv_w=place_conv(conv_w))
    m_small = dict(norm_mix_w=m_norm_mix_w, norm_mlp_w=m_norm_mlp_w, norm_final_w=m_norm_final_w,
                   ssd_norm_w=m_ssd_norm_w, conv_b=m_conv_b, conv_w=place_conv(m_conv_w), dt_bias=m_dt_bias,
                   a_log=m_a_log, d_skip=m_d_skip, f_bias=m_f_bias)
    v_small = dict(norm_mix_w=v_norm_mix_w, norm_mlp_w=v_norm_mlp_w, norm_final_w=v_norm_final_w,
                   ssd_norm_w=v_ssd_norm_w, conv_b=v_conv_b, conv_w=place_conv(v_conv_w), dt_bias=v_dt_bias,
                   a_log=v_a_log, d_skip=v_d_skip, f_bias=v_f_bias)
    small = _small_reduce_adamw(_pack_small(g_small), _pack_small(w_small), _pack_small(m_small), _pack_small(v_small))
    small = [_unpack_small(s, like) for s in small]
    for s in small:
        s["conv_w"] = lax.dynamic_slice(s["conv_w"], (zero, zero, me * conv_shard_w), (1, CONV_K, conv_shard_w))

    order = ["norm_mix_w", "w_in", "conv_w", "conv_b", "dt_bias", "a_log", "d_skip", "ssd_norm_w", "f_bias", "w_out",
             "norm_mlp_w", "w_up", "w_down", "norm_final_w"]
    outs = [loss, grad_x]
    for kind in range(4):
        for name in order:
            outs.append(big[name][kind][None] if name in big else small[kind][name])
    return tuple(outs)
```

```python
import jax
import jax.numpy as jnp
from jax import lax
from jax.experimental import pallas as pl
from jax.experimental.pallas import tpu as pltpu

F32, BF16 = jnp.float32, jnp.bfloat16

D_MODEL = 1024
SSD_HEADS = 16
HEAD_DIM = 64
SSD_STATE = 128
CHUNK = 128
CONV_CH = 1536
CONV_K = 4
D_FF = 4096
MIX_WIDTH = 2048
IN_WIDTH = 5664
NORM_EPS = 1e-5
ATT_SCALE = 0.125

LANES = 128
SUBLANES = 8
HEAD_PAIRS = SSD_HEADS // 2
NEG = -1e30
VMEM_LIMIT = 52 * 1024 * 1024

ROW_XBC, ROW_DT, ROW_Q, ROW_F = 1024, 2560, 2576, 5648
F_COL = SSD_HEADS

N_DEV = 8
MESH_AXES = ("x", "y", "c")
MESH = pl.DeviceIdType.MESH

ADAM_LR, ADAM_B1, ADAM_B2, ADAM_EPS, ADAM_WD, ADAM_STEP = 0.001, 0.9, 0.999, 1e-08, 0.01, 10
ADAM_C1 = 1.0 - ADAM_B1 ** ADAM_STEP
ADAM_C2 = 1.0 - ADAM_B2 ** ADAM_STEP

SMALL_COLS = CONV_CH


def _params(sem=None):
    return pltpu.CompilerParams(dimension_semantics=sem, vmem_limit_bytes=VMEM_LIMIT)


def _sigmoid(u):
    return 1.0 / (1.0 + jnp.exp(-u))


def _softplus(u):
    return jnp.maximum(u, 0.0) + jnp.log(1.0 + jnp.exp(-jnp.abs(u)))


def _log_sigmoid(u):
    return jnp.minimum(u, 0.0) - jnp.log(1.0 + jnp.exp(-jnp.abs(u)))


def _bdot(a, b):
    return jnp.dot(a.astype(BF16), b.astype(BF16), preferred_element_type=F32)


def _bdot_nt(a, b):
    return lax.dot_general(a.astype(BF16), b.astype(BF16), (((1,), (1,)), ((), ())), preferred_element_type=F32)


def _bdot_tn(a, b):
    return lax.dot_general(a.astype(BF16), b.astype(BF16), (((0,), (0,)), ((), ())), preferred_element_type=F32)


def _split3(x):
    x1 = x.astype(BF16)
    r1 = x - x1.astype(F32)
    x2 = r1.astype(BF16)
    return x1, x2, (r1 - x2.astype(F32)).astype(BF16)


def _dot_sel(x, sel):
    s = sel.astype(BF16)
    p1, p2, p3 = (jnp.dot(p, s, preferred_element_type=F32) for p in _split3(x))
    return p1 + p2 + p3


def _sel_dot(sel, x):
    s = sel.astype(BF16)
    p1, p2, p3 = (jnp.dot(s, p, preferred_element_type=F32) for p in _split3(x))
    return p1 + p2 + p3


def _iota(shape, dim):
    return lax.broadcasted_iota(jnp.int32, shape, dim)


def _head_expand():
    return (jnp.right_shift(_iota((LANES, D_MODEL), 1), 6) == _iota((LANES, D_MODEL), 0)).astype(F32)


def _head_reduce():
    return (jnp.right_shift(_iota((D_MODEL, LANES), 0), 6) == _iota((D_MODEL, LANES), 1)).astype(F32)


def _rms_fwd(x, w, name):
    n, d = x.shape
    tm = min(1024, n)

    def body(x_ref, w_ref, o_ref):
        xv = x_ref[...]
        r = lax.rsqrt(jnp.mean(xv * xv, axis=-1, keepdims=True) + NORM_EPS)
        o_ref[...] = (xv * r * w_ref[...]).astype(BF16)

    return pl.pallas_call(
        body, name=name, grid=(n // tm,),
        in_specs=[pl.BlockSpec((tm, d), lambda i: (i, 0)), pl.BlockSpec((1, d), lambda i: (0, 0))],
        out_specs=pl.BlockSpec((tm, d), lambda i: (i, 0)),
        out_shape=jax.ShapeDtypeStruct((n, d), BF16),
        compiler_params=_params(("parallel",)),
    )(x, w)


def _mm(pairs, name, out_dtype, tm, tn, nt=False, res=None, exchange=None, rms_fwd=None, rms_bwd=None):
    m = pairs[0][0].shape[0]
    n = pairs[0][1].shape[0 if nt else 1]
    tm, tn = min(tm, m), min(tn, n)
    gm, gn = m // tm, n // tn
    npairs = len(pairs)
    kind, xs = (None, []) if exchange is None else exchange
    nx = len(xs)
    extra_in = [] if res is None else [res]
    if rms_bwd is not None:
        extra_in += list(rms_bwd)
    elif rms_fwd is not None:
        extra_in.append(rms_fwd)
    n_in = 2 * npairs + len(extra_in)
    n_out = 1 + (rms_fwd is not None or rms_bwd is not None)
    assert (rms_fwd is None and rms_bwd is None) or tn == n

    def body(*refs):
        x_refs, o_refs = refs[n_in:n_in + nx], refs[n_in + nx:n_in + nx + n_out]
        xo_refs, sems = refs[n_in + nx + n_out:n_in + 2 * nx + n_out], refs[n_in + 2 * nx + n_out:]
        extra = refs[2 * npairs:n_in]
        i, j = pl.program_id(0), pl.program_id(1)
        if nx:
            @pl.when((i == 0) & (j == 0))
            def _():
                _exchange_start(kind, x_refs, xo_refs, *sems)

        acc = None
        for p in range(npairs):
            a_v, b_v = refs[2 * p][...], refs[2 * p + 1][...]
            d = _bdot_nt(a_v, b_v) if nt else _bdot(a_v, b_v)
            acc = d if acc is None else acc + d
        if rms_bwd is not None:
            xv = extra[1][...]
            r = lax.rsqrt(jnp.mean(xv * xv, axis=-1, keepdims=True) + NORM_EPS)
            nrm = xv * r
            g = acc * extra[2][...]
            o_refs[0][...] = extra[0][...] + r * (g - nrm * jnp.mean(g * nrm, axis=-1, keepdims=True))

            @pl.when(i == 0)
            def _():
                o_refs[1][...] = jnp.zeros_like(o_refs[1])

            o_refs[1][...] += jnp.sum(acc * nrm, axis=0, keepdims=True)
        else:
            if res is not None:
                acc = extra[0][...] + acc
            o_refs[0][...] = acc.astype(o_refs[0].dtype)
            if rms_fwd is not None:
                r = lax.rsqrt(jnp.mean(acc * acc, axis=-1, keepdims=True) + NORM_EPS)
                o_refs[1][...] = (acc * r * extra[-1][...]).astype(BF16)
        if nx:
            @pl.when((i == gm - 1) & (j == gn - 1))
            def _():
                _exchange_wait(kind, x_refs, xo_refs, *sems)

    tile = pl.BlockSpec((tm, tn), lambda i, j: (i, j))
    row = pl.BlockSpec((1, tn), lambda i, j: (0, j))
    in_specs, args = [], []
    for a, b in pairs:
        k = a.shape[1]
        in_specs.append(pl.BlockSpec((tm, k), lambda i, j: (i, 0)))
        in_specs.append(pl.BlockSpec((tn, k), lambda i, j: (j, 0)) if nt else pl.BlockSpec((k, tn), lambda i, j: (0, j)))
        args += [a, b]
    in_specs += [row if e.shape[0] == 1 else tile for e in extra_in]
    out_specs, out_shape = [tile], [jax.ShapeDtypeStruct((m, n), out_dtype)]
    if rms_bwd is not None:
        out_specs.append(row)
        out_shape.append(jax.ShapeDtypeStruct((1, n), F32))
    elif rms_fwd is not None:
        out_specs.append(tile)
        out_shape.append(jax.ShapeDtypeStruct((m, n), BF16))
    hbm = pl.BlockSpec(memory_space=pl.ANY)
    sequential = nx or rms_bwd is not None
    outs = pl.pallas_call(
        body, name=name, grid=(gm, gn), in_specs=in_specs + [hbm] * nx,
        out_specs=out_specs + [hbm] * nx,
        out_shape=out_shape + _exchange_shapes(kind, xs),
        scratch_shapes=_exchange_scratch(nx),
        compiler_params=_params(("arbitrary", "arbitrary") if sequential else ("parallel", "parallel")),
    )(*args, *extra_in, *xs)
    return outs if len(outs) > 1 else outs[0]


def _mm_tn(a, b, name, tm, tn, tk):
    t, m = a.shape
    n = b.shape[1]
    tm, tn, tk = min(tm, m), min(tn, n), min(tk, t)
    nk = t // tk

    def body(a_ref, b_ref, o_ref, acc_ref):
        kk = pl.program_id(2)

        @pl.when(kk == 0)
        def _():
            acc_ref[...] = jnp.zeros_like(acc_ref)

        acc_ref[...] += _bdot_tn(a_ref[...], b_ref[...])

        @pl.when(kk == nk - 1)
        def _():
            o_ref[...] = acc_ref[...].astype(o_ref.dtype)

    return pl.pallas_call(
        body, name=name, grid=(m // tm, n // tn, nk),
        in_specs=[pl.BlockSpec((tk, tm), lambda i, j, kk: (kk, i)), pl.BlockSpec((tk, tn), lambda i, j, kk: (kk, j))],
        out_specs=pl.BlockSpec((tm, tn), lambda i, j, kk: (i, j)),
        out_shape=jax.ShapeDtypeStruct((m, n), BF16),
        scratch_shapes=[pltpu.VMEM((tm, tn), F32)],
        compiler_params=_params(("parallel", "parallel", "arbitrary")),
    )(a, b)


def _mlp_down_loss(pre, w_down, h2, target, w):
    m, k = pre.shape
    d = w_down.shape[1]
    tm = min(512, m)

    def body(p_ref, b_ref, r_ref, t_ref, w_ref, dh_ref, loss_ref, gw_ref):
        u = jnp.square(jnp.maximum(p_ref[...].astype(F32), 0.0))
        xv = r_ref[...] + _bdot(u, b_ref[...])
        r = lax.rsqrt(jnp.mean(xv * xv, axis=-1, keepdims=True) + NORM_EPS)
        nrm = xv * r
        wv = w_ref[...]
        err = nrm * wv - t_ref[...]
        part = 0.5 * jnp.sum(jnp.mean(err * err, axis=-1, keepdims=True), axis=0, keepdims=True)
        dy = err * (1.0 / d)
        g = dy * wv
        dh_ref[...] = r * (g - nrm * jnp.mean(g * nrm, axis=-1, keepdims=True))

        @pl.when(pl.program_id(0) == 0)
        def _():
            loss_ref[...] = jnp.zeros_like(loss_ref)
            gw_ref[...] = jnp.zeros_like(gw_ref)

        loss_ref[...] += jnp.broadcast_to(part, loss_ref.shape)
        gw_ref[...] += jnp.sum(dy * nrm, axis=0, keepdims=True)

    tok = pl.BlockSpec((tm, d), lambda i: (i, 0))
    row = pl.BlockSpec((1, d), lambda i: (0, 0))
    return pl.pallas_call(
        body, name="mlp_down_loss", grid=(m // tm,),
        in_specs=[pl.BlockSpec((tm, k), lambda i: (i, 0)), pl.BlockSpec((k, d), lambda i: (0, 0)), tok, tok, row],
        out_specs=[tok, pl.BlockSpec((1, LANES), lambda i: (0, 0)), row],
        out_shape=[jax.ShapeDtypeStruct((m, d), F32), jax.ShapeDtypeStruct((1, LANES), F32),
                   jax.ShapeDtypeStruct((1, d), F32)],
        compiler_params=_params(("arbitrary",)),
    )(pre, w_down, h2, target, w)


def _mlp_down_bwd(dh3, w_down, pre):
    m, k = dh3.shape
    n = w_down.shape[0]
    tm, tn = min(256, m), n

    def body(a_ref, b_ref, p_ref, dpre_ref, u_ref):
        r = jnp.maximum(p_ref[...].astype(F32), 0.0)
        du = _bdot_nt(a_ref[...], b_ref[...])
        dpre_ref[...] = (du * (2.0 * r)).astype(BF16)
        u_ref[...] = (r * r).astype(BF16)

    blk = pl.BlockSpec((tm, tn), lambda i, j: (i, j))
    return pl.pallas_call(
        body, name="mlp_down_bwd", grid=(m // tm, n // tn),
        in_specs=[pl.BlockSpec((tm, k), lambda i, j: (i, 0)), pl.BlockSpec((tn, k), lambda i, j: (j, 0)), blk],
        out_specs=[blk, blk],
        out_shape=[jax.ShapeDtypeStruct((m, n), BF16), jax.ShapeDtypeStruct((m, n), BF16)],
        compiler_params=_params(("parallel", "parallel")),
    )(dh3, w_down, pre)


CONV_TC = 512


def _conv_fwd(xbc, cw, cb, bsz, seq):
    tt = min(512, seq)
    nt, hb = seq // tt, tt // SUBLANES
    x3 = xbc.reshape(bsz, seq, CONV_CH)

    def body(xm_ref, xh_ref, w_ref, b_ref, o_ref):
        i = pl.program_id(2)
        x = xm_ref[0]
        halo = jnp.where(i > 0, xh_ref[0], 0.0)
        xx = jnp.concatenate([halo, x], axis=0)
        acc = x * w_ref[3:4, :] + b_ref[...]
        for s in range(1, CONV_K):
            acc = acc + pltpu.roll(xx, s, 0)[SUBLANES:, :] * w_ref[3 - s:4 - s, :]
        o_ref[0] = acc * _sigmoid(acc)

    out = pl.pallas_call(
        body, name="conv_fwd", grid=(CONV_CH // CONV_TC, bsz, nt),
        in_specs=[pl.BlockSpec((1, tt, CONV_TC), lambda c, b, i: (b, i, c)),
                  pl.BlockSpec((1, SUBLANES, CONV_TC), lambda c, b, i: (b, jnp.maximum(i * hb - 1, 0), c)),
                  pl.BlockSpec((CONV_K, CONV_TC), lambda c, b, i: (0, c)),
                  pl.BlockSpec((1, CONV_TC), lambda c, b, i: (0, c))],
        out_specs=pl.BlockSpec((1, tt, CONV_TC), lambda c, b, i: (b, i, c)),
        out_shape=jax.ShapeDtypeStruct((bsz, seq, CONV_CH), F32),
        compiler_params=_params(("parallel", "parallel", "parallel")),
    )(x3, x3, cw, cb)
    return out.reshape(bsz * seq, CONV_CH)


def _conv_bwd(da, xbc, cw, cb, bsz, seq):
    tt = min(512, seq)
    nt, hb = seq // tt, tt // SUBLANES
    x3 = xbc.reshape(bsz, seq, CONV_CH)
    da3 = da.reshape(bsz, seq, CONV_CH)
    n2 = tt + SUBLANES

    def body(dam_ref, daa_ref, xm_ref, xb_ref, xa_ref, w_ref, b_ref, du_ref, gw_ref, gb_ref):
        bi, i = pl.program_id(1), pl.program_id(2)
        before = jnp.where(i > 0, xb_ref[0], 0.0)
        after = jnp.where(i < nt - 1, xa_ref[0], 0.0)
        xx = jnp.concatenate([before, xm_ref[0], after], axis=0)
        rolled = [xx] + [pltpu.roll(xx, s, 0) for s in range(1, CONV_K)]
        pre = b_ref[...]
        for s in range(CONV_K):
            pre = pre + rolled[s][SUBLANES:, :] * w_ref[3 - s:4 - s, :]
        da_ext = jnp.concatenate([dam_ref[0], jnp.where(i < nt - 1, daa_ref[0], 0.0)], axis=0)
        sg = _sigmoid(pre)
        dpre = da_ext * sg * (1.0 + pre * (1.0 - sg))
        du = dpre[:tt, :] * w_ref[3:4, :]
        for s in range(1, CONV_K):
            du = du + pltpu.roll(dpre, n2 - s, 0)[:tt, :] * w_ref[3 - s:4 - s, :]
        du_ref[0] = du.astype(BF16)
        dpm = dpre[:tt, :]
        gws = [jnp.sum(dpm * rolled[3 - kk][SUBLANES:SUBLANES + tt, :], axis=0, keepdims=True) for kk in range(CONV_K)]

        @pl.when((bi == 0) & (i == 0))
        def _():
            gw_ref[...] = jnp.zeros_like(gw_ref)
            gb_ref[...] = jnp.zeros_like(gb_ref)

        gw_ref[...] += jnp.concatenate(gws, axis=0)
        gb_ref[...] += jnp.sum(dpm, axis=0, keepdims=True)

    main = pl.BlockSpec((1, tt, CONV_TC), lambda c, b, i: (b, i, c))
    prev = pl.BlockSpec((1, SUBLANES, CONV_TC), lambda c, b, i: (b, jnp.maximum(i * hb - 1, 0), c))
    nxt = pl.BlockSpec((1, SUBLANES, CONV_TC), lambda c, b, i: (b, jnp.minimum((i + 1) * hb, seq // SUBLANES - 1), c))
    du, gw, gb = pl.pallas_call(
        body, name="conv_bwd", grid=(CONV_CH // CONV_TC, bsz, nt),
        in_specs=[main, nxt, main, prev, nxt,
                  pl.BlockSpec((CONV_K, CONV_TC), lambda c, b, i: (0, c)),
                  pl.BlockSpec((1, CONV_TC), lambda c, b, i: (0, c))],
        out_specs=[main, pl.BlockSpec((CONV_K, CONV_TC), lambda c, b, i: (0, c)),
                   pl.BlockSpec((1, CONV_TC), lambda c, b, i: (0, c))],
        out_shape=[jax.ShapeDtypeStruct((bsz, seq, CONV_CH), BF16), jax.ShapeDtypeStruct((CONV_K, CONV_CH), F32),
                   jax.ShapeDtypeStruct((1, CONV_CH), F32)],
        compiler_params=_params(("parallel", "arbitrary", "arbitrary")),
    )(da3, da3, x3, x3, x3, cw, cb)
    return du.reshape(bsz * seq, CONV_CH), gw, gb


def _ssd_constants():
    tri = (_iota((CHUNK, CHUNK), 1) <= _iota((CHUNK, CHUNK), 0)).astype(BF16)
    return tri, tri.T, _head_expand().astype(BF16), _head_reduce().astype(BF16)


def _ssd_prologue(dtf_ref, dtft_ref, bias_ref, biast_ref, arow_ref, acol_ref, tri_ref, triu_ref, expand_ref,
                  dtfull_scr, acfull_scr, acr_scr):
    dtc = _softplus(dtf_ref[0] + bias_ref[...])
    a = dtc * arow_ref[...]
    acum = _sel_dot(tri_ref[...], a)
    expand = expand_ref[...]
    dtfull_scr[...] = _dot_sel(dtc, expand)
    acfull_scr[...] = _dot_sel(acum, expand)
    dtr = _softplus(dtft_ref[0] + biast_ref[...])
    acr_scr[...] = _dot_sel(dtr * acol_ref[...], triu_ref[...])
    return dtc, acum


def _decay_matrix(acum, acr_scr, h):
    lane = _iota((CHUNK, LANES), 1)
    ac_col = jnp.sum(jnp.where(lane == h, acum, 0.0), axis=1, keepdims=True)
    ac_row = acr_scr[h:h + 1, :]
    causal = _iota((CHUNK, CHUNK), 1) <= _iota((CHUNK, CHUNK), 0)
    return jnp.exp(jnp.where(causal, ac_col - ac_row, NEG))


def _ssd_fwd(xbc, dtf, dtft, bias, biast, arow, acol, dfull, bsz, seq):
    nc = seq // CHUNK
    x3 = xbc.reshape(bsz, seq, CONV_CH)
    tri, triu, expand, _ = _ssd_constants()

    def body(xbc_ref, dtf_ref, dtft_ref, bias_ref, biast_ref, arow_ref, acol_ref, dfull_ref,
             tri_ref, triu_ref, expand_ref, y_ref, hall_ref, h_scr, dtfull_scr, acfull_scr, acr_scr):
        @pl.when(pl.program_id(1) == 0)
        def _():
            h_scr[...] = jnp.zeros_like(h_scr)

        _, acum = _ssd_prologue(dtf_ref, dtft_ref, bias_ref, biast_ref, arow_ref, acol_ref,
                                tri_ref, triu_ref, expand_ref, dtfull_scr, acfull_scr, acr_scr)
        lane = _iota((CHUNK, LANES), 1)
        for g in range(2):
            bg = xbc_ref[0, :, D_MODEL + LANES * g:D_MODEL + LANES * (g + 1)]
            cg = xbc_ref[0, :, D_MODEL + 2 * LANES + LANES * g:D_MODEL + 2 * LANES + LANES * (g + 1)]
            cg_bf = cg.astype(BF16)
            gmat = _bdot_nt(cg_bf, bg)
            bgt_bf = bg.T.astype(BF16)
            for j in range(4):
                p = 4 * g + j
                sl = slice(LANES * p, LANES * (p + 1))
                xs = xbc_ref[0, :, sl]
                ac = acfull_scr[:, sl]
                acl = acfull_scr[CHUNK - 1:CHUNK, sl]
                xdt = xs * dtfull_scr[:, sl]
                xdt_bf = xdt.astype(BF16)
                hp = h_scr[p]
                hall_ref[0, 0, p] = hp
                yo = _bdot(cg_bf, hp) * jnp.exp(ac)
                yd = []
                for hh in range(2):
                    mmat = gmat * _decay_matrix(acum, acr_scr, 2 * p + hh)
                    yd.append(_bdot(mmat, xdt_bf))
                y_ref[0, :, sl] = jnp.where(lane < HEAD_DIM, yd[0], yd[1]) + yo + dfull_ref[:, sl] * xs
                h_scr[p] = hp * jnp.exp(acl) + _bdot(bgt_bf, xdt * jnp.exp(acl - ac))

    row = lambda w: pl.BlockSpec((1, w), lambda b, c: (0, 0))
    whole = lambda a: pl.BlockSpec(a.shape, lambda b, c: (0, 0))
    y, hall = pl.pallas_call(
        body, name="ssd_fwd", grid=(bsz, nc),
        in_specs=[pl.BlockSpec((1, CHUNK, CONV_CH), lambda b, c: (b, c, 0)),
                  pl.BlockSpec((1, CHUNK, LANES), lambda b, c: (b, c, 0)),
                  pl.BlockSpec((1, LANES, CHUNK), lambda b, c: (b, 0, c)),
                  row(LANES), pl.BlockSpec((LANES, 1), lambda b, c: (0, 0)),
                  row(LANES), pl.BlockSpec((LANES, 1), lambda b, c: (0, 0)), row(D_MODEL),
                  whole(tri), whole(triu), whole(expand)],
        out_specs=[pl.BlockSpec((1, CHUNK, D_MODEL), lambda b, c: (b, c, 0)),
                   pl.BlockSpec((1, 1, HEAD_PAIRS, SSD_STATE, LANES), lambda b, c: (b, c, 0, 0, 0))],
        out_shape=[jax.ShapeDtypeStruct((bsz, seq, D_MODEL), F32),
                   jax.ShapeDtypeStruct((bsz, nc, HEAD_PAIRS, SSD_STATE, LANES), F32)],
        scratch_shapes=[pltpu.VMEM((HEAD_PAIRS, SSD_STATE, LANES), F32), pltpu.VMEM((CHUNK, D_MODEL), F32),
                        pltpu.VMEM((CHUNK, D_MODEL), F32), pltpu.VMEM((LANES, CHUNK), F32)],
        compiler_params=_params(("parallel", "arbitrary")),
    )(x3, dtf.reshape(bsz, seq, LANES), dtft, bias, biast, arow, acol, dfull, tri, triu, expand)
    return y.reshape(bsz * seq, D_MODEL), hall


def _ssd_bwd(dy, xbc, dtf, dtft, bias, biast, arow, acol, dfull, hall, bsz, seq):
    nc = seq // CHUNK
    x3 = xbc.reshape(bsz, seq, CONV_CH)
    dy3 = dy.reshape(bsz, seq, D_MODEL)
    consts = _ssd_constants()

    def body(dy_ref, xbc_ref, dtf_ref, dtft_ref, bias_ref, biast_ref, arow_ref, acol_ref, dfull_ref, hall_ref,
             tri_ref, triu_ref, expand_ref, reduce_ref,
             dx_ref, ddt_ref, acc_ref, dh_scr, dtfull_scr, acfull_scr, acr_scr, csum_scr, dacf_scr, ddtf_scr, ddl_scr):
        first = (pl.program_id(0) == 0) & (pl.program_id(1) == 0)

        @pl.when(first)
        def _():
            acc_ref[...] = jnp.zeros_like(acc_ref)

        @pl.when(pl.program_id(1) == 0)
        def _():
            dh_scr[...] = jnp.zeros_like(dh_scr)

        dtc, acum = _ssd_prologue(dtf_ref, dtft_ref, bias_ref, biast_ref, arow_ref, acol_ref,
                                  tri_ref, triu_ref, expand_ref, dtfull_scr, acfull_scr, acr_scr)
        csum_scr[...] = jnp.zeros_like(csum_scr)
        lane = _iota((CHUNK, LANES), 1)
        last_row = _iota((CHUNK, LANES), 0) == CHUNK - 1
        rs16 = jnp.zeros((CHUNK, LANES), F32)
        for g in range(2):
            bsl = slice(D_MODEL + LANES * g, D_MODEL + LANES * (g + 1))
            csl = slice(D_MODEL + 2 * LANES + LANES * g, D_MODEL + 2 * LANES + LANES * (g + 1))
            bg_bf = xbc_ref[0, :, bsl].astype(BF16)
            cg = xbc_ref[0, :, csl]
            cg_bf = cg.astype(BF16)
            cgt_bf = cg.T.astype(BF16)
            gmat = _bdot_nt(cg_bf, bg_bf)
            dg = jnp.zeros((CHUNK, CHUNK), F32)
            dbg = jnp.zeros((CHUNK, SSD_STATE), F32)
            dcg = jnp.zeros((CHUNK, SSD_STATE), F32)
            for j in range(4):
                p = 4 * g + j
                sl = slice(LANES * p, LANES * (p + 1))
                xs = xbc_ref[0, :, sl]
                dt = dtfull_scr[:, sl]
                ac = acfull_scr[:, sl]
                acl = acfull_scr[CHUNK - 1:CHUNK, sl]
                dyp = dy_ref[0, :, sl]
                xdt = xs * dt
                xdt_bf = xdt.astype(BF16)
                e = jnp.exp(ac)
                dsd = jnp.exp(acl - ac)
                cd = jnp.exp(acl)
                xds_bf = (xdt * dsd).astype(BF16)
                hp = hall_ref[0, 0, p]
                hp_bf = hp.astype(BF16)
                dhn = dh_scr[p]
                dhn_bf = dhn.astype(BF16)
                yo = _bdot(cg_bf, hp_bf) * e
                dw_bf = (dyp * e).astype(BF16)
                dcg = dcg + _bdot_nt(dw_bf, hp_bf)
                dhin = _bdot(cgt_bf, dw_bf)
                dac = dyp * yo
                dacl = jnp.sum(dhn * hp, axis=0, keepdims=True) * cd
                dxds = _bdot(bg_bf, dhn_bf)
                dbg = dbg + _bdot_nt(xds_bf, dhn_bf)
                dxdt = dxds * dsd
                tdec = dxds * xdt * dsd
                dacl = dacl + jnp.sum(tdec, axis=0, keepdims=True)
                dac = dac - tdec
                dh_scr[p] = dhn * cd + dhin
                for hh in range(2):
                    h = 2 * p + hh
                    lm = (lane < HEAD_DIM) if hh == 0 else (lane >= HEAD_DIM)
                    dec = _decay_matrix(acum, acr_scr, h)
                    mmat = gmat * dec
                    dyh_bf = jnp.where(lm, dyp, 0.0).astype(BF16)
                    dm = _bdot_nt(dyh_bf, xdt_bf)
                    dxdt = dxdt + _bdot(mmat.T, dyh_bf)
                    dg = dg + dm * dec
                    q = dm * mmat
                    rs16 = rs16 + jnp.where(lane == h, jnp.sum(q, axis=1, keepdims=True), 0.0)
                    csum_scr[h:h + 1, :] = jnp.sum(q, axis=0, keepdims=True)
                dx_ref[0, :, sl] = dfull_ref[:, sl] * dyp + dxdt * dt
                dacf_scr[:, sl] = dac + jnp.where(last_row, dacl, 0.0)
                ddtf_scr[:, sl] = dxdt * xs
                ddl_scr[:, sl] = jnp.broadcast_to(jnp.sum(dyp * xs, axis=0, keepdims=True), (SUBLANES, LANES))
            dg_bf = dg.astype(BF16)
            dx_ref[0, :, bsl] = dbg + _bdot(dg.T, cg_bf)
            dx_ref[0, :, csl] = dcg + _bdot(dg_bf, bg_bf)
        reduce, triu = reduce_ref[...], triu_ref[...]
        dac16 = _dot_sel(dacf_scr[...], reduce) + rs16 - csum_scr[...].T
        da16 = _sel_dot(triu, dac16)
        ddt16 = da16 * arow_ref[...] + _dot_sel(ddtf_scr[...], reduce)
        ddtraw = ddt16 * _sigmoid(dtf_ref[0] + bias_ref[...])
        ddt_ref[0] = ddtraw
        acc_ref[0:8, :] += jnp.broadcast_to(jnp.sum(da16 * dtc * arow_ref[...], axis=0, keepdims=True), (SUBLANES, LANES))
        acc_ref[8:16, :] += _dot_sel(ddl_scr[...], reduce)
        acc_ref[16:24, :] += jnp.broadcast_to(jnp.sum(ddtraw, axis=0, keepdims=True), (SUBLANES, LANES))

    rev = lambda b, c: (b, nc - 1 - c, 0)
    row = lambda w: pl.BlockSpec((1, w), lambda b, c: (0, 0))
    dx, ddt, acc = pl.pallas_call(
        body, name="ssd_bwd", grid=(bsz, nc),
        in_specs=[pl.BlockSpec((1, CHUNK, D_MODEL), rev), pl.BlockSpec((1, CHUNK, CONV_CH), rev),
                  pl.BlockSpec((1, CHUNK, LANES), rev),
                  pl.BlockSpec((1, LANES, CHUNK), lambda b, c: (b, 0, nc - 1 - c)),
                  row(LANES), pl.BlockSpec((LANES, 1), lambda b, c: (0, 0)),
                  row(LANES), pl.BlockSpec((LANES, 1), lambda b, c: (0, 0)), row(D_MODEL),
                  pl.BlockSpec((1, 1, HEAD_PAIRS, SSD_STATE, LANES), lambda b, c: (b, nc - 1 - c, 0, 0, 0))]
        + [pl.BlockSpec(a.shape, lambda b, c: (0, 0)) for a in consts],
        out_specs=[pl.BlockSpec((1, CHUNK, CONV_CH), rev), pl.BlockSpec((1, CHUNK, LANES), rev),
                   pl.BlockSpec((3 * SUBLANES, LANES), lambda b, c: (0, 0))],
        out_shape=[jax.ShapeDtypeStruct((bsz, seq, CONV_CH), F32), jax.ShapeDtypeStruct((bsz, seq, LANES), F32),
                   jax.ShapeDtypeStruct((3 * SUBLANES, LANES), F32)],
        scratch_shapes=[pltpu.VMEM((HEAD_PAIRS, SSD_STATE, LANES), F32), pltpu.VMEM((CHUNK, D_MODEL), F32),
                        pltpu.VMEM((CHUNK, D_MODEL), F32), pltpu.VMEM((LANES, CHUNK), F32),
                        pltpu.VMEM((LANES, CHUNK), F32), pltpu.VMEM((CHUNK, D_MODEL), F32),
                        pltpu.VMEM((CHUNK, D_MODEL), F32), pltpu.VMEM((SUBLANES, D_MODEL), F32)],
        compiler_params=_params(("arbitrary", "arbitrary")),
    )(dy3, x3, dtf.reshape(bsz, seq, LANES), dtft, bias, biast, arow, acol, dfull, hall, *consts)
    return dx.reshape(bsz * seq, CONV_CH), ddt.reshape(bsz * seq, LANES), acc


GROUP_W = D_MODEL // 2


def _gnorm_fwd(y, z, o_att, w):
    n = y.shape[0]
    tm = min(1024, n)

    def body(y_ref, z_ref, o_ref, w_ref, out_ref):
        zv = z_ref[...]
        g = y_ref[...] * (zv * _sigmoid(zv))
        for gi in range(2):
            sl = slice(GROUP_W * gi, GROUP_W * (gi + 1))
            gs = g[:, sl]
            r = lax.rsqrt(jnp.mean(gs * gs, axis=-1, keepdims=True) + NORM_EPS)
            out_ref[:, sl] = (gs * r * w_ref[:, sl]).astype(BF16)
        out_ref[:, D_MODEL:] = o_ref[...].astype(BF16)

    tok = pl.BlockSpec((tm, D_MODEL), lambda i: (i, 0))
    return pl.pallas_call(
        body, name="gnorm_fwd", grid=(n // tm,),
        in_specs=[tok, tok, tok, pl.BlockSpec((1, D_MODEL), lambda i: (0, 0))],
        out_specs=pl.BlockSpec((tm, MIX_WIDTH), lambda i: (i, 0)),
        out_shape=jax.ShapeDtypeStruct((n, MIX_WIDTH), BF16),
        compiler_params=_params(("parallel",)),
    )(y, z, o_att, w)


def _gnorm_bwd(dycat, y, z, w):
    n = y.shape[0]
    tm = min(512, n)

    def body(dn_ref, y_ref, z_ref, w_ref, dy_ref, dz_ref, gw_ref):
        zv, yv = z_ref[...], y_ref[...]
        sz = _sigmoid(zv)
        sil = zv * sz
        g = yv * sil

        @pl.when(pl.program_id(0) == 0)
        def _():
            gw_ref[...] = jnp.zeros_like(gw_ref)

        for gi in range(2):
            sl = slice(GROUP_W * gi, GROUP_W * (gi + 1))
            gs = g[:, sl]
            r = lax.rsqrt(jnp.mean(gs * gs, axis=-1, keepdims=True) + NORM_EPS)
            nrm = gs * r
            dyn = dn_ref[:, sl]
            dn = dyn * w_ref[:, sl]
            dgs = r * (dn - nrm * jnp.mean(dn * nrm, axis=-1, keepdims=True))
            dy_ref[:, sl] = dgs * sil[:, sl]
            dz_ref[:, sl] = (dgs * yv[:, sl] * (sz[:, sl] * (1.0 + zv[:, sl] * (1.0 - sz[:, sl])))).astype(BF16)
            gw_ref[:, sl] += jnp.sum(dyn * nrm, axis=0, keepdims=True)

    tok = pl.BlockSpec((tm, D_MODEL), lambda i: (i, 0))
    row = pl.BlockSpec((1, D_MODEL), lambda i: (0, 0))
    return pl.pallas_call(
        body, name="gnorm_bwd", grid=(n // tm,),
        in_specs=[tok, tok, tok, row], out_specs=[tok, tok, row],
        out_shape=[jax.ShapeDtypeStruct((n, D_MODEL), F32), jax.ShapeDtypeStruct((n, D_MODEL), BF16),
                   jax.ShapeDtypeStruct((1, D_MODEL), F32)],
        compiler_params=_params(("arbitrary",)),
    )(dycat, y, z, w)


AUX_C = 3
AUX_ONE = 3


def _aux_base(hh):
    return HEAD_DIM * (1 - hh)


def _fox_prep(dtf, fb, bsz, seq):
    def body(x_ref, b_ref, aux_ref):
        tri = (_iota((CHUNK, CHUNK), 1) <= _iota((CHUNK, CHUNK), 0)).astype(F32)
        row, col = _iota((LANES, D_MODEL), 0), _iota((LANES, D_MODEL), 1)
        lane = jnp.bitwise_and(col, LANES - 1)
        other = (lane < HEAD_DIM).astype(jnp.int32)
        term = lane - HEAD_DIM * (1 - other)
        src = F_COL + 2 * jnp.right_shift(col, 7) + other
        place = [jnp.where((row == src) & (term == i), -1.0, 0.0).astype(BF16) for i in range(AUX_C)]
        lane1 = jnp.bitwise_and(_iota((1, D_MODEL), 1), LANES - 1)
        ones_lane = (lane1 - HEAD_DIM * (1 - (lane1 < HEAD_DIM).astype(jnp.int32)) == AUX_ONE).astype(F32)
        carry = jnp.zeros((1, LANES), F32)
        for j in range(seq // CHUNK):
            sl = slice(CHUNK * j, CHUNK * (j + 1))
            lf = _log_sigmoid(x_ref[sl, :] + b_ref[...])
            c = _sel_dot(tri, lf) + carry
            carry = carry + jnp.sum(lf, axis=0, keepdims=True)
            t1, t2, t3 = (jnp.dot(t, p, preferred_element_type=F32) for t, p in zip(_split3(c), place))
            aux_ref[sl, :] = (t1 + t2 + t3 + ones_lane).astype(BF16)

    return pl.pallas_call(
        body, name="fox_prep", grid=(bsz,),
        in_specs=[pl.BlockSpec((seq, LANES), lambda b: (b, 0)), pl.BlockSpec((1, LANES), lambda b: (0, 0))],
        out_specs=pl.BlockSpec((seq, D_MODEL), lambda b: (b, 0)),
        out_shape=jax.ShapeDtypeStruct((bsz * seq, D_MODEL), BF16),
        compiler_params=_params(("parallel",)),
    )(dtf, fb)


def _fox_prep_bwd(dck, dcq, ddt, dtf, fb, bsz, seq):
    nj = seq // CHUNK

    def body(dck_ref, dcq_ref, ddt_ref, x_ref, b_ref, out_ref, gb_ref):
        triu = (_iota((CHUNK, CHUNK), 0) <= _iota((CHUNK, CHUNK), 1)).astype(F32)
        is_f = (_iota((CHUNK, LANES), 1) >= F_COL) & (_iota((CHUNK, LANES), 1) < 2 * F_COL)
        carry = jnp.zeros((1, LANES), F32)
        total = jnp.zeros((1, LANES), F32)
        for j in range(nj - 1, -1, -1):
            sl = slice(CHUNK * j, CHUNK * (j + 1))
            dcv = dck_ref[sl, :] + dcq_ref[sl, :]
            dlf = _sel_dot(triu, dcv) + carry
            carry = carry + jnp.sum(dcv, axis=0, keepdims=True)
            df = jnp.where(is_f, dlf * _sigmoid(-(x_ref[sl, :] + b_ref[...])), 0.0)
            out_ref[sl, :] = (df + ddt_ref[sl, :]).astype(BF16)
            total = total + jnp.sum(df, axis=0, keepdims=True)

        @pl.when(pl.program_id(0) == 0)
        def _():
            gb_ref[...] = jnp.zeros_like(gb_ref)

        gb_ref[...] += jnp.broadcast_to(total, (SUBLANES, LANES))

    blk = pl.BlockSpec((seq, LANES), lambda b: (b, 0))
    return pl.pallas_call(
        body, name="fox_prep_bwd", grid=(bsz,),
        in_specs=[blk, blk, blk, blk, pl.BlockSpec((1, LANES), lambda b: (0, 0))],
        out_specs=[blk, pl.BlockSpec((SUBLANES, LANES), lambda b: (0, 0))],
        out_shape=[jax.ShapeDtypeStruct((bsz * seq, LANES), BF16), jax.ShapeDtypeStruct((SUBLANES, LANES), F32)],
        compiler_params=_params(("arbitrary",)),
    )(dck, dcq, ddt, dtf, fb)


ATT_BLOCK_FWD = 2048
ATT_BLOCK_KEYS = 256
ATT_BLOCK_BWD = 512


def _att_operands(q_ref, k_ref, aux_ref, hh):
    lane = _iota((1, LANES), 1)
    lm = (lane < HEAD_DIM) if hh == 0 else (lane >= HEAD_DIM)
    base = _aux_base(hh)
    zero = jnp.zeros((1, LANES), BF16)
    ka = jnp.where(lm, k_ref[...], jnp.where((lane >= base) & (lane <= base + AUX_ONE), aux_ref[...], zero))
    qa = jnp.where(lm, q_ref[...] * ATT_SCALE,
                   jnp.where((lane >= base) & (lane < base + AUX_C), jnp.ones((1, LANES), BF16), zero))
    return lm, base, qa, ka


def _att_fwd(qkv, ccol, bsz, seq, gather):
    bq, bk = min(ATT_BLOCK_FWD, seq), min(ATT_BLOCK_KEYS, seq)
    nq, ratio = seq // bq, bq // bk
    nx = len(gather)

    def body(*refs):
        q_ref, k_ref, v_ref, c_ref = refs[:4]
        x_refs = refs[4:4 + nx]
        o_ref, lse_ref = refs[4 + nx:6 + nx]
        xo_refs = refs[6 + nx:6 + 2 * nx]
        qa_scr, ka_scr, vt_scr = refs[6 + 2 * nx:9 + 2 * nx]
        sems = refs[9 + 2 * nx:]
        pair = pl.program_id(1)

        @pl.when((pl.program_id(0) == 0) & (pair == 0))
        def _():
            _exchange_start("gather", x_refs, xo_refs, *sems)

        lse_ref[...] = jnp.zeros_like(lse_ref)
        for hh in range(2):
            _, _, qa, ka = _att_operands(q_ref, k_ref, c_ref, hh)
            qa_scr[hh] = qa
            ka_scr[hh] = ka
        for t0 in range(0, seq, bq):
            vt_scr[:, t0:t0 + bq] = v_ref[t0:t0 + bq, :].T
        key_minus_query = _iota((bk, bq), 0) - _iota((bk, bq), 1)

        def q_step(i, carry0):
            qrows = pl.ds(pl.multiple_of(i * bq, bq), bq)

            def scores(j):
                krows = pl.ds(pl.multiple_of(j * bk, bk), bk)
                return tuple(_bdot_nt(ka_scr[hh, krows, :], qa_scr[hh, qrows, :]) for hh in range(2))

            def scores_tail(r):
                krows = pl.ds(pl.multiple_of((i * ratio + r) * bk, bk), bk)
                qsub = pl.ds(pl.multiple_of(i * bq + r * bk, bk), bq - r * bk)
                return tuple(_bdot_nt(ka_scr[hh, krows, :], qa_scr[hh, qsub, :]) for hh in range(2))

            def update(state, st_pair, j, masked):
                krows = pl.ds(pl.multiple_of(j * bk, bk), bk)
                out = []
                for hh in range(2):
                    m_all, l_all, acc_all = state[hh]
                    st = st_pair[hh]
                    width = st.shape[1]
                    lo = bq - width
                    m, l, acc = m_all[:, lo:], l_all[:, lo:], acc_all[:, lo:]
                    if masked:
                        st = jnp.where(key_minus_query[:, :width] <= 0, st, NEG)
                    mn = jnp.maximum(m, jnp.max(st, axis=0, keepdims=True))
                    alpha = jnp.exp(m - mn)
                    pt = jnp.exp(st - mn)
                    l = alpha * l + jnp.sum(pt, axis=0, keepdims=True)
                    vt = vt_scr[HEAD_DIM * hh:HEAD_DIM * (hh + 1), krows]
                    acc = alpha * acc + _bdot(vt, pt)
                    if lo:
                        mn, l, acc = (jnp.concatenate([old[:, :lo], new], axis=1)
                                      for old, new in ((m_all, mn), (l_all, l), (acc_all, acc)))
                    out.append((mn, l, acc))
                return tuple(out)

            def step(j, carry):
                state, s_cur = carry
                s_next = scores(j + 1)
                return update(state, s_cur, j, False), s_next

            def step_pair(t, carry):
                state, s_cur = carry
                s_second = scores(2 * t + 1)
                s_next = scores(2 * t + 2)
                state = update(state, s_cur, 2 * t, False)
                return update(state, s_second, 2 * t + 1, False), s_next

            one = (jnp.full((1, bq), NEG, F32), jnp.zeros((1, bq), F32), jnp.zeros((HEAD_DIM, bq), F32))
            first_diag = i * ratio
            if ratio % 2 == 0:
                state, s_cur = lax.fori_loop(0, first_diag // 2, step_pair, ((one, one), scores(0)))
            else:
                state, s_cur = lax.fori_loop(0, first_diag, step, ((one, one), scores(0)))
            for r in range(ratio):
                s_next = scores_tail(r + 1) if r + 1 < ratio else None
                state = update(state, s_cur, first_diag + r, True)
                s_cur = s_next
            (m0, l0, acc0), (m1, l1, acc1) = state
            ot = jnp.concatenate([acc0 * (1.0 / l0), acc1 * (1.0 / l1)], axis=0)
            o_ref[qrows, :] = ot.T
            lse_ref[0, 0, 0:1, qrows] = m0 + jnp.log(l0)
            lse_ref[0, 0, 1:2, qrows] = m1 + jnp.log(l1)
            return carry0

        lax.fori_loop(0, nq, q_step, 0)

        @pl.when((pl.program_id(0) == bsz - 1) & (pair == HEAD_PAIRS - 1))
        def _():
            _exchange_wait("gather", x_refs, xo_refs, *sems)

    col = lambda off: pl.BlockSpec((seq, LANES), lambda b, p: (b, off + p))
    head2 = pltpu.VMEM((2, seq, LANES), BF16)
    hbm = pl.BlockSpec(memory_space=pl.ANY)
    return pl.pallas_call(
        body, name="att_fwd", grid=(bsz, HEAD_PAIRS),
        in_specs=[col(0), col(HEAD_PAIRS), col(2 * HEAD_PAIRS), col(0)] + [hbm] * nx,
        out_specs=[pl.BlockSpec((seq, LANES), lambda b, p: (b, p)),
                   pl.BlockSpec((1, 1, SUBLANES, seq), lambda b, p: (b, p, 0, 0))] + [hbm] * nx,
        out_shape=[jax.ShapeDtypeStruct((bsz * seq, D_MODEL), F32),
                   jax.ShapeDtypeStruct((bsz, HEAD_PAIRS, SUBLANES, seq), F32)] + _exchange_shapes("gather", gather),
        scratch_shapes=[head2, head2, pltpu.VMEM((LANES, seq), BF16)] + _exchange_scratch(nx),
        compiler_params=_params(("arbitrary", "arbitrary")),
    )(qkv, qkv, qkv, ccol, *gather)


def _att_bwd(qkv, dycat, o, lse, ccol, bsz, seq, scatter):
    blk = bq = min(ATT_BLOCK_BWD, seq)
    nb = nq = seq // blk
    nx = len(scatter)

    def body(*refs):
        q_ref, k_ref, v_ref, do_ref, o_ref, lse_ref, c_ref = refs[:7]
        x_refs = refs[7:7 + nx]
        dq_ref, dk_ref, dv_ref, dck_ref, dcq_ref = refs[7 + nx:12 + nx]
        xo_refs = refs[12 + nx:12 + 2 * nx]
        (qa_scr, ka_scr, va_scr, doa_scr, kat_scr, qat_scr, dot_scr, d_scr, dqt_scr,
         dkt_scr) = refs[12 + 2 * nx:22 + 2 * nx]
        sems = refs[22 + 2 * nx:]
        pair = pl.program_id(1)

        @pl.when((pl.program_id(0) == 0) & (pair == 0))
        def _():
            _exchange_start("scatter", x_refs, xo_refs, *sems)

        query_minus_key = _iota((blk, bq), 1) - _iota((blk, bq), 0)
        lane_b = _iota((blk, LANES), 1)
        row_t = _iota((LANES, seq), 0)
        bases = []
        ones8 = jnp.ones((SUBLANES, LANES), F32)
        for hh in range(2):
            lm, base, qa, ka = _att_operands(q_ref, k_ref, c_ref, hh)
            bases.append(base)
            qa_scr[hh] = qa
            ka_scr[hh] = ka
            va_scr[hh] = jnp.where(lm, v_ref[...], jnp.zeros((seq, LANES), BF16))
            doa = jnp.where(lm, do_ref[...], 0.0).astype(BF16)
            doa_scr[hh] = doa
            for t0 in range(0, seq, blk):
                kat_scr[hh, :, t0:t0 + blk] = ka[t0:t0 + blk, :].T
                qat_scr[hh, :, t0:t0 + blk] = qa[t0:t0 + blk, :].T
            d1, d2, d3 = (_bdot_nt(ones8, term) for term in _split3(doa.astype(F32) * o_ref[...]))
            d_scr[hh] = d1 + d2 + d3
        for t0 in range(0, seq, blk):
            dot_scr[:, t0:t0 + blk] = do_ref[t0:t0 + blk, :].astype(BF16).T
        dqt_scr[...] = jnp.zeros_like(dqt_scr)
        dck_ref[...] = jnp.zeros_like(dck_ref)

        for j in range(nb):
            krows = pl.ds(j * blk, blk)
            dkt, dvt = [None, None], [None, None]
            for i in range(j, nq):
                rows = pl.ds(i * bq, bq)
                for hh in range(2):
                    st = _bdot_nt(ka_scr[hh, krows, :], qa_scr[hh, rows, :])
                    dpt = _bdot_nt(va_scr[hh, krows, :], doa_scr[hh, rows, :])
                    if i == j:
                        st = jnp.where(query_minus_key >= 0, st, NEG)
                    pt = jnp.exp(st - lse_ref[0, 0, hh:hh + 1, rows])
                    dst = (pt * (dpt - d_scr[hh, 0:1, rows])).astype(BF16)
                    dv_part = _bdot_nt(dot_scr[HEAD_DIM * hh:HEAD_DIM * (hh + 1), rows], pt)
                    dk_part = _bdot_nt(qat_scr[hh, :, rows], dst)
                    dvt[hh] = dv_part if dvt[hh] is None else dvt[hh] + dv_part
                    dkt[hh] = dk_part if dkt[hh] is None else dkt[hh] + dk_part
                    dqt_scr[hh, :, rows] += _bdot(kat_scr[hh, :, krows], dst)
            dk_ref[krows, :] = jnp.where(_iota((LANES, blk), 0) < HEAD_DIM, dkt[0], dkt[1]).T.astype(BF16)
            dv_ref[krows, :] = jnp.concatenate(dvt, axis=0).T.astype(BF16)
            for hh in range(2):
                dkt_scr[hh] = dkt[hh]
                dck_ref[0, 0, hh:hh + 1, krows] = -dkt_scr[hh, bases[hh]:bases[hh] + 1, :]
        for t0 in range(0, seq, blk):
            dq0, dq1 = dqt_scr[0, :, t0:t0 + blk].T, dqt_scr[1, :, t0:t0 + blk].T
            dq_ref[t0:t0 + blk, :] = (jnp.where(lane_b < HEAD_DIM, dq0, dq1) * ATT_SCALE).astype(BF16)
        dcq_ref[...] = jnp.zeros_like(dcq_ref)
        for hh in range(2):
            dcq_ref[0, 0, hh:hh + 1, :] = jnp.sum(jnp.where(row_t == bases[hh] + AUX_ONE, dqt_scr[hh], 0.0),
                                                   axis=0, keepdims=True)

        @pl.when((pl.program_id(0) == bsz - 1) & (pair == HEAD_PAIRS - 1))
        def _():
            _exchange_wait("scatter", x_refs, xo_refs, *sems)

    col = lambda off: pl.BlockSpec((seq, LANES), lambda b, p: (b, off + p))
    rowvec = pl.BlockSpec((1, 1, SUBLANES, seq), lambda b, p: (b, p, 0, 0))
    out = jax.ShapeDtypeStruct((bsz * seq, D_MODEL), BF16)
    rows_shape = jax.ShapeDtypeStruct((bsz, HEAD_PAIRS, SUBLANES, seq), F32)
    head2 = pltpu.VMEM((2, seq, LANES), BF16)
    hbm = pl.BlockSpec(memory_space=pl.ANY)
    return pl.pallas_call(
        body, name="att_bwd", grid=(bsz, HEAD_PAIRS),
        in_specs=[col(0), col(HEAD_PAIRS), col(2 * HEAD_PAIRS), col(HEAD_PAIRS), col(0), rowvec, col(0)] + [hbm] * nx,
        out_specs=[col(0), col(0), col(0), rowvec, rowvec] + [hbm] * nx,
        out_shape=[out, out, out, rows_shape, rows_shape] + _exchange_shapes("scatter", scatter),
        scratch_shapes=[head2, head2, head2, head2, pltpu.VMEM((2, LANES, seq), BF16),
                        pltpu.VMEM((2, LANES, seq), BF16), pltpu.VMEM((LANES, seq), BF16),
                        pltpu.VMEM((2, SUBLANES, seq), F32), pltpu.VMEM((2, LANES, seq), F32),
                        pltpu.VMEM((2, LANES, blk), F32)] + _exchange_scratch(nx),
        compiler_params=_params(("arbitrary", "arbitrary")),
    )(qkv, qkv, qkv, dycat, o, lse, ccol, *scatter)


def _adamw_math(w, g, m, v):
    m = ADAM_B1 * m + (1.0 - ADAM_B1) * g
    v = ADAM_B2 * v + (1.0 - ADAM_B2) * jnp.square(g)
    m_hat = m / ADAM_C1
    v_hat = v / ADAM_C2
    delta = -ADAM_LR * (m_hat / (jnp.sqrt(v_hat) + ADAM_EPS) + ADAM_WD * w)
    return delta, m, v


def _row_tile(rows):
    for tr in (256, 128, 64, 32, 16, 8):
        if rows % tr == 0:
            return tr
    return rows


def _sum_parts(parts, name):
    nparts, rows, cols = parts.shape
    tr = rows if rows % SUBLANES else _row_tile(rows)

    def body(p_ref, o_ref):
        g = p_ref[0].astype(F32)
        for s in range(1, nparts):
            g = g + p_ref[s].astype(F32)
        o_ref[...] = g

    return pl.pallas_call(
        body, name=name, grid=(rows // tr,),
        in_specs=[pl.BlockSpec((nparts, tr, cols), lambda i: (0, i, 0))],
        out_specs=pl.BlockSpec((tr, cols), lambda i: (i, 0)),
        out_shape=jax.ShapeDtypeStruct((rows, cols), F32),
        compiler_params=_params(("parallel",)),
    )(parts)


def _adamw_parts(parts, w, m, v, name):
    nparts, rows, cols = parts.shape
    tr = _row_tile(rows)

    def body(p_ref, w_ref, m_ref, v_ref, g_ref, d_ref, mo_ref, vo_ref):
        g = p_ref[0].astype(F32)
        for s in range(1, nparts):
            g = g + p_ref[s].astype(F32)
        delta, mn, vn = _adamw_math(w_ref[...], g, m_ref[...], v_ref[...])
        g_ref[...] = g
        d_ref[...] = delta
        mo_ref[...] = mn
        vo_ref[...] = vn

    blk = pl.BlockSpec((tr, cols), lambda i: (i, 0))
    shp = jax.ShapeDtypeStruct((rows, cols), F32)
    return pl.pallas_call(
        body, name=name, grid=(rows // tr,),
        in_specs=[pl.BlockSpec((nparts, tr, cols), lambda i: (0, i, 0)), blk, blk, blk],
        out_specs=[blk, blk, blk, blk], out_shape=[shp, shp, shp, shp],
        compiler_params=_params(("parallel",)),
    )(parts, w, m, v)


def _me():
    return lax.axis_index("x"), lax.axis_index("y"), lax.axis_index("c")


def _flip(pos, k):
    x, y, c = pos
    kx, ky, kc = (k >> 2) & 1, (k >> 1) & 1, k & 1
    return (x ^ kx if kx else x, y ^ ky if ky else y, c ^ kc if kc else c)


def _logical(pos):
    return 4 * pos[0] + 2 * pos[1] + pos[2]


def _all_gather_two_level(shards):
    narr = len(shards)

    def body(*refs):
        x_refs, out_refs = refs[:narr], refs[narr:2 * narr]
        send_sems, recv_sems, local_sems = refs[2 * narr:]
        x, y, c = _me()
        me, sibling = (x, y, c), (x, y, 1 - c)
        chips = [(1 - x, y), (x, 1 - y), (1 - x, 1 - y)]

        def copy(a, k, block, to, src=None):
            slot = out_refs[a].at[_logical(block)]
            return pltpu.make_async_remote_copy(
                src_ref=slot if src is None else src, dst_ref=slot,
                send_sem=send_sems.at[a, k], recv_sem=recv_sems.at[a, k], device_id=to, device_id_type=MESH)

        mine = [pltpu.make_async_copy(x_refs[a], out_refs[a].at[_logical(me)], local_sems.at[a]) for a in range(narr)]
        for cp in mine:
            cp.start()
        first = []
        for a in range(narr):
            first.append(copy(a, 0, me, sibling, src=x_refs[a]))
            first += [copy(a, 1 + j, me, (*chip, c), src=x_refs[a]) for j, chip in enumerate(chips)]
        for cp in first:
            cp.start()
        passed = []
        for a in range(narr):
            for j, chip in enumerate(chips):
                copy(a, 1 + j, (*chip, c), me).wait_recv()
                fwd = copy(a, 4 + j, (*chip, c), sibling)
                fwd.start()
                passed.append(fwd)
        for a in range(narr):
            copy(a, 0, sibling, me).wait_recv()
            for j, chip in enumerate(chips):
                copy(a, 4 + j, (*chip, 1 - c), me).wait_recv()
        for cp in first + passed:
            cp.wait_send()
        for cp in mine:
            cp.wait()

    hbm = pl.BlockSpec(memory_space=pl.ANY)
    return pl.pallas_call(
        body, name="all_gather_big",
        out_shape=[jax.ShapeDtypeStruct((N_DEV,) + s.shape, s.dtype) for s in shards],
        in_specs=[hbm] * narr, out_specs=[hbm] * narr,
        scratch_shapes=[pltpu.SemaphoreType.DMA((narr, 7)), pltpu.SemaphoreType.DMA((narr, 7)),
                        pltpu.SemaphoreType.DMA((narr,))],
    )(*shards)


def _exchange_copies(kind, x_refs, out_refs, send_sems, recv_sems, local_sems):
    me = _me()
    mine = _logical(me)
    local, remote = [], []
    for a, (x_ref, out_ref) in enumerate(zip(x_refs, out_refs)):
        own = x_ref if kind == "gather" else x_ref.at[mine]
        local.append(pltpu.make_async_copy(own, out_ref.at[mine], local_sems.at[a]))
        for k in range(1, N_DEV):
            peer = _flip(me, k)
            src = x_ref if kind == "gather" else x_ref.at[_logical(peer)]
            remote.append(pltpu.make_async_remote_copy(
                src_ref=src, dst_ref=out_ref.at[mine], send_sem=send_sems.at[a, k - 1], recv_sem=recv_sems.at[a, k - 1],
                device_id=peer, device_id_type=MESH))
    return local, remote


def _exchange_start(kind, x_refs, out_refs, *sems):
    if not x_refs:
        return
    local, remote = _exchange_copies(kind, x_refs, out_refs, *sems)
    for cp in local + remote:
        cp.start()


def _exchange_wait(kind, x_refs, out_refs, *sems):
    if not x_refs:
        return
    local, remote = _exchange_copies(kind, x_refs, out_refs, *sems)
    for cp in remote:
        cp.wait_recv()
    for cp in remote:
        cp.wait_send()
    for cp in local:
        cp.wait()


def _exchange_shapes(kind, arrays):
    return [jax.ShapeDtypeStruct(((N_DEV,) if kind == "gather" else ()) + a.shape, a.dtype) for a in arrays]


def _exchange_scratch(narrays):
    if not narrays:
        return []
    return [pltpu.SemaphoreType.DMA((narrays, N_DEV - 1)), pltpu.SemaphoreType.DMA((narrays, N_DEV - 1)),
            pltpu.SemaphoreType.DMA((narrays,))]


def _exchange_rows(x_ref, buf_ref, send_sems, recv_sems):
    me = _me()
    buf_ref[_logical(me)] = x_ref[...]
    copies = []
    for k in range(1, N_DEV):
        peer = _flip(me, k)
        copies.append(pltpu.make_async_remote_copy(
            src_ref=x_ref, dst_ref=buf_ref.at[_logical(me)],
            send_sem=send_sems.at[k - 1], recv_sem=recv_sems.at[k - 1], device_id=peer, device_id_type=MESH))
    for cp in copies:
        cp.start()
    for cp in copies:
        cp.wait_recv()
    for cp in copies:
        cp.wait_send()


def _sum_slots(buf_ref):
    total = buf_ref[0]
    for s in range(1, N_DEV):
        total = total + buf_ref[s]
    return total


def _small_gather(x):
    def body(x_ref, o_ref, buf_ref, send_sems, recv_sems):
        _exchange_rows(x_ref, buf_ref, send_sems, recv_sems)
        o_ref[...] = _sum_slots(buf_ref)

    return pl.pallas_call(
        body, name="small_gather", out_shape=jax.ShapeDtypeStruct(x.shape, F32),
        in_specs=[pl.BlockSpec(memory_space=pltpu.VMEM)], out_specs=pl.BlockSpec(memory_space=pltpu.VMEM),
        scratch_shapes=[pltpu.VMEM((N_DEV,) + x.shape, F32), pltpu.SemaphoreType.DMA((7,)), pltpu.SemaphoreType.DMA((7,))],
    )(x)


def _small_reduce_adamw(g, w, m, v):
    def body(g_ref, w_ref, m_ref, v_ref, go_ref, d_ref, mo_ref, vo_ref, buf_ref, send_sems, recv_sems):
        _exchange_rows(g_ref, buf_ref, send_sems, recv_sems)
        gs = _sum_slots(buf_ref)
        delta, mn, vn = _adamw_math(w_ref[...], gs, m_ref[...], v_ref[...])
        go_ref[...] = gs
        d_ref[...] = delta
        mo_ref[...] = mn
        vo_ref[...] = vn

    vm = pl.BlockSpec(memory_space=pltpu.VMEM)
    shp = jax.ShapeDtypeStruct(g.shape, F32)
    return pl.pallas_call(
        body, name="small_reduce_adamw", out_shape=[shp, shp, shp, shp],
        in_specs=[vm, vm, vm, vm], out_specs=[vm, vm, vm, vm],
        scratch_shapes=[pltpu.VMEM((N_DEV,) + g.shape, F32), pltpu.SemaphoreType.DMA((7,)), pltpu.SemaphoreType.DMA((7,))],
    )(g, w, m, v)


def _pad_cols(a, width):
    return jnp.pad(a, ((0, 0), (0, width - a.shape[1])))


def _local_step(x, target, norm_mix_w, w_in_t, conv_w, conv_b, dt_bias, a_log, d_skip, ssd_norm_w, f_bias,
                late_shards, norm_mlp_w, norm_final_w):
    bsz, seq, _ = x.shape
    n = bsz * seq
    x2 = x.reshape(n, D_MODEL)
    t2 = target.reshape(n, D_MODEL)
    nfw = norm_final_w.reshape(1, D_MODEL)

    bias = _pad_cols(dt_bias, LANES)
    arow = _pad_cols(-jnp.exp(a_log), LANES)
    biast, acol = bias.reshape(LANES, 1), arow.reshape(LANES, 1)
    dfull = jnp.repeat(d_skip, HEAD_DIM, axis=1)
    fb = jnp.pad(f_bias, ((0, 0), (F_COL, LANES - 2 * F_COL)))

    wt_z, wt_xbc, wt_qkv = w_in_t[:ROW_XBC], w_in_t[ROW_XBC:ROW_DT], w_in_t[ROW_Q:ROW_F]
    wt_dtf = jnp.concatenate([w_in_t[ROW_DT:ROW_Q], w_in_t[ROW_F:], jnp.zeros((LANES - 2 * F_COL, D_MODEL), BF16)], axis=0)
    wt_q, wt_k, wt_v = wt_qkv[:D_MODEL], wt_qkv[D_MODEL:2 * D_MODEL], wt_qkv[2 * D_MODEL:]

    h1 = _rms_fwd(x2, norm_mix_w, "rms_fwd_mix")
    z = _mm([(h1, wt_z)], "inproj_z", F32, 1024, 1024, nt=True)
    xbc_raw = _mm([(h1, wt_xbc)], "inproj_xbc", F32, 512, 1536, nt=True)
    qkv = _mm([(h1, wt_qkv)], "inproj_qkv", BF16, 512, 3072, nt=True)
    dtf = _mm([(h1, wt_dtf)], "inproj_dtf", F32, 2048, LANES, nt=True)
    dtft = dtf.reshape(bsz, seq, LANES).transpose(0, 2, 1)

    xbc = _conv_fwd(xbc_raw, conv_w, conv_b, bsz, seq)
    y_pre, hall = _ssd_fwd(xbc, dtf, dtft, bias, biast, arow, acol, dfull, bsz, seq)

    ccol = _fox_prep(dtf, fb, bsz, seq)
    o_att, lse, w_out, w_up_t, w_down = _att_fwd(qkv, ccol, bsz, seq, late_shards)
    w_out, w_up_t, w_down = (w.reshape(-1, D_MODEL) for w in (w_out, w_up_t, w_down))

    ycat = _gnorm_fwd(y_pre, z, o_att, ssd_norm_w)
    h2, h2n = _mm([(ycat, w_out)], "outproj", F32, 512, 1024, res=x2, rms_fwd=norm_mlp_w)
    pre = _mm([(h2n, w_up_t)], "mlp_up", BF16, 512, 4096, nt=True)

    dh3, loss_row, g_nfw = _mlp_down_loss(pre, w_down, h2, t2, nfw)
    dpre, u = _mlp_down_bwd(dh3, w_down, pre)
    g_w_down = _mm_tn(u, dh3, "grad_w_down", 1024, 1024, 2048)
    g_w_up_t = _mm_tn(dpre, h2n, "grad_w_up", 1024, 1024, 2048)
    by_shard = lambda g: g.reshape(N_DEV, g.shape[0] // N_DEV, D_MODEL)
    dh2, g_nmlp = _mm([(dpre, w_up_t)], "mlp_up_bwd", F32, 512, 1024, res=dh3, rms_bwd=(h2, norm_mlp_w))

    g_w_out = _mm_tn(ycat, dh2, "grad_w_out", 1024, 1024, 2048)
    dycat = _mm([(dh2, w_out)], "outproj_bwd", F32, 512, 2048, nt=True)
    dy_pre, dz, g_ssdn = _gnorm_bwd(dycat, y_pre, z, ssd_norm_w)
    dq, dk, dv, dck, dcq, recv_down, recv_up_t, recv_out = _att_bwd(
        qkv, dycat, o_att, lse, ccol, bsz, seq, [by_shard(g_w_down), by_shard(g_w_up_t), by_shard(g_w_out)])

    dxbc, ddt, ssd_acc = _ssd_bwd(dy_pre, xbc, dtf, dtft, bias, biast, arow, acol, dfull, hall, bsz, seq)
    dxbc_raw, g_cw, g_cb = _conv_bwd(dxbc, xbc_raw, conv_w, conv_b, bsz, seq)

    def head_cols(rows):
        cols = rows[:, :, :2, :].reshape(bsz, SSD_HEADS, seq).transpose(0, 2, 1).reshape(n, SSD_HEADS)
        return jnp.pad(cols, ((0, 0), (F_COL, LANES - 2 * F_COL)))

    ddtf, g_fb = _fox_prep_bwd(head_cols(dck), head_cols(dcq), ddt, dtf, fb, bsz, seq)

    g_dtf = _mm_tn(ddtf, h1, "grad_w_in_dtf", LANES, 1024, 2048)
    g_w_in_t = jnp.concatenate([
        _mm_tn(dz, h1, "grad_w_in_z", 1024, 1024, 2048), _mm_tn(dxbc_raw, h1, "grad_w_in_xbc", 768, 1024, 2048),
        g_dtf[:F_COL], _mm_tn(dq, h1, "grad_w_in_q", 1024, 1024, 2048), _mm_tn(dk, h1, "grad_w_in_k", 1024, 1024, 2048),
        _mm_tn(dv, h1, "grad_w_in_v", 1024, 1024, 2048), g_dtf[F_COL:2 * F_COL]], axis=0)
    pieces = [(dz, wt_z), (dxbc_raw, wt_xbc), (dq, wt_q), (dk, wt_k), (dv, wt_v), (ddtf, wt_dtf)]
    dx, g_nmix, recv_in_t = _mm(pieces, "inproj_bwd", F32, 256, 1024, res=dh2, rms_bwd=(x2, norm_mix_w),
                                exchange=("scatter", [by_shard(g_w_in_t)]))

    small = dict(
        norm_mix_w=g_nmix, norm_mlp_w=g_nmlp, norm_final_w=g_nfw, ssd_norm_w=g_ssdn, conv_b=g_cb, conv_w=g_cw,
        dt_bias=ssd_acc[16:17, :SSD_HEADS], a_log=ssd_acc[0:1, :SSD_HEADS], d_skip=ssd_acc[8:9, :SSD_HEADS],
        f_bias=g_fb[0:1, F_COL:2 * F_COL])
    recv = dict(w_in_t=recv_in_t, w_out=recv_out, w_up_t=recv_up_t, w_down=recv_down)
    return loss_row[0, 0], dx.reshape(bsz, seq, D_MODEL), small, recv


SMALL_LAYOUT = (("norm_mix_w", 0, 1, 0, D_MODEL), ("norm_mlp_w", 1, 1, 0, D_MODEL), ("norm_final_w", 2, 1, 0, D_MODEL),
                ("ssd_norm_w", 3, 1, 0, D_MODEL), ("conv_b", 4, 1, 0, CONV_CH), ("conv_w", 5, CONV_K, 0, CONV_CH),
                ("dt_bias", 9, 1, 0, SSD_HEADS), ("a_log", 9, 1, LANES, SSD_HEADS), ("d_skip", 9, 1, 2 * LANES, SSD_HEADS),
                ("f_bias", 9, 1, 3 * LANES, SSD_HEADS))
SMALL_ROWS = 2 * SUBLANES


def _pack_small(vals):
    packed = jnp.zeros((SMALL_ROWS, SMALL_COLS), F32)
    for name, r0, nrows, c0, width in SMALL_LAYOUT:
        packed = packed.at[r0:r0 + nrows, c0:c0 + width].set(vals[name].reshape(nrows, width))
    return packed


def _unpack_small(packed, like):
    out = {}
    for name, r0, nrows, c0, width in SMALL_LAYOUT:
        out[name] = packed[r0:r0 + nrows, c0:c0 + width].reshape(like[name].shape)
    return out


def kernel(x, norm_mix_w, w_in, conv_w, conv_b, dt_bias, a_log, d_skip, ssd_norm_w, f_bias, w_out, norm_mlp_w, w_up, w_down, norm_final_w, loss_target, m_norm_mix_w, m_w_in, m_conv_w, m_conv_b, m_dt_bias, m_a_log, m_d_skip, m_ssd_norm_w, m_f_bias, m_w_out, m_norm_mlp_w, m_w_up, m_w_down, m_norm_final_w, v_norm_mix_w, v_w_in, v_conv_w, v_conv_b, v_dt_bias, v_a_log, v_d_skip, v_ssd_norm_w, v_f_bias, v_w_out, v_norm_mlp_w, v_w_up, v_w_down, v_norm_final_w):
    me = 4 * lax.axis_index("x") + 2 * lax.axis_index("y") + lax.axis_index("c")
    conv_shard_w = CONV_CH // N_DEV
    zero = jnp.zeros((), jnp.int32)

    def place_conv(shard):
        return lax.dynamic_update_slice(jnp.zeros((CONV_K, CONV_CH), F32), shard[0], (zero, me * conv_shard_w))

    (g_in_t,) = _all_gather_two_level([w_in[0].T.astype(BF16)])
    late_shards = [w_out[0].astype(BF16), w_up[0].T.astype(BF16), w_down[0].astype(BF16)]
    conv_full = _small_gather(jnp.pad(place_conv(conv_w), ((0, SUBLANES - CONV_K), (0, 0))))[:CONV_K]

    loss_local, grad_x, g_small, recv = _local_step(
        x, loss_target, norm_mix_w, g_in_t.reshape(IN_WIDTH, D_MODEL), conv_full, conv_b, dt_bias, a_log, d_skip,
        ssd_norm_w, f_bias, late_shards, norm_mlp_w, norm_final_w)
    loss = lax.psum(loss_local, MESH_AXES)

    grad_in = _sum_parts(recv["w_in_t"], "sum_w_in").T[None]
    grad_up = _sum_parts(recv["w_up_t"], "sum_w_up").T[None]
    big = {
        "w_in": _adamw_parts(grad_in, w_in[0], m_w_in[0], v_w_in[0], "adamw_w_in"),
        "w_out": _adamw_parts(recv["w_out"], w_out[0], m_w_out[0], v_w_out[0], "adamw_w_out"),
        "w_up": _adamw_parts(grad_up, w_up[0], m_w_up[0], v_w_up[0], "adamw_w_up"),
        "w_down": _adamw_parts(recv["w_down"], w_down[0], m_w_down[0], v_w_down[0], "adamw_w_down"),
    }

    like = dict(norm_mix_w=norm_mix_w, norm_mlp_w=norm_mlp_w, norm_final_w=norm_final_w, ssd_norm_w=ssd_norm_w,
                conv_b=conv_b, conv_w=jnp.zeros((1, CONV_K, CONV_CH), F32), dt_bias=dt_bias, a_log=a_log,
                d_skip=d_skip, f_bias=f_bias)
    w_small = dict(like, conv_w=place_conv(conv_w))
    m_small = dict(norm_mix_w=m_norm_mix_w, norm_mlp_w=m_norm_mlp_w, norm_final_w=m_norm_final_w,
                   ssd_norm_w=m_ssd_norm_w, conv_b=m_conv_b, conv_w=place_conv(m_conv_w), dt_bias=m_dt_bias,
                   a_log=m_a_log, d_skip=m_d_skip, f_bias=m_f_bias)
    v_small = dict(norm_mix_w=v_norm_mix_w, norm_mlp_w=v_norm_mlp_w, norm_final_w=v_norm_final_w,
                   ssd_norm_w=v_ssd_norm_w, conv_b=v_conv_b, conv_w=place_conv(v_conv_w), dt_bias=v_dt_bias,
                   a_log=v_a_log, d_skip=v_d_skip, f_bias=v_f_bias)
    small = _small_reduce_adamw(_pack_small(g_small), _pack_small(w_small), _pack_small(m_small), _pack_small(v_small))
    small = [_unpack_small(s, like) for s in small]
    for s in small:
        s["conv_w"] = lax.dynamic_slice(s["conv_w"], (zero, zero, me * conv_shard_w), (1, CONV_K, conv_shard_w))

    order = ["norm_mix_w", "w_in", "conv_w", "conv_b", "dt_bias", "a_log", "d_skip", "ssd_norm_w", "f_bias", "w_out",
             "norm_mlp_w", "w_up", "w_down", "norm_final_w"]
    outs = [loss, grad_x]
    for kind in range(4):
        for name in order:
            outs.append(big[name][kind][None] if name in big else small[kind][name])
    return tuple(outs)
```

```python
import jax
import jax.numpy as jnp
from jax import lax
from jax.experimental import pallas as pl
from jax.experimental.pallas import tpu as pltpu

F32, BF16 = jnp.float32, jnp.bfloat16

D_MODEL = 1024
SSD_HEADS = 16
HEAD_DIM = 64
SSD_STATE = 128
CHUNK = 128
CONV_CH = 1536
CONV_K = 4
D_FF = 4096
MIX_WIDTH = 2048
IN_WIDTH = 5664
NORM_EPS = 1e-5
ATT_SCALE = 0.125

LANES = 128
SUBLANES = 8
HEAD_PAIRS = SSD_HEADS // 2
NEG = -1e30
VMEM_LIMIT = 52 * 1024 * 1024

ROW_XBC, ROW_DT, ROW_Q, ROW_F = 1024, 2560, 2576, 5648
F_COL = SSD_HEADS

N_DEV = 8
MESH_AXES = ("x", "y", "c")
MESH = pl.DeviceIdType.MESH

ADAM_LR, ADAM_B1, ADAM_B2, ADAM_EPS, ADAM_WD, ADAM_STEP = 0.001, 0.9, 0.999, 1e-08, 0.01, 10
ADAM_C1 = 1.0 - ADAM_B1 ** ADAM_STEP
ADAM_C2 = 1.0 - ADAM_B2 ** ADAM_STEP

SMALL_COLS = CONV_CH


def _params(sem=None):
    return pltpu.CompilerParams(dimension_semantics=sem, vmem_limit_bytes=VMEM_LIMIT)


def _sigmoid(u):
    return 1.0 / (1.0 + jnp.exp(-u))


def _softplus(u):
    return jnp.maximum(u, 0.0) + jnp.log(1.0 + jnp.exp(-jnp.abs(u)))


def _log_sigmoid(u):
    return jnp.minimum(u, 0.0) - jnp.log(1.0 + jnp.exp(-jnp.abs(u)))


def _bdot(a, b):
    return jnp.dot(a.astype(BF16), b.astype(BF16), preferred_element_type=F32)


def _bdot_nt(a, b):
    return lax.dot_general(a.astype(BF16), b.astype(BF16), (((1,), (1,)), ((), ())), preferred_element_type=F32)


def _bdot_tn(a, b):
    return lax.dot_general(a.astype(BF16), b.astype(BF16), (((0,), (0,)), ((), ())), preferred_element_type=F32)


def _split3(x):
    x1 = x.astype(BF16)
    r1 = x - x1.astype(F32)
    x2 = r1.astype(BF16)
    return x1, x2, (r1 - x2.astype(F32)).astype(BF16)


def _dot_sel(x, sel):
    s = sel.astype(BF16)
    p1, p2, p3 = (jnp.dot(p, s, preferred_element_type=F32) for p in _split3(x))
    return p1 + p2 + p3


def _sel_dot(sel, x):
    s = sel.astype(BF16)
    p1, p2, p3 = (jnp.dot(s, p, preferred_element_type=F32) for p in _split3(x))
    return p1 + p2 + p3


def _iota(shape, dim):
    return lax.broadcasted_iota(jnp.int32, shape, dim)


def _head_expand():
    return (jnp.right_shift(_iota((LANES, D_MODEL), 1), 6) == _iota((LANES, D_MODEL), 0)).astype(F32)


def _head_reduce():
    return (jnp.right_shift(_iota((D_MODEL, LANES), 0), 6) == _iota((D_MODEL, LANES), 1)).astype(F32)


def _rms_fwd(x, w, name):
    n, d = x.shape
    tm = min(1024, n)

    def body(x_ref, w_ref, o_ref):
        xv = x_ref[...]
        r = lax.rsqrt(jnp.mean(xv * xv, axis=-1, keepdims=True) + NORM_EPS)
        o_ref[...] = (xv * r * w_ref[...]).astype(BF16)

    return pl.pallas_call(
        body, name=name, grid=(n // tm,),
        in_specs=[pl.BlockSpec((tm, d), lambda i: (i, 0)), pl.BlockSpec((1, d), lambda i: (0, 0))],
        out_specs=pl.BlockSpec((tm, d), lambda i: (i, 0)),
        out_shape=jax.ShapeDtypeStruct((n, d), BF16),
        compiler_params=_params(("parallel",)),
    )(x, w)


def _mm(pairs, name, out_dtype, tm, tn, nt=False, res=None, exchange=None, rms_fwd=None, rms_bwd=None):
    m = pairs[0][0].shape[0]
    n = pairs[0][1].shape[0 if nt else 1]
    tm, tn = min(tm, m), min(tn, n)
    gm, gn = m // tm, n // tn
    npairs = len(pairs)
    kind, xs = (None, []) if exchange is None else exchange
    nx = len(xs)
    extra_in = [] if res is None else [res]
    if rms_bwd is not None:
        extra_in += list(rms_bwd)
    elif rms_fwd is not None:
        extra_in.append(rms_fwd)
    n_in = 2 * npairs + len(extra_in)
    n_out = 1 + (rms_fwd is not None or rms_bwd is not None)
    assert (rms_fwd is None and rms_bwd is None) or tn == n

    def body(*refs):
        x_refs, o_refs = refs[n_in:n_in + nx], refs[n_in + nx:n_in + nx + n_out]
        xo_refs, sems = refs[n_in + nx + n_out:n_in + 2 * nx + n_out], refs[n_in + 2 * nx + n_out:]
        extra = refs[2 * npairs:n_in]
        i, j = pl.program_id(0), pl.program_id(1)
        if nx:
            @pl.when((i == 0) & (j == 0))
            def _():
                _exchange_start(kind, x_refs, xo_refs, *sems)

        acc = None
        for p in range(npairs):
            a_v, b_v = refs[2 * p][...], refs[2 * p + 1][...]
            d = _bdot_nt(a_v, b_v) if nt else _bdot(a_v, b_v)
            acc = d if acc is None else acc + d
        if rms_bwd is not None:
            xv = extra[1][...]
            r = lax.rsqrt(jnp.mean(xv * xv, axis=-1, keepdims=True) + NORM_EPS)
            nrm = xv * r
            g = acc * extra[2][...]
            o_refs[0][...] = extra[0][...] + r * (g - nrm * jnp.mean(g * nrm, axis=-1, keepdims=True))

            @pl.when(i == 0)
            def _():
                o_refs[1][...] = jnp.zeros_like(o_refs[1])

            o_refs[1][...] += jnp.sum(acc * nrm, axis=0, keepdims=True)
        else:
            if res is not None:
                acc = extra[0][...] + acc
            o_refs[0][...] = acc.astype(o_refs[0].dtype)
            if rms_fwd is not None:
                r = lax.rsqrt(jnp.mean(acc * acc, axis=-1, keepdims=True) + NORM_EPS)
                o_refs[1][...] = (acc * r * extra[-1][...]).astype(BF16)
        if nx:
            @pl.when((i == gm - 1) & (j == gn - 1))
            def _():
                _exchange_wait(kind, x_refs, xo_refs, *sems)

    tile = pl.BlockSpec((tm, tn), lambda i, j: (i, j))
    row = pl.BlockSpec((1, tn), lambda i, j: (0, j))
    in_specs, args = [], []
    for a, b in pairs:
        k = a.shape[1]
        in_specs.append(pl.BlockSpec((tm, k), lambda i, j: (i, 0)))
        in_specs.append(pl.BlockSpec((tn, k), lambda i, j: (j, 0)) if nt else pl.BlockSpec((k, tn), lambda i, j: (0, j)))
        args += [a, b]
    in_specs += [row if e.shape[0] == 1 else tile for e in extra_in]
    out_specs, out_shape = [tile], [jax.ShapeDtypeStruct((m, n), out_dtype)]
    if rms_bwd is not None:
        out_specs.append(row)
        out_shape.append(jax.ShapeDtypeStruct((1, n), F32))
    elif rms_fwd is not None:
        out_specs.append(tile)
        out_shape.append(jax.ShapeDtypeStruct((m, n), BF16))
    hbm = pl.BlockSpec(memory_space=pl.ANY)
    sequential = nx or rms_bwd is not None
    outs = pl.pallas_call(
        body, name=name, grid=(gm, gn), in_specs=in_specs + [hbm] * nx,
        out_specs=out_specs + [hbm] * nx,
        out_shape=out_shape + _exchange_shapes(kind, xs),
        scratch_shapes=_exchange_scratch(nx),
        compiler_params=_params(("arbitrary", "arbitrary") if sequential else ("parallel", "parallel")),
    )(*args, *extra_in, *xs)
    return outs if len(outs) > 1 else outs[0]


def _mm_tn(a, b, name, tm, tn, tk):
    t, m = a.shape
    n = b.shape[1]
    tm, tn, tk = min(tm, m), min(tn, n), min(tk, t)
    nk = t // tk

    def body(a_ref, b_ref, o_ref, acc_ref):
        kk = pl.program_id(2)

        @pl.when(kk == 0)
        def _():
            acc_ref[...] = jnp.zeros_like(acc_ref)

        acc_ref[...] += _bdot_tn(a_ref[...], b_ref[...])

        @pl.when(kk == nk - 1)
        def _():
            o_ref[...] = acc_ref[...].astype(o_ref.dtype)

    return pl.pallas_call(
        body, name=name, grid=(m // tm, n // tn, nk),
        in_specs=[pl.BlockSpec((tk, tm), lambda i, j, kk: (kk, i)), pl.BlockSpec((tk, tn), lambda i, j, kk: (kk, j))],
        out_specs=pl.BlockSpec((tm, tn), lambda i, j, kk: (i, j)),
        out_shape=jax.ShapeDtypeStruct((m, n), BF16),
        scratch_shapes=[pltpu.VMEM((tm, tn), F32)],
        compiler_params=_params(("parallel", "parallel", "arbitrary")),
    )(a, b)


def _mlp_down_loss(pre, w_down, h2, target, w):
    m, k = pre.shape
    d = w_down.shape[1]
    tm = min(512, m)

    def body(p_ref, b_ref, r_ref, t_ref, w_ref, dh_ref, loss_ref, gw_ref):
        u = jnp.square(jnp.maximum(p_ref[...].astype(F32), 0.0))
        xv = r_ref[...] + _bdot(u, b_ref[...])
        r = lax.rsqrt(jnp.mean(xv * xv, axis=-1, keepdims=True) + NORM_EPS)
        nrm = xv * r
        wv = w_ref[...]
        err = nrm * wv - t_ref[...]
        part = 0.5 * jnp.sum(jnp.mean(err * err, axis=-1, keepdims=True), axis=0, keepdims=True)
        dy = err * (1.0 / d)
        g = dy * wv
        dh_ref[...] = r * (g - nrm * jnp.mean(g * nrm, axis=-1, keepdims=True))

        @pl.when(pl.program_id(0) == 0)
        def _():
            loss_ref[...] = jnp.zeros_like(loss_ref)
            gw_ref[...] = jnp.zeros_like(gw_ref)

        loss_ref[...] += jnp.broadcast_to(part, loss_ref.shape)
        gw_ref[...] += jnp.sum(dy * nrm, axis=0, keepdims=True)

    tok = pl.BlockSpec((tm, d), lambda i: (i, 0))
    row = pl.BlockSpec((1, d), lambda i: (0, 0))
    return pl.pallas_call(
        body, name="mlp_down_loss", grid=(m // tm,),
        in_specs=[pl.BlockSpec((tm, k), lambda i: (i, 0)), pl.BlockSpec((k, d), lambda i: (0, 0)), tok, tok, row],
        out_specs=[tok, pl.BlockSpec((1, LANES), lambda i: (0, 0)), row],
        out_shape=[jax.ShapeDtypeStruct((m, d), F32), jax.ShapeDtypeStruct((1, LANES), F32),
                   jax.ShapeDtypeStruct((1, d), F32)],
        compiler_params=_params(("arbitrary",)),
    )(pre, w_down, h2, target, w)


def _mlp_down_bwd(dh3, w_down, pre):
    m, k = dh3.shape
    n = w_down.shape[0]
    tm, tn = min(256, m), n

    def body(a_ref, b_ref, p_ref, dpre_ref, u_ref):
        r = jnp.maximum(p_ref[...].astype(F32), 0.0)
        du = _bdot_nt(a_ref[...], b_ref[...])
        dpre_ref[...] = (du * (2.0 * r)).astype(BF16)
        u_ref[...] = (r * r).astype(BF16)

    blk = pl.BlockSpec((tm, tn), lambda i, j: (i, j))
    return pl.pallas_call(
        body, name="mlp_down_bwd", grid=(m // tm, n // tn),
        in_specs=[pl.BlockSpec((tm, k), lambda i, j: (i, 0)), pl.BlockSpec((tn, k), lambda i, j: (j, 0)), blk],
        out_specs=[blk, blk],
        out_shape=[jax.ShapeDtypeStruct((m, n), BF16), jax.ShapeDtypeStruct((m, n), BF16)],
        compiler_params=_params(("parallel", "parallel")),
    )(dh3, w_down, pre)


CONV_TC = 512


def _conv_fwd(xbc, cw, cb, bsz, seq):
    tt = min(512, seq)
    nt, hb = seq // tt, tt // SUBLANES
    x3 = xbc.reshape(bsz, seq, CONV_CH)

    def body(xm_ref, xh_ref, w_ref, b_ref, o_ref):
        i = pl.program_id(2)
        x = xm_ref[0]
        halo = jnp.where(i > 0, xh_ref[0], 0.0)
        xx = jnp.concatenate([halo, x], axis=0)
        acc = x * w_ref[3:4, :] + b_ref[...]
        for s in range(1, CONV_K):
            acc = acc + pltpu.roll(xx, s, 0)[SUBLANES:, :] * w_ref[3 - s:4 - s, :]
        o_ref[0] = acc * _sigmoid(acc)

    out = pl.pallas_call(
        body, name="conv_fwd", grid=(CONV_CH // CONV_TC, bsz, nt),
        in_specs=[pl.BlockSpec((1, tt, CONV_TC), lambda c, b, i: (b, i, c)),
                  pl.BlockSpec((1, SUBLANES, CONV_TC), lambda c, b, i: (b, jnp.maximum(i * hb - 1, 0), c)),
                  pl.BlockSpec((CONV_K, CONV_TC), lambda c, b, i: (0, c)),
                  pl.BlockSpec((1, CONV_TC), lambda c, b, i: (0, c))],
        out_specs=pl.BlockSpec((1, tt, CONV_TC), lambda c, b, i: (b, i, c)),
        out_shape=jax.ShapeDtypeStruct((bsz, seq, CONV_CH), F32),
        compiler_params=_params(("parallel", "parallel", "parallel")),
    )(x3, x3, cw, cb)
    return out.reshape(bsz * seq, CONV_CH)


def _conv_bwd(da, xbc, cw, cb, bsz, seq):
    tt = min(512, seq)
    nt, hb = seq // tt, tt // SUBLANES
    x3 = xbc.reshape(bsz, seq, CONV_CH)
    da3 = da.reshape(bsz, seq, CONV_CH)
    n2 = tt + SUBLANES

    def body(dam_ref, daa_ref, xm_ref, xb_ref, xa_ref, w_ref, b_ref, du_ref, gw_ref, gb_ref):
        bi, i = pl.program_id(1), pl.program_id(2)
        before = jnp.where(i > 0, xb_ref[0], 0.0)
        after = jnp.where(i < nt - 1, xa_ref[0], 0.0)
        xx = jnp.concatenate([before, xm_ref[0], after], axis=0)
        rolled = [xx] + [pltpu.roll(xx, s, 0) for s in range(1, CONV_K)]
        pre = b_ref[...]
        for s in range(CONV_K):
            pre = pre + rolled[s][SUBLANES:, :] * w_ref[3 - s:4 - s, :]
        da_ext = jnp.concatenate([dam_ref[0], jnp.where(i < nt - 1, daa_ref[0], 0.0)], axis=0)
        sg = _sigmoid(pre)
        dpre = da_ext * sg * (1.0 + pre * (1.0 - sg))
        du = dpre[:tt, :] * w_ref[3:4, :]
        for s in range(1, CONV_K):
            du = du + pltpu.roll(dpre, n2 - s, 0)[:tt, :] * w_ref[3 - s:4 - s, :]
        du_ref[0] = du.astype(BF16)
        dpm = dpre[:tt, :]
        gws = [jnp.sum(dpm * rolled[3 - kk][SUBLANES:SUBLANES + tt, :], axis=0, keepdims=True) for kk in range(CONV_K)]

        @pl.when((bi == 0) & (i == 0))
        def _():
            gw_ref[...] = jnp.zeros_like(gw_ref)
            gb_ref[...] = jnp.zeros_like(gb_ref)

        gw_ref[...] += jnp.concatenate(gws, axis=0)
        gb_ref[...] += jnp.sum(dpm, axis=0, keepdims=True)

    main = pl.BlockSpec((1, tt, CONV_TC), lambda c, b, i: (b, i, c))
    prev = pl.BlockSpec((1, SUBLANES, CONV_TC), lambda c, b, i: (b, jnp.maximum(i * hb - 1, 0), c))
    nxt = pl.BlockSpec((1, SUBLANES, CONV_TC), lambda c, b, i: (b, jnp.minimum((i + 1) * hb, seq // SUBLANES - 1), c))
    du, gw, gb = pl.pallas_call(
        body, name="conv_bwd", grid=(CONV_CH // CONV_TC, bsz, nt),
        in_specs=[main, nxt, main, prev, nxt,
                  pl.BlockSpec((CONV_K, CONV_TC), lambda c, b, i: (0, c)),
                  pl.BlockSpec((1, CONV_TC), lambda c, b, i: (0, c))],
        out_specs=[main, pl.BlockSpec((CONV_K, CONV_TC), lambda c, b, i: (0, c)),
                   pl.BlockSpec((1, CONV_TC), lambda c, b, i: (0, c))],
        out_shape=[jax.ShapeDtypeStruct((bsz, seq, CONV_CH), BF16), jax.ShapeDtypeStruct((CONV_K, CONV_CH), F32),
                   jax.ShapeDtypeStruct((1, CONV_CH), F32)],
        compiler_params=_params(("parallel", "arbitrary", "arbitrary")),
    )(da3, da3, x3, x3, x3, cw, cb)
    return du.reshape(bsz * seq, CONV_CH), gw, gb


def _ssd_constants():
    tri = (_iota((CHUNK, CHUNK), 1) <= _iota((CHUNK, CHUNK), 0)).astype(BF16)
    return tri, tri.T, _head_expand().astype(BF16), _head_reduce().astype(BF16)


def _ssd_prologue(dtf_ref, dtft_ref, bias_ref, biast_ref, arow_ref, acol_ref, tri_ref, triu_ref, expand_ref,
                  dtfull_scr, acfull_scr, acr_scr):
    dtc = _softplus(dtf_ref[0] + bias_ref[...])
    a = dtc * arow_ref[...]
    acum = _sel_dot(tri_ref[...], a)
    expand = expand_ref[...]
    dtfull_scr[...] = _dot_sel(dtc, expand)
    acfull_scr[...] = _dot_sel(acum, expand)
    dtr = _softplus(dtft_ref[0] + biast_ref[...])
    acr_scr[...] = _dot_sel(dtr * acol_ref[...], triu_ref[...])
    return dtc, acum


def _decay_matrix(acum, acr_scr, h):
    lane = _iota((CHUNK, LANES), 1)
    ac_col = jnp.sum(jnp.where(lane == h, acum, 0.0), axis=1, keepdims=True)
    ac_row = acr_scr[h:h + 1, :]
    causal = _iota((CHUNK, CHUNK), 1) <= _iota((CHUNK, CHUNK), 0)
    return jnp.exp(jnp.where(causal, ac_col - ac_row, NEG))


def _ssd_fwd(xbc, dtf, dtft, bias, biast, arow, acol, dfull, bsz, seq):
    nc = seq // CHUNK
    x3 = xbc.reshape(bsz, seq, CONV_CH)
    tri, triu, expand, _ = _ssd_constants()

    def body(xbc_ref, dtf_ref, dtft_ref, bias_ref, biast_ref, arow_ref, acol_ref, dfull_ref,
             tri_ref, triu_ref, expand_ref, y_ref, hall_ref, h_scr, dtfull_scr, acfull_scr, acr_scr):
        @pl.when(pl.program_id(1) == 0)
        def _():
            h_scr[...] = jnp.zeros_like(h_scr)

        _, acum = _ssd_prologue(dtf_ref, dtft_ref, bias_ref, biast_ref, arow_ref, acol_ref,
                                tri_ref, triu_ref, expand_ref, dtfull_scr, acfull_scr, acr_scr)
        lane = _iota((CHUNK, LANES), 1)
        for g in range(2):
            bg = xbc_ref[0, :, D_MODEL + LANES * g:D_MODEL + LANES * (g + 1)]
            cg = xbc_ref[0, :, D_MODEL + 2 * LANES + LANES * g:D_MODEL + 2 * LANES + LANES * (g + 1)]
            cg_bf = cg.astype(BF16)
            gmat = _bdot_nt(cg_bf, bg)
            bgt_bf = bg.T.astype(BF16)
            for j in range(4):
                p = 4 * g + j
                sl = slice(LANES * p, LANES * (p + 1))
                xs = xbc_ref[0, :, sl]
                ac = acfull_scr[:, sl]
                acl = acfull_scr[CHUNK - 1:CHUNK, sl]
                xdt = xs * dtfull_scr[:, sl]
                xdt_bf = xdt.astype(BF16)
                hp = h_scr[p]
                hall_ref[0, 0, p] = hp
                yo = _bdot(cg_bf, hp) * jnp.exp(ac)
                yd = []
                for hh in range(2):
                    mmat = gmat * _decay_matrix(acum, acr_scr, 2 * p + hh)
                    yd.append(_bdot(mmat, xdt_bf))
                y_ref[0, :, sl] = jnp.where(lane < HEAD_DIM, yd[0], yd[1]) + yo + dfull_ref[:, sl] * xs
                h_scr[p] = hp * jnp.exp(acl) + _bdot(bgt_bf, xdt * jnp.exp(acl - ac))

    row = lambda w: pl.BlockSpec((1, w), lambda b, c: (0, 0))
    whole = lambda a: pl.BlockSpec(a.shape, lambda b, c: (0, 0))
    y, hall = pl.pallas_call(
        body, name="ssd_fwd", grid=(bsz, nc),
        in_specs=[pl.BlockSpec((1, CHUNK, CONV_CH), lambda b, c: (b, c, 0)),
                  pl.BlockSpec((1, CHUNK, LANES), lambda b, c: (b, c, 0)),
                  pl.BlockSpec((1, LANES, CHUNK), lambda b, c: (b, 0, c)),
                  row(LANES), pl.BlockSpec((LANES, 1), lambda b, c: (0, 0)),
                  row(LANES), pl.BlockSpec((LANES, 1), lambda b, c: (0, 0)), row(D_MODEL),
                  whole(tri), whole(triu), whole(expand)],
        out_specs=[pl.BlockSpec((1, CHUNK, D_MODEL), lambda b, c: (b, c, 0)),
                   pl.BlockSpec((1, 1, HEAD_PAIRS, SSD_STATE, LANES), lambda b, c: (b, c, 0, 0, 0))],
        out_shape=[jax.ShapeDtypeStruct((bsz, seq, D_MODEL), F32),
                   jax.ShapeDtypeStruct((bsz, nc, HEAD_PAIRS, SSD_STATE, LANES), F32)],
        scratch_shapes=[pltpu.VMEM((HEAD_PAIRS, SSD_STATE, LANES), F32), pltpu.VMEM((CHUNK, D_MODEL), F32),
                        pltpu.VMEM((CHUNK, D_MODEL), F32), pltpu.VMEM((LANES, CHUNK), F32)],
        compiler_params=_params(("parallel", "arbitrary")),
    )(x3, dtf.reshape(bsz, seq, LANES), dtft, bias, biast, arow, acol, dfull, tri, triu, expand)
    return y.reshape(bsz * seq, D_MODEL), hall


def _ssd_bwd(dy, xbc, dtf, dtft, bias, biast, arow, acol, dfull, hall, bsz, seq):
    nc = seq // CHUNK
    x3 = xbc.reshape(bsz, seq, CONV_CH)
    dy3 = dy.reshape(bsz, seq, D_MODEL)
    consts = _ssd_constants()

    def body(dy_ref, xbc_ref, dtf_ref, dtft_ref, bias_ref, biast_ref, arow_ref, acol_ref, dfull_ref, hall_ref,
             tri_ref, triu_ref, expand_ref, reduce_ref,
             dx_ref, ddt_ref, acc_ref, dh_scr, dtfull_scr, acfull_scr, acr_scr, csum_scr, dacf_scr, ddtf_scr, ddl_scr):
        first = (pl.program_id(0) == 0) & (pl.program_id(1) == 0)

        @pl.when(first)
        def _():
            acc_ref[...] = jnp.zeros_like(acc_ref)

        @pl.when(pl.program_id(1) == 0)
        def _():
            dh_scr[...] = jnp.zeros_like(dh_scr)

        dtc, acum = _ssd_prologue(dtf_ref, dtft_ref, bias_ref, biast_ref, arow_ref, acol_ref,
                                  tri_ref, triu_ref, expand_ref, dtfull_scr, acfull_scr, acr_scr)
        csum_scr[...] = jnp.zeros_like(csum_scr)
        lane = _iota((CHUNK, LANES), 1)
        last_row = _iota((CHUNK, LANES), 0) == CHUNK - 1
        rs16 = jnp.zeros((CHUNK, LANES), F32)
        for g in range(2):
            bsl = slice(D_MODEL + LANES * g, D_MODEL + LANES * (g + 1))
            csl = slice(D_MODEL + 2 * LANES + LANES * g, D_MODEL + 2 * LANES + LANES * (g + 1))
            bg_bf = xbc_ref[0, :, bsl].astype(BF16)
            cg = xbc_ref[0, :, csl]
            cg_bf = cg.astype(BF16)
            cgt_bf = cg.T.astype(BF16)
            gmat = _bdot_nt(cg_bf, bg_bf)
            dg = jnp.zeros((CHUNK, CHUNK), F32)
            dbg = jnp.zeros((CHUNK, SSD_STATE), F32)
            dcg = jnp.zeros((CHUNK, SSD_STATE), F32)
            for j in range(4):
                p = 4 * g + j
                sl = slice(LANES * p, LANES * (p + 1))
                xs = xbc_ref[0, :, sl]
                dt = dtfull_scr[:, sl]
                ac = acfull_scr[:, sl]
                acl = acfull_scr[CHUNK - 1:CHUNK, sl]
                dyp = dy_ref[0, :, sl]
                xdt = xs * dt
                xdt_bf = xdt.astype(BF16)
                e = jnp.exp(ac)
                dsd = jnp.exp(acl - ac)
                cd = jnp.exp(acl)
                xds_bf = (xdt * dsd).astype(BF16)
                hp = hall_ref[0, 0, p]
                hp_bf = hp.astype(BF16)
                dhn = dh_scr[p]
                dhn_bf = dhn.astype(BF16)
                yo = _bdot(cg_bf, hp_bf) * e
                dw_bf = (dyp * e).astype(BF16)
                dcg = dcg + _bdot_nt(dw_bf, hp_bf)
                dhin = _bdot(cgt_bf, dw_bf)
                dac = dyp * yo
                dacl = jnp.sum(dhn * hp, axis=0, keepdims=True) * cd
                dxds = _bdot(bg_bf, dhn_bf)
                dbg = dbg + _bdot_nt(xds_bf, dhn_bf)
                dxdt = dxds * dsd
                tdec = dxds * xdt * dsd
                dacl = dacl + jnp.sum(tdec, axis=0, keepdims=True)
                dac = dac - tdec
                dh_scr[p] = dhn * cd + dhin
                for hh in range(2):
                    h = 2 * p + hh
                    lm = (lane < HEAD_DIM) if hh == 0 else (lane >= HEAD_DIM)
                    dec = _decay_matrix(acum, acr_scr, h)
                    mmat = gmat * dec
                    dyh_bf = jnp.where(lm, dyp, 0.0).astype(BF16)
                    dm = _bdot_nt(dyh_bf, xdt_bf)
                    dxdt = dxdt + _bdot(mmat.T, dyh_bf)
                    dg = dg + dm * dec
                    q = dm * mmat
                    rs16 = rs16 + jnp.where(lane == h, jnp.sum(q, axis=1, keepdims=True), 0.0)
                    csum_scr[h:h + 1, :] = jnp.sum(q, axis=0, keepdims=True)
                dx_ref[0, :, sl] = dfull_ref[:, sl] * dyp + dxdt * dt
                dacf_scr[:, sl] = dac + jnp.where(last_row, dacl, 0.0)
                ddtf_scr[:, sl] = dxdt * xs
                ddl_scr[:, sl] = jnp.broadcast_to(jnp.sum(dyp * xs, axis=0, keepdims=True), (SUBLANES, LANES))
            dg_bf = dg.astype(BF16)
            dx_ref[0, :, bsl] = dbg + _bdot(dg.T, cg_bf)
            dx_ref[0, :, csl] = dcg + _bdot(dg_bf, bg_bf)
        reduce, triu = reduce_ref[...], triu_ref[...]
        dac16 = _dot_sel(dacf_scr[...], reduce) + rs16 - csum_scr[...].T
        da16 = _sel_dot(triu, dac16)
        ddt16 = da16 * arow_ref[...] + _dot_sel(ddtf_scr[...], reduce)
        ddtraw = ddt16 * _sigmoid(dtf_ref[0] + bias_ref[...])
        ddt_ref[0] = ddtraw
        acc_ref[0:8, :] += jnp.broadcast_to(jnp.sum(da16 * dtc * arow_ref[...], axis=0, keepdims=True), (SUBLANES, LANES))
        acc_ref[8:16, :] += _dot_sel(ddl_scr[...], reduce)
        acc_ref[16:24, :] += jnp.broadcast_to(jnp.sum(ddtraw, axis=0, keepdims=True), (SUBLANES, LANES))

    rev = lambda b, c: (b, nc - 1 - c, 0)
    row = lambda w: pl.BlockSpec((1, w), lambda b, c: (0, 0))
    dx, ddt, acc = pl.pallas_call(
        body, name="ssd_bwd", grid=(bsz, nc),
        in_specs=[pl.BlockSpec((1, CHUNK, D_MODEL), rev), pl.BlockSpec((1, CHUNK, CONV_CH), rev),
                  pl.BlockSpec((1, CHUNK, LANES), rev),
                  pl.BlockSpec((1, LANES, CHUNK), lambda b, c: (b, 0, nc - 1 - c)),
                  row(LANES), pl.BlockSpec((LANES, 1), lambda b, c: (0, 0)),
                  row(LANES), pl.BlockSpec((LANES, 1), lambda b, c: (0, 0)), row(D_MODEL),
                  pl.BlockSpec((1, 1, HEAD_PAIRS, SSD_STATE, LANES), lambda b, c: (b, nc - 1 - c, 0, 0, 0))]
        + [pl.BlockSpec(a.shape, lambda b, c: (0, 0)) for a in consts],
        out_specs=[pl.BlockSpec((1, CHUNK, CONV_CH), rev), pl.BlockSpec((1, CHUNK, LANES), rev),
                   pl.BlockSpec((3 * SUBLANES, LANES), lambda b, c: (0, 0))],
        out_shape=[jax.ShapeDtypeStruct((bsz, seq, CONV_CH), F32), jax.ShapeDtypeStruct((bsz, seq, LANES), F32),
                   jax.ShapeDtypeStruct((3 * SUBLANES, LANES), F32)],
        scratch_shapes=[pltpu.VMEM((HEAD_PAIRS, SSD_STATE, LANES), F32), pltpu.VMEM((CHUNK, D_MODEL), F32),
                        pltpu.VMEM((CHUNK, D_MODEL), F32), pltpu.VMEM((LANES, CHUNK), F32),
                        pltpu.VMEM((LANES, CHUNK), F32), pltpu.VMEM((CHUNK, D_MODEL), F32),
                        pltpu.VMEM((CHUNK, D_MODEL), F32), pltpu.VMEM((SUBLANES, D_MODEL), F32)],
        compiler_params=_params(("arbitrary", "arbitrary")),
    )(dy3, x3, dtf.reshape(bsz, seq, LANES), dtft, bias, biast, arow, acol, dfull, hall, *consts)
    return dx.reshape(bsz * seq, CONV_CH), ddt.reshape(bsz * seq, LANES), acc


GROUP_W = D_MODEL // 2


def _gnorm_fwd(y, z, o_att, w):
    n = y.shape[0]
    tm = min(1024, n)

    def body(y_ref, z_ref, o_ref, w_ref, out_ref):
        zv = z_ref[...]
        g = y_ref[...] * (zv * _sigmoid(zv))
        for gi in range(2):
            sl = slice(GROUP_W * gi, GROUP_W * (gi + 1))
            gs = g[:, sl]
            r = lax.rsqrt(jnp.mean(gs * gs, axis=-1, keepdims=True) + NORM_EPS)
            out_ref[:, sl] = (gs * r * w_ref[:, sl]).astype(BF16)
        out_ref[:, D_MODEL:] = o_ref[...].astype(BF16)

    tok = pl.BlockSpec((tm, D_MODEL), lambda i: (i, 0))
    return pl.pallas_call(
        body, name="gnorm_fwd", grid=(n // tm,),
        in_specs=[tok, tok, tok, pl.BlockSpec((1, D_MODEL), lambda i: (0, 0))],
        out_specs=pl.BlockSpec((tm, MIX_WIDTH), lambda i: (i, 0)),
        out_shape=jax.ShapeDtypeStruct((n, MIX_WIDTH), BF16),
        compiler_params=_params(("parallel",)),
    )(y, z, o_att, w)


def _gnorm_bwd(dycat, y, z, w):
    n = y.shape[0]
    tm = min(512, n)

    def body(dn_ref, y_ref, z_ref, w_ref, dy_ref, dz_ref, gw_ref):
        zv, yv = z_ref[...], y_ref[...]
        sz = _sigmoid(zv)
        sil = zv * sz
        g = yv * sil

        @pl.when(pl.program_id(0) == 0)
        def _():
            gw_ref[...] = jnp.zeros_like(gw_ref)

        for gi in range(2):
            sl = slice(GROUP_W * gi, GROUP_W * (gi + 1))
            gs = g[:, sl]
            r = lax.rsqrt(jnp.mean(gs * gs, axis=-1, keepdims=True) + NORM_EPS)
            nrm = gs * r
            dyn = dn_ref[:, sl]
            dn = dyn * w_ref[:, sl]
            dgs = r * (dn - nrm * jnp.mean(dn * nrm, axis=-1, keepdims=True))
            dy_ref[:, sl] = dgs * sil[:, sl]
            dz_ref[:, sl] = (dgs * yv[:, sl] * (sz[:, sl] * (1.0 + zv[:, sl] * (1.0 - sz[:, sl])))).astype(BF16)
            gw_ref[:, sl] += jnp.sum(dyn * nrm, axis=0, keepdims=True)

    tok = pl.BlockSpec((tm, D_MODEL), lambda i: (i, 0))
    row = pl.BlockSpec((1, D_MODEL), lambda i: (0, 0))
    return pl.pallas_call(
        body, name="gnorm_bwd", grid=(n // tm,),
        in_specs=[tok, tok, tok, row], out_specs=[tok, tok, row],
        out_shape=[jax.ShapeDtypeStruct((n, D_MODEL), F32), jax.ShapeDtypeStruct((n, D_MODEL), BF16),
                   jax.ShapeDtypeStruct((1, D_MODEL), F32)],
        compiler_params=_params(("arbitrary",)),
    )(dycat, y, z, w)


AUX_C = 3
AUX_ONE = 3


def _aux_base(hh):
    return HEAD_DIM * (1 - hh)


def _fox_prep(dtf, fb, bsz, seq):
    def body(x_ref, b_ref, aux_ref):
        tri = (_iota((CHUNK, CHUNK), 1) <= _iota((CHUNK, CHUNK), 0)).astype(F32)
        row, col = _iota((LANES, D_MODEL), 0), _iota((LANES, D_MODEL), 1)
        lane = jnp.bitwise_and(col, LANES - 1)
        other = (lane < HEAD_DIM).astype(jnp.int32)
        term = lane - HEAD_DIM * (1 - other)
        src = F_COL + 2 * jnp.right_shift(col, 7) + other
        place = [jnp.where((row == src) & (term == i), -1.0, 0.0).astype(BF16) for i in range(AUX_C)]
        lane1 = jnp.bitwise_and(_iota((1, D_MODEL), 1), LANES - 1)
        ones_lane = (lane1 - HEAD_DIM * (1 - (lane1 < HEAD_DIM).astype(jnp.int32)) == AUX_ONE).astype(F32)
        carry = jnp.zeros((1, LANES), F32)
        for j in range(seq // CHUNK):
            sl = slice(CHUNK * j, CHUNK * (j + 1))
            lf = _log_sigmoid(x_ref[sl, :] + b_ref[...])
            c = _sel_dot(tri, lf) + carry
            carry = carry + jnp.sum(lf, axis=0, keepdims=True)
            t1, t2, t3 = (jnp.dot(t, p, preferred_element_type=F32) for t, p in zip(_split3(c), place))
            aux_ref[sl, :] = (t1 + t2 + t3 + ones_lane).astype(BF16)

    return pl.pallas_call(
        body, name="fox_prep", grid=(bsz,),
        in_specs=[pl.BlockSpec((seq, LANES), lambda b: (b, 0)), pl.BlockSpec((1, LANES), lambda b: (0, 0))],
        out_specs=pl.BlockSpec((seq, D_MODEL), lambda b: (b, 0)),
        out_shape=jax.ShapeDtypeStruct((bsz * seq, D_MODEL), BF16),
        compiler_params=_params(("parallel",)),
    )(dtf, fb)


def _fox_prep_bwd(dck, dcq, ddt, dtf, fb, bsz, seq):
    nj = seq // CHUNK

    def body(dck_ref, dcq_ref, ddt_ref, x_ref, b_ref, out_ref, gb_ref):
        triu = (_iota((CHUNK, CHUNK), 0) <= _iota((CHUNK, CHUNK), 1)).astype(F32)
        is_f = (_iota((CHUNK, LANES), 1) >= F_COL) & (_iota((CHUNK, LANES), 1) < 2 * F_COL)
        carry = jnp.zeros((1, LANES), F32)
        total = jnp.zeros((1, LANES), F32)
        for j in range(nj - 1, -1, -1):
            sl = slice(CHUNK * j, CHUNK * (j + 1))
            dcv = dck_ref[sl, :] + dcq_ref[sl, :]
            dlf = _sel_dot(triu, dcv) + carry
            carry = carry + jnp.sum(dcv, axis=0, keepdims=True)
            df = jnp.where(is_f, dlf * _sigmoid(-(x_ref[sl, :] + b_ref[...])), 0.0)
            out_ref[sl, :] = (df + ddt_ref[sl, :]).astype(BF16)
            total = total + jnp.sum(df, axis=0, keepdims=True)

        @pl.when(pl.program_id(0) == 0)
        def _():
            gb_ref[...] = jnp.zeros_like(gb_ref)

        gb_ref[...] += jnp.broadcast_to(total, (SUBLANES, LANES))

    blk = pl.BlockSpec((seq, LANES), lambda b: (b, 0))
    return pl.pallas_call(
        body, name="fox_prep_bwd", grid=(bsz,),
        in_specs=[blk, blk, blk, blk, pl.BlockSpec((1, LANES), lambda b: (0, 0))],
        out_specs=[blk, pl.BlockSpec((SUBLANES, LANES), lambda b: (0, 0))],
        out_shape=[jax.ShapeDtypeStruct((bsz * seq, LANES), BF16), jax.ShapeDtypeStruct((SUBLANES, LANES), F32)],
        compiler_params=_params(("arbitrary",)),
    )(dck, dcq, ddt, dtf, fb)


ATT_BLOCK_FWD = 2048
ATT_BLOCK_KEYS = 256
ATT_BLOCK_BWD = 512


def _att_operands(q_ref, k_ref, aux_ref, hh):
    lane = _iota((1, LANES), 1)
    lm = (lane < HEAD_DIM) if hh == 0 else (lane >= HEAD_DIM)
    base = _aux_base(hh)
    zero = jnp.zeros((1, LANES), BF16)
    ka = jnp.where(lm, k_ref[...], jnp.where((lane >= base) & (lane <= base + AUX_ONE), aux_ref[...], zero))
    qa = jnp.where(lm, q_ref[...] * ATT_SCALE,
                   jnp.where((lane >= base) & (lane < base + AUX_C), jnp.ones((1, LANES), BF16), zero))
    return lm, base, qa, ka


def _att_fwd(qkv, ccol, bsz, seq, gather):
    bq, bk = min(ATT_BLOCK_FWD, seq), min(ATT_BLOCK_KEYS, seq)
    nq, ratio = seq // bq, bq // bk
    nx = len(gather)

    def body(*refs):
        q_ref, k_ref, v_ref, c_ref = refs[:4]
        x_refs = refs[4:4 + nx]
        o_ref, lse_ref = refs[4 + nx:6 + nx]
        xo_refs = refs[6 + nx:6 + 2 * nx]
        qa_scr, ka_scr, vt_scr = refs[6 + 2 * nx:9 + 2 * nx]
        sems = refs[9 + 2 * nx:]
        pair = pl.program_id(1)

        @pl.when((pl.program_id(0) == 0) & (pair == 0))
        def _():
            _exchange_start("gather", x_refs, xo_refs, *sems)

        lse_ref[...] = jnp.zeros_like(lse_ref)
        for hh in range(2):
            _, _, qa, ka = _att_operands(q_ref, k_ref, c_ref, hh)
            qa_scr[hh] = qa
            ka_scr[hh] = ka
        for t0 in range(0, seq, bq):
            vt_scr[:, t0:t0 + bq] = v_ref[t0:t0 + bq, :].T
        key_minus_query = _iota((bk, bq), 0) - _iota((bk, bq), 1)

        def q_step(i, carry0):
            qrows = pl.ds(pl.multiple_of(i * bq, bq), bq)

            def scores(j):
                krows = pl.ds(pl.multiple_of(j * bk, bk), bk)
                return tuple(_bdot_nt(ka_scr[hh, krows, :], qa_scr[hh, qrows, :]) for hh in range(2))

            def scores_tail(r):
                krows = pl.ds(pl.multiple_of((i * ratio + r) * bk, bk), bk)
                qsub = pl.ds(pl.multiple_of(i * bq + r * bk, bk), bq - r * bk)
                return tuple(_bdot_nt(ka_scr[hh, krows, :], qa_scr[hh, qsub, :]) for hh in range(2))

            def update(state, st_pair, j, masked):
                krows = pl.ds(pl.multiple_of(j * bk, bk), bk)
                out = []
                for hh in range(2):
                    m_all, l_all, acc_all = state[hh]
                    st = st_pair[hh]
                    width = st.shape[1]
                    lo = bq - width
                    m, l, acc = m_all[:, lo:], l_all[:, lo:], acc_all[:, lo:]
                    if masked:
                        st = jnp.where(key_minus_query[:, :width] <= 0, st, NEG)
                    mn = jnp.maximum(m, jnp.max(st, axis=0, keepdims=True))
                    alpha = jnp.exp(m - mn)
                    pt = jnp.exp(st - mn)
                    l = alpha * l + jnp.sum(pt, axis=0, keepdims=True)
                    vt = vt_scr[HEAD_DIM * hh:HEAD_DIM * (hh + 1), krows]
                    acc = alpha * acc + _bdot(vt, pt)
                    if lo:
                        mn, l, acc = (jnp.concatenate([old[:, :lo], new], axis=1)
                                      for old, new in ((m_all, mn), (l_all, l), (acc_all, acc)))
                    out.append((mn, l, acc))
                return tuple(out)

            def step(j, carry):
                state, s_cur = carry
                s_next = scores(j + 1)
                return update(state, s_cur, j, False), s_next

            def step_pair(t, carry):
                state, s_cur = carry
                s_second = scores(2 * t + 1)
                s_next = scores(2 * t + 2)
                state = update(state, s_cur, 2 * t, False)
                return update(state, s_second, 2 * t + 1, False), s_next

            one = (jnp.full((1, bq), NEG, F32), jnp.zeros((1, bq), F32), jnp.zeros((HEAD_DIM, bq), F32))
            first_diag = i * ratio
            if ratio % 2 == 0:
                state, s_cur = lax.fori_loop(0, first_diag // 2, step_pair, ((one, one), scores(0)))
            else:
                state, s_cur = lax.fori_loop(0, first_diag, step, ((one, one), scores(0)))
            for r in range(ratio):
                s_next = scores_tail(r + 1) if r + 1 < ratio else None
                state = update(state, s_cur, first_diag + r, True)
                s_cur = s_next
            (m0, l0, acc0), (m1, l1, acc1) = state
            ot = jnp.concatenate([acc0 * (1.0 / l0), acc1 * (1.0 / l1)], axis=0)
            o_ref[qrows, :] = ot.T
            lse_ref[0, 0, 0:1, qrows] = m0 + jnp.log(l0)
            lse_ref[0, 0, 1:2, qrows] = m1 + jnp.log(l1)
            return carry0

        lax.fori_loop(0, nq, q_step, 0)

        @pl.when((pl.program_id(0) == bsz - 1) & (pair == HEAD_PAIRS - 1))
        def _():
            _exchange_wait("gather", x_refs, xo_refs, *sems)

    col = lambda off: pl.BlockSpec((seq, LANES), lambda b, p: (b, off + p))
    head2 = pltpu.VMEM((2, seq, LANES), BF16)
    hbm = pl.BlockSpec(memory_space=pl.ANY)
    return pl.pallas_call(
        body, name="att_fwd", grid=(bsz, HEAD_PAIRS),
        in_specs=[col(0), col(HEAD_PAIRS), col(2 * HEAD_PAIRS), col(0)] + [hbm] * nx,
        out_specs=[pl.BlockSpec((seq, LANES), lambda b, p: (b, p)),
                   pl.BlockSpec((1, 1, SUBLANES, seq), lambda b, p: (b, p, 0, 0))] + [hbm] * nx,
        out_shape=[jax.ShapeDtypeStruct((bsz * seq, D_MODEL), F32),
                   jax.ShapeDtypeStruct((bsz, HEAD_PAIRS, SUBLANES, seq), F32)] + _exchange_shapes("gather", gather),
        scratch_shapes=[head2, head2, pltpu.VMEM((LANES, seq), BF16)] + _exchange_scratch(nx),
        compiler_params=_params(("arbitrary", "arbitrary")),
    )(qkv, qkv, qkv, ccol, *gather)


def _att_bwd(qkv, dycat, o, lse, ccol, bsz, seq, scatter):
    blk = bq = min(ATT_BLOCK_BWD, seq)
    nb = nq = seq // blk
    nx = len(scatter)

    def body(*refs):
        q_ref, k_ref, v_ref, do_ref, o_ref, lse_ref, c_ref = refs[:7]
        x_refs = refs[7:7 + nx]
        dq_ref, dk_ref, dv_ref, dck_ref, dcq_ref = refs[7 + nx:12 + nx]
        xo_refs = refs[12 + nx:12 + 2 * nx]
        (qa_scr, ka_scr, va_scr, doa_scr, kat_scr, qat_scr, dot_scr, d_scr, dqt_scr,
         dkt_scr) = refs[12 + 2 * nx:22 + 2 * nx]
        sems = refs[22 + 2 * nx:]
        pair = pl.program_id(1)

        @pl.when((pl.program_id(0) == 0) & (pair == 0))
        def _():
            _exchange_start("scatter", x_refs, xo_refs, *sems)

        query_minus_key = _iota((blk, bq), 1) - _iota((blk, bq), 0)
        lane_b = _iota((blk, LANES), 1)
        row_t = _iota((LANES, seq), 0)
        bases = []
        ones8 = jnp.ones((SUBLANES, LANES), F32)
        for hh in range(2):
            lm, base, qa, ka = _att_operands(q_ref, k_ref, c_ref, hh)
            bases.append(base)
            qa_scr[hh] = qa
            ka_scr[hh] = ka
            va_scr[hh] = jnp.where(lm, v_ref[...], jnp.zeros((seq, LANES), BF16))
            doa = jnp.where(lm, do_ref[...], 0.0).astype(BF16)
            doa_scr[hh] = doa
            for t0 in range(0, seq, blk):
                kat_scr[hh, :, t0:t0 + blk] = ka[t0:t0 + blk, :].T
                qat_scr[hh, :, t0:t0 + blk] = qa[t0:t0 + blk, :].T
            d1, d2, d3 = (_bdot_nt(ones8, term) for term in _split3(doa.astype(F32) * o_ref[...]))
            d_scr[hh] = d1 + d2 + d3
        for t0 in range(0, seq, blk):
            dot_scr[:, t0:t0 + blk] = do_ref[t0:t0 + blk, :].astype(BF16).T
        dqt_scr[...] = jnp.zeros_like(dqt_scr)
        dck_ref[...] = jnp.zeros_like(dck_ref)

        half = blk // 2

        def block(hh, krows, rows, mask):
            st = _bdot_nt(ka_scr[hh, krows, :], qa_scr[hh, rows, :])
            dpt = _bdot_nt(va_scr[hh, krows, :], doa_scr[hh, rows, :])
            if mask is not None:
                st = jnp.where(mask, st, NEG)
            pt = jnp.exp(st - lse_ref[0, 0, hh:hh + 1, rows])
            dst = (pt * (dpt - d_scr[hh, 0:1, rows])).astype(BF16)
            dqt_scr[hh, :, rows] += _bdot(kat_scr[hh, :, krows], dst)
            return (_bdot_nt(dot_scr[HEAD_DIM * hh:HEAD_DIM * (hh + 1), rows], pt),
                    _bdot_nt(qat_scr[hh, :, rows], dst))

        for j in range(nb):
            krows = pl.ds(j * blk, blk)
            dkt, dvt = [None, None], [None, None]
            for hh in range(2):
                first = block(hh, pl.ds(j * blk, half), pl.ds(j * bq, bq), query_minus_key[:half, :] >= 0)
                second = block(hh, pl.ds(j * blk + half, half), pl.ds(j * bq + half, half),
                               query_minus_key[:half, :half] >= 0)
                dvt[hh] = jnp.concatenate([first[0], second[0]], axis=1)
                dkt[hh] = jnp.concatenate([first[1], second[1]], axis=1)
            for i in range(j + 1, nq):
                for hh in range(2):
                    dv_part, dk_part = block(hh, krows, pl.ds(i * bq, bq), None)
                    dvt[hh] = dvt[hh] + dv_part
                    dkt[hh] = dkt[hh] + dk_part
            dk_ref[krows, :] = jnp.where(_iota((LANES, blk), 0) < HEAD_DIM, dkt[0], dkt[1]).T.astype(BF16)
            dv_ref[krows, :] = jnp.concatenate(dvt, axis=0).T.astype(BF16)
            for hh in range(2):
                dkt_scr[hh] = dkt[hh]
                dck_ref[0, 0, hh:hh + 1, krows] = -dkt_scr[hh, bases[hh]:bases[hh] + 1, :]
        for t0 in range(0, seq, blk):
            dq0, dq1 = dqt_scr[0, :, t0:t0 + blk].T, dqt_scr[1, :, t0:t0 + blk].T
            dq_ref[t0:t0 + blk, :] = (jnp.where(lane_b < HEAD_DIM, dq0, dq1) * ATT_SCALE).astype(BF16)
        dcq_ref[...] = jnp.zeros_like(dcq_ref)
        for hh in range(2):
            dcq_ref[0, 0, hh:hh + 1, :] = jnp.sum(jnp.where(row_t == bases[hh] + AUX_ONE, dqt_scr[hh], 0.0),
                                                   axis=0, keepdims=True)

        @pl.when((pl.program_id(0) == bsz - 1) & (pair == HEAD_PAIRS - 1))
        def _():
            _exchange_wait("scatter", x_refs, xo_refs, *sems)

    col = lambda off: pl.BlockSpec((seq, LANES), lambda b, p: (b, off + p))
    rowvec = pl.BlockSpec((1, 1, SUBLANES, seq), lambda b, p: (b, p, 0, 0))
    out = jax.ShapeDtypeStruct((bsz * seq, D_MODEL), BF16)
    rows_shape = jax.ShapeDtypeStruct((bsz, HEAD_PAIRS, SUBLANES, seq), F32)
    head2 = pltpu.VMEM((2, seq, LANES), BF16)
    hbm = pl.BlockSpec(memory_space=pl.ANY)
    return pl.pallas_call(
        body, name="att_bwd", grid=(bsz, HEAD_PAIRS),
        in_specs=[col(0), col(HEAD_PAIRS), col(2 * HEAD_PAIRS), col(HEAD_PAIRS), col(0), rowvec, col(0)] + [hbm] * nx,
        out_specs=[col(0), col(0), col(0), rowvec, rowvec] + [hbm] * nx,
        out_shape=[out, out, out, rows_shape, rows_shape] + _exchange_shapes("scatter", scatter),
        scratch_shapes=[head2, head2, head2, head2, pltpu.VMEM((2, LANES, seq), BF16),
                        pltpu.VMEM((2, LANES, seq), BF16), pltpu.VMEM((LANES, seq), BF16),
                        pltpu.VMEM((2, SUBLANES, seq), F32), pltpu.VMEM((2, LANES, seq), F32),
                        pltpu.VMEM((2, LANES, blk), F32)] + _exchange_scratch(nx),
        compiler_params=_params(("arbitrary", "arbitrary")),
    )(qkv, qkv, qkv, dycat, o, lse, ccol, *scatter)


def _adamw_math(w, g, m, v):
    m = ADAM_B1 * m + (1.0 - ADAM_B1) * g
    v = ADAM_B2 * v + (1.0 - ADAM_B2) * jnp.square(g)
    m_hat = m / ADAM_C1
    v_hat = v / ADAM_C2
    delta = -ADAM_LR * (m_hat / (jnp.sqrt(v_hat) + ADAM_EPS) + ADAM_WD * w)
    return delta, m, v


def _row_tile(rows):
    for tr in (256, 128, 64, 32, 16, 8):
        if rows % tr == 0:
            return tr
    return rows


def _sum_parts(parts, name):
    nparts, rows, cols = parts.shape
    tr = rows if rows % SUBLANES else _row_tile(rows)

    def body(p_ref, o_ref):
        g = p_ref[0].astype(F32)
        for s in range(1, nparts):
            g = g + p_ref[s].astype(F32)
        o_ref[...] = g

    return pl.pallas_call(
        body, name=name, grid=(rows // tr,),
        in_specs=[pl.BlockSpec((nparts, tr, cols), lambda i: (0, i, 0))],
        out_specs=pl.BlockSpec((tr, cols), lambda i: (i, 0)),
        out_shape=jax.ShapeDtypeStruct((rows, cols), F32),
        compiler_params=_params(("parallel",)),
    )(parts)


def _adamw_parts(parts, w, m, v, name):
    nparts, rows, cols = parts.shape
    tr = _row_tile(rows)

    def body(p_ref, w_ref, m_ref, v_ref, g_ref, d_ref, mo_ref, vo_ref):
        g = p_ref[0].astype(F32)
        for s in range(1, nparts):
            g = g + p_ref[s].astype(F32)
        delta, mn, vn = _adamw_math(w_ref[...], g, m_ref[...], v_ref[...])
        g_ref[...] = g
        d_ref[...] = delta
        mo_ref[...] = mn
        vo_ref[...] = vn

    blk = pl.BlockSpec((tr, cols), lambda i: (i, 0))
    shp = jax.ShapeDtypeStruct((rows, cols), F32)
    return pl.pallas_call(
        body, name=name, grid=(rows // tr,),
        in_specs=[pl.BlockSpec((nparts, tr, cols), lambda i: (0, i, 0)), blk, blk, blk],
        out_specs=[blk, blk, blk, blk], out_shape=[shp, shp, shp, shp],
        compiler_params=_params(("parallel",)),
    )(parts, w, m, v)


def _me():
    return lax.axis_index("x"), lax.axis_index("y"), lax.axis_index("c")


def _flip(pos, k):
    x, y, c = pos
    kx, ky, kc = (k >> 2) & 1, (k >> 1) & 1, k & 1
    return (x ^ kx if kx else x, y ^ ky if ky else y, c ^ kc if kc else c)


def _logical(pos):
    return 4 * pos[0] + 2 * pos[1] + pos[2]


def _all_gather_two_level(shards):
    narr = len(shards)

    def body(*refs):
        x_refs, out_refs = refs[:narr], refs[narr:2 * narr]
        send_sems, recv_sems, local_sems = refs[2 * narr:]
        x, y, c = _me()
        me, sibling = (x, y, c), (x, y, 1 - c)
        chips = [(1 - x, y), (x, 1 - y), (1 - x, 1 - y)]

        def copy(a, k, block, to, src=None):
            slot = out_refs[a].at[_logical(block)]
            return pltpu.make_async_remote_copy(
                src_ref=slot if src is None else src, dst_ref=slot,
                send_sem=send_sems.at[a, k], recv_sem=recv_sems.at[a, k], device_id=to, device_id_type=MESH)

        mine = [pltpu.make_async_copy(x_refs[a], out_refs[a].at[_logical(me)], local_sems.at[a]) for a in range(narr)]
        for cp in mine:
            cp.start()
        first = []
        for a in range(narr):
            first.append(copy(a, 0, me, sibling, src=x_refs[a]))
            first += [copy(a, 1 + j, me, (*chip, c), src=x_refs[a]) for j, chip in enumerate(chips)]
        for cp in first:
            cp.start()
        passed = []
        for a in range(narr):
            for j, chip in enumerate(chips):
                copy(a, 1 + j, (*chip, c), me).wait_recv()
                fwd = copy(a, 4 + j, (*chip, c), sibling)
                fwd.start()
                passed.append(fwd)
        for a in range(narr):
            copy(a, 0, sibling, me).wait_recv()
            for j, chip in enumerate(chips):
                copy(a, 4 + j, (*chip, 1 - c), me).wait_recv()
        for cp in first + passed:
            cp.wait_send()
        for cp in mine:
            cp.wait()

    hbm = pl.BlockSpec(memory_space=pl.ANY)
    return pl.pallas_call(
        body, name="all_gather_big",
        out_shape=[jax.ShapeDtypeStruct((N_DEV,) + s.shape, s.dtype) for s in shards],
        in_specs=[hbm] * narr, out_specs=[hbm] * narr,
        scratch_shapes=[pltpu.SemaphoreType.DMA((narr, 7)), pltpu.SemaphoreType.DMA((narr, 7)),
                        pltpu.SemaphoreType.DMA((narr,))],
    )(*shards)


def _exchange_copies(kind, x_refs, out_refs, send_sems, recv_sems, local_sems):
    me = _me()
    mine = _logical(me)
    local, remote = [], []
    for a, (x_ref, out_ref) in enumerate(zip(x_refs, out_refs)):
        own = x_ref if kind == "gather" else x_ref.at[mine]
        local.append(pltpu.make_async_copy(own, out_ref.at[mine], local_sems.at[a]))
        for k in range(1, N_DEV):
            peer = _flip(me, k)
            src = x_ref if kind == "gather" else x_ref.at[_logical(peer)]
            remote.append(pltpu.make_async_remote_copy(
                src_ref=src, dst_ref=out_ref.at[mine], send_sem=send_sems.at[a, k - 1], recv_sem=recv_sems.at[a, k - 1],
                device_id=peer, device_id_type=MESH))
    return local, remote


def _exchange_start(kind, x_refs, out_refs, *sems):
    if not x_refs:
        return
    local, remote = _exchange_copies(kind, x_refs, out_refs, *sems)
    for cp in local + remote:
        cp.start()


def _exchange_wait(kind, x_refs, out_refs, *sems):
    if not x_refs:
        return
    local, remote = _exchange_copies(kind, x_refs, out_refs, *sems)
    for cp in remote:
        cp.wait_recv()
    for cp in remote:
        cp.wait_send()
    for cp in local:
        cp.wait()


def _exchange_shapes(kind, arrays):
    return [jax.ShapeDtypeStruct(((N_DEV,) if kind == "gather" else ()) + a.shape, a.dtype) for a in arrays]


def _exchange_scratch(narrays):
    if not narrays:
        return []
    return [pltpu.SemaphoreType.DMA((narrays, N_DEV - 1)), pltpu.SemaphoreType.DMA((narrays, N_DEV - 1)),
            pltpu.SemaphoreType.DMA((narrays,))]


def _exchange_rows(x_ref, buf_ref, send_sems, recv_sems):
    me = _me()
    buf_ref[_logical(me)] = x_ref[...]
    copies = []
    for k in range(1, N_DEV):
        peer = _flip(me, k)
        copies.append(pltpu.make_async_remote_copy(
            src_ref=x_ref, dst_ref=buf_ref.at[_logical(me)],
            send_sem=send_sems.at[k - 1], recv_sem=recv_sems.at[k - 1], device_id=peer, device_id_type=MESH))
    for cp in copies:
        cp.start()
    for cp in copies:
        cp.wait_recv()
    for cp in copies:
        cp.wait_send()


def _sum_slots(buf_ref):
    total = buf_ref[0]
    for s in range(1, N_DEV):
        total = total + buf_ref[s]
    return total


def _small_gather(x):
    def body(x_ref, o_ref, buf_ref, send_sems, recv_sems):
        _exchange_rows(x_ref, buf_ref, send_sems, recv_sems)
        o_ref[...] = _sum_slots(buf_ref)

    return pl.pallas_call(
        body, name="small_gather", out_shape=jax.ShapeDtypeStruct(x.shape, F32),
        in_specs=[pl.BlockSpec(memory_space=pltpu.VMEM)], out_specs=pl.BlockSpec(memory_space=pltpu.VMEM),
        scratch_shapes=[pltpu.VMEM((N_DEV,) + x.shape, F32), pltpu.SemaphoreType.DMA((7,)), pltpu.SemaphoreType.DMA((7,))],
    )(x)


def _small_reduce_adamw(g, w, m, v):
    def body(g_ref, w_ref, m_ref, v_ref, go_ref, d_ref, mo_ref, vo_ref, buf_ref, send_sems, recv_sems):
        _exchange_rows(g_ref, buf_ref, send_sems, recv_sems)
        gs = _sum_slots(buf_ref)
        delta, mn, vn = _adamw_math(w_ref[...], gs, m_ref[...], v_ref[...])
        go_ref[...] = gs
        d_ref[...] = delta
        mo_ref[...] = mn
        vo_ref[...] = vn

    vm = pl.BlockSpec(memory_space=pltpu.VMEM)
    shp = jax.ShapeDtypeStruct(g.shape, F32)
    return pl.pallas_call(
        body, name="small_reduce_adamw", out_shape=[shp, shp, shp, shp],
        in_specs=[vm, vm, vm, vm], out_specs=[vm, vm, vm, vm],
        scratch_shapes=[pltpu.VMEM((N_DEV,) + g.shape, F32), pltpu.SemaphoreType.DMA((7,)), pltpu.SemaphoreType.DMA((7,))],
    )(g, w, m, v)


def _pad_cols(a, width):
    return jnp.pad(a, ((0, 0), (0, width - a.shape[1])))


def _local_step(x, target, norm_mix_w, w_in_t, conv_w, conv_b, dt_bias, a_log, d_skip, ssd_norm_w, f_bias,
                late_shards, norm_mlp_w, norm_final_w):
    bsz, seq, _ = x.shape
    n = bsz * seq
    x2 = x.reshape(n, D_MODEL)
    t2 = target.reshape(n, D_MODEL)
    nfw = norm_final_w.reshape(1, D_MODEL)

    bias = _pad_cols(dt_bias, LANES)
    arow = _pad_cols(-jnp.exp(a_log), LANES)
    biast, acol = bias.reshape(LANES, 1), arow.reshape(LANES, 1)
    dfull = jnp.repeat(d_skip, HEAD_DIM, axis=1)
    fb = jnp.pad(f_bias, ((0, 0), (F_COL, LANES - 2 * F_COL)))

    wt_z, wt_xbc, wt_qkv = w_in_t[:ROW_XBC], w_in_t[ROW_XBC:ROW_DT], w_in_t[ROW_Q:ROW_F]
    wt_dtf = jnp.concatenate([w_in_t[ROW_DT:ROW_Q], w_in_t[ROW_F:], jnp.zeros((LANES - 2 * F_COL, D_MODEL), BF16)], axis=0)
    wt_q, wt_k, wt_v = wt_qkv[:D_MODEL], wt_qkv[D_MODEL:2 * D_MODEL], wt_qkv[2 * D_MODEL:]

    h1 = _rms_fwd(x2, norm_mix_w, "rms_fwd_mix")
    z = _mm([(h1, wt_z)], "inproj_z", F32, 1024, 1024, nt=True)
    xbc_raw = _mm([(h1, wt_xbc)], "inproj_xbc", F32, 512, 1536, nt=True)
    qkv = _mm([(h1, wt_qkv)], "inproj_qkv", BF16, 512, 3072, nt=True)
    dtf = _mm([(h1, wt_dtf)], "inproj_dtf", F32, 2048, LANES, nt=True)
    dtft = dtf.reshape(bsz, seq, LANES).transpose(0, 2, 1)

    xbc = _conv_fwd(xbc_raw, conv_w, conv_b, bsz, seq)
    y_pre, hall = _ssd_fwd(xbc, dtf, dtft, bias, biast, arow, acol, dfull, bsz, seq)

    ccol = _fox_prep(dtf, fb, bsz, seq)
    o_att, lse, w_out, w_up_t, w_down = _att_fwd(qkv, ccol, bsz, seq, late_shards)
    w_out, w_up_t, w_down = (w.reshape(-1, D_MODEL) for w in (w_out, w_up_t, w_down))

    ycat = _gnorm_fwd(y_pre, z, o_att, ssd_norm_w)
    h2, h2n = _mm([(ycat, w_out)], "outproj", F32, 512, 1024, res=x2, rms_fwd=norm_mlp_w)
    pre = _mm([(h2n, w_up_t)], "mlp_up", BF16, 512, 4096, nt=True)

    dh3, loss_row, g_nfw = _mlp_down_loss(pre, w_down, h2, t2, nfw)
    dpre, u = _mlp_down_bwd(dh3, w_down, pre)
    g_w_down = _mm_tn(u, dh3, "grad_w_down", 1024, 1024, 2048)
    g_w_up_t = _mm_tn(dpre, h2n, "grad_w_up", 1024, 1024, 2048)
    by_shard = lambda g: g.reshape(N_DEV, g.shape[0] // N_DEV, D_MODEL)
    dh2, g_nmlp = _mm([(dpre, w_up_t)], "mlp_up_bwd", F32, 512, 1024, res=dh3, rms_bwd=(h2, norm_mlp_w))

    g_w_out = _mm_tn(ycat, dh2, "grad_w_out", 1024, 1024, 2048)
    dycat = _mm([(dh2, w_out)], "outproj_bwd", F32, 512, 2048, nt=True)
    dy_pre, dz, g_ssdn = _gnorm_bwd(dycat, y_pre, z, ssd_norm_w)
    dq, dk, dv, dck, dcq, recv_down, recv_up_t, recv_out = _att_bwd(
        qkv, dycat, o_att, lse, ccol, bsz, seq, [by_shard(g_w_down), by_shard(g_w_up_t), by_shard(g_w_out)])

    dxbc, ddt, ssd_acc = _ssd_bwd(dy_pre, xbc, dtf, dtft, bias, biast, arow, acol, dfull, hall, bsz, seq)
    dxbc_raw, g_cw, g_cb = _conv_bwd(dxbc, xbc_raw, conv_w, conv_b, bsz, seq)

    def head_cols(rows):
        cols = rows[:, :, :2, :].reshape(bsz, SSD_HEADS, seq).transpose(0, 2, 1).reshape(n, SSD_HEADS)
        return jnp.pad(cols, ((0, 0), (F_COL, LANES - 2 * F_COL)))

    ddtf, g_fb = _fox_prep_bwd(head_cols(dck), head_cols(dcq), ddt, dtf, fb, bsz, seq)

    g_dtf = _mm_tn(ddtf, h1, "grad_w_in_dtf", LANES, 1024, 2048)
    g_w_in_t = jnp.concatenate([
        _mm_tn(dz, h1, "grad_w_in_z", 1024, 1024, 2048), _mm_tn(dxbc_raw, h1, "grad_w_in_xbc", 768, 1024, 2048),
        g_dtf[:F_COL], _mm_tn(dq, h1, "grad_w_in_q", 1024, 1024, 2048), _mm_tn(dk, h1, "grad_w_in_k", 1024, 1024, 2048),
        _mm_tn(dv, h1, "grad_w_in_v", 1024, 1024, 2048), g_dtf[F_COL:2 * F_COL]], axis=0)
    pieces = [(dz, wt_z), (dxbc_raw, wt_xbc), (dq, wt_q), (dk, wt_k), (dv, wt_v), (ddtf, wt_dtf)]
    dx, g_nmix, recv_in_t = _mm(pieces, "inproj_bwd", F32, 256, 1024, res=dh2, rms_bwd=(x2, norm_mix_w),
                                exchange=("scatter", [by_shard(g_w_in_t)]))

    small = dict(
        norm_mix_w=g_nmix, norm_mlp_w=g_nmlp, norm_final_w=g_nfw, ssd_norm_w=g_ssdn, conv_b=g_cb, conv_w=g_cw,
        dt_bias=ssd_acc[16:17, :SSD_HEADS], a_log=ssd_acc[0:1, :SSD_HEADS], d_skip=ssd_acc[8:9, :SSD_HEADS],
        f_bias=g_fb[0:1, F_COL:2 * F_COL])
    recv = dict(w_in_t=recv_in_t, w_out=recv_out, w_up_t=recv_up_t, w_down=recv_down)
    return loss_row[0, 0], dx.reshape(bsz, seq, D_MODEL), small, recv


SMALL_LAYOUT = (("norm_mix_w", 0, 1, 0, D_MODEL), ("norm_mlp_w", 1, 1, 0, D_MODEL), ("norm_final_w", 2, 1, 0, D_MODEL),
                ("ssd_norm_w", 3, 1, 0, D_MODEL), ("conv_b", 4, 1, 0, CONV_CH), ("conv_w", 5, CONV_K, 0, CONV_CH),
                ("dt_bias", 9, 1, 0, SSD_HEADS), ("a_log", 9, 1, LANES, SSD_HEADS), ("d_skip", 9, 1, 2 * LANES, SSD_HEADS),
                ("f_bias", 9, 1, 3 * LANES, SSD_HEADS))
SMALL_ROWS = 2 * SUBLANES


def _pack_small(vals):
    packed = jnp.zeros((SMALL_ROWS, SMALL_COLS), F32)
    for name, r0, nrows, c0, width in SMALL_LAYOUT:
        packed = packed.at[r0:r0 + nrows, c0:c0 + width].set(vals[name].reshape(nrows, width))
    return packed


def _unpack_small(packed, like):
    out = {}
    for name, r0, nrows, c0, width in SMALL_LAYOUT:
        out[name] = packed[r0:r0 + nrows, c0:c0 + width].reshape(like[name].shape)
    return out


def kernel(x, norm_mix_w, w_in, conv_w, conv_b, dt_bias, a_log, d_skip, ssd_norm_w, f_bias, w_out, norm_mlp_w, w_up, w_down, norm_final_w, loss_target, m_norm_mix_w, m_w_in, m_conv_w, m_conv_b, m_dt_bias, m_a_log, m_d_skip, m_ssd_norm_w, m_f_bias, m_w_out, m_norm_mlp_w, m_w_up, m_w_down, m_norm_final_w, v_norm_mix_w, v_w_in, v_conv_w, v_conv_b, v_dt_bias, v_a_log, v_d_skip, v_ssd_norm_w, v_f_bias, v_w_out, v_norm_mlp_w, v_w_up, v_w_down, v_norm_final_w):
    me = 4 * lax.axis_index("x") + 2 * lax.axis_index("y") + lax.axis_index("c")
    conv_shard_w = CONV_CH // N_DEV
    zero = jnp.zeros((), jnp.int32)

    def place_conv(shard):
        return lax.dynamic_update_slice(jnp.zeros((CONV_K, CONV_CH), F32), shard[0], (zero, me * conv_shard_w))

    (g_in_t,) = _all_gather_two_level([w_in[0].T.astype(BF16)])
    late_shards = [w_out[0].astype(BF16), w_up[0].T.astype(BF16), w_down[0].astype(BF16)]
    conv_full = _small_gather(jnp.pad(place_conv(conv_w), ((0, SUBLANES - CONV_K), (0, 0))))[:CONV_K]

    loss_local, grad_x, g_small, recv = _local_step(
        x, loss_target, norm_mix_w, g_in_t.reshape(IN_WIDTH, D_MODEL), conv_full, conv_b, dt_bias, a_log, d_skip,
        ssd_norm_w, f_bias, late_shards, norm_mlp_w, norm_final_w)
    loss = lax.psum(loss_local, MESH_AXES)

    grad_in = _sum_parts(recv["w_in_t"], "sum_w_in").T[None]
    grad_up = _sum_parts(recv["w_up_t"], "sum_w_up").T[None]
    big = {
        "w_in": _adamw_parts(grad_in, w_in[0], m_w_in[0], v_w_in[0], "adamw_w_in"),
        "w_out": _adamw_parts(recv["w_out"], w_out[0], m_w_out[0], v_w_out[0], "adamw_w_out"),
        "w_up": _adamw_parts(grad_up, w_up[0], m_w_up[0], v_w_up[0], "adamw_w_up"),
        "w_down": _adamw_parts(recv["w_down"], w_down[0], m_w_down[0], v_w_down[0], "adamw_w_down"),
    }

    like = dict(norm_mix_w=norm_mix_w, norm_mlp_w=norm_mlp_w, norm_final_w=norm_final_w, ssd_norm_w=ssd_norm_w,
                conv_b=conv_b, conv_w=jnp.zeros((1, CONV_K, CONV_CH), F32), dt_bias=dt_bias, a_log=a_log,
                d_skip=d_skip, f_bias=f_bias)
    w_small = dict(like, conv_w=place_conv(conv_w))
    m_small = dict(norm_mix_w=m_norm_mix_w, norm_mlp_w=m_norm_mlp_w, norm_final_w=m_norm_final_w,
                   ssd_norm_w=m_ssd_norm_w, conv_b=m_conv_b, conv_w=place_conv(m_conv_w), dt_bias=m_dt_bias,
                   a_log=m_a_log, d_skip=m_d_skip, f_bias=m_f_bias)
    v_small = dict(norm_mix_w=v_norm_mix_w, norm_mlp_w=v_norm_mlp_w, norm_final_w=v_norm_final_w,
                   ssd_norm_w=v_ssd_norm_w, conv_b=v_conv_b, conv_w=place_conv(v_conv_w), dt_bias=v_dt_bias,
                   a_log=v_a_log, d_skip=v_d_skip, f_bias=v_f_bias)
    small = _small_reduce_adamw(_pack_small(g_small), _pack_small(w_small), _pack_small(m_small), _pack_small(v_small))
    small = [_unpack_small(s, like) for s in small]
    for s in small:
        s["conv_w"] = lax.dynamic_slice(s["conv_w"], (zero, zero, me * conv_shard_w), (1, CONV_K, conv_shard_w))

    order = ["norm_mix_w", "w_in", "conv_w", "conv_b", "dt_bias", "a_log", "d_skip", "ssd_norm_w", "f_bias", "w_out",
             "norm_mlp_w", "w_up", "w_down", "norm_final_w"]
    outs = [loss, grad_x]
    for kind in range(4):
        for name in order:
            outs.append(big[name][kind][None] if name in big else small[kind][name])
    return tuple(outs)
```
